```python
import math
import jax, jax.numpy as jnp
from jax import lax
import numpy as np

D_MODEL = 2048
BATCH = 8
SEQ = 4096
DEPTH = 1

MIX_WIDTH = D_MODEL
HGRN_WIDTH = MIX_WIDTH // 2
HGRN_HEAD_DIM = 128
HGRN_HEADS = HGRN_WIDTH // HGRN_HEAD_DIM
MLSTM_WIDTH = MIX_WIDTH - HGRN_WIDTH
MLSTM_HEADS = 4
MLSTM_HEAD_DIM = MLSTM_WIDTH // MLSTM_HEADS
CHUNK = 64
CONV_WIDTH = 5
D_FF = 256 * ((8 * D_MODEL // 3 + 255) // 256)
DN_ALPHA = (2.0 * DEPTH) ** 0.25
DN_BETA = (8.0 * DEPTH) ** -0.25
LN_EPS = 1e-5
NORM_EPS = 1e-6
M_INIT = -1e30
IN_SPLITS = (HGRN_WIDTH, HGRN_WIDTH, HGRN_WIDTH, HGRN_WIDTH, HGRN_WIDTH,
             MLSTM_WIDTH, MLSTM_WIDTH, MLSTM_WIDTH, MLSTM_WIDTH,
             MLSTM_HEADS, MLSTM_HEADS, MLSTM_HEADS, MLSTM_HEADS)
IN_COLS = 5 * HGRN_WIDTH + 4 * MLSTM_WIDTH + 4 * MLSTM_HEADS

kernel_name = 'bidir_hgrn2_mlstm_macaron_deepnorm'


def _layer_norm(x, g, b):
    xf = x.astype(jnp.float32)
    mu = jnp.mean(xf, axis=-1, keepdims=True)
    var = jnp.mean(jnp.square(xf - mu), axis=-1, keepdims=True)
    y = (xf - mu) * lax.rsqrt(var + LN_EPS)
    return (y * g.astype(jnp.float32) + b.astype(jnp.float32)).astype(x.dtype)


def _swiglu(x, w1, w3, w2):
    return (jax.nn.silu(x @ w1) * (x @ w3)) @ w2


def _to_heads(t, n):
    b_, t_, w = t.shape
    return t.reshape(b_, t_, n, w // n).transpose(0, 2, 1, 3)


def _merge_heads(t):
    b_, h_, t_, d = t.shape
    return t.transpose(0, 2, 1, 3).reshape(b_, t_, h_ * d)


def _chunk(t):
    b_, h_, t_ = t.shape[:3]
    return jnp.moveaxis(t.reshape(b_, h_, t_ // CHUNK, CHUNK, *t.shape[3:]), 2, 0)


def _unchunk(t):
    t = jnp.moveaxis(t, 0, 2)
    return t.reshape(t.shape[0], t.shape[1], t.shape[2] * t.shape[3], *t.shape[4:])


def _flip(t):
    return jnp.flip(t, axis=2)


def _hgrn2_scan(q, k, v, logf):
    b_, h_, _, dk = q.shape
    dv = v.shape[-1]
    mask = jnp.tril(jnp.ones((CHUNK, CHUNK), dtype=bool))

    def step(state, inp):
        qc, kc, vc, gc = inp
        bcum = jnp.cumsum(gc, axis=-2)
        o_inter = jnp.einsum('bhtd,bhde->bhte', qc * jnp.exp(bcum), state)
        rel = bcum[..., :, None, :] - bcum[..., None, :, :]
        decay = jnp.exp(jnp.where(mask[:, :, None], rel, -jnp.inf))
        attn = jnp.einsum('bhtd,bhsd,bhtsd->bhts', qc, kc, decay)
        o = o_inter + jnp.einsum('bhts,bhse->bhte', attn, vc)
        b_last = bcum[..., -1:, :]
        state = (jnp.exp(b_last[..., 0, :])[..., None] * state
                 + jnp.einsum('bhsd,bhse->bhde', kc * jnp.exp(b_last - bcum), vc))
        return state, o

    s0 = jnp.zeros((b_, h_, dk, dv), jnp.float32)
    _, o = lax.scan(step, s0, (_chunk(q), _chunk(k), _chunk(v), _chunk(logf)))
    return _unchunk(o)


def _mlstm_scan(q, k, v, ig, lf):
    b_, h_, _, dk = q.shape
    dv = v.shape[-1]
    mask = jnp.tril(jnp.ones((CHUNK, CHUNK), dtype=bool))

    def step(carry, inp):
        c_st, n_st, m_st = carry
        qc, kc, vc, igc, lfc = inp
        bcum = jnp.cumsum(lfc, axis=-1)
        dmat = jnp.where(mask, bcum[..., :, None] - bcum[..., None, :] + igc[..., None, :], -jnp.inf)
        m_inter = bcum + m_st[..., None]
        m_t = jnp.maximum(m_inter, jnp.max(dmat, axis=-1))
        inter_scale = jnp.exp(m_inter - m_t)
        sc = jnp.einsum('bhtd,bhsd->bhts', qc, kc) * jnp.exp(dmat - m_t[..., None])
        num = jnp.einsum('bhts,bhse->bhte', sc, vc) + inter_scale[..., None] * jnp.einsum('bhtd,bhde->bhte', qc, c_st)
        den = jnp.sum(sc, axis=-1) + inter_scale * jnp.einsum('bhtd,bhd->bht', qc, n_st)
        h = num / jnp.maximum(jnp.abs(den), jnp.exp(-m_t))[..., None]
        b_last = bcum[..., -1]
        w = b_last[..., None] - bcum + igc
        m_new = jnp.maximum(b_last + m_st, jnp.max(w, axis=-1))
        carry_scale = jnp.exp(b_last + m_st - m_new)
        kw = kc * jnp.exp(w - m_new[..., None])[..., None]
        c_new = carry_scale[..., None, None] * c_st + jnp.einsum('bhsd,bhse->bhde', kw, vc)
        n_new = carry_scale[..., None] * n_st + jnp.sum(kw, axis=-2)
        return (c_new, n_new, m_new), h

    carry0 = (jnp.zeros((b_, h_, dk, dv), jnp.float32),
              jnp.zeros((b_, h_, dk), jnp.float32),
              jnp.full((b_, h_), M_INIT, jnp.float32))
    _, h = lax.scan(step, carry0, (_chunk(q), _chunk(k), _chunk(v), _chunk(ig), _chunk(lf)))
    return _unchunk(h)


def _centred_dwconv(t, w, b):
    y = lax.conv_general_dilated(t, w[:, None, :], window_strides=(1,),
                                 padding=[(CONV_WIDTH // 2, CONV_WIDTH // 2)],
                                 dimension_numbers=('NWC', 'WIO', 'NWC'),
                                 feature_group_count=t.shape[-1])
    return y + b


def _mixer(x, layer, w_in, hgrn_lb, hgrn_norm_g, conv_w, conv_b, ig_b, fg_b, mlstm_norm_g, w_out):
    f32 = jnp.float32
    z = x @ w_in
    offsets = [int(o) for o in np.cumsum(IN_SPLITS)[:-1]]
    (hq, hi, hg, hf_fw, hf_bw, mq, mk, mv, mo, mi_fw, mi_bw, mf_fw, mf_bw) = jnp.split(z, offsets, axis=-1)

    lb = jnp.cumsum(jax.nn.softmax(hgrn_lb.astype(f32), axis=1), axis=1)[:, layer]
    q_h = _to_heads(jax.nn.silu(hq.astype(f32)) * (HGRN_HEAD_DIM ** -0.5), HGRN_HEADS)
    v_h = _to_heads(hi.astype(f32), HGRN_HEADS)

    def _forget(zf, lbd):
        f = lbd + (1.0 - lbd) * jax.nn.sigmoid(zf.astype(f32))
        return _to_heads(jnp.log(f), HGRN_HEADS), _to_heads(1.0 - f, HGRN_HEADS)

    logf_fw, k_fw = _forget(hf_fw, lb[0])
    logf_bw, k_bw = _forget(hf_bw, lb[1])
    o_h = (_hgrn2_scan(q_h, k_fw, v_h, logf_fw)
           + _flip(_hgrn2_scan(_flip(q_h), _flip(k_bw), _flip(v_h), _flip(logf_bw))))
    o_h = o_h * lax.rsqrt(jnp.mean(jnp.square(o_h), axis=-1, keepdims=True) + NORM_EPS)
    y_h = _merge_heads(o_h) * hgrn_norm_g.astype(f32) * jax.nn.silu(hg.astype(f32))

    qk = jax.nn.silu(_centred_dwconv(jnp.concatenate([mq, mk], axis=-1), conv_w, conv_b))
    mq_c, mk_c = jnp.split(qk, 2, axis=-1)
    q_m = _to_heads(mq_c.astype(f32), MLSTM_HEADS) * (MLSTM_HEAD_DIM ** -0.5)
    k_m = _to_heads(mk_c.astype(f32), MLSTM_HEADS)
    v_m = _to_heads(mv.astype(f32), MLSTM_HEADS)
    ig_fw = (mi_fw.astype(f32) + ig_b[0].astype(f32)).transpose(0, 2, 1)
    ig_bw = (mi_bw.astype(f32) + ig_b[1].astype(f32)).transpose(0, 2, 1)
    lf_fw = jax.nn.log_sigmoid(mf_fw.astype(f32) + fg_b[0].astype(f32)).transpose(0, 2, 1)
    lf_bw = jax.nn.log_sigmoid(mf_bw.astype(f32) + fg_b[1].astype(f32)).transpose(0, 2, 1)
    h_m = (_mlstm_scan(q_m, k_m, v_m, ig_fw, lf_fw)
           + _flip(_mlstm_scan(_flip(q_m), _flip(k_m), _flip(v_m), _flip(ig_bw), _flip(lf_bw))))
    mu = jnp.mean(h_m, axis=-1, keepdims=True)
    var = jnp.mean(jnp.square(h_m - mu), axis=-1, keepdims=True)
    h_m = (h_m - mu) * lax.rsqrt(var + NORM_EPS)
    y_m = _merge_heads(h_m) * mlstm_norm_g.astype(f32) * jax.nn.sigmoid(mo.astype(f32))

    y = jnp.concatenate([y_h, y_m], axis=-1).astype(x.dtype)
    return y @ w_out


def _fwd_setup_inputs(seed: int = 0) -> dict:
    key = jax.random.key(seed)
    ks = jax.random.split(key, 24)
    f32 = jnp.float32
    d_sc = D_MODEL ** -0.5
    ff_sc = D_FF ** -0.5

    def nrm(k, shape, scale):
        return jax.random.normal(k, shape, f32) * scale

    col_scale = jnp.concatenate([
        jnp.ones((HGRN_WIDTH,), f32),
        jnp.full((HGRN_WIDTH,), DN_BETA, f32),
        jnp.ones((3 * HGRN_WIDTH,), f32),
        jnp.ones((2 * MLSTM_WIDTH,), f32),
        jnp.full((MLSTM_WIDTH,), DN_BETA, f32),
        jnp.ones((MLSTM_WIDTH,), f32),
        jnp.full((4 * MLSTM_HEADS,), 0.1, f32),
    ])
    fg_bias = jnp.broadcast_to(jnp.linspace(3.0, 6.0, MLSTM_HEADS, dtype=f32), (DEPTH, 2, MLSTM_HEADS))
    return {
        'x': jax.random.normal(ks[0], (BATCH, SEQ, D_MODEL), f32),
        'ffn1_w1': nrm(ks[1], (DEPTH, D_MODEL, D_FF), d_sc),
        'ffn1_w3': nrm(ks[2], (DEPTH, D_MODEL, D_FF), d_sc),
        'ffn1_w2': nrm(ks[3], (DEPTH, D_FF, D_MODEL), ff_sc * DN_BETA),
        'ln1_g': 1.0 + nrm(ks[4], (DEPTH, D_MODEL), 0.02),
        'ln1_b': nrm(ks[5], (DEPTH, D_MODEL), 0.02),
        'w_in': nrm(ks[6], (DEPTH, D_MODEL, IN_COLS), d_sc) * col_scale,
        'hgrn_lb': nrm(ks[7], (2, DEPTH + 1, HGRN_WIDTH), 0.1),
        'hgrn_norm_g': 1.0 + nrm(ks[8], (DEPTH, HGRN_WIDTH), 0.02),
        'mlstm_conv_w': nrm(ks[9], (DEPTH, CONV_WIDTH, 2 * MLSTM_WIDTH), CONV_WIDTH ** -0.5),
        'mlstm_conv_b': nrm(ks[10], (DEPTH, 2 * MLSTM_WIDTH), 0.02),
        'mlstm_ig_b': nrm(ks[11], (DEPTH, 2, MLSTM_HEADS), 0.1),
        'mlstm_fg_b': fg_bias + nrm(ks[12], (DEPTH, 2, MLSTM_HEADS), 0.1),
        'mlstm_norm_g': 1.0 + nrm(ks[13], (DEPTH, MLSTM_WIDTH), 0.02),
        'w_out': nrm(ks[14], (DEPTH, MIX_WIDTH, D_MODEL), (MIX_WIDTH ** -0.5) * DN_BETA),
        'ln2_g': 1.0 + nrm(ks[15], (DEPTH, D_MODEL), 0.02),
        'ln2_b': nrm(ks[16], (DEPTH, D_MODEL), 0.02),
        'ffn2_w1': nrm(ks[17], (DEPTH, D_MODEL, D_FF), d_sc),
        'ffn2_w3': nrm(ks[18], (DEPTH, D_MODEL, D_FF), d_sc),
        'ffn2_w2': nrm(ks[19], (DEPTH, D_FF, D_MODEL), ff_sc * DN_BETA),
        'ln3_g': 1.0 + nrm(ks[20], (DEPTH, D_MODEL), 0.02),
        'ln3_b': nrm(ks[21], (DEPTH, D_MODEL), 0.02),
    }


def _fwd_reference(x, ffn1_w1, ffn1_w3, ffn1_w2, ln1_g, ln1_b, w_in, hgrn_lb, hgrn_norm_g,
              mlstm_conv_w, mlstm_conv_b, mlstm_ig_b, mlstm_fg_b, mlstm_norm_g, w_out,
              ln2_g, ln2_b, ffn2_w1, ffn2_w3, ffn2_w2, ln3_g, ln3_b):
    for l in range(DEPTH):
        x = _layer_norm(x * DN_ALPHA + 0.5 * _swiglu(x, ffn1_w1[l], ffn1_w3[l], ffn1_w2[l]), ln1_g[l], ln1_b[l])
        y = _mixer(x, l, w_in[l], hgrn_lb, hgrn_norm_g[l], mlstm_conv_w[l], mlstm_conv_b[l],
                   mlstm_ig_b[l], mlstm_fg_b[l], mlstm_norm_g[l], w_out[l])
        x = _layer_norm(x * DN_ALPHA + y, ln2_g[l], ln2_b[l])
        x = _layer_norm(x * DN_ALPHA + 0.5 * _swiglu(x, ffn2_w1[l], ffn2_w3[l], ffn2_w2[l]), ln3_g[l], ln3_b[l])
    return x


import jax as _jax
import jax.numpy as _jnp

TWIN_FORMAT = 'train_step'
FWD_PARAMS = ['x', 'ffn1_w1', 'ffn1_w3', 'ffn1_w2', 'ln1_g', 'ln1_b', 'w_in', 'hgrn_lb', 'hgrn_norm_g', 'mlstm_conv_w', 'mlstm_conv_b', 'mlstm_ig_b', 'mlstm_fg_b', 'mlstm_norm_g', 'w_out', 'ln2_g', 'ln2_b', 'ffn2_w1', 'ffn2_w3', 'ffn2_w2', 'ln3_g', 'ln3_b']
TWIN_WEIGHTS = ['ffn1_w1', 'ffn1_w3', 'ffn1_w2', 'ln1_g', 'ln1_b', 'w_in', 'hgrn_lb', 'hgrn_norm_g', 'mlstm_conv_w', 'mlstm_conv_b', 'mlstm_ig_b', 'mlstm_fg_b', 'mlstm_norm_g', 'w_out', 'ln2_g', 'ln2_b', 'ffn2_w1', 'ffn2_w3', 'ffn2_w2', 'ln3_g', 'ln3_b']
TWIN_DIFF_INPUT = 'x'
TWIN_INPUTS = ['x', 'ffn1_w1', 'ffn1_w3', 'ffn1_w2', 'ln1_g', 'ln1_b', 'w_in', 'hgrn_lb', 'hgrn_norm_g', 'mlstm_conv_w', 'mlstm_conv_b', 'mlstm_ig_b', 'mlstm_fg_b', 'mlstm_norm_g', 'w_out', 'ln2_g', 'ln2_b', 'ffn2_w1', 'ffn2_w3', 'ffn2_w2', 'ln3_g', 'ln3_b', 'loss_target', 'm_ffn1_w1', 'm_ffn1_w3', 'm_ffn1_w2', 'm_ln1_g', 'm_ln1_b', 'm_w_in', 'm_hgrn_lb', 'm_hgrn_norm_g', 'm_mlstm_conv_w', 'm_mlstm_conv_b', 'm_mlstm_ig_b', 'm_mlstm_fg_b', 'm_mlstm_norm_g', 'm_w_out', 'm_ln2_g', 'm_ln2_b', 'm_ffn2_w1', 'm_ffn2_w3', 'm_ffn2_w2', 'm_ln3_g', 'm_ln3_b', 'v_ffn1_w1', 'v_ffn1_w3', 'v_ffn1_w2', 'v_ln1_g', 'v_ln1_b', 'v_w_in', 'v_hgrn_lb', 'v_hgrn_norm_g', 'v_mlstm_conv_w', 'v_mlstm_conv_b', 'v_mlstm_ig_b', 'v_mlstm_fg_b', 'v_mlstm_norm_g', 'v_w_out', 'v_ln2_g', 'v_ln2_b', 'v_ffn2_w1', 'v_ffn2_w3', 'v_ffn2_w2', 'v_ln3_g', 'v_ln3_b']
TWIN_OUTPUTS = ['loss', 'grad_x', 'grad_ffn1_w1', 'grad_ffn1_w3', 'grad_ffn1_w2', 'grad_ln1_g', 'grad_ln1_b', 'grad_w_in', 'grad_hgrn_lb', 'grad_hgrn_norm_g', 'grad_mlstm_conv_w', 'grad_mlstm_conv_b', 'grad_mlstm_ig_b', 'grad_mlstm_fg_b', 'grad_mlstm_norm_g', 'grad_w_out', 'grad_ln2_g', 'grad_ln2_b', 'grad_ffn2_w1', 'grad_ffn2_w3', 'grad_ffn2_w2', 'grad_ln3_g', 'grad_ln3_b', 'delta_ffn1_w1', 'delta_ffn1_w3', 'delta_ffn1_w2', 'delta_ln1_g', 'delta_ln1_b', 'delta_w_in', 'delta_hgrn_lb', 'delta_hgrn_norm_g', 'delta_mlstm_conv_w', 'delta_mlstm_conv_b', 'delta_mlstm_ig_b', 'delta_mlstm_fg_b', 'delta_mlstm_norm_g', 'delta_w_out', 'delta_ln2_g', 'delta_ln2_b', 'delta_ffn2_w1', 'delta_ffn2_w3', 'delta_ffn2_w2', 'delta_ln3_g', 'delta_ln3_b', 'new_m_ffn1_w1', 'new_m_ffn1_w3', 'new_m_ffn1_w2', 'new_m_ln1_g', 'new_m_ln1_b', 'new_m_w_in', 'new_m_hgrn_lb', 'new_m_hgrn_norm_g', 'new_m_mlstm_conv_w', 'new_m_mlstm_conv_b', 'new_m_mlstm_ig_b', 'new_m_mlstm_fg_b', 'new_m_mlstm_norm_g', 'new_m_w_out', 'new_m_ln2_g', 'new_m_ln2_b', 'new_m_ffn2_w1', 'new_m_ffn2_w3', 'new_m_ffn2_w2', 'new_m_ln3_g', 'new_m_ln3_b', 'new_v_ffn1_w1', 'new_v_ffn1_w3', 'new_v_ffn1_w2', 'new_v_ln1_g', 'new_v_ln1_b', 'new_v_w_in', 'new_v_hgrn_lb', 'new_v_hgrn_norm_g', 'new_v_mlstm_conv_w', 'new_v_mlstm_conv_b', 'new_v_mlstm_ig_b', 'new_v_mlstm_fg_b', 'new_v_mlstm_norm_g', 'new_v_w_out', 'new_v_ln2_g', 'new_v_ln2_b', 'new_v_ffn2_w1', 'new_v_ffn2_w3', 'new_v_ffn2_w2', 'new_v_ln3_g', 'new_v_ln3_b']
TWIN_LEAF_KINDS = {'loss': 'loss', 'grad_x': 'grad_x', 'grad_ffn1_w1': 'grad_w', 'grad_ffn1_w3': 'grad_w', 'grad_ffn1_w2': 'grad_w', 'grad_ln1_g': 'grad_w', 'grad_ln1_b': 'grad_w', 'grad_w_in': 'grad_w', 'grad_hgrn_lb': 'grad_w', 'grad_hgrn_norm_g': 'grad_w', 'grad_mlstm_conv_w': 'grad_w', 'grad_mlstm_conv_b': 'grad_w', 'grad_mlstm_ig_b': 'grad_w', 'grad_mlstm_fg_b': 'grad_w', 'grad_mlstm_norm_g': 'grad_w', 'grad_w_out': 'grad_w', 'grad_ln2_g': 'grad_w', 'grad_ln2_b': 'grad_w', 'grad_ffn2_w1': 'grad_w', 'grad_ffn2_w3': 'grad_w', 'grad_ffn2_w2': 'grad_w', 'grad_ln3_g': 'grad_w', 'grad_ln3_b': 'grad_w', 'delta_ffn1_w1': 'delta_w', 'delta_ffn1_w3': 'delta_w', 'delta_ffn1_w2': 'delta_w', 'delta_ln1_g': 'delta_w', 'delta_ln1_b': 'delta_w', 'delta_w_in': 'delta_w', 'delta_hgrn_lb': 'delta_w', 'delta_hgrn_norm_g': 'delta_w', 'delta_mlstm_conv_w': 'delta_w', 'delta_mlstm_conv_b': 'delta_w', 'delta_mlstm_ig_b': 'delta_w', 'delta_mlstm_fg_b': 'delta_w', 'delta_mlstm_norm_g': 'delta_w', 'delta_w_out': 'delta_w', 'delta_ln2_g': 'delta_w', 'delta_ln2_b': 'delta_w', 'delta_ffn2_w1': 'delta_w', 'delta_ffn2_w3': 'delta_w', 'delta_ffn2_w2': 'delta_w', 'delta_ln3_g': 'delta_w', 'delta_ln3_b': 'delta_w', 'new_m_ffn1_w1': 'new_m', 'new_m_ffn1_w3': 'new_m', 'new_m_ffn1_w2': 'new_m', 'new_m_ln1_g': 'new_m', 'new_m_ln1_b': 'new_m', 'new_m_w_in': 'new_m', 'new_m_hgrn_lb': 'new_m', 'new_m_hgrn_norm_g': 'new_m', 'new_m_mlstm_conv_w': 'new_m', 'new_m_mlstm_conv_b': 'new_m', 'new_m_mlstm_ig_b': 'new_m', 'new_m_mlstm_fg_b': 'new_m', 'new_m_mlstm_norm_g': 'new_m', 'new_m_w_out': 'new_m', 'new_m_ln2_g': 'new_m', 'new_m_ln2_b': 'new_m', 'new_m_ffn2_w1': 'new_m', 'new_m_ffn2_w3': 'new_m', 'new_m_ffn2_w2': 'new_m', 'new_m_ln3_g': 'new_m', 'new_m_ln3_b': 'new_m', 'new_v_ffn1_w1': 'new_v', 'new_v_ffn1_w3': 'new_v', 'new_v_ffn1_w2': 'new_v', 'new_v_ln1_g': 'new_v', 'new_v_ln1_b': 'new_v', 'new_v_w_in': 'new_v', 'new_v_hgrn_lb': 'new_v', 'new_v_hgrn_norm_g': 'new_v', 'new_v_mlstm_conv_w': 'new_v', 'new_v_mlstm_conv_b': 'new_v', 'new_v_mlstm_ig_b': 'new_v', 'new_v_mlstm_fg_b': 'new_v', 'new_v_mlstm_norm_g': 'new_v', 'new_v_w_out': 'new_v', 'new_v_ln2_g': 'new_v', 'new_v_ln2_b': 'new_v', 'new_v_ffn2_w1': 'new_v', 'new_v_ffn2_w3': 'new_v', 'new_v_ffn2_w2': 'new_v', 'new_v_ln3_g': 'new_v', 'new_v_ln3_b': 'new_v'}


def _forward(args):
    return _fwd_reference(*[args[k] for k in FWD_PARAMS])


def _output_shape():
    def fwd():
        inp = _fwd_setup_inputs(0)
        return _fwd_reference(*[inp[k] for k in FWD_PARAMS])
    out = _jax.eval_shape(fwd)
    return out.shape, out.dtype

N_MICROBATCH = 1
ADAM_LR = 0.001
ADAM_B1 = 0.9
ADAM_B2 = 0.999
ADAM_EPS = 1e-08
ADAM_WD = 0.01
ADAM_STEP = 10
PER_EXAMPLE_BATCH_AXIS = {'x': 0, 'loss_target': 0}
SHARED_INPUTS = []
_WEIGHT_DTYPES = {'ffn1_w1': _jnp.float32, 'ffn1_w3': _jnp.float32, 'ffn1_w2': _jnp.float32, 'ln1_g': _jnp.float32, 'ln1_b': _jnp.float32, 'w_in': _jnp.float32, 'hgrn_lb': _jnp.float32, 'hgrn_norm_g': _jnp.float32, 'mlstm_conv_w': _jnp.float32, 'mlstm_conv_b': _jnp.float32, 'mlstm_ig_b': _jnp.float32, 'mlstm_fg_b': _jnp.float32, 'mlstm_norm_g': _jnp.float32, 'w_out': _jnp.float32, 'ln2_g': _jnp.float32, 'ln2_b': _jnp.float32, 'ffn2_w1': _jnp.float32, 'ffn2_w3': _jnp.float32, 'ffn2_w2': _jnp.float32, 'ln3_g': _jnp.float32, 'ln3_b': _jnp.float32}
MOMENT_SCALE = {'ffn1_w1': 8.555237e-03, 'ffn1_w3': 8.287899e-03, 'ffn1_w2': 2.312979e-02, 'ln1_g': 5.299450e-01, 'ln1_b': 9.588858e-01, 'w_in': 2.616120e-02, 'hgrn_lb': 1.342610e-03, 'hgrn_norm_g': 2.633109e-02, 'mlstm_conv_w': 1.196241e-02, 'mlstm_conv_b': 1.051826e-02, 'mlstm_ig_b': 7.726549e-03, 'mlstm_fg_b': 1.258339e-01, 'mlstm_norm_g': 3.051151e-02, 'w_out': 4.651184e-02, 'ln2_g': 5.794210e-01, 'ln2_b': 2.806415e-01, 'ffn2_w1': 8.161094e-03, 'ffn2_w3': 7.911631e-03, 'ffn2_w2': 2.206162e-02, 'ln3_g': 1.603329e+01, 'ln3_b': 1.088905e+00}


def _to_microbatches(a, axis):
    t = _jnp.moveaxis(a, axis, 0)
    t = t.reshape((N_MICROBATCH, t.shape[0] // N_MICROBATCH) + t.shape[1:])
    return _jnp.moveaxis(t, 1, axis + 1)


def setup_inputs(seed: int = 0) -> dict:
    inp = _fwd_setup_inputs(seed)
    key = _jax.random.fold_in(_jax.random.key(seed), 7919)
    shape, _ = _output_shape()
    out = dict(inp)
    out["loss_target"] = _jax.random.normal(_jax.random.fold_in(key, 0), shape, _jnp.float32)
    for i, name in enumerate(TWIN_WEIGHTS):
        w = inp[name].astype(_jnp.float32)
        if MOMENT_SCALE is None:
            s = _jnp.sqrt(_jnp.mean(_jnp.square(w)) + 1e-30)
        else:
            s = MOMENT_SCALE[name]
        km, kv = _jax.random.split(_jax.random.fold_in(key, i + 1))
        out[name] = w
        out["m_" + name] = s * _jax.random.normal(km, w.shape, _jnp.float32)
        out["v_" + name] = (s * s) * _jax.random.uniform(kv, w.shape, _jnp.float32, 0.5, 1.5)
    if N_MICROBATCH > 1:
        for name, axis in PER_EXAMPLE_BATCH_AXIS.items():
            out[name] = _to_microbatches(out[name], axis)
    return {'x': out['x'], 'ffn1_w1': out['ffn1_w1'], 'ffn1_w3': out['ffn1_w3'], 'ffn1_w2': out['ffn1_w2'], 'ln1_g': out['ln1_g'], 'ln1_b': out['ln1_b'], 'w_in': out['w_in'], 'hgrn_lb': out['hgrn_lb'], 'hgrn_norm_g': out['hgrn_norm_g'], 'mlstm_conv_w': out['mlstm_conv_w'], 'mlstm_conv_b': out['mlstm_conv_b'], 'mlstm_ig_b': out['mlstm_ig_b'], 'mlstm_fg_b': out['mlstm_fg_b'], 'mlstm_norm_g': out['mlstm_norm_g'], 'w_out': out['w_out'], 'ln2_g': out['ln2_g'], 'ln2_b': out['ln2_b'], 'ffn2_w1': out['ffn2_w1'], 'ffn2_w3': out['ffn2_w3'], 'ffn2_w2': out['ffn2_w2'], 'ln3_g': out['ln3_g'], 'ln3_b': out['ln3_b'], 'loss_target': out['loss_target'], 'm_ffn1_w1': out['m_ffn1_w1'], 'm_ffn1_w3': out['m_ffn1_w3'], 'm_ffn1_w2': out['m_ffn1_w2'], 'm_ln1_g': out['m_ln1_g'], 'm_ln1_b': out['m_ln1_b'], 'm_w_in': out['m_w_in'], 'm_hgrn_lb': out['m_hgrn_lb'], 'm_hgrn_norm_g': out['m_hgrn_norm_g'], 'm_mlstm_conv_w': out['m_mlstm_conv_w'], 'm_mlstm_conv_b': out['m_mlstm_conv_b'], 'm_mlstm_ig_b': out['m_mlstm_ig_b'], 'm_mlstm_fg_b': out['m_mlstm_fg_b'], 'm_mlstm_norm_g': out['m_mlstm_norm_g'], 'm_w_out': out['m_w_out'], 'm_ln2_g': out['m_ln2_g'], 'm_ln2_b': out['m_ln2_b'], 'm_ffn2_w1': out['m_ffn2_w1'], 'm_ffn2_w3': out['m_ffn2_w3'], 'm_ffn2_w2': out['m_ffn2_w2'], 'm_ln3_g': out['m_ln3_g'], 'm_ln3_b': out['m_ln3_b'], 'v_ffn1_w1': out['v_ffn1_w1'], 'v_ffn1_w3': out['v_ffn1_w3'], 'v_ffn1_w2': out['v_ffn1_w2'], 'v_ln1_g': out['v_ln1_g'], 'v_ln1_b': out['v_ln1_b'], 'v_w_in': out['v_w_in'], 'v_hgrn_lb': out['v_hgrn_lb'], 'v_hgrn_norm_g': out['v_hgrn_norm_g'], 'v_mlstm_conv_w': out['v_mlstm_conv_w'], 'v_mlstm_conv_b': out['v_mlstm_conv_b'], 'v_mlstm_ig_b': out['v_mlstm_ig_b'], 'v_mlstm_fg_b': out['v_mlstm_fg_b'], 'v_mlstm_norm_g': out['v_mlstm_norm_g'], 'v_w_out': out['v_w_out'], 'v_ln2_g': out['v_ln2_g'], 'v_ln2_b': out['v_ln2_b'], 'v_ffn2_w1': out['v_ffn2_w1'], 'v_ffn2_w3': out['v_ffn2_w3'], 'v_ffn2_w2': out['v_ffn2_w2'], 'v_ln3_g': out['v_ln3_g'], 'v_ln3_b': out['v_ln3_b']}


def _loss(weights, diff, rest, loss_target):
    with _jax.named_scope("forward"):
        args = {**rest, TWIN_DIFF_INPUT: diff, **{k: w.astype(_WEIGHT_DTYPES[k]) for k, w in weights.items()}}
        y = _forward(args)
    with _jax.named_scope("loss_head"):
        err = _jnp.square(y.astype(_jnp.float32) - loss_target)
        return 0.5 * _jnp.sum(_jnp.mean(err, axis=-1)) if err.ndim else 0.5 * err


def _adamw(w, g, m, v):
    m = ADAM_B1 * m + (1.0 - ADAM_B1) * g
    v = ADAM_B2 * v + (1.0 - ADAM_B2) * _jnp.square(g)
    m_hat = m / (1.0 - ADAM_B1 ** ADAM_STEP)
    v_hat = v / (1.0 - ADAM_B2 ** ADAM_STEP)
    delta = -ADAM_LR * (m_hat / (_jnp.sqrt(v_hat) + ADAM_EPS) + ADAM_WD * w)
    return delta, m, v


def reference(x, ffn1_w1, ffn1_w3, ffn1_w2, ln1_g, ln1_b, w_in, hgrn_lb, hgrn_norm_g, mlstm_conv_w, mlstm_conv_b, mlstm_ig_b, mlstm_fg_b, mlstm_norm_g, w_out, ln2_g, ln2_b, ffn2_w1, ffn2_w3, ffn2_w2, ln3_g, ln3_b, loss_target, m_ffn1_w1, m_ffn1_w3, m_ffn1_w2, m_ln1_g, m_ln1_b, m_w_in, m_hgrn_lb, m_hgrn_norm_g, m_mlstm_conv_w, m_mlstm_conv_b, m_mlstm_ig_b, m_mlstm_fg_b, m_mlstm_norm_g, m_w_out, m_ln2_g, m_ln2_b, m_ffn2_w1, m_ffn2_w3, m_ffn2_w2, m_ln3_g, m_ln3_b, v_ffn1_w1, v_ffn1_w3, v_ffn1_w2, v_ln1_g, v_ln1_b, v_w_in, v_hgrn_lb, v_hgrn_norm_g, v_mlstm_conv_w, v_mlstm_conv_b, v_mlstm_ig_b, v_mlstm_fg_b, v_mlstm_norm_g, v_w_out, v_ln2_g, v_ln2_b, v_ffn2_w1, v_ffn2_w3, v_ffn2_w2, v_ln3_g, v_ln3_b):
    given = dict(x=x, ffn1_w1=ffn1_w1, ffn1_w3=ffn1_w3, ffn1_w2=ffn1_w2, ln1_g=ln1_g, ln1_b=ln1_b, w_in=w_in, hgrn_lb=hgrn_lb, hgrn_norm_g=hgrn_norm_g, mlstm_conv_w=mlstm_conv_w, mlstm_conv_b=mlstm_conv_b, mlstm_ig_b=mlstm_ig_b, mlstm_fg_b=mlstm_fg_b, mlstm_norm_g=mlstm_norm_g, w_out=w_out, ln2_g=ln2_g, ln2_b=ln2_b, ffn2_w1=ffn2_w1, ffn2_w3=ffn2_w3, ffn2_w2=ffn2_w2, ln3_g=ln3_g, ln3_b=ln3_b, loss_target=loss_target, m_ffn1_w1=m_ffn1_w1, m_ffn1_w3=m_ffn1_w3, m_ffn1_w2=m_ffn1_w2, m_ln1_g=m_ln1_g, m_ln1_b=m_ln1_b, m_w_in=m_w_in, m_hgrn_lb=m_hgrn_lb, m_hgrn_norm_g=m_hgrn_norm_g, m_mlstm_conv_w=m_mlstm_conv_w, m_mlstm_conv_b=m_mlstm_conv_b, m_mlstm_ig_b=m_mlstm_ig_b, m_mlstm_fg_b=m_mlstm_fg_b, m_mlstm_norm_g=m_mlstm_norm_g, m_w_out=m_w_out, m_ln2_g=m_ln2_g, m_ln2_b=m_ln2_b, m_ffn2_w1=m_ffn2_w1, m_ffn2_w3=m_ffn2_w3, m_ffn2_w2=m_ffn2_w2, m_ln3_g=m_ln3_g, m_ln3_b=m_ln3_b, v_ffn1_w1=v_ffn1_w1, v_ffn1_w3=v_ffn1_w3, v_ffn1_w2=v_ffn1_w2, v_ln1_g=v_ln1_g, v_ln1_b=v_ln1_b, v_w_in=v_w_in, v_hgrn_lb=v_hgrn_lb, v_hgrn_norm_g=v_hgrn_norm_g, v_mlstm_conv_w=v_mlstm_conv_w, v_mlstm_conv_b=v_mlstm_conv_b, v_mlstm_ig_b=v_mlstm_ig_b, v_mlstm_fg_b=v_mlstm_fg_b, v_mlstm_norm_g=v_mlstm_norm_g, v_w_out=v_w_out, v_ln2_g=v_ln2_g, v_ln2_b=v_ln2_b, v_ffn2_w1=v_ffn2_w1, v_ffn2_w3=v_ffn2_w3, v_ffn2_w2=v_ffn2_w2, v_ln3_g=v_ln3_g, v_ln3_b=v_ln3_b)
    weights = {n: given[n] for n in TWIN_WEIGHTS}
    shared = {n: given[n] for n in SHARED_INPUTS}
    per_example = {n: given[n] for n in ['x']}
    grad_fn = _jax.value_and_grad(_loss, argnums=(0, 1))

    def one_microbatch(ex, loss_target):
        ex = dict(ex)
        diff = ex.pop(TWIN_DIFF_INPUT)
        return grad_fn(weights, diff, {**shared, **ex}, loss_target)

    if N_MICROBATCH == 1:
        loss, (grad_w, grad_x) = one_microbatch(per_example, given["loss_target"])
    else:
        def body(carry, xs):
            loss_sum, grad_sum = carry
            l_k, (gw_k, gx_k) = one_microbatch(xs[0], xs[1])
            with _jax.named_scope("update"):
                return (loss_sum + l_k, _jax.tree.map(_jnp.add, grad_sum, gw_k)), gx_k

        init = (_jnp.zeros((), _jnp.float32), _jax.tree.map(_jnp.zeros_like, weights))
        (loss, grad_w), grad_x = _jax.lax.scan(body, init, (per_example, given["loss_target"]))
    with _jax.named_scope("update"):
        delta_w, new_m, new_v = {}, {}, {}
        for n in TWIN_WEIGHTS:
            delta_w[n], new_m[n], new_v[n] = _adamw(weights[n], grad_w[n], given["m_" + n], given["v_" + n])
    return (loss, grad_x, *[grad_w[n] for n in TWIN_WEIGHTS], *[delta_w[n] for n in TWIN_WEIGHTS],
            *[new_m[n] for n in TWIN_WEIGHTS], *[new_v[n] for n in TWIN_WEIGHTS])
```

```python
import functools
import math

import jax
import jax.numpy as jnp
from jax import lax
from jax.experimental import pallas as pl
from jax.experimental.pallas import tpu as pltpu

F32 = jnp.float32
BF16 = jnp.bfloat16

CHUNK = 64
HGRN_HEAD_DIM = 128
CONV_WIDTH = 5
DN_ALPHA = 2.0 ** 0.25
LN_EPS = 1e-5
NORM_EPS = 1e-6
M_INIT = -1e30
NEG = -1e30
EXP_CLAMP = 80.0
ADAM_LR = 0.001
ADAM_B1 = 0.9
ADAM_B2 = 0.999
ADAM_EPS = 1e-08
ADAM_WD = 0.01
ADAM_STEP = 10
N_DEV = 8
LANES = 128
VMEM_LIMIT = 56 * 1024 * 1024
MESH = pl.DeviceIdType.MESH


def _params(*sem):
    return pltpu.CompilerParams(dimension_semantics=sem, vmem_limit_bytes=VMEM_LIMIT)


def _sigmoid(x):
    return 1.0 / (1.0 + jnp.exp(-x))


def _log_sigmoid(x):
    return jnp.minimum(x, 0.0) - jnp.log(1.0 + jnp.exp(-jnp.abs(x)))


def _dot(a, b, dims):
    return lax.dot_general(a.astype(BF16), b.astype(BF16), (dims, ((), ())),
                           preferred_element_type=F32)


def _dot3(a, b, dims):
    ah = a.astype(BF16)
    al = (a - ah.astype(F32)).astype(BF16)
    bh = b.astype(BF16)
    bl = (b - bh.astype(F32)).astype(BF16)
    d = (dims, ((), ()))
    out = lax.dot_general(ah, bh, d, preferred_element_type=F32)
    out = out + lax.dot_general(ah, bl, d, preferred_element_type=F32)
    return out + lax.dot_general(al, bh, d, preferred_element_type=F32)


_DIMS = {"nn": ((1,), (0,)), "nt": ((1,), (1,)), "tn": ((0,), (0,))}


def _dot_nn(a, b):
    return _dot3(a, b, _DIMS["nn"])


def _dot_nt(a, b):
    return _dot3(a, b, _DIMS["nt"])


def _dot_tn(a, b):
    return _dot3(a, b, _DIMS["tn"])


def _split3(x):
    hi = x.astype(BF16)
    r1 = x - hi.astype(F32)
    mid = r1.astype(BF16)
    lo = (r1 - mid.astype(F32)).astype(BF16)
    return hi, mid, lo


def _dot01(mask01, x, mode="nn"):
    m = mask01.astype(BF16)
    hi, mid, lo = _split3(x)
    d = (_DIMS[mode], ((), ()))
    out = lax.dot_general(m, hi, d, preferred_element_type=F32)
    out = out + lax.dot_general(m, mid, d, preferred_element_type=F32)
    return out + lax.dot_general(m, lo, d, preferred_element_type=F32)


def _matmul(a, b, *, mode="nn", out_dtype=F32, tm=512, tn=512, tk=None,
            add=None, scale=1.0, add_scale=1.0, name):
    if mode == "nn":
        (M, K), (K2, N) = a.shape, b.shape
    elif mode == "nt":
        (M, K), (N, K2) = a.shape, b.shape
    else:
        (K, M), (K2, N) = a.shape, b.shape
    assert K == K2, (a.shape, b.shape, mode)
    tm, tn = min(tm, M), min(tn, N)
    tk = min(tk or K, K)
    assert M % tm == 0 and N % tn == 0 and K % tk == 0, (M, N, K, tm, tn, tk)
    nk = K // tk
    dims = _DIMS[mode]
    has_add = add is not None

    def body(*refs):
        if has_add:
            a_ref, b_ref, add_ref, o_ref = refs[:4]
        else:
            a_ref, b_ref, o_ref = refs[:3]
            add_ref = None
        acc_ref = refs[-1] if nk > 1 else None

        def finish(acc):
            out = acc if scale == 1.0 else acc * scale
            if has_add:
                out = out + add_ref[...].astype(F32) * add_scale
            o_ref[...] = out.astype(o_ref.dtype)

        part = _dot(a_ref[...], b_ref[...], dims)
        if nk == 1:
            finish(part)
        else:
            k = pl.program_id(2)

            @pl.when(k == 0)
            def _():
                acc_ref[...] = part

            @pl.when(k > 0)
            def _():
                acc_ref[...] += part

            @pl.when(k == nk - 1)
            def _():
                finish(acc_ref[...])

    if mode == "tn":
        a_spec = pl.BlockSpec((tk, tm), lambda i, j, k: (k, i))
    else:
        a_spec = pl.BlockSpec((tm, tk), lambda i, j, k: (i, k))
    if mode == "nt":
        b_spec = pl.BlockSpec((tn, tk), lambda i, j, k: (j, k))
    else:
        b_spec = pl.BlockSpec((tk, tn), lambda i, j, k: (k, j))
    o_spec = pl.BlockSpec((tm, tn), lambda i, j, k: (i, j))
    in_specs = [a_spec, b_spec] + ([o_spec] if has_add else [])
    args = (a, b) + ((add,) if has_add else ())
    return pl.pallas_call(
        body, name=name,
        out_shape=jax.ShapeDtypeStruct((M, N), out_dtype),
        grid=(M // tm, N // tn, nk),
        in_specs=in_specs, out_specs=o_spec,
        scratch_shapes=[pltpu.VMEM((tm, tn), F32)] if nk > 1 else [],
        compiler_params=_params("parallel", "parallel", "arbitrary"),
    )(*args)


def _ln_stats(x):
    mu = jnp.mean(x, axis=-1, keepdims=True)
    xc = x - mu
    var = jnp.mean(xc * xc, axis=-1, keepdims=True)
    rstd = lax.rsqrt(var + LN_EPS)
    return xc * rstd, rstd


def _ln_fwd(r, g, b, *, tm=256, name):
    T, D = r.shape
    tm = min(tm, T)

    def body(r_ref, g_ref, b_ref, y_ref):
        xhat, _ = _ln_stats(r_ref[...])
        y_ref[...] = xhat * g_ref[...] + b_ref[...]

    row = pl.BlockSpec((tm, D), lambda i: (i, 0))
    vec = pl.BlockSpec((1, D), lambda i: (0, 0))
    return pl.pallas_call(
        body, name=name, out_shape=jax.ShapeDtypeStruct((T, D), F32),
        grid=(T // tm,), in_specs=[row, vec, vec], out_specs=row,
        compiler_params=_params("parallel"),
    )(r, g, b)


def _ln_bwd(dy, r, g, *, tm=256, name, b=None, target=None):
    T, D = r.shape
    tm = min(tm, T)
    with_loss = target is not None

    def body(*refs):
        if with_loss:
            r_ref, g_ref, b_ref, t_ref, dr_ref, dg_ref, db_ref, loss_ref = refs
        else:
            dy_ref, r_ref, g_ref, dr_ref, dg_ref, db_ref = refs
        i = pl.program_id(0)
        xhat, rstd = _ln_stats(r_ref[...])
        gg = g_ref[...]
        if with_loss:
            err = xhat * gg + b_ref[...] - t_ref[...]
            dyv = err * (1.0 / D)
            part = jnp.sum(jnp.sum(err * err, axis=1, keepdims=True), axis=0, keepdims=True)
            part = jnp.broadcast_to(part * (0.5 / D), (1, LANES))
        else:
            dyv = dy_ref[...]
        dxh = dyv * gg
        m1 = jnp.mean(dxh, axis=-1, keepdims=True)
        m2 = jnp.mean(dxh * xhat, axis=-1, keepdims=True)
        dr_ref[...] = rstd * (dxh - m1 - xhat * m2)
        dgp = jnp.sum(dyv * xhat, axis=0, keepdims=True)
        dbp = jnp.sum(dyv, axis=0, keepdims=True)

        @pl.when(i == 0)
        def _():
            dg_ref[...] = dgp
            db_ref[...] = dbp
            if with_loss:
                loss_ref[...] = part

        @pl.when(i > 0)
        def _():
            dg_ref[...] += dgp
            db_ref[...] += dbp
            if with_loss:
                loss_ref[...] += part

    row = pl.BlockSpec((tm, D), lambda i: (i, 0))
    vec = pl.BlockSpec((1, D), lambda i: (0, 0))
    out_shape = [jax.ShapeDtypeStruct((T, D), F32), jax.ShapeDtypeStruct((1, D), F32),
                 jax.ShapeDtypeStruct((1, D), F32)]
    out_specs = [row, vec, vec]
    if with_loss:
        in_specs, args = [row, vec, vec, row], (r, g, b, target)
        out_shape.append(jax.ShapeDtypeStruct((1, LANES), F32))
        out_specs.append(pl.BlockSpec((1, LANES), lambda i: (0, 0)))
    else:
        in_specs, args = [row, row, vec], (dy, r, g)
    return pl.pallas_call(
        body, name=name, out_shape=out_shape, grid=(T // tm,),
        in_specs=in_specs, out_specs=out_specs,
        compiler_params=_params("arbitrary"),
    )(*args)


def _swiglu_fwd(a, b, *, tm=512, tf=512, name):
    T, F = a.shape
    tm, tf = min(tm, T), min(tf, F)

    def body(a_ref, b_ref, h_ref):
        av = a_ref[...]
        h_ref[...] = (av * _sigmoid(av) * b_ref[...]).astype(h_ref.dtype)

    blk = pl.BlockSpec((tm, tf), lambda i, j: (i, j))
    return pl.pallas_call(
        body, name=name, out_shape=jax.ShapeDtypeStruct((T, F), BF16),
        grid=(T // tm, F // tf), in_specs=[blk, blk], out_specs=blk,
        compiler_params=_params("parallel", "parallel"),
    )(a, b)


def _swiglu_bwd(dh, a, b, *, tm=512, tf=512, name):
    T, F = a.shape
    tm, tf = min(tm, T), min(tf, F)

    def body(dh_ref, a_ref, b_ref, da_ref, db_ref):
        av = a_ref[...]
        d = dh_ref[...]
        sig = _sigmoid(av)
        da_ref[...] = (d * b_ref[...] * sig * (1.0 + av * (1.0 - sig))).astype(da_ref.dtype)
        db_ref[...] = (d * av * sig).astype(db_ref.dtype)

    blk = pl.BlockSpec((tm, tf), lambda i, j: (i, j))
    return pl.pallas_call(
        body, name=name,
        out_shape=[jax.ShapeDtypeStruct((T, F), BF16), jax.ShapeDtypeStruct((T, F), BF16)],
        grid=(T // tm, F // tf), in_specs=[blk, blk, blk], out_specs=[blk, blk],
        compiler_params=_params("parallel", "parallel"),
    )(dh, a, b)


def _chunk_mask(reverse, transpose=False):
    row = lax.broadcasted_iota(jnp.int32, (CHUNK, CHUNK), 0)
    col = lax.broadcasted_iota(jnp.int32, (CHUNK, CHUNK), 1)
    if reverse != transpose:
        return col >= row
    return col <= row


def _hgrn_pre(hq, hf, lb):
    sig = _sigmoid(hf)
    f = lb + (1.0 - lb) * sig
    q = hq * _sigmoid(hq) * (HGRN_HEAD_DIM ** -0.5)
    return q, f, sig


def _hgrn_decays(f, cmf, reverse):
    bc = _dot01(cmf, jnp.log(f))
    last = 0 if reverse else CHUNK - 1
    blast = bc[last:last + 1, :]
    bref = bc[CHUNK // 2:CHUNK // 2 + 1, :]
    eq = jnp.exp(jnp.minimum(bc - bref, EXP_CLAMP))
    ek = jnp.exp(jnp.minimum(bref - bc, EXP_CLAMP))
    return bc, blast, eq, ek


def _hgrn_fwd(z, lb, *, wh, reverse, name):
    T = z.shape[0]
    nch = T // CHUNK
    nh = wh // HGRN_HEAD_DIM
    hd = HGRN_HEAD_DIM

    def ci(i):
        return nch - 1 - i if reverse else i

    def body(hq_ref, hi_ref, hf_ref, lb_ref, o_ref, st_ref, s_ref):
        @pl.when(pl.program_id(0) == 0)
        def _():
            s_ref[...] = jnp.zeros_like(s_ref)

        cm = _chunk_mask(reverse)
        cmf = cm.astype(F32)
        for h in range(nh):
            sl = slice(h * hd, (h + 1) * hd)
            q, f, _ = _hgrn_pre(hq_ref[:, sl], hf_ref[:, sl], lb_ref[:, sl])
            v = hi_ref[:, sl]
            k = 1.0 - f
            bc, blast, eq, ek = _hgrn_decays(f, cmf, reverse)
            st = s_ref[h]
            st_ref[0, h] = st
            att = jnp.where(cm, _dot_nt(q * eq, k * ek), 0.0)
            o_ref[:, sl] = _dot_nt(q * jnp.exp(bc), st) + _dot_nn(att, v)
            s_ref[h] = jnp.exp(blast) * st + _dot_tn(v, k * jnp.exp(blast - bc))

    blk = lambda c: pl.BlockSpec((CHUNK, wh), lambda i: (ci(i), c))
    return pl.pallas_call(
        body, name=name,
        out_shape=[jax.ShapeDtypeStruct((T, wh), F32),
                   jax.ShapeDtypeStruct((nch, nh, hd, hd), F32)],
        grid=(nch,),
        in_specs=[blk(0), blk(1), blk(3 + int(reverse)),
                  pl.BlockSpec((1, wh), lambda i: (0, 0))],
        out_specs=[blk(0), pl.BlockSpec((1, nh, hd, hd), lambda i: (ci(i), 0, 0, 0))],
        scratch_shapes=[pltpu.VMEM((nh, hd, hd), F32)],
        compiler_params=_params("arbitrary"),
    )(z, z, z, lb)


def _hgrn_bwd(z, lb, do, states, prev, *, wh, reverse, name):
    T = z.shape[0]
    nch = T // CHUNK
    nh = wh // HGRN_HEAD_DIM
    hd = HGRN_HEAD_DIM
    has_prev = prev is not None

    def ci(i):
        return i if reverse else nch - 1 - i

    def body(*refs):
        hq_ref, hi_ref, hf_ref, lb_ref, do_ref, st_ref = refs[:6]
        n_in = 8 if has_prev else 6
        dhq_ref, dhi_ref, dhf_ref, dlb_ref, ds_ref, gs_ref = refs[n_in:]

        @pl.when(pl.program_id(0) == 0)
        def _():
            ds_ref[...] = jnp.zeros_like(ds_ref)
            gs_ref[...] = jnp.zeros_like(gs_ref)
            dlb_ref[...] = jnp.zeros_like(dlb_ref)

        cm = _chunk_mask(reverse)
        cmf = cm.astype(F32)
        cmtf = _chunk_mask(reverse, transpose=True).astype(F32)
        for h in range(nh):
            sl = slice(h * hd, (h + 1) * hd)
            hq = hq_ref[:, sl]
            lb = lb_ref[:, sl]
            q, f, sig = _hgrn_pre(hq, hf_ref[:, sl], lb)
            v = hi_ref[:, sl]
            k = 1.0 - f
            bc, blast, eq, ek = _hgrn_decays(f, cmf, reverse)
            ebc = jnp.exp(bc)
            eb2 = jnp.exp(blast - bc)
            st = st_ref[0, h]
            dst = ds_ref[h]
            dov = do_ref[:, sl]
            qh, kh = q * eq, k * ek
            att = jnp.where(cm, _dot_nt(qh, kh), 0.0)
            datt = jnp.where(cm, _dot_nt(dov, v), 0.0)
            dq = _dot_nn(datt, kh) * eq + ebc * _dot_nn(dov, st)
            dk = _dot_tn(datt, qh) * ek + eb2 * _dot_nn(v, dst)
            dv = _dot_tn(att, dov) + _dot_nt(k * eb2, dst)
            ds_ref[h] = jnp.exp(blast) * dst + _dot_tn(dov, q * ebc)
            db = q * dq - k * dk
            dg = _dot01(cmtf, db) + gs_ref[:, sl]
            gs_ref[:, sl] += jnp.sum(db, axis=0, keepdims=True)
            df = dg / f - dk
            dhf_ref[:, sl] = df * (1.0 - lb) * sig * (1.0 - sig)
            dlb_ref[:, sl] += jnp.sum(df * (1.0 - sig), axis=0, keepdims=True)
            sq = _sigmoid(hq)
            dhq = dq * (HGRN_HEAD_DIM ** -0.5) * sq * (1.0 + hq * (1.0 - sq))
            if has_prev:
                dhq = dhq + refs[6][:, sl]
                dv = dv + refs[7][:, sl]
            dhq_ref[:, sl] = dhq
            dhi_ref[:, sl] = dv

    blk = lambda c: pl.BlockSpec((CHUNK, wh), lambda i: (ci(i), c))
    vec = pl.BlockSpec((1, wh), lambda i: (0, 0))
    in_specs = [blk(0), blk(1), blk(3 + int(reverse)), vec, blk(0),
                pl.BlockSpec((1, nh, hd, hd), lambda i: (ci(i), 0, 0, 0))]
    args = [z, z, z, lb, do, states]
    if has_prev:
        in_specs += [blk(0), blk(0)]
        args += list(prev)
    big = jax.ShapeDtypeStruct((T, wh), F32)
    return pl.pallas_call(
        body, name=name,
        out_shape=[big, big, big, jax.ShapeDtypeStruct((1, wh), F32)],
        grid=(nch,), in_specs=in_specs, out_specs=[blk(0), blk(0), blk(0), vec],
        scratch_shapes=[pltpu.VMEM((nh, hd, hd), F32), pltpu.VMEM((1, wh), F32)],
        compiler_params=_params("arbitrary"),
    )(*args)


def _mlstm_intra(cm, cmt, ig_row, ig_col, xf_row, xf_col, m_st):
    lf_row, lf_col = _log_sigmoid(xf_row), _log_sigmoid(xf_col)
    bcol = jnp.sum(jnp.where(cm, lf_row, 0.0), axis=1, keepdims=True)
    brow = jnp.sum(jnp.where(cmt, lf_col, 0.0), axis=0, keepdims=True)
    blast = jnp.sum(lf_row, axis=1, keepdims=True)
    dmat = jnp.where(cm, bcol - brow + ig_row, NEG)
    m_inter = bcol + m_st
    m_t = jnp.maximum(m_inter, jnp.max(dmat, axis=1, keepdims=True))
    p = jnp.exp(dmat - m_t)
    inter = jnp.exp(m_inter - m_t)
    w_col = blast - bcol + ig_col
    m_new = jnp.maximum(blast + m_st, jnp.max(w_col, axis=0, keepdims=True))
    cs = jnp.exp(blast + m_st - m_new)
    kscale = jnp.exp(w_col - m_new)
    return p, inter, m_t, m_new, cs, kscale


def _mlstm_specs(T, wm, nhm, reverse, backward):
    nch = T // CHUNK
    dm = wm // nhm
    ng = 4 * nhm

    def ci(i):
        fwd_order = nch - 1 - i if reverse else i
        return nch - 1 - fwd_order if backward else fwd_order

    row = lambda w, c: pl.BlockSpec((CHUNK, w), lambda i: (ci(i), c))
    gates_row = pl.BlockSpec((1, ng, CHUNK), lambda i: (ci(i), 0, 0))
    bias_row = pl.BlockSpec((1, LANES), lambda i: (0, 0))
    bias_col = pl.BlockSpec((ng, 1), lambda i: (0, 0))
    st_c = pl.BlockSpec((1, nhm, dm, dm), lambda i: (ci(i), 0, 0, 0))
    st_n = pl.BlockSpec((1, nhm, 1, dm), lambda i: (ci(i), 0, 0, 0))
    st_m = pl.BlockSpec((1, nhm, 1, LANES), lambda i: (ci(i), 0, 0, 0))
    return nch, dm, ng, ci, row, gates_row, bias_row, bias_col, st_c, st_n, st_m


def _mlstm_fwd(qk, z, gc, gr, br, bcl, *, wm, nhm, vcol, reverse, name):
    T = qk.shape[0]
    nch, dm, ng, ci, row, gates_row, bias_row, bias_col, st_c, st_n, st_m = _mlstm_specs(
        T, wm, nhm, reverse, False)
    d = int(reverse)

    def body(q_ref, k_ref, v_ref, gc_ref, gr_ref, br_ref, bc_ref,
             h_ref, cst_ref, nst_ref, mst_ref, c_ref, n_ref, m_ref):
        @pl.when(pl.program_id(0) == 0)
        def _():
            c_ref[...] = jnp.zeros_like(c_ref)
            n_ref[...] = jnp.zeros_like(n_ref)
            m_ref[...] = jnp.full_like(m_ref, M_INIT)

        cm = _chunk_mask(reverse)
        cmt = _chunk_mask(reverse, transpose=True)
        G = gc_ref[...] + br_ref[...]
        Gr = gr_ref[0] + bc_ref[...]
        for h in range(nhm):
            sl = slice(h * dm, (h + 1) * dm)
            ii, fi = d * nhm + h, 2 * nhm + d * nhm + h
            m_all = m_ref[h]
            m_st = m_all[:, 0:1]
            p, inter, m_t, m_new, cs, kscale = _mlstm_intra(
                cm, cmt, Gr[ii:ii + 1, :], G[:, ii:ii + 1], Gr[fi:fi + 1, :], G[:, fi:fi + 1], m_st)
            q = q_ref[:, sl] * (dm ** -0.5)
            k = k_ref[:, sl]
            v = v_ref[:, sl]
            ct = c_ref[h]
            n = n_ref[h]
            cst_ref[0, h] = ct
            nst_ref[0, h] = n
            mst_ref[0, h] = m_all
            sc = _dot_nt(q, k) * p
            num = _dot_nn(sc, v) + inter * _dot_nt(q, ct)
            den = jnp.sum(sc, axis=1, keepdims=True) + inter * jnp.sum(q * n, axis=1, keepdims=True)
            h_ref[:, sl] = num / jnp.maximum(jnp.abs(den), jnp.exp(-m_t))
            kw = k * kscale
            c_ref[h] = cs * ct + _dot_tn(v, kw)
            n_ref[h] = cs * n + jnp.sum(kw, axis=0, keepdims=True)
            m_ref[h] = jnp.broadcast_to(m_new, (1, LANES))

    return pl.pallas_call(
        body, name=name,
        out_shape=[jax.ShapeDtypeStruct((T, wm), F32),
                   jax.ShapeDtypeStruct((nch, nhm, dm, dm), F32),
                   jax.ShapeDtypeStruct((nch, nhm, 1, dm), F32),
                   jax.ShapeDtypeStruct((nch, nhm, 1, LANES), F32)],
        grid=(nch,),
        in_specs=[row(wm, 0), row(wm, 1), row(wm, vcol), row(LANES, 0), gates_row, bias_row, bias_col],
        out_specs=[row(wm, 0), st_c, st_n, st_m],
        scratch_shapes=[pltpu.VMEM((nhm, dm, dm), F32), pltpu.VMEM((nhm, 1, dm), F32),
                        pltpu.VMEM((nhm, 1, LANES), F32)],
        compiler_params=_params("arbitrary"),
    )(qk, qk, z, gc, gr, br, bcl)


def _mlstm_bwd(qk, z, gc, gr, br, bcl, dh, states, prev, *, wm, nhm, vcol, reverse, name):
    T = qk.shape[0]
    nch, dm, ng, ci, row, gates_row, bias_row, bias_col, st_c, st_n, st_m = _mlstm_specs(
        T, wm, nhm, reverse, True)
    d = int(reverse)
    has_prev = prev is not None
    n_in = 13 if has_prev else 11

    def body(*refs):
        (q_ref, k_ref, v_ref, gc_ref, gr_ref, br_ref, bc_ref, dh_ref,
         cst_ref, nst_ref, mst_ref) = refs[:11]
        dqk_ref, dv_ref, dgr_ref, dgs_ref, e_ref, en_ref, fs_ref = refs[n_in:]

        @pl.when(pl.program_id(0) == 0)
        def _():
            e_ref[...] = jnp.zeros_like(e_ref)
            en_ref[...] = jnp.zeros_like(en_ref)
            fs_ref[...] = jnp.zeros_like(fs_ref)
            dgs_ref[...] = jnp.zeros_like(dgs_ref)

        cm = _chunk_mask(reverse)
        cmt = _chunk_mask(reverse, transpose=True)
        row_i = lax.broadcasted_iota(jnp.int32, (CHUNK, CHUNK), 0)
        col_i = lax.broadcasted_iota(jnp.int32, (CHUNK, CHUNK), 1)
        eye = row_i == col_i
        G = gc_ref[...] + br_ref[...]
        Gr = gr_ref[0] + bc_ref[...]
        for h in range(nhm):
            sl = slice(h * dm, (h + 1) * dm)
            slk = slice(wm + h * dm, wm + (h + 1) * dm)
            ii, fi = d * nhm + h, 2 * nhm + d * nhm + h
            m_st = mst_ref[0, h][:, 0:1]
            xf_row = Gr[fi:fi + 1, :]
            p, inter, m_t, m_new, cs, kscale = _mlstm_intra(
                cm, cmt, Gr[ii:ii + 1, :], G[:, ii:ii + 1], xf_row, G[:, fi:fi + 1], m_st)
            q = q_ref[:, sl] * (dm ** -0.5)
            k = k_ref[:, sl]
            v = v_ref[:, sl]
            ct = cst_ref[0, h]
            n = nst_ref[0, h]
            et = e_ref[h]
            en = en_ref[h]
            sc = _dot_nt(q, k) * p
            num = _dot_nn(sc, v) + inter * _dot_nt(q, ct)
            den = jnp.sum(sc, axis=1, keepdims=True) + inter * jnp.sum(q * n, axis=1, keepdims=True)
            floor = jnp.exp(-m_t)
            nstab = jnp.maximum(jnp.abs(den), floor)
            hout = num / nstab
            dhv = dh_ref[:, sl]
            gh = dhv / nstab
            dd = -jnp.sum(dhv * hout, axis=1, keepdims=True) / nstab
            dd = dd * jnp.where(jnp.abs(den) > floor, jnp.sign(den), 0.0)
            dsqk = (_dot_nt(gh, v) + dd) * p
            qi = q * inter
            kw = k * kscale
            dq = _dot_nn(dsqk, k) + inter * (_dot_nn(gh, ct) + dd * n)
            dk = _dot_tn(dsqk, q) + kscale * (_dot_nn(v, et) + en)
            dv = _dot_tn(sc, gh) + _dot_nt(kw, et)
            e_ref[h] = cs * et + _dot_tn(gh, qi)
            en_ref[h] = cs * en + jnp.sum(qi * dd, axis=0, keepdims=True)
            di_col = jnp.sum(k * dk, axis=1, keepdims=True)
            df_col = jnp.sum(q * dq, axis=1, keepdims=True) - di_col
            di_row = jnp.sum(jnp.where(eye, di_col, 0.0), axis=0, keepdims=True)
            carry = fs_ref[h][:, 0:1]
            dlf_row = jnp.sum(jnp.where(cm, df_col, 0.0), axis=0, keepdims=True) + carry
            fs_ref[h] = jnp.broadcast_to(carry + jnp.sum(df_col, axis=0, keepdims=True), (1, LANES))
            dxf_row = dlf_row * (1.0 - _sigmoid(xf_row))
            dgr_ref[0, h:h + 1, :] = di_row
            dgr_ref[0, nhm + h:nhm + h + 1, :] = dxf_row
            dgs_ref[h:h + 1, :] += jnp.broadcast_to(jnp.sum(di_row, axis=1, keepdims=True), (1, LANES))
            dgs_ref[nhm + h:nhm + h + 1, :] += jnp.broadcast_to(
                jnp.sum(dxf_row, axis=1, keepdims=True), (1, LANES))
            dq = dq * (dm ** -0.5)
            if has_prev:
                dq = dq + refs[11][:, sl]
                dk = dk + refs[11][:, slk]
                dv = dv + refs[12][:, sl]
            dqk_ref[:, sl] = dq
            dqk_ref[:, slk] = dk
            dv_ref[:, sl] = dv

    in_specs = [row(wm, 0), row(wm, 1), row(wm, vcol), row(LANES, 0), gates_row, bias_row, bias_col,
                row(wm, 0), st_c, st_n, st_m]
    args = [qk, qk, z, gc, gr, br, bcl, dh] + list(states)
    if has_prev:
        in_specs += [row(2 * wm, 0), row(wm, 0)]
        args += list(prev)
    return pl.pallas_call(
        body, name=name,
        out_shape=[jax.ShapeDtypeStruct((T, 2 * wm), F32), jax.ShapeDtypeStruct((T, wm), F32),
                   jax.ShapeDtypeStruct((nch, 2 * nhm, CHUNK), F32),
                   jax.ShapeDtypeStruct((2 * nhm, LANES), F32)],
        grid=(nch,), in_specs=in_specs,
        out_specs=[row(2 * wm, 0), row(wm, 0),
                   pl.BlockSpec((1, 2 * nhm, CHUNK), lambda i: (ci(i), 0, 0)),
                   pl.BlockSpec((2 * nhm, LANES), lambda i: (0, 0))],
        scratch_shapes=[pltpu.VMEM((nhm, dm, dm), F32), pltpu.VMEM((nhm, 1, dm), F32),
                        pltpu.VMEM((nhm, 1, LANES), F32)],
        compiler_params=_params("arbitrary"),
    )(*args)


def _conv_taps(x, w_ref):
    T = x.shape[0]
    t = lax.broadcasted_iota(jnp.int32, x.shape, 0)
    taps = []
    acc = None
    for j in range(CONV_WIDTH):
        s = CONV_WIDTH // 2 - j
        if s == 0:
            xs = x
        else:
            xs = jnp.where((t - s >= 0) & (t - s < T), pltpu.roll(x, s % T, 0), 0.0)
        taps.append(xs)
        term = w_ref[j:j + 1, :] * xs
        acc = term if acc is None else acc + term
    return taps, acc


def _conv_fwd(z, w, b, *, col0, tc=LANES, name):
    T = z.shape[0]
    C2 = w.shape[1]
    assert col0 % tc == 0 and C2 % tc == 0

    def body(z_ref, w_ref, b_ref, o_ref):
        _, acc = _conv_taps(z_ref[...], w_ref)
        c = acc + b_ref[...]
        o_ref[...] = c * _sigmoid(c)

    return pl.pallas_call(
        body, name=name, out_shape=jax.ShapeDtypeStruct((T, C2), F32),
        grid=(C2 // tc,),
        in_specs=[pl.BlockSpec((T, tc), lambda j: (0, col0 // tc + j)),
                  pl.BlockSpec((CONV_WIDTH, tc), lambda j: (0, j)),
                  pl.BlockSpec((1, tc), lambda j: (0, j))],
        out_specs=pl.BlockSpec((T, tc), lambda j: (0, j)),
        compiler_params=_params("parallel"),
    )(z, w, b)


def _conv_bwd(dy, z, w, b, *, col0, tc=LANES, name):
    T = z.shape[0]
    C2 = w.shape[1]

    def body(dy_ref, z_ref, w_ref, b_ref, dx_ref, dw_ref, db_ref):
        taps, acc = _conv_taps(z_ref[...], w_ref)
        c = acc + b_ref[...]
        sg = _sigmoid(c)
        dc = dy_ref[...] * sg * (1.0 + c * (1.0 - sg))
        t = lax.broadcasted_iota(jnp.int32, dc.shape, 0)
        dx = None
        for j in range(CONV_WIDTH):
            s = j - CONV_WIDTH // 2
            if s == 0:
                ds = dc
            else:
                ds = jnp.where((t - s >= 0) & (t - s < T), pltpu.roll(dc, s % T, 0), 0.0)
            term = w_ref[j:j + 1, :] * ds
            dx = term if dx is None else dx + term
            dw_ref[j:j + 1, :] = jnp.sum(dc * taps[j], axis=0, keepdims=True)
        dx_ref[...] = dx
        db_ref[...] = jnp.sum(dc, axis=0, keepdims=True)

    col = pl.BlockSpec((T, tc), lambda j: (0, j))
    wsp = pl.BlockSpec((CONV_WIDTH, tc), lambda j: (0, j))
    bsp = pl.BlockSpec((1, tc), lambda j: (0, j))
    return pl.pallas_call(
        body, name=name,
        out_shape=[jax.ShapeDtypeStruct((T, C2), F32), jax.ShapeDtypeStruct((CONV_WIDTH, C2), F32),
                   jax.ShapeDtypeStruct((1, C2), F32)],
        grid=(C2 // tc,),
        in_specs=[col, pl.BlockSpec((T, tc), lambda j: (0, col0 // tc + j)), wsp, bsp],
        out_specs=[col, wsp, bsp],
        compiler_params=_params("parallel"),
    )(dy, z, w, b)


def _mix_heads(o_fw, o_bw, h_fw, h_bw, wh, wm, nhm):
    out = []
    hd = HGRN_HEAD_DIM
    for h in range(wh // hd):
        sl = slice(h * hd, (h + 1) * hd)
        o = o_fw[:, sl] + o_bw[:, sl]
        r = lax.rsqrt(jnp.mean(o * o, axis=-1, keepdims=True) + NORM_EPS)
        out.append((0, sl, o * r, r))
    dm = wm // nhm
    for h in range(nhm):
        sl = slice(h * dm, (h + 1) * dm)
        x = h_fw[:, sl] + h_bw[:, sl]
        xc = x - jnp.mean(x, axis=-1, keepdims=True)
        r = lax.rsqrt(jnp.mean(xc * xc, axis=-1, keepdims=True) + NORM_EPS)
        out.append((1, sl, xc * r, r))
    return out


def _mix_specs(T, wh, wm, tm, gcol, ocol):
    rowh = pl.BlockSpec((tm, wh), lambda i: (i, 0))
    rowm = pl.BlockSpec((tm, wm), lambda i: (i, 0))
    hg = pl.BlockSpec((tm, wh), lambda i: (i, gcol))
    mo = pl.BlockSpec((tm, wm), lambda i: (i, ocol))
    vh = pl.BlockSpec((1, wh), lambda i: (0, 0))
    vm = pl.BlockSpec((1, wm), lambda i: (0, 0))
    return rowh, rowm, hg, mo, vh, vm


def _mix_fwd(o_fw, o_bw, h_fw, h_bw, z, gh, gm, *, nhm, gcol, ocol, tm=256, name):
    T, wh = o_fw.shape
    wm = h_fw.shape[1]
    tm = min(tm, T)
    rowh, rowm, hg, mo, vh, vm = _mix_specs(T, wh, wm, tm, gcol, ocol)

    def body(of_ref, ob_ref, hf_ref, hb_ref, hg_ref, mo_ref, gh_ref, gm_ref, y_ref):
        heads = _mix_heads(of_ref[...], ob_ref[...], hf_ref[...], hb_ref[...], wh, wm, nhm)
        for grp, sl, nv, _ in heads:
            if grp == 0:
                gate = hg_ref[:, sl]
                gate = gate * _sigmoid(gate)
                y = nv * gh_ref[:, sl] * gate
                y_ref[:, sl] = y.astype(y_ref.dtype)
            else:
                y = nv * gm_ref[:, sl] * _sigmoid(mo_ref[:, sl])
                y_ref[:, slice(wh + sl.start, wh + sl.stop)] = y.astype(y_ref.dtype)

    return pl.pallas_call(
        body, name=name, out_shape=jax.ShapeDtypeStruct((T, wh + wm), BF16),
        grid=(T // tm,),
        in_specs=[rowh, rowh, rowm, rowm, hg, mo, vh, vm],
        out_specs=pl.BlockSpec((tm, wh + wm), lambda i: (i, 0)),
        compiler_params=_params("parallel"),
    )(o_fw, o_bw, h_fw, h_bw, z, z, gh, gm)


def _mix_bwd(dy, o_fw, o_bw, h_fw, h_bw, z, gh, gm, *, nhm, gcol, ocol, tm=256, name):
    T, wh = o_fw.shape
    wm = h_fw.shape[1]
    tm = min(tm, T)
    rowh, rowm, hg, mo, vh, vm = _mix_specs(T, wh, wm, tm, gcol, ocol)

    def body(dy_ref, of_ref, ob_ref, hf_ref, hb_ref, hg_ref, mo_ref, gh_ref, gm_ref,
             do_ref, dh_ref, dhg_ref, dmo_ref, dgh_ref, dgm_ref):
        @pl.when(pl.program_id(0) == 0)
        def _():
            dgh_ref[...] = jnp.zeros_like(dgh_ref)
            dgm_ref[...] = jnp.zeros_like(dgm_ref)

        heads = _mix_heads(of_ref[...], ob_ref[...], hf_ref[...], hb_ref[...], wh, wm, nhm)
        for grp, sl, nv, r in heads:
            if grp == 0:
                d = dy_ref[:, sl]
                x = hg_ref[:, sl]
                sg = _sigmoid(x)
                gate = x * sg
                g = gh_ref[:, sl]
                dgh_ref[:, sl] += jnp.sum(d * nv * gate, axis=0, keepdims=True)
                dhg_ref[:, sl] = d * nv * g * sg * (1.0 + x * (1.0 - sg))
                t = d * g * gate
                do_ref[:, sl] = r * (t - nv * jnp.mean(t * nv, axis=-1, keepdims=True))
            else:
                d = dy_ref[:, slice(wh + sl.start, wh + sl.stop)]
                sg = _sigmoid(mo_ref[:, sl])
                g = gm_ref[:, sl]
                dgm_ref[:, sl] += jnp.sum(d * nv * sg, axis=0, keepdims=True)
                dmo_ref[:, sl] = d * nv * g * sg * (1.0 - sg)
                t = d * g * sg
                dh_ref[:, sl] = r * (t - jnp.mean(t, axis=-1, keepdims=True)
                                     - nv * jnp.mean(t * nv, axis=-1, keepdims=True))

    return pl.pallas_call(
        body, name=name,
        out_shape=[jax.ShapeDtypeStruct((T, wh), F32), jax.ShapeDtypeStruct((T, wm), F32),
                   jax.ShapeDtypeStruct((T, wh), F32), jax.ShapeDtypeStruct((T, wm), F32),
                   jax.ShapeDtypeStruct((1, wh), F32), jax.ShapeDtypeStruct((1, wm), F32)],
        grid=(T // tm,),
        in_specs=[pl.BlockSpec((tm, wh + wm), lambda i: (i, 0)), rowh, rowh, rowm, rowm, hg, mo, vh, vm],
        out_specs=[rowh, rowm, rowh, rowm, vh, vm],
        compiler_params=_params("arbitrary"),
    )(dy, o_fw, o_bw, h_fw, h_bw, z, z, gh, gm)


def _pick(n, pref):
    for c in range(pref - pref % LANES, 0, -LANES):
        if n % c == 0:
            return c
    return n


def _mm(a, b, mode="nn", **kw):
    if mode == "nn":
        (M, K), N = a.shape, b.shape[1]
    elif mode == "nt":
        (M, K), N = a.shape, b.shape[0]
    else:
        (K, M), N = a.shape, b.shape[1]
    tk = _pick(K, 2048) if mode != "tn" else _pick(K, 512)
    return _matmul(a, b, mode=mode, tm=_pick(M, 512), tn=_pick(N, 512), tk=tk, **kw)


def _ffn_fwd(x, w1, w3, w2, tag):
    a = _mm(x, w1, name=f"{tag}_a")
    b = _mm(x, w3, name=f"{tag}_b")
    h = _swiglu_fwd(a, b, tm=_pick(x.shape[0], 512), tf=_pick(a.shape[1], 512), name=f"{tag}_swiglu")
    r = _mm(h, w2, add=x, scale=0.5, add_scale=DN_ALPHA, name=f"{tag}_r")
    return a, b, h, r


def _ffn_bwd(dr, x, a, b, h, w1, w3, w2, tag):
    dh = _mm(dr, w2, "nt", scale=0.5, name=f"{tag}_dh")
    dw2 = _mm(h, dr, "tn", scale=0.5, name=f"{tag}_dw2")
    da, db = _swiglu_bwd(dh, a, b, tm=_pick(x.shape[0], 512), tf=_pick(a.shape[1], 512),
                         name=f"{tag}_dswiglu")
    dw1 = _mm(x, da, "tn", name=f"{tag}_dw1")
    dw3 = _mm(x, db, "tn", name=f"{tag}_dw3")
    t = _mm(da, w1, "nt", add=dr, add_scale=DN_ALPHA, name=f"{tag}_dxa")
    dx = _mm(db, w3, "nt", add=t, name=f"{tag}_dxb")
    return dx, dw1, dw3, dw2


def _local_step(x, target, W, P):
    T, D = x.shape
    wh = P["hgrn_norm_g"].shape[1]
    wm = P["mlstm_norm_g"].shape[1]
    nhm = P["ig_b"].shape[1]
    ng = 4 * nhm
    nch = T // CHUNK
    assert wh == wm and T % CHUNK == 0
    vcol, ocol, gcol = 7, 8, 2
    col0 = 5 * wh

    a1, b1, h1, r1 = _ffn_fwd(x, W["w1a"], W["w3a"], W["w2a"], "ffn1")
    x1 = _ln_fwd(r1, P["ln1_g"], P["ln1_b"], name="ln1")
    zm = _mm(x1, W["win"], name="zm")
    zg = _mm(x1, W["wg"], name="zg")
    gr = zg[:, :ng].reshape(nch, CHUNK, ng).transpose(0, 2, 1)
    bias = jnp.concatenate([P["ig_b"].reshape(-1), P["fg_b"].reshape(-1)])
    br = jnp.zeros((1, LANES), F32).at[0, :ng].set(bias)
    bcl = bias.reshape(ng, 1)
    lbf, lbb = P["lb"][0:1], P["lb"][1:2]
    o_fw, s_fw = _hgrn_fwd(zm, lbf, wh=wh, reverse=False, name="hgrn_fw")
    o_bw, s_bw = _hgrn_fwd(zm, lbb, wh=wh, reverse=True, name="hgrn_bw")
    qk = _conv_fwd(zm, P["conv_w"], P["conv_b"], col0=col0, name="conv")
    h_fw, *st_fw = _mlstm_fwd(qk, zm, zg, gr, br, bcl, wm=wm, nhm=nhm, vcol=vcol, reverse=False,
                              name="mlstm_fw")
    h_bw, *st_bw = _mlstm_fwd(qk, zm, zg, gr, br, bcl, wm=wm, nhm=nhm, vcol=vcol, reverse=True,
                              name="mlstm_bw")
    y = _mix_fwd(o_fw, o_bw, h_fw, h_bw, zm, P["hgrn_norm_g"], P["mlstm_norm_g"],
                 nhm=nhm, gcol=gcol, ocol=ocol, name="mix")
    r2 = _mm(y, W["wout"], add=x1, add_scale=DN_ALPHA, name="r2")
    x2 = _ln_fwd(r2, P["ln2_g"], P["ln2_b"], name="ln2")
    a2, b2, h2, r3 = _ffn_fwd(x2, W["w1b"], W["w3b"], W["w2b"], "ffn2")

    G = {}
    dr3, G["ln3_g"], G["ln3_b"], loss = _ln_bwd(None, r3, P["ln3_g"], b=P["ln3_b"], target=target,
                                                name="ln3_bwd")
    dx2, G["w1b"], G["w3b"], G["w2b"] = _ffn_bwd(dr3, x2, a2, b2, h2, W["w1b"], W["w3b"], W["w2b"], "ffn2")
    dr2, G["ln2_g"], G["ln2_b"] = _ln_bwd(dx2, r2, P["ln2_g"], name="ln2_bwd")
    dy = _mm(dr2, W["wout"], "nt", name="dy")
    G["wout"] = _mm(y, dr2, "tn", name="dwout")
    do, dh, dhg, dmo, G["hgrn_norm_g"], G["mlstm_norm_g"] = _mix_bwd(
        dy, o_fw, o_bw, h_fw, h_bw, zm, P["hgrn_norm_g"], P["mlstm_norm_g"],
        nhm=nhm, gcol=gcol, ocol=ocol, name="mix_bwd")
    dqk_f, dv_f, dgr_f, dgs_f = _mlstm_bwd(qk, zm, zg, gr, br, bcl, dh, st_fw, None, wm=wm, nhm=nhm,
                                           vcol=vcol, reverse=False, name="mlstm_fw_bwd")
    dqk, dv, dgr_b, dgs_b = _mlstm_bwd(qk, zm, zg, gr, br, bcl, dh, st_bw, (dqk_f, dv_f), wm=wm, nhm=nhm,
                                       vcol=vcol, reverse=True, name="mlstm_bw_bwd")
    dmqk, G["conv_w"], G["conv_b"] = _conv_bwd(dqk, zm, P["conv_w"], P["conv_b"], col0=col0,
                                               name="conv_bwd")
    dhq_f, dhi_f, dhf_fw, dlb_f = _hgrn_bwd(zm, lbf, do, s_fw, None, wh=wh, reverse=False,
                                            name="hgrn_fw_bwd")
    dhq, dhi, dhf_bw, dlb_b = _hgrn_bwd(zm, lbb, do, s_bw, (dhq_f, dhi_f), wh=wh, reverse=True,
                                        name="hgrn_bw_bwd")
    G["lb"] = jnp.concatenate([dlb_f, dlb_b], axis=0)
    G["ig_b"] = jnp.stack([dgs_f[:nhm, 0], dgs_b[:nhm, 0]])
    G["fg_b"] = jnp.stack([dgs_f[nhm:, 0], dgs_b[nhm:, 0]])
    dzm = jnp.concatenate([dhq, dhi, dhg, dhf_fw, dhf_bw, dmqk, dv, dmo], axis=1).astype(BF16)
    dgr = jnp.concatenate([dgr_f[:, :nhm], dgr_b[:, :nhm], dgr_f[:, nhm:], dgr_b[:, nhm:]], axis=1)
    dzg = jnp.zeros((T, LANES), F32).at[:, :ng].set(dgr.transpose(0, 2, 1).reshape(T, ng)).astype(BF16)
    G["win"] = _mm(x1, dzm, "tn", name="dwin")
    G["wg"] = _mm(x1, dzg, "tn", name="dwg")
    t = _mm(dzg, W["wg"], "nt", add=dr2, add_scale=DN_ALPHA, name="dx1_g")
    dx1 = _mm(dzm, W["win"], "nt", add=t, name="dx1")
    dr1, G["ln1_g"], G["ln1_b"] = _ln_bwd(dx1, r1, P["ln1_g"], name="ln1_bwd")
    gx, G["w1a"], G["w3a"], G["w2a"] = _ffn_bwd(dr1, x, a1, b1, h1, W["w1a"], W["w3a"], W["w2a"], "ffn1")
    return loss, gx, G


ANY = pl.BlockSpec(memory_space=pl.ANY)


def _place():
    x, y, c = lax.axis_index("x"), lax.axis_index("y"), lax.axis_index("c")
    chips = [(1 - x, y), (x, 1 - y), (1 - x, 1 - y)]
    return x, y, c, chips


def _all_gather(parts, name):
    n = len(parts)

    def body(*refs):
        ins, outs = refs[:n], refs[n:2 * n]
        send_sems, recv_sems, local_sems = refs[2 * n:]
        x, y, c, chips = _place()
        me, sibling = (x, y, c), (x, y, 1 - c)

        def copy(a, k, block, to, src=None):
            bx, by, bc = block
            slot = outs[a].at[4 * bx + 2 * by + bc]
            return pltpu.make_async_remote_copy(
                src_ref=slot if src is None else src, dst_ref=slot,
                send_sem=send_sems.at[a, k], recv_sem=recv_sems.at[a, k],
                device_id=to, device_id_type=MESH)

        started = []
        for a in range(n):
            mine = pltpu.make_async_copy(ins[a], outs[a].at[4 * x + 2 * y + c], local_sems.at[a])
            mine.start()
            started.append(mine)
        sends = []
        for a in range(n):
            first = [copy(a, 0, me, sibling, src=ins[a])]
            first += [copy(a, 1 + j, me, (*chip, c), src=ins[a]) for j, chip in enumerate(chips)]
            for cp in first:
                cp.start()
            sends += first
        for a in range(n):
            for j, chip in enumerate(chips):
                copy(a, 1 + j, (*chip, c), me).wait_recv()
                fwd = copy(a, 4 + j, (*chip, c), sibling)
                fwd.start()
                sends.append(fwd)
        for a in range(n):
            copy(a, 0, sibling, me).wait_recv()
            for j, chip in enumerate(chips):
                copy(a, 4 + j, (*chip, 1 - c), me).wait_recv()
        for cp in sends:
            cp.wait_send()
        for mine in started:
            mine.wait()

    return pl.pallas_call(
        body, name=name,
        out_shape=[jax.ShapeDtypeStruct((N_DEV,) + p.shape, p.dtype) for p in parts],
        in_specs=[ANY] * n, out_specs=[ANY] * n,
        scratch_shapes=[pltpu.SemaphoreType.DMA((n, 7)), pltpu.SemaphoreType.DMA((n, 7)),
                        pltpu.SemaphoreType.DMA((n,))],
    )(*parts)


def _pair_exchange(g_send, name):
    def body(g_ref, land_ref, send_sem, recv_sem):
        x, y, c, _ = _place()
        cp = pltpu.make_async_remote_copy(
            src_ref=g_ref, dst_ref=land_ref, send_sem=send_sem, recv_sem=recv_sem,
            device_id=(x, y, 1 - c), device_id_type=MESH)
        cp.start()
        cp.wait()

    return pl.pallas_call(
        body, name=name, out_shape=jax.ShapeDtypeStruct(g_send.shape, g_send.dtype),
        in_specs=[ANY], out_specs=ANY,
        scratch_shapes=[pltpu.SemaphoreType.DMA, pltpu.SemaphoreType.DMA],
    )(g_send)


def _chip_exchange(p, name):
    def body(p_ref, land_ref, send_sems, recv_sems, local_sem):
        x, y, c, chips = _place()
        mine = 2 * x + y
        own = pltpu.make_async_copy(p_ref.at[mine], land_ref.at[mine], local_sem)
        own.start()

        def copy(j, chip):
            return pltpu.make_async_remote_copy(
                src_ref=p_ref.at[2 * chip[0] + chip[1]], dst_ref=land_ref.at[mine],
                send_sem=send_sems.at[j], recv_sem=recv_sems.at[j],
                device_id=(*chip, c), device_id_type=MESH)

        cps = [copy(j, chip) for j, chip in enumerate(chips)]
        for cp in cps:
            cp.start()
        for j, chip in enumerate(chips):
            pltpu.make_async_remote_copy(
                src_ref=p_ref.at[mine], dst_ref=land_ref.at[2 * chip[0] + chip[1]],
                send_sem=send_sems.at[j], recv_sem=recv_sems.at[j],
                device_id=(*chip, c), device_id_type=MESH).wait_recv()
        for cp in cps:
            cp.wait_send()
        own.wait()

    return pl.pallas_call(
        body, name=name, out_shape=jax.ShapeDtypeStruct(p.shape, p.dtype),
        in_specs=[ANY], out_specs=ANY,
        scratch_shapes=[pltpu.SemaphoreType.DMA((3,)), pltpu.SemaphoreType.DMA((3,)),
                        pltpu.SemaphoreType.DMA],
    )(p)


def _all_reduce_small(buf, name):
    rows = buf.shape[0]

    def body(b_ref, o_ref, slots, send_sems, recv_sems):
        x, y, c, _ = _place()
        me = 4 * x + 2 * y + c
        slots[me] = b_ref[...]
        cps = []
        for k in range(1, N_DEV):
            fx, fy, fc = (k >> 2) & 1, (k >> 1) & 1, k & 1
            peer = (x ^ fx, y ^ fy, c ^ fc)
            cps.append(pltpu.make_async_remote_copy(
                src_ref=b_ref, dst_ref=slots.at[me],
                send_sem=send_sems.at[k - 1], recv_sem=recv_sems.at[k - 1],
                device_id=peer, device_id_type=MESH))
        for cp in cps:
            cp.start()
        for k in range(1, N_DEV):
            fx, fy, fc = (k >> 2) & 1, (k >> 1) & 1, k & 1
            src = 4 * (x ^ fx) + 2 * (y ^ fy) + (c ^ fc)
            pltpu.make_async_remote_copy(
                src_ref=b_ref, dst_ref=slots.at[src],
                send_sem=send_sems.at[k - 1], recv_sem=recv_sems.at[k - 1],
                device_id=(x ^ fx, y ^ fy, c ^ fc), device_id_type=MESH).wait_recv()
        for cp in cps:
            cp.wait_send()
        acc = slots[0]
        for k in range(1, N_DEV):
            acc = acc + slots[k]
        o_ref[...] = acc

    vm = pl.BlockSpec(memory_space=pltpu.VMEM)
    return pl.pallas_call(
        body, name=name, out_shape=jax.ShapeDtypeStruct(buf.shape, F32),
        in_specs=[vm], out_specs=vm,
        scratch_shapes=[pltpu.VMEM((N_DEV, rows, LANES), F32),
                        pltpu.SemaphoreType.DMA((N_DEV - 1,)), pltpu.SemaphoreType.DMA((N_DEV - 1,))],
    )(buf)


def _row_tile(rows, pref=2048):
    for t in range(min(pref, rows) - min(pref, rows) % 8, 0, -8):
        if rows % t == 0:
            return t
    return rows


def _add2(a, b, name):
    shp = a.shape
    a2, b2 = a.reshape(-1, LANES), b.reshape(-1, LANES)
    tr = _row_tile(a2.shape[0])

    def body(a_ref, b_ref, o_ref):
        o_ref[...] = a_ref[...] + b_ref[...]

    blk = pl.BlockSpec((tr, LANES), lambda i: (i, 0))
    return pl.pallas_call(
        body, name=name, out_shape=jax.ShapeDtypeStruct(a2.shape, F32),
        grid=(a2.shape[0] // tr,), in_specs=[blk, blk], out_specs=blk,
        compiler_params=_params("parallel"),
    )(a2, b2).reshape(shp)


def _sum4(land, name):
    rows = land.shape[1]
    tr = _row_tile(rows)

    def body(a_ref, b_ref, c_ref, d_ref, o_ref):
        o_ref[...] = ((a_ref[0] + b_ref[0]) + c_ref[0]) + d_ref[0]

    slab = lambda k: pl.BlockSpec((1, tr, LANES), lambda i: (k, i, 0))
    return pl.pallas_call(
        body, name=name, out_shape=jax.ShapeDtypeStruct((rows, LANES), F32),
        grid=(rows // tr,), in_specs=[slab(0), slab(1), slab(2), slab(3)],
        out_specs=pl.BlockSpec((tr, LANES), lambda i: (i, 0)),
        compiler_params=_params("parallel"),
    )(land, land, land, land)


def _adamw(w, g, m, v, name):
    shp = w.shape
    cols = shp[-1]
    w2, g2, m2, v2 = (t.reshape(-1, cols) for t in (w, g, m, v))
    rows = w2.shape[0]
    tr = _row_tile(rows, 256)
    c1 = 1.0 / (1.0 - ADAM_B1 ** ADAM_STEP)
    c2 = 1.0 / (1.0 - ADAM_B2 ** ADAM_STEP)

    def body(w_ref, g_ref, m_ref, v_ref, d_ref, nm_ref, nv_ref):
        gv = g_ref[...]
        nm = ADAM_B1 * m_ref[...] + (1.0 - ADAM_B1) * gv
        nv = ADAM_B2 * v_ref[...] + (1.0 - ADAM_B2) * (gv * gv)
        nm_ref[...] = nm
        nv_ref[...] = nv
        d_ref[...] = -ADAM_LR * ((nm * c1) / (jnp.sqrt(nv * c2) + ADAM_EPS) + ADAM_WD * w_ref[...])

    blk = pl.BlockSpec((tr, cols), lambda i: (i, 0))
    out = jax.ShapeDtypeStruct((rows, cols), F32)
    d, nm, nv = pl.pallas_call(
        body, name=name, out_shape=[out, out, out], grid=(rows // tr,),
        in_specs=[blk] * 4, out_specs=[blk] * 3, compiler_params=_params("parallel"),
    )(w2, g2, m2, v2)
    return d.reshape(shp), nm.reshape(shp), nv.reshape(shp)


def _pack_rows(arrs):
    rows = []
    for a in arrs:
        flat = a.reshape(-1).astype(F32)
        pad = (-flat.shape[0]) % LANES
        if pad:
            flat = jnp.concatenate([flat, jnp.zeros((pad,), F32)])
        rows.append(flat.reshape(-1, LANES))
    out = jnp.concatenate(rows, axis=0)
    pad = (-out.shape[0]) % 8
    if pad:
        out = jnp.concatenate([out, jnp.zeros((pad, LANES), F32)], axis=0)
    return out


def _unpack_rows(buf, shapes):
    outs, r = [], 0
    for s in shapes:
        n = math.prod(s)
        nr = -(-n // LANES)
        outs.append(buf[r:r + nr].reshape(-1)[:n].reshape(s))
        r += nr
    return outs


BIG = ("ffn1_w1", "ffn1_w3", "ffn1_w2", "w_in", "w_out", "ffn2_w1", "ffn2_w3", "ffn2_w2")
ROW_SHARDED = ("ffn1_w2", "w_out", "ffn2_w2")
SMALL = ("ln1_g", "ln1_b", "hgrn_lb", "hgrn_norm_g", "mlstm_conv_w", "mlstm_conv_b", "mlstm_ig_b",
         "mlstm_fg_b", "mlstm_norm_g", "ln2_g", "ln2_b", "ln3_g", "ln3_b")
WEIGHTS = ("ffn1_w1", "ffn1_w3", "ffn1_w2", "ln1_g", "ln1_b", "w_in", "hgrn_lb", "hgrn_norm_g",
           "mlstm_conv_w", "mlstm_conv_b", "mlstm_ig_b", "mlstm_fg_b", "mlstm_norm_g", "w_out",
           "ln2_g", "ln2_b", "ffn2_w1", "ffn2_w3", "ffn2_w2", "ln3_g", "ln3_b")


def _full_from_shards(sh, row_sharded):
    if row_sharded:
        return sh.reshape(-1, sh.shape[2])
    return sh.transpose(1, 0, 2).reshape(sh.shape[1], -1)


def _shards_for_core(full, row_sharded, core):
    if row_sharded:
        t = full.reshape(4, 2, -1, full.shape[1])
        t = lax.dynamic_index_in_dim(t, core, 1, keepdims=False)
        return t.reshape(4, -1)
    t = full.reshape(full.shape[0], 4, 2, -1)
    t = lax.dynamic_index_in_dim(t, core, 2, keepdims=False)
    return t.transpose(1, 0, 2).reshape(4, -1)


def kernel(x, ffn1_w1, ffn1_w3, ffn1_w2, ln1_g, ln1_b, w_in, hgrn_lb, hgrn_norm_g, mlstm_conv_w, mlstm_conv_b, mlstm_ig_b, mlstm_fg_b, mlstm_norm_g, w_out, ln2_g, ln2_b, ffn2_w1, ffn2_w3, ffn2_w2, ln3_g, ln3_b, loss_target, m_ffn1_w1, m_ffn1_w3, m_ffn1_w2, m_ln1_g, m_ln1_b, m_w_in, m_hgrn_lb, m_hgrn_norm_g, m_mlstm_conv_w, m_mlstm_conv_b, m_mlstm_ig_b, m_mlstm_fg_b, m_mlstm_norm_g, m_w_out, m_ln2_g, m_ln2_b, m_ffn2_w1, m_ffn2_w3, m_ffn2_w2, m_ln3_g, m_ln3_b, v_ffn1_w1, v_ffn1_w3, v_ffn1_w2, v_ln1_g, v_ln1_b, v_w_in, v_hgrn_lb, v_hgrn_norm_g, v_mlstm_conv_w, v_mlstm_conv_b, v_mlstm_ig_b, v_mlstm_fg_b, v_mlstm_norm_g, v_w_out, v_ln2_g, v_ln2_b, v_ffn2_w1, v_ffn2_w3, v_ffn2_w2, v_ln3_g, v_ln3_b):
    args = (ffn1_w1, ffn1_w3, ffn1_w2, ln1_g, ln1_b, w_in, hgrn_lb, hgrn_norm_g, mlstm_conv_w,
            mlstm_conv_b, mlstm_ig_b, mlstm_fg_b, mlstm_norm_g, w_out, ln2_g, ln2_b, ffn2_w1, ffn2_w3,
            ffn2_w2, ln3_g, ln3_b)
    ms = (m_ffn1_w1, m_ffn1_w3, m_ffn1_w2, m_ln1_g, m_ln1_b, m_w_in, m_hgrn_lb, m_hgrn_norm_g,
          m_mlstm_conv_w, m_mlstm_conv_b, m_mlstm_ig_b, m_mlstm_fg_b, m_mlstm_norm_g, m_w_out, m_ln2_g,
          m_ln2_b, m_ffn2_w1, m_ffn2_w3, m_ffn2_w2, m_ln3_g, m_ln3_b)
    vs = (v_ffn1_w1, v_ffn1_w3, v_ffn1_w2, v_ln1_g, v_ln1_b, v_w_in, v_hgrn_lb, v_hgrn_norm_g,
          v_mlstm_conv_w, v_mlstm_conv_b, v_mlstm_ig_b, v_mlstm_fg_b, v_mlstm_norm_g, v_w_out, v_ln2_g,
          v_ln2_b, v_ffn2_w1, v_ffn2_w3, v_ffn2_w2, v_ln3_g, v_ln3_b)
    w = dict(zip(WEIGHTS, args))
    m = dict(zip(WEIGHTS, ms))
    v = dict(zip(WEIGHTS, vs))
    core = lax.axis_index("c")
    dev = 4 * lax.axis_index("x") + 2 * lax.axis_index("y") + core
    D = x.shape[-1]
    wh = hgrn_norm_g.shape[-1]
    nhm = mlstm_ig_b.shape[-1]
    ng = 4 * nhm

    shard2d = {k: w[k][0] for k in BIG}
    flat = jnp.concatenate([shard2d[k].astype(BF16).reshape(-1) for k in BIG]).reshape(-1, LANES)
    small_parts = _pack_rows([hgrn_lb, mlstm_conv_w])
    gathered, gathered_small = _all_gather([flat, small_parts], "all_gather")
    full, r = {}, 0
    for k in BIG:
        nr = shard2d[k].size // LANES
        sh = gathered[:, r:r + nr].reshape((N_DEV,) + shard2d[k].shape)
        full[k] = _full_from_shards(sh, k in ROW_SHARDED)
        r += nr
    lb_sh, cw_sh = zip(*[_unpack_rows(gathered_small[i], [hgrn_lb.shape, mlstm_conv_w.shape])
                         for i in range(N_DEV)])
    hgrn_lb_full = jnp.concatenate(lb_sh, axis=-1)
    conv_w_full = jnp.concatenate(cw_sh, axis=-1)[0]
    ncols = full["w_in"].shape[1]
    W = dict(w1a=full["ffn1_w1"], w3a=full["ffn1_w3"], w2a=full["ffn1_w2"],
             win=full["w_in"][:, :ncols - ng],
             wg=jnp.zeros((D, LANES), BF16).at[:, :ng].set(full["w_in"][:, ncols - ng:]),
             wout=full["w_out"], w1b=full["ffn2_w1"], w3b=full["ffn2_w3"], w2b=full["ffn2_w2"])
    p_lb = jax.nn.softmax(hgrn_lb_full, axis=1)
    P = dict(ln1_g=ln1_g, ln1_b=ln1_b, ln2_g=ln2_g, ln2_b=ln2_b, ln3_g=ln3_g, ln3_b=ln3_b,
             lb=p_lb[:, 0], hgrn_norm_g=hgrn_norm_g, conv_w=conv_w_full, conv_b=mlstm_conv_b,
             ig_b=mlstm_ig_b[0], fg_b=mlstm_fg_b[0], mlstm_norm_g=mlstm_norm_g)

    loss_row, grad_x, G = _local_step(x[0], loss_target[0], W, P)

    Gfull = {"ffn1_w1": G["w1a"], "ffn1_w3": G["w3a"], "ffn1_w2": G["w2a"],
             "w_in": jnp.concatenate([G["win"], G["wg"][:, :ng]], axis=1), "w_out": G["wout"],
             "ffn2_w1": G["w1b"], "ffn2_w3": G["w3b"], "ffn2_w2": G["w2b"]}

    def packed(c_sel):
        t = jnp.concatenate([_shards_for_core(Gfull[k], k in ROW_SHARDED, c_sel) for k in BIG], axis=1)
        return t.reshape(4, -1, LANES)

    g_keep, g_send = packed(core), packed(1 - core)
    part = _add2(g_keep, _pair_exchange(g_send, "rs_pair"), "rs_pair_add")
    g_flat = _sum4(_chip_exchange(part, "rs_chips"), "rs_sum")
    grads, r = {}, 0
    for k in BIG:
        nr = shard2d[k].size // LANES
        grads[k] = g_flat[r:r + nr].reshape(w[k].shape)
        r += nr

    dlb = G["lb"]
    g_lb = jnp.stack([dlb * p_lb[:, 0] * (1.0 - p_lb[:, 0]), -dlb * p_lb[:, 0] * p_lb[:, 1]], axis=1)
    small_full = {"ln1_g": G["ln1_g"], "ln1_b": G["ln1_b"], "hgrn_lb": g_lb, "hgrn_norm_g": G["hgrn_norm_g"],
                  "mlstm_conv_w": G["conv_w"][None], "mlstm_conv_b": G["conv_b"],
                  "mlstm_ig_b": G["ig_b"][None], "mlstm_fg_b": G["fg_b"][None],
                  "mlstm_norm_g": G["mlstm_norm_g"], "ln2_g": G["ln2_g"], "ln2_b": G["ln2_b"],
                  "ln3_g": G["ln3_g"], "ln3_b": G["ln3_b"]}
    small_list = [small_full[k] for k in SMALL] + [loss_row]
    reduced = _all_reduce_small(_pack_rows(small_list), "all_reduce_small")
    red = _unpack_rows(reduced, [a.shape for a in small_list])
    loss = red[-1][0, 0]
    for k, gk in zip(SMALL, red[:-1]):
        if k in ("hgrn_lb", "mlstm_conv_w"):
            n = w[k].shape[-1]
            gk = lax.dynamic_slice_in_dim(gk, dev * n, n, axis=gk.ndim - 1)
        grads[k] = gk

    delta, new_m, new_v = {}, {}, {}
    for k in BIG:
        delta[k], new_m[k], new_v[k] = _adamw(w[k], grads[k], m[k], v[k], "adamw_" + k)
    sm = _adamw(_pack_rows([w[k] for k in SMALL]), _pack_rows([grads[k] for k in SMALL]),
                _pack_rows([m[k] for k in SMALL]), _pack_rows([v[k] for k in SMALL]), "adamw_small")
    shapes = [w[k].shape for k in SMALL]
    for dst, buf in zip((delta, new_m, new_v), sm):
        for k, val in zip(SMALL, _unpack_rows(buf, shapes)):
            dst[k] = val
    return (loss, grad_x[None], *[grads[k] for k in WEIGHTS], *[delta[k] for k in WEIGHTS],
            *[new_m[k] for k in WEIGHTS], *[new_v[k] for k in WEIGHTS])
```

```python
import math

import jax
import jax.numpy as jnp
from jax import lax
from jax.experimental import pallas as pl
from jax.experimental.pallas import tpu as pltpu

F32 = jnp.float32
BF16 = jnp.bfloat16

CHUNK = 64
HGRN_HEAD_DIM = 128
CONV_WIDTH = 5
DN_ALPHA = 2.0 ** 0.25
LN_EPS = 1e-5
NORM_EPS = 1e-6
M_INIT = -1e30
NEG = -1e30
EXP_CLAMP = 80.0
ADAM_LR = 0.001
ADAM_B1 = 0.9
ADAM_B2 = 0.999
ADAM_EPS = 1e-08
ADAM_WD = 0.01
ADAM_STEP = 10
N_DEV = 8
LANES = 128
VMEM_LIMIT = 56 * 1024 * 1024
MESH = pl.DeviceIdType.MESH


def _params(*sem):
    return pltpu.CompilerParams(dimension_semantics=sem, vmem_limit_bytes=VMEM_LIMIT)


def _sigmoid(x):
    return 1.0 / (1.0 + jnp.exp(-x))


def _log_sigmoid(x):
    return jnp.minimum(x, 0.0) - jnp.log(1.0 + jnp.exp(-jnp.abs(x)))


def _dot(a, b, dims):
    return lax.dot_general(a.astype(BF16), b.astype(BF16), (dims, ((), ())),
                           preferred_element_type=F32)


def _dot3(a, b, dims):
    ah = a.astype(BF16)
    al = (a - ah.astype(F32)).astype(BF16)
    bh = b.astype(BF16)
    bl = (b - bh.astype(F32)).astype(BF16)
    d = (dims, ((), ()))
    out = lax.dot_general(ah, bh, d, preferred_element_type=F32)
    out = out + lax.dot_general(ah, bl, d, preferred_element_type=F32)
    return out + lax.dot_general(al, bh, d, preferred_element_type=F32)


_DIMS = {"nn": ((1,), (0,)), "nt": ((1,), (1,)), "tn": ((0,), (0,))}


def _dot_nn(a, b):
    return _dot3(a, b, _DIMS["nn"])


def _dot_nt(a, b):
    return _dot3(a, b, _DIMS["nt"])


def _dot_tn(a, b):
    return _dot3(a, b, _DIMS["tn"])


def _split3(x):
    hi = x.astype(BF16)
    r1 = x - hi.astype(F32)
    mid = r1.astype(BF16)
    lo = (r1 - mid.astype(F32)).astype(BF16)
    return hi, mid, lo


def _dot01(mask01, x, mode="nn"):
    m = mask01.astype(BF16)
    hi, mid, lo = _split3(x)
    d = (_DIMS[mode], ((), ()))
    out = lax.dot_general(m, hi, d, preferred_element_type=F32)
    out = out + lax.dot_general(m, mid, d, preferred_element_type=F32)
    return out + lax.dot_general(m, lo, d, preferred_element_type=F32)


def _matmul(a, b, *, mode="nn", out_dtype=F32, tm=512, tn=512, tk=None,
            add=None, scale=1.0, add_scale=1.0, name):
    if mode == "nn":
        (M, K), (K2, N) = a.shape, b.shape
    elif mode == "nt":
        (M, K), (N, K2) = a.shape, b.shape
    else:
        (K, M), (K2, N) = a.shape, b.shape
    assert K == K2, (a.shape, b.shape, mode)
    tm, tn = min(tm, M), min(tn, N)
    tk = min(tk or K, K)
    assert M % tm == 0 and N % tn == 0 and K % tk == 0, (M, N, K, tm, tn, tk)
    nk = K // tk
    dims = _DIMS[mode]
    has_add = add is not None

    def body(*refs):
        if has_add:
            a_ref, b_ref, add_ref, o_ref = refs[:4]
        else:
            a_ref, b_ref, o_ref = refs[:3]
            add_ref = None
        acc_ref = refs[-1] if nk > 1 else None

        def finish(acc):
            out = acc if scale == 1.0 else acc * scale
            if has_add:
                out = out + add_ref[...].astype(F32) * add_scale
            o_ref[...] = out.astype(o_ref.dtype)

        part = _dot(a_ref[...], b_ref[...], dims)
        if nk == 1:
            finish(part)
        else:
            k = pl.program_id(2)

            @pl.when(k == 0)
            def _():
                acc_ref[...] = part

            @pl.when(k > 0)
            def _():
                acc_ref[...] += part

            @pl.when(k == nk - 1)
            def _():
                finish(acc_ref[...])

    if mode == "tn":
        a_spec = pl.BlockSpec((tk, tm), lambda i, j, k: (k, i))
    else:
        a_spec = pl.BlockSpec((tm, tk), lambda i, j, k: (i, k))
    if mode == "nt":
        b_spec = pl.BlockSpec((tn, tk), lambda i, j, k: (j, k))
    else:
        b_spec = pl.BlockSpec((tk, tn), lambda i, j, k: (k, j))
    o_spec = pl.BlockSpec((tm, tn), lambda i, j, k: (i, j))
    in_specs = [a_spec, b_spec] + ([o_spec] if has_add else [])
    args = (a, b) + ((add,) if has_add else ())
    return pl.pallas_call(
        body, name=name,
        out_shape=jax.ShapeDtypeStruct((M, N), out_dtype),
        grid=(M // tm, N // tn, nk),
        in_specs=in_specs, out_specs=o_spec,
        scratch_shapes=[pltpu.VMEM((tm, tn), F32)] if nk > 1 else [],
        compiler_params=_params("parallel", "parallel", "arbitrary"),
    )(*args)


def _ln_stats(x):
    mu = jnp.mean(x, axis=-1, keepdims=True)
    xc = x - mu
    var = jnp.mean(xc * xc, axis=-1, keepdims=True)
    rstd = lax.rsqrt(var + LN_EPS)
    return xc * rstd, rstd


def _ln_fwd(r, g, b, *, tm=256, name):
    T, D = r.shape
    tm = min(tm, T)

    def body(r_ref, g_ref, b_ref, y_ref, yb_ref, yt_ref):
        xhat, _ = _ln_stats(r_ref[...])
        y = xhat * g_ref[...] + b_ref[...]
        y_ref[...] = y
        yb_ref[...] = y.astype(BF16)
        yt_ref[...] = y.T.astype(BF16)

    row = pl.BlockSpec((tm, D), lambda i: (i, 0))
    vec = pl.BlockSpec((1, D), lambda i: (0, 0))
    return pl.pallas_call(
        body, name=name,
        out_shape=[jax.ShapeDtypeStruct((T, D), F32), jax.ShapeDtypeStruct((T, D), BF16),
                   jax.ShapeDtypeStruct((D, T), BF16)],
        grid=(T // tm,), in_specs=[row, vec, vec],
        out_specs=[row, row, pl.BlockSpec((D, tm), lambda i: (0, i))],
        compiler_params=_params("parallel"),
    )(r, g, b)


def _cast_t(x, *, tm=256, name):
    T, D = x.shape
    tm = min(tm, T)

    def body(x_ref, xb_ref, xt_ref):
        xv = x_ref[...]
        xb_ref[...] = xv.astype(BF16)
        xt_ref[...] = xv.T.astype(BF16)

    row = pl.BlockSpec((tm, D), lambda i: (i, 0))
    return pl.pallas_call(
        body, name=name,
        out_shape=[jax.ShapeDtypeStruct((T, D), BF16), jax.ShapeDtypeStruct((D, T), BF16)],
        grid=(T // tm,), in_specs=[row], out_specs=[row, pl.BlockSpec((D, tm), lambda i: (0, i))],
        compiler_params=_params("parallel"),
    )(x)


def _ln_bwd(dy, r, g, *, tm=256, name, b=None, target=None):
    T, D = r.shape
    tm = min(tm, T)
    with_loss = target is not None

    def body(*refs):
        if with_loss:
            r_ref, g_ref, b_ref, t_ref, dr_ref, drb_ref, drt_ref, dg_ref, db_ref, loss_ref = refs
        else:
            dy_ref, r_ref, g_ref, dr_ref, drb_ref, drt_ref, dg_ref, db_ref = refs
        i = pl.program_id(0)
        xhat, rstd = _ln_stats(r_ref[...])
        gg = g_ref[...]
        if with_loss:
            err = xhat * gg + b_ref[...] - t_ref[...]
            dyv = err * (1.0 / D)
            part = jnp.sum(jnp.sum(err * err, axis=1, keepdims=True), axis=0, keepdims=True)
            part = jnp.broadcast_to(part * (0.5 / D), (1, LANES))
        else:
            dyv = dy_ref[...]
        dxh = dyv * gg
        m1 = jnp.mean(dxh, axis=-1, keepdims=True)
        m2 = jnp.mean(dxh * xhat, axis=-1, keepdims=True)
        dr = rstd * (dxh - m1 - xhat * m2)
        dr_ref[...] = dr
        drb_ref[...] = dr.astype(BF16)
        drt_ref[...] = dr.T.astype(BF16)
        dgp = jnp.sum(dyv * xhat, axis=0, keepdims=True)
        dbp = jnp.sum(dyv, axis=0, keepdims=True)

        @pl.when(i == 0)
        def _():
            dg_ref[...] = dgp
            db_ref[...] = dbp
            if with_loss:
                loss_ref[...] = part

        @pl.when(i > 0)
        def _():
            dg_ref[...] += dgp
            db_ref[...] += dbp
            if with_loss:
                loss_ref[...] += part

    row = pl.BlockSpec((tm, D), lambda i: (i, 0))
    vec = pl.BlockSpec((1, D), lambda i: (0, 0))
    out_shape = [jax.ShapeDtypeStruct((T, D), F32), jax.ShapeDtypeStruct((T, D), BF16),
                 jax.ShapeDtypeStruct((D, T), BF16), jax.ShapeDtypeStruct((1, D), F32),
                 jax.ShapeDtypeStruct((1, D), F32)]
    out_specs = [row, row, pl.BlockSpec((D, tm), lambda i: (0, i)), vec, vec]
    if with_loss:
        in_specs, args = [row, vec, vec, row], (r, g, b, target)
        out_shape.append(jax.ShapeDtypeStruct((1, LANES), F32))
        out_specs.append(pl.BlockSpec((1, LANES), lambda i: (0, 0)))
    else:
        in_specs, args = [row, row, vec], (dy, r, g)
    return pl.pallas_call(
        body, name=name, out_shape=out_shape, grid=(T // tm,),
        in_specs=in_specs, out_specs=out_specs,
        compiler_params=_params("arbitrary"),
    )(*args)


def _ffn_up(xb, w1g, w3g, *, tm=512, name):
    S, D, Fs = w1g.shape
    T = xb.shape[0]
    tm = _pick(T, tm)

    def body(x_ref, w1_ref, w3_ref, a_ref, b_ref, h_ref):
        xv = x_ref[...]
        a = _dot(xv, w1_ref[0], _DIMS["nn"])
        b = _dot(xv, w3_ref[0], _DIMS["nn"])
        a_ref[0] = a
        b_ref[0] = b
        h_ref[0] = (a * _sigmoid(a) * b).astype(BF16)

    wsp = pl.BlockSpec((1, D, Fs), lambda j, i: (j, 0, 0))
    osp = pl.BlockSpec((1, tm, Fs), lambda j, i: (j, i, 0))
    act = jax.ShapeDtypeStruct((S, T, Fs), F32)
    return pl.pallas_call(
        body, name=name, out_shape=[act, act, jax.ShapeDtypeStruct((S, T, Fs), BF16)],
        grid=(S, T // tm),
        in_specs=[pl.BlockSpec((tm, D), lambda j, i: (i, 0)), wsp, wsp],
        out_specs=[osp, osp, osp],
        compiler_params=_params("parallel", "parallel"),
    )(xb, w1g, w3g)


def _ffn_down(h, w2g, x, *, tm=512, name):
    S, Fs, D = w2g.shape
    T = x.shape[0]
    tm = _pick(T, tm)

    def body(h_ref, w2_ref, x_ref, r_ref, acc_ref):
        j = pl.program_id(1)
        part = _dot(h_ref[0], w2_ref[0], _DIMS["nn"])

        @pl.when(j == 0)
        def _():
            acc_ref[...] = part

        @pl.when(j > 0)
        def _():
            acc_ref[...] += part

        @pl.when(j == S - 1)
        def _():
            r_ref[...] = DN_ALPHA * x_ref[...] + 0.5 * acc_ref[...]

    row = pl.BlockSpec((tm, D), lambda i, j: (i, 0))
    return pl.pallas_call(
        body, name=name, out_shape=jax.ShapeDtypeStruct((T, D), F32),
        grid=(T // tm, S),
        in_specs=[pl.BlockSpec((1, tm, Fs), lambda i, j: (j, i, 0)),
                  pl.BlockSpec((1, Fs, D), lambda i, j: (j, 0, 0)), row],
        out_specs=row, scratch_shapes=[pltpu.VMEM((tm, D), F32)],
        compiler_params=_params("parallel", "arbitrary"),
    )(h, w2g, x)


def _ffn_bwd_act(drb, w2g, a, b, *, tm=512, name):
    S, Fs, D = w2g.shape
    T = drb.shape[0]
    tm = _pick(T, tm)

    def body(dr_ref, w2_ref, a_ref, b_ref, da_ref, db_ref):
        d = 0.5 * _dot(dr_ref[...], w2_ref[0], _DIMS["nt"])
        av = a_ref[0]
        sig = _sigmoid(av)
        da_ref[0] = (d * b_ref[0] * sig * (1.0 + av * (1.0 - sig))).astype(BF16)
        db_ref[0] = (d * av * sig).astype(BF16)

    asp = pl.BlockSpec((1, tm, Fs), lambda j, i: (j, i, 0))
    out = jax.ShapeDtypeStruct((S, T, Fs), BF16)
    return pl.pallas_call(
        body, name=name, out_shape=[out, out], grid=(S, T // tm),
        in_specs=[pl.BlockSpec((tm, D), lambda j, i: (i, 0)),
                  pl.BlockSpec((1, Fs, D), lambda j, i: (j, 0, 0)), asp, asp],
        out_specs=[asp, asp],
        compiler_params=_params("parallel", "parallel"),
    )(drb, w2g, a, b)


def _ffn_dw(lhs_t, rhs, *, scale=1.0, tm=1024, tk=1024, name):
    n = len(rhs)
    S, T, Fs = rhs[0].shape
    D = lhs_t.shape[0]
    tm, tk = _pick(D, tm), _pick(T, tk)
    nk = T // tk

    def body(*refs):
        l_ref = refs[0]
        r_refs, o_refs, acc_refs = refs[1:1 + n], refs[1 + n:1 + 2 * n], refs[1 + 2 * n:]
        k = pl.program_id(2)
        lv = l_ref[...]
        for r_ref, o_ref, acc_ref in zip(r_refs, o_refs, acc_refs):
            part = _dot(lv, r_ref[0], _DIMS["nn"])

            @pl.when(k == 0)
            def _():
                acc_ref[...] = part

            @pl.when(k > 0)
            def _():
                acc_ref[...] += part

            @pl.when(k == nk - 1)
            def _():
                o_ref[0] = acc_ref[...] * scale

    osp = pl.BlockSpec((1, tm, Fs), lambda j, i, k: (j, i, 0))
    return pl.pallas_call(
        body, name=name, out_shape=[jax.ShapeDtypeStruct((S, D, Fs), F32)] * n,
        grid=(S, D // tm, nk),
        in_specs=[pl.BlockSpec((tm, tk), lambda j, i, k: (i, k))]
        + [pl.BlockSpec((1, tk, Fs), lambda j, i, k: (j, k, 0))] * n,
        out_specs=[osp] * n, scratch_shapes=[pltpu.VMEM((tm, Fs), F32)] * n,
        compiler_params=_params("parallel", "parallel", "arbitrary"),
    )(lhs_t, *rhs)


def _ffn_dx(da, db, w1g, w3g, dr, *, tm=512, name):
    S, D, Fs = w1g.shape
    T = dr.shape[0]
    tm = _pick(T, tm)

    def body(da_ref, db_ref, w1_ref, w3_ref, dr_ref, o_ref, acc_ref):
        j = pl.program_id(1)
        part = (_dot(da_ref[0], w1_ref[0], _DIMS["nt"]) + _dot(db_ref[0], w3_ref[0], _DIMS["nt"]))

        @pl.when(j == 0)
        def _():
            acc_ref[...] = part

        @pl.when(j > 0)
        def _():
            acc_ref[...] += part

        @pl.when(j == S - 1)
        def _():
            o_ref[...] = DN_ALPHA * dr_ref[...] + acc_ref[...]

    asp = pl.BlockSpec((1, tm, Fs), lambda i, j: (j, i, 0))
    wsp = pl.BlockSpec((1, D, Fs), lambda i, j: (j, 0, 0))
    row = pl.BlockSpec((tm, D), lambda i, j: (i, 0))
    return pl.pallas_call(
        body, name=name, out_shape=jax.ShapeDtypeStruct((T, D), F32),
        grid=(T // tm, S), in_specs=[asp, asp, wsp, wsp, row], out_specs=row,
        scratch_shapes=[pltpu.VMEM((tm, D), F32)],
        compiler_params=_params("parallel", "arbitrary"),
    )(da, db, w1g, w3g, dr)


def _chunk_mask(reverse, transpose=False):
    row = lax.broadcasted_iota(jnp.int32, (CHUNK, CHUNK), 0)
    col = lax.broadcasted_iota(jnp.int32, (CHUNK, CHUNK), 1)
    if reverse != transpose:
        return col >= row
    return col <= row


def _hgrn_pre(hq, hf, lb):
    sig = _sigmoid(hf)
    f = lb + (1.0 - lb) * sig
    q = hq * _sigmoid(hq) * (HGRN_HEAD_DIM ** -0.5)
    return q, f, sig


def _hgrn_decays(f, cmf, reverse):
    bc = _dot01(cmf, jnp.log(f))
    last = 0 if reverse else CHUNK - 1
    blast = bc[last:last + 1, :]
    bref = bc[CHUNK // 2:CHUNK // 2 + 1, :]
    eq = jnp.exp(jnp.minimum(bc - bref, EXP_CLAMP))
    ek = jnp.exp(jnp.minimum(bref - bc, EXP_CLAMP))
    return bc, blast, eq, ek


def _hgrn_fwd(z, lb, *, wh, reverse, name):
    T = z.shape[0]
    nch = T // CHUNK
    nh = wh // HGRN_HEAD_DIM
    hd = HGRN_HEAD_DIM

    def ci(i):
        return nch - 1 - i if reverse else i

    def body(hq_ref, hi_ref, hf_ref, lb_ref, o_ref, st_ref, s_ref):
        @pl.when(pl.program_id(0) == 0)
        def _():
            s_ref[...] = jnp.zeros_like(s_ref)

        cm = _chunk_mask(reverse)
        cmf = cm.astype(F32)
        for h in range(nh):
            sl = slice(h * hd, (h + 1) * hd)
            q, f, _ = _hgrn_pre(hq_ref[:, sl], hf_ref[:, sl], lb_ref[:, sl])
            v = hi_ref[:, sl]
            k = 1.0 - f
            bc, blast, eq, ek = _hgrn_decays(f, cmf, reverse)
            st = s_ref[h]
            st_ref[0, h] = st
            att = jnp.where(cm, _dot_nt(q * eq, k * ek), 0.0)
            o_ref[:, sl] = _dot_nt(q * jnp.exp(bc), st) + _dot_nn(att, v)
            s_ref[h] = jnp.exp(blast) * st + _dot_tn(v, k * jnp.exp(blast - bc))

    blk = lambda c: pl.BlockSpec((CHUNK, wh), lambda i: (ci(i), c))
    return pl.pallas_call(
        body, name=name,
        out_shape=[jax.ShapeDtypeStruct((T, wh), F32),
                   jax.ShapeDtypeStruct((nch, nh, hd, hd), F32)],
        grid=(nch,),
        in_specs=[blk(0), blk(1), blk(3 + int(reverse)),
                  pl.BlockSpec((1, wh), lambda i: (0, 0))],
        out_specs=[blk(0), pl.BlockSpec((1, nh, hd, hd), lambda i: (ci(i), 0, 0, 0))],
        scratch_shapes=[pltpu.VMEM((nh, hd, hd), F32)],
        compiler_params=_params("arbitrary"),
    )(z, z, z, lb)


def _hgrn_bwd(z, lb, do, states, prev, *, wh, reverse, name):
    T = z.shape[0]
    nch = T // CHUNK
    nh = wh // HGRN_HEAD_DIM
    hd = HGRN_HEAD_DIM
    has_prev = prev is not None

    def ci(i):
        return i if reverse else nch - 1 - i

    def body(*refs):
        hq_ref, hi_ref, hf_ref, lb_ref, do_ref, st_ref = refs[:6]
        n_in = 8 if has_prev else 6
        dhq_ref, dhi_ref, dhf_ref, dlb_ref, ds_ref, gs_ref = refs[n_in:]

        @pl.when(pl.program_id(0) == 0)
        def _():
            ds_ref[...] = jnp.zeros_like(ds_ref)
            gs_ref[...] = jnp.zeros_like(gs_ref)
            dlb_ref[...] = jnp.zeros_like(dlb_ref)

        cm = _chunk_mask(reverse)
        cmf = cm.astype(F32)
        cmtf = _chunk_mask(reverse, transpose=True).astype(F32)
        for h in range(nh):
            sl = slice(h * hd, (h + 1) * hd)
            hq = hq_ref[:, sl]
            lb = lb_ref[:, sl]
            q, f, sig = _hgrn_pre(hq, hf_ref[:, sl], lb)
            v = hi_ref[:, sl]
            k = 1.0 - f
            bc, blast, eq, ek = _hgrn_decays(f, cmf, reverse)
            ebc = jnp.exp(bc)
            eb2 = jnp.exp(blast - bc)
            st = st_ref[0, h]
            dst = ds_ref[h]
            dov = do_ref[:, sl]
            qh, kh = q * eq, k * ek
            att = jnp.where(cm, _dot_nt(qh, kh), 0.0)
            datt = jnp.where(cm, _dot_nt(dov, v), 0.0)
            dq = _dot_nn(datt, kh) * eq + ebc * _dot_nn(dov, st)
            dk = _dot_tn(datt, qh) * ek + eb2 * _dot_nn(v, dst)
            dv = _dot_tn(att, dov) + _dot_nt(k * eb2, dst)
            ds_ref[h] = jnp.exp(blast) * dst + _dot_tn(dov, q * ebc)
            db = q * dq - k * dk
            dg = _dot01(cmtf, db) + gs_ref[:, sl]
            gs_ref[:, sl] += jnp.sum(db, axis=0, keepdims=True)
            df = dg / f - dk
            dhf_ref[:, sl] = df * (1.0 - lb) * sig * (1.0 - sig)
            dlb_ref[:, sl] += jnp.sum(df * (1.0 - sig), axis=0, keepdims=True)
            sq = _sigmoid(hq)
            dhq = dq * (HGRN_HEAD_DIM ** -0.5) * sq * (1.0 + hq * (1.0 - sq))
            if has_prev:
                dhq = dhq + refs[6][:, sl]
                dv = dv + refs[7][:, sl]
            dhq_ref[:, sl] = dhq
            dhi_ref[:, sl] = dv

    blk = lambda c: pl.BlockSpec((CHUNK, wh), lambda i: (ci(i), c))
    vec = pl.BlockSpec((1, wh), lambda i: (0, 0))
    in_specs = [blk(0), blk(1), blk(3 + int(reverse)), vec, blk(0),
                pl.BlockSpec((1, nh, hd, hd), lambda i: (ci(i), 0, 0, 0))]
    args = [z, z, z, lb, do, states]
    if has_prev:
        in_specs += [blk(0), blk(0)]
        args += list(prev)
    big = jax.ShapeDtypeStruct((T, wh), F32)
    return pl.pallas_call(
        body, name=name,
        out_shape=[big, big, big, jax.ShapeDtypeStruct((1, wh), F32)],
        grid=(nch,), in_specs=in_specs, out_specs=[blk(0), blk(0), blk(0), vec],
        scratch_shapes=[pltpu.VMEM((nh, hd, hd), F32), pltpu.VMEM((1, wh), F32)],
        compiler_params=_params("arbitrary"),
    )(*args)


def _mlstm_intra(cm, cmt, ig_row, ig_col, xf_row, xf_col, m_st):
    lf_row, lf_col = _log_sigmoid(xf_row), _log_sigmoid(xf_col)
    bcol = jnp.sum(jnp.where(cm, lf_row, 0.0), axis=1, keepdims=True)
    brow = jnp.sum(jnp.where(cmt, lf_col, 0.0), axis=0, keepdims=True)
    blast = jnp.sum(lf_row, axis=1, keepdims=True)
    dmat = jnp.where(cm, bcol - brow + ig_row, NEG)
    m_inter = bcol + m_st
    m_t = jnp.maximum(m_inter, jnp.max(dmat, axis=1, keepdims=True))
    p = jnp.exp(dmat - m_t)
    inter = jnp.exp(m_inter - m_t)
    w_col = blast - bcol + ig_col
    m_new = jnp.maximum(blast + m_st, jnp.max(w_col, axis=0, keepdims=True))
    cs = jnp.exp(blast + m_st - m_new)
    kscale = jnp.exp(w_col - m_new)
    return p, inter, m_t, m_new, cs, kscale


def _mlstm_specs(T, wm, nhm, reverse, backward):
    nch = T // CHUNK
    dm = wm // nhm
    ng = 4 * nhm

    def ci(i):
        fwd_order = nch - 1 - i if reverse else i
        return nch - 1 - fwd_order if backward else fwd_order

    row = lambda w, c: pl.BlockSpec((CHUNK, w), lambda i: (ci(i), c))
    gates_row = pl.BlockSpec((1, ng, CHUNK), lambda i: (ci(i), 0, 0))
    bias_row = pl.BlockSpec((1, LANES), lambda i: (0, 0))
    bias_col = pl.BlockSpec((ng, 1), lambda i: (0, 0))
    st_c = pl.BlockSpec((1, nhm, dm, dm), lambda i: (ci(i), 0, 0, 0))
    st_n = pl.BlockSpec((1, nhm, 1, dm), lambda i: (ci(i), 0, 0, 0))
    st_m = pl.BlockSpec((1, nhm, 1, LANES), lambda i: (ci(i), 0, 0, 0))
    return nch, dm, ng, ci, row, gates_row, bias_row, bias_col, st_c, st_n, st_m


def _mlstm_fwd(qk, z, gc, gr, br, bcl, *, wm, nhm, vcol, reverse, name):
    T = qk.shape[0]
    nch, dm, ng, ci, row, gates_row, bias_row, bias_col, st_c, st_n, st_m = _mlstm_specs(
        T, wm, nhm, reverse, False)
    d = int(reverse)

    def body(q_ref, k_ref, v_ref, gc_ref, gr_ref, br_ref, bc_ref,
             h_ref, cst_ref, nst_ref, mst_ref, c_ref, n_ref, m_ref):
        @pl.when(pl.program_id(0) == 0)
        def _():
            c_ref[...] = jnp.zeros_like(c_ref)
            n_ref[...] = jnp.zeros_like(n_ref)
            m_ref[...] = jnp.full_like(m_ref, M_INIT)

        cm = _chunk_mask(reverse)
        cmt = _chunk_mask(reverse, transpose=True)
        G = gc_ref[...] + br_ref[...]
        Gr = gr_ref[0] + bc_ref[...]
        for h in range(nhm):
            sl = slice(h * dm, (h + 1) * dm)
            ii, fi = d * nhm + h, 2 * nhm + d * nhm + h
            m_all = m_ref[h]
            m_st = m_all[:, 0:1]
            p, inter, m_t, m_new, cs, kscale = _mlstm_intra(
                cm, cmt, Gr[ii:ii + 1, :], G[:, ii:ii + 1], Gr[fi:fi + 1, :], G[:, fi:fi + 1], m_st)
            q = q_ref[:, sl] * (dm ** -0.5)
            k = k_ref[:, sl]
            v = v_ref[:, sl]
            ct = c_ref[h]
            n = n_ref[h]
            cst_ref[0, h] = ct
            nst_ref[0, h] = n
            mst_ref[0, h] = m_all
            sc = _dot_nt(q, k) * p
            num = _dot_nn(sc, v) + inter * _dot_nt(q, ct)
            den = jnp.sum(sc, axis=1, keepdims=True) + inter * jnp.sum(q * n, axis=1, keepdims=True)
            h_ref[:, sl] = num / jnp.maximum(jnp.abs(den), jnp.exp(-m_t))
            kw = k * kscale
            c_ref[h] = cs * ct + _dot_tn(v, kw)
            n_ref[h] = cs * n + jnp.sum(kw, axis=0, keepdims=True)
            m_ref[h] = jnp.broadcast_to(m_new, (1, LANES))

    return pl.pallas_call(
        body, name=name,
        out_shape=[jax.ShapeDtypeStruct((T, wm), F32),
                   jax.ShapeDtypeStruct((nch, nhm, dm, dm), F32),
                   jax.ShapeDtypeStruct((nch, nhm, 1, dm), F32),
                   jax.ShapeDtypeStruct((nch, nhm, 1, LANES), F32)],
        grid=(nch,),
        in_specs=[row(wm, 0), row(wm, 1), row(wm, vcol), row(LANES, 0), gates_row, bias_row, bias_col],
        out_specs=[row(wm, 0), st_c, st_n, st_m],
        scratch_shapes=[pltpu.VMEM((nhm, dm, dm), F32), pltpu.VMEM((nhm, 1, dm), F32),
                        pltpu.VMEM((nhm, 1, LANES), F32)],
        compiler_params=_params("arbitrary"),
    )(qk, qk, z, gc, gr, br, bcl)


def _mlstm_bwd(qk, z, gc, gr, br, bcl, dh, states, prev, *, wm, nhm, vcol, reverse, name):
    T = qk.shape[0]
    nch, dm, ng, ci, row, gates_row, bias_row, bias_col, st_c, st_n, st_m = _mlstm_specs(
        T, wm, nhm, reverse, True)
    d = int(reverse)
    has_prev = prev is not None
    n_in = 13 if has_prev else 11

    def body(*refs):
        (q_ref, k_ref, v_ref, gc_ref, gr_ref, br_ref, bc_ref, dh_ref,
         cst_ref, nst_ref, mst_ref) = refs[:11]
        dqk_ref, dv_ref, dgr_ref, dgs_ref, e_ref, en_ref, fs_ref = refs[n_in:]

        @pl.when(pl.program_id(0) == 0)
        def _():
            e_ref[...] = jnp.zeros_like(e_ref)
            en_ref[...] = jnp.zeros_like(en_ref)
            fs_ref[...] = jnp.zeros_like(fs_ref)
            dgs_ref[...] = jnp.zeros_like(dgs_ref)

        cm = _chunk_mask(reverse)
        cmt = _chunk_mask(reverse, transpose=True)
        row_i = lax.broadcasted_iota(jnp.int32, (CHUNK, CHUNK), 0)
        col_i = lax.broadcasted_iota(jnp.int32, (CHUNK, CHUNK), 1)
        eye = row_i == col_i
        G = gc_ref[...] + br_ref[...]
        Gr = gr_ref[0] + bc_ref[...]
        for h in range(nhm):
            sl = slice(h * dm, (h + 1) * dm)
            slk = slice(wm + h * dm, wm + (h + 1) * dm)
            ii, fi = d * nhm + h, 2 * nhm + d * nhm + h
            m_st = mst_ref[0, h][:, 0:1]
            xf_row = Gr[fi:fi + 1, :]
            p, inter, m_t, m_new, cs, kscale = _mlstm_intra(
                cm, cmt, Gr[ii:ii + 1, :], G[:, ii:ii + 1], xf_row, G[:, fi:fi + 1], m_st)
            q = q_ref[:, sl] * (dm ** -0.5)
            k = k_ref[:, sl]
            v = v_ref[:, sl]
            ct = cst_ref[0, h]
            n = nst_ref[0, h]
            et = e_ref[h]
            en = en_ref[h]
            sc = _dot_nt(q, k) * p
            num = _dot_nn(sc, v) + inter * _dot_nt(q, ct)
            den = jnp.sum(sc, axis=1, keepdims=True) + inter * jnp.sum(q * n, axis=1, keepdims=True)
            floor = jnp.exp(-m_t)
            nstab = jnp.maximum(jnp.abs(den), floor)
            hout = num / nstab
            dhv = dh_ref[:, sl]
            gh = dhv / nstab
            dd = -jnp.sum(dhv * hout, axis=1, keepdims=True) / nstab
            dd = dd * jnp.where(jnp.abs(den) > floor, jnp.sign(den), 0.0)
            dsqk = (_dot_nt(gh, v) + dd) * p
            qi = q * inter
            kw = k * kscale
            dq = _dot_nn(dsqk, k) + inter * (_dot_nn(gh, ct) + dd * n)
            dk = _dot_tn(dsqk, q) + kscale * (_dot_nn(v, et) + en)
            dv = _dot_tn(sc, gh) + _dot_nt(kw, et)
            e_ref[h] = cs * et + _dot_tn(gh, qi)
            en_ref[h] = cs * en + jnp.sum(qi * dd, axis=0, keepdims=True)
            di_col = jnp.sum(k * dk, axis=1, keepdims=True)
            df_col = jnp.sum(q * dq, axis=1, keepdims=True) - di_col
            di_row = jnp.sum(jnp.where(eye, di_col, 0.0), axis=0, keepdims=True)
            carry = fs_ref[h][:, 0:1]
            dlf_row = jnp.sum(jnp.where(cm, df_col, 0.0), axis=0, keepdims=True) + carry
            fs_ref[h] = jnp.broadcast_to(carry + jnp.sum(df_col, axis=0, keepdims=True), (1, LANES))
            dxf_row = dlf_row * (1.0 - _sigmoid(xf_row))
            dgr_ref[0, h:h + 1, :] = di_row
            dgr_ref[0, nhm + h:nhm + h + 1, :] = dxf_row
            dgs_ref[h:h + 1, :] += jnp.broadcast_to(jnp.sum(di_row, axis=1, keepdims=True), (1, LANES))
            dgs_ref[nhm + h:nhm + h + 1, :] += jnp.broadcast_to(
                jnp.sum(dxf_row, axis=1, keepdims=True), (1, LANES))
            dq = dq * (dm ** -0.5)
            if has_prev:
                dq = dq + refs[11][:, sl]
                dk = dk + refs[11][:, slk]
                dv = dv + refs[12][:, sl]
            dqk_ref[:, sl] = dq
            dqk_ref[:, slk] = dk
            dv_ref[:, sl] = dv

    in_specs = [row(wm, 0), row(wm, 1), row(wm, vcol), row(LANES, 0), gates_row, bias_row, bias_col,
                row(wm, 0), st_c, st_n, st_m]
    args = [qk, qk, z, gc, gr, br, bcl, dh] + list(states)
    if has_prev:
        in_specs += [row(2 * wm, 0), row(wm, 0)]
        args += list(prev)
    return pl.pallas_call(
        body, name=name,
        out_shape=[jax.ShapeDtypeStruct((T, 2 * wm), F32), jax.ShapeDtypeStruct((T, wm), F32),
                   jax.ShapeDtypeStruct((nch, 2 * nhm, CHUNK), F32),
                   jax.ShapeDtypeStruct((2 * nhm, LANES), F32)],
        grid=(nch,), in_specs=in_specs,
        out_specs=[row(2 * wm, 0), row(wm, 0),
                   pl.BlockSpec((1, 2 * nhm, CHUNK), lambda i: (ci(i), 0, 0)),
                   pl.BlockSpec((2 * nhm, LANES), lambda i: (0, 0))],
        scratch_shapes=[pltpu.VMEM((nhm, dm, dm), F32), pltpu.VMEM((nhm, 1, dm), F32),
                        pltpu.VMEM((nhm, 1, LANES), F32)],
        compiler_params=_params("arbitrary"),
    )(*args)


def _conv_taps(x, w_ref):
    T = x.shape[0]
    t = lax.broadcasted_iota(jnp.int32, x.shape, 0)
    taps = []
    acc = None
    for j in range(CONV_WIDTH):
        s = CONV_WIDTH // 2 - j
        if s == 0:
            xs = x
        else:
            xs = jnp.where((t - s >= 0) & (t - s < T), pltpu.roll(x, s % T, 0), 0.0)
        taps.append(xs)
        term = w_ref[j:j + 1, :] * xs
        acc = term if acc is None else acc + term
    return taps, acc


def _conv_fwd(z, w, b, *, col0, tc=LANES, name):
    T = z.shape[0]
    C2 = w.shape[1]
    assert col0 % tc == 0 and C2 % tc == 0

    def body(z_ref, w_ref, b_ref, o_ref):
        _, acc = _conv_taps(z_ref[...], w_ref)
        c = acc + b_ref[...]
        o_ref[...] = c * _sigmoid(c)

    return pl.pallas_call(
        body, name=name, out_shape=jax.ShapeDtypeStruct((T, C2), F32),
        grid=(C2 // tc,),
        in_specs=[pl.BlockSpec((T, tc), lambda j: (0, col0 // tc + j)),
                  pl.BlockSpec((CONV_WIDTH, tc), lambda j: (0, j)),
                  pl.BlockSpec((1, tc), lambda j: (0, j))],
        out_specs=pl.BlockSpec((T, tc), lambda j: (0, j)),
        compiler_params=_params("parallel"),
    )(z, w, b)


def _conv_bwd(dy, z, w, b, *, col0, tc=LANES, name):
    T = z.shape[0]
    C2 = w.shape[1]

    def body(dy_ref, z_ref, w_ref, b_ref, dx_ref, dw_ref, db_ref):
        taps, acc = _conv_taps(z_ref[...], w_ref)
        c = acc + b_ref[...]
        sg = _sigmoid(c)
        dc = dy_ref[...] * sg * (1.0 + c * (1.0 - sg))
        t = lax.broadcasted_iota(jnp.int32, dc.shape, 0)
        dx = None
        for j in range(CONV_WIDTH):
            s = j - CONV_WIDTH // 2
            if s == 0:
                ds = dc
            else:
                ds = jnp.where((t - s >= 0) & (t - s < T), pltpu.roll(dc, s % T, 0), 0.0)
            term = w_ref[j:j + 1, :] * ds
            dx = term if dx is None else dx + term
            dw_ref[j:j + 1, :] = jnp.sum(dc * taps[j], axis=0, keepdims=True)
        dx_ref[...] = dx
        db_ref[...] = jnp.sum(dc, axis=0, keepdims=True)

    col = pl.BlockSpec((T, tc), lambda j: (0, j))
    wsp = pl.BlockSpec((CONV_WIDTH, tc), lambda j: (0, j))
    bsp = pl.BlockSpec((1, tc), lambda j: (0, j))
    return pl.pallas_call(
        body, name=name,
        out_shape=[jax.ShapeDtypeStruct((T, C2), F32), jax.ShapeDtypeStruct((CONV_WIDTH, C2), F32),
                   jax.ShapeDtypeStruct((1, C2), F32)],
        grid=(C2 // tc,),
        in_specs=[col, pl.BlockSpec((T, tc), lambda j: (0, col0 // tc + j)), wsp, bsp],
        out_specs=[col, wsp, bsp],
        compiler_params=_params("parallel"),
    )(dy, z, w, b)


def _mix_heads(o_fw, o_bw, h_fw, h_bw, wh, wm, nhm):
    out = []
    hd = HGRN_HEAD_DIM
    for h in range(wh // hd):
        sl = slice(h * hd, (h + 1) * hd)
        o = o_fw[:, sl] + o_bw[:, sl]
        r = lax.rsqrt(jnp.mean(o * o, axis=-1, keepdims=True) + NORM_EPS)
        out.append((0, sl, o * r, r))
    dm = wm // nhm
    for h in range(nhm):
        sl = slice(h * dm, (h + 1) * dm)
        x = h_fw[:, sl] + h_bw[:, sl]
        xc = x - jnp.mean(x, axis=-1, keepdims=True)
        r = lax.rsqrt(jnp.mean(xc * xc, axis=-1, keepdims=True) + NORM_EPS)
        out.append((1, sl, xc * r, r))
    return out


def _mix_specs(T, wh, wm, tm, gcol, ocol):
    rowh = pl.BlockSpec((tm, wh), lambda i: (i, 0))
    rowm = pl.BlockSpec((tm, wm), lambda i: (i, 0))
    hg = pl.BlockSpec((tm, wh), lambda i: (i, gcol))
    mo = pl.BlockSpec((tm, wm), lambda i: (i, ocol))
    vh = pl.BlockSpec((1, wh), lambda i: (0, 0))
    vm = pl.BlockSpec((1, wm), lambda i: (0, 0))
    return rowh, rowm, hg, mo, vh, vm


def _mix_fwd(o_fw, o_bw, h_fw, h_bw, z, gh, gm, *, nhm, gcol, ocol, tm=256, name):
    T, wh = o_fw.shape
    wm = h_fw.shape[1]
    tm = min(tm, T)
    rowh, rowm, hg, mo, vh, vm = _mix_specs(T, wh, wm, tm, gcol, ocol)

    def body(of_ref, ob_ref, hf_ref, hb_ref, hg_ref, mo_ref, gh_ref, gm_ref, y_ref, yt_ref):
        heads = _mix_heads(of_ref[...], ob_ref[...], hf_ref[...], hb_ref[...], wh, wm, nhm)
        for grp, sl, nv, _ in heads:
            if grp == 0:
                gate = hg_ref[:, sl]
                gate = gate * _sigmoid(gate)
                y = nv * gh_ref[:, sl] * gate
                osl = sl
            else:
                y = nv * gm_ref[:, sl] * _sigmoid(mo_ref[:, sl])
                osl = slice(wh + sl.start, wh + sl.stop)
            y_ref[:, osl] = y.astype(BF16)
            yt_ref[osl, :] = y.T.astype(BF16)

    return pl.pallas_call(
        body, name=name,
        out_shape=[jax.ShapeDtypeStruct((T, wh + wm), BF16), jax.ShapeDtypeStruct((wh + wm, T), BF16)],
        grid=(T // tm,),
        in_specs=[rowh, rowh, rowm, rowm, hg, mo, vh, vm],
        out_specs=[pl.BlockSpec((tm, wh + wm), lambda i: (i, 0)),
                   pl.BlockSpec((wh + wm, tm), lambda i: (0, i))],
        compiler_params=_params("parallel"),
    )(o_fw, o_bw, h_fw, h_bw, z, z, gh, gm)


def _mix_bwd(dy, o_fw, o_bw, h_fw, h_bw, z, gh, gm, *, nhm, gcol, ocol, tm=256, name):
    T, wh = o_fw.shape
    wm = h_fw.shape[1]
    tm = min(tm, T)
    rowh, rowm, hg, mo, vh, vm = _mix_specs(T, wh, wm, tm, gcol, ocol)

    def body(dy_ref, of_ref, ob_ref, hf_ref, hb_ref, hg_ref, mo_ref, gh_ref, gm_ref,
             do_ref, dh_ref, dhg_ref, dmo_ref, dgh_ref, dgm_ref):
        @pl.when(pl.program_id(0) == 0)
        def _():
            dgh_ref[...] = jnp.zeros_like(dgh_ref)
            dgm_ref[...] = jnp.zeros_like(dgm_ref)

        heads = _mix_heads(of_ref[...], ob_ref[...], hf_ref[...], hb_ref[...], wh, wm, nhm)
        for grp, sl, nv, r in heads:
            if grp == 0:
                d = dy_ref[:, sl]
                x = hg_ref[:, sl]
                sg = _sigmoid(x)
                gate = x * sg
                g = gh_ref[:, sl]
                dgh_ref[:, sl] += jnp.sum(d * nv * gate, axis=0, keepdims=True)
                dhg_ref[:, sl] = d * nv * g * sg * (1.0 + x * (1.0 - sg))
                t = d * g * gate
                do_ref[:, sl] = r * (t - nv * jnp.mean(t * nv, axis=-1, keepdims=True))
            else:
                d = dy_ref[:, slice(wh + sl.start, wh + sl.stop)]
                sg = _sigmoid(mo_ref[:, sl])
                g = gm_ref[:, sl]
                dgm_ref[:, sl] += jnp.sum(d * nv * sg, axis=0, keepdims=True)
                dmo_ref[:, sl] = d * nv * g * sg * (1.0 - sg)
                t = d * g * sg
                dh_ref[:, sl] = r * (t - jnp.mean(t, axis=-1, keepdims=True)
                                     - nv * jnp.mean(t * nv, axis=-1, keepdims=True))

    return pl.pallas_call(
        body, name=name,
        out_shape=[jax.ShapeDtypeStruct((T, wh), F32), jax.ShapeDtypeStruct((T, wm), F32),
                   jax.ShapeDtypeStruct((T, wh), F32), jax.ShapeDtypeStruct((T, wm), F32),
                   jax.ShapeDtypeStruct((1, wh), F32), jax.ShapeDtypeStruct((1, wm), F32)],
        grid=(T // tm,),
        in_specs=[pl.BlockSpec((tm, wh + wm), lambda i: (i, 0)), rowh, rowh, rowm, rowm, hg, mo, vh, vm],
        out_specs=[rowh, rowm, rowh, rowm, vh, vm],
        compiler_params=_params("arbitrary"),
    )(dy, o_fw, o_bw, h_fw, h_bw, z, z, gh, gm)


def _pick(n, pref):
    for c in range(pref - pref % LANES, 0, -LANES):
        if n % c == 0:
            return c
    return n


def _mm(a, b, mode="nn", **kw):
    if mode == "nn":
        (M, K), N = a.shape, b.shape[1]
    elif mode == "nt":
        (M, K), N = a.shape, b.shape[0]
    else:
        (K, M), N = a.shape, b.shape[1]
    tk = _pick(K, 2048) if mode != "tn" else _pick(K, 512)
    return _matmul(a, b, mode=mode, tm=_pick(M, 1024), tn=_pick(N, 1024), tk=tk, **kw)


def _ffn_fwd(x, xb, w1g, w3g, w2g, tag):
    a, b, h = _ffn_up(xb, w1g, w3g, name=f"{tag}_up")
    r = _ffn_down(h, w2g, x, name=f"{tag}_down")
    return a, b, h, r


def _ffn_bwd(dr, drb, drt, xt, a, b, h, w1g, w3g, w2g, tag):
    da, db = _ffn_bwd_act(drb, w2g, a, b, name=f"{tag}_dact")
    dw1, dw3 = _ffn_dw(xt, [da, db], name=f"{tag}_dw13")
    (dw2t,) = _ffn_dw(drt, [h], scale=0.5, name=f"{tag}_dw2t")
    dx = _ffn_dx(da, db, w1g, w3g, dr, name=f"{tag}_dx")
    return dx, dw1, dw3, dw2t


def _local_step(x, target, W, P):
    T, D = x.shape
    wh = P["hgrn_norm_g"].shape[1]
    wm = P["mlstm_norm_g"].shape[1]
    nhm = P["ig_b"].shape[1]
    ng = 4 * nhm
    nch = T // CHUNK
    assert wh == wm and T % CHUNK == 0
    vcol, ocol, gcol = 7, 8, 2
    col0 = 5 * wh

    xb, xt = _cast_t(x, name="x_cast")
    a1, b1, h1, r1 = _ffn_fwd(x, xb, W["w1a"], W["w3a"], W["w2a"], "ffn1")
    x1, x1b, x1t = _ln_fwd(r1, P["ln1_g"], P["ln1_b"], name="ln1")
    zm = _mm(x1b, W["win"], name="zm")
    zg = _mm(x1b, W["wg"], name="zg")
    gr = zg[:, :ng].reshape(nch, CHUNK, ng).transpose(0, 2, 1)
    bias = jnp.concatenate([P["ig_b"].reshape(-1), P["fg_b"].reshape(-1)])
    br = jnp.zeros((1, LANES), F32).at[0, :ng].set(bias)
    bcl = bias.reshape(ng, 1)
    lbf, lbb = P["lb"][0:1], P["lb"][1:2]
    o_fw, s_fw = _hgrn_fwd(zm, lbf, wh=wh, reverse=False, name="hgrn_fw")
    o_bw, s_bw = _hgrn_fwd(zm, lbb, wh=wh, reverse=True, name="hgrn_bw")
    qk = _conv_fwd(zm, P["conv_w"], P["conv_b"], col0=col0, name="conv")
    h_fw, *st_fw = _mlstm_fwd(qk, zm, zg, gr, br, bcl, wm=wm, nhm=nhm, vcol=vcol, reverse=False,
                              name="mlstm_fw")
    h_bw, *st_bw = _mlstm_fwd(qk, zm, zg, gr, br, bcl, wm=wm, nhm=nhm, vcol=vcol, reverse=True,
                              name="mlstm_bw")
    y, yt = _mix_fwd(o_fw, o_bw, h_fw, h_bw, zm, P["hgrn_norm_g"], P["mlstm_norm_g"],
                     nhm=nhm, gcol=gcol, ocol=ocol, name="mix")
    r2 = _mm(y, W["wout"], add=x1, add_scale=DN_ALPHA, name="r2")
    x2, x2b, x2t = _ln_fwd(r2, P["ln2_g"], P["ln2_b"], name="ln2")
    a2, b2, h2, r3 = _ffn_fwd(x2, x2b, W["w1b"], W["w3b"], W["w2b"], "ffn2")

    G = {}
    dr3, dr3b, dr3t, G["ln3_g"], G["ln3_b"], loss = _ln_bwd(
        None, r3, P["ln3_g"], b=P["ln3_b"], target=target, name="ln3_bwd")
    dx2, G["w1b"], G["w3b"], G["w2bt"] = _ffn_bwd(dr3, dr3b, dr3t, x2t, a2, b2, h2,
                                                  W["w1b"], W["w3b"], W["w2b"], "ffn2")
    dr2, dr2b, _, G["ln2_g"], G["ln2_b"] = _ln_bwd(dx2, r2, P["ln2_g"], name="ln2_bwd")
    dy = _mm(dr2b, W["wout"], "nt", name="dy")
    G["wout"] = _mm(yt, dr2b, name="dwout")
    do, dh, dhg, dmo, G["hgrn_norm_g"], G["mlstm_norm_g"] = _mix_bwd(
        dy, o_fw, o_bw, h_fw, h_bw, zm, P["hgrn_norm_g"], P["mlstm_norm_g"],
        nhm=nhm, gcol=gcol, ocol=ocol, name="mix_bwd")
    dqk_f, dv_f, dgr_f, dgs_f = _mlstm_bwd(qk, zm, zg, gr, br, bcl, dh, st_fw, None, wm=wm, nhm=nhm,
                                           vcol=vcol, reverse=False, name="mlstm_fw_bwd")
    dqk, dv, dgr_b, dgs_b = _mlstm_bwd(qk, zm, zg, gr, br, bcl, dh, st_bw, (dqk_f, dv_f), wm=wm, nhm=nhm,
                                       vcol=vcol, reverse=True, name="mlstm_bw_bwd")
    dmqk, G["conv_w"], G["conv_b"] = _conv_bwd(dqk, zm, P["conv_w"], P["conv_b"], col0=col0,
                                               name="conv_bwd")
    dhq_f, dhi_f, dhf_fw, dlb_f = _hgrn_bwd(zm, lbf, do, s_fw, None, wh=wh, reverse=False,
                                            name="hgrn_fw_bwd")
    dhq, dhi, dhf_bw, dlb_b = _hgrn_bwd(zm, lbb, do, s_bw, (dhq_f, dhi_f), wh=wh, reverse=True,
                                        name="hgrn_bw_bwd")
    G["lb"] = jnp.concatenate([dlb_f, dlb_b], axis=0)
    G["ig_b"] = jnp.stack([dgs_f[:nhm, 0], dgs_b[:nhm, 0]])
    G["fg_b"] = jnp.stack([dgs_f[nhm:, 0], dgs_b[nhm:, 0]])
    dzm = jnp.concatenate([dhq, dhi, dhg, dhf_fw, dhf_bw, dmqk, dv, dmo], axis=1).astype(BF16)
    dgr = jnp.concatenate([dgr_f[:, :nhm], dgr_b[:, :nhm], dgr_f[:, nhm:], dgr_b[:, nhm:]], axis=1)
    dzg = jnp.zeros((T, LANES), F32).at[:, :ng].set(dgr.transpose(0, 2, 1).reshape(T, ng)).astype(BF16)
    G["win"] = _mm(x1t, dzm, name="dwin")
    G["wg"] = _mm(x1t, dzg, name="dwg")
    t = _mm(dzg, W["wg"], "nt", add=dr2, add_scale=DN_ALPHA, name="dx1_g")
    dx1 = _mm(dzm, W["win"], "nt", add=t, name="dx1")
    dr1, dr1b, dr1t, G["ln1_g"], G["ln1_b"] = _ln_bwd(dx1, r1, P["ln1_g"], name="ln1_bwd")
    gx, G["w1a"], G["w3a"], G["w2at"] = _ffn_bwd(dr1, dr1b, dr1t, xt, a1, b1, h1,
                                                 W["w1a"], W["w3a"], W["w2a"], "ffn1")
    return loss, gx, G


ANY = pl.BlockSpec(memory_space=pl.ANY)


def _place():
    x, y, c = lax.axis_index("x"), lax.axis_index("y"), lax.axis_index("c")
    chips = [(1 - x, y), (x, 1 - y), (1 - x, 1 - y)]
    return x, y, c, chips


def _all_gather(parts, name):
    n = len(parts)

    def body(*refs):
        ins, outs = refs[:n], refs[n:2 * n]
        send_sems, recv_sems, local_sems = refs[2 * n:]
        x, y, c, chips = _place()
        me, sibling = (x, y, c), (x, y, 1 - c)

        def copy(a, k, block, to, src=None):
            bx, by, bc = block
            slot = outs[a].at[4 * bx + 2 * by + bc]
            return pltpu.make_async_remote_copy(
                src_ref=slot if src is None else src, dst_ref=slot,
                send_sem=send_sems.at[a, k], recv_sem=recv_sems.at[a, k],
                device_id=to, device_id_type=MESH)

        started = []
        for a in range(n):
            mine = pltpu.make_async_copy(ins[a], outs[a].at[4 * x + 2 * y + c], local_sems.at[a])
            mine.start()
            started.append(mine)
        sends = []
        for a in range(n):
            first = [copy(a, 0, me, sibling, src=ins[a])]
            first += [copy(a, 1 + j, me, (*chip, c), src=ins[a]) for j, chip in enumerate(chips)]
            for cp in first:
                cp.start()
            sends += first
        for a in range(n):
            for j, chip in enumerate(chips):
                copy(a, 1 + j, (*chip, c), me).wait_recv()
                fwd = copy(a, 4 + j, (*chip, c), sibling)
                fwd.start()
                sends.append(fwd)
        for a in range(n):
            copy(a, 0, sibling, me).wait_recv()
            for j, chip in enumerate(chips):
                copy(a, 4 + j, (*chip, 1 - c), me).wait_recv()
        for cp in sends:
            cp.wait_send()
        for mine in started:
            mine.wait()

    return pl.pallas_call(
        body, name=name,
        out_shape=[jax.ShapeDtypeStruct((N_DEV,) + p.shape, p.dtype) for p in parts],
        in_specs=[ANY] * n, out_specs=[ANY] * n,
        scratch_shapes=[pltpu.SemaphoreType.DMA((n, 7)), pltpu.SemaphoreType.DMA((n, 7)),
                        pltpu.SemaphoreType.DMA((n,))],
    )(*parts)


def _pair_exchange(gs, name):
    n = len(gs)

    def body(*refs):
        g_refs, land_refs = refs[:n], refs[n:2 * n]
        send_sems, recv_sems = refs[2 * n:]
        x, y, c, _ = _place()
        cps = []
        for a in range(n):
            for k in range(4):
                cp = pltpu.make_async_remote_copy(
                    src_ref=g_refs[a].at[2 * k + 1 - c], dst_ref=land_refs[a].at[k],
                    send_sem=send_sems.at[a, k], recv_sem=recv_sems.at[a, k],
                    device_id=(x, y, 1 - c), device_id_type=MESH)
                cp.start()
                cps.append(cp)
        for cp in cps:
            cp.wait()

    return pl.pallas_call(
        body, name=name,
        out_shape=[jax.ShapeDtypeStruct((4,) + g.shape[1:], g.dtype) for g in gs],
        in_specs=[ANY] * n, out_specs=[ANY] * n,
        scratch_shapes=[pltpu.SemaphoreType.DMA((n, 4)), pltpu.SemaphoreType.DMA((n, 4))],
    )(*gs)


def _chip_exchange(ps, name):
    n = len(ps)

    def body(*refs):
        p_refs, land_refs = refs[:n], refs[n:2 * n]
        send_sems, recv_sems, local_sems = refs[2 * n:]
        x, y, c, chips = _place()
        mine = 2 * x + y
        owns, cps = [], []
        for a in range(n):
            own = pltpu.make_async_copy(p_refs[a].at[mine], land_refs[a].at[mine], local_sems.at[a])
            own.start()
            owns.append(own)
        for a in range(n):
            for j, chip in enumerate(chips):
                cp = pltpu.make_async_remote_copy(
                    src_ref=p_refs[a].at[2 * chip[0] + chip[1]], dst_ref=land_refs[a].at[mine],
                    send_sem=send_sems.at[a, j], recv_sem=recv_sems.at[a, j],
                    device_id=(*chip, c), device_id_type=MESH)
                cp.start()
                cps.append(cp)
        for a in range(n):
            for j, chip in enumerate(chips):
                pltpu.make_async_remote_copy(
                    src_ref=p_refs[a].at[mine], dst_ref=land_refs[a].at[2 * chip[0] + chip[1]],
                    send_sem=send_sems.at[a, j], recv_sem=recv_sems.at[a, j],
                    device_id=(*chip, c), device_id_type=MESH).wait_recv()
        for cp in cps:
            cp.wait_send()
        for own in owns:
            own.wait()

    return pl.pallas_call(
        body, name=name, out_shape=[jax.ShapeDtypeStruct(p.shape, p.dtype) for p in ps],
        in_specs=[ANY] * n, out_specs=[ANY] * n,
        scratch_shapes=[pltpu.SemaphoreType.DMA((n, 3)), pltpu.SemaphoreType.DMA((n, 3)),
                        pltpu.SemaphoreType.DMA((n,))],
    )(*ps)


def _all_reduce_small(buf, name):
    rows = buf.shape[0]

    def body(b_ref, o_ref, slots, send_sems, recv_sems):
        x, y, c, _ = _place()
        me = 4 * x + 2 * y + c
        slots[me] = b_ref[...]
        cps = []
        for k in range(1, N_DEV):
            fx, fy, fc = (k >> 2) & 1, (k >> 1) & 1, k & 1
            peer = (x ^ fx, y ^ fy, c ^ fc)
            cps.append(pltpu.make_async_remote_copy(
                src_ref=b_ref, dst_ref=slots.at[me],
                send_sem=send_sems.at[k - 1], recv_sem=recv_sems.at[k - 1],
                device_id=peer, device_id_type=MESH))
        for cp in cps:
            cp.start()
        for k in range(1, N_DEV):
            fx, fy, fc = (k >> 2) & 1, (k >> 1) & 1, k & 1
            src = 4 * (x ^ fx) + 2 * (y ^ fy) + (c ^ fc)
            pltpu.make_async_remote_copy(
                src_ref=b_ref, dst_ref=slots.at[src],
                send_sem=send_sems.at[k - 1], recv_sem=recv_sems.at[k - 1],
                device_id=(x ^ fx, y ^ fy, c ^ fc), device_id_type=MESH).wait_recv()
        for cp in cps:
            cp.wait_send()
        acc = slots[0]
        for k in range(1, N_DEV):
            acc = acc + slots[k]
        o_ref[...] = acc

    vm = pl.BlockSpec(memory_space=pltpu.VMEM)
    return pl.pallas_call(
        body, name=name, out_shape=jax.ShapeDtypeStruct(buf.shape, F32),
        in_specs=[vm], out_specs=vm,
        scratch_shapes=[pltpu.VMEM((N_DEV, rows, LANES), F32),
                        pltpu.SemaphoreType.DMA((N_DEV - 1,)), pltpu.SemaphoreType.DMA((N_DEV - 1,))],
    )(buf)


def _row_tile(rows, pref=2048):
    for t in range(min(pref, rows) - min(pref, rows) % 8, 0, -8):
        if rows % t == 0:
            return t
    return rows


def _tile_rows(rows, cols, nbuf):
    budget = VMEM_LIMIT // 2 // (2 * nbuf * 4 * cols)
    return _row_tile(rows, max(8, min(512, budget // 8 * 8)))


def _pair_add(g, land, core, name):
    _, r, c = g.shape
    tr = _tile_rows(r, c, 3)

    def body(core_ref, g_ref, l_ref, o_ref):
        o_ref[...] = g_ref[...] + l_ref[...]

    blk = pl.BlockSpec((1, tr, c), lambda k, i, cr: (k, i, 0))
    return pl.pallas_call(
        body, name=name, out_shape=jax.ShapeDtypeStruct(land.shape, F32),
        grid_spec=pltpu.PrefetchScalarGridSpec(
            num_scalar_prefetch=1, grid=(4, r // tr),
            in_specs=[pl.BlockSpec((1, tr, c), lambda k, i, cr: (2 * k + cr[0], i, 0)), blk],
            out_specs=blk),
        compiler_params=_params("parallel", "parallel"),
    )(core, g, land)


def _sum4(land, name):
    _, r, c = land.shape
    tr = _tile_rows(r, c, 5)

    def body(a_ref, b_ref, c_ref, d_ref, o_ref):
        o_ref[...] = ((a_ref[0] + b_ref[0]) + c_ref[0]) + d_ref[0]

    slab = lambda k: pl.BlockSpec((1, tr, c), lambda i: (k, i, 0))
    return pl.pallas_call(
        body, name=name, out_shape=jax.ShapeDtypeStruct((r, c), F32),
        grid=(r // tr,), in_specs=[slab(0), slab(1), slab(2), slab(3)],
        out_specs=pl.BlockSpec((tr, c), lambda i: (i, 0)),
        compiler_params=_params("parallel"),
    )(land, land, land, land)


def _adamw(w, parts, m, v, name):
    shp = w.shape
    cols = shp[-1]
    w2, m2, v2 = (t.reshape(-1, cols) for t in (w, m, v))
    rows = w2.shape[0]
    n = parts.shape[0]
    assert parts.shape[1:] == (rows, cols), (parts.shape, shp)
    tr = _tile_rows(rows, cols, n + 7)
    c1 = 1.0 / (1.0 - ADAM_B1 ** ADAM_STEP)
    c2 = 1.0 / (1.0 - ADAM_B2 ** ADAM_STEP)

    def body(*refs):
        p_refs = refs[:n]
        w_ref, m_ref, v_ref, g_ref, d_ref, nm_ref, nv_ref = refs[n:]
        gv = p_refs[0][0]
        for p_ref in p_refs[1:]:
            gv = gv + p_ref[0]
        nm = ADAM_B1 * m_ref[...] + (1.0 - ADAM_B1) * gv
        nv = ADAM_B2 * v_ref[...] + (1.0 - ADAM_B2) * (gv * gv)
        g_ref[...] = gv
        nm_ref[...] = nm
        nv_ref[...] = nv
        d_ref[...] = -ADAM_LR * ((nm * c1) / (jnp.sqrt(nv * c2) + ADAM_EPS) + ADAM_WD * w_ref[...])

    blk = pl.BlockSpec((tr, cols), lambda i: (i, 0))
    slab = lambda k: pl.BlockSpec((1, tr, cols), lambda i: (k, i, 0))
    out = jax.ShapeDtypeStruct((rows, cols), F32)
    outs = pl.pallas_call(
        body, name=name, out_shape=[out] * 4, grid=(rows // tr,),
        in_specs=[slab(k) for k in range(n)] + [blk] * 3, out_specs=[blk] * 4,
        compiler_params=_params("parallel"),
    )(*([parts] * n), w2, m2, v2)
    return tuple(o.reshape(shp) for o in outs)


def _pack_rows(arrs):
    rows = []
    for a in arrs:
        flat = a.reshape(-1).astype(F32)
        pad = (-flat.shape[0]) % LANES
        if pad:
            flat = jnp.concatenate([flat, jnp.zeros((pad,), F32)])
        rows.append(flat.reshape(-1, LANES))
    out = jnp.concatenate(rows, axis=0)
    pad = (-out.shape[0]) % 8
    if pad:
        out = jnp.concatenate([out, jnp.zeros((pad, LANES), F32)], axis=0)
    return out


def _unpack_rows(buf, shapes):
    outs, r = [], 0
    for s in shapes:
        n = math.prod(s)
        nr = -(-n // LANES)
        outs.append(buf[r:r + nr].reshape(-1)[:n].reshape(s))
        r += nr
    return outs


BIG = ("ffn1_w1", "ffn1_w3", "ffn1_w2", "w_in", "w_out", "ffn2_w1", "ffn2_w3", "ffn2_w2")
ROW_SHARDED = ("ffn1_w2", "w_out", "ffn2_w2")
SMALL = ("ln1_g", "ln1_b", "hgrn_lb", "hgrn_norm_g", "mlstm_conv_w", "mlstm_conv_b", "mlstm_ig_b",
         "mlstm_fg_b", "mlstm_norm_g", "ln2_g", "ln2_b", "ln3_g", "ln3_b")
WEIGHTS = ("ffn1_w1", "ffn1_w3", "ffn1_w2", "ln1_g", "ln1_b", "w_in", "hgrn_lb", "hgrn_norm_g",
           "mlstm_conv_w", "mlstm_conv_b", "mlstm_ig_b", "mlstm_fg_b", "mlstm_norm_g", "w_out",
           "ln2_g", "ln2_b", "ffn2_w1", "ffn2_w3", "ffn2_w2", "ln3_g", "ln3_b")


def kernel(x, ffn1_w1, ffn1_w3, ffn1_w2, ln1_g, ln1_b, w_in, hgrn_lb, hgrn_norm_g, mlstm_conv_w, mlstm_conv_b, mlstm_ig_b, mlstm_fg_b, mlstm_norm_g, w_out, ln2_g, ln2_b, ffn2_w1, ffn2_w3, ffn2_w2, ln3_g, ln3_b, loss_target, m_ffn1_w1, m_ffn1_w3, m_ffn1_w2, m_ln1_g, m_ln1_b, m_w_in, m_hgrn_lb, m_hgrn_norm_g, m_mlstm_conv_w, m_mlstm_conv_b, m_mlstm_ig_b, m_mlstm_fg_b, m_mlstm_norm_g, m_w_out, m_ln2_g, m_ln2_b, m_ffn2_w1, m_ffn2_w3, m_ffn2_w2, m_ln3_g, m_ln3_b, v_ffn1_w1, v_ffn1_w3, v_ffn1_w2, v_ln1_g, v_ln1_b, v_w_in, v_hgrn_lb, v_hgrn_norm_g, v_mlstm_conv_w, v_mlstm_conv_b, v_mlstm_ig_b, v_mlstm_fg_b, v_mlstm_norm_g, v_w_out, v_ln2_g, v_ln2_b, v_ffn2_w1, v_ffn2_w3, v_ffn2_w2, v_ln3_g, v_ln3_b):
    args = (ffn1_w1, ffn1_w3, ffn1_w2, ln1_g, ln1_b, w_in, hgrn_lb, hgrn_norm_g, mlstm_conv_w,
            mlstm_conv_b, mlstm_ig_b, mlstm_fg_b, mlstm_norm_g, w_out, ln2_g, ln2_b, ffn2_w1, ffn2_w3,
            ffn2_w2, ln3_g, ln3_b)
    ms = (m_ffn1_w1, m_ffn1_w3, m_ffn1_w2, m_ln1_g, m_ln1_b, m_w_in, m_hgrn_lb, m_hgrn_norm_g,
          m_mlstm_conv_w, m_mlstm_conv_b, m_mlstm_ig_b, m_mlstm_fg_b, m_mlstm_norm_g, m_w_out, m_ln2_g,
          m_ln2_b, m_ffn2_w1, m_ffn2_w3, m_ffn2_w2, m_ln3_g, m_ln3_b)
    vs = (v_ffn1_w1, v_ffn1_w3, v_ffn1_w2, v_ln1_g, v_ln1_b, v_w_in, v_hgrn_lb, v_hgrn_norm_g,
          v_mlstm_conv_w, v_mlstm_conv_b, v_mlstm_ig_b, v_mlstm_fg_b, v_mlstm_norm_g, v_w_out, v_ln2_g,
          v_ln2_b, v_ffn2_w1, v_ffn2_w3, v_ffn2_w2, v_ln3_g, v_ln3_b)
    w = dict(zip(WEIGHTS, args))
    m = dict(zip(WEIGHTS, ms))
    v = dict(zip(WEIGHTS, vs))
    core = lax.axis_index("c")
    dev = 4 * lax.axis_index("x") + 2 * lax.axis_index("y") + core
    D = x.shape[-1]
    wh = hgrn_norm_g.shape[-1]
    nhm = mlstm_ig_b.shape[-1]
    ng = 4 * nhm

    small_parts = _pack_rows([hgrn_lb, mlstm_conv_w])
    gathered = _all_gather([w[k][0].astype(BF16) for k in BIG] + [small_parts], "all_gather")
    gw = dict(zip(BIG, gathered))
    gathered_small = gathered[-1]
    lb_sh, cw_sh = zip(*[_unpack_rows(gathered_small[i], [hgrn_lb.shape, mlstm_conv_w.shape])
                         for i in range(N_DEV)])
    hgrn_lb_full = jnp.concatenate(lb_sh, axis=-1)
    conv_w_full = jnp.concatenate(cw_sh, axis=-1)[0]
    win_full = gw["w_in"].transpose(1, 0, 2).reshape(D, -1)
    ncols = win_full.shape[1]
    W = dict(w1a=gw["ffn1_w1"], w3a=gw["ffn1_w3"], w2a=gw["ffn1_w2"],
             win=win_full[:, :ncols - ng],
             wg=jnp.zeros((D, LANES), BF16).at[:, :ng].set(win_full[:, ncols - ng:]),
             wout=gw["w_out"].reshape(D, D),
             w1b=gw["ffn2_w1"], w3b=gw["ffn2_w3"], w2b=gw["ffn2_w2"])
    p_lb = jax.nn.softmax(hgrn_lb_full, axis=1)
    P = dict(ln1_g=ln1_g, ln1_b=ln1_b, ln2_g=ln2_g, ln2_b=ln2_b, ln3_g=ln3_g, ln3_b=ln3_b,
             lb=p_lb[:, 0], hgrn_norm_g=hgrn_norm_g, conv_w=conv_w_full, conv_b=mlstm_conv_b,
             ig_b=mlstm_ig_b[0], fg_b=mlstm_fg_b[0], mlstm_norm_g=mlstm_norm_g)

    loss_row, grad_x, G = _local_step(x[0], loss_target[0], W, P)

    dwin = jnp.concatenate([G["win"], G["wg"][:, :ng]], axis=1)
    gsh = {"ffn1_w1": G["w1a"], "ffn1_w3": G["w3a"], "ffn1_w2": G["w2at"],
           "w_in": dwin.reshape(D, N_DEV, -1).transpose(1, 0, 2),
           "w_out": G["wout"].reshape(N_DEV, -1, D),
           "ffn2_w1": G["w1b"], "ffn2_w3": G["w3b"], "ffn2_w2": G["w2bt"]}
    glist = [gsh[k] for k in BIG]
    core_arr = jnp.reshape(core, (1,)).astype(jnp.int32)
    land1 = _pair_exchange(glist, "rs_pair")
    parts = [_pair_add(g, l, core_arr, "rs_pair_add_" + k) for k, g, l in zip(BIG, glist, land1)]
    land2 = dict(zip(BIG, _chip_exchange(parts, "rs_chips")))

    dlb = G["lb"]
    g_lb = jnp.stack([dlb * p_lb[:, 0] * (1.0 - p_lb[:, 0]), -dlb * p_lb[:, 0] * p_lb[:, 1]], axis=1)
    small_full = {"ln1_g": G["ln1_g"], "ln1_b": G["ln1_b"], "hgrn_lb": g_lb, "hgrn_norm_g": G["hgrn_norm_g"],
                  "mlstm_conv_w": G["conv_w"][None], "mlstm_conv_b": G["conv_b"],
                  "mlstm_ig_b": G["ig_b"][None], "mlstm_fg_b": G["fg_b"][None],
                  "mlstm_norm_g": G["mlstm_norm_g"], "ln2_g": G["ln2_g"], "ln2_b": G["ln2_b"],
                  "ln3_g": G["ln3_g"], "ln3_b": G["ln3_b"]}
    small_list = [small_full[k] for k in SMALL] + [loss_row]
    reduced = _all_reduce_small(_pack_rows(small_list), "all_reduce_small")
    red = _unpack_rows(reduced, [a.shape for a in small_list])
    loss = red[-1][0, 0]
    small_g = {}
    for k, gk in zip(SMALL, red[:-1]):
        if k in ("hgrn_lb", "mlstm_conv_w"):
            n = w[k].shape[-1]
            gk = lax.dynamic_slice_in_dim(gk, dev * n, n, axis=gk.ndim - 1)
        small_g[k] = gk

    grads, delta, new_m, new_v = {}, {}, {}, {}
    for k in BIG:
        contrib = land2[k]
        if k in ROW_SHARDED and k != "w_out":
            contrib = _sum4(contrib, "rs_sum_" + k).T[None]
        grads[k], delta[k], new_m[k], new_v[k] = _adamw(w[k], contrib, m[k], v[k], "adamw_" + k)
    sm = _adamw(_pack_rows([w[k] for k in SMALL]), _pack_rows([small_g[k] for k in SMALL])[None],
                _pack_rows([m[k] for k in SMALL]), _pack_rows([v[k] for k in SMALL]), "adamw_small")
    shapes = [w[k].shape for k in SMALL]
    for dst, buf in zip((grads, delta, new_m, new_v), sm):
        for k, val in zip(SMALL, _unpack_rows(buf, shapes)):
            dst[k] = val
    return (loss, grad_x[None], *[grads[k] for k in WEIGHTS], *[delta[k] for k in WEIGHTS],
            *[new_m[k] for k in WEIGHTS], *[new_v[k] for k in WEIGHTS])
```

```python
import math

import jax
import jax.numpy as jnp
from jax import lax
from jax.experimental import pallas as pl
from jax.experimental.pallas import tpu as pltpu

F32 = jnp.float32
BF16 = jnp.bfloat16

CHUNK = 64
HGRN_HEAD_DIM = 128
CONV_WIDTH = 5
DN_ALPHA = 2.0 ** 0.25
LN_EPS = 1e-5
NORM_EPS = 1e-6
M_INIT = -1e30
NEG = -1e30
EXP_CLAMP = 80.0
ADAM_LR = 0.001
ADAM_B1 = 0.9
ADAM_B2 = 0.999
ADAM_EPS = 1e-08
ADAM_WD = 0.01
ADAM_STEP = 10
N_DEV = 8
LANES = 128
VMEM_LIMIT = 56 * 1024 * 1024
MESH = pl.DeviceIdType.MESH


def _params(*sem):
    return pltpu.CompilerParams(dimension_semantics=sem, vmem_limit_bytes=VMEM_LIMIT)


class _Job:
    def __init__(self, inputs, out_shape, sems, start, finish):
        self.inputs, self.out_shape, self.sems = list(inputs), list(out_shape), list(sems)
        self.start, self.finish = start, finish


def _join(*jobs):
    def split(refs, counts):
        out, p = [], 0
        for c in counts:
            out.append(refs[p:p + c])
            p += c
        return out

    n_in = [len(j.inputs) for j in jobs]
    n_out = [len(j.out_shape) for j in jobs]
    n_sem = [len(j.sems) for j in jobs]

    def start(ins, outs, sems):
        for j, i, o, s in zip(jobs, split(ins, n_in), split(outs, n_out), split(sems, n_sem)):
            j.start(i, o, s)

    def finish(ins, outs, sems):
        for j, i, o, s in zip(jobs, split(ins, n_in), split(outs, n_out), split(sems, n_sem)):
            j.finish(i, o, s)

    return _Job(sum((j.inputs for j in jobs), []), sum((j.out_shape for j in jobs), []),
                sum((j.sems for j in jobs), []), start, finish)


def _pcall(body, *, name, out_shape, grid, in_specs, out_specs, sem, scratch_shapes=(), comm=None):
    single = not isinstance(out_shape, (list, tuple))
    out_shape = [out_shape] if single else list(out_shape)
    out_specs = [out_specs] if single else list(out_specs)
    in_specs, scratch_shapes = list(in_specs), list(scratch_shapes)
    if comm is None:
        call = pl.pallas_call(body, name=name, out_shape=out_shape, grid=grid, in_specs=in_specs,
                              out_specs=out_specs, scratch_shapes=scratch_shapes,
                              compiler_params=_params(*sem))

        def run(*args):
            outs = call(*args)
            return outs[0] if single else outs
        return run

    n_in, n_out, n_sc = len(in_specs), len(out_shape), len(scratch_shapes)
    c_in, c_out = len(comm.inputs), len(comm.out_shape)

    def hosted(*refs):
        ins, cins = refs[:n_in], refs[n_in:n_in + c_in]
        p = n_in + c_in
        outs, couts = refs[p:p + n_out], refs[p + n_out:p + n_out + c_out]
        p += n_out + c_out
        scratch, csems = refs[p:p + n_sc], refs[p + n_sc:]
        first = pl.program_id(0) == 0
        last = pl.program_id(0) == grid[0] - 1
        for d in range(1, len(grid)):
            first = first & (pl.program_id(d) == 0)
            last = last & (pl.program_id(d) == grid[d] - 1)

        @pl.when(first)
        def _():
            comm.start(cins, couts, csems)

        body(*ins, *outs, *scratch)

        @pl.when(last)
        def _():
            comm.finish(cins, couts, csems)

    any_spec = pl.BlockSpec(memory_space=pl.ANY)
    call = pl.pallas_call(
        hosted, name=name, out_shape=out_shape + comm.out_shape, grid=grid,
        in_specs=in_specs + [any_spec] * c_in, out_specs=out_specs + [any_spec] * c_out,
        scratch_shapes=scratch_shapes + comm.sems,
        compiler_params=_params(*(["arbitrary"] * len(grid))))

    def run(*args):
        res = call(*args, *comm.inputs)
        outs, couts = res[:n_out], list(res[n_out:])
        return (outs[0] if single else outs), couts
    return run


def _sigmoid(x):
    return 1.0 / (1.0 + jnp.exp(-x))


def _log_sigmoid(x):
    return jnp.minimum(x, 0.0) - jnp.log(1.0 + jnp.exp(-jnp.abs(x)))


def _dot(a, b, dims):
    return lax.dot_general(a.astype(BF16), b.astype(BF16), (dims, ((), ())),
                           preferred_element_type=F32)


def _dot3(a, b, dims):
    ah = a.astype(BF16)
    al = (a - ah.astype(F32)).astype(BF16)
    bh = b.astype(BF16)
    bl = (b - bh.astype(F32)).astype(BF16)
    d = (dims, ((), ()))
    out = lax.dot_general(ah, bh, d, preferred_element_type=F32)
    out = out + lax.dot_general(ah, bl, d, preferred_element_type=F32)
    return out + lax.dot_general(al, bh, d, preferred_element_type=F32)


_DIMS = {"nn": ((1,), (0,)), "nt": ((1,), (1,)), "tn": ((0,), (0,))}


def _dot_nn(a, b):
    return _dot3(a, b, _DIMS["nn"])


def _dot_nt(a, b):
    return _dot3(a, b, _DIMS["nt"])


def _dot_tn(a, b):
    return _dot3(a, b, _DIMS["tn"])


def _split3(x):
    hi = x.astype(BF16)
    r1 = x - hi.astype(F32)
    mid = r1.astype(BF16)
    lo = (r1 - mid.astype(F32)).astype(BF16)
    return hi, mid, lo


def _dot01(mask01, x, mode="nn"):
    m = mask01.astype(BF16)
    hi, mid, lo = _split3(x)
    d = (_DIMS[mode], ((), ()))
    out = lax.dot_general(m, hi, d, preferred_element_type=F32)
    out = out + lax.dot_general(m, mid, d, preferred_element_type=F32)
    return out + lax.dot_general(m, lo, d, preferred_element_type=F32)


def _matmul(a, b, *, mode="nn", out_dtype=F32, tm=512, tn=512, tk=None,
            add=None, scale=1.0, add_scale=1.0, name, comm=None):
    if mode == "nn":
        (M, K), (K2, N) = a.shape, b.shape
    elif mode == "nt":
        (M, K), (N, K2) = a.shape, b.shape
    else:
        (K, M), (K2, N) = a.shape, b.shape
    assert K == K2, (a.shape, b.shape, mode)
    tm, tn = min(tm, M), min(tn, N)
    tk = min(tk or K, K)
    assert M % tm == 0 and N % tn == 0 and K % tk == 0, (M, N, K, tm, tn, tk)
    nk = K // tk
    dims = _DIMS[mode]
    has_add = add is not None

    def body(*refs):
        if has_add:
            a_ref, b_ref, add_ref, o_ref = refs[:4]
        else:
            a_ref, b_ref, o_ref = refs[:3]
            add_ref = None
        acc_ref = refs[-1] if nk > 1 else None

        def finish(acc):
            out = acc if scale == 1.0 else acc * scale
            if has_add:
                out = out + add_ref[...].astype(F32) * add_scale
            o_ref[...] = out.astype(o_ref.dtype)

        part = _dot(a_ref[...], b_ref[...], dims)
        if nk == 1:
            finish(part)
        else:
            k = pl.program_id(2)

            @pl.when(k == 0)
            def _():
                acc_ref[...] = part

            @pl.when(k > 0)
            def _():
                acc_ref[...] += part

            @pl.when(k == nk - 1)
            def _():
                finish(acc_ref[...])

    if mode == "tn":
        a_spec = pl.BlockSpec((tk, tm), lambda i, j, k: (k, i))
    else:
        a_spec = pl.BlockSpec((tm, tk), lambda i, j, k: (i, k))
    if mode == "nt":
        b_spec = pl.BlockSpec((tn, tk), lambda i, j, k: (j, k))
    else:
        b_spec = pl.BlockSpec((tk, tn), lambda i, j, k: (k, j))
    o_spec = pl.BlockSpec((tm, tn), lambda i, j, k: (i, j))
    in_specs = [a_spec, b_spec] + ([o_spec] if has_add else [])
    args = (a, b) + ((add,) if has_add else ())
    return _pcall(
        body, name=name, comm=comm,
        out_shape=jax.ShapeDtypeStruct((M, N), out_dtype),
        grid=(M // tm, N // tn, nk),
        in_specs=in_specs, out_specs=o_spec,
        scratch_shapes=[pltpu.VMEM((tm, tn), F32)] if nk > 1 else [],
        sem=("parallel", "parallel", "arbitrary"),
    )(*args)


def _ln_stats(x):
    mu = jnp.mean(x, axis=-1, keepdims=True)
    xc = x - mu
    var = jnp.mean(xc * xc, axis=-1, keepdims=True)
    rstd = lax.rsqrt(var + LN_EPS)
    return xc * rstd, rstd


def _ln_fwd(r, g, b, *, tm=256, name):
    T, D = r.shape
    tm = min(tm, T)

    def body(r_ref, g_ref, b_ref, y_ref, yb_ref, yt_ref):
        xhat, _ = _ln_stats(r_ref[...])
        y = xhat * g_ref[...] + b_ref[...]
        y_ref[...] = y
        yb_ref[...] = y.astype(BF16)
        yt_ref[...] = y.T.astype(BF16)

    row = pl.BlockSpec((tm, D), lambda i: (i, 0))
    vec = pl.BlockSpec((1, D), lambda i: (0, 0))
    return pl.pallas_call(
        body, name=name,
        out_shape=[jax.ShapeDtypeStruct((T, D), F32), jax.ShapeDtypeStruct((T, D), BF16),
                   jax.ShapeDtypeStruct((D, T), BF16)],
        grid=(T // tm,), in_specs=[row, vec, vec],
        out_specs=[row, row, pl.BlockSpec((D, tm), lambda i: (0, i))],
        compiler_params=_params("parallel"),
    )(r, g, b)


def _cast_t(x, *, tm=256, name):
    T, D = x.shape
    tm = min(tm, T)

    def body(x_ref, xb_ref, xt_ref):
        xv = x_ref[...]
        xb_ref[...] = xv.astype(BF16)
        xt_ref[...] = xv.T.astype(BF16)

    row = pl.BlockSpec((tm, D), lambda i: (i, 0))
    return pl.pallas_call(
        body, name=name,
        out_shape=[jax.ShapeDtypeStruct((T, D), BF16), jax.ShapeDtypeStruct((D, T), BF16)],
        grid=(T // tm,), in_specs=[row], out_specs=[row, pl.BlockSpec((D, tm), lambda i: (0, i))],
        compiler_params=_params("parallel"),
    )(x)


def _ln_bwd(dy, r, g, *, tm=256, name, b=None, target=None):
    T, D = r.shape
    tm = min(tm, T)
    with_loss = target is not None

    def body(*refs):
        if with_loss:
            r_ref, g_ref, b_ref, t_ref, dr_ref, drb_ref, drt_ref, dg_ref, db_ref, loss_ref = refs
        else:
            dy_ref, r_ref, g_ref, dr_ref, drb_ref, drt_ref, dg_ref, db_ref = refs
        i = pl.program_id(0)
        xhat, rstd = _ln_stats(r_ref[...])
        gg = g_ref[...]
        if with_loss:
            err = xhat * gg + b_ref[...] - t_ref[...]
            dyv = err * (1.0 / D)
            part = jnp.sum(jnp.sum(err * err, axis=1, keepdims=True), axis=0, keepdims=True)
            part = jnp.broadcast_to(part * (0.5 / D), (1, LANES))
        else:
            dyv = dy_ref[...]
        dxh = dyv * gg
        m1 = jnp.mean(dxh, axis=-1, keepdims=True)
        m2 = jnp.mean(dxh * xhat, axis=-1, keepdims=True)
        dr = rstd * (dxh - m1 - xhat * m2)
        dr_ref[...] = dr
        drb_ref[...] = dr.astype(BF16)
        drt_ref[...] = dr.T.astype(BF16)
        dgp = jnp.sum(dyv * xhat, axis=0, keepdims=True)
        dbp = jnp.sum(dyv, axis=0, keepdims=True)

        @pl.when(i == 0)
        def _():
            dg_ref[...] = dgp
            db_ref[...] = dbp
            if with_loss:
                loss_ref[...] = part

        @pl.when(i > 0)
        def _():
            dg_ref[...] += dgp
            db_ref[...] += dbp
            if with_loss:
                loss_ref[...] += part

    row = pl.BlockSpec((tm, D), lambda i: (i, 0))
    vec = pl.BlockSpec((1, D), lambda i: (0, 0))
    out_shape = [jax.ShapeDtypeStruct((T, D), F32), jax.ShapeDtypeStruct((T, D), BF16),
                 jax.ShapeDtypeStruct((D, T), BF16), jax.ShapeDtypeStruct((1, D), F32),
                 jax.ShapeDtypeStruct((1, D), F32)]
    out_specs = [row, row, pl.BlockSpec((D, tm), lambda i: (0, i)), vec, vec]
    if with_loss:
        in_specs, args = [row, vec, vec, row], (r, g, b, target)
        out_shape.append(jax.ShapeDtypeStruct((1, LANES), F32))
        out_specs.append(pl.BlockSpec((1, LANES), lambda i: (0, 0)))
    else:
        in_specs, args = [row, row, vec], (dy, r, g)
    return pl.pallas_call(
        body, name=name, out_shape=out_shape, grid=(T // tm,),
        in_specs=in_specs, out_specs=out_specs,
        compiler_params=_params("arbitrary"),
    )(*args)


def _ffn_up(xb, w1g, w3g, *, tm=512, name, comm=None):
    S, D, Fs = w1g.shape
    T = xb.shape[0]
    tm = _pick(T, tm)

    def body(x_ref, w1_ref, w3_ref, a_ref, b_ref, h_ref):
        xv = x_ref[...]
        a = _dot(xv, w1_ref[0], _DIMS["nn"])
        b = _dot(xv, w3_ref[0], _DIMS["nn"])
        a_ref[0] = a
        b_ref[0] = b
        h_ref[0] = (a * _sigmoid(a) * b).astype(BF16)

    wsp = pl.BlockSpec((1, D, Fs), lambda j, i: (j, 0, 0))
    osp = pl.BlockSpec((1, tm, Fs), lambda j, i: (j, i, 0))
    act = jax.ShapeDtypeStruct((S, T, Fs), F32)
    return _pcall(
        body, name=name, comm=comm, out_shape=[act, act, jax.ShapeDtypeStruct((S, T, Fs), BF16)],
        grid=(S, T // tm),
        in_specs=[pl.BlockSpec((tm, D), lambda j, i: (i, 0)), wsp, wsp],
        out_specs=[osp, osp, osp],
        sem=("parallel", "parallel"),
    )(xb, w1g, w3g)


def _ffn_down(h, w2g, x, *, tm=512, name, comm=None):
    S, Fs, D = w2g.shape
    T = x.shape[0]
    tm = _pick(T, tm)

    def body(h_ref, w2_ref, x_ref, r_ref, acc_ref):
        j = pl.program_id(1)
        part = _dot(h_ref[0], w2_ref[0], _DIMS["nn"])

        @pl.when(j == 0)
        def _():
            acc_ref[...] = part

        @pl.when(j > 0)
        def _():
            acc_ref[...] += part

        @pl.when(j == S - 1)
        def _():
            r_ref[...] = DN_ALPHA * x_ref[...] + 0.5 * acc_ref[...]

    row = pl.BlockSpec((tm, D), lambda i, j: (i, 0))
    return _pcall(
        body, name=name, comm=comm, out_shape=jax.ShapeDtypeStruct((T, D), F32),
        grid=(T // tm, S),
        in_specs=[pl.BlockSpec((1, tm, Fs), lambda i, j: (j, i, 0)),
                  pl.BlockSpec((1, Fs, D), lambda i, j: (j, 0, 0)), row],
        out_specs=row, scratch_shapes=[pltpu.VMEM((tm, D), F32)],
        sem=("parallel", "arbitrary"),
    )(h, w2g, x)


def _ffn_bwd_act(drb, w2g, a, b, *, tm=512, name, comm=None):
    S, Fs, D = w2g.shape
    T = drb.shape[0]
    tm = _pick(T, tm)

    def body(dr_ref, w2_ref, a_ref, b_ref, da_ref, db_ref):
        d = 0.5 * _dot(dr_ref[...], w2_ref[0], _DIMS["nt"])
        av = a_ref[0]
        sig = _sigmoid(av)
        da_ref[0] = (d * b_ref[0] * sig * (1.0 + av * (1.0 - sig))).astype(BF16)
        db_ref[0] = (d * av * sig).astype(BF16)

    asp = pl.BlockSpec((1, tm, Fs), lambda j, i: (j, i, 0))
    out = jax.ShapeDtypeStruct((S, T, Fs), BF16)
    return _pcall(
        body, name=name, comm=comm, out_shape=[out, out], grid=(S, T // tm),
        in_specs=[pl.BlockSpec((tm, D), lambda j, i: (i, 0)),
                  pl.BlockSpec((1, Fs, D), lambda j, i: (j, 0, 0)), asp, asp],
        out_specs=[asp, asp],
        sem=("parallel", "parallel"),
    )(drb, w2g, a, b)


def _ffn_dw(lhs_t, rhs, *, scale=1.0, tm=1024, tk=1024, name, comm=None):
    n = len(rhs)
    S, T, Fs = rhs[0].shape
    D = lhs_t.shape[0]
    tm, tk = _pick(D, tm), _pick(T, tk)
    nk = T // tk

    def body(*refs):
        l_ref = refs[0]
        r_refs, o_refs, acc_refs = refs[1:1 + n], refs[1 + n:1 + 2 * n], refs[1 + 2 * n:]
        k = pl.program_id(2)
        lv = l_ref[...]
        for r_ref, o_ref, acc_ref in zip(r_refs, o_refs, acc_refs):
            part = _dot(lv, r_ref[0], _DIMS["nn"])

            @pl.when(k == 0)
            def _():
                acc_ref[...] = part

            @pl.when(k > 0)
            def _():
                acc_ref[...] += part

            @pl.when(k == nk - 1)
            def _():
                o_ref[0] = (acc_ref[...] * scale).astype(o_ref.dtype)

    osp = pl.BlockSpec((1, tm, Fs), lambda j, i, k: (j, i, 0))
    return _pcall(
        body, name=name, comm=comm, out_shape=[jax.ShapeDtypeStruct((S, D, Fs), BF16)] * n,
        grid=(S, D // tm, nk),
        in_specs=[pl.BlockSpec((tm, tk), lambda j, i, k: (i, k))]
        + [pl.BlockSpec((1, tk, Fs), lambda j, i, k: (j, k, 0))] * n,
        out_specs=[osp] * n, scratch_shapes=[pltpu.VMEM((tm, Fs), F32)] * n,
        sem=("parallel", "parallel", "arbitrary"),
    )(lhs_t, *rhs)


def _ffn_dx(da, db, w1g, w3g, dr, *, tm=512, name, comm=None):
    S, D, Fs = w1g.shape
    T = dr.shape[0]
    tm = _pick(T, tm)

    def body(da_ref, db_ref, w1_ref, w3_ref, dr_ref, o_ref, acc_ref):
        j = pl.program_id(1)
        part = (_dot(da_ref[0], w1_ref[0], _DIMS["nt"]) + _dot(db_ref[0], w3_ref[0], _DIMS["nt"]))

        @pl.when(j == 0)
        def _():
            acc_ref[...] = part

        @pl.when(j > 0)
        def _():
            acc_ref[...] += part

        @pl.when(j == S - 1)
        def _():
            o_ref[...] = DN_ALPHA * dr_ref[...] + acc_ref[...]

    asp = pl.BlockSpec((1, tm, Fs), lambda i, j: (j, i, 0))
    wsp = pl.BlockSpec((1, D, Fs), lambda i, j: (j, 0, 0))
    row = pl.BlockSpec((tm, D), lambda i, j: (i, 0))
    return _pcall(
        body, name=name, comm=comm, out_shape=jax.ShapeDtypeStruct((T, D), F32),
        grid=(T // tm, S), in_specs=[asp, asp, wsp, wsp, row], out_specs=row,
        scratch_shapes=[pltpu.VMEM((tm, D), F32)],
        sem=("parallel", "arbitrary"),
    )(da, db, w1g, w3g, dr)


def _chunk_mask(reverse, transpose=False):
    row = lax.broadcasted_iota(jnp.int32, (CHUNK, CHUNK), 0)
    col = lax.broadcasted_iota(jnp.int32, (CHUNK, CHUNK), 1)
    if reverse != transpose:
        return col >= row
    return col <= row


def _hgrn_pre(hq, hf, lb):
    sig = _sigmoid(hf)
    f = lb + (1.0 - lb) * sig
    q = hq * _sigmoid(hq) * (HGRN_HEAD_DIM ** -0.5)
    return q, f, sig


def _hgrn_decays(f, cmf, reverse):
    bc = _dot01(cmf, jnp.log(f))
    last = 0 if reverse else CHUNK - 1
    blast = bc[last:last + 1, :]
    bref = bc[CHUNK // 2:CHUNK // 2 + 1, :]
    eq = jnp.exp(jnp.minimum(bc - bref, EXP_CLAMP))
    ek = jnp.exp(jnp.minimum(bref - bc, EXP_CLAMP))
    return bc, blast, eq, ek


def _hgrn_fwd(z, lb, *, wh, reverse, name, comm=None):
    T = z.shape[0]
    nch = T // CHUNK
    nh = wh // HGRN_HEAD_DIM
    hd = HGRN_HEAD_DIM

    def ci(i):
        return nch - 1 - i if reverse else i

    def body(hq_ref, hi_ref, hf_ref, lb_ref, o_ref, st_ref, s_ref):
        @pl.when(pl.program_id(0) == 0)
        def _():
            s_ref[...] = jnp.zeros_like(s_ref)

        cm = _chunk_mask(reverse)
        cmf = cm.astype(F32)
        for h in range(nh):
            sl = slice(h * hd, (h + 1) * hd)
            q, f, _ = _hgrn_pre(hq_ref[:, sl], hf_ref[:, sl], lb_ref[:, sl])
            v = hi_ref[:, sl]
            k = 1.0 - f
            bc, blast, eq, ek = _hgrn_decays(f, cmf, reverse)
            st = s_ref[h]
            st_ref[0, h] = st
            att = jnp.where(cm, _dot_nt(q * eq, k * ek), 0.0)
            o_ref[:, sl] = _dot_nt(q * jnp.exp(bc), st) + _dot_nn(att, v)
            s_ref[h] = jnp.exp(blast) * st + _dot_tn(v, k * jnp.exp(blast - bc))

    blk = lambda c: pl.BlockSpec((CHUNK, wh), lambda i: (ci(i), c))
    return _pcall(
        body, name=name, comm=comm,
        out_shape=[jax.ShapeDtypeStruct((T, wh), F32),
                   jax.ShapeDtypeStruct((nch, nh, hd, hd), F32)],
        grid=(nch,),
        in_specs=[blk(0), blk(1), blk(3 + int(reverse)),
                  pl.BlockSpec((1, wh), lambda i: (0, 0))],
        out_specs=[blk(0), pl.BlockSpec((1, nh, hd, hd), lambda i: (ci(i), 0, 0, 0))],
        scratch_shapes=[pltpu.VMEM((nh, hd, hd), F32)],
        sem=("arbitrary",),
    )(z, z, z, lb)


def _hgrn_bwd(z, lb, do, states, prev, *, wh, reverse, name):
    T = z.shape[0]
    nch = T // CHUNK
    nh = wh // HGRN_HEAD_DIM
    hd = HGRN_HEAD_DIM
    has_prev = prev is not None

    def ci(i):
        return i if reverse else nch - 1 - i

    def body(*refs):
        hq_ref, hi_ref, hf_ref, lb_ref, do_ref, st_ref = refs[:6]
        n_in = 8 if has_prev else 6
        dhq_ref, dhi_ref, dhf_ref, dlb_ref, ds_ref, gs_ref = refs[n_in:]

        @pl.when(pl.program_id(0) == 0)
        def _():
            ds_ref[...] = jnp.zeros_like(ds_ref)
            gs_ref[...] = jnp.zeros_like(gs_ref)
            dlb_ref[...] = jnp.zeros_like(dlb_ref)

        cm = _chunk_mask(reverse)
        cmf = cm.astype(F32)
        cmtf = _chunk_mask(reverse, transpose=True).astype(F32)
        for h in range(nh):
            sl = slice(h * hd, (h + 1) * hd)
            hq = hq_ref[:, sl]
            lb = lb_ref[:, sl]
            q, f, sig = _hgrn_pre(hq, hf_ref[:, sl], lb)
            v = hi_ref[:, sl]
            k = 1.0 - f
            bc, blast, eq, ek = _hgrn_decays(f, cmf, reverse)
            ebc = jnp.exp(bc)
            eb2 = jnp.exp(blast - bc)
            st = st_ref[0, h]
            dst = ds_ref[h]
            dov = do_ref[:, sl]
            qh, kh = q * eq, k * ek
            att = jnp.where(cm, _dot_nt(qh, kh), 0.0)
            datt = jnp.where(cm, _dot_nt(dov, v), 0.0)
            dq = _dot_nn(datt, kh) * eq + ebc * _dot_nn(dov, st)
            dk = _dot_tn(datt, qh) * ek + eb2 * _dot_nn(v, dst)
            dv = _dot_tn(att, dov) + _dot_nt(k * eb2, dst)
            ds_ref[h] = jnp.exp(blast) * dst + _dot_tn(dov, q * ebc)
            db = q * dq - k * dk
            dg = _dot01(cmtf, db) + gs_ref[:, sl]
            gs_ref[:, sl] += jnp.sum(db, axis=0, keepdims=True)
            df = dg / f - dk
            dhf_ref[:, sl] = df * (1.0 - lb) * sig * (1.0 - sig)
            dlb_ref[:, sl] += jnp.sum(df * (1.0 - sig), axis=0, keepdims=True)
            sq = _sigmoid(hq)
            dhq = dq * (HGRN_HEAD_DIM ** -0.5) * sq * (1.0 + hq * (1.0 - sq))
            if has_prev:
                dhq = dhq + refs[6][:, sl]
                dv = dv + refs[7][:, sl]
            dhq_ref[:, sl] = dhq
            dhi_ref[:, sl] = dv

    blk = lambda c: pl.BlockSpec((CHUNK, wh), lambda i: (ci(i), c))
    vec = pl.BlockSpec((1, wh), lambda i: (0, 0))
    in_specs = [blk(0), blk(1), blk(3 + int(reverse)), vec, blk(0),
                pl.BlockSpec((1, nh, hd, hd), lambda i: (ci(i), 0, 0, 0))]
    args = [z, z, z, lb, do, states]
    if has_prev:
        in_specs += [blk(0), blk(0)]
        args += list(prev)
    big = jax.ShapeDtypeStruct((T, wh), F32)
    return pl.pallas_call(
        body, name=name,
        out_shape=[big, big, big, jax.ShapeDtypeStruct((1, wh), F32)],
        grid=(nch,), in_specs=in_specs, out_specs=[blk(0), blk(0), blk(0), vec],
        scratch_shapes=[pltpu.VMEM((nh, hd, hd), F32), pltpu.VMEM((1, wh), F32)],
        compiler_params=_params("arbitrary"),
    )(*args)


def _mlstm_intra(cm, cmt, ig_row, ig_col, xf_row, xf_col, m_st):
    lf_row, lf_col = _log_sigmoid(xf_row), _log_sigmoid(xf_col)
    bcol = jnp.sum(jnp.where(cm, lf_row, 0.0), axis=1, keepdims=True)
    brow = jnp.sum(jnp.where(cmt, lf_col, 0.0), axis=0, keepdims=True)
    blast = jnp.sum(lf_row, axis=1, keepdims=True)
    dmat = jnp.where(cm, bcol - brow + ig_row, NEG)
    m_inter = bcol + m_st
    m_t = jnp.maximum(m_inter, jnp.max(dmat, axis=1, keepdims=True))
    p = jnp.exp(dmat - m_t)
    inter = jnp.exp(m_inter - m_t)
    w_col = blast - bcol + ig_col
    m_new = jnp.maximum(blast + m_st, jnp.max(w_col, axis=0, keepdims=True))
    cs = jnp.exp(blast + m_st - m_new)
    kscale = jnp.exp(w_col - m_new)
    return p, inter, m_t, m_new, cs, kscale


def _mlstm_specs(T, wm, nhm, reverse, backward):
    nch = T // CHUNK
    dm = wm // nhm
    ng = 4 * nhm

    def ci(i):
        fwd_order = nch - 1 - i if reverse else i
        return nch - 1 - fwd_order if backward else fwd_order

    row = lambda w, c: pl.BlockSpec((CHUNK, w), lambda i: (ci(i), c))
    gates_row = pl.BlockSpec((1, ng, CHUNK), lambda i: (ci(i), 0, 0))
    bias_row = pl.BlockSpec((1, LANES), lambda i: (0, 0))
    bias_col = pl.BlockSpec((ng, 1), lambda i: (0, 0))
    st_c = pl.BlockSpec((1, nhm, dm, dm), lambda i: (ci(i), 0, 0, 0))
    st_n = pl.BlockSpec((1, nhm, 1, dm), lambda i: (ci(i), 0, 0, 0))
    st_m = pl.BlockSpec((1, nhm, 1, LANES), lambda i: (ci(i), 0, 0, 0))
    return nch, dm, ng, ci, row, gates_row, bias_row, bias_col, st_c, st_n, st_m


def _mlstm_fwd(qk, z, gc, gr, br, bcl, *, wm, nhm, vcol, reverse, name, comm=None):
    T = qk.shape[0]
    nch, dm, ng, ci, row, gates_row, bias_row, bias_col, st_c, st_n, st_m = _mlstm_specs(
        T, wm, nhm, reverse, False)
    d = int(reverse)

    def body(q_ref, k_ref, v_ref, gc_ref, gr_ref, br_ref, bc_ref,
             h_ref, cst_ref, nst_ref, mst_ref, c_ref, n_ref, m_ref):
        @pl.when(pl.program_id(0) == 0)
        def _():
            c_ref[...] = jnp.zeros_like(c_ref)
            n_ref[...] = jnp.zeros_like(n_ref)
            m_ref[...] = jnp.full_like(m_ref, M_INIT)

        cm = _chunk_mask(reverse)
        cmt = _chunk_mask(reverse, transpose=True)
        G = gc_ref[...] + br_ref[...]
        Gr = gr_ref[0] + bc_ref[...]
        for h in range(nhm):
            sl = slice(h * dm, (h + 1) * dm)
            ii, fi = d * nhm + h, 2 * nhm + d * nhm + h
            m_all = m_ref[h]
            m_st = m_all[:, 0:1]
            p, inter, m_t, m_new, cs, kscale = _mlstm_intra(
                cm, cmt, Gr[ii:ii + 1, :], G[:, ii:ii + 1], Gr[fi:fi + 1, :], G[:, fi:fi + 1], m_st)
            q = q_ref[:, sl] * (dm ** -0.5)
            k = k_ref[:, sl]
            v = v_ref[:, sl]
            ct = c_ref[h]
            n = n_ref[h]
            cst_ref[0, h] = ct
            nst_ref[0, h] = n
            mst_ref[0, h] = m_all
            sc = _dot_nt(q, k) * p
            num = _dot_nn(sc, v) + inter * _dot_nt(q, ct)
            den = jnp.sum(sc, axis=1, keepdims=True) + inter * jnp.sum(q * n, axis=1, keepdims=True)
            h_ref[:, sl] = num / jnp.maximum(jnp.abs(den), jnp.exp(-m_t))
            kw = k * kscale
            c_ref[h] = cs * ct + _dot_tn(v, kw)
            n_ref[h] = cs * n + jnp.sum(kw, axis=0, keepdims=True)
            m_ref[h] = jnp.broadcast_to(m_new, (1, LANES))

    return _pcall(
        body, name=name, comm=comm,
        out_shape=[jax.ShapeDtypeStruct((T, wm), F32),
                   jax.ShapeDtypeStruct((nch, nhm, dm, dm), F32),
                   jax.ShapeDtypeStruct((nch, nhm, 1, dm), F32),
                   jax.ShapeDtypeStruct((nch, nhm, 1, LANES), F32)],
        grid=(nch,),
        in_specs=[row(wm, 0), row(wm, 1), row(wm, vcol), row(LANES, 0), gates_row, bias_row, bias_col],
        out_specs=[row(wm, 0), st_c, st_n, st_m],
        scratch_shapes=[pltpu.VMEM((nhm, dm, dm), F32), pltpu.VMEM((nhm, 1, dm), F32),
                        pltpu.VMEM((nhm, 1, LANES), F32)],
        sem=("arbitrary",),
    )(qk, qk, z, gc, gr, br, bcl)


def _mlstm_bwd(qk, z, gc, gr, br, bcl, dh, states, prev, *, wm, nhm, vcol, reverse, name, comm=None):
    T = qk.shape[0]
    nch, dm, ng, ci, row, gates_row, bias_row, bias_col, st_c, st_n, st_m = _mlstm_specs(
        T, wm, nhm, reverse, True)
    d = int(reverse)
    has_prev = prev is not None
    n_in = 13 if has_prev else 11

    def body(*refs):
        (q_ref, k_ref, v_ref, gc_ref, gr_ref, br_ref, bc_ref, dh_ref,
         cst_ref, nst_ref, mst_ref) = refs[:11]
        dqk_ref, dv_ref, dgr_ref, dgs_ref, e_ref, en_ref, fs_ref = refs[n_in:]

        @pl.when(pl.program_id(0) == 0)
        def _():
            e_ref[...] = jnp.zeros_like(e_ref)
            en_ref[...] = jnp.zeros_like(en_ref)
            fs_ref[...] = jnp.zeros_like(fs_ref)
            dgs_ref[...] = jnp.zeros_like(dgs_ref)

        cm = _chunk_mask(reverse)
        cmt = _chunk_mask(reverse, transpose=True)
        row_i = lax.broadcasted_iota(jnp.int32, (CHUNK, CHUNK), 0)
        col_i = lax.broadcasted_iota(jnp.int32, (CHUNK, CHUNK), 1)
        eye = row_i == col_i
        G = gc_ref[...] + br_ref[...]
        Gr = gr_ref[0] + bc_ref[...]
        for h in range(nhm):
            sl = slice(h * dm, (h + 1) * dm)
            slk = slice(wm + h * dm, wm + (h + 1) * dm)
            ii, fi = d * nhm + h, 2 * nhm + d * nhm + h
            m_st = mst_ref[0, h][:, 0:1]
            xf_row = Gr[fi:fi + 1, :]
            p, inter, m_t, m_new, cs, kscale = _mlstm_intra(
                cm, cmt, Gr[ii:ii + 1, :], G[:, ii:ii + 1], xf_row, G[:, fi:fi + 1], m_st)
            q = q_ref[:, sl] * (dm ** -0.5)
            k = k_ref[:, sl]
            v = v_ref[:, sl]
            ct = cst_ref[0, h]
            n = nst_ref[0, h]
            et = e_ref[h]
            en = en_ref[h]
            sc = _dot_nt(q, k) * p
            num = _dot_nn(sc, v) + inter * _dot_nt(q, ct)
            den = jnp.sum(sc, axis=1, keepdims=True) + inter * jnp.sum(q * n, axis=1, keepdims=True)
            floor = jnp.exp(-m_t)
            nstab = jnp.maximum(jnp.abs(den), floor)
            hout = num / nstab
            dhv = dh_ref[:, sl]
            gh = dhv / nstab
            dd = -jnp.sum(dhv * hout, axis=1, keepdims=True) / nstab
            dd = dd * jnp.where(jnp.abs(den) > floor, jnp.sign(den), 0.0)
            dsqk = (_dot_nt(gh, v) + dd) * p
            qi = q * inter
            kw = k * kscale
            dq = _dot_nn(dsqk, k) + inter * (_dot_nn(gh, ct) + dd * n)
            dk = _dot_tn(dsqk, q) + kscale * (_dot_nn(v, et) + en)
            dv = _dot_tn(sc, gh) + _dot_nt(kw, et)
            e_ref[h] = cs * et + _dot_tn(gh, qi)
            en_ref[h] = cs * en + jnp.sum(qi * dd, axis=0, keepdims=True)
            di_col = jnp.sum(k * dk, axis=1, keepdims=True)
            df_col = jnp.sum(q * dq, axis=1, keepdims=True) - di_col
            di_row = jnp.sum(jnp.where(eye, di_col, 0.0), axis=0, keepdims=True)
            carry = fs_ref[h][:, 0:1]
            dlf_row = jnp.sum(jnp.where(cm, df_col, 0.0), axis=0, keepdims=True) + carry
            fs_ref[h] = jnp.broadcast_to(carry + jnp.sum(df_col, axis=0, keepdims=True), (1, LANES))
            dxf_row = dlf_row * (1.0 - _sigmoid(xf_row))
            dgr_ref[0, h:h + 1, :] = di_row
            dgr_ref[0, nhm + h:nhm + h + 1, :] = dxf_row
            dgs_ref[h:h + 1, :] += jnp.broadcast_to(jnp.sum(di_row, axis=1, keepdims=True), (1, LANES))
            dgs_ref[nhm + h:nhm + h + 1, :] += jnp.broadcast_to(
                jnp.sum(dxf_row, axis=1, keepdims=True), (1, LANES))
            dq = dq * (dm ** -0.5)
            if has_prev:
                dq = dq + refs[11][:, sl]
                dk = dk + refs[11][:, slk]
                dv = dv + refs[12][:, sl]
            dqk_ref[:, sl] = dq
            dqk_ref[:, slk] = dk
            dv_ref[:, sl] = dv

    in_specs = [row(wm, 0), row(wm, 1), row(wm, vcol), row(LANES, 0), gates_row, bias_row, bias_col,
                row(wm, 0), st_c, st_n, st_m]
    args = [qk, qk, z, gc, gr, br, bcl, dh] + list(states)
    if has_prev:
        in_specs += [row(2 * wm, 0), row(wm, 0)]
        args += list(prev)
    return _pcall(
        body, name=name, comm=comm,
        out_shape=[jax.ShapeDtypeStruct((T, 2 * wm), F32), jax.ShapeDtypeStruct((T, wm), F32),
                   jax.ShapeDtypeStruct((nch, 2 * nhm, CHUNK), F32),
                   jax.ShapeDtypeStruct((2 * nhm, LANES), F32)],
        grid=(nch,), in_specs=in_specs,
        out_specs=[row(2 * wm, 0), row(wm, 0),
                   pl.BlockSpec((1, 2 * nhm, CHUNK), lambda i: (ci(i), 0, 0)),
                   pl.BlockSpec((2 * nhm, LANES), lambda i: (0, 0))],
        scratch_shapes=[pltpu.VMEM((nhm, dm, dm), F32), pltpu.VMEM((nhm, 1, dm), F32),
                        pltpu.VMEM((nhm, 1, LANES), F32)],
        sem=("arbitrary",),
    )(*args)


def _conv_taps(x, w_ref):
    T = x.shape[0]
    t = lax.broadcasted_iota(jnp.int32, x.shape, 0)
    taps = []
    acc = None
    for j in range(CONV_WIDTH):
        s = CONV_WIDTH // 2 - j
        if s == 0:
            xs = x
        else:
            xs = jnp.where((t - s >= 0) & (t - s < T), pltpu.roll(x, s % T, 0), 0.0)
        taps.append(xs)
        term = w_ref[j:j + 1, :] * xs
        acc = term if acc is None else acc + term
    return taps, acc


def _conv_fwd(z, w, b, *, col0, tc=LANES, name):
    T = z.shape[0]
    C2 = w.shape[1]
    assert col0 % tc == 0 and C2 % tc == 0

    def body(z_ref, w_ref, b_ref, o_ref):
        _, acc = _conv_taps(z_ref[...], w_ref)
        c = acc + b_ref[...]
        o_ref[...] = c * _sigmoid(c)

    return pl.pallas_call(
        body, name=name, out_shape=jax.ShapeDtypeStruct((T, C2), F32),
        grid=(C2 // tc,),
        in_specs=[pl.BlockSpec((T, tc), lambda j: (0, col0 // tc + j)),
                  pl.BlockSpec((CONV_WIDTH, tc), lambda j: (0, j)),
                  pl.BlockSpec((1, tc), lambda j: (0, j))],
        out_specs=pl.BlockSpec((T, tc), lambda j: (0, j)),
        compiler_params=_params("parallel"),
    )(z, w, b)


def _conv_bwd(dy, z, w, b, *, col0, tc=LANES, name):
    T = z.shape[0]
    C2 = w.shape[1]

    def body(dy_ref, z_ref, w_ref, b_ref, dx_ref, dw_ref, db_ref):
        taps, acc = _conv_taps(z_ref[...], w_ref)
        c = acc + b_ref[...]
        sg = _sigmoid(c)
        dc = dy_ref[...] * sg * (1.0 + c * (1.0 - sg))
        t = lax.broadcasted_iota(jnp.int32, dc.shape, 0)
        dx = None
        for j in range(CONV_WIDTH):
            s = j - CONV_WIDTH // 2
            if s == 0:
                ds = dc
            else:
                ds = jnp.where((t - s >= 0) & (t - s < T), pltpu.roll(dc, s % T, 0), 0.0)
            term = w_ref[j:j + 1, :] * ds
            dx = term if dx is None else dx + term
            dw_ref[j:j + 1, :] = jnp.sum(dc * taps[j], axis=0, keepdims=True)
        dx_ref[...] = dx
        db_ref[...] = jnp.sum(dc, axis=0, keepdims=True)

    col = pl.BlockSpec((T, tc), lambda j: (0, j))
    wsp = pl.BlockSpec((CONV_WIDTH, tc), lambda j: (0, j))
    bsp = pl.BlockSpec((1, tc), lambda j: (0, j))
    return pl.pallas_call(
        body, name=name,
        out_shape=[jax.ShapeDtypeStruct((T, C2), F32), jax.ShapeDtypeStruct((CONV_WIDTH, C2), F32),
                   jax.ShapeDtypeStruct((1, C2), F32)],
        grid=(C2 // tc,),
        in_specs=[col, pl.BlockSpec((T, tc), lambda j: (0, col0 // tc + j)), wsp, bsp],
        out_specs=[col, wsp, bsp],
        compiler_params=_params("parallel"),
    )(dy, z, w, b)


def _mix_heads(o_fw, o_bw, h_fw, h_bw, wh, wm, nhm):
    out = []
    hd = HGRN_HEAD_DIM
    for h in range(wh // hd):
        sl = slice(h * hd, (h + 1) * hd)
        o = o_fw[:, sl] + o_bw[:, sl]
        r = lax.rsqrt(jnp.mean(o * o, axis=-1, keepdims=True) + NORM_EPS)
        out.append((0, sl, o * r, r))
    dm = wm // nhm
    for h in range(nhm):
        sl = slice(h * dm, (h + 1) * dm)
        x = h_fw[:, sl] + h_bw[:, sl]
        xc = x - jnp.mean(x, axis=-1, keepdims=True)
        r = lax.rsqrt(jnp.mean(xc * xc, axis=-1, keepdims=True) + NORM_EPS)
        out.append((1, sl, xc * r, r))
    return out


def _mix_specs(T, wh, wm, tm, gcol, ocol):
    rowh = pl.BlockSpec((tm, wh), lambda i: (i, 0))
    rowm = pl.BlockSpec((tm, wm), lambda i: (i, 0))
    hg = pl.BlockSpec((tm, wh), lambda i: (i, gcol))
    mo = pl.BlockSpec((tm, wm), lambda i: (i, ocol))
    vh = pl.BlockSpec((1, wh), lambda i: (0, 0))
    vm = pl.BlockSpec((1, wm), lambda i: (0, 0))
    return rowh, rowm, hg, mo, vh, vm


def _mix_fwd(o_fw, o_bw, h_fw, h_bw, z, gh, gm, *, nhm, gcol, ocol, tm=256, name):
    T, wh = o_fw.shape
    wm = h_fw.shape[1]
    tm = min(tm, T)
    rowh, rowm, hg, mo, vh, vm = _mix_specs(T, wh, wm, tm, gcol, ocol)

    def body(of_ref, ob_ref, hf_ref, hb_ref, hg_ref, mo_ref, gh_ref, gm_ref, y_ref, yt_ref):
        heads = _mix_heads(of_ref[...], ob_ref[...], hf_ref[...], hb_ref[...], wh, wm, nhm)
        for grp, sl, nv, _ in heads:
            if grp == 0:
                gate = hg_ref[:, sl]
                gate = gate * _sigmoid(gate)
                y = nv * gh_ref[:, sl] * gate
                osl = sl
            else:
                y = nv * gm_ref[:, sl] * _sigmoid(mo_ref[:, sl])
                osl = slice(wh + sl.start, wh + sl.stop)
            y_ref[:, osl] = y.astype(BF16)
            yt_ref[osl, :] = y.T.astype(BF16)

    return pl.pallas_call(
        body, name=name,
        out_shape=[jax.ShapeDtypeStruct((T, wh + wm), BF16), jax.ShapeDtypeStruct((wh + wm, T), BF16)],
        grid=(T // tm,),
        in_specs=[rowh, rowh, rowm, rowm, hg, mo, vh, vm],
        out_specs=[pl.BlockSpec((tm, wh + wm), lambda i: (i, 0)),
                   pl.BlockSpec((wh + wm, tm), lambda i: (0, i))],
        compiler_params=_params("parallel"),
    )(o_fw, o_bw, h_fw, h_bw, z, z, gh, gm)


def _mix_bwd(dy, o_fw, o_bw, h_fw, h_bw, z, gh, gm, *, nhm, gcol, ocol, tm=256, name):
    T, wh = o_fw.shape
    wm = h_fw.shape[1]
    tm = min(tm, T)
    rowh, rowm, hg, mo, vh, vm = _mix_specs(T, wh, wm, tm, gcol, ocol)

    def body(dy_ref, of_ref, ob_ref, hf_ref, hb_ref, hg_ref, mo_ref, gh_ref, gm_ref,
             do_ref, dh_ref, dhg_ref, dmo_ref, dgh_ref, dgm_ref):
        @pl.when(pl.program_id(0) == 0)
        def _():
            dgh_ref[...] = jnp.zeros_like(dgh_ref)
            dgm_ref[...] = jnp.zeros_like(dgm_ref)

        heads = _mix_heads(of_ref[...], ob_ref[...], hf_ref[...], hb_ref[...], wh, wm, nhm)
        for grp, sl, nv, r in heads:
            if grp == 0:
                d = dy_ref[:, sl]
                x = hg_ref[:, sl]
                sg = _sigmoid(x)
                gate = x * sg
                g = gh_ref[:, sl]
                dgh_ref[:, sl] += jnp.sum(d * nv * gate, axis=0, keepdims=True)
                dhg_ref[:, sl] = d * nv * g * sg * (1.0 + x * (1.0 - sg))
                t = d * g * gate
                do_ref[:, sl] = r * (t - nv * jnp.mean(t * nv, axis=-1, keepdims=True))
            else:
                d = dy_ref[:, slice(wh + sl.start, wh + sl.stop)]
                sg = _sigmoid(mo_ref[:, sl])
                g = gm_ref[:, sl]
                dgm_ref[:, sl] += jnp.sum(d * nv * sg, axis=0, keepdims=True)
                dmo_ref[:, sl] = d * nv * g * sg * (1.0 - sg)
                t = d * g * sg
                dh_ref[:, sl] = r * (t - jnp.mean(t, axis=-1, keepdims=True)
                                     - nv * jnp.mean(t * nv, axis=-1, keepdims=True))

    return pl.pallas_call(
        body, name=name,
        out_shape=[jax.ShapeDtypeStruct((T, wh), F32), jax.ShapeDtypeStruct((T, wm), F32),
                   jax.ShapeDtypeStruct((T, wh), F32), jax.ShapeDtypeStruct((T, wm), F32),
                   jax.ShapeDtypeStruct((1, wh), F32), jax.ShapeDtypeStruct((1, wm), F32)],
        grid=(T // tm,),
        in_specs=[pl.BlockSpec((tm, wh + wm), lambda i: (i, 0)), rowh, rowh, rowm, rowm, hg, mo, vh, vm],
        out_specs=[rowh, rowm, rowh, rowm, vh, vm],
        compiler_params=_params("arbitrary"),
    )(dy, o_fw, o_bw, h_fw, h_bw, z, z, gh, gm)


def _pick(n, pref):
    for c in range(pref - pref % LANES, 0, -LANES):
        if n % c == 0:
            return c
    return n


def _mm(a, b, mode="nn", **kw):
    if mode == "nn":
        (M, K), N = a.shape, b.shape[1]
    elif mode == "nt":
        (M, K), N = a.shape, b.shape[0]
    else:
        (K, M), N = a.shape, b.shape[1]
    tk = _pick(K, 2048) if mode != "tn" else _pick(K, 512)
    return _matmul(a, b, mode=mode, tm=_pick(M, 1024), tn=_pick(N, 1024), tk=tk, **kw)


def _device_step(x, target, sh, small_parts, small_params, core):
    T, D = x.shape

    def pair_add(name, g, land):
        return _pair_add(g, land, core, "rs_add_" + name)

    g_w1a, g_w3a, g_small = _comm_call(_ag_job([sh["ffn1_w1"], sh["ffn1_w3"], small_parts]), "ag_ffn1")
    P = small_params(g_small)
    wh = P["hgrn_norm_g"].shape[1]
    wm = P["mlstm_norm_g"].shape[1]
    nhm = P["ig_b"].shape[1]
    ng = 4 * nhm
    nch = T // CHUNK
    assert wh == wm and T % CHUNK == 0
    vcol, ocol, gcol = 7, 8, 2
    col0 = 5 * wh
    xb, xt = _cast_t(x, name="x_cast")
    (a1, b1, h1), (g_w2a, g_wout) = _ffn_up(xb, g_w1a, g_w3a, name="ffn1_up",
                                            comm=_ag_job([sh["ffn1_w2"], sh["w_out"]]))
    r1, (g_win,) = _ffn_down(h1, g_w2a, x, name="ffn1_down", comm=_ag_job([sh["w_in"]]))
    win_full = g_win.transpose(1, 0, 2).reshape(D, -1)
    ncols = win_full.shape[1]
    w_win = win_full[:, :ncols - ng]
    w_wg = jnp.zeros((D, LANES), BF16).at[:, :ng].set(win_full[:, ncols - ng:])
    w_wout = g_wout.reshape(D, D)
    x1, x1b, x1t = _ln_fwd(r1, P["ln1_g"], P["ln1_b"], name="ln1")
    zm = _mm(x1b, w_win, name="zm")
    zg = _mm(x1b, w_wg, name="zg")
    gr = zg[:, :ng].reshape(nch, CHUNK, ng).transpose(0, 2, 1)
    bias = jnp.concatenate([P["ig_b"].reshape(-1), P["fg_b"].reshape(-1)])
    br = jnp.zeros((1, LANES), F32).at[0, :ng].set(bias)
    bcl = bias.reshape(ng, 1)
    lbf, lbb = P["lb"][0:1], P["lb"][1:2]
    (o_fw, s_fw), (g_w1b,) = _hgrn_fwd(zm, lbf, wh=wh, reverse=False, name="hgrn_fw",
                                       comm=_ag_job([sh["ffn2_w1"]]))
    (o_bw, s_bw), (g_w3b,) = _hgrn_fwd(zm, lbb, wh=wh, reverse=True, name="hgrn_bw",
                                       comm=_ag_job([sh["ffn2_w3"]]))
    qk = _conv_fwd(zm, P["conv_w"], P["conv_b"], col0=col0, name="conv")
    (h_fw, *st_fw), (g_w2b,) = _mlstm_fwd(qk, zm, zg, gr, br, bcl, wm=wm, nhm=nhm, vcol=vcol,
                                          reverse=False, name="mlstm_fw", comm=_ag_job([sh["ffn2_w2"]]))
    h_bw, *st_bw = _mlstm_fwd(qk, zm, zg, gr, br, bcl, wm=wm, nhm=nhm, vcol=vcol, reverse=True,
                              name="mlstm_bw")
    y, yt = _mix_fwd(o_fw, o_bw, h_fw, h_bw, zm, P["hgrn_norm_g"], P["mlstm_norm_g"],
                     nhm=nhm, gcol=gcol, ocol=ocol, name="mix")
    r2 = _mm(y, w_wout, add=x1, add_scale=DN_ALPHA, name="r2")
    x2, x2b, x2t = _ln_fwd(r2, P["ln2_g"], P["ln2_b"], name="ln2")
    a2, b2, h2 = _ffn_up(x2b, g_w1b, g_w3b, name="ffn2_up")
    r3 = _ffn_down(h2, g_w2b, x2, name="ffn2_down")

    G = {}
    dr3, dr3b, dr3t, G["ln3_g"], G["ln3_b"], loss = _ln_bwd(
        None, r3, P["ln3_g"], b=P["ln3_b"], target=target, name="ln3_bwd")
    land = {}
    da2, db2 = _ffn_bwd_act(dr3b, g_w2b, a2, b2, name="ffn2_dact")
    gw1b, gw3b = _ffn_dw(x2t, [da2, db2], name="ffn2_dw13")
    (gw2bt,) = _ffn_dw(dr3t, [h2], scale=0.5, name="ffn2_dw2t")
    dx2, l1 = _ffn_dx(da2, db2, g_w1b, g_w3b, dr3, name="ffn2_dx", comm=_pair_job([gw1b, gw3b, gw2bt]))
    p_w1b, p_w3b, p_w2bt = [pair_add(k, g, l) for k, g, l in
                            zip(("ffn2_w1", "ffn2_w3", "ffn2_w2"), (gw1b, gw3b, gw2bt), l1)]
    dr2, dr2b, _, G["ln2_g"], G["ln2_b"] = _ln_bwd(dx2, r2, P["ln2_g"], name="ln2_bwd")
    dy = _mm(dr2b, w_wout, "nt", name="dy")
    gwout = _mm(yt, dr2b, out_dtype=BF16, name="dwout").reshape(N_DEV, -1, D)
    do, dh, dhg, dmo, G["hgrn_norm_g"], G["mlstm_norm_g"] = _mix_bwd(
        dy, o_fw, o_bw, h_fw, h_bw, zm, P["hgrn_norm_g"], P["mlstm_norm_g"],
        nhm=nhm, gcol=gcol, ocol=ocol, name="mix_bwd")
    (dqk_f, dv_f, dgr_f, dgs_f), (land["ffn2_w1"], land["ffn2_w3"]) = _mlstm_bwd(
        qk, zm, zg, gr, br, bcl, dh, st_fw, None, wm=wm, nhm=nhm, vcol=vcol, reverse=False,
        name="mlstm_fw_bwd", comm=_chips_job([p_w1b, p_w3b]))
    (dqk, dv, dgr_b, dgs_b), (land["ffn2_w2"],) = _mlstm_bwd(
        qk, zm, zg, gr, br, bcl, dh, st_bw, (dqk_f, dv_f), wm=wm, nhm=nhm, vcol=vcol, reverse=True,
        name="mlstm_bw_bwd", comm=_chips_job([p_w2bt]))
    dmqk, G["conv_w"], G["conv_b"] = _conv_bwd(dqk, zm, P["conv_w"], P["conv_b"], col0=col0,
                                               name="conv_bwd")
    dhq_f, dhi_f, dhf_fw, dlb_f = _hgrn_bwd(zm, lbf, do, s_fw, None, wh=wh, reverse=False,
                                            name="hgrn_fw_bwd")
    dhq, dhi, dhf_bw, dlb_b = _hgrn_bwd(zm, lbb, do, s_bw, (dhq_f, dhi_f), wh=wh, reverse=True,
                                        name="hgrn_bw_bwd")
    G["lb"] = jnp.concatenate([dlb_f, dlb_b], axis=0)
    G["ig_b"] = jnp.stack([dgs_f[:nhm, 0], dgs_b[:nhm, 0]])
    G["fg_b"] = jnp.stack([dgs_f[nhm:, 0], dgs_b[nhm:, 0]])
    dzm = jnp.concatenate([dhq, dhi, dhg, dhf_fw, dhf_bw, dmqk, dv, dmo], axis=1).astype(BF16)
    dgr = jnp.concatenate([dgr_f[:, :nhm], dgr_b[:, :nhm], dgr_f[:, nhm:], dgr_b[:, nhm:]], axis=1)
    dzg = jnp.zeros((T, LANES), F32).at[:, :ng].set(dgr.transpose(0, 2, 1).reshape(T, ng)).astype(BF16)
    dwin = jnp.concatenate([_mm(x1t, dzm, out_dtype=BF16, name="dwin"),
                            _mm(x1t, dzg, out_dtype=BF16, name="dwg")[:, :ng]], axis=1)
    gwin = dwin.reshape(D, N_DEV, -1).transpose(1, 0, 2)
    t = _mm(dzg, w_wg, "nt", add=dr2, add_scale=DN_ALPHA, name="dx1_g")
    dx1, l1 = _mm(dzm, w_win, "nt", add=t, name="dx1", comm=_pair_job([gwout, gwin]))
    p_wout, p_win = pair_add("w_out", gwout, l1[0]), pair_add("w_in", gwin, l1[1])
    dr1, dr1b, dr1t, G["ln1_g"], G["ln1_b"] = _ln_bwd(dx1, r1, P["ln1_g"], name="ln1_bwd")
    (da1, db1), (land["w_in"],) = _ffn_bwd_act(dr1b, g_w2a, a1, b1, name="ffn1_dact",
                                               comm=_chips_job([p_win]))
    (gw1a, gw3a), (land["w_out"],) = _ffn_dw(xt, [da1, db1], name="ffn1_dw13", comm=_chips_job([p_wout]))
    (gw2at,), l1 = _ffn_dw(dr1t, [h1], scale=0.5, name="ffn1_dw2t", comm=_pair_job([gw1a, gw3a]))
    p_w1a, p_w3a = pair_add("ffn1_w1", gw1a, l1[0]), pair_add("ffn1_w3", gw3a, l1[1])
    gx, (land["ffn1_w1"], land["ffn1_w3"], l1) = _ffn_dx(
        da1, db1, g_w1a, g_w3a, dr1, name="ffn1_dx",
        comm=_join(_chips_job([p_w1a, p_w3a]), _pair_job([gw2at])))
    (land["ffn1_w2"],) = _comm_call(_chips_job([pair_add("ffn1_w2", gw2at, l1)]), "rs_ffn1_w2")
    return loss, gx, land, G


ANY = pl.BlockSpec(memory_space=pl.ANY)


def _place():
    x, y, c = lax.axis_index("x"), lax.axis_index("y"), lax.axis_index("c")
    chips = [(1 - x, y), (x, 1 - y), (1 - x, 1 - y)]
    return x, y, c, chips


def _comm_call(job, name):
    n_in, n_out = len(job.inputs), len(job.out_shape)

    def body(*refs):
        ins, outs, sems = refs[:n_in], refs[n_in:n_in + n_out], refs[n_in + n_out:]
        job.start(ins, outs, sems)
        job.finish(ins, outs, sems)

    return pl.pallas_call(body, name=name, out_shape=job.out_shape, in_specs=[ANY] * n_in,
                          out_specs=[ANY] * n_out, scratch_shapes=job.sems)(*job.inputs)


def _ag_job(parts):
    n = len(parts)

    def copies(ins, outs, sems):
        send_sems, recv_sems, local_sems = sems
        x, y, c, chips = _place()
        me, sibling = (x, y, c), (x, y, 1 - c)

        def copy(a, k, block, to, src=None):
            bx, by, bc = block
            slot = outs[a].at[4 * bx + 2 * by + bc]
            return pltpu.make_async_remote_copy(
                src_ref=slot if src is None else src, dst_ref=slot,
                send_sem=send_sems.at[a, k], recv_sem=recv_sems.at[a, k],
                device_id=to, device_id_type=MESH)

        local = [pltpu.make_async_copy(ins[a], outs[a].at[4 * x + 2 * y + c], local_sems.at[a])
                 for a in range(n)]
        first = []
        for a in range(n):
            first.append(copy(a, 0, me, sibling, src=ins[a]))
            first += [copy(a, 1 + j, me, (*chip, c), src=ins[a]) for j, chip in enumerate(chips)]
        return copy, local, first, chips, me, sibling, c

    def start(ins, outs, sems):
        _, local, first, *_ = copies(ins, outs, sems)
        for cp in local + first:
            cp.start()

    def finish(ins, outs, sems):
        copy, local, first, chips, me, sibling, c = copies(ins, outs, sems)
        passed = []
        for a in range(n):
            for j, chip in enumerate(chips):
                copy(a, 1 + j, (*chip, c), me).wait_recv()
                fwd = copy(a, 4 + j, (*chip, c), sibling)
                fwd.start()
                passed.append(fwd)
        for a in range(n):
            copy(a, 0, sibling, me).wait_recv()
            for j, chip in enumerate(chips):
                copy(a, 4 + j, (*chip, 1 - c), me).wait_recv()
        for cp in first + passed:
            cp.wait_send()
        for cp in local:
            cp.wait()

    return _Job(parts, [jax.ShapeDtypeStruct((N_DEV,) + p.shape, p.dtype) for p in parts],
                [pltpu.SemaphoreType.DMA((n, 7)), pltpu.SemaphoreType.DMA((n, 7)),
                 pltpu.SemaphoreType.DMA((n,))], start, finish)


def _pair_job(gs):
    n = len(gs)

    def copies(g_refs, land_refs, sems):
        send_sems, recv_sems = sems
        x, y, c, _ = _place()
        return [pltpu.make_async_remote_copy(
            src_ref=g_refs[a].at[2 * k + 1 - c], dst_ref=land_refs[a].at[k],
            send_sem=send_sems.at[a, k], recv_sem=recv_sems.at[a, k],
            device_id=(x, y, 1 - c), device_id_type=MESH) for a in range(n) for k in range(4)]

    def start(ins, outs, sems):
        for cp in copies(ins, outs, sems):
            cp.start()

    def finish(ins, outs, sems):
        for cp in copies(ins, outs, sems):
            cp.wait()

    return _Job(gs, [jax.ShapeDtypeStruct((4,) + g.shape[1:], g.dtype) for g in gs],
                [pltpu.SemaphoreType.DMA((n, 4)), pltpu.SemaphoreType.DMA((n, 4))], start, finish)


def _chips_job(ps):
    n = len(ps)

    def copies(p_refs, land_refs, sems):
        send_sems, recv_sems, local_sems = sems
        x, y, c, chips = _place()
        mine = 2 * x + y
        owns = [pltpu.make_async_copy(p_refs[a].at[mine], land_refs[a].at[mine], local_sems.at[a])
                for a in range(n)]
        sends = [pltpu.make_async_remote_copy(
            src_ref=p_refs[a].at[2 * chip[0] + chip[1]], dst_ref=land_refs[a].at[mine],
            send_sem=send_sems.at[a, j], recv_sem=recv_sems.at[a, j],
            device_id=(*chip, c), device_id_type=MESH) for a in range(n) for j, chip in enumerate(chips)]
        recvs = [pltpu.make_async_remote_copy(
            src_ref=p_refs[a].at[mine], dst_ref=land_refs[a].at[2 * chip[0] + chip[1]],
            send_sem=send_sems.at[a, j], recv_sem=recv_sems.at[a, j],
            device_id=(*chip, c), device_id_type=MESH) for a in range(n) for j, chip in enumerate(chips)]
        return owns, sends, recvs

    def start(ins, outs, sems):
        owns, sends, _ = copies(ins, outs, sems)
        for cp in owns + sends:
            cp.start()

    def finish(ins, outs, sems):
        owns, sends, recvs = copies(ins, outs, sems)
        for cp in recvs:
            cp.wait_recv()
        for cp in sends:
            cp.wait_send()
        for cp in owns:
            cp.wait()

    return _Job(ps, [jax.ShapeDtypeStruct(p.shape, p.dtype) for p in ps],
                [pltpu.SemaphoreType.DMA((n, 3)), pltpu.SemaphoreType.DMA((n, 3)),
                 pltpu.SemaphoreType.DMA((n,))], start, finish)


def _all_reduce_small(buf, name):
    rows = buf.shape[0]

    def body(b_ref, o_ref, slots, send_sems, recv_sems):
        x, y, c, _ = _place()
        me = 4 * x + 2 * y + c
        slots[me] = b_ref[...]
        cps = []
        for k in range(1, N_DEV):
            fx, fy, fc = (k >> 2) & 1, (k >> 1) & 1, k & 1
            peer = (x ^ fx, y ^ fy, c ^ fc)
            cps.append(pltpu.make_async_remote_copy(
                src_ref=b_ref, dst_ref=slots.at[me],
                send_sem=send_sems.at[k - 1], recv_sem=recv_sems.at[k - 1],
                device_id=peer, device_id_type=MESH))
        for cp in cps:
            cp.start()
        for k in range(1, N_DEV):
            fx, fy, fc = (k >> 2) & 1, (k >> 1) & 1, k & 1
            src = 4 * (x ^ fx) + 2 * (y ^ fy) + (c ^ fc)
            pltpu.make_async_remote_copy(
                src_ref=b_ref, dst_ref=slots.at[src],
                send_sem=send_sems.at[k - 1], recv_sem=recv_sems.at[k - 1],
                device_id=(x ^ fx, y ^ fy, c ^ fc), device_id_type=MESH).wait_recv()
        for cp in cps:
            cp.wait_send()
        acc = slots[0]
        for k in range(1, N_DEV):
            acc = acc + slots[k]
        o_ref[...] = acc

    vm = pl.BlockSpec(memory_space=pltpu.VMEM)
    return pl.pallas_call(
        body, name=name, out_shape=jax.ShapeDtypeStruct(buf.shape, F32),
        in_specs=[vm], out_specs=vm,
        scratch_shapes=[pltpu.VMEM((N_DEV, rows, LANES), F32),
                        pltpu.SemaphoreType.DMA((N_DEV - 1,)), pltpu.SemaphoreType.DMA((N_DEV - 1,))],
    )(buf)


def _row_tile(rows, pref=2048):
    for t in range(min(pref, rows) - min(pref, rows) % 8, 0, -8):
        if rows % t == 0:
            return t
    return rows


def _tile_rows(rows, cols, nbuf):
    budget = VMEM_LIMIT // 2 // (2 * nbuf * 4 * cols)
    return _row_tile(rows, max(8, min(512, budget // 8 * 8)))


def _pair_add(g, land, core, name):
    _, r, c = g.shape
    tr = _tile_rows(r, c, 3)

    def body(core_ref, g_ref, l_ref, o_ref):
        o_ref[...] = (g_ref[...].astype(F32) + l_ref[...].astype(F32)).astype(o_ref.dtype)

    blk = pl.BlockSpec((1, tr, c), lambda k, i, cr: (k, i, 0))
    return pl.pallas_call(
        body, name=name, out_shape=jax.ShapeDtypeStruct(land.shape, land.dtype),
        grid_spec=pltpu.PrefetchScalarGridSpec(
            num_scalar_prefetch=1, grid=(4, r // tr),
            in_specs=[pl.BlockSpec((1, tr, c), lambda k, i, cr: (2 * k + cr[0], i, 0)), blk],
            out_specs=blk),
        compiler_params=_params("parallel", "parallel"),
    )(core, g, land)


def _sum4(land, name):
    _, r, c = land.shape
    tr = _tile_rows(r, c, 5)

    def body(a_ref, b_ref, c_ref, d_ref, o_ref):
        o_ref[...] = ((a_ref[0].astype(F32) + b_ref[0].astype(F32)) + c_ref[0].astype(F32)
                      + d_ref[0].astype(F32))

    slab = lambda k: pl.BlockSpec((1, tr, c), lambda i: (k, i, 0))
    return pl.pallas_call(
        body, name=name, out_shape=jax.ShapeDtypeStruct((r, c), F32),
        grid=(r // tr,), in_specs=[slab(0), slab(1), slab(2), slab(3)],
        out_specs=pl.BlockSpec((tr, c), lambda i: (i, 0)),
        compiler_params=_params("parallel"),
    )(land, land, land, land)


def _adamw(w, parts, m, v, name):
    shp = w.shape
    cols = shp[-1]
    w2, m2, v2 = (t.reshape(-1, cols) for t in (w, m, v))
    rows = w2.shape[0]
    n = parts.shape[0]
    assert parts.shape[1:] == (rows, cols), (parts.shape, shp)
    tr = _tile_rows(rows, cols, n + 7)
    c1 = 1.0 / (1.0 - ADAM_B1 ** ADAM_STEP)
    c2 = 1.0 / (1.0 - ADAM_B2 ** ADAM_STEP)

    def body(*refs):
        p_refs = refs[:n]
        w_ref, m_ref, v_ref, g_ref, d_ref, nm_ref, nv_ref = refs[n:]
        gv = p_refs[0][0].astype(F32)
        for p_ref in p_refs[1:]:
            gv = gv + p_ref[0].astype(F32)
        nm = ADAM_B1 * m_ref[...] + (1.0 - ADAM_B1) * gv
        nv = ADAM_B2 * v_ref[...] + (1.0 - ADAM_B2) * (gv * gv)
        g_ref[...] = gv
        nm_ref[...] = nm
        nv_ref[...] = nv
        d_ref[...] = -ADAM_LR * ((nm * c1) / (jnp.sqrt(nv * c2) + ADAM_EPS) + ADAM_WD * w_ref[...])

    blk = pl.BlockSpec((tr, cols), lambda i: (i, 0))
    slab = lambda k: pl.BlockSpec((1, tr, cols), lambda i: (k, i, 0))
    out = jax.ShapeDtypeStruct((rows, cols), F32)
    outs = pl.pallas_call(
        body, name=name, out_shape=[out] * 4, grid=(rows // tr,),
        in_specs=[slab(k) for k in range(n)] + [blk] * 3, out_specs=[blk] * 4,
        compiler_params=_params("parallel"),
    )(*([parts] * n), w2, m2, v2)
    return tuple(o.reshape(shp) for o in outs)


def _pack_rows(arrs):
    rows = []
    for a in arrs:
        flat = a.reshape(-1).astype(F32)
        pad = (-flat.shape[0]) % LANES
        if pad:
            flat = jnp.concatenate([flat, jnp.zeros((pad,), F32)])
        rows.append(flat.reshape(-1, LANES))
    out = jnp.concatenate(rows, axis=0)
    pad = (-out.shape[0]) % 8
    if pad:
        out = jnp.concatenate([out, jnp.zeros((pad, LANES), F32)], axis=0)
    return out


def _unpack_rows(buf, shapes):
    outs, r = [], 0
    for s in shapes:
        n = math.prod(s)
        nr = -(-n // LANES)
        outs.append(buf[r:r + nr].reshape(-1)[:n].reshape(s))
        r += nr
    return outs


BIG = ("ffn1_w1", "ffn1_w3", "ffn1_w2", "w_in", "w_out", "ffn2_w1", "ffn2_w3", "ffn2_w2")
ROW_SHARDED = ("ffn1_w2", "w_out", "ffn2_w2")
SMALL = ("ln1_g", "ln1_b", "hgrn_lb", "hgrn_norm_g", "mlstm_conv_w", "mlstm_conv_b", "mlstm_ig_b",
         "mlstm_fg_b", "mlstm_norm_g", "ln2_g", "ln2_b", "ln3_g", "ln3_b")
WEIGHTS = ("ffn1_w1", "ffn1_w3", "ffn1_w2", "ln1_g", "ln1_b", "w_in", "hgrn_lb", "hgrn_norm_g",
           "mlstm_conv_w", "mlstm_conv_b", "mlstm_ig_b", "mlstm_fg_b", "mlstm_norm_g", "w_out",
           "ln2_g", "ln2_b", "ffn2_w1", "ffn2_w3", "ffn2_w2", "ln3_g", "ln3_b")


def kernel(x, ffn1_w1, ffn1_w3, ffn1_w2, ln1_g, ln1_b, w_in, hgrn_lb, hgrn_norm_g, mlstm_conv_w, mlstm_conv_b, mlstm_ig_b, mlstm_fg_b, mlstm_norm_g, w_out, ln2_g, ln2_b, ffn2_w1, ffn2_w3, ffn2_w2, ln3_g, ln3_b, loss_target, m_ffn1_w1, m_ffn1_w3, m_ffn1_w2, m_ln1_g, m_ln1_b, m_w_in, m_hgrn_lb, m_hgrn_norm_g, m_mlstm_conv_w, m_mlstm_conv_b, m_mlstm_ig_b, m_mlstm_fg_b, m_mlstm_norm_g, m_w_out, m_ln2_g, m_ln2_b, m_ffn2_w1, m_ffn2_w3, m_ffn2_w2, m_ln3_g, m_ln3_b, v_ffn1_w1, v_ffn1_w3, v_ffn1_w2, v_ln1_g, v_ln1_b, v_w_in, v_hgrn_lb, v_hgrn_norm_g, v_mlstm_conv_w, v_mlstm_conv_b, v_mlstm_ig_b, v_mlstm_fg_b, v_mlstm_norm_g, v_w_out, v_ln2_g, v_ln2_b, v_ffn2_w1, v_ffn2_w3, v_ffn2_w2, v_ln3_g, v_ln3_b):
    args = (ffn1_w1, ffn1_w3, ffn1_w2, ln1_g, ln1_b, w_in, hgrn_lb, hgrn_norm_g, mlstm_conv_w,
            mlstm_conv_b, mlstm_ig_b, mlstm_fg_b, mlstm_norm_g, w_out, ln2_g, ln2_b, ffn2_w1, ffn2_w3,
            ffn2_w2, ln3_g, ln3_b)
    ms = (m_ffn1_w1, m_ffn1_w3, m_ffn1_w2, m_ln1_g, m_ln1_b, m_w_in, m_hgrn_lb, m_hgrn_norm_g,
          m_mlstm_conv_w, m_mlstm_conv_b, m_mlstm_ig_b, m_mlstm_fg_b, m_mlstm_norm_g, m_w_out, m_ln2_g,
          m_ln2_b, m_ffn2_w1, m_ffn2_w3, m_ffn2_w2, m_ln3_g, m_ln3_b)
    vs = (v_ffn1_w1, v_ffn1_w3, v_ffn1_w2, v_ln1_g, v_ln1_b, v_w_in, v_hgrn_lb, v_hgrn_norm_g,
          v_mlstm_conv_w, v_mlstm_conv_b, v_mlstm_ig_b, v_mlstm_fg_b, v_mlstm_norm_g, v_w_out, v_ln2_g,
          v_ln2_b, v_ffn2_w1, v_ffn2_w3, v_ffn2_w2, v_ln3_g, v_ln3_b)
    w = dict(zip(WEIGHTS, args))
    m = dict(zip(WEIGHTS, ms))
    v = dict(zip(WEIGHTS, vs))
    core = lax.axis_index("c")
    dev = 4 * lax.axis_index("x") + 2 * lax.axis_index("y") + core
    p_lb = []

    def small_params(gathered_small):
        lb_sh, cw_sh = zip(*[_unpack_rows(gathered_small[i], [hgrn_lb.shape, mlstm_conv_w.shape])
                             for i in range(N_DEV)])
        hgrn_lb_full = jnp.concatenate(lb_sh, axis=-1)
        conv_w_full = jnp.concatenate(cw_sh, axis=-1)[0]
        p_lb.append(jax.nn.softmax(hgrn_lb_full, axis=1))
        return dict(ln1_g=ln1_g, ln1_b=ln1_b, ln2_g=ln2_g, ln2_b=ln2_b, ln3_g=ln3_g, ln3_b=ln3_b,
                    lb=p_lb[0][:, 0], hgrn_norm_g=hgrn_norm_g, conv_w=conv_w_full, conv_b=mlstm_conv_b,
                    ig_b=mlstm_ig_b[0], fg_b=mlstm_fg_b[0], mlstm_norm_g=mlstm_norm_g)

    loss_row, grad_x, land2, G = _device_step(
        x[0], loss_target[0], {k: w[k][0].astype(BF16) for k in BIG},
        _pack_rows([hgrn_lb, mlstm_conv_w]), small_params, jnp.reshape(core, (1,)).astype(jnp.int32))
    p_lb = p_lb[0]

    dlb = G["lb"]
    g_lb = jnp.stack([dlb * p_lb[:, 0] * (1.0 - p_lb[:, 0]), -dlb * p_lb[:, 0] * p_lb[:, 1]], axis=1)
    small_full = {"ln1_g": G["ln1_g"], "ln1_b": G["ln1_b"], "hgrn_lb": g_lb, "hgrn_norm_g": G["hgrn_norm_g"],
                  "mlstm_conv_w": G["conv_w"][None], "mlstm_conv_b": G["conv_b"],
                  "mlstm_ig_b": G["ig_b"][None], "mlstm_fg_b": G["fg_b"][None],
                  "mlstm_norm_g": G["mlstm_norm_g"], "ln2_g": G["ln2_g"], "ln2_b": G["ln2_b"],
                  "ln3_g": G["ln3_g"], "ln3_b": G["ln3_b"]}
    small_list = [small_full[k] for k in SMALL] + [loss_row]
    reduced = _all_reduce_small(_pack_rows(small_list), "all_reduce_small")
    red = _unpack_rows(reduced, [a.shape for a in small_list])
    loss = red[-1][0, 0]
    small_g = {}
    for k, gk in zip(SMALL, red[:-1]):
        if k in ("hgrn_lb", "mlstm_conv_w"):
            n = w[k].shape[-1]
            gk = lax.dynamic_slice_in_dim(gk, dev * n, n, axis=gk.ndim - 1)
        small_g[k] = gk

    grads, delta, new_m, new_v = {}, {}, {}, {}
    for k in BIG:
        contrib = land2[k]
        if k in ROW_SHARDED and k != "w_out":
            contrib = _sum4(contrib, "rs_sum_" + k).T[None]
        grads[k], delta[k], new_m[k], new_v[k] = _adamw(w[k], contrib, m[k], v[k], "adamw_" + k)
    sm = _adamw(_pack_rows([w[k] for k in SMALL]), _pack_rows([small_g[k] for k in SMALL])[None],
                _pack_rows([m[k] for k in SMALL]), _pack_rows([v[k] for k in SMALL]), "adamw_small")
    shapes = [w[k].shape for k in SMALL]
    for dst, buf in zip((grads, delta, new_m, new_v), sm):
        for k, val in zip(SMALL, _unpack_rows(buf, shapes)):
            dst[k] = val
    return (loss, grad_x[None], *[grads[k] for k in WEIGHTS], *[delta[k] for k in WEIGHTS],
            *[new_m[k] for k in WEIGHTS], *[new_v[k] for k in WEIGHTS])
```

```python
import math

import jax
import jax.numpy as jnp
from jax import lax
from jax.experimental import pallas as pl
from jax.experimental.pallas import tpu as pltpu

F32 = jnp.float32
BF16 = jnp.bfloat16

CHUNK = 64
HGRN_HEAD_DIM = 128
CONV_WIDTH = 5
DN_ALPHA = 2.0 ** 0.25
LN_EPS = 1e-5
NORM_EPS = 1e-6
M_INIT = -1e30
NEG = -1e30
EXP_CLAMP = 80.0
ADAM_LR = 0.001
ADAM_B1 = 0.9
ADAM_B2 = 0.999
ADAM_EPS = 1e-08
ADAM_WD = 0.01
ADAM_STEP = 10
N_DEV = 8
SHARDS_PER_STEP = 2
LANES = 128
VMEM_LIMIT = 56 * 1024 * 1024
MESH = pl.DeviceIdType.MESH


def _params(*sem):
    return pltpu.CompilerParams(dimension_semantics=sem, vmem_limit_bytes=VMEM_LIMIT)


class _Job:
    def __init__(self, inputs, out_shape, sems, start, finish):
        self.inputs, self.out_shape, self.sems = list(inputs), list(out_shape), list(sems)
        self.start, self.finish = start, finish


def _join(*jobs):
    def split(refs, counts):
        out, p = [], 0
        for c in counts:
            out.append(refs[p:p + c])
            p += c
        return out

    n_in = [len(j.inputs) for j in jobs]
    n_out = [len(j.out_shape) for j in jobs]
    n_sem = [len(j.sems) for j in jobs]

    def start(ins, outs, sems):
        for j, i, o, s in zip(jobs, split(ins, n_in), split(outs, n_out), split(sems, n_sem)):
            j.start(i, o, s)

    def finish(ins, outs, sems):
        for j, i, o, s in zip(jobs, split(ins, n_in), split(outs, n_out), split(sems, n_sem)):
            j.finish(i, o, s)

    return _Job(sum((j.inputs for j in jobs), []), sum((j.out_shape for j in jobs), []),
                sum((j.sems for j in jobs), []), start, finish)


def _pcall(body, *, name, out_shape, grid, in_specs, out_specs, sem, scratch_shapes=(), comm=None):
    single = not isinstance(out_shape, (list, tuple))
    out_shape = [out_shape] if single else list(out_shape)
    out_specs = [out_specs] if single else list(out_specs)
    in_specs, scratch_shapes = list(in_specs), list(scratch_shapes)
    if comm is None:
        call = pl.pallas_call(body, name=name, out_shape=out_shape, grid=grid, in_specs=in_specs,
                              out_specs=out_specs, scratch_shapes=scratch_shapes,
                              compiler_params=_params(*sem))

        def run(*args):
            outs = call(*args)
            return outs[0] if single else outs
        return run

    n_in, n_out, n_sc = len(in_specs), len(out_shape), len(scratch_shapes)
    c_in, c_out = len(comm.inputs), len(comm.out_shape)

    def hosted(*refs):
        ins, cins = refs[:n_in], refs[n_in:n_in + c_in]
        p = n_in + c_in
        outs, couts = refs[p:p + n_out], refs[p + n_out:p + n_out + c_out]
        p += n_out + c_out
        scratch, csems = refs[p:p + n_sc], refs[p + n_sc:]
        first = pl.program_id(0) == 0
        last = pl.program_id(0) == grid[0] - 1
        for d in range(1, len(grid)):
            first = first & (pl.program_id(d) == 0)
            last = last & (pl.program_id(d) == grid[d] - 1)

        @pl.when(first)
        def _():
            comm.start(cins, couts, csems)

        body(*ins, *outs, *scratch)

        @pl.when(last)
        def _():
            comm.finish(cins, couts, csems)

    any_spec = pl.BlockSpec(memory_space=pl.ANY)
    call = pl.pallas_call(
        hosted, name=name, out_shape=out_shape + comm.out_shape, grid=grid,
        in_specs=in_specs + [any_spec] * c_in, out_specs=out_specs + [any_spec] * c_out,
        scratch_shapes=scratch_shapes + comm.sems,
        compiler_params=_params(*(["arbitrary"] * len(grid))))

    def run(*args):
        res = call(*args, *comm.inputs)
        outs, couts = res[:n_out], list(res[n_out:])
        return (outs[0] if single else outs), couts
    return run


def _sigmoid(x):
    return 1.0 / (1.0 + jnp.exp(-x))


def _log_sigmoid(x):
    return jnp.minimum(x, 0.0) - jnp.log(1.0 + jnp.exp(-jnp.abs(x)))


def _dot(a, b, dims):
    return lax.dot_general(a.astype(BF16), b.astype(BF16), (dims, ((), ())),
                           preferred_element_type=F32)


def _dot3(a, b, dims):
    ah = a.astype(BF16)
    al = (a - ah.astype(F32)).astype(BF16)
    bh = b.astype(BF16)
    bl = (b - bh.astype(F32)).astype(BF16)
    d = (dims, ((), ()))
    out = lax.dot_general(ah, bh, d, preferred_element_type=F32)
    out = out + lax.dot_general(ah, bl, d, preferred_element_type=F32)
    return out + lax.dot_general(al, bh, d, preferred_element_type=F32)


_DIMS = {"nn": ((1,), (0,)), "nt": ((1,), (1,)), "tn": ((0,), (0,))}


def _dot_nn(a, b):
    return _dot3(a, b, _DIMS["nn"])


def _dot_nt(a, b):
    return _dot3(a, b, _DIMS["nt"])


def _dot_tn(a, b):
    return _dot3(a, b, _DIMS["tn"])


def _split3(x):
    hi = x.astype(BF16)
    r1 = x - hi.astype(F32)
    mid = r1.astype(BF16)
    lo = (r1 - mid.astype(F32)).astype(BF16)
    return hi, mid, lo


def _dot01(mask01, x, mode="nn"):
    m = mask01.astype(BF16)
    hi, mid, lo = _split3(x)
    d = (_DIMS[mode], ((), ()))
    out = lax.dot_general(m, hi, d, preferred_element_type=F32)
    out = out + lax.dot_general(m, mid, d, preferred_element_type=F32)
    return out + lax.dot_general(m, lo, d, preferred_element_type=F32)


def _matmul(a, b, *, mode="nn", out_dtype=F32, tm=512, tn=512, tk=None,
            add=None, scale=1.0, add_scale=1.0, name, comm=None):
    if mode == "nn":
        (M, K), (K2, N) = a.shape, b.shape
    elif mode == "nt":
        (M, K), (N, K2) = a.shape, b.shape
    else:
        (K, M), (K2, N) = a.shape, b.shape
    assert K == K2, (a.shape, b.shape, mode)
    tm, tn = min(tm, M), min(tn, N)
    tk = min(tk or K, K)
    assert M % tm == 0 and N % tn == 0 and K % tk == 0, (M, N, K, tm, tn, tk)
    nk = K // tk
    dims = _DIMS[mode]
    has_add = add is not None

    def body(*refs):
        if has_add:
            a_ref, b_ref, add_ref, o_ref = refs[:4]
        else:
            a_ref, b_ref, o_ref = refs[:3]
            add_ref = None
        acc_ref = refs[-1] if nk > 1 else None

        def finish(acc):
            out = acc if scale == 1.0 else acc * scale
            if has_add:
                out = out + add_ref[...].astype(F32) * add_scale
            o_ref[...] = out.astype(o_ref.dtype)

        if nk == 1:
            finish(_dot(a_ref[...], b_ref[...], dims))
        else:
            k = pl.program_id(2)

            @pl.when(k == 0)
            def _():
                acc_ref[...] = jnp.zeros_like(acc_ref)

            acc_ref[...] += _dot(a_ref[...], b_ref[...], dims)

            @pl.when(k == nk - 1)
            def _():
                finish(acc_ref[...])

    if mode == "tn":
        a_spec = pl.BlockSpec((tk, tm), lambda i, j, k: (k, i))
    else:
        a_spec = pl.BlockSpec((tm, tk), lambda i, j, k: (i, k))
    if mode == "nt":
        b_spec = pl.BlockSpec((tn, tk), lambda i, j, k: (j, k))
    else:
        b_spec = pl.BlockSpec((tk, tn), lambda i, j, k: (k, j))
    o_spec = pl.BlockSpec((tm, tn), lambda i, j, k: (i, j))
    in_specs = [a_spec, b_spec] + ([o_spec] if has_add else [])
    args = (a, b) + ((add,) if has_add else ())
    return _pcall(
        body, name=name, comm=comm,
        out_shape=jax.ShapeDtypeStruct((M, N), out_dtype),
        grid=(M // tm, N // tn, nk),
        in_specs=in_specs, out_specs=o_spec,
        scratch_shapes=[pltpu.VMEM((tm, tn), F32)] if nk > 1 else [],
        sem=("parallel", "parallel", "arbitrary"),
    )(*args)


def _ln_stats(x):
    mu = jnp.mean(x, axis=-1, keepdims=True)
    xc = x - mu
    var = jnp.mean(xc * xc, axis=-1, keepdims=True)
    rstd = lax.rsqrt(var + LN_EPS)
    return xc * rstd, rstd


def _ln_fwd(r, g, b, *, tm=256, name):
    T, D = r.shape
    tm = min(tm, T)

    def body(r_ref, g_ref, b_ref, y_ref, yb_ref, yt_ref):
        xhat, _ = _ln_stats(r_ref[...])
        y = xhat * g_ref[...] + b_ref[...]
        y_ref[...] = y
        yb_ref[...] = y.astype(BF16)
        yt_ref[...] = y.T.astype(BF16)

    row = pl.BlockSpec((tm, D), lambda i: (i, 0))
    vec = pl.BlockSpec((1, D), lambda i: (0, 0))
    return pl.pallas_call(
        body, name=name,
        out_shape=[jax.ShapeDtypeStruct((T, D), F32), jax.ShapeDtypeStruct((T, D), BF16),
                   jax.ShapeDtypeStruct((D, T), BF16)],
        grid=(T // tm,), in_specs=[row, vec, vec],
        out_specs=[row, row, pl.BlockSpec((D, tm), lambda i: (0, i))],
        compiler_params=_params("parallel"),
    )(r, g, b)


def _cast_t(x, *, tm=256, name):
    T, D = x.shape
    tm = min(tm, T)

    def body(x_ref, xb_ref, xt_ref):
        xv = x_ref[...]
        xb_ref[...] = xv.astype(BF16)
        xt_ref[...] = xv.T.astype(BF16)

    row = pl.BlockSpec((tm, D), lambda i: (i, 0))
    return pl.pallas_call(
        body, name=name,
        out_shape=[jax.ShapeDtypeStruct((T, D), BF16), jax.ShapeDtypeStruct((D, T), BF16)],
        grid=(T // tm,), in_specs=[row], out_specs=[row, pl.BlockSpec((D, tm), lambda i: (0, i))],
        compiler_params=_params("parallel"),
    )(x)


def _ln_bwd(dy, r, g, *, tm=256, name, b=None, target=None):
    T, D = r.shape
    tm = min(tm, T)
    with_loss = target is not None

    def body(*refs):
        if with_loss:
            r_ref, g_ref, b_ref, t_ref, dr_ref, drb_ref, drt_ref, dg_ref, db_ref, loss_ref = refs
        else:
            dy_ref, r_ref, g_ref, dr_ref, drb_ref, drt_ref, dg_ref, db_ref = refs
        i = pl.program_id(0)
        xhat, rstd = _ln_stats(r_ref[...])
        gg = g_ref[...]
        if with_loss:
            err = xhat * gg + b_ref[...] - t_ref[...]
            dyv = err * (1.0 / D)
            part = jnp.sum(jnp.sum(err * err, axis=1, keepdims=True), axis=0, keepdims=True)
            part = jnp.broadcast_to(part * (0.5 / D), (1, LANES))
        else:
            dyv = dy_ref[...]
        dxh = dyv * gg
        m1 = jnp.mean(dxh, axis=-1, keepdims=True)
        m2 = jnp.mean(dxh * xhat, axis=-1, keepdims=True)
        dr = rstd * (dxh - m1 - xhat * m2)
        dr_ref[...] = dr
        drb_ref[...] = dr.astype(BF16)
        drt_ref[...] = dr.T.astype(BF16)
        dgp = jnp.sum(dyv * xhat, axis=0, keepdims=True)
        dbp = jnp.sum(dyv, axis=0, keepdims=True)

        @pl.when(i == 0)
        def _():
            dg_ref[...] = dgp
            db_ref[...] = dbp
            if with_loss:
                loss_ref[...] = part

        @pl.when(i > 0)
        def _():
            dg_ref[...] += dgp
            db_ref[...] += dbp
            if with_loss:
                loss_ref[...] += part

    row = pl.BlockSpec((tm, D), lambda i: (i, 0))
    vec = pl.BlockSpec((1, D), lambda i: (0, 0))
    out_shape = [jax.ShapeDtypeStruct((T, D), F32), jax.ShapeDtypeStruct((T, D), BF16),
                 jax.ShapeDtypeStruct((D, T), BF16), jax.ShapeDtypeStruct((1, D), F32),
                 jax.ShapeDtypeStruct((1, D), F32)]
    out_specs = [row, row, pl.BlockSpec((D, tm), lambda i: (0, i)), vec, vec]
    if with_loss:
        in_specs, args = [row, vec, vec, row], (r, g, b, target)
        out_shape.append(jax.ShapeDtypeStruct((1, LANES), F32))
        out_specs.append(pl.BlockSpec((1, LANES), lambda i: (0, 0)))
    else:
        in_specs, args = [row, row, vec], (dy, r, g)
    return pl.pallas_call(
        body, name=name, out_shape=out_shape, grid=(T // tm,),
        in_specs=in_specs, out_specs=out_specs,
        compiler_params=_params("arbitrary"),
    )(*args)


def _ffn_up(xb, w1g, w3g, *, tm=512, name, comm=None):
    S, D, Fs = w1g.shape
    T = xb.shape[0]
    tm = _pick(T, tm)

    def body(x_ref, w1_ref, w3_ref, a_ref, b_ref, h_ref):
        xv = x_ref[...]
        a = _dot(xv, w1_ref[0], _DIMS["nn"])
        b = _dot(xv, w3_ref[0], _DIMS["nn"])
        a_ref[0] = a
        b_ref[0] = b
        h_ref[0] = (a * _sigmoid(a) * b).astype(BF16)

    wsp = pl.BlockSpec((1, D, Fs), lambda j, i: (j, 0, 0))
    osp = pl.BlockSpec((1, tm, Fs), lambda j, i: (j, i, 0))
    act = jax.ShapeDtypeStruct((S, T, Fs), F32)
    return _pcall(
        body, name=name, comm=comm, out_shape=[act, act, jax.ShapeDtypeStruct((S, T, Fs), BF16)],
        grid=(S, T // tm),
        in_specs=[pl.BlockSpec((tm, D), lambda j, i: (i, 0)), wsp, wsp],
        out_specs=[osp, osp, osp],
        sem=("parallel", "parallel"),
    )(xb, w1g, w3g)


def _ffn_down(h, w2g, x, *, tm=512, name, comm=None):
    S, Fs, D = w2g.shape
    T = x.shape[0]
    tm = _pick(T, tm)

    sp = SHARDS_PER_STEP

    def body(h_ref, w2_ref, x_ref, r_ref, acc_ref):
        j = pl.program_id(1)
        @pl.when(j == 0)
        def _():
            acc_ref[...] = jnp.zeros_like(acc_ref)

        part = _dot(h_ref[0], w2_ref[0], _DIMS["nn"])
        for s in range(1, sp):
            part = part + _dot(h_ref[s], w2_ref[s], _DIMS["nn"])
        acc_ref[...] += part

        @pl.when(j == S // sp - 1)
        def _():
            r_ref[...] = DN_ALPHA * x_ref[...] + 0.5 * acc_ref[...]

    row = pl.BlockSpec((tm, D), lambda i, j: (i, 0))
    return _pcall(
        body, name=name, comm=comm, out_shape=jax.ShapeDtypeStruct((T, D), F32),
        grid=(T // tm, S // sp),
        in_specs=[pl.BlockSpec((sp, tm, Fs), lambda i, j: (j, i, 0)),
                  pl.BlockSpec((sp, Fs, D), lambda i, j: (j, 0, 0)), row],
        out_specs=row, scratch_shapes=[pltpu.VMEM((tm, D), F32)],
        sem=("parallel", "arbitrary"),
    )(h, w2g, x)


def _ffn_bwd_act(drb, w2g, a, b, *, tm=512, name, comm=None):
    S, Fs, D = w2g.shape
    T = drb.shape[0]
    tm = _pick(T, tm)

    def body(dr_ref, w2_ref, a_ref, b_ref, da_ref, db_ref):
        d = 0.5 * _dot(dr_ref[...], w2_ref[0], _DIMS["nt"])
        av = a_ref[0]
        sig = _sigmoid(av)
        da_ref[0] = (d * b_ref[0] * sig * (1.0 + av * (1.0 - sig))).astype(BF16)
        db_ref[0] = (d * av * sig).astype(BF16)

    asp = pl.BlockSpec((1, tm, Fs), lambda j, i: (j, i, 0))
    out = jax.ShapeDtypeStruct((S, T, Fs), BF16)
    return _pcall(
        body, name=name, comm=comm, out_shape=[out, out], grid=(S, T // tm),
        in_specs=[pl.BlockSpec((tm, D), lambda j, i: (i, 0)),
                  pl.BlockSpec((1, Fs, D), lambda j, i: (j, 0, 0)), asp, asp],
        out_specs=[asp, asp],
        sem=("parallel", "parallel"),
    )(drb, w2g, a, b)


def _ffn_dw(lhs_t, rhs, *, scale=1.0, tm=512, name, comm=None):
    n = len(rhs)
    S, T, Fs = rhs[0].shape
    D = lhs_t.shape[0]
    tm = _pick(D, tm)

    def body(*refs):
        l_ref = refs[0]
        r_refs, o_refs = refs[1:1 + n], refs[1 + n:]
        lv = l_ref[...]
        for r_ref, o_ref in zip(r_refs, o_refs):
            res = _dot(lv, r_ref[0], _DIMS["nn"])
            o_ref[0] = (res if scale == 1.0 else res * scale).astype(o_ref.dtype)

    osp = pl.BlockSpec((1, tm, Fs), lambda j, i: (j, i, 0))
    return _pcall(
        body, name=name, comm=comm, out_shape=[jax.ShapeDtypeStruct((S, D, Fs), BF16)] * n,
        grid=(S, D // tm),
        in_specs=[pl.BlockSpec((tm, T), lambda j, i: (i, 0))]
        + [pl.BlockSpec((1, T, Fs), lambda j, i: (j, 0, 0))] * n,
        out_specs=[osp] * n,
        sem=("parallel", "parallel"),
    )(lhs_t, *rhs)


def _ffn_dx(da, db, w1g, w3g, dr, *, tm=512, name, comm=None):
    S, D, Fs = w1g.shape
    T = dr.shape[0]
    tm = _pick(T, tm)

    sp = SHARDS_PER_STEP

    def body(da_ref, db_ref, w1_ref, w3_ref, dr_ref, o_ref, acc_ref):
        j = pl.program_id(1)
        @pl.when(j == 0)
        def _():
            acc_ref[...] = jnp.zeros_like(acc_ref)

        part = None
        for s in range(sp):
            t = (_dot(da_ref[s], w1_ref[s], _DIMS["nt"]) + _dot(db_ref[s], w3_ref[s], _DIMS["nt"]))
            part = t if part is None else part + t
        acc_ref[...] += part

        @pl.when(j == S // sp - 1)
        def _():
            o_ref[...] = DN_ALPHA * dr_ref[...] + acc_ref[...]

    asp = pl.BlockSpec((sp, tm, Fs), lambda i, j: (j, i, 0))
    wsp = pl.BlockSpec((sp, D, Fs), lambda i, j: (j, 0, 0))
    row = pl.BlockSpec((tm, D), lambda i, j: (i, 0))
    return _pcall(
        body, name=name, comm=comm, out_shape=jax.ShapeDtypeStruct((T, D), F32),
        grid=(T // tm, S // sp), in_specs=[asp, asp, wsp, wsp, row], out_specs=row,
        scratch_shapes=[pltpu.VMEM((tm, D), F32)],
        sem=("parallel", "arbitrary"),
    )(da, db, w1g, w3g, dr)


def _chunk_mask(reverse, transpose=False):
    row = lax.broadcasted_iota(jnp.int32, (CHUNK, CHUNK), 0)
    col = lax.broadcasted_iota(jnp.int32, (CHUNK, CHUNK), 1)
    if reverse != transpose:
        return col >= row
    return col <= row


def _hgrn_pre(hq, hf, lb):
    sig = _sigmoid(hf)
    f = lb + (1.0 - lb) * sig
    q = hq * _sigmoid(hq) * (HGRN_HEAD_DIM ** -0.5)
    return q, f, sig


def _hgrn_decays(f, cmf, reverse):
    bc = _dot01(cmf, jnp.log(f))
    last = 0 if reverse else CHUNK - 1
    blast = bc[last:last + 1, :]
    bref = bc[CHUNK // 2:CHUNK // 2 + 1, :]
    eq = jnp.exp(jnp.minimum(bc - bref, EXP_CLAMP))
    ek = jnp.exp(jnp.minimum(bref - bc, EXP_CLAMP))
    return bc, blast, eq, ek


def _hgrn_fwd(z, lb, *, wh, reverse, name, comm=None):
    T = z.shape[0]
    nch = T // CHUNK
    nh = wh // HGRN_HEAD_DIM
    hd = HGRN_HEAD_DIM

    def ci(i):
        return nch - 1 - i if reverse else i

    def body(hq_ref, hi_ref, hf_ref, lb_ref, o_ref, st_ref, s_ref):
        @pl.when(pl.program_id(0) == 0)
        def _():
            s_ref[...] = jnp.zeros_like(s_ref)

        cm = _chunk_mask(reverse)
        cmf = cm.astype(F32)
        H = range(nh)
        sls = [slice(h * hd, (h + 1) * hd) for h in H]
        q, f, _ = _hgrn_pre(hq_ref[...], hf_ref[...], lb_ref[...])
        v = hi_ref[...]
        k = 1.0 - f
        bc, blast, eq, ek = _hgrn_decays(f, cmf, reverse)
        qh, kh, qe = q * eq, k * ek, q * jnp.exp(bc)
        k2, eblast = k * jnp.exp(blast - bc), jnp.exp(blast)
        st = [s_ref[h] for h in H]
        att = [jnp.where(cm, _dot_nt(qh[:, sl], kh[:, sl]), 0.0) for sl in sls]
        inter = [_dot_nt(qe[:, sls[h]], st[h]) for h in H]
        out = [inter[h] + _dot_nn(att[h], v[:, sls[h]]) for h in H]
        snew = [eblast[:, sls[h]] * st[h] + _dot_tn(v[:, sls[h]], k2[:, sls[h]]) for h in H]
        for h in H:
            st_ref[0, h] = st[h]
            o_ref[:, sls[h]] = out[h]
            s_ref[h] = snew[h]

    blk = lambda c: pl.BlockSpec((CHUNK, wh), lambda i: (ci(i), c))
    return _pcall(
        body, name=name, comm=comm,
        out_shape=[jax.ShapeDtypeStruct((T, wh), F32),
                   jax.ShapeDtypeStruct((nch, nh, hd, hd), F32)],
        grid=(nch,),
        in_specs=[blk(0), blk(1), blk(3 + int(reverse)),
                  pl.BlockSpec((1, wh), lambda i: (0, 0))],
        out_specs=[blk(0), pl.BlockSpec((1, nh, hd, hd), lambda i: (ci(i), 0, 0, 0))],
        scratch_shapes=[pltpu.VMEM((nh, hd, hd), F32)],
        sem=("arbitrary",),
    )(z, z, z, lb)


def _hgrn_bwd(z, lb, do, states, prev, *, wh, reverse, name):
    T = z.shape[0]
    nch = T // CHUNK
    nh = wh // HGRN_HEAD_DIM
    hd = HGRN_HEAD_DIM
    has_prev = prev is not None

    def ci(i):
        return i if reverse else nch - 1 - i

    def body(*refs):
        hq_ref, hi_ref, hf_ref, lb_ref, do_ref, st_ref = refs[:6]
        n_in = 8 if has_prev else 6
        dhq_ref, dhi_ref, dhf_ref, dlb_ref, ds_ref, gs_ref = refs[n_in:]

        @pl.when(pl.program_id(0) == 0)
        def _():
            ds_ref[...] = jnp.zeros_like(ds_ref)
            gs_ref[...] = jnp.zeros_like(gs_ref)
            dlb_ref[...] = jnp.zeros_like(dlb_ref)

        cm = _chunk_mask(reverse)
        cmf = cm.astype(F32)
        cmtf = _chunk_mask(reverse, transpose=True).astype(F32)
        H = range(nh)
        sls = [slice(h * hd, (h + 1) * hd) for h in H]
        hq, lb = hq_ref[...], lb_ref[...]
        q, f, sig = _hgrn_pre(hq, hf_ref[...], lb)
        v, dov = hi_ref[...], do_ref[...]
        k = 1.0 - f
        bc, blast, eq, ek = _hgrn_decays(f, cmf, reverse)
        ebc, eb2, eblast = jnp.exp(bc), jnp.exp(blast - bc), jnp.exp(blast)
        qh, kh, qe, k2 = q * eq, k * ek, q * ebc, k * eb2
        st = [st_ref[0, h] for h in H]
        dst = [ds_ref[h] for h in H]
        att = [jnp.where(cm, _dot_nt(qh[:, sl], kh[:, sl]), 0.0) for sl in sls]
        datt = [jnp.where(cm, _dot_nt(dov[:, sl], v[:, sl]), 0.0) for sl in sls]
        dq_a = [_dot_nn(datt[h], kh[:, sls[h]]) for h in H]
        dq_s = [_dot_nn(dov[:, sls[h]], st[h]) for h in H]
        dk_a = [_dot_tn(datt[h], qh[:, sls[h]]) for h in H]
        dk_s = [_dot_nn(v[:, sls[h]], dst[h]) for h in H]
        dv = [_dot_tn(att[h], dov[:, sls[h]]) + _dot_nt(k2[:, sls[h]], dst[h]) for h in H]
        dsn = [eblast[:, sls[h]] * dst[h] + _dot_tn(dov[:, sls[h]], qe[:, sls[h]]) for h in H]
        for h in H:
            ds_ref[h] = dsn[h]
        dq = jnp.concatenate(dq_a, axis=1) * eq + ebc * jnp.concatenate(dq_s, axis=1)
        dk = jnp.concatenate(dk_a, axis=1) * ek + eb2 * jnp.concatenate(dk_s, axis=1)
        dvv = jnp.concatenate(dv, axis=1)
        db = q * dq - k * dk
        dg = _dot01(cmtf, db) + gs_ref[...]
        gs_ref[...] += jnp.sum(db, axis=0, keepdims=True)
        df = dg / f - dk
        dhf_ref[...] = df * (1.0 - lb) * sig * (1.0 - sig)
        dlb_ref[...] += jnp.sum(df * (1.0 - sig), axis=0, keepdims=True)
        sq = _sigmoid(hq)
        dhq = dq * (HGRN_HEAD_DIM ** -0.5) * sq * (1.0 + hq * (1.0 - sq))
        if has_prev:
            dhq = dhq + refs[6][...]
            dvv = dvv + refs[7][...]
        dhq_ref[...] = dhq
        dhi_ref[...] = dvv

    blk = lambda c: pl.BlockSpec((CHUNK, wh), lambda i: (ci(i), c))
    vec = pl.BlockSpec((1, wh), lambda i: (0, 0))
    in_specs = [blk(0), blk(1), blk(3 + int(reverse)), vec, blk(0),
                pl.BlockSpec((1, nh, hd, hd), lambda i: (ci(i), 0, 0, 0))]
    args = [z, z, z, lb, do, states]
    if has_prev:
        in_specs += [blk(0), blk(0)]
        args += list(prev)
    big = jax.ShapeDtypeStruct((T, wh), F32)
    return pl.pallas_call(
        body, name=name,
        out_shape=[big, big, big, jax.ShapeDtypeStruct((1, wh), F32)],
        grid=(nch,), in_specs=in_specs, out_specs=[blk(0), blk(0), blk(0), vec],
        scratch_shapes=[pltpu.VMEM((nh, hd, hd), F32), pltpu.VMEM((1, wh), F32)],
        compiler_params=_params("arbitrary"),
    )(*args)


def _mlstm_intra(cm, cmt, ig_row, ig_col, xf_row, xf_col, m_st):
    lf_row, lf_col = _log_sigmoid(xf_row), _log_sigmoid(xf_col)
    bcol = jnp.sum(jnp.where(cm, lf_row, 0.0), axis=1, keepdims=True)
    brow = jnp.sum(jnp.where(cmt, lf_col, 0.0), axis=0, keepdims=True)
    blast = jnp.sum(lf_row, axis=1, keepdims=True)
    dmat = jnp.where(cm, bcol - brow + ig_row, NEG)
    m_inter = bcol + m_st
    m_t = jnp.maximum(m_inter, jnp.max(dmat, axis=1, keepdims=True))
    p = jnp.exp(dmat - m_t)
    inter = jnp.exp(m_inter - m_t)
    w_col = blast - bcol + ig_col
    m_new = jnp.maximum(blast + m_st, jnp.max(w_col, axis=0, keepdims=True))
    cs = jnp.exp(blast + m_st - m_new)
    kscale = jnp.exp(w_col - m_new)
    return p, inter, m_t, m_new, cs, kscale


def _mlstm_specs(T, wm, nhm, reverse, backward):
    nch = T // CHUNK
    dm = wm // nhm
    ng = 4 * nhm

    def ci(i):
        fwd_order = nch - 1 - i if reverse else i
        return nch - 1 - fwd_order if backward else fwd_order

    row = lambda w, c: pl.BlockSpec((CHUNK, w), lambda i: (ci(i), c))
    gates_row = pl.BlockSpec((1, ng, CHUNK), lambda i: (ci(i), 0, 0))
    bias_row = pl.BlockSpec((1, LANES), lambda i: (0, 0))
    bias_col = pl.BlockSpec((ng, 1), lambda i: (0, 0))
    st_c = pl.BlockSpec((1, nhm, dm, dm), lambda i: (ci(i), 0, 0, 0))
    st_n = pl.BlockSpec((1, nhm, 1, dm), lambda i: (ci(i), 0, 0, 0))
    st_m = pl.BlockSpec((1, nhm, 1, LANES), lambda i: (ci(i), 0, 0, 0))
    return nch, dm, ng, ci, row, gates_row, bias_row, bias_col, st_c, st_n, st_m


def _mlstm_fwd(qk, z, gc, gr, br, bcl, *, wm, nhm, vcol, reverse, name, comm=None):
    T = qk.shape[0]
    nch, dm, ng, ci, row, gates_row, bias_row, bias_col, st_c, st_n, st_m = _mlstm_specs(
        T, wm, nhm, reverse, False)
    d = int(reverse)

    def body(q_ref, k_ref, v_ref, gc_ref, gr_ref, br_ref, bc_ref,
             h_ref, cst_ref, nst_ref, mst_ref, c_ref, n_ref, m_ref):
        @pl.when(pl.program_id(0) == 0)
        def _():
            c_ref[...] = jnp.zeros_like(c_ref)
            n_ref[...] = jnp.zeros_like(n_ref)
            m_ref[...] = jnp.full_like(m_ref, M_INIT)

        cm = _chunk_mask(reverse)
        cmt = _chunk_mask(reverse, transpose=True)
        G = gc_ref[...] + br_ref[...]
        Gr = gr_ref[0] + bc_ref[...]
        H = range(nhm)
        sls = [slice(h * dm, (h + 1) * dm) for h in H]
        m_all = [m_ref[h] for h in H]
        intra = [_mlstm_intra(cm, cmt, Gr[d * nhm + h:d * nhm + h + 1, :], G[:, d * nhm + h:d * nhm + h + 1],
                              Gr[2 * nhm + d * nhm + h:2 * nhm + d * nhm + h + 1, :],
                              G[:, 2 * nhm + d * nhm + h:2 * nhm + d * nhm + h + 1], m_all[h][:, 0:1])
                 for h in H]
        p, inter, m_t, m_new, cs, kscale = zip(*intra)
        q = [q_ref[:, sl] * (dm ** -0.5) for sl in sls]
        k = [k_ref[:, sl] for sl in sls]
        v = [v_ref[:, sl] for sl in sls]
        ct = [c_ref[h] for h in H]
        n = [n_ref[h] for h in H]
        sc = [_dot_nt(q[h], k[h]) * p[h] for h in H]
        qc = [_dot_nt(q[h], ct[h]) for h in H]
        num = [_dot_nn(sc[h], v[h]) + inter[h] * qc[h] for h in H]
        den = [jnp.sum(sc[h], axis=1, keepdims=True) + inter[h] * jnp.sum(q[h] * n[h], axis=1, keepdims=True)
               for h in H]
        kw = [k[h] * kscale[h] for h in H]
        cnew = [cs[h] * ct[h] + _dot_tn(v[h], kw[h]) for h in H]
        for h in H:
            cst_ref[0, h] = ct[h]
            nst_ref[0, h] = n[h]
            mst_ref[0, h] = m_all[h]
            h_ref[:, sls[h]] = num[h] / jnp.maximum(jnp.abs(den[h]), jnp.exp(-m_t[h]))
            c_ref[h] = cnew[h]
            n_ref[h] = cs[h] * n[h] + jnp.sum(kw[h], axis=0, keepdims=True)
            m_ref[h] = jnp.broadcast_to(m_new[h], (1, LANES))

    return _pcall(
        body, name=name, comm=comm,
        out_shape=[jax.ShapeDtypeStruct((T, wm), F32),
                   jax.ShapeDtypeStruct((nch, nhm, dm, dm), F32),
                   jax.ShapeDtypeStruct((nch, nhm, 1, dm), F32),
                   jax.ShapeDtypeStruct((nch, nhm, 1, LANES), F32)],
        grid=(nch,),
        in_specs=[row(wm, 0), row(wm, 1), row(wm, vcol), row(LANES, 0), gates_row, bias_row, bias_col],
        out_specs=[row(wm, 0), st_c, st_n, st_m],
        scratch_shapes=[pltpu.VMEM((nhm, dm, dm), F32), pltpu.VMEM((nhm, 1, dm), F32),
                        pltpu.VMEM((nhm, 1, LANES), F32)],
        sem=("arbitrary",),
    )(qk, qk, z, gc, gr, br, bcl)


def _mlstm_bwd(qk, z, gc, gr, br, bcl, dh, states, prev, *, wm, nhm, vcol, reverse, name, comm=None):
    T = qk.shape[0]
    nch, dm, ng, ci, row, gates_row, bias_row, bias_col, st_c, st_n, st_m = _mlstm_specs(
        T, wm, nhm, reverse, True)
    d = int(reverse)
    has_prev = prev is not None
    n_in = 13 if has_prev else 11

    def body(*refs):
        (q_ref, k_ref, v_ref, gc_ref, gr_ref, br_ref, bc_ref, dh_ref,
         cst_ref, nst_ref, mst_ref) = refs[:11]
        dqk_ref, dv_ref, dgr_ref, dgs_ref, e_ref, en_ref, fs_ref = refs[n_in:]

        @pl.when(pl.program_id(0) == 0)
        def _():
            e_ref[...] = jnp.zeros_like(e_ref)
            en_ref[...] = jnp.zeros_like(en_ref)
            fs_ref[...] = jnp.zeros_like(fs_ref)
            dgs_ref[...] = jnp.zeros_like(dgs_ref)

        cm = _chunk_mask(reverse)
        cmt = _chunk_mask(reverse, transpose=True)
        row_i = lax.broadcasted_iota(jnp.int32, (CHUNK, CHUNK), 0)
        col_i = lax.broadcasted_iota(jnp.int32, (CHUNK, CHUNK), 1)
        eye = row_i == col_i
        G = gc_ref[...] + br_ref[...]
        Gr = gr_ref[0] + bc_ref[...]
        H = range(nhm)
        sls = [slice(h * dm, (h + 1) * dm) for h in H]
        xf_row = [Gr[2 * nhm + d * nhm + h:2 * nhm + d * nhm + h + 1, :] for h in H]
        intra = [_mlstm_intra(cm, cmt, Gr[d * nhm + h:d * nhm + h + 1, :], G[:, d * nhm + h:d * nhm + h + 1],
                              xf_row[h], G[:, 2 * nhm + d * nhm + h:2 * nhm + d * nhm + h + 1],
                              mst_ref[0, h][:, 0:1]) for h in H]
        p, inter, m_t, _, cs, kscale = zip(*intra)
        q = [q_ref[:, sls[h]] * (dm ** -0.5) for h in H]
        k = [k_ref[:, sls[h]] for h in H]
        v = [v_ref[:, sls[h]] for h in H]
        dhv = [dh_ref[:, sls[h]] for h in H]
        ct = [cst_ref[0, h] for h in H]
        n = [nst_ref[0, h] for h in H]
        et = [e_ref[h] for h in H]
        en = [en_ref[h] for h in H]
        carry = [fs_ref[h][:, 0:1] for h in H]
        sc = [_dot_nt(q[h], k[h]) * p[h] for h in H]
        qc = [_dot_nt(q[h], ct[h]) for h in H]
        num = [_dot_nn(sc[h], v[h]) + inter[h] * qc[h] for h in H]
        den = [jnp.sum(sc[h], axis=1, keepdims=True) + inter[h] * jnp.sum(q[h] * n[h], axis=1, keepdims=True)
               for h in H]
        floor = [jnp.exp(-m_t[h]) for h in H]
        nstab = [jnp.maximum(jnp.abs(den[h]), floor[h]) for h in H]
        gh = [dhv[h] / nstab[h] for h in H]
        dd = [-jnp.sum(dhv[h] * (num[h] / nstab[h]), axis=1, keepdims=True) / nstab[h]
              * jnp.where(jnp.abs(den[h]) > floor[h], jnp.sign(den[h]), 0.0) for h in H]
        dsqk = [(_dot_nt(gh[h], v[h]) + dd[h]) * p[h] for h in H]
        qi = [q[h] * inter[h] for h in H]
        kw = [k[h] * kscale[h] for h in H]
        ghc = [_dot_nn(gh[h], ct[h]) for h in H]
        vet = [_dot_nn(v[h], et[h]) for h in H]
        dq = [_dot_nn(dsqk[h], k[h]) + inter[h] * (ghc[h] + dd[h] * n[h]) for h in H]
        dk = [_dot_tn(dsqk[h], q[h]) + kscale[h] * (vet[h] + en[h]) for h in H]
        dv = [_dot_tn(sc[h], gh[h]) + _dot_nt(kw[h], et[h]) for h in H]
        e_new = [cs[h] * et[h] + _dot_tn(gh[h], qi[h]) for h in H]
        en_new = [cs[h] * en[h] + jnp.sum(qi[h] * dd[h], axis=0, keepdims=True) for h in H]
        di_col = [jnp.sum(k[h] * dk[h], axis=1, keepdims=True) for h in H]
        df_col = [jnp.sum(q[h] * dq[h], axis=1, keepdims=True) - di_col[h] for h in H]
        di_row = [jnp.sum(jnp.where(eye, di_col[h], 0.0), axis=0, keepdims=True) for h in H]
        dlf_row = [jnp.sum(jnp.where(cm, df_col[h], 0.0), axis=0, keepdims=True) + carry[h] for h in H]
        dxf_row = [dlf_row[h] * (1.0 - _sigmoid(xf_row[h])) for h in H]
        for h in H:
            sl = sls[h]
            slk = slice(wm + h * dm, wm + (h + 1) * dm)
            e_ref[h] = e_new[h]
            en_ref[h] = en_new[h]
            fs_ref[h] = jnp.broadcast_to(carry[h] + jnp.sum(df_col[h], axis=0, keepdims=True), (1, LANES))
            dgr_ref[0, h:h + 1, :] = di_row[h]
            dgr_ref[0, nhm + h:nhm + h + 1, :] = dxf_row[h]
            dgs_ref[h:h + 1, :] += jnp.broadcast_to(jnp.sum(di_row[h], axis=1, keepdims=True), (1, LANES))
            dgs_ref[nhm + h:nhm + h + 1, :] += jnp.broadcast_to(
                jnp.sum(dxf_row[h], axis=1, keepdims=True), (1, LANES))
            dqh, dkh, dvh = dq[h] * (dm ** -0.5), dk[h], dv[h]
            if has_prev:
                dqh = dqh + refs[11][:, sl]
                dkh = dkh + refs[11][:, slk]
                dvh = dvh + refs[12][:, sl]
            dqk_ref[:, sl] = dqh
            dqk_ref[:, slk] = dkh
            dv_ref[:, sl] = dvh

    in_specs = [row(wm, 0), row(wm, 1), row(wm, vcol), row(LANES, 0), gates_row, bias_row, bias_col,
                row(wm, 0), st_c, st_n, st_m]
    args = [qk, qk, z, gc, gr, br, bcl, dh] + list(states)
    if has_prev:
        in_specs += [row(2 * wm, 0), row(wm, 0)]
        args += list(prev)
    return _pcall(
        body, name=name, comm=comm,
        out_shape=[jax.ShapeDtypeStruct((T, 2 * wm), F32), jax.ShapeDtypeStruct((T, wm), F32),
                   jax.ShapeDtypeStruct((nch, 2 * nhm, CHUNK), F32),
                   jax.ShapeDtypeStruct((2 * nhm, LANES), F32)],
        grid=(nch,), in_specs=in_specs,
        out_specs=[row(2 * wm, 0), row(wm, 0),
                   pl.BlockSpec((1, 2 * nhm, CHUNK), lambda i: (ci(i), 0, 0)),
                   pl.BlockSpec((2 * nhm, LANES), lambda i: (0, 0))],
        scratch_shapes=[pltpu.VMEM((nhm, dm, dm), F32), pltpu.VMEM((nhm, 1, dm), F32),
                        pltpu.VMEM((nhm, 1, LANES), F32)],
        sem=("arbitrary",),
    )(*args)


def _conv_taps(x, w_ref):
    T = x.shape[0]
    t = lax.broadcasted_iota(jnp.int32, x.shape, 0)
    taps = []
    acc = None
    for j in range(CONV_WIDTH):
        s = CONV_WIDTH // 2 - j
        if s == 0:
            xs = x
        else:
            xs = jnp.where((t - s >= 0) & (t - s < T), pltpu.roll(x, s % T, 0), 0.0)
        taps.append(xs)
        term = w_ref[j:j + 1, :] * xs
        acc = term if acc is None else acc + term
    return taps, acc


def _conv_fwd(z, w, b, *, col0, tc=LANES, name):
    T = z.shape[0]
    C2 = w.shape[1]
    assert col0 % tc == 0 and C2 % tc == 0

    def body(z_ref, w_ref, b_ref, o_ref):
        _, acc = _conv_taps(z_ref[...], w_ref)
        c = acc + b_ref[...]
        o_ref[...] = c * _sigmoid(c)

    return pl.pallas_call(
        body, name=name, out_shape=jax.ShapeDtypeStruct((T, C2), F32),
        grid=(C2 // tc,),
        in_specs=[pl.BlockSpec((T, tc), lambda j: (0, col0 // tc + j)),
                  pl.BlockSpec((CONV_WIDTH, tc), lambda j: (0, j)),
                  pl.BlockSpec((1, tc), lambda j: (0, j))],
        out_specs=pl.BlockSpec((T, tc), lambda j: (0, j)),
        compiler_params=_params("parallel"),
    )(z, w, b)


def _conv_bwd(dy, z, w, b, *, col0, tc=LANES, name):
    T = z.shape[0]
    C2 = w.shape[1]

    def body(dy_ref, z_ref, w_ref, b_ref, dx_ref, dw_ref, db_ref):
        taps, acc = _conv_taps(z_ref[...], w_ref)
        c = acc + b_ref[...]
        sg = _sigmoid(c)
        dc = dy_ref[...] * sg * (1.0 + c * (1.0 - sg))
        t = lax.broadcasted_iota(jnp.int32, dc.shape, 0)
        dx = None
        for j in range(CONV_WIDTH):
            s = j - CONV_WIDTH // 2
            if s == 0:
                ds = dc
            else:
                ds = jnp.where((t - s >= 0) & (t - s < T), pltpu.roll(dc, s % T, 0), 0.0)
            term = w_ref[j:j + 1, :] * ds
            dx = term if dx is None else dx + term
            dw_ref[j:j + 1, :] = jnp.sum(dc * taps[j], axis=0, keepdims=True)
        dx_ref[...] = dx
        db_ref[...] = jnp.sum(dc, axis=0, keepdims=True)

    col = pl.BlockSpec((T, tc), lambda j: (0, j))
    wsp = pl.BlockSpec((CONV_WIDTH, tc), lambda j: (0, j))
    bsp = pl.BlockSpec((1, tc), lambda j: (0, j))
    return pl.pallas_call(
        body, name=name,
        out_shape=[jax.ShapeDtypeStruct((T, C2), F32), jax.ShapeDtypeStruct((CONV_WIDTH, C2), F32),
                   jax.ShapeDtypeStruct((1, C2), F32)],
        grid=(C2 // tc,),
        in_specs=[col, pl.BlockSpec((T, tc), lambda j: (0, col0 // tc + j)), wsp, bsp],
        out_specs=[col, wsp, bsp],
        compiler_params=_params("parallel"),
    )(dy, z, w, b)


def _mix_heads(o_fw, o_bw, h_fw, h_bw, wh, wm, nhm):
    out = []
    hd = HGRN_HEAD_DIM
    for h in range(wh // hd):
        sl = slice(h * hd, (h + 1) * hd)
        o = o_fw[:, sl] + o_bw[:, sl]
        r = lax.rsqrt(jnp.mean(o * o, axis=-1, keepdims=True) + NORM_EPS)
        out.append((0, sl, o * r, r))
    dm = wm // nhm
    for h in range(nhm):
        sl = slice(h * dm, (h + 1) * dm)
        x = h_fw[:, sl] + h_bw[:, sl]
        xc = x - jnp.mean(x, axis=-1, keepdims=True)
        r = lax.rsqrt(jnp.mean(xc * xc, axis=-1, keepdims=True) + NORM_EPS)
        out.append((1, sl, xc * r, r))
    return out


def _mix_specs(T, wh, wm, tm, gcol, ocol):
    rowh = pl.BlockSpec((tm, wh), lambda i: (i, 0))
    rowm = pl.BlockSpec((tm, wm), lambda i: (i, 0))
    hg = pl.BlockSpec((tm, wh), lambda i: (i, gcol))
    mo = pl.BlockSpec((tm, wm), lambda i: (i, ocol))
    vh = pl.BlockSpec((1, wh), lambda i: (0, 0))
    vm = pl.BlockSpec((1, wm), lambda i: (0, 0))
    return rowh, rowm, hg, mo, vh, vm


def _mix_fwd(o_fw, o_bw, h_fw, h_bw, z, gh, gm, *, nhm, gcol, ocol, tm=256, name):
    T, wh = o_fw.shape
    wm = h_fw.shape[1]
    tm = min(tm, T)
    rowh, rowm, hg, mo, vh, vm = _mix_specs(T, wh, wm, tm, gcol, ocol)

    def body(of_ref, ob_ref, hf_ref, hb_ref, hg_ref, mo_ref, gh_ref, gm_ref, y_ref, yt_ref):
        heads = _mix_heads(of_ref[...], ob_ref[...], hf_ref[...], hb_ref[...], wh, wm, nhm)
        for grp, sl, nv, _ in heads:
            if grp == 0:
                gate = hg_ref[:, sl]
                gate = gate * _sigmoid(gate)
                y = nv * gh_ref[:, sl] * gate
                osl = sl
            else:
                y = nv * gm_ref[:, sl] * _sigmoid(mo_ref[:, sl])
                osl = slice(wh + sl.start, wh + sl.stop)
            y_ref[:, osl] = y.astype(BF16)
            yt_ref[osl, :] = y.T.astype(BF16)

    return pl.pallas_call(
        body, name=name,
        out_shape=[jax.ShapeDtypeStruct((T, wh + wm), BF16), jax.ShapeDtypeStruct((wh + wm, T), BF16)],
        grid=(T // tm,),
        in_specs=[rowh, rowh, rowm, rowm, hg, mo, vh, vm],
        out_specs=[pl.BlockSpec((tm, wh + wm), lambda i: (i, 0)),
                   pl.BlockSpec((wh + wm, tm), lambda i: (0, i))],
        compiler_params=_params("parallel"),
    )(o_fw, o_bw, h_fw, h_bw, z, z, gh, gm)


def _mix_bwd(dy, o_fw, o_bw, h_fw, h_bw, z, gh, gm, *, nhm, gcol, ocol, tm=256, name):
    T, wh = o_fw.shape
    wm = h_fw.shape[1]
    tm = min(tm, T)
    rowh, rowm, hg, mo, vh, vm = _mix_specs(T, wh, wm, tm, gcol, ocol)

    def body(dy_ref, of_ref, ob_ref, hf_ref, hb_ref, hg_ref, mo_ref, gh_ref, gm_ref,
             do_ref, dh_ref, dhg_ref, dmo_ref, dgh_ref, dgm_ref):
        @pl.when(pl.program_id(0) == 0)
        def _():
            dgh_ref[...] = jnp.zeros_like(dgh_ref)
            dgm_ref[...] = jnp.zeros_like(dgm_ref)

        heads = _mix_heads(of_ref[...], ob_ref[...], hf_ref[...], hb_ref[...], wh, wm, nhm)
        for grp, sl, nv, r in heads:
            if grp == 0:
                d = dy_ref[:, sl]
                x = hg_ref[:, sl]
                sg = _sigmoid(x)
                gate = x * sg
                g = gh_ref[:, sl]
                dgh_ref[:, sl] += jnp.sum(d * nv * gate, axis=0, keepdims=True)
                dhg_ref[:, sl] = d * nv * g * sg * (1.0 + x * (1.0 - sg))
                t = d * g * gate
                do_ref[:, sl] = r * (t - nv * jnp.mean(t * nv, axis=-1, keepdims=True))
            else:
                d = dy_ref[:, slice(wh + sl.start, wh + sl.stop)]
                sg = _sigmoid(mo_ref[:, sl])
                g = gm_ref[:, sl]
                dgm_ref[:, sl] += jnp.sum(d * nv * sg, axis=0, keepdims=True)
                dmo_ref[:, sl] = d * nv * g * sg * (1.0 - sg)
                t = d * g * sg
                dh_ref[:, sl] = r * (t - jnp.mean(t, axis=-1, keepdims=True)
                                     - nv * jnp.mean(t * nv, axis=-1, keepdims=True))

    return pl.pallas_call(
        body, name=name,
        out_shape=[jax.ShapeDtypeStruct((T, wh), F32), jax.ShapeDtypeStruct((T, wm), F32),
                   jax.ShapeDtypeStruct((T, wh), F32), jax.ShapeDtypeStruct((T, wm), F32),
                   jax.ShapeDtypeStruct((1, wh), F32), jax.ShapeDtypeStruct((1, wm), F32)],
        grid=(T // tm,),
        in_specs=[pl.BlockSpec((tm, wh + wm), lambda i: (i, 0)), rowh, rowh, rowm, rowm, hg, mo, vh, vm],
        out_specs=[rowh, rowm, rowh, rowm, vh, vm],
        compiler_params=_params("arbitrary"),
    )(dy, o_fw, o_bw, h_fw, h_bw, z, z, gh, gm)


def _pick(n, pref):
    for c in range(pref - pref % LANES, 0, -LANES):
        if n % c == 0:
            return c
    return n


def _mm(a, b, mode="nn", **kw):
    if mode == "nn":
        (M, K), N = a.shape, b.shape[1]
    elif mode == "nt":
        (M, K), N = a.shape, b.shape[0]
    else:
        (K, M), N = a.shape, b.shape[1]
    tk = _pick(K, 2048) if mode != "tn" else _pick(K, 512)
    return _matmul(a, b, mode=mode, tm=_pick(M, 1024), tn=_pick(N, 1024), tk=tk, **kw)


def _device_step(x, target, sh, small_parts, small_params, core):
    T, D = x.shape

    def pair_add(name, g, land):
        return _pair_add(g, land, core, "rs_add_" + name)

    g_w1a, g_w3a, g_small = _comm_call(_ag_job([sh["ffn1_w1"], sh["ffn1_w3"], small_parts]), "ag_ffn1")
    P = small_params(g_small)
    wh = P["hgrn_norm_g"].shape[1]
    wm = P["mlstm_norm_g"].shape[1]
    nhm = P["ig_b"].shape[1]
    ng = 4 * nhm
    nch = T // CHUNK
    assert wh == wm and T % CHUNK == 0
    vcol, ocol, gcol = 7, 8, 2
    col0 = 5 * wh
    xb, xt = _cast_t(x, name="x_cast")
    (a1, b1, h1), (g_w2a, g_wout) = _ffn_up(xb, g_w1a, g_w3a, name="ffn1_up",
                                            comm=_ag_job([sh["ffn1_w2"], sh["w_out"]]))
    r1, (g_win,) = _ffn_down(h1, g_w2a, x, name="ffn1_down", comm=_ag_job([sh["w_in"]]))
    win_full = g_win.transpose(1, 0, 2).reshape(D, -1)
    ncols = win_full.shape[1]
    w_win = win_full[:, :ncols - ng]
    w_wg = jnp.zeros((D, LANES), BF16).at[:, :ng].set(win_full[:, ncols - ng:])
    w_wout = g_wout.reshape(D, D)
    x1, x1b, x1t = _ln_fwd(r1, P["ln1_g"], P["ln1_b"], name="ln1")
    zm = _mm(x1b, w_win, name="zm")
    zg = _mm(x1b, w_wg, name="zg")
    gr = zg[:, :ng].reshape(nch, CHUNK, ng).transpose(0, 2, 1)
    bias = jnp.concatenate([P["ig_b"].reshape(-1), P["fg_b"].reshape(-1)])
    br = jnp.zeros((1, LANES), F32).at[0, :ng].set(bias)
    bcl = bias.reshape(ng, 1)
    lbf, lbb = P["lb"][0:1], P["lb"][1:2]
    (o_fw, s_fw), (g_w1b,) = _hgrn_fwd(zm, lbf, wh=wh, reverse=False, name="hgrn_fw",
                                       comm=_ag_job([sh["ffn2_w1"]]))
    (o_bw, s_bw), (g_w3b,) = _hgrn_fwd(zm, lbb, wh=wh, reverse=True, name="hgrn_bw",
                                       comm=_ag_job([sh["ffn2_w3"]]))
    qk = _conv_fwd(zm, P["conv_w"], P["conv_b"], col0=col0, name="conv")
    (h_fw, *st_fw), (g_w2b,) = _mlstm_fwd(qk, zm, zg, gr, br, bcl, wm=wm, nhm=nhm, vcol=vcol,
                                          reverse=False, name="mlstm_fw", comm=_ag_job([sh["ffn2_w2"]]))
    h_bw, *st_bw = _mlstm_fwd(qk, zm, zg, gr, br, bcl, wm=wm, nhm=nhm, vcol=vcol, reverse=True,
                              name="mlstm_bw")
    y, yt = _mix_fwd(o_fw, o_bw, h_fw, h_bw, zm, P["hgrn_norm_g"], P["mlstm_norm_g"],
                     nhm=nhm, gcol=gcol, ocol=ocol, name="mix")
    r2 = _mm(y, w_wout, add=x1, add_scale=DN_ALPHA, name="r2")
    x2, x2b, x2t = _ln_fwd(r2, P["ln2_g"], P["ln2_b"], name="ln2")
    a2, b2, h2 = _ffn_up(x2b, g_w1b, g_w3b, name="ffn2_up")
    r3 = _ffn_down(h2, g_w2b, x2, name="ffn2_down")

    G = {}
    dr3, dr3b, dr3t, G["ln3_g"], G["ln3_b"], loss = _ln_bwd(
        None, r3, P["ln3_g"], b=P["ln3_b"], target=target, name="ln3_bwd")
    land = {}
    da2, db2 = _ffn_bwd_act(dr3b, g_w2b, a2, b2, name="ffn2_dact")
    gw1b, gw3b = _ffn_dw(x2t, [da2, db2], name="ffn2_dw13")
    (gw2bt,) = _ffn_dw(dr3t, [h2], scale=0.5, name="ffn2_dw2t")
    dx2, l1 = _ffn_dx(da2, db2, g_w1b, g_w3b, dr3, name="ffn2_dx", comm=_pair_job([gw1b, gw3b, gw2bt]))
    p_w1b, p_w3b, p_w2bt = [pair_add(k, g, l) for k, g, l in
                            zip(("ffn2_w1", "ffn2_w3", "ffn2_w2"), (gw1b, gw3b, gw2bt), l1)]
    dr2, dr2b, _, G["ln2_g"], G["ln2_b"] = _ln_bwd(dx2, r2, P["ln2_g"], name="ln2_bwd")
    dy = _mm(dr2b, w_wout, "nt", name="dy")
    gwout = _mm(yt, dr2b, out_dtype=BF16, name="dwout").reshape(N_DEV, -1, D)
    do, dh, dhg, dmo, G["hgrn_norm_g"], G["mlstm_norm_g"] = _mix_bwd(
        dy, o_fw, o_bw, h_fw, h_bw, zm, P["hgrn_norm_g"], P["mlstm_norm_g"],
        nhm=nhm, gcol=gcol, ocol=ocol, name="mix_bwd")
    (dqk_f, dv_f, dgr_f, dgs_f), (land["ffn2_w1"], land["ffn2_w3"]) = _mlstm_bwd(
        qk, zm, zg, gr, br, bcl, dh, st_fw, None, wm=wm, nhm=nhm, vcol=vcol, reverse=False,
        name="mlstm_fw_bwd", comm=_chips_job([p_w1b, p_w3b]))
    (dqk, dv, dgr_b, dgs_b), (land["ffn2_w2"],) = _mlstm_bwd(
        qk, zm, zg, gr, br, bcl, dh, st_bw, (dqk_f, dv_f), wm=wm, nhm=nhm, vcol=vcol, reverse=True,
        name="mlstm_bw_bwd", comm=_chips_job([p_w2bt]))
    dmqk, G["conv_w"], G["conv_b"] = _conv_bwd(dqk, zm, P["conv_w"], P["conv_b"], col0=col0,
                                               name="conv_bwd")
    dhq_f, dhi_f, dhf_fw, dlb_f = _hgrn_bwd(zm, lbf, do, s_fw, None, wh=wh, reverse=False,
                                            name="hgrn_fw_bwd")
    dhq, dhi, dhf_bw, dlb_b = _hgrn_bwd(zm, lbb, do, s_bw, (dhq_f, dhi_f), wh=wh, reverse=True,
                                        name="hgrn_bw_bwd")
    G["lb"] = jnp.concatenate([dlb_f, dlb_b], axis=0)
    G["ig_b"] = jnp.stack([dgs_f[:nhm, 0], dgs_b[:nhm, 0]])
    G["fg_b"] = jnp.stack([dgs_f[nhm:, 0], dgs_b[nhm:, 0]])
    dzm = jnp.concatenate([dhq, dhi, dhg, dhf_fw, dhf_bw, dmqk, dv, dmo], axis=1).astype(BF16)
    dgr = jnp.concatenate([dgr_f[:, :nhm], dgr_b[:, :nhm], dgr_f[:, nhm:], dgr_b[:, nhm:]], axis=1)
    dzg = jnp.zeros((T, LANES), F32).at[:, :ng].set(dgr.transpose(0, 2, 1).reshape(T, ng)).astype(BF16)
    dwin = jnp.concatenate([_mm(x1t, dzm, out_dtype=BF16, name="dwin"),
                            _mm(x1t, dzg, out_dtype=BF16, name="dwg")[:, :ng]], axis=1)
    gwin = dwin.reshape(D, N_DEV, -1).transpose(1, 0, 2)
    t = _mm(dzg, w_wg, "nt", add=dr2, add_scale=DN_ALPHA, name="dx1_g")
    dx1, l1 = _mm(dzm, w_win, "nt", add=t, name="dx1", comm=_pair_job([gwout, gwin]))
    p_wout, p_win = pair_add("w_out", gwout, l1[0]), pair_add("w_in", gwin, l1[1])
    dr1, dr1b, dr1t, G["ln1_g"], G["ln1_b"] = _ln_bwd(dx1, r1, P["ln1_g"], name="ln1_bwd")
    (da1, db1), (land["w_in"],) = _ffn_bwd_act(dr1b, g_w2a, a1, b1, name="ffn1_dact",
                                               comm=_chips_job([p_win]))
    (gw1a, gw3a), (land["w_out"],) = _ffn_dw(xt, [da1, db1], name="ffn1_dw13", comm=_chips_job([p_wout]))
    (gw2at,), l1 = _ffn_dw(dr1t, [h1], scale=0.5, name="ffn1_dw2t", comm=_pair_job([gw1a, gw3a]))
    p_w1a, p_w3a = pair_add("ffn1_w1", gw1a, l1[0]), pair_add("ffn1_w3", gw3a, l1[1])
    gx, (land["ffn1_w1"], land["ffn1_w3"], l1) = _ffn_dx(
        da1, db1, g_w1a, g_w3a, dr1, name="ffn1_dx",
        comm=_join(_chips_job([p_w1a, p_w3a]), _pair_job([gw2at])))
    (land["ffn1_w2"],) = _comm_call(_chips_job([pair_add("ffn1_w2", gw2at, l1)]), "rs_ffn1_w2")
    return loss, gx, land, G


ANY = pl.BlockSpec(memory_space=pl.ANY)


def _place():
    x, y, c = lax.axis_index("x"), lax.axis_index("y"), lax.axis_index("c")
    chips = [(1 - x, y), (x, 1 - y), (1 - x, 1 - y)]
    return x, y, c, chips


def _comm_call(job, name):
    n_in, n_out = len(job.inputs), len(job.out_shape)

    def body(*refs):
        ins, outs, sems = refs[:n_in], refs[n_in:n_in + n_out], refs[n_in + n_out:]
        job.start(ins, outs, sems)
        job.finish(ins, outs, sems)

    return pl.pallas_call(body, name=name, out_shape=job.out_shape, in_specs=[ANY] * n_in,
                          out_specs=[ANY] * n_out, scratch_shapes=job.sems)(*job.inputs)


def _ag_job(parts):
    n = len(parts)

    def copies(ins, outs, sems):
        send_sems, recv_sems, local_sems = sems
        x, y, c, chips = _place()
        me, sibling = (x, y, c), (x, y, 1 - c)

        def copy(a, k, block, to, src=None):
            bx, by, bc = block
            slot = outs[a].at[4 * bx + 2 * by + bc]
            return pltpu.make_async_remote_copy(
                src_ref=slot if src is None else src, dst_ref=slot,
                send_sem=send_sems.at[a, k], recv_sem=recv_sems.at[a, k],
                device_id=to, device_id_type=MESH)

        local = [pltpu.make_async_copy(ins[a], outs[a].at[4 * x + 2 * y + c], local_sems.at[a])
                 for a in range(n)]
        first = []
        for a in range(n):
            first.append(copy(a, 0, me, sibling, src=ins[a]))
            first += [copy(a, 1 + j, me, (*chip, c), src=ins[a]) for j, chip in enumerate(chips)]
        return copy, local, first, chips, me, sibling, c

    def start(ins, outs, sems):
        _, local, first, *_ = copies(ins, outs, sems)
        for cp in local + first:
            cp.start()

    def finish(ins, outs, sems):
        copy, local, first, chips, me, sibling, c = copies(ins, outs, sems)
        passed = []
        for a in range(n):
            for j, chip in enumerate(chips):
                copy(a, 1 + j, (*chip, c), me).wait_recv()
                fwd = copy(a, 4 + j, (*chip, c), sibling)
                fwd.start()
                passed.append(fwd)
        for a in range(n):
            copy(a, 0, sibling, me).wait_recv()
            for j, chip in enumerate(chips):
                copy(a, 4 + j, (*chip, 1 - c), me).wait_recv()
        for cp in first + passed:
            cp.wait_send()
        for cp in local:
            cp.wait()

    return _Job(parts, [jax.ShapeDtypeStruct((N_DEV,) + p.shape, p.dtype) for p in parts],
                [pltpu.SemaphoreType.DMA((n, 7)), pltpu.SemaphoreType.DMA((n, 7)),
                 pltpu.SemaphoreType.DMA((n,))], start, finish)


def _pair_job(gs):
    n = len(gs)

    def copies(g_refs, land_refs, sems):
        send_sems, recv_sems = sems
        x, y, c, _ = _place()
        return [pltpu.make_async_remote_copy(
            src_ref=g_refs[a].at[2 * k + 1 - c], dst_ref=land_refs[a].at[k],
            send_sem=send_sems.at[a, k], recv_sem=recv_sems.at[a, k],
            device_id=(x, y, 1 - c), device_id_type=MESH) for a in range(n) for k in range(4)]

    def start(ins, outs, sems):
        for cp in copies(ins, outs, sems):
            cp.start()

    def finish(ins, outs, sems):
        for cp in copies(ins, outs, sems):
            cp.wait()

    return _Job(gs, [jax.ShapeDtypeStruct((4,) + g.shape[1:], g.dtype) for g in gs],
                [pltpu.SemaphoreType.DMA((n, 4)), pltpu.SemaphoreType.DMA((n, 4))], start, finish)


def _chips_job(ps):
    n = len(ps)

    def copies(p_refs, land_refs, sems):
        send_sems, recv_sems, local_sems = sems
        x, y, c, chips = _place()
        mine = 2 * x + y
        owns = [pltpu.make_async_copy(p_refs[a].at[mine], land_refs[a].at[mine], local_sems.at[a])
                for a in range(n)]
        sends = [pltpu.make_async_remote_copy(
            src_ref=p_refs[a].at[2 * chip[0] + chip[1]], dst_ref=land_refs[a].at[mine],
            send_sem=send_sems.at[a, j], recv_sem=recv_sems.at[a, j],
            device_id=(*chip, c), device_id_type=MESH) for a in range(n) for j, chip in enumerate(chips)]
        recvs = [pltpu.make_async_remote_copy(
            src_ref=p_refs[a].at[mine], dst_ref=land_refs[a].at[2 * chip[0] + chip[1]],
            send_sem=send_sems.at[a, j], recv_sem=recv_sems.at[a, j],
            device_id=(*chip, c), device_id_type=MESH) for a in range(n) for j, chip in enumerate(chips)]
        return owns, sends, recvs

    def start(ins, outs, sems):
        owns, sends, _ = copies(ins, outs, sems)
        for cp in owns + sends:
            cp.start()

    def finish(ins, outs, sems):
        owns, sends, recvs = copies(ins, outs, sems)
        for cp in recvs:
            cp.wait_recv()
        for cp in sends:
            cp.wait_send()
        for cp in owns:
            cp.wait()

    return _Job(ps, [jax.ShapeDtypeStruct(p.shape, p.dtype) for p in ps],
                [pltpu.SemaphoreType.DMA((n, 3)), pltpu.SemaphoreType.DMA((n, 3)),
                 pltpu.SemaphoreType.DMA((n,))], start, finish)


def _all_reduce_small(buf, name):
    rows = buf.shape[0]

    def body(b_ref, o_ref, slots, send_sems, recv_sems):
        x, y, c, _ = _place()
        me = 4 * x + 2 * y + c
        slots[me] = b_ref[...]
        cps = []
        for k in range(1, N_DEV):
            fx, fy, fc = (k >> 2) & 1, (k >> 1) & 1, k & 1
            peer = (x ^ fx, y ^ fy, c ^ fc)
            cps.append(pltpu.make_async_remote_copy(
                src_ref=b_ref, dst_ref=slots.at[me],
                send_sem=send_sems.at[k - 1], recv_sem=recv_sems.at[k - 1],
                device_id=peer, device_id_type=MESH))
        for cp in cps:
            cp.start()
        for k in range(1, N_DEV):
            fx, fy, fc = (k >> 2) & 1, (k >> 1) & 1, k & 1
            src = 4 * (x ^ fx) + 2 * (y ^ fy) + (c ^ fc)
            pltpu.make_async_remote_copy(
                src_ref=b_ref, dst_ref=slots.at[src],
                send_sem=send_sems.at[k - 1], recv_sem=recv_sems.at[k - 1],
                device_id=(x ^ fx, y ^ fy, c ^ fc), device_id_type=MESH).wait_recv()
        for cp in cps:
            cp.wait_send()
        acc = slots[0]
        for k in range(1, N_DEV):
            acc = acc + slots[k]
        o_ref[...] = acc

    vm = pl.BlockSpec(memory_space=pltpu.VMEM)
    return pl.pallas_call(
        body, name=name, out_shape=jax.ShapeDtypeStruct(buf.shape, F32),
        in_specs=[vm], out_specs=vm,
        scratch_shapes=[pltpu.VMEM((N_DEV, rows, LANES), F32),
                        pltpu.SemaphoreType.DMA((N_DEV - 1,)), pltpu.SemaphoreType.DMA((N_DEV - 1,))],
    )(buf)


def _row_tile(rows, pref=2048):
    for t in range(min(pref, rows) - min(pref, rows) % 8, 0, -8):
        if rows % t == 0:
            return t
    return rows


def _tile_rows(rows, cols, nbuf):
    budget = VMEM_LIMIT // 2 // (2 * nbuf * 4 * cols)
    return _row_tile(rows, max(8, min(512, budget // 8 * 8)))


def _pair_add(g, land, core, name):
    _, r, c = g.shape
    tr = _tile_rows(r, c, 3)

    def body(core_ref, g_ref, l_ref, o_ref):
        o_ref[...] = (g_ref[...].astype(F32) + l_ref[...].astype(F32)).astype(o_ref.dtype)

    blk = pl.BlockSpec((1, tr, c), lambda k, i, cr: (k, i, 0))
    return pl.pallas_call(
        body, name=name, out_shape=jax.ShapeDtypeStruct(land.shape, land.dtype),
        grid_spec=pltpu.PrefetchScalarGridSpec(
            num_scalar_prefetch=1, grid=(4, r // tr),
            in_specs=[pl.BlockSpec((1, tr, c), lambda k, i, cr: (2 * k + cr[0], i, 0)), blk],
            out_specs=blk),
        compiler_params=_params("parallel", "parallel"),
    )(core, g, land)


def _sum4(land, name):
    _, r, c = land.shape
    tr = _tile_rows(r, c, 5)

    def body(a_ref, b_ref, c_ref, d_ref, o_ref):
        o_ref[...] = ((a_ref[0].astype(F32) + b_ref[0].astype(F32)) + c_ref[0].astype(F32)
                      + d_ref[0].astype(F32))

    slab = lambda k: pl.BlockSpec((1, tr, c), lambda i: (k, i, 0))
    return pl.pallas_call(
        body, name=name, out_shape=jax.ShapeDtypeStruct((r, c), F32),
        grid=(r // tr,), in_specs=[slab(0), slab(1), slab(2), slab(3)],
        out_specs=pl.BlockSpec((tr, c), lambda i: (i, 0)),
        compiler_params=_params("parallel"),
    )(land, land, land, land)


def _adamw(w, parts, m, v, name):
    shp = w.shape
    cols = shp[-1]
    w2, m2, v2 = (t.reshape(-1, cols) for t in (w, m, v))
    rows = w2.shape[0]
    n = parts.shape[0]
    assert parts.shape[1:] == (rows, cols), (parts.shape, shp)
    tr = _tile_rows(rows, cols, n + 7)
    c1 = 1.0 / (1.0 - ADAM_B1 ** ADAM_STEP)
    c2 = 1.0 / (1.0 - ADAM_B2 ** ADAM_STEP)

    def body(*refs):
        p_refs = refs[:n]
        w_ref, m_ref, v_ref, g_ref, d_ref, nm_ref, nv_ref = refs[n:]
        gv = p_refs[0][0].astype(F32)
        for p_ref in p_refs[1:]:
            gv = gv + p_ref[0].astype(F32)
        nm = ADAM_B1 * m_ref[...] + (1.0 - ADAM_B1) * gv
        nv = ADAM_B2 * v_ref[...] + (1.0 - ADAM_B2) * (gv * gv)
        g_ref[...] = gv
        nm_ref[...] = nm
        nv_ref[...] = nv
        d_ref[...] = -ADAM_LR * ((nm * c1) / (jnp.sqrt(nv * c2) + ADAM_EPS) + ADAM_WD * w_ref[...])

    blk = pl.BlockSpec((tr, cols), lambda i: (i, 0))
    slab = lambda k: pl.BlockSpec((1, tr, cols), lambda i: (k, i, 0))
    out = jax.ShapeDtypeStruct((rows, cols), F32)
    outs = pl.pallas_call(
        body, name=name, out_shape=[out] * 4, grid=(rows // tr,),
        in_specs=[slab(k) for k in range(n)] + [blk] * 3, out_specs=[blk] * 4,
        compiler_params=_params("parallel"),
    )(*([parts] * n), w2, m2, v2)
    return tuple(o.reshape(shp) for o in outs)


def _pack_rows(arrs):
    rows = []
    for a in arrs:
        flat = a.reshape(-1).astype(F32)
        pad = (-flat.shape[0]) % LANES
        if pad:
            flat = jnp.concatenate([flat, jnp.zeros((pad,), F32)])
        rows.append(flat.reshape(-1, LANES))
    out = jnp.concatenate(rows, axis=0)
    pad = (-out.shape[0]) % 8
    if pad:
        out = jnp.concatenate([out, jnp.zeros((pad, LANES), F32)], axis=0)
    return out


def _unpack_rows(buf, shapes):
    outs, r = [], 0
    for s in shapes:
        n = math.prod(s)
        nr = -(-n // LANES)
        outs.append(buf[r:r + nr].reshape(-1)[:n].reshape(s))
        r += nr
    return outs


BIG = ("ffn1_w1", "ffn1_w3", "ffn1_w2", "w_in", "w_out", "ffn2_w1", "ffn2_w3", "ffn2_w2")
ROW_SHARDED = ("ffn1_w2", "w_out", "ffn2_w2")
SMALL = ("ln1_g", "ln1_b", "hgrn_lb", "hgrn_norm_g", "mlstm_conv_w", "mlstm_conv_b", "mlstm_ig_b",
         "mlstm_fg_b", "mlstm_norm_g", "ln2_g", "ln2_b", "ln3_g", "ln3_b")
WEIGHTS = ("ffn1_w1", "ffn1_w3", "ffn1_w2", "ln1_g", "ln1_b", "w_in", "hgrn_lb", "hgrn_norm_g",
           "mlstm_conv_w", "mlstm_conv_b", "mlstm_ig_b", "mlstm_fg_b", "mlstm_norm_g", "w_out",
           "ln2_g", "ln2_b", "ffn2_w1", "ffn2_w3", "ffn2_w2", "ln3_g", "ln3_b")


def kernel(x, ffn1_w1, ffn1_w3, ffn1_w2, ln1_g, ln1_b, w_in, hgrn_lb, hgrn_norm_g, mlstm_conv_w, mlstm_conv_b, mlstm_ig_b, mlstm_fg_b, mlstm_norm_g, w_out, ln2_g, ln2_b, ffn2_w1, ffn2_w3, ffn2_w2, ln3_g, ln3_b, loss_target, m_ffn1_w1, m_ffn1_w3, m_ffn1_w2, m_ln1_g, m_ln1_b, m_w_in, m_hgrn_lb, m_hgrn_norm_g, m_mlstm_conv_w, m_mlstm_conv_b, m_mlstm_ig_b, m_mlstm_fg_b, m_mlstm_norm_g, m_w_out, m_ln2_g, m_ln2_b, m_ffn2_w1, m_ffn2_w3, m_ffn2_w2, m_ln3_g, m_ln3_b, v_ffn1_w1, v_ffn1_w3, v_ffn1_w2, v_ln1_g, v_ln1_b, v_w_in, v_hgrn_lb, v_hgrn_norm_g, v_mlstm_conv_w, v_mlstm_conv_b, v_mlstm_ig_b, v_mlstm_fg_b, v_mlstm_norm_g, v_w_out, v_ln2_g, v_ln2_b, v_ffn2_w1, v_ffn2_w3, v_ffn2_w2, v_ln3_g, v_ln3_b):
    args = (ffn1_w1, ffn1_w3, ffn1_w2, ln1_g, ln1_b, w_in, hgrn_lb, hgrn_norm_g, mlstm_conv_w,
            mlstm_conv_b, mlstm_ig_b, mlstm_fg_b, mlstm_norm_g, w_out, ln2_g, ln2_b, ffn2_w1, ffn2_w3,
            ffn2_w2, ln3_g, ln3_b)
    ms = (m_ffn1_w1, m_ffn1_w3, m_ffn1_w2, m_ln1_g, m_ln1_b, m_w_in, m_hgrn_lb, m_hgrn_norm_g,
          m_mlstm_conv_w, m_mlstm_conv_b, m_mlstm_ig_b, m_mlstm_fg_b, m_mlstm_norm_g, m_w_out, m_ln2_g,
          m_ln2_b, m_ffn2_w1, m_ffn2_w3, m_ffn2_w2, m_ln3_g, m_ln3_b)
    vs = (v_ffn1_w1, v_ffn1_w3, v_ffn1_w2, v_ln1_g, v_ln1_b, v_w_in, v_hgrn_lb, v_hgrn_norm_g,
          v_mlstm_conv_w, v_mlstm_conv_b, v_mlstm_ig_b, v_mlstm_fg_b, v_mlstm_norm_g, v_w_out, v_ln2_g,
          v_ln2_b, v_ffn2_w1, v_ffn2_w3, v_ffn2_w2, v_ln3_g, v_ln3_b)
    w = dict(zip(WEIGHTS, args))
    m = dict(zip(WEIGHTS, ms))
    v = dict(zip(WEIGHTS, vs))
    core = lax.axis_index("c")
    dev = 4 * lax.axis_index("x") + 2 * lax.axis_index("y") + core
    p_lb = []

    def small_params(gathered_small):
        lb_sh, cw_sh = zip(*[_unpack_rows(gathered_small[i], [hgrn_lb.shape, mlstm_conv_w.shape])
                             for i in range(N_DEV)])
        hgrn_lb_full = jnp.concatenate(lb_sh, axis=-1)
        conv_w_full = jnp.concatenate(cw_sh, axis=-1)[0]
        p_lb.append(jax.nn.softmax(hgrn_lb_full, axis=1))
        return dict(ln1_g=ln1_g, ln1_b=ln1_b, ln2_g=ln2_g, ln2_b=ln2_b, ln3_g=ln3_g, ln3_b=ln3_b,
                    lb=p_lb[0][:, 0], hgrn_norm_g=hgrn_norm_g, conv_w=conv_w_full, conv_b=mlstm_conv_b,
                    ig_b=mlstm_ig_b[0], fg_b=mlstm_fg_b[0], mlstm_norm_g=mlstm_norm_g)

    loss_row, grad_x, land2, G = _device_step(
        x[0], loss_target[0], {k: w[k][0].astype(BF16) for k in BIG},
        _pack_rows([hgrn_lb, mlstm_conv_w]), small_params, jnp.reshape(core, (1,)).astype(jnp.int32))
    p_lb = p_lb[0]

    dlb = G["lb"]
    g_lb = jnp.stack([dlb * p_lb[:, 0] * (1.0 - p_lb[:, 0]), -dlb * p_lb[:, 0] * p_lb[:, 1]], axis=1)
    small_full = {"ln1_g": G["ln1_g"], "ln1_b": G["ln1_b"], "hgrn_lb": g_lb, "hgrn_norm_g": G["hgrn_norm_g"],
                  "mlstm_conv_w": G["conv_w"][None], "mlstm_conv_b": G["conv_b"],
                  "mlstm_ig_b": G["ig_b"][None], "mlstm_fg_b": G["fg_b"][None],
                  "mlstm_norm_g": G["mlstm_norm_g"], "ln2_g": G["ln2_g"], "ln2_b": G["ln2_b"],
                  "ln3_g": G["ln3_g"], "ln3_b": G["ln3_b"]}
    small_list = [small_full[k] for k in SMALL] + [loss_row]
    reduced = _all_reduce_small(_pack_rows(small_list), "all_reduce_small")
    red = _unpack_rows(reduced, [a.shape for a in small_list])
    loss = red[-1][0, 0]
    small_g = {}
    for k, gk in zip(SMALL, red[:-1]):
        if k in ("hgrn_lb", "mlstm_conv_w"):
            n = w[k].shape[-1]
            gk = lax.dynamic_slice_in_dim(gk, dev * n, n, axis=gk.ndim - 1)
        small_g[k] = gk

    grads, delta, new_m, new_v = {}, {}, {}, {}
    for k in BIG:
        contrib = land2[k]
        if k in ROW_SHARDED and k != "w_out":
            contrib = _sum4(contrib, "rs_sum_" + k).T[None]
        grads[k], delta[k], new_m[k], new_v[k] = _adamw(w[k], contrib, m[k], v[k], "adamw_" + k)
    sm = _adamw(_pack_rows([w[k] for k in SMALL]), _pack_rows([small_g[k] for k in SMALL])[None],
                _pack_rows([m[k] for k in SMALL]), _pack_rows([v[k] for k in SMALL]), "adamw_small")
    shapes = [w[k].shape for k in SMALL]
    for dst, buf in zip((grads, delta, new_m, new_v), sm):
        for k, val in zip(SMALL, _unpack_rows(buf, shapes)):
            dst[k] = val
    return (loss, grad_x[None], *[grads[k] for k in WEIGHTS], *[delta[k] for k in WEIGHTS],
            *[new_m[k] for k in WEIGHTS], *[new_v[k] for k in WEIGHTS])
```

```python
import math

import jax
import jax.numpy as jnp
from jax import lax
from jax.experimental import pallas as pl
from jax.experimental.pallas import tpu as pltpu

F32 = jnp.float32
BF16 = jnp.bfloat16

CHUNK = 64
HGRN_HEAD_DIM = 128
CONV_WIDTH = 5
DN_ALPHA = 2.0 ** 0.25
LN_EPS = 1e-5
NORM_EPS = 1e-6
M_INIT = -1e30
NEG = -1e30
EXP_CLAMP = 80.0
ADAM_LR = 0.001
ADAM_B1 = 0.9
ADAM_B2 = 0.999
ADAM_EPS = 1e-08
ADAM_WD = 0.01
ADAM_STEP = 10
N_DEV = 8
SHARDS_PER_STEP = 2
LANES = 128
VMEM_LIMIT = 56 * 1024 * 1024
MESH = pl.DeviceIdType.MESH


def _params(*sem):
    return pltpu.CompilerParams(dimension_semantics=sem, vmem_limit_bytes=VMEM_LIMIT)


class _Job:
    def __init__(self, inputs, out_shape, sems, start, finish):
        self.inputs, self.out_shape, self.sems = list(inputs), list(out_shape), list(sems)
        self.start, self.finish = start, finish


def _join(*jobs):
    def split(refs, counts):
        out, p = [], 0
        for c in counts:
            out.append(refs[p:p + c])
            p += c
        return out

    n_in = [len(j.inputs) for j in jobs]
    n_out = [len(j.out_shape) for j in jobs]
    n_sem = [len(j.sems) for j in jobs]

    def start(ins, outs, sems):
        for j, i, o, s in zip(jobs, split(ins, n_in), split(outs, n_out), split(sems, n_sem)):
            j.start(i, o, s)

    def finish(ins, outs, sems):
        for j, i, o, s in zip(jobs, split(ins, n_in), split(outs, n_out), split(sems, n_sem)):
            j.finish(i, o, s)

    return _Job(sum((j.inputs for j in jobs), []), sum((j.out_shape for j in jobs), []),
                sum((j.sems for j in jobs), []), start, finish)


def _pcall(body, *, name, out_shape, grid, in_specs, out_specs, sem, scratch_shapes=(), comm=None):
    single = not isinstance(out_shape, (list, tuple))
    out_shape = [out_shape] if single else list(out_shape)
    out_specs = [out_specs] if single else list(out_specs)
    in_specs, scratch_shapes = list(in_specs), list(scratch_shapes)
    if comm is None:
        call = pl.pallas_call(body, name=name, out_shape=out_shape, grid=grid, in_specs=in_specs,
                              out_specs=out_specs, scratch_shapes=scratch_shapes,
                              compiler_params=_params(*sem))

        def run(*args):
            outs = call(*args)
            return outs[0] if single else outs
        return run

    n_in, n_out, n_sc = len(in_specs), len(out_shape), len(scratch_shapes)
    c_in, c_out = len(comm.inputs), len(comm.out_shape)

    def hosted(*refs):
        ins, cins = refs[:n_in], refs[n_in:n_in + c_in]
        p = n_in + c_in
        outs, couts = refs[p:p + n_out], refs[p + n_out:p + n_out + c_out]
        p += n_out + c_out
        scratch, csems = refs[p:p + n_sc], refs[p + n_sc:]
        first = pl.program_id(0) == 0
        last = pl.program_id(0) == grid[0] - 1
        for d in range(1, len(grid)):
            first = first & (pl.program_id(d) == 0)
            last = last & (pl.program_id(d) == grid[d] - 1)

        @pl.when(first)
        def _():
            comm.start(cins, couts, csems)

        body(*ins, *outs, *scratch)

        @pl.when(last)
        def _():
            comm.finish(cins, couts, csems)

    any_spec = pl.BlockSpec(memory_space=pl.ANY)
    call = pl.pallas_call(
        hosted, name=name, out_shape=out_shape + comm.out_shape, grid=grid,
        in_specs=in_specs + [any_spec] * c_in, out_specs=out_specs + [any_spec] * c_out,
        scratch_shapes=scratch_shapes + comm.sems,
        compiler_params=_params(*(["arbitrary"] * len(grid))))

    def run(*args):
        res = call(*args, *comm.inputs)
        outs, couts = res[:n_out], list(res[n_out:])
        return (outs[0] if single else outs), couts
    return run


def _sigmoid(x):
    return 1.0 / (1.0 + jnp.exp(-x))


def _log_sigmoid(x):
    return jnp.minimum(x, 0.0) - jnp.log(1.0 + jnp.exp(-jnp.abs(x)))


def _dot(a, b, dims):
    return lax.dot_general(a.astype(BF16), b.astype(BF16), (dims, ((), ())),
                           preferred_element_type=F32)


def _dot3(a, b, dims):
    ah = a.astype(BF16)
    al = (a - ah.astype(F32)).astype(BF16)
    bh = b.astype(BF16)
    bl = (b - bh.astype(F32)).astype(BF16)
    d = (dims, ((), ()))
    out = lax.dot_general(ah, bh, d, preferred_element_type=F32)
    out = out + lax.dot_general(ah, bl, d, preferred_element_type=F32)
    return out + lax.dot_general(al, bh, d, preferred_element_type=F32)


_DIMS = {"nn": ((1,), (0,)), "nt": ((1,), (1,)), "tn": ((0,), (0,))}


def _dot_nn(a, b):
    return _dot3(a, b, _DIMS["nn"])


def _dot_nt(a, b):
    return _dot3(a, b, _DIMS["nt"])


def _dot_tn(a, b):
    return _dot3(a, b, _DIMS["tn"])


def _split3(x):
    hi = x.astype(BF16)
    r1 = x - hi.astype(F32)
    mid = r1.astype(BF16)
    lo = (r1 - mid.astype(F32)).astype(BF16)
    return hi, mid, lo


def _dot01(mask01, x, mode="nn"):
    m = mask01.astype(BF16)
    hi, mid, lo = _split3(x)
    d = (_DIMS[mode], ((), ()))
    out = lax.dot_general(m, hi, d, preferred_element_type=F32)
    out = out + lax.dot_general(m, mid, d, preferred_element_type=F32)
    return out + lax.dot_general(m, lo, d, preferred_element_type=F32)


def _matmul(a, b, *, mode="nn", out_dtype=F32, tm=512, tn=512, tk=None,
            add=None, scale=1.0, add_scale=1.0, name, comm=None):
    if mode == "nn":
        (M, K), (K2, N) = a.shape, b.shape
    elif mode == "nt":
        (M, K), (N, K2) = a.shape, b.shape
    else:
        (K, M), (K2, N) = a.shape, b.shape
    assert K == K2, (a.shape, b.shape, mode)
    tm, tn = min(tm, M), min(tn, N)
    tk = min(tk or K, K)
    assert M % tm == 0 and N % tn == 0 and K % tk == 0, (M, N, K, tm, tn, tk)
    nk = K // tk
    dims = _DIMS[mode]
    has_add = add is not None

    def body(*refs):
        if has_add:
            a_ref, b_ref, add_ref, o_ref = refs[:4]
        else:
            a_ref, b_ref, o_ref = refs[:3]
            add_ref = None
        acc_ref = refs[-1] if nk > 1 else None

        def finish(acc):
            out = acc if scale == 1.0 else acc * scale
            if has_add:
                out = out + add_ref[...].astype(F32) * add_scale
            o_ref[...] = out.astype(o_ref.dtype)

        if nk == 1:
            finish(_dot(a_ref[...], b_ref[...], dims))
        else:
            k = pl.program_id(2)

            @pl.when(k == 0)
            def _():
                acc_ref[...] = jnp.zeros_like(acc_ref)

            acc_ref[...] += _dot(a_ref[...], b_ref[...], dims)

            @pl.when(k == nk - 1)
            def _():
                finish(acc_ref[...])

    if mode == "tn":
        a_spec = pl.BlockSpec((tk, tm), lambda i, j, k: (k, i))
    else:
        a_spec = pl.BlockSpec((tm, tk), lambda i, j, k: (i, k))
    if mode == "nt":
        b_spec = pl.BlockSpec((tn, tk), lambda i, j, k: (j, k))
    else:
        b_spec = pl.BlockSpec((tk, tn), lambda i, j, k: (k, j))
    o_spec = pl.BlockSpec((tm, tn), lambda i, j, k: (i, j))
    in_specs = [a_spec, b_spec] + ([o_spec] if has_add else [])
    args = (a, b) + ((add,) if has_add else ())
    return _pcall(
        body, name=name, comm=comm,
        out_shape=jax.ShapeDtypeStruct((M, N), out_dtype),
        grid=(M // tm, N // tn, nk),
        in_specs=in_specs, out_specs=o_spec,
        scratch_shapes=[pltpu.VMEM((tm, tn), F32)] if nk > 1 else [],
        sem=("parallel", "parallel", "arbitrary"),
    )(*args)


def _ln_stats(x):
    mu = jnp.mean(x, axis=-1, keepdims=True)
    xc = x - mu
    var = jnp.mean(xc * xc, axis=-1, keepdims=True)
    rstd = lax.rsqrt(var + LN_EPS)
    return xc * rstd, rstd


def _ln_fwd(r, g, b, *, tm=256, name):
    T, D = r.shape
    tm = min(tm, T)

    def body(r_ref, g_ref, b_ref, y_ref, yb_ref, yt_ref):
        xhat, _ = _ln_stats(r_ref[...])
        y = xhat * g_ref[...] + b_ref[...]
        y_ref[...] = y
        yb_ref[...] = y.astype(BF16)
        yt_ref[...] = y.T.astype(BF16)

    row = pl.BlockSpec((tm, D), lambda i: (i, 0))
    vec = pl.BlockSpec((1, D), lambda i: (0, 0))
    return pl.pallas_call(
        body, name=name,
        out_shape=[jax.ShapeDtypeStruct((T, D), F32), jax.ShapeDtypeStruct((T, D), BF16),
                   jax.ShapeDtypeStruct((D, T), BF16)],
        grid=(T // tm,), in_specs=[row, vec, vec],
        out_specs=[row, row, pl.BlockSpec((D, tm), lambda i: (0, i))],
        compiler_params=_params("parallel"),
    )(r, g, b)


def _cast_t(x, *, tm=256, name):
    T, D = x.shape
    tm = min(tm, T)

    def body(x_ref, xb_ref, xt_ref):
        xv = x_ref[...]
        xb_ref[...] = xv.astype(BF16)
        xt_ref[...] = xv.T.astype(BF16)

    row = pl.BlockSpec((tm, D), lambda i: (i, 0))
    return pl.pallas_call(
        body, name=name,
        out_shape=[jax.ShapeDtypeStruct((T, D), BF16), jax.ShapeDtypeStruct((D, T), BF16)],
        grid=(T // tm,), in_specs=[row], out_specs=[row, pl.BlockSpec((D, tm), lambda i: (0, i))],
        compiler_params=_params("parallel"),
    )(x)


def _ln_bwd(dy, r, g, *, tm=256, name, b=None, target=None):
    T, D = r.shape
    tm = min(tm, T)
    with_loss = target is not None

    def body(*refs):
        if with_loss:
            r_ref, g_ref, b_ref, t_ref, dr_ref, drb_ref, drt_ref, dg_ref, db_ref, loss_ref = refs
        else:
            dy_ref, r_ref, g_ref, dr_ref, drb_ref, drt_ref, dg_ref, db_ref = refs
        i = pl.program_id(0)
        xhat, rstd = _ln_stats(r_ref[...])
        gg = g_ref[...]
        if with_loss:
            err = xhat * gg + b_ref[...] - t_ref[...]
            dyv = err * (1.0 / D)
            part = jnp.sum(jnp.sum(err * err, axis=1, keepdims=True), axis=0, keepdims=True)
            part = jnp.broadcast_to(part * (0.5 / D), (1, LANES))
        else:
            dyv = dy_ref[...]
        dxh = dyv * gg
        m1 = jnp.mean(dxh, axis=-1, keepdims=True)
        m2 = jnp.mean(dxh * xhat, axis=-1, keepdims=True)
        dr = rstd * (dxh - m1 - xhat * m2)
        dr_ref[...] = dr
        drb_ref[...] = dr.astype(BF16)
        drt_ref[...] = dr.T.astype(BF16)
        dgp = jnp.sum(dyv * xhat, axis=0, keepdims=True)
        dbp = jnp.sum(dyv, axis=0, keepdims=True)

        @pl.when(i == 0)
        def _():
            dg_ref[...] = dgp
            db_ref[...] = dbp
            if with_loss:
                loss_ref[...] = part

        @pl.when(i > 0)
        def _():
            dg_ref[...] += dgp
            db_ref[...] += dbp
            if with_loss:
                loss_ref[...] += part

    row = pl.BlockSpec((tm, D), lambda i: (i, 0))
    vec = pl.BlockSpec((1, D), lambda i: (0, 0))
    out_shape = [jax.ShapeDtypeStruct((T, D), F32), jax.ShapeDtypeStruct((T, D), BF16),
                 jax.ShapeDtypeStruct((D, T), BF16), jax.ShapeDtypeStruct((1, D), F32),
                 jax.ShapeDtypeStruct((1, D), F32)]
    out_specs = [row, row, pl.BlockSpec((D, tm), lambda i: (0, i)), vec, vec]
    if with_loss:
        in_specs, args = [row, vec, vec, row], (r, g, b, target)
        out_shape.append(jax.ShapeDtypeStruct((1, LANES), F32))
        out_specs.append(pl.BlockSpec((1, LANES), lambda i: (0, 0)))
    else:
        in_specs, args = [row, row, vec], (dy, r, g)
    return pl.pallas_call(
        body, name=name, out_shape=out_shape, grid=(T // tm,),
        in_specs=in_specs, out_specs=out_specs,
        compiler_params=_params("arbitrary"),
    )(*args)


def _ffn_up(xb, w1g, w3g, *, tm=512, name, comm=None):
    S, D, Fs = w1g.shape
    T = xb.shape[0]
    tm = _pick(T, tm)

    def body(x_ref, w1_ref, w3_ref, a_ref, b_ref, h_ref):
        xv = x_ref[...]
        a = _dot(xv, w1_ref[0], _DIMS["nn"])
        b = _dot(xv, w3_ref[0], _DIMS["nn"])
        a_ref[0] = a
        b_ref[0] = b
        h_ref[0] = (a * _sigmoid(a) * b).astype(BF16)

    wsp = pl.BlockSpec((1, D, Fs), lambda j, i: (j, 0, 0))
    osp = pl.BlockSpec((1, tm, Fs), lambda j, i: (j, i, 0))
    act = jax.ShapeDtypeStruct((S, T, Fs), F32)
    return _pcall(
        body, name=name, comm=comm, out_shape=[act, act, jax.ShapeDtypeStruct((S, T, Fs), BF16)],
        grid=(S, T // tm),
        in_specs=[pl.BlockSpec((tm, D), lambda j, i: (i, 0)), wsp, wsp],
        out_specs=[osp, osp, osp],
        sem=("parallel", "parallel"),
    )(xb, w1g, w3g)


def _ffn_down(h, w2g, x, *, tm=512, name, comm=None):
    S, Fs, D = w2g.shape
    T = x.shape[0]
    tm = _pick(T, tm)

    sp = SHARDS_PER_STEP

    def body(h_ref, w2_ref, x_ref, r_ref, acc_ref):
        j = pl.program_id(1)
        @pl.when(j == 0)
        def _():
            acc_ref[...] = jnp.zeros_like(acc_ref)

        part = _dot(h_ref[0], w2_ref[0], _DIMS["nn"])
        for s in range(1, sp):
            part = part + _dot(h_ref[s], w2_ref[s], _DIMS["nn"])
        acc_ref[...] += part

        @pl.when(j == S // sp - 1)
        def _():
            r_ref[...] = DN_ALPHA * x_ref[...] + 0.5 * acc_ref[...]

    row = pl.BlockSpec((tm, D), lambda i, j: (i, 0))
    return _pcall(
        body, name=name, comm=comm, out_shape=jax.ShapeDtypeStruct((T, D), F32),
        grid=(T // tm, S // sp),
        in_specs=[pl.BlockSpec((sp, tm, Fs), lambda i, j: (j, i, 0)),
                  pl.BlockSpec((sp, Fs, D), lambda i, j: (j, 0, 0)), row],
        out_specs=row, scratch_shapes=[pltpu.VMEM((tm, D), F32)],
        sem=("parallel", "arbitrary"),
    )(h, w2g, x)


def _ffn_bwd_act(drb, w2g, a, b, *, tm=512, name, comm=None):
    S, Fs, D = w2g.shape
    T = drb.shape[0]
    tm = _pick(T, tm)

    def body(dr_ref, w2_ref, a_ref, b_ref, da_ref, db_ref):
        d = 0.5 * _dot(dr_ref[...], w2_ref[0], _DIMS["nt"])
        av = a_ref[0]
        sig = _sigmoid(av)
        da_ref[0] = (d * b_ref[0] * sig * (1.0 + av * (1.0 - sig))).astype(BF16)
        db_ref[0] = (d * av * sig).astype(BF16)

    asp = pl.BlockSpec((1, tm, Fs), lambda j, i: (j, i, 0))
    out = jax.ShapeDtypeStruct((S, T, Fs), BF16)
    return _pcall(
        body, name=name, comm=comm, out_shape=[out, out], grid=(S, T // tm),
        in_specs=[pl.BlockSpec((tm, D), lambda j, i: (i, 0)),
                  pl.BlockSpec((1, Fs, D), lambda j, i: (j, 0, 0)), asp, asp],
        out_specs=[asp, asp],
        sem=("parallel", "parallel"),
    )(drb, w2g, a, b)


def _ffn_dw(lhs_t, rhs, *, scale=1.0, tm=512, name, comm=None):
    n = len(rhs)
    S, T, Fs = rhs[0].shape
    D = lhs_t.shape[0]
    tm = _pick(D, tm)

    def body(*refs):
        l_ref = refs[0]
        r_refs, o_refs = refs[1:1 + n], refs[1 + n:]
        lv = l_ref[...]
        for r_ref, o_ref in zip(r_refs, o_refs):
            res = _dot(lv, r_ref[0], _DIMS["nn"])
            o_ref[0] = (res if scale == 1.0 else res * scale).astype(o_ref.dtype)

    osp = pl.BlockSpec((1, tm, Fs), lambda j, i: (j, i, 0))
    return _pcall(
        body, name=name, comm=comm, out_shape=[jax.ShapeDtypeStruct((S, D, Fs), BF16)] * n,
        grid=(S, D // tm),
        in_specs=[pl.BlockSpec((tm, T), lambda j, i: (i, 0))]
        + [pl.BlockSpec((1, T, Fs), lambda j, i: (j, 0, 0))] * n,
        out_specs=[osp] * n,
        sem=("parallel", "parallel"),
    )(lhs_t, *rhs)


def _ffn_dx(da, db, w1g, w3g, dr, *, tm=512, name, comm=None):
    S, D, Fs = w1g.shape
    T = dr.shape[0]
    tm = _pick(T, tm)

    sp = SHARDS_PER_STEP

    def body(da_ref, db_ref, w1_ref, w3_ref, dr_ref, o_ref, acc_ref):
        j = pl.program_id(1)
        @pl.when(j == 0)
        def _():
            acc_ref[...] = jnp.zeros_like(acc_ref)

        part = None
        for s in range(sp):
            t = (_dot(da_ref[s], w1_ref[s], _DIMS["nt"]) + _dot(db_ref[s], w3_ref[s], _DIMS["nt"]))
            part = t if part is None else part + t
        acc_ref[...] += part

        @pl.when(j == S // sp - 1)
        def _():
            o_ref[...] = DN_ALPHA * dr_ref[...] + acc_ref[...]

    asp = pl.BlockSpec((sp, tm, Fs), lambda i, j: (j, i, 0))
    wsp = pl.BlockSpec((sp, D, Fs), lambda i, j: (j, 0, 0))
    row = pl.BlockSpec((tm, D), lambda i, j: (i, 0))
    return _pcall(
        body, name=name, comm=comm, out_shape=jax.ShapeDtypeStruct((T, D), F32),
        grid=(T // tm, S // sp), in_specs=[asp, asp, wsp, wsp, row], out_specs=row,
        scratch_shapes=[pltpu.VMEM((tm, D), F32)],
        sem=("parallel", "arbitrary"),
    )(da, db, w1g, w3g, dr)


def _chunk_mask(reverse, transpose=False):
    row = lax.broadcasted_iota(jnp.int32, (CHUNK, CHUNK), 0)
    col = lax.broadcasted_iota(jnp.int32, (CHUNK, CHUNK), 1)
    if reverse != transpose:
        return col >= row
    return col <= row


def _hgrn_pre(hq, hf, lb):
    sig = _sigmoid(hf)
    f = lb + (1.0 - lb) * sig
    q = hq * _sigmoid(hq) * (HGRN_HEAD_DIM ** -0.5)
    return q, f, sig


def _hgrn_decays(f, cmf, reverse):
    bc = _dot01(cmf, jnp.log(f))
    last = 0 if reverse else CHUNK - 1
    blast = bc[last:last + 1, :]
    bref = bc[CHUNK // 2:CHUNK // 2 + 1, :]
    eq = jnp.exp(jnp.minimum(bc - bref, EXP_CLAMP))
    ek = jnp.exp(jnp.minimum(bref - bc, EXP_CLAMP))
    return bc, blast, eq, ek


def _hgrn_fwd(z, lb, *, wh, reverse, name, comm=None):
    T = z.shape[0]
    nch = T // CHUNK
    nh = wh // HGRN_HEAD_DIM
    hd = HGRN_HEAD_DIM

    def ci(i):
        return nch - 1 - i if reverse else i

    def body(hq_ref, hi_ref, hf_ref, lb_ref, o_ref, st_ref, s_ref):
        @pl.when(pl.program_id(0) == 0)
        def _():
            s_ref[...] = jnp.zeros_like(s_ref)

        cm = _chunk_mask(reverse)
        cmf = cm.astype(F32)
        H = range(nh)
        sls = [slice(h * hd, (h + 1) * hd) for h in H]
        q, f, _ = _hgrn_pre(hq_ref[...], hf_ref[...], lb_ref[...])
        v = hi_ref[...]
        k = 1.0 - f
        bc, blast, eq, ek = _hgrn_decays(f, cmf, reverse)
        qh, kh, qe = q * eq, k * ek, q * jnp.exp(bc)
        k2, eblast = k * jnp.exp(blast - bc), jnp.exp(blast)
        st = [s_ref[h] for h in H]
        att = [jnp.where(cm, _dot_nt(qh[:, sl], kh[:, sl]), 0.0) for sl in sls]
        inter = [_dot_nt(qe[:, sls[h]], st[h]) for h in H]
        out = [inter[h] + _dot_nn(att[h], v[:, sls[h]]) for h in H]
        snew = [eblast[:, sls[h]] * st[h] + _dot_tn(v[:, sls[h]], k2[:, sls[h]]) for h in H]
        for h in H:
            st_ref[0, h] = st[h]
            o_ref[:, sls[h]] = out[h]
            s_ref[h] = snew[h]

    blk = lambda c: pl.BlockSpec((CHUNK, wh), lambda i: (ci(i), c))
    return _pcall(
        body, name=name, comm=comm,
        out_shape=[jax.ShapeDtypeStruct((T, wh), F32),
                   jax.ShapeDtypeStruct((nch, nh, hd, hd), F32)],
        grid=(nch,),
        in_specs=[blk(0), blk(1), blk(3 + int(reverse)),
                  pl.BlockSpec((1, wh), lambda i: (0, 0))],
        out_specs=[blk(0), pl.BlockSpec((1, nh, hd, hd), lambda i: (ci(i), 0, 0, 0))],
        scratch_shapes=[pltpu.VMEM((nh, hd, hd), F32)],
        sem=("arbitrary",),
    )(z, z, z, lb)


def _hgrn_bwd(z, lb, do, states, prev, *, wh, reverse, name, comm=None):
    T = z.shape[0]
    nch = T // CHUNK
    nh = wh // HGRN_HEAD_DIM
    hd = HGRN_HEAD_DIM
    has_prev = prev is not None

    def ci(i):
        return i if reverse else nch - 1 - i

    def body(*refs):
        hq_ref, hi_ref, hf_ref, lb_ref, do_ref, st_ref = refs[:6]
        n_in = 8 if has_prev else 6
        dhq_ref, dhi_ref, dhf_ref, dlb_ref, ds_ref, gs_ref = refs[n_in:]

        @pl.when(pl.program_id(0) == 0)
        def _():
            ds_ref[...] = jnp.zeros_like(ds_ref)
            gs_ref[...] = jnp.zeros_like(gs_ref)
            dlb_ref[...] = jnp.zeros_like(dlb_ref)

        cm = _chunk_mask(reverse)
        cmf = cm.astype(F32)
        cmtf = _chunk_mask(reverse, transpose=True).astype(F32)
        H = range(nh)
        sls = [slice(h * hd, (h + 1) * hd) for h in H]
        hq, lb = hq_ref[...], lb_ref[...]
        q, f, sig = _hgrn_pre(hq, hf_ref[...], lb)
        v, dov = hi_ref[...], do_ref[...]
        k = 1.0 - f
        bc, blast, eq, ek = _hgrn_decays(f, cmf, reverse)
        ebc, eb2, eblast = jnp.exp(bc), jnp.exp(blast - bc), jnp.exp(blast)
        qh, kh, qe, k2 = q * eq, k * ek, q * ebc, k * eb2
        st = [st_ref[0, h] for h in H]
        dst = [ds_ref[h] for h in H]
        att = [jnp.where(cm, _dot_nt(qh[:, sl], kh[:, sl]), 0.0) for sl in sls]
        datt = [jnp.where(cm, _dot_nt(dov[:, sl], v[:, sl]), 0.0) for sl in sls]
        dq_a = [_dot_nn(datt[h], kh[:, sls[h]]) for h in H]
        dq_s = [_dot_nn(dov[:, sls[h]], st[h]) for h in H]
        dk_a = [_dot_tn(datt[h], qh[:, sls[h]]) for h in H]
        dk_s = [_dot_nn(v[:, sls[h]], dst[h]) for h in H]
        dv = [_dot_tn(att[h], dov[:, sls[h]]) + _dot_nt(k2[:, sls[h]], dst[h]) for h in H]
        dsn = [eblast[:, sls[h]] * dst[h] + _dot_tn(dov[:, sls[h]], qe[:, sls[h]]) for h in H]
        for h in H:
            ds_ref[h] = dsn[h]
        dq = jnp.concatenate(dq_a, axis=1) * eq + ebc * jnp.concatenate(dq_s, axis=1)
        dk = jnp.concatenate(dk_a, axis=1) * ek + eb2 * jnp.concatenate(dk_s, axis=1)
        dvv = jnp.concatenate(dv, axis=1)
        db = q * dq - k * dk
        dg = _dot01(cmtf, db) + gs_ref[...]
        gs_ref[...] += jnp.sum(db, axis=0, keepdims=True)
        df = dg / f - dk
        dhf_ref[...] = df * (1.0 - lb) * sig * (1.0 - sig)
        dlb_ref[...] += jnp.sum(df * (1.0 - sig), axis=0, keepdims=True)
        sq = _sigmoid(hq)
        dhq = dq * (HGRN_HEAD_DIM ** -0.5) * sq * (1.0 + hq * (1.0 - sq))
        if has_prev:
            dhq = dhq + refs[6][...]
            dvv = dvv + refs[7][...]
        dhq_ref[...] = dhq
        dhi_ref[...] = dvv

    blk = lambda c: pl.BlockSpec((CHUNK, wh), lambda i: (ci(i), c))
    vec = pl.BlockSpec((1, wh), lambda i: (0, 0))
    in_specs = [blk(0), blk(1), blk(3 + int(reverse)), vec, blk(0),
                pl.BlockSpec((1, nh, hd, hd), lambda i: (ci(i), 0, 0, 0))]
    args = [z, z, z, lb, do, states]
    if has_prev:
        in_specs += [blk(0), blk(0)]
        args += list(prev)
    big = jax.ShapeDtypeStruct((T, wh), F32)
    return _pcall(
        body, name=name, comm=comm,
        out_shape=[big, big, big, jax.ShapeDtypeStruct((1, wh), F32)],
        grid=(nch,), in_specs=in_specs, out_specs=[blk(0), blk(0), blk(0), vec],
        scratch_shapes=[pltpu.VMEM((nh, hd, hd), F32), pltpu.VMEM((1, wh), F32)],
        sem=("arbitrary",),
    )(*args)


def _mlstm_intra(cm, cmt, ig_row, ig_col, xf_row, xf_col, m_st):
    lf_row, lf_col = _log_sigmoid(xf_row), _log_sigmoid(xf_col)
    bcol = jnp.sum(jnp.where(cm, lf_row, 0.0), axis=1, keepdims=True)
    brow = jnp.sum(jnp.where(cmt, lf_col, 0.0), axis=0, keepdims=True)
    blast = jnp.sum(lf_row, axis=1, keepdims=True)
    dmat = jnp.where(cm, bcol - brow + ig_row, NEG)
    m_inter = bcol + m_st
    m_t = jnp.maximum(m_inter, jnp.max(dmat, axis=1, keepdims=True))
    p = jnp.exp(dmat - m_t)
    inter = jnp.exp(m_inter - m_t)
    w_col = blast - bcol + ig_col
    m_new = jnp.maximum(blast + m_st, jnp.max(w_col, axis=0, keepdims=True))
    cs = jnp.exp(blast + m_st - m_new)
    kscale = jnp.exp(w_col - m_new)
    return p, inter, m_t, m_new, cs, kscale


def _mlstm_specs(T, wm, nhm, reverse, backward):
    nch = T // CHUNK
    dm = wm // nhm
    ng = 4 * nhm

    def ci(i):
        fwd_order = nch - 1 - i if reverse else i
        return nch - 1 - fwd_order if backward else fwd_order

    row = lambda w, c: pl.BlockSpec((CHUNK, w), lambda i: (ci(i), c))
    gates_row = pl.BlockSpec((1, ng, CHUNK), lambda i: (ci(i), 0, 0))
    bias_row = pl.BlockSpec((1, LANES), lambda i: (0, 0))
    bias_col = pl.BlockSpec((ng, 1), lambda i: (0, 0))
    st_c = pl.BlockSpec((1, nhm, dm, dm), lambda i: (ci(i), 0, 0, 0))
    st_n = pl.BlockSpec((1, nhm, 1, dm), lambda i: (ci(i), 0, 0, 0))
    st_m = pl.BlockSpec((1, nhm, 1, LANES), lambda i: (ci(i), 0, 0, 0))
    return nch, dm, ng, ci, row, gates_row, bias_row, bias_col, st_c, st_n, st_m


def _mlstm_fwd(qk, z, gc, gr, br, bcl, *, wm, nhm, vcol, reverse, name, comm=None):
    T = qk.shape[0]
    nch, dm, ng, ci, row, gates_row, bias_row, bias_col, st_c, st_n, st_m = _mlstm_specs(
        T, wm, nhm, reverse, False)
    d = int(reverse)

    def body(q_ref, k_ref, v_ref, gc_ref, gr_ref, br_ref, bc_ref,
             h_ref, cst_ref, nst_ref, mst_ref, c_ref, n_ref, m_ref):
        @pl.when(pl.program_id(0) == 0)
        def _():
            c_ref[...] = jnp.zeros_like(c_ref)
            n_ref[...] = jnp.zeros_like(n_ref)
            m_ref[...] = jnp.full_like(m_ref, M_INIT)

        cm = _chunk_mask(reverse)
        cmt = _chunk_mask(reverse, transpose=True)
        G = gc_ref[...] + br_ref[...]
        Gr = gr_ref[0] + bc_ref[...]
        H = range(nhm)
        sls = [slice(h * dm, (h + 1) * dm) for h in H]
        m_all = [m_ref[h] for h in H]
        intra = [_mlstm_intra(cm, cmt, Gr[d * nhm + h:d * nhm + h + 1, :], G[:, d * nhm + h:d * nhm + h + 1],
                              Gr[2 * nhm + d * nhm + h:2 * nhm + d * nhm + h + 1, :],
                              G[:, 2 * nhm + d * nhm + h:2 * nhm + d * nhm + h + 1], m_all[h][:, 0:1])
                 for h in H]
        p, inter, m_t, m_new, cs, kscale = zip(*intra)
        q = [q_ref[:, sl] * (dm ** -0.5) for sl in sls]
        k = [k_ref[:, sl] for sl in sls]
        v = [v_ref[:, sl] for sl in sls]
        ct = [c_ref[h] for h in H]
        n = [n_ref[h] for h in H]
        sc = [_dot_nt(q[h], k[h]) * p[h] for h in H]
        qc = [_dot_nt(q[h], ct[h]) for h in H]
        num = [_dot_nn(sc[h], v[h]) + inter[h] * qc[h] for h in H]
        den = [jnp.sum(sc[h], axis=1, keepdims=True) + inter[h] * jnp.sum(q[h] * n[h], axis=1, keepdims=True)
               for h in H]
        kw = [k[h] * kscale[h] for h in H]
        cnew = [cs[h] * ct[h] + _dot_tn(v[h], kw[h]) for h in H]
        for h in H:
            cst_ref[0, h] = ct[h]
            nst_ref[0, h] = n[h]
            mst_ref[0, h] = m_all[h]
            h_ref[:, sls[h]] = num[h] / jnp.maximum(jnp.abs(den[h]), jnp.exp(-m_t[h]))
            c_ref[h] = cnew[h]
            n_ref[h] = cs[h] * n[h] + jnp.sum(kw[h], axis=0, keepdims=True)
            m_ref[h] = jnp.broadcast_to(m_new[h], (1, LANES))

    return _pcall(
        body, name=name, comm=comm,
        out_shape=[jax.ShapeDtypeStruct((T, wm), F32),
                   jax.ShapeDtypeStruct((nch, nhm, dm, dm), F32),
                   jax.ShapeDtypeStruct((nch, nhm, 1, dm), F32),
                   jax.ShapeDtypeStruct((nch, nhm, 1, LANES), F32)],
        grid=(nch,),
        in_specs=[row(wm, 0), row(wm, 1), row(wm, vcol), row(LANES, 0), gates_row, bias_row, bias_col],
        out_specs=[row(wm, 0), st_c, st_n, st_m],
        scratch_shapes=[pltpu.VMEM((nhm, dm, dm), F32), pltpu.VMEM((nhm, 1, dm), F32),
                        pltpu.VMEM((nhm, 1, LANES), F32)],
        sem=("arbitrary",),
    )(qk, qk, z, gc, gr, br, bcl)


def _mlstm_bwd(qk, z, gc, gr, br, bcl, dh, states, prev, *, wm, nhm, vcol, reverse, name, comm=None):
    T = qk.shape[0]
    nch, dm, ng, ci, row, gates_row, bias_row, bias_col, st_c, st_n, st_m = _mlstm_specs(
        T, wm, nhm, reverse, True)
    d = int(reverse)
    has_prev = prev is not None
    n_in = 13 if has_prev else 11

    def body(*refs):
        (q_ref, k_ref, v_ref, gc_ref, gr_ref, br_ref, bc_ref, dh_ref,
         cst_ref, nst_ref, mst_ref) = refs[:11]
        dqk_ref, dv_ref, dgr_ref, dgs_ref, e_ref, en_ref, fs_ref = refs[n_in:]

        @pl.when(pl.program_id(0) == 0)
        def _():
            e_ref[...] = jnp.zeros_like(e_ref)
            en_ref[...] = jnp.zeros_like(en_ref)
            fs_ref[...] = jnp.zeros_like(fs_ref)
            dgs_ref[...] = jnp.zeros_like(dgs_ref)

        cm = _chunk_mask(reverse)
        cmt = _chunk_mask(reverse, transpose=True)
        row_i = lax.broadcasted_iota(jnp.int32, (CHUNK, CHUNK), 0)
        col_i = lax.broadcasted_iota(jnp.int32, (CHUNK, CHUNK), 1)
        eye = row_i == col_i
        G = gc_ref[...] + br_ref[...]
        Gr = gr_ref[0] + bc_ref[...]
        H = range(nhm)
        sls = [slice(h * dm, (h + 1) * dm) for h in H]
        xf_row = [Gr[2 * nhm + d * nhm + h:2 * nhm + d * nhm + h + 1, :] for h in H]
        intra = [_mlstm_intra(cm, cmt, Gr[d * nhm + h:d * nhm + h + 1, :], G[:, d * nhm + h:d * nhm + h + 1],
                              xf_row[h], G[:, 2 * nhm + d * nhm + h:2 * nhm + d * nhm + h + 1],
                              mst_ref[0, h][:, 0:1]) for h in H]
        p, inter, m_t, _, cs, kscale = zip(*intra)
        q = [q_ref[:, sls[h]] * (dm ** -0.5) for h in H]
        k = [k_ref[:, sls[h]] for h in H]
        v = [v_ref[:, sls[h]] for h in H]
        dhv = [dh_ref[:, sls[h]] for h in H]
        ct = [cst_ref[0, h] for h in H]
        n = [nst_ref[0, h] for h in H]
        et = [e_ref[h] for h in H]
        en = [en_ref[h] for h in H]
        carry = [fs_ref[h][:, 0:1] for h in H]
        sc = [_dot_nt(q[h], k[h]) * p[h] for h in H]
        qc = [_dot_nt(q[h], ct[h]) for h in H]
        num = [_dot_nn(sc[h], v[h]) + inter[h] * qc[h] for h in H]
        den = [jnp.sum(sc[h], axis=1, keepdims=True) + inter[h] * jnp.sum(q[h] * n[h], axis=1, keepdims=True)
               for h in H]
        floor = [jnp.exp(-m_t[h]) for h in H]
        nstab = [jnp.maximum(jnp.abs(den[h]), floor[h]) for h in H]
        gh = [dhv[h] / nstab[h] for h in H]
        dd = [-jnp.sum(dhv[h] * (num[h] / nstab[h]), axis=1, keepdims=True) / nstab[h]
              * jnp.where(jnp.abs(den[h]) > floor[h], jnp.sign(den[h]), 0.0) for h in H]
        dsqk = [(_dot_nt(gh[h], v[h]) + dd[h]) * p[h] for h in H]
        qi = [q[h] * inter[h] for h in H]
        kw = [k[h] * kscale[h] for h in H]
        ghc = [_dot_nn(gh[h], ct[h]) for h in H]
        vet = [_dot_nn(v[h], et[h]) for h in H]
        dq = [_dot_nn(dsqk[h], k[h]) + inter[h] * (ghc[h] + dd[h] * n[h]) for h in H]
        dk = [_dot_tn(dsqk[h], q[h]) + kscale[h] * (vet[h] + en[h]) for h in H]
        dv = [_dot_tn(sc[h], gh[h]) + _dot_nt(kw[h], et[h]) for h in H]
        e_new = [cs[h] * et[h] + _dot_tn(gh[h], qi[h]) for h in H]
        en_new = [cs[h] * en[h] + jnp.sum(qi[h] * dd[h], axis=0, keepdims=True) for h in H]
        di_col = [jnp.sum(k[h] * dk[h], axis=1, keepdims=True) for h in H]
        df_col = [jnp.sum(q[h] * dq[h], axis=1, keepdims=True) - di_col[h] for h in H]
        di_row = [jnp.sum(jnp.where(eye, di_col[h], 0.0), axis=0, keepdims=True) for h in H]
        dlf_row = [jnp.sum(jnp.where(cm, df_col[h], 0.0), axis=0, keepdims=True) + carry[h] for h in H]
        dxf_row = [dlf_row[h] * (1.0 - _sigmoid(xf_row[h])) for h in H]
        for h in H:
            sl = sls[h]
            slk = slice(wm + h * dm, wm + (h + 1) * dm)
            e_ref[h] = e_new[h]
            en_ref[h] = en_new[h]
            fs_ref[h] = jnp.broadcast_to(carry[h] + jnp.sum(df_col[h], axis=0, keepdims=True), (1, LANES))
            dgr_ref[0, h:h + 1, :] = di_row[h]
            dgr_ref[0, nhm + h:nhm + h + 1, :] = dxf_row[h]
            dgs_ref[h:h + 1, :] += jnp.broadcast_to(jnp.sum(di_row[h], axis=1, keepdims=True), (1, LANES))
            dgs_ref[nhm + h:nhm + h + 1, :] += jnp.broadcast_to(
                jnp.sum(dxf_row[h], axis=1, keepdims=True), (1, LANES))
            dqh, dkh, dvh = dq[h] * (dm ** -0.5), dk[h], dv[h]
            if has_prev:
                dqh = dqh + refs[11][:, sl]
                dkh = dkh + refs[11][:, slk]
                dvh = dvh + refs[12][:, sl]
            dqk_ref[:, sl] = dqh
            dqk_ref[:, slk] = dkh
            dv_ref[:, sl] = dvh

    in_specs = [row(wm, 0), row(wm, 1), row(wm, vcol), row(LANES, 0), gates_row, bias_row, bias_col,
                row(wm, 0), st_c, st_n, st_m]
    args = [qk, qk, z, gc, gr, br, bcl, dh] + list(states)
    if has_prev:
        in_specs += [row(2 * wm, 0), row(wm, 0)]
        args += list(prev)
    return _pcall(
        body, name=name, comm=comm,
        out_shape=[jax.ShapeDtypeStruct((T, 2 * wm), F32), jax.ShapeDtypeStruct((T, wm), F32),
                   jax.ShapeDtypeStruct((nch, 2 * nhm, CHUNK), F32),
                   jax.ShapeDtypeStruct((2 * nhm, LANES), F32)],
        grid=(nch,), in_specs=in_specs,
        out_specs=[row(2 * wm, 0), row(wm, 0),
                   pl.BlockSpec((1, 2 * nhm, CHUNK), lambda i: (ci(i), 0, 0)),
                   pl.BlockSpec((2 * nhm, LANES), lambda i: (0, 0))],
        scratch_shapes=[pltpu.VMEM((nhm, dm, dm), F32), pltpu.VMEM((nhm, 1, dm), F32),
                        pltpu.VMEM((nhm, 1, LANES), F32)],
        sem=("arbitrary",),
    )(*args)


def _conv_taps(x, w_ref):
    T = x.shape[0]
    t = lax.broadcasted_iota(jnp.int32, x.shape, 0)
    taps = []
    acc = None
    for j in range(CONV_WIDTH):
        s = CONV_WIDTH // 2 - j
        if s == 0:
            xs = x
        else:
            xs = jnp.where((t - s >= 0) & (t - s < T), pltpu.roll(x, s % T, 0), 0.0)
        taps.append(xs)
        term = w_ref[j:j + 1, :] * xs
        acc = term if acc is None else acc + term
    return taps, acc


def _conv_fwd(z, w, b, *, col0, tc=LANES, name):
    T = z.shape[0]
    C2 = w.shape[1]
    assert col0 % tc == 0 and C2 % tc == 0

    def body(z_ref, w_ref, b_ref, o_ref):
        _, acc = _conv_taps(z_ref[...], w_ref)
        c = acc + b_ref[...]
        o_ref[...] = c * _sigmoid(c)

    return pl.pallas_call(
        body, name=name, out_shape=jax.ShapeDtypeStruct((T, C2), F32),
        grid=(C2 // tc,),
        in_specs=[pl.BlockSpec((T, tc), lambda j: (0, col0 // tc + j)),
                  pl.BlockSpec((CONV_WIDTH, tc), lambda j: (0, j)),
                  pl.BlockSpec((1, tc), lambda j: (0, j))],
        out_specs=pl.BlockSpec((T, tc), lambda j: (0, j)),
        compiler_params=_params("parallel"),
    )(z, w, b)


def _conv_bwd(dy, z, w, b, *, col0, tc=LANES, name):
    T = z.shape[0]
    C2 = w.shape[1]

    def body(dy_ref, z_ref, w_ref, b_ref, dx_ref, dw_ref, db_ref):
        taps, acc = _conv_taps(z_ref[...], w_ref)
        c = acc + b_ref[...]
        sg = _sigmoid(c)
        dc = dy_ref[...] * sg * (1.0 + c * (1.0 - sg))
        t = lax.broadcasted_iota(jnp.int32, dc.shape, 0)
        dx = None
        for j in range(CONV_WIDTH):
            s = j - CONV_WIDTH // 2
            if s == 0:
                ds = dc
            else:
                ds = jnp.where((t - s >= 0) & (t - s < T), pltpu.roll(dc, s % T, 0), 0.0)
            term = w_ref[j:j + 1, :] * ds
            dx = term if dx is None else dx + term
            dw_ref[j:j + 1, :] = jnp.sum(dc * taps[j], axis=0, keepdims=True)
        dx_ref[...] = dx
        db_ref[...] = jnp.sum(dc, axis=0, keepdims=True)

    col = pl.BlockSpec((T, tc), lambda j: (0, j))
    wsp = pl.BlockSpec((CONV_WIDTH, tc), lambda j: (0, j))
    bsp = pl.BlockSpec((1, tc), lambda j: (0, j))
    return pl.pallas_call(
        body, name=name,
        out_shape=[jax.ShapeDtypeStruct((T, C2), F32), jax.ShapeDtypeStruct((CONV_WIDTH, C2), F32),
                   jax.ShapeDtypeStruct((1, C2), F32)],
        grid=(C2 // tc,),
        in_specs=[col, pl.BlockSpec((T, tc), lambda j: (0, col0 // tc + j)), wsp, bsp],
        out_specs=[col, wsp, bsp],
        compiler_params=_params("parallel"),
    )(dy, z, w, b)


def _mix_heads(o_fw, o_bw, h_fw, h_bw, wh, wm, nhm):
    out = []
    hd = HGRN_HEAD_DIM
    for h in range(wh // hd):
        sl = slice(h * hd, (h + 1) * hd)
        o = o_fw[:, sl] + o_bw[:, sl]
        r = lax.rsqrt(jnp.mean(o * o, axis=-1, keepdims=True) + NORM_EPS)
        out.append((0, sl, o * r, r))
    dm = wm // nhm
    for h in range(nhm):
        sl = slice(h * dm, (h + 1) * dm)
        x = h_fw[:, sl] + h_bw[:, sl]
        xc = x - jnp.mean(x, axis=-1, keepdims=True)
        r = lax.rsqrt(jnp.mean(xc * xc, axis=-1, keepdims=True) + NORM_EPS)
        out.append((1, sl, xc * r, r))
    return out


def _mix_specs(T, wh, wm, tm, gcol, ocol):
    rowh = pl.BlockSpec((tm, wh), lambda i: (i, 0))
    rowm = pl.BlockSpec((tm, wm), lambda i: (i, 0))
    hg = pl.BlockSpec((tm, wh), lambda i: (i, gcol))
    mo = pl.BlockSpec((tm, wm), lambda i: (i, ocol))
    vh = pl.BlockSpec((1, wh), lambda i: (0, 0))
    vm = pl.BlockSpec((1, wm), lambda i: (0, 0))
    return rowh, rowm, hg, mo, vh, vm


def _mix_fwd(o_fw, o_bw, h_fw, h_bw, z, gh, gm, *, nhm, gcol, ocol, tm=256, name):
    T, wh = o_fw.shape
    wm = h_fw.shape[1]
    tm = min(tm, T)
    rowh, rowm, hg, mo, vh, vm = _mix_specs(T, wh, wm, tm, gcol, ocol)

    def body(of_ref, ob_ref, hf_ref, hb_ref, hg_ref, mo_ref, gh_ref, gm_ref, y_ref, yt_ref):
        heads = _mix_heads(of_ref[...], ob_ref[...], hf_ref[...], hb_ref[...], wh, wm, nhm)
        for grp, sl, nv, _ in heads:
            if grp == 0:
                gate = hg_ref[:, sl]
                gate = gate * _sigmoid(gate)
                y = nv * gh_ref[:, sl] * gate
                osl = sl
            else:
                y = nv * gm_ref[:, sl] * _sigmoid(mo_ref[:, sl])
                osl = slice(wh + sl.start, wh + sl.stop)
            y_ref[:, osl] = y.astype(BF16)
            yt_ref[osl, :] = y.T.astype(BF16)

    return pl.pallas_call(
        body, name=name,
        out_shape=[jax.ShapeDtypeStruct((T, wh + wm), BF16), jax.ShapeDtypeStruct((wh + wm, T), BF16)],
        grid=(T // tm,),
        in_specs=[rowh, rowh, rowm, rowm, hg, mo, vh, vm],
        out_specs=[pl.BlockSpec((tm, wh + wm), lambda i: (i, 0)),
                   pl.BlockSpec((wh + wm, tm), lambda i: (0, i))],
        compiler_params=_params("parallel"),
    )(o_fw, o_bw, h_fw, h_bw, z, z, gh, gm)


def _mix_bwd(dy, o_fw, o_bw, h_fw, h_bw, z, gh, gm, *, nhm, gcol, ocol, tm=256, name):
    T, wh = o_fw.shape
    wm = h_fw.shape[1]
    tm = min(tm, T)
    rowh, rowm, hg, mo, vh, vm = _mix_specs(T, wh, wm, tm, gcol, ocol)

    def body(dy_ref, of_ref, ob_ref, hf_ref, hb_ref, hg_ref, mo_ref, gh_ref, gm_ref,
             do_ref, dh_ref, dhg_ref, dmo_ref, dgh_ref, dgm_ref):
        @pl.when(pl.program_id(0) == 0)
        def _():
            dgh_ref[...] = jnp.zeros_like(dgh_ref)
            dgm_ref[...] = jnp.zeros_like(dgm_ref)

        heads = _mix_heads(of_ref[...], ob_ref[...], hf_ref[...], hb_ref[...], wh, wm, nhm)
        for grp, sl, nv, r in heads:
            if grp == 0:
                d = dy_ref[:, sl]
                x = hg_ref[:, sl]
                sg = _sigmoid(x)
                gate = x * sg
                g = gh_ref[:, sl]
                dgh_ref[:, sl] += jnp.sum(d * nv * gate, axis=0, keepdims=True)
                dhg_ref[:, sl] = d * nv * g * sg * (1.0 + x * (1.0 - sg))
                t = d * g * gate
                do_ref[:, sl] = r * (t - nv * jnp.mean(t * nv, axis=-1, keepdims=True))
            else:
                d = dy_ref[:, slice(wh + sl.start, wh + sl.stop)]
                sg = _sigmoid(mo_ref[:, sl])
                g = gm_ref[:, sl]
                dgm_ref[:, sl] += jnp.sum(d * nv * sg, axis=0, keepdims=True)
                dmo_ref[:, sl] = d * nv * g * sg * (1.0 - sg)
                t = d * g * sg
                dh_ref[:, sl] = r * (t - jnp.mean(t, axis=-1, keepdims=True)
                                     - nv * jnp.mean(t * nv, axis=-1, keepdims=True))

    return pl.pallas_call(
        body, name=name,
        out_shape=[jax.ShapeDtypeStruct((T, wh), F32), jax.ShapeDtypeStruct((T, wm), F32),
                   jax.ShapeDtypeStruct((T, wh), F32), jax.ShapeDtypeStruct((T, wm), F32),
                   jax.ShapeDtypeStruct((1, wh), F32), jax.ShapeDtypeStruct((1, wm), F32)],
        grid=(T // tm,),
        in_specs=[pl.BlockSpec((tm, wh + wm), lambda i: (i, 0)), rowh, rowh, rowm, rowm, hg, mo, vh, vm],
        out_specs=[rowh, rowm, rowh, rowm, vh, vm],
        compiler_params=_params("arbitrary"),
    )(dy, o_fw, o_bw, h_fw, h_bw, z, z, gh, gm)


def _pick(n, pref):
    for c in range(pref - pref % LANES, 0, -LANES):
        if n % c == 0:
            return c
    return n


def _mm(a, b, mode="nn", **kw):
    if mode == "nn":
        (M, K), N = a.shape, b.shape[1]
    elif mode == "nt":
        (M, K), N = a.shape, b.shape[0]
    else:
        (K, M), N = a.shape, b.shape[1]
    tk = _pick(K, 2048) if mode != "tn" else _pick(K, 512)
    return _matmul(a, b, mode=mode, tm=_pick(M, 1024), tn=_pick(N, 1024), tk=tk, **kw)


def _device_step(x, target, sh, small_parts, small_params, core):
    T, D = x.shape

    def pair_add(name, g, land):
        return _pair_add(g, land, core, "rs_add_" + name)

    g_w1a, g_w3a, g_small = _comm_call(_ag_job([sh["ffn1_w1"], sh["ffn1_w3"], small_parts]), "ag_ffn1")
    P = small_params(g_small)
    wh = P["hgrn_norm_g"].shape[1]
    wm = P["mlstm_norm_g"].shape[1]
    nhm = P["ig_b"].shape[1]
    ng = 4 * nhm
    nch = T // CHUNK
    assert wh == wm and T % CHUNK == 0
    vcol, ocol, gcol = 7, 8, 2
    col0 = 5 * wh
    xb, xt = _cast_t(x, name="x_cast")
    win_top, win_bot = sh["w_in"][:D // 2], sh["w_in"][D // 2:]
    (a1, b1, h1), (g_w2a, g_wtop) = _ffn_up(xb, g_w1a, g_w3a, name="ffn1_up",
                                            comm=_ag_job([sh["ffn1_w2"], win_top]))
    r1, (g_wbot,) = _ffn_down(h1, g_w2a, x, name="ffn1_down", comm=_ag_job([win_bot]))
    win_full = jnp.concatenate([g.transpose(1, 0, 2).reshape(D // 2, -1) for g in (g_wtop, g_wbot)], axis=0)
    ncols = win_full.shape[1]
    w_win = win_full[:, :ncols - ng]
    w_wg = jnp.zeros((D, LANES), BF16).at[:, :ng].set(win_full[:, ncols - ng:])
    x1, x1b, x1t = _ln_fwd(r1, P["ln1_g"], P["ln1_b"], name="ln1")
    zm, (g_wout,) = _mm(x1b, w_win, name="zm", comm=_ag_job([sh["w_out"]]))
    w_wout = g_wout.reshape(D, D)
    zg = _mm(x1b, w_wg, name="zg")
    gr = zg[:, :ng].reshape(nch, CHUNK, ng).transpose(0, 2, 1)
    bias = jnp.concatenate([P["ig_b"].reshape(-1), P["fg_b"].reshape(-1)])
    br = jnp.zeros((1, LANES), F32).at[0, :ng].set(bias)
    bcl = bias.reshape(ng, 1)
    lbf, lbb = P["lb"][0:1], P["lb"][1:2]
    (o_fw, s_fw), (g_w1b,) = _hgrn_fwd(zm, lbf, wh=wh, reverse=False, name="hgrn_fw",
                                       comm=_ag_job([sh["ffn2_w1"]]))
    (o_bw, s_bw), (g_w3b,) = _hgrn_fwd(zm, lbb, wh=wh, reverse=True, name="hgrn_bw",
                                       comm=_ag_job([sh["ffn2_w3"]]))
    qk = _conv_fwd(zm, P["conv_w"], P["conv_b"], col0=col0, name="conv")
    (h_fw, *st_fw), (g_w2b,) = _mlstm_fwd(qk, zm, zg, gr, br, bcl, wm=wm, nhm=nhm, vcol=vcol,
                                          reverse=False, name="mlstm_fw", comm=_ag_job([sh["ffn2_w2"]]))
    h_bw, *st_bw = _mlstm_fwd(qk, zm, zg, gr, br, bcl, wm=wm, nhm=nhm, vcol=vcol, reverse=True,
                              name="mlstm_bw")
    y, yt = _mix_fwd(o_fw, o_bw, h_fw, h_bw, zm, P["hgrn_norm_g"], P["mlstm_norm_g"],
                     nhm=nhm, gcol=gcol, ocol=ocol, name="mix")
    r2 = _mm(y, w_wout, add=x1, add_scale=DN_ALPHA, name="r2")
    x2, x2b, x2t = _ln_fwd(r2, P["ln2_g"], P["ln2_b"], name="ln2")
    a2, b2, h2 = _ffn_up(x2b, g_w1b, g_w3b, name="ffn2_up")
    r3 = _ffn_down(h2, g_w2b, x2, name="ffn2_down")

    G = {}
    dr3, dr3b, dr3t, G["ln3_g"], G["ln3_b"], loss = _ln_bwd(
        None, r3, P["ln3_g"], b=P["ln3_b"], target=target, name="ln3_bwd")
    land = {}
    da2, db2 = _ffn_bwd_act(dr3b, g_w2b, a2, b2, name="ffn2_dact")
    gw1b, gw3b = _ffn_dw(x2t, [da2, db2], name="ffn2_dw13")
    (gw2bt,) = _ffn_dw(dr3t, [h2], scale=0.5, name="ffn2_dw2t")
    dx2, l1 = _ffn_dx(da2, db2, g_w1b, g_w3b, dr3, name="ffn2_dx", comm=_pair_job([gw1b, gw3b, gw2bt]))
    p_w1b, p_w3b, p_w2bt = [pair_add(k, g, l) for k, g, l in
                            zip(("ffn2_w1", "ffn2_w3", "ffn2_w2"), (gw1b, gw3b, gw2bt), l1)]
    dr2, dr2b, _, G["ln2_g"], G["ln2_b"] = _ln_bwd(dx2, r2, P["ln2_g"], name="ln2_bwd")
    dy = _mm(dr2b, w_wout, "nt", name="dy")
    gwout = _mm(yt, dr2b, out_dtype=BF16, name="dwout").reshape(N_DEV, -1, D)
    do, dh, dhg, dmo, G["hgrn_norm_g"], G["mlstm_norm_g"] = _mix_bwd(
        dy, o_fw, o_bw, h_fw, h_bw, zm, P["hgrn_norm_g"], P["mlstm_norm_g"],
        nhm=nhm, gcol=gcol, ocol=ocol, name="mix_bwd")
    (dqk_f, dv_f, dgr_f, dgs_f), (land["ffn2_w1"],) = _mlstm_bwd(
        qk, zm, zg, gr, br, bcl, dh, st_fw, None, wm=wm, nhm=nhm, vcol=vcol, reverse=False,
        name="mlstm_fw_bwd", comm=_chips_job([p_w1b]))
    (dqk, dv, dgr_b, dgs_b), (land["ffn2_w3"],) = _mlstm_bwd(
        qk, zm, zg, gr, br, bcl, dh, st_bw, (dqk_f, dv_f), wm=wm, nhm=nhm, vcol=vcol, reverse=True,
        name="mlstm_bw_bwd", comm=_chips_job([p_w3b]))
    dmqk, G["conv_w"], G["conv_b"] = _conv_bwd(dqk, zm, P["conv_w"], P["conv_b"], col0=col0,
                                               name="conv_bwd")
    (dhq_f, dhi_f, dhf_fw, dlb_f), (land["ffn2_w2"],) = _hgrn_bwd(
        zm, lbf, do, s_fw, None, wh=wh, reverse=False, name="hgrn_fw_bwd", comm=_chips_job([p_w2bt]))
    dhq, dhi, dhf_bw, dlb_b = _hgrn_bwd(zm, lbb, do, s_bw, (dhq_f, dhi_f), wh=wh, reverse=True,
                                        name="hgrn_bw_bwd")
    G["lb"] = jnp.concatenate([dlb_f, dlb_b], axis=0)
    G["ig_b"] = jnp.stack([dgs_f[:nhm, 0], dgs_b[:nhm, 0]])
    G["fg_b"] = jnp.stack([dgs_f[nhm:, 0], dgs_b[nhm:, 0]])
    dzm = jnp.concatenate([dhq, dhi, dhg, dhf_fw, dhf_bw, dmqk, dv, dmo], axis=1).astype(BF16)
    dgr = jnp.concatenate([dgr_f[:, :nhm], dgr_b[:, :nhm], dgr_f[:, nhm:], dgr_b[:, nhm:]], axis=1)
    dzg = jnp.zeros((T, LANES), F32).at[:, :ng].set(dgr.transpose(0, 2, 1).reshape(T, ng)).astype(BF16)
    dwin = jnp.concatenate([_mm(x1t, dzm, out_dtype=BF16, name="dwin"),
                            _mm(x1t, dzg, out_dtype=BF16, name="dwg")[:, :ng]], axis=1)
    gwin = dwin.reshape(D, N_DEV, -1).transpose(1, 0, 2)
    t = _mm(dzg, w_wg, "nt", add=dr2, add_scale=DN_ALPHA, name="dx1_g")
    dx1, l1 = _mm(dzm, w_win, "nt", add=t, name="dx1", comm=_pair_job([gwout, gwin]))
    p_wout, p_win = pair_add("w_out", gwout, l1[0]), pair_add("w_in", gwin, l1[1])
    dr1, dr1b, dr1t, G["ln1_g"], G["ln1_b"] = _ln_bwd(dx1, r1, P["ln1_g"], name="ln1_bwd")
    (gw2at,), (land["w_out"],) = _ffn_dw(dr1t, [h1], scale=0.5, name="ffn1_dw2t",
                                         comm=_chips_job([p_wout]))
    (da1, db1), (l1, land["w_in"]) = _ffn_bwd_act(
        dr1b, g_w2a, a1, b1, name="ffn1_dact", comm=_join(_pair_job([gw2at]), _chips_job([p_win])))
    p_w2at = pair_add("ffn1_w2", gw2at, l1)
    (gw1a, gw3a), (land["ffn1_w2"],) = _ffn_dw(xt, [da1, db1], name="ffn1_dw13",
                                               comm=_chips_job([p_w2at]))
    l1 = _comm_call(_pair_job([gw1a, gw3a]), "rs_pair_ffn1_w13")
    p_w1a, p_w3a = pair_add("ffn1_w1", gw1a, l1[0]), pair_add("ffn1_w3", gw3a, l1[1])
    gx, (land["ffn1_w1"], land["ffn1_w3"]) = _ffn_dx(
        da1, db1, g_w1a, g_w3a, dr1, name="ffn1_dx", comm=_chips_job([p_w1a, p_w3a]))
    return loss, gx, land, G


ANY = pl.BlockSpec(memory_space=pl.ANY)


def _place():
    x, y, c = lax.axis_index("x"), lax.axis_index("y"), lax.axis_index("c")
    chips = [(1 - x, y), (x, 1 - y), (1 - x, 1 - y)]
    return x, y, c, chips


def _comm_call(job, name):
    n_in, n_out = len(job.inputs), len(job.out_shape)

    def body(*refs):
        ins, outs, sems = refs[:n_in], refs[n_in:n_in + n_out], refs[n_in + n_out:]
        job.start(ins, outs, sems)
        job.finish(ins, outs, sems)

    return pl.pallas_call(body, name=name, out_shape=job.out_shape, in_specs=[ANY] * n_in,
                          out_specs=[ANY] * n_out, scratch_shapes=job.sems)(*job.inputs)


def _ag_job(parts):
    n = len(parts)

    def copies(ins, outs, sems):
        send_sems, recv_sems, local_sems = sems
        x, y, c, chips = _place()
        me, sibling = (x, y, c), (x, y, 1 - c)

        def copy(a, k, block, to, src=None):
            bx, by, bc = block
            slot = outs[a].at[4 * bx + 2 * by + bc]
            return pltpu.make_async_remote_copy(
                src_ref=slot if src is None else src, dst_ref=slot,
                send_sem=send_sems.at[a, k], recv_sem=recv_sems.at[a, k],
                device_id=to, device_id_type=MESH)

        local = [pltpu.make_async_copy(ins[a], outs[a].at[4 * x + 2 * y + c], local_sems.at[a])
                 for a in range(n)]
        first = []
        for a in range(n):
            first.append(copy(a, 0, me, sibling, src=ins[a]))
            first += [copy(a, 1 + j, me, (*chip, c), src=ins[a]) for j, chip in enumerate(chips)]
        return copy, local, first, chips, me, sibling, c

    def start(ins, outs, sems):
        _, local, first, *_ = copies(ins, outs, sems)
        for cp in local + first:
            cp.start()

    def finish(ins, outs, sems):
        copy, local, first, chips, me, sibling, c = copies(ins, outs, sems)
        passed = []
        for a in range(n):
            for j, chip in enumerate(chips):
                copy(a, 1 + j, (*chip, c), me).wait_recv()
                fwd = copy(a, 4 + j, (*chip, c), sibling)
                fwd.start()
                passed.append(fwd)
        for a in range(n):
            copy(a, 0, sibling, me).wait_recv()
            for j, chip in enumerate(chips):
                copy(a, 4 + j, (*chip, 1 - c), me).wait_recv()
        for cp in first + passed:
            cp.wait_send()
        for cp in local:
            cp.wait()

    return _Job(parts, [jax.ShapeDtypeStruct((N_DEV,) + p.shape, p.dtype) for p in parts],
                [pltpu.SemaphoreType.DMA((n, 7)), pltpu.SemaphoreType.DMA((n, 7)),
                 pltpu.SemaphoreType.DMA((n,))], start, finish)


def _pair_job(gs):
    n = len(gs)

    def copies(g_refs, land_refs, sems):
        send_sems, recv_sems = sems
        x, y, c, _ = _place()
        return [pltpu.make_async_remote_copy(
            src_ref=g_refs[a].at[2 * k + 1 - c], dst_ref=land_refs[a].at[k],
            send_sem=send_sems.at[a, k], recv_sem=recv_sems.at[a, k],
            device_id=(x, y, 1 - c), device_id_type=MESH) for a in range(n) for k in range(4)]

    def start(ins, outs, sems):
        for cp in copies(ins, outs, sems):
            cp.start()

    def finish(ins, outs, sems):
        for cp in copies(ins, outs, sems):
            cp.wait()

    return _Job(gs, [jax.ShapeDtypeStruct((4,) + g.shape[1:], g.dtype) for g in gs],
                [pltpu.SemaphoreType.DMA((n, 4)), pltpu.SemaphoreType.DMA((n, 4))], start, finish)


def _chips_job(ps):
    n = len(ps)

    def copies(p_refs, land_refs, sems):
        send_sems, recv_sems, local_sems = sems
        x, y, c, chips = _place()
        mine = 2 * x + y
        owns = [pltpu.make_async_copy(p_refs[a].at[mine], land_refs[a].at[mine], local_sems.at[a])
                for a in range(n)]
        sends = [pltpu.make_async_remote_copy(
            src_ref=p_refs[a].at[2 * chip[0] + chip[1]], dst_ref=land_refs[a].at[mine],
            send_sem=send_sems.at[a, j], recv_sem=recv_sems.at[a, j],
            device_id=(*chip, c), device_id_type=MESH) for a in range(n) for j, chip in enumerate(chips)]
        recvs = [pltpu.make_async_remote_copy(
            src_ref=p_refs[a].at[mine], dst_ref=land_refs[a].at[2 * chip[0] + chip[1]],
            send_sem=send_sems.at[a, j], recv_sem=recv_sems.at[a, j],
            device_id=(*chip, c), device_id_type=MESH) for a in range(n) for j, chip in enumerate(chips)]
        return owns, sends, recvs

    def start(ins, outs, sems):
        owns, sends, _ = copies(ins, outs, sems)
        for cp in owns + sends:
            cp.start()

    def finish(ins, outs, sems):
        owns, sends, recvs = copies(ins, outs, sems)
        for cp in recvs:
            cp.wait_recv()
        for cp in sends:
            cp.wait_send()
        for cp in owns:
            cp.wait()

    return _Job(ps, [jax.ShapeDtypeStruct(p.shape, p.dtype) for p in ps],
                [pltpu.SemaphoreType.DMA((n, 3)), pltpu.SemaphoreType.DMA((n, 3)),
                 pltpu.SemaphoreType.DMA((n,))], start, finish)


def _all_reduce_small(buf, name):
    rows = buf.shape[0]

    def body(b_ref, o_ref, slots, send_sems, recv_sems):
        x, y, c, _ = _place()
        me = 4 * x + 2 * y + c
        slots[me] = b_ref[...]
        cps = []
        for k in range(1, N_DEV):
            fx, fy, fc = (k >> 2) & 1, (k >> 1) & 1, k & 1
            peer = (x ^ fx, y ^ fy, c ^ fc)
            cps.append(pltpu.make_async_remote_copy(
                src_ref=b_ref, dst_ref=slots.at[me],
                send_sem=send_sems.at[k - 1], recv_sem=recv_sems.at[k - 1],
                device_id=peer, device_id_type=MESH))
        for cp in cps:
            cp.start()
        for k in range(1, N_DEV):
            fx, fy, fc = (k >> 2) & 1, (k >> 1) & 1, k & 1
            src = 4 * (x ^ fx) + 2 * (y ^ fy) + (c ^ fc)
            pltpu.make_async_remote_copy(
                src_ref=b_ref, dst_ref=slots.at[src],
                send_sem=send_sems.at[k - 1], recv_sem=recv_sems.at[k - 1],
                device_id=(x ^ fx, y ^ fy, c ^ fc), device_id_type=MESH).wait_recv()
        for cp in cps:
            cp.wait_send()
        acc = slots[0]
        for k in range(1, N_DEV):
            acc = acc + slots[k]
        o_ref[...] = acc

    vm = pl.BlockSpec(memory_space=pltpu.VMEM)
    return pl.pallas_call(
        body, name=name, out_shape=jax.ShapeDtypeStruct(buf.shape, F32),
        in_specs=[vm], out_specs=vm,
        scratch_shapes=[pltpu.VMEM((N_DEV, rows, LANES), F32),
                        pltpu.SemaphoreType.DMA((N_DEV - 1,)), pltpu.SemaphoreType.DMA((N_DEV - 1,))],
    )(buf)


def _row_tile(rows, pref=2048):
    for t in range(min(pref, rows) - min(pref, rows) % 8, 0, -8):
        if rows % t == 0:
            return t
    return rows


def _tile_rows(rows, cols, nbuf):
    budget = VMEM_LIMIT // 2 // (2 * nbuf * 4 * cols)
    return _row_tile(rows, max(8, min(512, budget // 8 * 8)))


def _pair_add(g, land, core, name):
    _, r, c = g.shape
    tr = _tile_rows(r, c, 3)

    def body(core_ref, g_ref, l_ref, o_ref):
        o_ref[...] = (g_ref[...].astype(F32) + l_ref[...].astype(F32)).astype(o_ref.dtype)

    blk = pl.BlockSpec((1, tr, c), lambda k, i, cr: (k, i, 0))
    return pl.pallas_call(
        body, name=name, out_shape=jax.ShapeDtypeStruct(land.shape, land.dtype),
        grid_spec=pltpu.PrefetchScalarGridSpec(
            num_scalar_prefetch=1, grid=(4, r // tr),
            in_specs=[pl.BlockSpec((1, tr, c), lambda k, i, cr: (2 * k + cr[0], i, 0)), blk],
            out_specs=blk),
        compiler_params=_params("parallel", "parallel"),
    )(core, g, land)


def _sum4(land, name):
    _, r, c = land.shape
    tr = _tile_rows(r, c, 5)

    def body(a_ref, b_ref, c_ref, d_ref, o_ref):
        o_ref[...] = ((a_ref[0].astype(F32) + b_ref[0].astype(F32)) + c_ref[0].astype(F32)
                      + d_ref[0].astype(F32))

    slab = lambda k: pl.BlockSpec((1, tr, c), lambda i: (k, i, 0))
    return pl.pallas_call(
        body, name=name, out_shape=jax.ShapeDtypeStruct((r, c), F32),
        grid=(r // tr,), in_specs=[slab(0), slab(1), slab(2), slab(3)],
        out_specs=pl.BlockSpec((tr, c), lambda i: (i, 0)),
        compiler_params=_params("parallel"),
    )(land, land, land, land)


def _adamw(w, parts, m, v, name):
    shp = w.shape
    cols = shp[-1]
    w2, m2, v2 = (t.reshape(-1, cols) for t in (w, m, v))
    rows = w2.shape[0]
    n = parts.shape[0]
    assert parts.shape[1:] == (rows, cols), (parts.shape, shp)
    tr = _tile_rows(rows, cols, n + 7)
    c1 = 1.0 / (1.0 - ADAM_B1 ** ADAM_STEP)
    c2 = 1.0 / (1.0 - ADAM_B2 ** ADAM_STEP)

    def body(*refs):
        p_refs = refs[:n]
        w_ref, m_ref, v_ref, g_ref, d_ref, nm_ref, nv_ref = refs[n:]
        gv = p_refs[0][0].astype(F32)
        for p_ref in p_refs[1:]:
            gv = gv + p_ref[0].astype(F32)
        nm = ADAM_B1 * m_ref[...] + (1.0 - ADAM_B1) * gv
        nv = ADAM_B2 * v_ref[...] + (1.0 - ADAM_B2) * (gv * gv)
        g_ref[...] = gv
        nm_ref[...] = nm
        nv_ref[...] = nv
        d_ref[...] = -ADAM_LR * ((nm * c1) / (jnp.sqrt(nv * c2) + ADAM_EPS) + ADAM_WD * w_ref[...])

    blk = pl.BlockSpec((tr, cols), lambda i: (i, 0))
    slab = lambda k: pl.BlockSpec((1, tr, cols), lambda i: (k, i, 0))
    out = jax.ShapeDtypeStruct((rows, cols), F32)
    outs = pl.pallas_call(
        body, name=name, out_shape=[out] * 4, grid=(rows // tr,),
        in_specs=[slab(k) for k in range(n)] + [blk] * 3, out_specs=[blk] * 4,
        compiler_params=_params("parallel"),
    )(*([parts] * n), w2, m2, v2)
    return tuple(o.reshape(shp) for o in outs)


def _pack_rows(arrs):
    rows = []
    for a in arrs:
        flat = a.reshape(-1).astype(F32)
        pad = (-flat.shape[0]) % LANES
        if pad:
            flat = jnp.concatenate([flat, jnp.zeros((pad,), F32)])
        rows.append(flat.reshape(-1, LANES))
    out = jnp.concatenate(rows, axis=0)
    pad = (-out.shape[0]) % 8
    if pad:
        out = jnp.concatenate([out, jnp.zeros((pad, LANES), F32)], axis=0)
    return out


def _unpack_rows(buf, shapes):
    outs, r = [], 0
    for s in shapes:
        n = math.prod(s)
        nr = -(-n // LANES)
        outs.append(buf[r:r + nr].reshape(-1)[:n].reshape(s))
        r += nr
    return outs


BIG = ("ffn1_w1", "ffn1_w3", "ffn1_w2", "w_in", "w_out", "ffn2_w1", "ffn2_w3", "ffn2_w2")
ROW_SHARDED = ("ffn1_w2", "w_out", "ffn2_w2")
SMALL = ("ln1_g", "ln1_b", "hgrn_lb", "hgrn_norm_g", "mlstm_conv_w", "mlstm_conv_b", "mlstm_ig_b",
         "mlstm_fg_b", "mlstm_norm_g", "ln2_g", "ln2_b", "ln3_g", "ln3_b")
WEIGHTS = ("ffn1_w1", "ffn1_w3", "ffn1_w2", "ln1_g", "ln1_b", "w_in", "hgrn_lb", "hgrn_norm_g",
           "mlstm_conv_w", "mlstm_conv_b", "mlstm_ig_b", "mlstm_fg_b", "mlstm_norm_g", "w_out",
           "ln2_g", "ln2_b", "ffn2_w1", "ffn2_w3", "ffn2_w2", "ln3_g", "ln3_b")


def kernel(x, ffn1_w1, ffn1_w3, ffn1_w2, ln1_g, ln1_b, w_in, hgrn_lb, hgrn_norm_g, mlstm_conv_w, mlstm_conv_b, mlstm_ig_b, mlstm_fg_b, mlstm_norm_g, w_out, ln2_g, ln2_b, ffn2_w1, ffn2_w3, ffn2_w2, ln3_g, ln3_b, loss_target, m_ffn1_w1, m_ffn1_w3, m_ffn1_w2, m_ln1_g, m_ln1_b, m_w_in, m_hgrn_lb, m_hgrn_norm_g, m_mlstm_conv_w, m_mlstm_conv_b, m_mlstm_ig_b, m_mlstm_fg_b, m_mlstm_norm_g, m_w_out, m_ln2_g, m_ln2_b, m_ffn2_w1, m_ffn2_w3, m_ffn2_w2, m_ln3_g, m_ln3_b, v_ffn1_w1, v_ffn1_w3, v_ffn1_w2, v_ln1_g, v_ln1_b, v_w_in, v_hgrn_lb, v_hgrn_norm_g, v_mlstm_conv_w, v_mlstm_conv_b, v_mlstm_ig_b, v_mlstm_fg_b, v_mlstm_norm_g, v_w_out, v_ln2_g, v_ln2_b, v_ffn2_w1, v_ffn2_w3, v_ffn2_w2, v_ln3_g, v_ln3_b):
    args = (ffn1_w1, ffn1_w3, ffn1_w2, ln1_g, ln1_b, w_in, hgrn_lb, hgrn_norm_g, mlstm_conv_w,
            mlstm_conv_b, mlstm_ig_b, mlstm_fg_b, mlstm_norm_g, w_out, ln2_g, ln2_b, ffn2_w1, ffn2_w3,
            ffn2_w2, ln3_g, ln3_b)
    ms = (m_ffn1_w1, m_ffn1_w3, m_ffn1_w2, m_ln1_g, m_ln1_b, m_w_in, m_hgrn_lb, m_hgrn_norm_g,
          m_mlstm_conv_w, m_mlstm_conv_b, m_mlstm_ig_b, m_mlstm_fg_b, m_mlstm_norm_g, m_w_out, m_ln2_g,
          m_ln2_b, m_ffn2_w1, m_ffn2_w3, m_ffn2_w2, m_ln3_g, m_ln3_b)
    vs = (v_ffn1_w1, v_ffn1_w3, v_ffn1_w2, v_ln1_g, v_ln1_b, v_w_in, v_hgrn_lb, v_hgrn_norm_g,
          v_mlstm_conv_w, v_mlstm_conv_b, v_mlstm_ig_b, v_mlstm_fg_b, v_mlstm_norm_g, v_w_out, v_ln2_g,
          v_ln2_b, v_ffn2_w1, v_ffn2_w3, v_ffn2_w2, v_ln3_g, v_ln3_b)
    w = dict(zip(WEIGHTS, args))
    m = dict(zip(WEIGHTS, ms))
    v = dict(zip(WEIGHTS, vs))
    core = lax.axis_index("c")
    dev = 4 * lax.axis_index("x") + 2 * lax.axis_index("y") + core
    p_lb = []

    def small_params(gathered_small):
        lb_sh, cw_sh = zip(*[_unpack_rows(gathered_small[i], [hgrn_lb.shape, mlstm_conv_w.shape])
                             for i in range(N_DEV)])
        hgrn_lb_full = jnp.concatenate(lb_sh, axis=-1)
        conv_w_full = jnp.concatenate(cw_sh, axis=-1)[0]
        p_lb.append(jax.nn.softmax(hgrn_lb_full, axis=1))
        return dict(ln1_g=ln1_g, ln1_b=ln1_b, ln2_g=ln2_g, ln2_b=ln2_b, ln3_g=ln3_g, ln3_b=ln3_b,
                    lb=p_lb[0][:, 0], hgrn_norm_g=hgrn_norm_g, conv_w=conv_w_full, conv_b=mlstm_conv_b,
                    ig_b=mlstm_ig_b[0], fg_b=mlstm_fg_b[0], mlstm_norm_g=mlstm_norm_g)

    loss_row, grad_x, land2, G = _device_step(
        x[0], loss_target[0], {k: w[k][0].astype(BF16) for k in BIG},
        _pack_rows([hgrn_lb, mlstm_conv_w]), small_params, jnp.reshape(core, (1,)).astype(jnp.int32))
    p_lb = p_lb[0]

    dlb = G["lb"]
    g_lb = jnp.stack([dlb * p_lb[:, 0] * (1.0 - p_lb[:, 0]), -dlb * p_lb[:, 0] * p_lb[:, 1]], axis=1)
    small_full = {"ln1_g": G["ln1_g"], "ln1_b": G["ln1_b"], "hgrn_lb": g_lb, "hgrn_norm_g": G["hgrn_norm_g"],
                  "mlstm_conv_w": G["conv_w"][None], "mlstm_conv_b": G["conv_b"],
                  "mlstm_ig_b": G["ig_b"][None], "mlstm_fg_b": G["fg_b"][None],
                  "mlstm_norm_g": G["mlstm_norm_g"], "ln2_g": G["ln2_g"], "ln2_b": G["ln2_b"],
                  "ln3_g": G["ln3_g"], "ln3_b": G["ln3_b"]}
    small_list = [small_full[k] for k in SMALL] + [loss_row]
    reduced = _all_reduce_small(_pack_rows(small_list), "all_reduce_small")
    red = _unpack_rows(reduced, [a.shape for a in small_list])
    loss = red[-1][0, 0]
    small_g = {}
    for k, gk in zip(SMALL, red[:-1]):
        if k in ("hgrn_lb", "mlstm_conv_w"):
            n = w[k].shape[-1]
            gk = lax.dynamic_slice_in_dim(gk, dev * n, n, axis=gk.ndim - 1)
        small_g[k] = gk

    grads, delta, new_m, new_v = {}, {}, {}, {}
    for k in BIG:
        contrib = land2[k]
        if k in ROW_SHARDED and k != "w_out":
            contrib = _sum4(contrib, "rs_sum_" + k).T[None]
        grads[k], delta[k], new_m[k], new_v[k] = _adamw(w[k], contrib, m[k], v[k], "adamw_" + k)
    sm = _adamw(_pack_rows([w[k] for k in SMALL]), _pack_rows([small_g[k] for k in SMALL])[None],
                _pack_rows([m[k] for k in SMALL]), _pack_rows([v[k] for k in SMALL]), "adamw_small")
    shapes = [w[k].shape for k in SMALL]
    for dst, buf in zip((grads, delta, new_m, new_v), sm):
        for k, val in zip(SMALL, _unpack_rows(buf, shapes)):
            dst[k] = val
    return (loss, grad_x[None], *[grads[k] for k in WEIGHTS], *[delta[k] for k in WEIGHTS],
            *[new_m[k] for k in WEIGHTS], *[new_v[k] for k in WEIGHTS])
```

```python
import math

import jax
import jax.numpy as jnp
from jax import lax
from jax.experimental import pallas as pl
from jax.experimental.pallas import tpu as pltpu

F32 = jnp.float32
BF16 = jnp.bfloat16

CHUNK = 64
HGRN_HEAD_DIM = 128
CONV_WIDTH = 5
DN_ALPHA = 2.0 ** 0.25
LN_EPS = 1e-5
NORM_EPS = 1e-6
M_INIT = -1e30
NEG = -1e30
EXP_CLAMP = 80.0
ADAM_LR = 0.001
ADAM_B1 = 0.9
ADAM_B2 = 0.999
ADAM_EPS = 1e-08
ADAM_WD = 0.01
ADAM_STEP = 10
N_DEV = 8
LANES = 128
VMEM_LIMIT = 56 * 1024 * 1024
MESH = pl.DeviceIdType.MESH


def _params(*sem):
    return pltpu.CompilerParams(dimension_semantics=sem, vmem_limit_bytes=VMEM_LIMIT)


class _Job:
    def __init__(self, inputs, out_shape, sems, start, finish):
        self.inputs, self.out_shape, self.sems = list(inputs), list(out_shape), list(sems)
        self.start, self.finish = start, finish


def _join(*jobs):
    def split(refs, counts):
        out, p = [], 0
        for c in counts:
            out.append(refs[p:p + c])
            p += c
        return out

    n_in = [len(j.inputs) for j in jobs]
    n_out = [len(j.out_shape) for j in jobs]
    n_sem = [len(j.sems) for j in jobs]

    def start(ins, outs, sems):
        for j, i, o, s in zip(jobs, split(ins, n_in), split(outs, n_out), split(sems, n_sem)):
            j.start(i, o, s)

    def finish(ins, outs, sems):
        for j, i, o, s in zip(jobs, split(ins, n_in), split(outs, n_out), split(sems, n_sem)):
            j.finish(i, o, s)

    return _Job(sum((j.inputs for j in jobs), []), sum((j.out_shape for j in jobs), []),
                sum((j.sems for j in jobs), []), start, finish)


def _pcall(body, *, name, out_shape, grid, in_specs, out_specs, sem, scratch_shapes=(), comm=None):
    single = not isinstance(out_shape, (list, tuple))
    out_shape = [out_shape] if single else list(out_shape)
    out_specs = [out_specs] if single else list(out_specs)
    in_specs, scratch_shapes = list(in_specs), list(scratch_shapes)
    if comm is None:
        call = pl.pallas_call(body, name=name, out_shape=out_shape, grid=grid, in_specs=in_specs,
                              out_specs=out_specs, scratch_shapes=scratch_shapes,
                              compiler_params=_params(*sem))

        def run(*args):
            outs = call(*args)
            return outs[0] if single else outs
        return run

    n_in, n_out, n_sc = len(in_specs), len(out_shape), len(scratch_shapes)
    c_in, c_out = len(comm.inputs), len(comm.out_shape)

    def hosted(*refs):
        ins, cins = refs[:n_in], refs[n_in:n_in + c_in]
        p = n_in + c_in
        outs, couts = refs[p:p + n_out], refs[p + n_out:p + n_out + c_out]
        p += n_out + c_out
        scratch, csems = refs[p:p + n_sc], refs[p + n_sc:]
        first = pl.program_id(0) == 0
        last = pl.program_id(0) == grid[0] - 1
        for d in range(1, len(grid)):
            first = first & (pl.program_id(d) == 0)
            last = last & (pl.program_id(d) == grid[d] - 1)

        @pl.when(first)
        def _():
            comm.start(cins, couts, csems)

        body(*ins, *outs, *scratch)

        @pl.when(last)
        def _():
            comm.finish(cins, couts, csems)

    any_spec = pl.BlockSpec(memory_space=pl.ANY)
    call = pl.pallas_call(
        hosted, name=name, out_shape=out_shape + comm.out_shape, grid=grid,
        in_specs=in_specs + [any_spec] * c_in, out_specs=out_specs + [any_spec] * c_out,
        scratch_shapes=scratch_shapes + comm.sems,
        compiler_params=_params(*(["arbitrary"] * len(grid))))

    def run(*args):
        res = call(*args, *comm.inputs)
        outs, couts = res[:n_out], list(res[n_out:])
        return (outs[0] if single else outs), couts
    return run


def _sigmoid(x):
    return 1.0 / (1.0 + jnp.exp(-x))


def _log_sigmoid(x):
    return jnp.minimum(x, 0.0) - jnp.log(1.0 + jnp.exp(-jnp.abs(x)))


def _dot(a, b, dims):
    return lax.dot_general(a.astype(BF16), b.astype(BF16), (dims, ((), ())),
                           preferred_element_type=F32)


def _dot3(a, b, dims):
    ah = a.astype(BF16)
    al = (a - ah.astype(F32)).astype(BF16)
    bh = b.astype(BF16)
    bl = (b - bh.astype(F32)).astype(BF16)
    d = (dims, ((), ()))
    out = lax.dot_general(ah, bh, d, preferred_element_type=F32)
    out = out + lax.dot_general(ah, bl, d, preferred_element_type=F32)
    return out + lax.dot_general(al, bh, d, preferred_element_type=F32)


_DIMS = {"nn": ((1,), (0,)), "nt": ((1,), (1,)), "tn": ((0,), (0,))}


def _dot_nn(a, b):
    return _dot3(a, b, _DIMS["nn"])


def _dot_nt(a, b):
    return _dot3(a, b, _DIMS["nt"])


def _dot_tn(a, b):
    return _dot3(a, b, _DIMS["tn"])


def _split3(x):
    hi = x.astype(BF16)
    r1 = x - hi.astype(F32)
    mid = r1.astype(BF16)
    lo = (r1 - mid.astype(F32)).astype(BF16)
    return hi, mid, lo


def _dot01(mask01, x, mode="nn"):
    m = mask01.astype(BF16)
    hi, mid, lo = _split3(x)
    d = (_DIMS[mode], ((), ()))
    out = lax.dot_general(m, hi, d, preferred_element_type=F32)
    out = out + lax.dot_general(m, mid, d, preferred_element_type=F32)
    return out + lax.dot_general(m, lo, d, preferred_element_type=F32)


def _matmul(a, b, *, mode="nn", out_dtype=F32, tm=512, tn=512, tk=None,
            add=None, scale=1.0, add_scale=1.0, name, comm=None, rows=None):
    if mode == "nn":
        (M, K), (K2, N) = a.shape, b.shape
        K2 = rows or K2
    elif mode == "nt":
        (M, K), (N, K2) = a.shape, b.shape
        N = rows or N
    else:
        (K, M), (K2, N) = a.shape, b.shape
    assert K == K2, (a.shape, b.shape, mode)
    tm, tn = min(tm, M), min(tn, N)
    tk = min(tk or K, K)
    assert M % tm == 0 and N % tn == 0 and K % tk == 0, (M, N, K, tm, tn, tk)
    nk = K // tk
    dims = _DIMS[mode]
    has_add = add is not None

    def body(*refs):
        if has_add:
            a_ref, b_ref, add_ref, o_ref = refs[:4]
        else:
            a_ref, b_ref, o_ref = refs[:3]
            add_ref = None
        acc_ref = refs[-1] if nk > 1 else None

        def finish(acc):
            out = acc if scale == 1.0 else acc * scale
            if has_add:
                out = out + add_ref[...].astype(F32) * add_scale
            o_ref[...] = out.astype(o_ref.dtype)

        if nk == 1:
            finish(_dot(a_ref[...], b_ref[...], dims))
        else:
            k = pl.program_id(2)

            @pl.when(k == 0)
            def _():
                acc_ref[...] = jnp.zeros_like(acc_ref)

            acc_ref[...] += _dot(a_ref[...], b_ref[...], dims)

            @pl.when(k == nk - 1)
            def _():
                finish(acc_ref[...])

    if mode == "tn":
        a_spec = pl.BlockSpec((tk, tm), lambda i, j, k: (k, i))
    else:
        a_spec = pl.BlockSpec((tm, tk), lambda i, j, k: (i, k))
    if mode == "nt":
        b_spec = pl.BlockSpec((tn, tk), lambda i, j, k: (j, k))
    else:
        b_spec = pl.BlockSpec((tk, tn), lambda i, j, k: (k, j))
    o_spec = pl.BlockSpec((tm, tn), lambda i, j, k: (i, j))
    in_specs = [a_spec, b_spec] + ([o_spec] if has_add else [])
    args = (a, b) + ((add,) if has_add else ())
    return _pcall(
        body, name=name, comm=comm,
        out_shape=jax.ShapeDtypeStruct((M, N), out_dtype),
        grid=(M // tm, N // tn, nk),
        in_specs=in_specs, out_specs=o_spec,
        scratch_shapes=[pltpu.VMEM((tm, tn), F32)] if nk > 1 else [],
        sem=("parallel", "parallel", "arbitrary"),
    )(*args)


def _ln_stats(x):
    mu = jnp.mean(x, axis=-1, keepdims=True)
    xc = x - mu
    var = jnp.mean(xc * xc, axis=-1, keepdims=True)
    rstd = lax.rsqrt(var + LN_EPS)
    return xc * rstd, rstd


def _ln_fwd(r, g, b, *, tm=256, name):
    T, D = r.shape
    tm = min(tm, T)

    def body(r_ref, g_ref, b_ref, y_ref, yb_ref):
        xhat, _ = _ln_stats(r_ref[...])
        y = xhat * g_ref[...] + b_ref[...]
        y_ref[...] = y
        yb_ref[...] = y.astype(BF16)

    row = pl.BlockSpec((tm, D), lambda i: (i, 0))
    vec = pl.BlockSpec((1, D), lambda i: (0, 0))
    return pl.pallas_call(
        body, name=name,
        out_shape=[jax.ShapeDtypeStruct((T, D), F32), jax.ShapeDtypeStruct((T, D), BF16)],
        grid=(T // tm,), in_specs=[row, vec, vec], out_specs=[row, row],
        compiler_params=_params("parallel"),
    )(r, g, b)


def _cast(x, *, tm=512, name):
    T, D = x.shape
    tm = min(tm, T)

    def body(x_ref, xb_ref):
        xb_ref[...] = x_ref[...].astype(BF16)

    row = pl.BlockSpec((tm, D), lambda i: (i, 0))
    return pl.pallas_call(
        body, name=name, out_shape=jax.ShapeDtypeStruct((T, D), BF16),
        grid=(T // tm,), in_specs=[row], out_specs=row,
        compiler_params=_params("parallel"),
    )(x)


def _ln_bwd(dy, r, g, *, tm=256, name, b=None, target=None):
    T, D = r.shape
    tm = min(tm, T)
    with_loss = target is not None

    def body(*refs):
        if with_loss:
            r_ref, g_ref, b_ref, t_ref, dr_ref, drb_ref, dg_ref, db_ref, loss_ref = refs
        else:
            dy_ref, r_ref, g_ref, dr_ref, drb_ref, dg_ref, db_ref = refs
        i = pl.program_id(0)
        xhat, rstd = _ln_stats(r_ref[...])
        gg = g_ref[...]
        if with_loss:
            err = xhat * gg + b_ref[...] - t_ref[...]
            dyv = err * (1.0 / D)
            part = jnp.sum(jnp.sum(err * err, axis=1, keepdims=True), axis=0, keepdims=True)
            part = jnp.broadcast_to(part * (0.5 / D), (1, LANES))
        else:
            dyv = dy_ref[...]
        dxh = dyv * gg
        m1 = jnp.mean(dxh, axis=-1, keepdims=True)
        m2 = jnp.mean(dxh * xhat, axis=-1, keepdims=True)
        dr = rstd * (dxh - m1 - xhat * m2)
        dr_ref[...] = dr
        drb_ref[...] = dr.astype(BF16)
        dgp = jnp.sum(dyv * xhat, axis=0, keepdims=True)
        dbp = jnp.sum(dyv, axis=0, keepdims=True)

        @pl.when(i == 0)
        def _():
            dg_ref[...] = dgp
            db_ref[...] = dbp
            if with_loss:
                loss_ref[...] = part

        @pl.when(i > 0)
        def _():
            dg_ref[...] += dgp
            db_ref[...] += dbp
            if with_loss:
                loss_ref[...] += part

    row = pl.BlockSpec((tm, D), lambda i: (i, 0))
    vec = pl.BlockSpec((1, D), lambda i: (0, 0))
    out_shape = [jax.ShapeDtypeStruct((T, D), F32), jax.ShapeDtypeStruct((T, D), BF16),
                 jax.ShapeDtypeStruct((1, D), F32), jax.ShapeDtypeStruct((1, D), F32)]
    out_specs = [row, row, vec, vec]
    if with_loss:
        in_specs, args = [row, vec, vec, row], (r, g, b, target)
        out_shape.append(jax.ShapeDtypeStruct((1, LANES), F32))
        out_specs.append(pl.BlockSpec((1, LANES), lambda i: (0, 0)))
    else:
        in_specs, args = [row, row, vec], (dy, r, g)
    return pl.pallas_call(
        body, name=name, out_shape=out_shape, grid=(T // tm,),
        in_specs=in_specs, out_specs=out_specs,
        compiler_params=_params("arbitrary"),
    )(*args)


def _ffn_up(xb, w1t, w3t, *, tm=1024, tf=512, name, comm=None):
    F, D = w1t.shape
    T = xb.shape[0]
    tm, tf = _pick(T, tm), _pick(F, tf)

    def body(x_ref, w1_ref, w3_ref, a_ref, b_ref, h_ref, ht_ref):
        xv = x_ref[...]
        a = _dot(xv, w1_ref[...], _DIMS["nt"])
        b = _dot(xv, w3_ref[...], _DIMS["nt"])
        a_ref[...] = a.astype(BF16)
        b_ref[...] = b.astype(BF16)
        h = a * _sigmoid(a) * b
        h_ref[...] = h.astype(BF16)
        ht_ref[...] = h.T.astype(BF16)

    wsp = pl.BlockSpec((tf, D), lambda j, i: (j, 0))
    osp = pl.BlockSpec((tm, tf), lambda j, i: (i, j))
    act = jax.ShapeDtypeStruct((T, F), BF16)
    return _pcall(
        body, name=name, comm=comm, out_shape=[act, act, act, jax.ShapeDtypeStruct((F, T), BF16)],
        grid=(F // tf, T // tm),
        in_specs=[pl.BlockSpec((tm, D), lambda j, i: (i, 0)), wsp, wsp],
        out_specs=[osp, osp, osp, pl.BlockSpec((tf, tm), lambda j, i: (j, i))],
        sem=("parallel", "parallel"),
    )(xb, w1t, w3t)


def _ffn_bwd_act(drb, w2, a, b, *, tm=1024, tf=512, name, comm=None):
    F, D = w2.shape
    T = drb.shape[0]
    tm, tf = _pick(T, tm), _pick(F, tf)

    def body(dr_ref, w2_ref, a_ref, b_ref, da_ref, db_ref, dat_ref, dbt_ref):
        d = 0.5 * _dot(dr_ref[...], w2_ref[...], _DIMS["nt"])
        av = a_ref[...].astype(F32)
        sig = _sigmoid(av)
        da = d * b_ref[...].astype(F32) * sig * (1.0 + av * (1.0 - sig))
        db = d * av * sig
        da_ref[...] = da.astype(BF16)
        db_ref[...] = db.astype(BF16)
        dat_ref[...] = da.T.astype(BF16)
        dbt_ref[...] = db.T.astype(BF16)

    asp = pl.BlockSpec((tm, tf), lambda i, j: (i, j))
    tsp = pl.BlockSpec((tf, tm), lambda i, j: (j, i))
    out, out_t = jax.ShapeDtypeStruct((T, F), BF16), jax.ShapeDtypeStruct((F, T), BF16)
    return _pcall(
        body, name=name, comm=comm, out_shape=[out, out, out_t, out_t], grid=(T // tm, F // tf),
        in_specs=[pl.BlockSpec((tm, D), lambda i, j: (i, 0)),
                  pl.BlockSpec((tf, D), lambda i, j: (j, 0)), asp, asp],
        out_specs=[asp, asp, tsp, tsp],
        sem=("parallel", "parallel"),
    )(drb, w2, a, b)


def _dz_pack(pieces, *, tm=256, name):
    T = pieces[0].shape[0]
    widths = [p.shape[1] for p in pieces]
    W = sum(widths)
    tm = _pick(T, tm)
    n = len(pieces)

    def body(*refs):
        o_ref, ot_ref = refs[n], refs[n + 1]
        c = 0
        for p_ref, w in zip(refs[:n], widths):
            v = p_ref[...]
            o_ref[:, c:c + w] = v.astype(BF16)
            ot_ref[c:c + w, :] = v.T.astype(BF16)
            c += w

    return pl.pallas_call(
        body, name=name,
        out_shape=[jax.ShapeDtypeStruct((T, W), BF16), jax.ShapeDtypeStruct((W, T), BF16)],
        grid=(T // tm,),
        in_specs=[pl.BlockSpec((tm, w), lambda i: (i, 0)) for w in widths],
        out_specs=[pl.BlockSpec((tm, W), lambda i: (i, 0)), pl.BlockSpec((W, tm), lambda i: (0, i))],
        compiler_params=_params("parallel"),
    )(*pieces)


def _ffn_dx(da, db, w1t, w3t, dr, *, tm=1024, tn=1024, tk=None, name, comm=None):
    T, F = da.shape
    D = w1t.shape[1]
    tm, tn = _pick(T, tm), _pick(D, tn)
    tk = _pick(F, tk or F // 4)
    nk = F // tk

    def body(da_ref, db_ref, w1_ref, w3_ref, dr_ref, o_ref, acc_ref):
        k = pl.program_id(2)

        @pl.when(k == 0)
        def _():
            acc_ref[...] = jnp.zeros_like(acc_ref)

        acc_ref[...] += (_dot(da_ref[...], w1_ref[...], _DIMS["nn"])
                         + _dot(db_ref[...], w3_ref[...], _DIMS["nn"]))

        @pl.when(k == nk - 1)
        def _():
            o_ref[...] = DN_ALPHA * dr_ref[...] + acc_ref[...]

    asp = pl.BlockSpec((tm, tk), lambda i, j, k: (i, k))
    wsp = pl.BlockSpec((tk, tn), lambda i, j, k: (k, j))
    osp = pl.BlockSpec((tm, tn), lambda i, j, k: (i, j))
    return _pcall(
        body, name=name, comm=comm, out_shape=jax.ShapeDtypeStruct((T, D), F32),
        grid=(T // tm, D // tn, nk), in_specs=[asp, asp, wsp, wsp, osp], out_specs=osp,
        scratch_shapes=[pltpu.VMEM((tm, tn), F32)],
        sem=("parallel", "parallel", "arbitrary"),
    )(da, db, w1t, w3t, dr)


def _chunk_mask(reverse, transpose=False):
    row = lax.broadcasted_iota(jnp.int32, (CHUNK, CHUNK), 0)
    col = lax.broadcasted_iota(jnp.int32, (CHUNK, CHUNK), 1)
    if reverse != transpose:
        return col >= row
    return col <= row


def _hgrn_pre(hq, hf, lb):
    sig = _sigmoid(hf)
    f = lb + (1.0 - lb) * sig
    q = hq * _sigmoid(hq) * (HGRN_HEAD_DIM ** -0.5)
    return q, f, sig


def _hgrn_decays(f, cmf, reverse):
    bc = _dot01(cmf, jnp.log(f))
    last = 0 if reverse else CHUNK - 1
    blast = bc[last:last + 1, :]
    bref = bc[CHUNK // 2:CHUNK // 2 + 1, :]
    eq = jnp.exp(jnp.minimum(bc - bref, EXP_CLAMP))
    ek = jnp.exp(jnp.minimum(bref - bc, EXP_CLAMP))
    return bc, blast, eq, ek


def _hgrn_fwd(z, lb, *, wh, reverse, name, comm=None):
    T = z.shape[0]
    nch = T // CHUNK
    nh = wh // HGRN_HEAD_DIM
    hd = HGRN_HEAD_DIM

    def ci(i):
        return nch - 1 - i if reverse else i

    def body(hq_ref, hi_ref, hf_ref, lb_ref, o_ref, st_ref, s_ref):
        @pl.when(pl.program_id(0) == 0)
        def _():
            s_ref[...] = jnp.zeros_like(s_ref)

        cm = _chunk_mask(reverse)
        cmf = cm.astype(F32)
        H = range(nh)
        sls = [slice(h * hd, (h + 1) * hd) for h in H]
        q, f, _ = _hgrn_pre(hq_ref[...], hf_ref[...], lb_ref[...])
        v = hi_ref[...]
        k = 1.0 - f
        bc, blast, eq, ek = _hgrn_decays(f, cmf, reverse)
        qh, kh, qe = q * eq, k * ek, q * jnp.exp(bc)
        k2, eblast = k * jnp.exp(blast - bc), jnp.exp(blast)
        st = [s_ref[h] for h in H]
        att = [jnp.where(cm, _dot_nt(qh[:, sl], kh[:, sl]), 0.0) for sl in sls]
        inter = [_dot_nt(qe[:, sls[h]], st[h]) for h in H]
        out = [inter[h] + _dot_nn(att[h], v[:, sls[h]]) for h in H]
        snew = [eblast[:, sls[h]] * st[h] + _dot_tn(v[:, sls[h]], k2[:, sls[h]]) for h in H]
        for h in H:
            st_ref[0, h] = st[h]
            o_ref[:, sls[h]] = out[h]
            s_ref[h] = snew[h]

    blk = lambda c: pl.BlockSpec((CHUNK, wh), lambda i: (ci(i), c))
    return _pcall(
        body, name=name, comm=comm,
        out_shape=[jax.ShapeDtypeStruct((T, wh), F32),
                   jax.ShapeDtypeStruct((nch, nh, hd, hd), F32)],
        grid=(nch,),
        in_specs=[blk(0), blk(1), blk(3 + int(reverse)),
                  pl.BlockSpec((1, wh), lambda i: (0, 0))],
        out_specs=[blk(0), pl.BlockSpec((1, nh, hd, hd), lambda i: (ci(i), 0, 0, 0))],
        scratch_shapes=[pltpu.VMEM((nh, hd, hd), F32)],
        sem=("arbitrary",),
    )(z, z, z, lb)


def _hgrn_bwd(z, lb, do, states, prev, *, wh, reverse, name, comm=None):
    T = z.shape[0]
    nch = T // CHUNK
    nh = wh // HGRN_HEAD_DIM
    hd = HGRN_HEAD_DIM
    has_prev = prev is not None

    def ci(i):
        return i if reverse else nch - 1 - i

    def body(*refs):
        hq_ref, hi_ref, hf_ref, lb_ref, do_ref, st_ref = refs[:6]
        n_in = 8 if has_prev else 6
        dhq_ref, dhi_ref, dhf_ref, dlb_ref, ds_ref, gs_ref = refs[n_in:]

        @pl.when(pl.program_id(0) == 0)
        def _():
            ds_ref[...] = jnp.zeros_like(ds_ref)
            gs_ref[...] = jnp.zeros_like(gs_ref)
            dlb_ref[...] = jnp.zeros_like(dlb_ref)

        cm = _chunk_mask(reverse)
        cmf = cm.astype(F32)
        cmtf = _chunk_mask(reverse, transpose=True).astype(F32)
        H = range(nh)
        sls = [slice(h * hd, (h + 1) * hd) for h in H]
        hq, lb = hq_ref[...], lb_ref[...]
        q, f, sig = _hgrn_pre(hq, hf_ref[...], lb)
        v, dov = hi_ref[...], do_ref[...]
        k = 1.0 - f
        bc, blast, eq, ek = _hgrn_decays(f, cmf, reverse)
        ebc, eb2, eblast = jnp.exp(bc), jnp.exp(blast - bc), jnp.exp(blast)
        qh, kh, qe, k2 = q * eq, k * ek, q * ebc, k * eb2
        st = [st_ref[0, h] for h in H]
        dst = [ds_ref[h] for h in H]
        att = [jnp.where(cm, _dot_nt(qh[:, sl], kh[:, sl]), 0.0) for sl in sls]
        datt = [jnp.where(cm, _dot_nt(dov[:, sl], v[:, sl]), 0.0) for sl in sls]
        dq_a = [_dot_nn(datt[h], kh[:, sls[h]]) for h in H]
        dq_s = [_dot_nn(dov[:, sls[h]], st[h]) for h in H]
        dk_a = [_dot_tn(datt[h], qh[:, sls[h]]) for h in H]
        dk_s = [_dot_nn(v[:, sls[h]], dst[h]) for h in H]
        dv = [_dot_tn(att[h], dov[:, sls[h]]) + _dot_nt(k2[:, sls[h]], dst[h]) for h in H]
        dsn = [eblast[:, sls[h]] * dst[h] + _dot_tn(dov[:, sls[h]], qe[:, sls[h]]) for h in H]
        for h in H:
            ds_ref[h] = dsn[h]
        dq = jnp.concatenate(dq_a, axis=1) * eq + ebc * jnp.concatenate(dq_s, axis=1)
        dk = jnp.concatenate(dk_a, axis=1) * ek + eb2 * jnp.concatenate(dk_s, axis=1)
        dvv = jnp.concatenate(dv, axis=1)
        db = q * dq - k * dk
        dg = _dot01(cmtf, db) + gs_ref[...]
        gs_ref[...] += jnp.sum(db, axis=0, keepdims=True)
        df = dg / f - dk
        dhf_ref[...] = df * (1.0 - lb) * sig * (1.0 - sig)
        dlb_ref[...] += jnp.sum(df * (1.0 - sig), axis=0, keepdims=True)
        sq = _sigmoid(hq)
        dhq = dq * (HGRN_HEAD_DIM ** -0.5) * sq * (1.0 + hq * (1.0 - sq))
        if has_prev:
            dhq = dhq + refs[6][...]
            dvv = dvv + refs[7][...]
        dhq_ref[...] = dhq
        dhi_ref[...] = dvv

    blk = lambda c: pl.BlockSpec((CHUNK, wh), lambda i: (ci(i), c))
    vec = pl.BlockSpec((1, wh), lambda i: (0, 0))
    in_specs = [blk(0), blk(1), blk(3 + int(reverse)), vec, blk(0),
                pl.BlockSpec((1, nh, hd, hd), lambda i: (ci(i), 0, 0, 0))]
    args = [z, z, z, lb, do, states]
    if has_prev:
        in_specs += [blk(0), blk(0)]
        args += list(prev)
    big = jax.ShapeDtypeStruct((T, wh), F32)
    return _pcall(
        body, name=name, comm=comm,
        out_shape=[big, big, big, jax.ShapeDtypeStruct((1, wh), F32)],
        grid=(nch,), in_specs=in_specs, out_specs=[blk(0), blk(0), blk(0), vec],
        scratch_shapes=[pltpu.VMEM((nh, hd, hd), F32), pltpu.VMEM((1, wh), F32)],
        sem=("arbitrary",),
    )(*args)


def _mlstm_intra(cm, cmt, ig_row, ig_col, xf_row, xf_col, m_st):
    lf_row, lf_col = _log_sigmoid(xf_row), _log_sigmoid(xf_col)
    bcol = jnp.sum(jnp.where(cm, lf_row, 0.0), axis=1, keepdims=True)
    brow = jnp.sum(jnp.where(cmt, lf_col, 0.0), axis=0, keepdims=True)
    blast = jnp.sum(lf_row, axis=1, keepdims=True)
    dmat = jnp.where(cm, bcol - brow + ig_row, NEG)
    m_inter = bcol + m_st
    m_t = jnp.maximum(m_inter, jnp.max(dmat, axis=1, keepdims=True))
    p = jnp.exp(dmat - m_t)
    inter = jnp.exp(m_inter - m_t)
    w_col = blast - bcol + ig_col
    m_new = jnp.maximum(blast + m_st, jnp.max(w_col, axis=0, keepdims=True))
    cs = jnp.exp(blast + m_st - m_new)
    kscale = jnp.exp(w_col - m_new)
    return p, inter, m_t, m_new, cs, kscale


def _mlstm_specs(T, wm, nhm, reverse, backward):
    nch = T // CHUNK
    dm = wm // nhm
    ng = 4 * nhm

    def ci(i):
        fwd_order = nch - 1 - i if reverse else i
        return nch - 1 - fwd_order if backward else fwd_order

    row = lambda w, c: pl.BlockSpec((CHUNK, w), lambda i: (ci(i), c))
    gates_row = pl.BlockSpec((1, ng, CHUNK), lambda i: (ci(i), 0, 0))
    bias_row = pl.BlockSpec((1, LANES), lambda i: (0, 0))
    bias_col = pl.BlockSpec((ng, 1), lambda i: (0, 0))
    st_c = pl.BlockSpec((1, nhm, dm, dm), lambda i: (ci(i), 0, 0, 0))
    st_n = pl.BlockSpec((1, nhm, 1, dm), lambda i: (ci(i), 0, 0, 0))
    st_m = pl.BlockSpec((1, nhm, 1, LANES), lambda i: (ci(i), 0, 0, 0))
    return nch, dm, ng, ci, row, gates_row, bias_row, bias_col, st_c, st_n, st_m


def _mlstm_fwd(qk, z, gc, gr, br, bcl, *, wm, nhm, vcol, reverse, name, comm=None):
    T = qk.shape[0]
    nch, dm, ng, ci, row, gates_row, bias_row, bias_col, st_c, st_n, st_m = _mlstm_specs(
        T, wm, nhm, reverse, False)
    d = int(reverse)

    def body(q_ref, k_ref, v_ref, gc_ref, gr_ref, br_ref, bc_ref,
             h_ref, cst_ref, nst_ref, mst_ref, c_ref, n_ref, m_ref):
        @pl.when(pl.program_id(0) == 0)
        def _():
            c_ref[...] = jnp.zeros_like(c_ref)
            n_ref[...] = jnp.zeros_like(n_ref)
            m_ref[...] = jnp.full_like(m_ref, M_INIT)

        cm = _chunk_mask(reverse)
        cmt = _chunk_mask(reverse, transpose=True)
        G = gc_ref[...] + br_ref[...]
        Gr = gr_ref[0] + bc_ref[...]
        H = range(nhm)
        sls = [slice(h * dm, (h + 1) * dm) for h in H]
        m_all = [m_ref[h] for h in H]
        intra = [_mlstm_intra(cm, cmt, Gr[d * nhm + h:d * nhm + h + 1, :], G[:, d * nhm + h:d * nhm + h + 1],
                              Gr[2 * nhm + d * nhm + h:2 * nhm + d * nhm + h + 1, :],
                              G[:, 2 * nhm + d * nhm + h:2 * nhm + d * nhm + h + 1], m_all[h][:, 0:1])
                 for h in H]
        p, inter, m_t, m_new, cs, kscale = zip(*intra)
        q = [q_ref[:, sl] * (dm ** -0.5) for sl in sls]
        k = [k_ref[:, sl] for sl in sls]
        v = [v_ref[:, sl] for sl in sls]
        ct = [c_ref[h] for h in H]
        n = [n_ref[h] for h in H]
        sc = [_dot_nt(q[h], k[h]) * p[h] for h in H]
        qc = [_dot_nt(q[h], ct[h]) for h in H]
        num = [_dot_nn(sc[h], v[h]) + inter[h] * qc[h] for h in H]
        den = [jnp.sum(sc[h], axis=1, keepdims=True) + inter[h] * jnp.sum(q[h] * n[h], axis=1, keepdims=True)
               for h in H]
        kw = [k[h] * kscale[h] for h in H]
        cnew = [cs[h] * ct[h] + _dot_tn(v[h], kw[h]) for h in H]
        for h in H:
            cst_ref[0, h] = ct[h]
            nst_ref[0, h] = n[h]
            mst_ref[0, h] = m_all[h]
            h_ref[:, sls[h]] = num[h] / jnp.maximum(jnp.abs(den[h]), jnp.exp(-m_t[h]))
            c_ref[h] = cnew[h]
            n_ref[h] = cs[h] * n[h] + jnp.sum(kw[h], axis=0, keepdims=True)
            m_ref[h] = jnp.broadcast_to(m_new[h], (1, LANES))

    return _pcall(
        body, name=name, comm=comm,
        out_shape=[jax.ShapeDtypeStruct((T, wm), F32),
                   jax.ShapeDtypeStruct((nch, nhm, dm, dm), F32),
                   jax.ShapeDtypeStruct((nch, nhm, 1, dm), F32),
                   jax.ShapeDtypeStruct((nch, nhm, 1, LANES), F32)],
        grid=(nch,),
        in_specs=[row(wm, 0), row(wm, 1), row(wm, vcol), row(LANES, 0), gates_row, bias_row, bias_col],
        out_specs=[row(wm, 0), st_c, st_n, st_m],
        scratch_shapes=[pltpu.VMEM((nhm, dm, dm), F32), pltpu.VMEM((nhm, 1, dm), F32),
                        pltpu.VMEM((nhm, 1, LANES), F32)],
        sem=("arbitrary",),
    )(qk, qk, z, gc, gr, br, bcl)


def _mlstm_bwd(qk, z, gc, gr, br, bcl, dh, states, prev, *, wm, nhm, vcol, reverse, name, comm=None):
    T = qk.shape[0]
    nch, dm, ng, ci, row, gates_row, bias_row, bias_col, st_c, st_n, st_m = _mlstm_specs(
        T, wm, nhm, reverse, True)
    d = int(reverse)
    has_prev = prev is not None
    n_in = 13 if has_prev else 11

    def body(*refs):
        (q_ref, k_ref, v_ref, gc_ref, gr_ref, br_ref, bc_ref, dh_ref,
         cst_ref, nst_ref, mst_ref) = refs[:11]
        dqk_ref, dv_ref, dgr_ref, dgs_ref, e_ref, en_ref, fs_ref = refs[n_in:]

        @pl.when(pl.program_id(0) == 0)
        def _():
            e_ref[...] = jnp.zeros_like(e_ref)
            en_ref[...] = jnp.zeros_like(en_ref)
            fs_ref[...] = jnp.zeros_like(fs_ref)
            dgs_ref[...] = jnp.zeros_like(dgs_ref)

        cm = _chunk_mask(reverse)
        cmt = _chunk_mask(reverse, transpose=True)
        row_i = lax.broadcasted_iota(jnp.int32, (CHUNK, CHUNK), 0)
        col_i = lax.broadcasted_iota(jnp.int32, (CHUNK, CHUNK), 1)
        eye = row_i == col_i
        G = gc_ref[...] + br_ref[...]
        Gr = gr_ref[0] + bc_ref[...]
        H = range(nhm)
        sls = [slice(h * dm, (h + 1) * dm) for h in H]
        xf_row = [Gr[2 * nhm + d * nhm + h:2 * nhm + d * nhm + h + 1, :] for h in H]
        intra = [_mlstm_intra(cm, cmt, Gr[d * nhm + h:d * nhm + h + 1, :], G[:, d * nhm + h:d * nhm + h + 1],
                              xf_row[h], G[:, 2 * nhm + d * nhm + h:2 * nhm + d * nhm + h + 1],
                              mst_ref[0, h][:, 0:1]) for h in H]
        p, inter, m_t, _, cs, kscale = zip(*intra)
        q = [q_ref[:, sls[h]] * (dm ** -0.5) for h in H]
        k = [k_ref[:, sls[h]] for h in H]
        v = [v_ref[:, sls[h]] for h in H]
        dhv = [dh_ref[:, sls[h]] for h in H]
        ct = [cst_ref[0, h] for h in H]
        n = [nst_ref[0, h] for h in H]
        et = [e_ref[h] for h in H]
        en = [en_ref[h] for h in H]
        carry = [fs_ref[h][:, 0:1] for h in H]
        sc = [_dot_nt(q[h], k[h]) * p[h] for h in H]
        qc = [_dot_nt(q[h], ct[h]) for h in H]
        num = [_dot_nn(sc[h], v[h]) + inter[h] * qc[h] for h in H]
        den = [jnp.sum(sc[h], axis=1, keepdims=True) + inter[h] * jnp.sum(q[h] * n[h], axis=1, keepdims=True)
               for h in H]
        floor = [jnp.exp(-m_t[h]) for h in H]
        nstab = [jnp.maximum(jnp.abs(den[h]), floor[h]) for h in H]
        gh = [dhv[h] / nstab[h] for h in H]
        dd = [-jnp.sum(dhv[h] * (num[h] / nstab[h]), axis=1, keepdims=True) / nstab[h]
              * jnp.where(jnp.abs(den[h]) > floor[h], jnp.sign(den[h]), 0.0) for h in H]
        dsqk = [(_dot_nt(gh[h], v[h]) + dd[h]) * p[h] for h in H]
        qi = [q[h] * inter[h] for h in H]
        kw = [k[h] * kscale[h] for h in H]
        ghc = [_dot_nn(gh[h], ct[h]) for h in H]
        vet = [_dot_nn(v[h], et[h]) for h in H]
        dq = [_dot_nn(dsqk[h], k[h]) + inter[h] * (ghc[h] + dd[h] * n[h]) for h in H]
        dk = [_dot_tn(dsqk[h], q[h]) + kscale[h] * (vet[h] + en[h]) for h in H]
        dv = [_dot_tn(sc[h], gh[h]) + _dot_nt(kw[h], et[h]) for h in H]
        e_new = [cs[h] * et[h] + _dot_tn(gh[h], qi[h]) for h in H]
        en_new = [cs[h] * en[h] + jnp.sum(qi[h] * dd[h], axis=0, keepdims=True) for h in H]
        di_col = [jnp.sum(k[h] * dk[h], axis=1, keepdims=True) for h in H]
        df_col = [jnp.sum(q[h] * dq[h], axis=1, keepdims=True) - di_col[h] for h in H]
        di_row = [jnp.sum(jnp.where(eye, di_col[h], 0.0), axis=0, keepdims=True) for h in H]
        dlf_row = [jnp.sum(jnp.where(cm, df_col[h], 0.0), axis=0, keepdims=True) + carry[h] for h in H]
        dxf_row = [dlf_row[h] * (1.0 - _sigmoid(xf_row[h])) for h in H]
        for h in H:
            sl = sls[h]
            slk = slice(wm + h * dm, wm + (h + 1) * dm)
            e_ref[h] = e_new[h]
            en_ref[h] = en_new[h]
            fs_ref[h] = jnp.broadcast_to(carry[h] + jnp.sum(df_col[h], axis=0, keepdims=True), (1, LANES))
            dgr_ref[0, h:h + 1, :] = di_row[h]
            dgr_ref[0, nhm + h:nhm + h + 1, :] = dxf_row[h]
            dgs_ref[h:h + 1, :] += jnp.broadcast_to(jnp.sum(di_row[h], axis=1, keepdims=True), (1, LANES))
            dgs_ref[nhm + h:nhm + h + 1, :] += jnp.broadcast_to(
                jnp.sum(dxf_row[h], axis=1, keepdims=True), (1, LANES))
            dqh, dkh, dvh = dq[h] * (dm ** -0.5), dk[h], dv[h]
            if has_prev:
                dqh = dqh + refs[11][:, sl]
                dkh = dkh + refs[11][:, slk]
                dvh = dvh + refs[12][:, sl]
            dqk_ref[:, sl] = dqh
            dqk_ref[:, slk] = dkh
            dv_ref[:, sl] = dvh

    in_specs = [row(wm, 0), row(wm, 1), row(wm, vcol), row(LANES, 0), gates_row, bias_row, bias_col,
                row(wm, 0), st_c, st_n, st_m]
    args = [qk, qk, z, gc, gr, br, bcl, dh] + list(states)
    if has_prev:
        in_specs += [row(2 * wm, 0), row(wm, 0)]
        args += list(prev)
    return _pcall(
        body, name=name, comm=comm,
        out_shape=[jax.ShapeDtypeStruct((T, 2 * wm), F32), jax.ShapeDtypeStruct((T, wm), F32),
                   jax.ShapeDtypeStruct((nch, 2 * nhm, CHUNK), F32),
                   jax.ShapeDtypeStruct((2 * nhm, LANES), F32)],
        grid=(nch,), in_specs=in_specs,
        out_specs=[row(2 * wm, 0), row(wm, 0),
                   pl.BlockSpec((1, 2 * nhm, CHUNK), lambda i: (ci(i), 0, 0)),
                   pl.BlockSpec((2 * nhm, LANES), lambda i: (0, 0))],
        scratch_shapes=[pltpu.VMEM((nhm, dm, dm), F32), pltpu.VMEM((nhm, 1, dm), F32),
                        pltpu.VMEM((nhm, 1, LANES), F32)],
        sem=("arbitrary",),
    )(*args)


def _conv_taps(x, w_ref):
    T = x.shape[0]
    t = lax.broadcasted_iota(jnp.int32, x.shape, 0)
    taps = []
    acc = None
    for j in range(CONV_WIDTH):
        s = CONV_WIDTH // 2 - j
        if s == 0:
            xs = x
        else:
            xs = jnp.where((t - s >= 0) & (t - s < T), pltpu.roll(x, s % T, 0), 0.0)
        taps.append(xs)
        term = w_ref[j:j + 1, :] * xs
        acc = term if acc is None else acc + term
    return taps, acc


def _conv_fwd(z, w, b, *, col0, tc=LANES, name):
    T = z.shape[0]
    C2 = w.shape[1]
    assert col0 % tc == 0 and C2 % tc == 0

    def body(z_ref, w_ref, b_ref, o_ref):
        _, acc = _conv_taps(z_ref[...], w_ref)
        c = acc + b_ref[...]
        o_ref[...] = c * _sigmoid(c)

    return pl.pallas_call(
        body, name=name, out_shape=jax.ShapeDtypeStruct((T, C2), F32),
        grid=(C2 // tc,),
        in_specs=[pl.BlockSpec((T, tc), lambda j: (0, col0 // tc + j)),
                  pl.BlockSpec((CONV_WIDTH, tc), lambda j: (0, j)),
                  pl.BlockSpec((1, tc), lambda j: (0, j))],
        out_specs=pl.BlockSpec((T, tc), lambda j: (0, j)),
        compiler_params=_params("parallel"),
    )(z, w, b)


def _conv_bwd(dy, z, w, b, *, col0, tc=LANES, name):
    T = z.shape[0]
    C2 = w.shape[1]

    def body(dy_ref, z_ref, w_ref, b_ref, dx_ref, dw_ref, db_ref):
        taps, acc = _conv_taps(z_ref[...], w_ref)
        c = acc + b_ref[...]
        sg = _sigmoid(c)
        dc = dy_ref[...] * sg * (1.0 + c * (1.0 - sg))
        t = lax.broadcasted_iota(jnp.int32, dc.shape, 0)
        dx = None
        for j in range(CONV_WIDTH):
            s = j - CONV_WIDTH // 2
            if s == 0:
                ds = dc
            else:
                ds = jnp.where((t - s >= 0) & (t - s < T), pltpu.roll(dc, s % T, 0), 0.0)
            term = w_ref[j:j + 1, :] * ds
            dx = term if dx is None else dx + term
            dw_ref[j:j + 1, :] = jnp.sum(dc * taps[j], axis=0, keepdims=True)
        dx_ref[...] = dx
        db_ref[...] = jnp.sum(dc, axis=0, keepdims=True)

    col = pl.BlockSpec((T, tc), lambda j: (0, j))
    wsp = pl.BlockSpec((CONV_WIDTH, tc), lambda j: (0, j))
    bsp = pl.BlockSpec((1, tc), lambda j: (0, j))
    return pl.pallas_call(
        body, name=name,
        out_shape=[jax.ShapeDtypeStruct((T, C2), F32), jax.ShapeDtypeStruct((CONV_WIDTH, C2), F32),
                   jax.ShapeDtypeStruct((1, C2), F32)],
        grid=(C2 // tc,),
        in_specs=[col, pl.BlockSpec((T, tc), lambda j: (0, col0 // tc + j)), wsp, bsp],
        out_specs=[col, wsp, bsp],
        compiler_params=_params("parallel"),
    )(dy, z, w, b)


def _mix_heads(o_fw, o_bw, h_fw, h_bw, wh, wm, nhm):
    out = []
    hd = HGRN_HEAD_DIM
    for h in range(wh // hd):
        sl = slice(h * hd, (h + 1) * hd)
        o = o_fw[:, sl] + o_bw[:, sl]
        r = lax.rsqrt(jnp.mean(o * o, axis=-1, keepdims=True) + NORM_EPS)
        out.append((0, sl, o * r, r))
    dm = wm // nhm
    for h in range(nhm):
        sl = slice(h * dm, (h + 1) * dm)
        x = h_fw[:, sl] + h_bw[:, sl]
        xc = x - jnp.mean(x, axis=-1, keepdims=True)
        r = lax.rsqrt(jnp.mean(xc * xc, axis=-1, keepdims=True) + NORM_EPS)
        out.append((1, sl, xc * r, r))
    return out


def _mix_specs(T, wh, wm, tm, gcol, ocol):
    rowh = pl.BlockSpec((tm, wh), lambda i: (i, 0))
    rowm = pl.BlockSpec((tm, wm), lambda i: (i, 0))
    hg = pl.BlockSpec((tm, wh), lambda i: (i, gcol))
    mo = pl.BlockSpec((tm, wm), lambda i: (i, ocol))
    vh = pl.BlockSpec((1, wh), lambda i: (0, 0))
    vm = pl.BlockSpec((1, wm), lambda i: (0, 0))
    return rowh, rowm, hg, mo, vh, vm


def _mix_fwd(o_fw, o_bw, h_fw, h_bw, z, gh, gm, *, nhm, gcol, ocol, tm=256, name):
    T, wh = o_fw.shape
    wm = h_fw.shape[1]
    tm = min(tm, T)
    rowh, rowm, hg, mo, vh, vm = _mix_specs(T, wh, wm, tm, gcol, ocol)

    def body(of_ref, ob_ref, hf_ref, hb_ref, hg_ref, mo_ref, gh_ref, gm_ref, y_ref, yt_ref):
        heads = _mix_heads(of_ref[...], ob_ref[...], hf_ref[...], hb_ref[...], wh, wm, nhm)
        for grp, sl, nv, _ in heads:
            if grp == 0:
                gate = hg_ref[:, sl]
                gate = gate * _sigmoid(gate)
                y = nv * gh_ref[:, sl] * gate
                osl = sl
            else:
                y = nv * gm_ref[:, sl] * _sigmoid(mo_ref[:, sl])
                osl = slice(wh + sl.start, wh + sl.stop)
            y_ref[:, osl] = y.astype(BF16)
            yt_ref[osl, :] = y.T.astype(BF16)

    return pl.pallas_call(
        body, name=name,
        out_shape=[jax.ShapeDtypeStruct((T, wh + wm), BF16), jax.ShapeDtypeStruct((wh + wm, T), BF16)],
        grid=(T // tm,),
        in_specs=[rowh, rowh, rowm, rowm, hg, mo, vh, vm],
        out_specs=[pl.BlockSpec((tm, wh + wm), lambda i: (i, 0)),
                   pl.BlockSpec((wh + wm, tm), lambda i: (0, i))],
        compiler_params=_params("parallel"),
    )(o_fw, o_bw, h_fw, h_bw, z, z, gh, gm)


def _mix_bwd(dy, o_fw, o_bw, h_fw, h_bw, z, gh, gm, *, nhm, gcol, ocol, tm=256, name, comm=None):
    T, wh = o_fw.shape
    wm = h_fw.shape[1]
    tm = min(tm, T)
    rowh, rowm, hg, mo, vh, vm = _mix_specs(T, wh, wm, tm, gcol, ocol)

    def body(dy_ref, of_ref, ob_ref, hf_ref, hb_ref, hg_ref, mo_ref, gh_ref, gm_ref,
             do_ref, dh_ref, dhg_ref, dmo_ref, dgh_ref, dgm_ref):
        @pl.when(pl.program_id(0) == 0)
        def _():
            dgh_ref[...] = jnp.zeros_like(dgh_ref)
            dgm_ref[...] = jnp.zeros_like(dgm_ref)

        heads = _mix_heads(of_ref[...], ob_ref[...], hf_ref[...], hb_ref[...], wh, wm, nhm)
        for grp, sl, nv, r in heads:
            if grp == 0:
                d = dy_ref[:, sl]
                x = hg_ref[:, sl]
                sg = _sigmoid(x)
                gate = x * sg
                g = gh_ref[:, sl]
                dgh_ref[:, sl] += jnp.sum(d * nv * gate, axis=0, keepdims=True)
                dhg_ref[:, sl] = d * nv * g * sg * (1.0 + x * (1.0 - sg))
                t = d * g * gate
                do_ref[:, sl] = r * (t - nv * jnp.mean(t * nv, axis=-1, keepdims=True))
            else:
                d = dy_ref[:, slice(wh + sl.start, wh + sl.stop)]
                sg = _sigmoid(mo_ref[:, sl])
                g = gm_ref[:, sl]
                dgm_ref[:, sl] += jnp.sum(d * nv * sg, axis=0, keepdims=True)
                dmo_ref[:, sl] = d * nv * g * sg * (1.0 - sg)
                t = d * g * sg
                dh_ref[:, sl] = r * (t - jnp.mean(t, axis=-1, keepdims=True)
                                     - nv * jnp.mean(t * nv, axis=-1, keepdims=True))

    return _pcall(
        body, name=name, comm=comm,
        out_shape=[jax.ShapeDtypeStruct((T, wh), F32), jax.ShapeDtypeStruct((T, wm), F32),
                   jax.ShapeDtypeStruct((T, wh), F32), jax.ShapeDtypeStruct((T, wm), F32),
                   jax.ShapeDtypeStruct((1, wh), F32), jax.ShapeDtypeStruct((1, wm), F32)],
        grid=(T // tm,),
        in_specs=[pl.BlockSpec((tm, wh + wm), lambda i: (i, 0)), rowh, rowh, rowm, rowm, hg, mo, vh, vm],
        out_specs=[rowh, rowm, rowh, rowm, vh, vm],
        sem=("arbitrary",),
    )(dy, o_fw, o_bw, h_fw, h_bw, z, z, gh, gm)


def _pick(n, pref):
    for c in range(pref - pref % LANES, 0, -LANES):
        if n % c == 0:
            return c
    return n


def _mm(a, b, mode="nn", *, tm=1024, tn=1024, tk=2048, **kw):
    if mode == "nn":
        (M, K), N = a.shape, b.shape[1]
    else:
        (M, K), N = a.shape, kw.get("rows") or b.shape[0]
    return _matmul(a, b, mode=mode, tm=_pick(M, tm), tn=_pick(N, tn), tk=_pick(K, tk), **kw)


def _device_step(x, target, sh, small_parts, small_params, core):
    T, D = x.shape

    def full(g):
        return g.reshape(-1, D)

    def shards(g):
        return g.reshape(N_DEV, -1, D)

    def pair_add(name, g, land):
        return _pair_add(g, land, core, "rs_add_" + name)

    def dw(lhs_t, rhs, name, scale=1.0):
        return shards(_mm(lhs_t, rhs, out_dtype=BF16, scale=scale, tm=704, tk=T, name=name))

    g_w1a, g_w3a, g_small = _comm_call(_ag_job([sh["ffn1_w1"], sh["ffn1_w3"], small_parts]), "ag_ffn1")
    w1a, w3a = full(g_w1a), full(g_w3a)
    P = small_params(g_small)
    wh = P["hgrn_norm_g"].shape[1]
    wm = P["mlstm_norm_g"].shape[1]
    nhm = P["ig_b"].shape[1]
    ng = 4 * nhm
    nch = T // CHUNK
    assert wh == wm and T % CHUNK == 0
    vcol, ocol, gcol = 7, 8, 2
    col0 = 5 * wh
    nz = 5 * wh + 4 * wm
    xb = _cast(x, name="x_cast")
    (a1, b1, h1, h1t), (g_w2a,) = _ffn_up(xb, w1a, w3a, name="ffn1_up", comm=_ag_job([sh["ffn1_w2"]]))
    w2a = full(g_w2a)
    r1, (g_win,) = _mm(h1, w2a, add=x, scale=0.5, add_scale=DN_ALPHA, tk=w2a.shape[0] // 4,
                       name="ffn1_down", comm=_ag_job([sh["w_in"]]))
    w_int = full(g_win)
    w_wgt = jnp.zeros((LANES, D), BF16).at[:ng].set(w_int[nz:])
    x1, x1b = _ln_fwd(r1, P["ln1_g"], P["ln1_b"], name="ln1")
    zm, (g_wout,) = _mm(x1b, w_int, "nt", rows=nz, name="zm", comm=_ag_job([sh["w_out"]]))
    w_wout = full(g_wout)
    zg = _mm(x1b, w_wgt, "nt", name="zg")
    gr = zg[:, :ng].reshape(nch, CHUNK, ng).transpose(0, 2, 1)
    bias = jnp.concatenate([P["ig_b"].reshape(-1), P["fg_b"].reshape(-1)])
    br = jnp.zeros((1, LANES), F32).at[0, :ng].set(bias)
    bcl = bias.reshape(ng, 1)
    lbf, lbb = P["lb"][0:1], P["lb"][1:2]
    (o_fw, s_fw), (g_w1b,) = _hgrn_fwd(zm, lbf, wh=wh, reverse=False, name="hgrn_fw",
                                       comm=_ag_job([sh["ffn2_w1"]]))
    (o_bw, s_bw), (g_w3b,) = _hgrn_fwd(zm, lbb, wh=wh, reverse=True, name="hgrn_bw",
                                       comm=_ag_job([sh["ffn2_w3"]]))
    qk = _conv_fwd(zm, P["conv_w"], P["conv_b"], col0=col0, name="conv")
    (h_fw, *st_fw), (g_w2b,) = _mlstm_fwd(qk, zm, zg, gr, br, bcl, wm=wm, nhm=nhm, vcol=vcol,
                                          reverse=False, name="mlstm_fw", comm=_ag_job([sh["ffn2_w2"]]))
    w1b, w3b, w2b = full(g_w1b), full(g_w3b), full(g_w2b)
    h_bw, *st_bw = _mlstm_fwd(qk, zm, zg, gr, br, bcl, wm=wm, nhm=nhm, vcol=vcol, reverse=True,
                              name="mlstm_bw")
    y, yt = _mix_fwd(o_fw, o_bw, h_fw, h_bw, zm, P["hgrn_norm_g"], P["mlstm_norm_g"],
                     nhm=nhm, gcol=gcol, ocol=ocol, name="mix")
    r2 = _mm(y, w_wout, add=x1, add_scale=DN_ALPHA, name="r2")
    x2, x2b = _ln_fwd(r2, P["ln2_g"], P["ln2_b"], name="ln2")
    a2, b2, h2, h2t = _ffn_up(x2b, w1b, w3b, name="ffn2_up")
    r3 = _mm(h2, w2b, add=x2, scale=0.5, add_scale=DN_ALPHA, tk=w2b.shape[0] // 4, name="ffn2_down")

    G, land = {}, {}
    dr3, dr3b, G["ln3_g"], G["ln3_b"], loss = _ln_bwd(
        None, r3, P["ln3_g"], b=P["ln3_b"], target=target, name="ln3_bwd")
    da2, db2, da2t, db2t = _ffn_bwd_act(dr3b, w2b, a2, b2, name="ffn2_dact")
    gw1b, gw3b = dw(da2t, x2b, "ffn2_dw1"), dw(db2t, x2b, "ffn2_dw3")
    gw2b = dw(h2t, dr3b, "ffn2_dw2", scale=0.5)
    dx2, l1 = _ffn_dx(da2, db2, w1b, w3b, dr3, name="ffn2_dx", comm=_pair_job([gw1b, gw3b, gw2b]))
    p_w1b, p_w3b, p_w2b = [pair_add(k, g, l) for k, g, l in
                           zip(("ffn2_w1", "ffn2_w3", "ffn2_w2"), (gw1b, gw3b, gw2b), l1)]
    dr2, dr2b, G["ln2_g"], G["ln2_b"] = _ln_bwd(dx2, r2, P["ln2_g"], name="ln2_bwd")
    dy = _mm(dr2b, w_wout, "nt", name="dy")
    gwout = shards(_mm(yt, dr2b, out_dtype=BF16, name="dwout"))
    (do, dh, dhg, dmo, G["hgrn_norm_g"], G["mlstm_norm_g"]), (l1,) = _mix_bwd(
        dy, o_fw, o_bw, h_fw, h_bw, zm, P["hgrn_norm_g"], P["mlstm_norm_g"],
        nhm=nhm, gcol=gcol, ocol=ocol, name="mix_bwd", comm=_pair_job([gwout]))
    p_wout = pair_add("w_out", gwout, l1)
    (dqk_f, dv_f, dgr_f, dgs_f), (land["ffn2_w1"],) = _mlstm_bwd(
        qk, zm, zg, gr, br, bcl, dh, st_fw, None, wm=wm, nhm=nhm, vcol=vcol, reverse=False,
        name="mlstm_fw_bwd", comm=_chips_job([p_w1b]))
    (dqk, dv, dgr_b, dgs_b), (land["ffn2_w3"],) = _mlstm_bwd(
        qk, zm, zg, gr, br, bcl, dh, st_bw, (dqk_f, dv_f), wm=wm, nhm=nhm, vcol=vcol, reverse=True,
        name="mlstm_bw_bwd", comm=_chips_job([p_w3b]))
    dmqk, G["conv_w"], G["conv_b"] = _conv_bwd(dqk, zm, P["conv_w"], P["conv_b"], col0=col0,
                                               name="conv_bwd")
    (dhq_f, dhi_f, dhf_fw, dlb_f), (land["ffn2_w2"],) = _hgrn_bwd(
        zm, lbf, do, s_fw, None, wh=wh, reverse=False, name="hgrn_fw_bwd", comm=_chips_job([p_w2b]))
    (dhq, dhi, dhf_bw, dlb_b), (land["w_out"],) = _hgrn_bwd(
        zm, lbb, do, s_bw, (dhq_f, dhi_f), wh=wh, reverse=True, name="hgrn_bw_bwd",
        comm=_chips_job([p_wout]))
    G["lb"] = jnp.concatenate([dlb_f, dlb_b], axis=0)
    G["ig_b"] = jnp.stack([dgs_f[:nhm, 0], dgs_b[:nhm, 0]])
    G["fg_b"] = jnp.stack([dgs_f[nhm:, 0], dgs_b[nhm:, 0]])
    dzm, dzmt = _dz_pack([dhq, dhi, dhg, dhf_fw, dhf_bw, dmqk, dv, dmo], name="dz_pack")
    dgr = jnp.concatenate([dgr_f[:, :nhm], dgr_b[:, :nhm], dgr_f[:, nhm:], dgr_b[:, nhm:]], axis=1)
    dzgt = jnp.zeros((LANES, T), F32).at[:ng].set(dgr.transpose(1, 0, 2).reshape(ng, T)).astype(BF16)
    dzg = jnp.zeros((T, LANES), F32).at[:, :ng].set(dgr.transpose(0, 2, 1).reshape(T, ng)).astype(BF16)
    gwin = shards(jnp.concatenate([_mm(dzmt, x1b, out_dtype=BF16, tk=T, name="dwin"),
                                   _mm(dzgt, x1b, out_dtype=BF16, tk=T, name="dwg")[:ng]], axis=0))
    t, (l1,) = _mm(dzg, w_wgt, add=dr2, add_scale=DN_ALPHA, name="dx1_g", comm=_pair_job([gwin]))
    p_win = pair_add("w_in", gwin, l1)
    dx1, (land["w_in"],) = _mm(dzm, w_int, rows=nz, add=t, tk=1536, name="dx1", comm=_chips_job([p_win]))
    dr1, dr1b, G["ln1_g"], G["ln1_b"] = _ln_bwd(dx1, r1, P["ln1_g"], name="ln1_bwd")
    gw2a = dw(h1t, dr1b, "ffn1_dw2", scale=0.5)
    (da1, db1, da1t, db1t), (l1,) = _ffn_bwd_act(dr1b, w2a, a1, b1, name="ffn1_dact",
                                                 comm=_pair_job([gw2a]))
    p_w2a = pair_add("ffn1_w2", gw2a, l1)
    gw1a, (land["ffn1_w2"],) = _mm(da1t, xb, out_dtype=BF16, tm=704, tk=T, name="ffn1_dw1",
                                   comm=_chips_job([p_w2a]))
    gw1a = shards(gw1a)
    gw3a, (l1,) = _mm(db1t, xb, out_dtype=BF16, tm=704, tk=T, name="ffn1_dw3", comm=_pair_job([gw1a]))
    gw3a = shards(gw3a)
    p_w1a = pair_add("ffn1_w1", gw1a, l1)
    (l1,) = _comm_call(_pair_job([gw3a]), "rs_pair_ffn1_w3")
    p_w3a = pair_add("ffn1_w3", gw3a, l1)
    gx, (land["ffn1_w1"], land["ffn1_w3"]) = _ffn_dx(
        da1, db1, w1a, w3a, dr1, name="ffn1_dx", comm=_chips_job([p_w1a, p_w3a]))
    return loss, gx, land, G


ANY = pl.BlockSpec(memory_space=pl.ANY)


def _place():
    x, y, c = lax.axis_index("x"), lax.axis_index("y"), lax.axis_index("c")
    chips = [(1 - x, y), (x, 1 - y), (1 - x, 1 - y)]
    return x, y, c, chips


def _comm_call(job, name):
    n_in, n_out = len(job.inputs), len(job.out_shape)

    def body(*refs):
        ins, outs, sems = refs[:n_in], refs[n_in:n_in + n_out], refs[n_in + n_out:]
        job.start(ins, outs, sems)
        job.finish(ins, outs, sems)

    return pl.pallas_call(body, name=name, out_shape=job.out_shape, in_specs=[ANY] * n_in,
                          out_specs=[ANY] * n_out, scratch_shapes=job.sems)(*job.inputs)


def _ag_job(parts):
    n = len(parts)

    def copies(ins, outs, sems):
        send_sems, recv_sems, local_sems = sems
        x, y, c, chips = _place()
        me, sibling = (x, y, c), (x, y, 1 - c)

        def copy(a, k, block, to, src=None):
            bx, by, bc = block
            slot = outs[a].at[4 * bx + 2 * by + bc]
            return pltpu.make_async_remote_copy(
                src_ref=slot if src is None else src, dst_ref=slot,
                send_sem=send_sems.at[a, k], recv_sem=recv_sems.at[a, k],
                device_id=to, device_id_type=MESH)

        local = [pltpu.make_async_copy(ins[a], outs[a].at[4 * x + 2 * y + c], local_sems.at[a])
                 for a in range(n)]
        first = []
        for a in range(n):
            first.append(copy(a, 0, me, sibling, src=ins[a]))
            first += [copy(a, 1 + j, me, (*chip, c), src=ins[a]) for j, chip in enumerate(chips)]
        return copy, local, first, chips, me, sibling, c

    def start(ins, outs, sems):
        _, local, first, *_ = copies(ins, outs, sems)
        for cp in local + first:
            cp.start()

    def finish(ins, outs, sems):
        copy, local, first, chips, me, sibling, c = copies(ins, outs, sems)
        passed = []
        for a in range(n):
            for j, chip in enumerate(chips):
                copy(a, 1 + j, (*chip, c), me).wait_recv()
                fwd = copy(a, 4 + j, (*chip, c), sibling)
                fwd.start()
                passed.append(fwd)
        for a in range(n):
            copy(a, 0, sibling, me).wait_recv()
            for j, chip in enumerate(chips):
                copy(a, 4 + j, (*chip, 1 - c), me).wait_recv()
        for cp in first + passed:
            cp.wait_send()
        for cp in local:
            cp.wait()

    return _Job(parts, [jax.ShapeDtypeStruct((N_DEV,) + p.shape, p.dtype) for p in parts],
                [pltpu.SemaphoreType.DMA((n, 7)), pltpu.SemaphoreType.DMA((n, 7)),
                 pltpu.SemaphoreType.DMA((n,))], start, finish)


def _pair_job(gs):
    n = len(gs)

    def copies(g_refs, land_refs, sems):
        send_sems, recv_sems = sems
        x, y, c, _ = _place()
        return [pltpu.make_async_remote_copy(
            src_ref=g_refs[a].at[2 * k + 1 - c], dst_ref=land_refs[a].at[k],
            send_sem=send_sems.at[a, k], recv_sem=recv_sems.at[a, k],
            device_id=(x, y, 1 - c), device_id_type=MESH) for a in range(n) for k in range(4)]

    def start(ins, outs, sems):
        for cp in copies(ins, outs, sems):
            cp.start()

    def finish(ins, outs, sems):
        for cp in copies(ins, outs, sems):
            cp.wait()

    return _Job(gs, [jax.ShapeDtypeStruct((4,) + g.shape[1:], g.dtype) for g in gs],
                [pltpu.SemaphoreType.DMA((n, 4)), pltpu.SemaphoreType.DMA((n, 4))], start, finish)


def _chips_job(ps):
    n = len(ps)

    def copies(p_refs, land_refs, sems):
        send_sems, recv_sems, local_sems = sems
        x, y, c, chips = _place()
        mine = 2 * x + y
        owns = [pltpu.make_async_copy(p_refs[a].at[mine], land_refs[a].at[mine], local_sems.at[a])
                for a in range(n)]
        sends = [pltpu.make_async_remote_copy(
            src_ref=p_refs[a].at[2 * chip[0] + chip[1]], dst_ref=land_refs[a].at[mine],
            send_sem=send_sems.at[a, j], recv_sem=recv_sems.at[a, j],
            device_id=(*chip, c), device_id_type=MESH) for a in range(n) for j, chip in enumerate(chips)]
        recvs = [pltpu.make_async_remote_copy(
            src_ref=p_refs[a].at[mine], dst_ref=land_refs[a].at[2 * chip[0] + chip[1]],
            send_sem=send_sems.at[a, j], recv_sem=recv_sems.at[a, j],
            device_id=(*chip, c), device_id_type=MESH) for a in range(n) for j, chip in enumerate(chips)]
        return owns, sends, recvs

    def start(ins, outs, sems):
        owns, sends, _ = copies(ins, outs, sems)
        for cp in owns + sends:
            cp.start()

    def finish(ins, outs, sems):
        owns, sends, recvs = copies(ins, outs, sems)
        for cp in recvs:
            cp.wait_recv()
        for cp in sends:
            cp.wait_send()
        for cp in owns:
            cp.wait()

    return _Job(ps, [jax.ShapeDtypeStruct(p.shape, p.dtype) for p in ps],
                [pltpu.SemaphoreType.DMA((n, 3)), pltpu.SemaphoreType.DMA((n, 3)),
                 pltpu.SemaphoreType.DMA((n,))], start, finish)


def _all_reduce_small(buf, name):
    rows = buf.shape[0]

    def body(b_ref, o_ref, slots, send_sems, recv_sems):
        x, y, c, _ = _place()
        me = 4 * x + 2 * y + c
        slots[me] = b_ref[...]
        cps = []
        for k in range(1, N_DEV):
            fx, fy, fc = (k >> 2) & 1, (k >> 1) & 1, k & 1
            peer = (x ^ fx, y ^ fy, c ^ fc)
            cps.append(pltpu.make_async_remote_copy(
                src_ref=b_ref, dst_ref=slots.at[me],
                send_sem=send_sems.at[k - 1], recv_sem=recv_sems.at[k - 1],
                device_id=peer, device_id_type=MESH))
        for cp in cps:
            cp.start()
        for k in range(1, N_DEV):
            fx, fy, fc = (k >> 2) & 1, (k >> 1) & 1, k & 1
            src = 4 * (x ^ fx) + 2 * (y ^ fy) + (c ^ fc)
            pltpu.make_async_remote_copy(
                src_ref=b_ref, dst_ref=slots.at[src],
                send_sem=send_sems.at[k - 1], recv_sem=recv_sems.at[k - 1],
                device_id=(x ^ fx, y ^ fy, c ^ fc), device_id_type=MESH).wait_recv()
        for cp in cps:
            cp.wait_send()
        acc = slots[0]
        for k in range(1, N_DEV):
            acc = acc + slots[k]
        o_ref[...] = acc

    vm = pl.BlockSpec(memory_space=pltpu.VMEM)
    return pl.pallas_call(
        body, name=name, out_shape=jax.ShapeDtypeStruct(buf.shape, F32),
        in_specs=[vm], out_specs=vm,
        scratch_shapes=[pltpu.VMEM((N_DEV, rows, LANES), F32),
                        pltpu.SemaphoreType.DMA((N_DEV - 1,)), pltpu.SemaphoreType.DMA((N_DEV - 1,))],
    )(buf)


def _row_tile(rows, pref=2048):
    for t in range(min(pref, rows) - min(pref, rows) % 8, 0, -8):
        if rows % t == 0:
            return t
    return rows


def _tile2(rows, cols, nbuf):
    budget = VMEM_LIMIT // 2 // (2 * nbuf * 4)
    for t in range(min(512, rows) // 16 * 16, 0, -16):
        if rows % t == 0 and t * cols <= budget:
            return t, cols
    return rows, _pick(cols, max(LANES, budget // rows // LANES * LANES))


def _pair_add(g, land, core, name):
    _, r, c = g.shape
    tr, tc = _tile2(r, c, 3)

    def body(core_ref, g_ref, l_ref, o_ref):
        o_ref[...] = (g_ref[...].astype(F32) + l_ref[...].astype(F32)).astype(o_ref.dtype)

    blk = pl.BlockSpec((1, tr, tc), lambda k, i, j, cr: (k, i, j))
    return pl.pallas_call(
        body, name=name, out_shape=jax.ShapeDtypeStruct(land.shape, land.dtype),
        grid_spec=pltpu.PrefetchScalarGridSpec(
            num_scalar_prefetch=1, grid=(4, r // tr, c // tc),
            in_specs=[pl.BlockSpec((1, tr, tc), lambda k, i, j, cr: (2 * k + cr[0], i, j)), blk],
            out_specs=blk),
        compiler_params=_params("parallel", "parallel", "parallel"),
    )(core, g, land)


def _adamw(w, parts, m, v, name):
    shp = w.shape
    cols = shp[-1]
    w2, m2, v2 = (t.reshape(-1, cols) for t in (w, m, v))
    rows = w2.shape[0]
    n = parts.shape[0]
    assert parts.shape[1:] == (rows, cols), (parts.shape, shp)
    tr, tc = _tile2(rows, cols, n + 7)
    c1 = 1.0 / (1.0 - ADAM_B1 ** ADAM_STEP)
    c2 = 1.0 / (1.0 - ADAM_B2 ** ADAM_STEP)

    def body(*refs):
        p_refs = refs[:n]
        w_ref, m_ref, v_ref, g_ref, d_ref, nm_ref, nv_ref = refs[n:]
        gv = p_refs[0][0].astype(F32)
        for p_ref in p_refs[1:]:
            gv = gv + p_ref[0].astype(F32)
        nm = ADAM_B1 * m_ref[...] + (1.0 - ADAM_B1) * gv
        nv = ADAM_B2 * v_ref[...] + (1.0 - ADAM_B2) * (gv * gv)
        g_ref[...] = gv
        nm_ref[...] = nm
        nv_ref[...] = nv
        d_ref[...] = -ADAM_LR * ((nm * c1) / (jnp.sqrt(nv * c2) + ADAM_EPS) + ADAM_WD * w_ref[...])

    blk = pl.BlockSpec((tr, tc), lambda i, j: (i, j))
    slab = lambda k: pl.BlockSpec((1, tr, tc), lambda i, j: (k, i, j))
    out = jax.ShapeDtypeStruct((rows, cols), F32)
    outs = pl.pallas_call(
        body, name=name, out_shape=[out] * 4, grid=(rows // tr, cols // tc),
        in_specs=[slab(k) for k in range(n)] + [blk] * 3, out_specs=[blk] * 4,
        compiler_params=_params("parallel", "parallel"),
    )(*([parts] * n), w2, m2, v2)
    return tuple(o.reshape(shp) for o in outs)


def _pack_rows(arrs):
    rows = []
    for a in arrs:
        flat = a.reshape(-1).astype(F32)
        pad = (-flat.shape[0]) % LANES
        if pad:
            flat = jnp.concatenate([flat, jnp.zeros((pad,), F32)])
        rows.append(flat.reshape(-1, LANES))
    out = jnp.concatenate(rows, axis=0)
    pad = (-out.shape[0]) % 8
    if pad:
        out = jnp.concatenate([out, jnp.zeros((pad, LANES), F32)], axis=0)
    return out


def _unpack_rows(buf, shapes):
    outs, r = [], 0
    for s in shapes:
        n = math.prod(s)
        nr = -(-n // LANES)
        outs.append(buf[r:r + nr].reshape(-1)[:n].reshape(s))
        r += nr
    return outs


BIG = ("ffn1_w1", "ffn1_w3", "ffn1_w2", "w_in", "w_out", "ffn2_w1", "ffn2_w3", "ffn2_w2")
ROW_SHARDED = ("ffn1_w2", "w_out", "ffn2_w2")
SMALL = ("ln1_g", "ln1_b", "hgrn_lb", "hgrn_norm_g", "mlstm_conv_w", "mlstm_conv_b", "mlstm_ig_b",
         "mlstm_fg_b", "mlstm_norm_g", "ln2_g", "ln2_b", "ln3_g", "ln3_b")
WEIGHTS = ("ffn1_w1", "ffn1_w3", "ffn1_w2", "ln1_g", "ln1_b", "w_in", "hgrn_lb", "hgrn_norm_g",
           "mlstm_conv_w", "mlstm_conv_b", "mlstm_ig_b", "mlstm_fg_b", "mlstm_norm_g", "w_out",
           "ln2_g", "ln2_b", "ffn2_w1", "ffn2_w3", "ffn2_w2", "ln3_g", "ln3_b")


def kernel(x, ffn1_w1, ffn1_w3, ffn1_w2, ln1_g, ln1_b, w_in, hgrn_lb, hgrn_norm_g, mlstm_conv_w, mlstm_conv_b, mlstm_ig_b, mlstm_fg_b, mlstm_norm_g, w_out, ln2_g, ln2_b, ffn2_w1, ffn2_w3, ffn2_w2, ln3_g, ln3_b, loss_target, m_ffn1_w1, m_ffn1_w3, m_ffn1_w2, m_ln1_g, m_ln1_b, m_w_in, m_hgrn_lb, m_hgrn_norm_g, m_mlstm_conv_w, m_mlstm_conv_b, m_mlstm_ig_b, m_mlstm_fg_b, m_mlstm_norm_g, m_w_out, m_ln2_g, m_ln2_b, m_ffn2_w1, m_ffn2_w3, m_ffn2_w2, m_ln3_g, m_ln3_b, v_ffn1_w1, v_ffn1_w3, v_ffn1_w2, v_ln1_g, v_ln1_b, v_w_in, v_hgrn_lb, v_hgrn_norm_g, v_mlstm_conv_w, v_mlstm_conv_b, v_mlstm_ig_b, v_mlstm_fg_b, v_mlstm_norm_g, v_w_out, v_ln2_g, v_ln2_b, v_ffn2_w1, v_ffn2_w3, v_ffn2_w2, v_ln3_g, v_ln3_b):
    args = (ffn1_w1, ffn1_w3, ffn1_w2, ln1_g, ln1_b, w_in, hgrn_lb, hgrn_norm_g, mlstm_conv_w,
            mlstm_conv_b, mlstm_ig_b, mlstm_fg_b, mlstm_norm_g, w_out, ln2_g, ln2_b, ffn2_w1, ffn2_w3,
            ffn2_w2, ln3_g, ln3_b)
    ms = (m_ffn1_w1, m_ffn1_w3, m_ffn1_w2, m_ln1_g, m_ln1_b, m_w_in, m_hgrn_lb, m_hgrn_norm_g,
          m_mlstm_conv_w, m_mlstm_conv_b, m_mlstm_ig_b, m_mlstm_fg_b, m_mlstm_norm_g, m_w_out, m_ln2_g,
          m_ln2_b, m_ffn2_w1, m_ffn2_w3, m_ffn2_w2, m_ln3_g, m_ln3_b)
    vs = (v_ffn1_w1, v_ffn1_w3, v_ffn1_w2, v_ln1_g, v_ln1_b, v_w_in, v_hgrn_lb, v_hgrn_norm_g,
          v_mlstm_conv_w, v_mlstm_conv_b, v_mlstm_ig_b, v_mlstm_fg_b, v_mlstm_norm_g, v_w_out, v_ln2_g,
          v_ln2_b, v_ffn2_w1, v_ffn2_w3, v_ffn2_w2, v_ln3_g, v_ln3_b)
    w = dict(zip(WEIGHTS, args))
    m = dict(zip(WEIGHTS, ms))
    v = dict(zip(WEIGHTS, vs))
    core = lax.axis_index("c")
    dev = 4 * lax.axis_index("x") + 2 * lax.axis_index("y") + core
    p_lb = []

    def small_params(gathered_small):
        lb_sh, cw_sh = zip(*[_unpack_rows(gathered_small[i], [hgrn_lb.shape, mlstm_conv_w.shape])
                             for i in range(N_DEV)])
        hgrn_lb_full = jnp.concatenate(lb_sh, axis=-1)
        conv_w_full = jnp.concatenate(cw_sh, axis=-1)[0]
        p_lb.append(jax.nn.softmax(hgrn_lb_full, axis=1))
        return dict(ln1_g=ln1_g, ln1_b=ln1_b, ln2_g=ln2_g, ln2_b=ln2_b, ln3_g=ln3_g, ln3_b=ln3_b,
                    lb=p_lb[0][:, 0], hgrn_norm_g=hgrn_norm_g, conv_w=conv_w_full, conv_b=mlstm_conv_b,
                    ig_b=mlstm_ig_b[0], fg_b=mlstm_fg_b[0], mlstm_norm_g=mlstm_norm_g)

    def rows_first(t, k):
        return t[0] if k in ROW_SHARDED else t[0].T

    loss_row, grad_x, land2, G = _device_step(
        x[0], loss_target[0], {k: rows_first(w[k], k).astype(BF16) for k in BIG},
        _pack_rows([hgrn_lb, mlstm_conv_w]), small_params, jnp.reshape(core, (1,)).astype(jnp.int32))
    p_lb = p_lb[0]

    dlb = G["lb"]
    g_lb = jnp.stack([dlb * p_lb[:, 0] * (1.0 - p_lb[:, 0]), -dlb * p_lb[:, 0] * p_lb[:, 1]], axis=1)
    small_full = {"ln1_g": G["ln1_g"], "ln1_b": G["ln1_b"], "hgrn_lb": g_lb, "hgrn_norm_g": G["hgrn_norm_g"],
                  "mlstm_conv_w": G["conv_w"][None], "mlstm_conv_b": G["conv_b"],
                  "mlstm_ig_b": G["ig_b"][None], "mlstm_fg_b": G["fg_b"][None],
                  "mlstm_norm_g": G["mlstm_norm_g"], "ln2_g": G["ln2_g"], "ln2_b": G["ln2_b"],
                  "ln3_g": G["ln3_g"], "ln3_b": G["ln3_b"]}
    small_list = [small_full[k] for k in SMALL] + [loss_row]
    reduced = _all_reduce_small(_pack_rows(small_list), "all_reduce_small")
    red = _unpack_rows(reduced, [a.shape for a in small_list])
    loss = red[-1][0, 0]
    small_g = {}
    for k, gk in zip(SMALL, red[:-1]):
        if k in ("hgrn_lb", "mlstm_conv_w"):
            n = w[k].shape[-1]
            gk = lax.dynamic_slice_in_dim(gk, dev * n, n, axis=gk.ndim - 1)
        small_g[k] = gk

    grads, delta, new_m, new_v = {}, {}, {}, {}
    for k in BIG:
        res = _adamw(rows_first(w[k], k), land2[k], rows_first(m[k], k), rows_first(v[k], k), "adamw_" + k)
        grads[k], delta[k], new_m[k], new_v[k] = [(r if k in ROW_SHARDED else r.T)[None] for r in res]
    sm = _adamw(_pack_rows([w[k] for k in SMALL]), _pack_rows([small_g[k] for k in SMALL])[None],
                _pack_rows([m[k] for k in SMALL]), _pack_rows([v[k] for k in SMALL]), "adamw_small")
    shapes = [w[k].shape for k in SMALL]
    for dst, buf in zip((grads, delta, new_m, new_v), sm):
        for k, val in zip(SMALL, _unpack_rows(buf, shapes)):
            dst[k] = val
    return (loss, grad_x[None], *[grads[k] for k in WEIGHTS], *[delta[k] for k in WEIGHTS],
            *[new_m[k] for k in WEIGHTS], *[new_v[k] for k in WEIGHTS])
```

```python
import math

import jax
import jax.numpy as jnp
from jax import lax
from jax.experimental import pallas as pl
from jax.experimental.pallas import tpu as pltpu

F32 = jnp.float32
BF16 = jnp.bfloat16

CHUNK = 64
HGRN_HEAD_DIM = 128
CONV_WIDTH = 5
DN_ALPHA = 2.0 ** 0.25
LN_EPS = 1e-5
NORM_EPS = 1e-6
M_INIT = -1e30
NEG = -1e30
EXP_CLAMP = 80.0
ADAM_LR = 0.001
ADAM_B1 = 0.9
ADAM_B2 = 0.999
ADAM_EPS = 1e-08
ADAM_WD = 0.01
ADAM_STEP = 10
N_DEV = 8
LANES = 128
VMEM_LIMIT = 56 * 1024 * 1024
MESH = pl.DeviceIdType.MESH


def _params(*sem):
    return pltpu.CompilerParams(dimension_semantics=sem, vmem_limit_bytes=VMEM_LIMIT)


class _Job:
    def __init__(self, inputs, out_shape, sems, start, finish):
        self.inputs, self.out_shape, self.sems = list(inputs), list(out_shape), list(sems)
        self.start, self.finish = start, finish


def _join(*jobs):
    def split(refs, counts):
        out, p = [], 0
        for c in counts:
            out.append(refs[p:p + c])
            p += c
        return out

    n_in = [len(j.inputs) for j in jobs]
    n_out = [len(j.out_shape) for j in jobs]
    n_sem = [len(j.sems) for j in jobs]

    def start(ins, outs, sems):
        for j, i, o, s in zip(jobs, split(ins, n_in), split(outs, n_out), split(sems, n_sem)):
            j.start(i, o, s)

    def finish(ins, outs, sems):
        for j, i, o, s in zip(jobs, split(ins, n_in), split(outs, n_out), split(sems, n_sem)):
            j.finish(i, o, s)

    return _Job(sum((j.inputs for j in jobs), []), sum((j.out_shape for j in jobs), []),
                sum((j.sems for j in jobs), []), start, finish)


def _pcall(body, *, name, out_shape, grid, in_specs, out_specs, sem, scratch_shapes=(), comm=None):
    single = not isinstance(out_shape, (list, tuple))
    out_shape = [out_shape] if single else list(out_shape)
    out_specs = [out_specs] if single else list(out_specs)
    in_specs, scratch_shapes = list(in_specs), list(scratch_shapes)
    if comm is None:
        call = pl.pallas_call(body, name=name, out_shape=out_shape, grid=grid, in_specs=in_specs,
                              out_specs=out_specs, scratch_shapes=scratch_shapes,
                              compiler_params=_params(*sem))

        def run(*args):
            outs = call(*args)
            return outs[0] if single else outs
        return run

    n_in, n_out, n_sc = len(in_specs), len(out_shape), len(scratch_shapes)
    c_in, c_out = len(comm.inputs), len(comm.out_shape)

    def hosted(*refs):
        ins, cins = refs[:n_in], refs[n_in:n_in + c_in]
        p = n_in + c_in
        outs, couts = refs[p:p + n_out], refs[p + n_out:p + n_out + c_out]
        p += n_out + c_out
        scratch, csems = refs[p:p + n_sc], refs[p + n_sc:]
        first = pl.program_id(0) == 0
        last = pl.program_id(0) == grid[0] - 1
        for d in range(1, len(grid)):
            first = first & (pl.program_id(d) == 0)
            last = last & (pl.program_id(d) == grid[d] - 1)

        @pl.when(first)
        def _():
            comm.start(cins, couts, csems)

        body(*ins, *outs, *scratch)

        @pl.when(last)
        def _():
            comm.finish(cins, couts, csems)

    any_spec = pl.BlockSpec(memory_space=pl.ANY)
    call = pl.pallas_call(
        hosted, name=name, out_shape=out_shape + comm.out_shape, grid=grid,
        in_specs=in_specs + [any_spec] * c_in, out_specs=out_specs + [any_spec] * c_out,
        scratch_shapes=scratch_shapes + comm.sems,
        compiler_params=_params(*(["arbitrary"] * len(grid))))

    def run(*args):
        res = call(*args, *comm.inputs)
        outs, couts = res[:n_out], list(res[n_out:])
        return (outs[0] if single else outs), couts
    return run


def _sigmoid(x):
    return 1.0 / (1.0 + jnp.exp(-x))


def _log_sigmoid(x):
    return jnp.minimum(x, 0.0) - jnp.log(1.0 + jnp.exp(-jnp.abs(x)))


def _dot(a, b, dims):
    return lax.dot_general(a.astype(BF16), b.astype(BF16), (dims, ((), ())),
                           preferred_element_type=F32)


def _dot3(a, b, dims):
    ah = a.astype(BF16)
    al = (a - ah.astype(F32)).astype(BF16)
    bh = b.astype(BF16)
    bl = (b - bh.astype(F32)).astype(BF16)
    d = (dims, ((), ()))
    out = lax.dot_general(ah, bh, d, preferred_element_type=F32)
    out = out + lax.dot_general(ah, bl, d, preferred_element_type=F32)
    return out + lax.dot_general(al, bh, d, preferred_element_type=F32)


_DIMS = {"nn": ((1,), (0,)), "nt": ((1,), (1,)), "tn": ((0,), (0,))}


def _dot_nn(a, b):
    return _dot3(a, b, _DIMS["nn"])


def _dot_nt(a, b):
    return _dot3(a, b, _DIMS["nt"])


def _dot_tn(a, b):
    return _dot3(a, b, _DIMS["tn"])


def _split3(x):
    hi = x.astype(BF16)
    r1 = x - hi.astype(F32)
    mid = r1.astype(BF16)
    lo = (r1 - mid.astype(F32)).astype(BF16)
    return hi, mid, lo


def _dot01(mask01, x, mode="nn"):
    m = mask01.astype(BF16)
    hi, mid, lo = _split3(x)
    d = (_DIMS[mode], ((), ()))
    out = lax.dot_general(m, hi, d, preferred_element_type=F32)
    out = out + lax.dot_general(m, mid, d, preferred_element_type=F32)
    return out + lax.dot_general(m, lo, d, preferred_element_type=F32)


def _matmul(a, b, *, mode="nn", out_dtype=F32, tm=512, tn=512, tk=None,
            add=None, scale=1.0, add_scale=1.0, name, comm=None, rows=None):
    if mode == "nn":
        (M, K), (K2, N) = a.shape, b.shape
        K2 = rows or K2
    elif mode == "nt":
        (M, K), (N, K2) = a.shape, b.shape
        N = rows or N
    else:
        (K, M), (K2, N) = a.shape, b.shape
    assert K == K2, (a.shape, b.shape, mode)
    tm, tn = min(tm, M), min(tn, N)
    tk = min(tk or K, K)
    assert M % tm == 0 and N % tn == 0 and K % tk == 0, (M, N, K, tm, tn, tk)
    nk = K // tk
    dims = _DIMS[mode]
    has_add = add is not None

    def body(*refs):
        if has_add:
            a_ref, b_ref, add_ref, o_ref = refs[:4]
        else:
            a_ref, b_ref, o_ref = refs[:3]
            add_ref = None
        acc_ref = refs[-1] if nk > 1 else None

        def finish(acc):
            out = acc if scale == 1.0 else acc * scale
            if has_add:
                out = out + add_ref[...].astype(F32) * add_scale
            o_ref[...] = out.astype(o_ref.dtype)

        if nk == 1:
            finish(_dot(a_ref[...], b_ref[...], dims))
        else:
            k = pl.program_id(2)

            @pl.when(k == 0)
            def _():
                acc_ref[...] = jnp.zeros_like(acc_ref)

            acc_ref[...] += _dot(a_ref[...], b_ref[...], dims)

            @pl.when(k == nk - 1)
            def _():
                finish(acc_ref[...])

    if mode == "tn":
        a_spec = pl.BlockSpec((tk, tm), lambda i, j, k: (k, i))
    else:
        a_spec = pl.BlockSpec((tm, tk), lambda i, j, k: (i, k))
    if mode == "nt":
        b_spec = pl.BlockSpec((tn, tk), lambda i, j, k: (j, k))
    else:
        b_spec = pl.BlockSpec((tk, tn), lambda i, j, k: (k, j))
    o_spec = pl.BlockSpec((tm, tn), lambda i, j, k: (i, j))
    in_specs = [a_spec, b_spec] + ([o_spec] if has_add else [])
    args = (a, b) + ((add,) if has_add else ())
    return _pcall(
        body, name=name, comm=comm,
        out_shape=jax.ShapeDtypeStruct((M, N), out_dtype),
        grid=(M // tm, N // tn, nk),
        in_specs=in_specs, out_specs=o_spec,
        scratch_shapes=[pltpu.VMEM((tm, tn), F32)] if nk > 1 else [],
        sem=("parallel", "parallel", "arbitrary"),
    )(*args)


def _ln_stats(x):
    mu = jnp.mean(x, axis=-1, keepdims=True)
    xc = x - mu
    var = jnp.mean(xc * xc, axis=-1, keepdims=True)
    rstd = lax.rsqrt(var + LN_EPS)
    return xc * rstd, rstd


def _ln_fwd(r, g, b, *, tm=256, name):
    T, D = r.shape
    tm = min(tm, T)

    def body(r_ref, g_ref, b_ref, y_ref, yb_ref):
        xhat, _ = _ln_stats(r_ref[...])
        y = xhat * g_ref[...] + b_ref[...]
        y_ref[...] = y
        yb_ref[...] = y.astype(BF16)

    row = pl.BlockSpec((tm, D), lambda i: (i, 0))
    vec = pl.BlockSpec((1, D), lambda i: (0, 0))
    return pl.pallas_call(
        body, name=name,
        out_shape=[jax.ShapeDtypeStruct((T, D), F32), jax.ShapeDtypeStruct((T, D), BF16)],
        grid=(T // tm,), in_specs=[row, vec, vec], out_specs=[row, row],
        compiler_params=_params("parallel"),
    )(r, g, b)


def _cast(x, *, tm=512, name):
    T, D = x.shape
    tm = min(tm, T)

    def body(x_ref, xb_ref):
        xb_ref[...] = x_ref[...].astype(BF16)

    row = pl.BlockSpec((tm, D), lambda i: (i, 0))
    return pl.pallas_call(
        body, name=name, out_shape=jax.ShapeDtypeStruct((T, D), BF16),
        grid=(T // tm,), in_specs=[row], out_specs=row,
        compiler_params=_params("parallel"),
    )(x)


def _ln_bwd(dy, r, g, *, tm=256, name, b=None, target=None):
    T, D = r.shape
    tm = min(tm, T)
    with_loss = target is not None

    def body(*refs):
        if with_loss:
            r_ref, g_ref, b_ref, t_ref, dr_ref, drb_ref, dg_ref, db_ref, loss_ref = refs
        else:
            dy_ref, r_ref, g_ref, dr_ref, drb_ref, dg_ref, db_ref = refs
        i = pl.program_id(0)
        xhat, rstd = _ln_stats(r_ref[...])
        gg = g_ref[...]
        if with_loss:
            err = xhat * gg + b_ref[...] - t_ref[...]
            dyv = err * (1.0 / D)
            part = jnp.sum(jnp.sum(err * err, axis=1, keepdims=True), axis=0, keepdims=True)
            part = jnp.broadcast_to(part * (0.5 / D), (1, LANES))
        else:
            dyv = dy_ref[...]
        dxh = dyv * gg
        m1 = jnp.mean(dxh, axis=-1, keepdims=True)
        m2 = jnp.mean(dxh * xhat, axis=-1, keepdims=True)
        dr = rstd * (dxh - m1 - xhat * m2)
        dr_ref[...] = dr
        drb_ref[...] = dr.astype(BF16)
        dgp = jnp.sum(dyv * xhat, axis=0, keepdims=True)
        dbp = jnp.sum(dyv, axis=0, keepdims=True)

        @pl.when(i == 0)
        def _():
            dg_ref[...] = dgp
            db_ref[...] = dbp
            if with_loss:
                loss_ref[...] = part

        @pl.when(i > 0)
        def _():
            dg_ref[...] += dgp
            db_ref[...] += dbp
            if with_loss:
                loss_ref[...] += part

    row = pl.BlockSpec((tm, D), lambda i: (i, 0))
    vec = pl.BlockSpec((1, D), lambda i: (0, 0))
    out_shape = [jax.ShapeDtypeStruct((T, D), F32), jax.ShapeDtypeStruct((T, D), BF16),
                 jax.ShapeDtypeStruct((1, D), F32), jax.ShapeDtypeStruct((1, D), F32)]
    out_specs = [row, row, vec, vec]
    if with_loss:
        in_specs, args = [row, vec, vec, row], (r, g, b, target)
        out_shape.append(jax.ShapeDtypeStruct((1, LANES), F32))
        out_specs.append(pl.BlockSpec((1, LANES), lambda i: (0, 0)))
    else:
        in_specs, args = [row, row, vec], (dy, r, g)
    return pl.pallas_call(
        body, name=name, out_shape=out_shape, grid=(T // tm,),
        in_specs=in_specs, out_specs=out_specs,
        compiler_params=_params("arbitrary"),
    )(*args)


def _ffn_up(xb, w1t, w3t, *, a=None, tm=1024, tf=512, name, comm=None):
    F, D = w3t.shape
    T = xb.shape[0]
    tm, tf = _pick(T, tm), _pick(F, tf)
    given = a is not None

    def body(x_ref, w1_ref, w3_ref, a_ref, b_ref, h_ref, ht_ref):
        xv = x_ref[...]
        av = w1_ref[...] if given else _dot(xv, w1_ref[...], _DIMS["nt"])
        b = _dot(xv, w3_ref[...], _DIMS["nt"])
        a_ref[...] = av.astype(BF16)
        b_ref[...] = b.astype(BF16)
        h = av * _sigmoid(av) * b
        h_ref[...] = h.astype(BF16)
        ht_ref[...] = h.T.astype(BF16)

    wsp = pl.BlockSpec((tf, D), lambda j, i: (j, 0))
    osp = pl.BlockSpec((tm, tf), lambda j, i: (i, j))
    act = jax.ShapeDtypeStruct((T, F), BF16)
    return _pcall(
        body, name=name, comm=comm, out_shape=[act, act, act, jax.ShapeDtypeStruct((F, T), BF16)],
        grid=(F // tf, T // tm),
        in_specs=[pl.BlockSpec((tm, D), lambda j, i: (i, 0)), osp if given else wsp, wsp],
        out_specs=[osp, osp, osp, pl.BlockSpec((tf, tm), lambda j, i: (j, i))],
        sem=("parallel", "parallel"),
    )(xb, a if given else w1t, w3t)


def _ffn_bwd_act(drb, w2, a, b, *, tm=1024, tf=512, name, comm=None):
    F, D = w2.shape
    T = drb.shape[0]
    tm, tf = _pick(T, tm), _pick(F, tf)

    def body(dr_ref, w2_ref, a_ref, b_ref, da_ref, db_ref, dat_ref, dbt_ref):
        d = 0.5 * _dot(dr_ref[...], w2_ref[...], _DIMS["nt"])
        av = a_ref[...].astype(F32)
        sig = _sigmoid(av)
        da = d * b_ref[...].astype(F32) * sig * (1.0 + av * (1.0 - sig))
        db = d * av * sig
        da_ref[...] = da.astype(BF16)
        db_ref[...] = db.astype(BF16)
        dat_ref[...] = da.T.astype(BF16)
        dbt_ref[...] = db.T.astype(BF16)

    asp = pl.BlockSpec((tm, tf), lambda i, j: (i, j))
    tsp = pl.BlockSpec((tf, tm), lambda i, j: (j, i))
    out, out_t = jax.ShapeDtypeStruct((T, F), BF16), jax.ShapeDtypeStruct((F, T), BF16)
    return _pcall(
        body, name=name, comm=comm, out_shape=[out, out, out_t, out_t], grid=(T // tm, F // tf),
        in_specs=[pl.BlockSpec((tm, D), lambda i, j: (i, 0)),
                  pl.BlockSpec((tf, D), lambda i, j: (j, 0)), asp, asp],
        out_specs=[asp, asp, tsp, tsp],
        sem=("parallel", "parallel"),
    )(drb, w2, a, b)


def _dz_pack(pieces, *, tm=256, name):
    groups = [p if isinstance(p, tuple) else (p,) for p in pieces]
    T = groups[0][0].shape[0]
    widths = [g[0].shape[1] for g in groups]
    W = sum(widths)
    tm = _pick(T, tm)
    flat = [a for g in groups for a in g]
    n = len(flat)

    def body(*refs):
        o_ref, ot_ref = refs[n], refs[n + 1]
        c, p = 0, 0
        for g, w in zip(groups, widths):
            v = refs[p][...]
            for r in refs[p + 1:p + len(g)]:
                v = v + r[...]
            p += len(g)
            o_ref[:, c:c + w] = v.astype(BF16)
            ot_ref[c:c + w, :] = v.T.astype(BF16)
            c += w

    return pl.pallas_call(
        body, name=name,
        out_shape=[jax.ShapeDtypeStruct((T, W), BF16), jax.ShapeDtypeStruct((W, T), BF16)],
        grid=(T // tm,),
        in_specs=[pl.BlockSpec((tm, a.shape[1]), lambda i: (i, 0)) for a in flat],
        out_specs=[pl.BlockSpec((tm, W), lambda i: (i, 0)), pl.BlockSpec((W, tm), lambda i: (0, i))],
        compiler_params=_params("parallel"),
    )(*flat)


def _ffn_dx(da, db, w1t, w3t, dr, *, tm=1024, tn=1024, tk=None, name, comm=None):
    T, F = da.shape
    D = w1t.shape[1]
    tm, tn = _pick(T, tm), _pick(D, tn)
    tk = _pick(F, tk or F // 4)
    nk = F // tk

    def body(da_ref, db_ref, w1_ref, w3_ref, dr_ref, o_ref, acc_ref):
        k = pl.program_id(2)

        @pl.when(k == 0)
        def _():
            acc_ref[...] = jnp.zeros_like(acc_ref)

        acc_ref[...] += (_dot(da_ref[...], w1_ref[...], _DIMS["nn"])
                         + _dot(db_ref[...], w3_ref[...], _DIMS["nn"]))

        @pl.when(k == nk - 1)
        def _():
            o_ref[...] = DN_ALPHA * dr_ref[...] + acc_ref[...]

    asp = pl.BlockSpec((tm, tk), lambda i, j, k: (i, k))
    wsp = pl.BlockSpec((tk, tn), lambda i, j, k: (k, j))
    osp = pl.BlockSpec((tm, tn), lambda i, j, k: (i, j))
    return _pcall(
        body, name=name, comm=comm, out_shape=jax.ShapeDtypeStruct((T, D), F32),
        grid=(T // tm, D // tn, nk), in_specs=[asp, asp, wsp, wsp, osp], out_specs=osp,
        scratch_shapes=[pltpu.VMEM((tm, tn), F32)],
        sem=("parallel", "parallel", "arbitrary"),
    )(da, db, w1t, w3t, dr)


def _chunk_mask(reverse, transpose=False):
    row = lax.broadcasted_iota(jnp.int32, (CHUNK, CHUNK), 0)
    col = lax.broadcasted_iota(jnp.int32, (CHUNK, CHUNK), 1)
    if reverse != transpose:
        return col >= row
    return col <= row


def _run_specs(specs, *, name, comm=None):
    counts = [(len(s["in_specs"]), len(s["out_specs"]), len(s["scratch_shapes"])) for s in specs]

    def body(*refs):
        p, parts = 0, [[], [], []]
        for kind in range(3):
            for c in counts:
                parts[kind].append(refs[p:p + c[kind]])
                p += c[kind]
        live = [s["body"](*parts[0][k], *parts[1][k], *parts[2][k]) for k, s in enumerate(specs)]
        while live:
            for g in list(live):
                if next(g, StopIteration) is StopIteration:
                    live.remove(g)

    cat = lambda key: [v for s in specs for v in s[key]]
    res = _pcall(body, name=name, comm=comm, out_shape=cat("out_shape"), grid=specs[0]["grid"],
                 in_specs=cat("in_specs"), out_specs=cat("out_specs"),
                 scratch_shapes=cat("scratch_shapes"), sem=("arbitrary",))(*cat("args"))
    outs, couts = res if comm is not None else (res, None)
    split, p = [], 0
    for c in counts:
        split.append(outs[p:p + c[1]])
        p += c[1]
    return (split, couts) if comm is not None else split


def _hgrn_pre(hq, hf, lb):
    sig = _sigmoid(hf)
    f = lb + (1.0 - lb) * sig
    q = hq * _sigmoid(hq) * (HGRN_HEAD_DIM ** -0.5)
    return q, f, sig


def _hgrn_decays(f, cmf, reverse):
    bc = _dot01(cmf, jnp.log(f))
    last = 0 if reverse else CHUNK - 1
    blast = bc[last:last + 1, :]
    bref = bc[CHUNK // 2:CHUNK // 2 + 1, :]
    eq = jnp.exp(jnp.minimum(bc - bref, EXP_CLAMP))
    ek = jnp.exp(jnp.minimum(bref - bc, EXP_CLAMP))
    return bc, blast, eq, ek


def _hgrn_fwd(z, lb, *, wh, reverse):
    T = z.shape[0]
    nch = T // CHUNK
    nh = wh // HGRN_HEAD_DIM
    hd = HGRN_HEAD_DIM

    def ci(i):
        return nch - 1 - i if reverse else i

    def body(hq_ref, hi_ref, hf_ref, lb_ref, o_ref, st_ref, s_ref):
        @pl.when(pl.program_id(0) == 0)
        def _():
            s_ref[...] = jnp.zeros_like(s_ref)

        cm = _chunk_mask(reverse)
        cmf = cm.astype(F32)
        H = range(nh)
        sls = [slice(h * hd, (h + 1) * hd) for h in H]
        q, f, _ = _hgrn_pre(hq_ref[...], hf_ref[...], lb_ref[...])
        v = hi_ref[...]
        k = 1.0 - f
        yield
        bc, blast, eq, ek = _hgrn_decays(f, cmf, reverse)
        yield
        qh, kh, qe = q * eq, k * ek, q * jnp.exp(bc)
        k2, eblast = k * jnp.exp(blast - bc), jnp.exp(blast)
        st = [s_ref[h] for h in H]
        yield
        att = [jnp.where(cm, _dot_nt(qh[:, sl], kh[:, sl]), 0.0) for sl in sls]
        yield
        inter = [_dot_nt(qe[:, sls[h]], st[h]) for h in H]
        yield
        out = [inter[h] + _dot_nn(att[h], v[:, sls[h]]) for h in H]
        yield
        snew = [eblast[:, sls[h]] * st[h] + _dot_tn(v[:, sls[h]], k2[:, sls[h]]) for h in H]
        yield
        for h in H:
            st_ref[0, h] = st[h]
            o_ref[:, sls[h]] = out[h]
            s_ref[h] = snew[h]

    blk = lambda c: pl.BlockSpec((CHUNK, wh), lambda i: (ci(i), c))
    return dict(
        body=body, grid=(nch,), args=[z, z, z, lb],
        out_shape=[jax.ShapeDtypeStruct((T, wh), F32),
                   jax.ShapeDtypeStruct((nch, nh, hd, hd), F32)],
        in_specs=[blk(0), blk(1), blk(3 + int(reverse)),
                  pl.BlockSpec((1, wh), lambda i: (0, 0))],
        out_specs=[blk(0), pl.BlockSpec((1, nh, hd, hd), lambda i: (ci(i), 0, 0, 0))],
        scratch_shapes=[pltpu.VMEM((nh, hd, hd), F32)])


def _hgrn_bwd(z, lb, do, states, *, wh, reverse):
    T = z.shape[0]
    nch = T // CHUNK
    nh = wh // HGRN_HEAD_DIM
    hd = HGRN_HEAD_DIM

    def ci(i):
        return i if reverse else nch - 1 - i

    def body(hq_ref, hi_ref, hf_ref, lb_ref, do_ref, st_ref,
             dhq_ref, dhi_ref, dhf_ref, dlb_ref, ds_ref, gs_ref):
        @pl.when(pl.program_id(0) == 0)
        def _():
            ds_ref[...] = jnp.zeros_like(ds_ref)
            gs_ref[...] = jnp.zeros_like(gs_ref)
            dlb_ref[...] = jnp.zeros_like(dlb_ref)

        cm = _chunk_mask(reverse)
        cmf = cm.astype(F32)
        cmtf = _chunk_mask(reverse, transpose=True).astype(F32)
        H = range(nh)
        sls = [slice(h * hd, (h + 1) * hd) for h in H]
        hq, lb = hq_ref[...], lb_ref[...]
        q, f, sig = _hgrn_pre(hq, hf_ref[...], lb)
        v, dov = hi_ref[...], do_ref[...]
        k = 1.0 - f
        yield
        bc, blast, eq, ek = _hgrn_decays(f, cmf, reverse)
        yield
        ebc, eb2, eblast = jnp.exp(bc), jnp.exp(blast - bc), jnp.exp(blast)
        qh, kh, qe, k2 = q * eq, k * ek, q * ebc, k * eb2
        st = [st_ref[0, h] for h in H]
        dst = [ds_ref[h] for h in H]
        yield
        att = [jnp.where(cm, _dot_nt(qh[:, sl], kh[:, sl]), 0.0) for sl in sls]
        yield
        datt = [jnp.where(cm, _dot_nt(dov[:, sl], v[:, sl]), 0.0) for sl in sls]
        yield
        dq_a = [_dot_nn(datt[h], kh[:, sls[h]]) for h in H]
        yield
        dq_s = [_dot_nn(dov[:, sls[h]], st[h]) for h in H]
        yield
        dk_a = [_dot_tn(datt[h], qh[:, sls[h]]) for h in H]
        yield
        dk_s = [_dot_nn(v[:, sls[h]], dst[h]) for h in H]
        yield
        dv = [_dot_tn(att[h], dov[:, sls[h]]) + _dot_nt(k2[:, sls[h]], dst[h]) for h in H]
        yield
        dsn = [eblast[:, sls[h]] * dst[h] + _dot_tn(dov[:, sls[h]], qe[:, sls[h]]) for h in H]
        for h in H:
            ds_ref[h] = dsn[h]
        yield
        dq = jnp.concatenate(dq_a, axis=1) * eq + ebc * jnp.concatenate(dq_s, axis=1)
        dk = jnp.concatenate(dk_a, axis=1) * ek + eb2 * jnp.concatenate(dk_s, axis=1)
        db = q * dq - k * dk
        yield
        dg = _dot01(cmtf, db) + gs_ref[...]
        gs_ref[...] += jnp.sum(db, axis=0, keepdims=True)
        yield
        df = dg / f - dk
        dhf_ref[...] = df * (1.0 - lb) * sig * (1.0 - sig)
        dlb_ref[...] += jnp.sum(df * (1.0 - sig), axis=0, keepdims=True)
        sq = _sigmoid(hq)
        dhq_ref[...] = dq * (HGRN_HEAD_DIM ** -0.5) * sq * (1.0 + hq * (1.0 - sq))
        dhi_ref[...] = jnp.concatenate(dv, axis=1)

    blk = lambda c: pl.BlockSpec((CHUNK, wh), lambda i: (ci(i), c))
    vec = pl.BlockSpec((1, wh), lambda i: (0, 0))
    big = jax.ShapeDtypeStruct((T, wh), F32)
    return dict(
        body=body, grid=(nch,), args=[z, z, z, lb, do, states],
        in_specs=[blk(0), blk(1), blk(3 + int(reverse)), vec, blk(0),
                  pl.BlockSpec((1, nh, hd, hd), lambda i: (ci(i), 0, 0, 0))],
        out_shape=[big, big, big, jax.ShapeDtypeStruct((1, wh), F32)],
        out_specs=[blk(0), blk(0), blk(0), vec],
        scratch_shapes=[pltpu.VMEM((nh, hd, hd), F32), pltpu.VMEM((1, wh), F32)])


def _mlstm_intra(cm, cmt, ig_row, ig_col, xf_row, xf_col, m_st):
    lf_row, lf_col = _log_sigmoid(xf_row), _log_sigmoid(xf_col)
    bcol = jnp.sum(jnp.where(cm, lf_row, 0.0), axis=1, keepdims=True)
    brow = jnp.sum(jnp.where(cmt, lf_col, 0.0), axis=0, keepdims=True)
    blast = jnp.sum(lf_row, axis=1, keepdims=True)
    dmat = jnp.where(cm, bcol - brow + ig_row, NEG)
    m_inter = bcol + m_st
    m_t = jnp.maximum(m_inter, jnp.max(dmat, axis=1, keepdims=True))
    p = jnp.exp(dmat - m_t)
    inter = jnp.exp(m_inter - m_t)
    w_col = blast - bcol + ig_col
    m_new = jnp.maximum(blast + m_st, jnp.max(w_col, axis=0, keepdims=True))
    cs = jnp.exp(blast + m_st - m_new)
    kscale = jnp.exp(w_col - m_new)
    return p, inter, m_t, m_new, cs, kscale


def _mlstm_specs(T, wm, nhm, reverse, backward):
    nch = T // CHUNK
    dm = wm // nhm
    ng = 4 * nhm

    def ci(i):
        fwd_order = nch - 1 - i if reverse else i
        return nch - 1 - fwd_order if backward else fwd_order

    row = lambda w, c: pl.BlockSpec((CHUNK, w), lambda i: (ci(i), c))
    gates_row = pl.BlockSpec((1, ng, CHUNK), lambda i: (ci(i), 0, 0))
    bias_row = pl.BlockSpec((1, LANES), lambda i: (0, 0))
    bias_col = pl.BlockSpec((ng, 1), lambda i: (0, 0))
    st_c = pl.BlockSpec((1, nhm, dm, dm), lambda i: (ci(i), 0, 0, 0))
    st_n = pl.BlockSpec((1, nhm, 1, dm), lambda i: (ci(i), 0, 0, 0))
    st_m = pl.BlockSpec((1, nhm, 1, LANES), lambda i: (ci(i), 0, 0, 0))
    return nch, dm, ng, ci, row, gates_row, bias_row, bias_col, st_c, st_n, st_m


def _mlstm_fwd(qk, z, gc, gr, br, bcl, *, wm, nhm, vcol, reverse):
    T = qk.shape[0]
    nch, dm, ng, ci, row, gates_row, bias_row, bias_col, st_c, st_n, st_m = _mlstm_specs(
        T, wm, nhm, reverse, False)
    d = int(reverse)

    def body(q_ref, k_ref, v_ref, gc_ref, gr_ref, br_ref, bc_ref,
             h_ref, cst_ref, nst_ref, mst_ref, c_ref, n_ref, m_ref):
        @pl.when(pl.program_id(0) == 0)
        def _():
            c_ref[...] = jnp.zeros_like(c_ref)
            n_ref[...] = jnp.zeros_like(n_ref)
            m_ref[...] = jnp.full_like(m_ref, M_INIT)

        cm = _chunk_mask(reverse)
        cmt = _chunk_mask(reverse, transpose=True)
        G = gc_ref[...] + br_ref[...]
        Gr = gr_ref[0] + bc_ref[...]
        H = range(nhm)
        sls = [slice(h * dm, (h + 1) * dm) for h in H]
        m_all = [m_ref[h] for h in H]
        intra = [_mlstm_intra(cm, cmt, Gr[d * nhm + h:d * nhm + h + 1, :], G[:, d * nhm + h:d * nhm + h + 1],
                              Gr[2 * nhm + d * nhm + h:2 * nhm + d * nhm + h + 1, :],
                              G[:, 2 * nhm + d * nhm + h:2 * nhm + d * nhm + h + 1], m_all[h][:, 0:1])
                 for h in H]
        p, inter, m_t, m_new, cs, kscale = zip(*intra)
        yield
        q = [q_ref[:, sl] * (dm ** -0.5) for sl in sls]
        k = [k_ref[:, sl] for sl in sls]
        v = [v_ref[:, sl] for sl in sls]
        ct = [c_ref[h] for h in H]
        n = [n_ref[h] for h in H]
        yield
        sc = [_dot_nt(q[h], k[h]) * p[h] for h in H]
        yield
        qc = [_dot_nt(q[h], ct[h]) for h in H]
        yield
        num = [_dot_nn(sc[h], v[h]) + inter[h] * qc[h] for h in H]
        yield
        den = [jnp.sum(sc[h], axis=1, keepdims=True) + inter[h] * jnp.sum(q[h] * n[h], axis=1, keepdims=True)
               for h in H]
        kw = [k[h] * kscale[h] for h in H]
        yield
        cnew = [cs[h] * ct[h] + _dot_tn(v[h], kw[h]) for h in H]
        yield
        for h in H:
            cst_ref[0, h] = ct[h]
            nst_ref[0, h] = n[h]
            mst_ref[0, h] = m_all[h]
            h_ref[:, sls[h]] = num[h] / jnp.maximum(jnp.abs(den[h]), jnp.exp(-m_t[h]))
            c_ref[h] = cnew[h]
            n_ref[h] = cs[h] * n[h] + jnp.sum(kw[h], axis=0, keepdims=True)
            m_ref[h] = jnp.broadcast_to(m_new[h], (1, LANES))

    return dict(
        body=body, grid=(nch,), args=[qk, qk, z, gc, gr, br, bcl],
        out_shape=[jax.ShapeDtypeStruct((T, wm), F32),
                   jax.ShapeDtypeStruct((nch, nhm, dm, dm), F32),
                   jax.ShapeDtypeStruct((nch, nhm, 1, dm), F32),
                   jax.ShapeDtypeStruct((nch, nhm, 1, LANES), F32)],
        in_specs=[row(wm, 0), row(wm, 1), row(wm, vcol), row(LANES, 0), gates_row, bias_row, bias_col],
        out_specs=[row(wm, 0), st_c, st_n, st_m],
        scratch_shapes=[pltpu.VMEM((nhm, dm, dm), F32), pltpu.VMEM((nhm, 1, dm), F32),
                        pltpu.VMEM((nhm, 1, LANES), F32)])


def _mlstm_bwd(qk, z, gc, gr, br, bcl, dh, states, *, wm, nhm, vcol, reverse):
    T = qk.shape[0]
    nch, dm, ng, ci, row, gates_row, bias_row, bias_col, st_c, st_n, st_m = _mlstm_specs(
        T, wm, nhm, reverse, True)
    d = int(reverse)

    def body(q_ref, k_ref, v_ref, gc_ref, gr_ref, br_ref, bc_ref, dh_ref, cst_ref, nst_ref, mst_ref,
             dqk_ref, dv_ref, dgr_ref, dgs_ref, e_ref, en_ref, fs_ref):
        @pl.when(pl.program_id(0) == 0)
        def _():
            e_ref[...] = jnp.zeros_like(e_ref)
            en_ref[...] = jnp.zeros_like(en_ref)
            fs_ref[...] = jnp.zeros_like(fs_ref)
            dgs_ref[...] = jnp.zeros_like(dgs_ref)

        cm = _chunk_mask(reverse)
        cmt = _chunk_mask(reverse, transpose=True)
        row_i = lax.broadcasted_iota(jnp.int32, (CHUNK, CHUNK), 0)
        col_i = lax.broadcasted_iota(jnp.int32, (CHUNK, CHUNK), 1)
        eye = row_i == col_i
        G = gc_ref[...] + br_ref[...]
        Gr = gr_ref[0] + bc_ref[...]
        H = range(nhm)
        sls = [slice(h * dm, (h + 1) * dm) for h in H]
        xf_row = [Gr[2 * nhm + d * nhm + h:2 * nhm + d * nhm + h + 1, :] for h in H]
        intra = [_mlstm_intra(cm, cmt, Gr[d * nhm + h:d * nhm + h + 1, :], G[:, d * nhm + h:d * nhm + h + 1],
                              xf_row[h], G[:, 2 * nhm + d * nhm + h:2 * nhm + d * nhm + h + 1],
                              mst_ref[0, h][:, 0:1]) for h in H]
        p, inter, m_t, _, cs, kscale = zip(*intra)
        yield
        q = [q_ref[:, sls[h]] * (dm ** -0.5) for h in H]
        k = [k_ref[:, sls[h]] for h in H]
        v = [v_ref[:, sls[h]] for h in H]
        dhv = [dh_ref[:, sls[h]] for h in H]
        ct = [cst_ref[0, h] for h in H]
        n = [nst_ref[0, h] for h in H]
        et = [e_ref[h] for h in H]
        en = [en_ref[h] for h in H]
        carry = [fs_ref[h][:, 0:1] for h in H]
        yield
        sc = [_dot_nt(q[h], k[h]) * p[h] for h in H]
        yield
        qc = [_dot_nt(q[h], ct[h]) for h in H]
        yield
        num = [_dot_nn(sc[h], v[h]) + inter[h] * qc[h] for h in H]
        yield
        den = [jnp.sum(sc[h], axis=1, keepdims=True) + inter[h] * jnp.sum(q[h] * n[h], axis=1, keepdims=True)
               for h in H]
        floor = [jnp.exp(-m_t[h]) for h in H]
        nstab = [jnp.maximum(jnp.abs(den[h]), floor[h]) for h in H]
        gh = [dhv[h] / nstab[h] for h in H]
        dd = [-jnp.sum(dhv[h] * (num[h] / nstab[h]), axis=1, keepdims=True) / nstab[h]
              * jnp.where(jnp.abs(den[h]) > floor[h], jnp.sign(den[h]), 0.0) for h in H]
        yield
        dsqk = [(_dot_nt(gh[h], v[h]) + dd[h]) * p[h] for h in H]
        qi = [q[h] * inter[h] for h in H]
        kw = [k[h] * kscale[h] for h in H]
        yield
        ghc = [_dot_nn(gh[h], ct[h]) for h in H]
        yield
        vet = [_dot_nn(v[h], et[h]) for h in H]
        yield
        dq = [_dot_nn(dsqk[h], k[h]) + inter[h] * (ghc[h] + dd[h] * n[h]) for h in H]
        yield
        dk = [_dot_tn(dsqk[h], q[h]) + kscale[h] * (vet[h] + en[h]) for h in H]
        yield
        dv = [_dot_tn(sc[h], gh[h]) + _dot_nt(kw[h], et[h]) for h in H]
        yield
        e_new = [cs[h] * et[h] + _dot_tn(gh[h], qi[h]) for h in H]
        en_new = [cs[h] * en[h] + jnp.sum(qi[h] * dd[h], axis=0, keepdims=True) for h in H]
        yield
        di_col = [jnp.sum(k[h] * dk[h], axis=1, keepdims=True) for h in H]
        df_col = [jnp.sum(q[h] * dq[h], axis=1, keepdims=True) - di_col[h] for h in H]
        di_row = [jnp.sum(jnp.where(eye, di_col[h], 0.0), axis=0, keepdims=True) for h in H]
        dlf_row = [jnp.sum(jnp.where(cm, df_col[h], 0.0), axis=0, keepdims=True) + carry[h] for h in H]
        dxf_row = [dlf_row[h] * (1.0 - _sigmoid(xf_row[h])) for h in H]
        yield
        for h in H:
            sl = sls[h]
            slk = slice(wm + h * dm, wm + (h + 1) * dm)
            e_ref[h] = e_new[h]
            en_ref[h] = en_new[h]
            fs_ref[h] = jnp.broadcast_to(carry[h] + jnp.sum(df_col[h], axis=0, keepdims=True), (1, LANES))
            dgr_ref[0, h:h + 1, :] = di_row[h]
            dgr_ref[0, nhm + h:nhm + h + 1, :] = dxf_row[h]
            dgs_ref[h:h + 1, :] += jnp.broadcast_to(jnp.sum(di_row[h], axis=1, keepdims=True), (1, LANES))
            dgs_ref[nhm + h:nhm + h + 1, :] += jnp.broadcast_to(
                jnp.sum(dxf_row[h], axis=1, keepdims=True), (1, LANES))
            dqk_ref[:, sl] = dq[h] * (dm ** -0.5)
            dqk_ref[:, slk] = dk[h]
            dv_ref[:, sl] = dv[h]

    return dict(
        body=body, grid=(nch,), args=[qk, qk, z, gc, gr, br, bcl, dh] + list(states),
        in_specs=[row(wm, 0), row(wm, 1), row(wm, vcol), row(LANES, 0), gates_row, bias_row, bias_col,
                  row(wm, 0), st_c, st_n, st_m],
        out_shape=[jax.ShapeDtypeStruct((T, 2 * wm), F32), jax.ShapeDtypeStruct((T, wm), F32),
                   jax.ShapeDtypeStruct((nch, 2 * nhm, CHUNK), F32),
                   jax.ShapeDtypeStruct((2 * nhm, LANES), F32)],
        out_specs=[row(2 * wm, 0), row(wm, 0),
                   pl.BlockSpec((1, 2 * nhm, CHUNK), lambda i: (ci(i), 0, 0)),
                   pl.BlockSpec((2 * nhm, LANES), lambda i: (0, 0))],
        scratch_shapes=[pltpu.VMEM((nhm, dm, dm), F32), pltpu.VMEM((nhm, 1, dm), F32),
                        pltpu.VMEM((nhm, 1, LANES), F32)])


def _conv_taps(x, w_ref):
    T = x.shape[0]
    t = lax.broadcasted_iota(jnp.int32, x.shape, 0)
    taps = []
    acc = None
    for j in range(CONV_WIDTH):
        s = CONV_WIDTH // 2 - j
        if s == 0:
            xs = x
        else:
            xs = jnp.where((t - s >= 0) & (t - s < T), pltpu.roll(x, s % T, 0), 0.0)
        taps.append(xs)
        term = w_ref[j:j + 1, :] * xs
        acc = term if acc is None else acc + term
    return taps, acc


def _conv_fwd(z, w, b, *, col0, tc=LANES, name):
    T = z.shape[0]
    C2 = w.shape[1]
    assert col0 % tc == 0 and C2 % tc == 0

    def body(z_ref, w_ref, b_ref, o_ref):
        _, acc = _conv_taps(z_ref[...], w_ref)
        c = acc + b_ref[...]
        o_ref[...] = c * _sigmoid(c)

    return pl.pallas_call(
        body, name=name, out_shape=jax.ShapeDtypeStruct((T, C2), F32),
        grid=(C2 // tc,),
        in_specs=[pl.BlockSpec((T, tc), lambda j: (0, col0 // tc + j)),
                  pl.BlockSpec((CONV_WIDTH, tc), lambda j: (0, j)),
                  pl.BlockSpec((1, tc), lambda j: (0, j))],
        out_specs=pl.BlockSpec((T, tc), lambda j: (0, j)),
        compiler_params=_params("parallel"),
    )(z, w, b)


def _conv_bwd(dy_fw, dy_bw, z, w, b, *, col0, tc=LANES, name):
    T = z.shape[0]
    C2 = w.shape[1]

    def body(dyf_ref, dyb_ref, z_ref, w_ref, b_ref, dx_ref, dw_ref, db_ref):
        taps, acc = _conv_taps(z_ref[...], w_ref)
        c = acc + b_ref[...]
        sg = _sigmoid(c)
        dc = (dyf_ref[...] + dyb_ref[...]) * sg * (1.0 + c * (1.0 - sg))
        t = lax.broadcasted_iota(jnp.int32, dc.shape, 0)
        dx = None
        for j in range(CONV_WIDTH):
            s = j - CONV_WIDTH // 2
            if s == 0:
                ds = dc
            else:
                ds = jnp.where((t - s >= 0) & (t - s < T), pltpu.roll(dc, s % T, 0), 0.0)
            term = w_ref[j:j + 1, :] * ds
            dx = term if dx is None else dx + term
            dw_ref[j:j + 1, :] = jnp.sum(dc * taps[j], axis=0, keepdims=True)
        dx_ref[...] = dx
        db_ref[...] = jnp.sum(dc, axis=0, keepdims=True)

    col = pl.BlockSpec((T, tc), lambda j: (0, j))
    wsp = pl.BlockSpec((CONV_WIDTH, tc), lambda j: (0, j))
    bsp = pl.BlockSpec((1, tc), lambda j: (0, j))
    return pl.pallas_call(
        body, name=name,
        out_shape=[jax.ShapeDtypeStruct((T, C2), F32), jax.ShapeDtypeStruct((CONV_WIDTH, C2), F32),
                   jax.ShapeDtypeStruct((1, C2), F32)],
        grid=(C2 // tc,),
        in_specs=[col, col, pl.BlockSpec((T, tc), lambda j: (0, col0 // tc + j)), wsp, bsp],
        out_specs=[col, wsp, bsp],
        compiler_params=_params("parallel"),
    )(dy_fw, dy_bw, z, w, b)


def _mix_heads(o_fw, o_bw, h_fw, h_bw, wh, wm, nhm):
    out = []
    hd = HGRN_HEAD_DIM
    for h in range(wh // hd):
        sl = slice(h * hd, (h + 1) * hd)
        o = o_fw[:, sl] + o_bw[:, sl]
        r = lax.rsqrt(jnp.mean(o * o, axis=-1, keepdims=True) + NORM_EPS)
        out.append((0, sl, o * r, r))
    dm = wm // nhm
    for h in range(nhm):
        sl = slice(h * dm, (h + 1) * dm)
        x = h_fw[:, sl] + h_bw[:, sl]
        xc = x - jnp.mean(x, axis=-1, keepdims=True)
        r = lax.rsqrt(jnp.mean(xc * xc, axis=-1, keepdims=True) + NORM_EPS)
        out.append((1, sl, xc * r, r))
    return out


def _mix_specs(T, wh, wm, tm, gcol, ocol):
    rowh = pl.BlockSpec((tm, wh), lambda i: (i, 0))
    rowm = pl.BlockSpec((tm, wm), lambda i: (i, 0))
    hg = pl.BlockSpec((tm, wh), lambda i: (i, gcol))
    mo = pl.BlockSpec((tm, wm), lambda i: (i, ocol))
    vh = pl.BlockSpec((1, wh), lambda i: (0, 0))
    vm = pl.BlockSpec((1, wm), lambda i: (0, 0))
    return rowh, rowm, hg, mo, vh, vm


def _mix_fwd(o_fw, o_bw, h_fw, h_bw, z, gh, gm, *, nhm, gcol, ocol, tm=256, name):
    T, wh = o_fw.shape
    wm = h_fw.shape[1]
    tm = min(tm, T)
    rowh, rowm, hg, mo, vh, vm = _mix_specs(T, wh, wm, tm, gcol, ocol)

    def body(of_ref, ob_ref, hf_ref, hb_ref, hg_ref, mo_ref, gh_ref, gm_ref, y_ref, yt_ref):
        heads = _mix_heads(of_ref[...], ob_ref[...], hf_ref[...], hb_ref[...], wh, wm, nhm)
        for grp, sl, nv, _ in heads:
            if grp == 0:
                gate = hg_ref[:, sl]
                gate = gate * _sigmoid(gate)
                y = nv * gh_ref[:, sl] * gate
                osl = sl
            else:
                y = nv * gm_ref[:, sl] * _sigmoid(mo_ref[:, sl])
                osl = slice(wh + sl.start, wh + sl.stop)
            y_ref[:, osl] = y.astype(BF16)
            yt_ref[osl, :] = y.T.astype(BF16)

    return pl.pallas_call(
        body, name=name,
        out_shape=[jax.ShapeDtypeStruct((T, wh + wm), BF16), jax.ShapeDtypeStruct((wh + wm, T), BF16)],
        grid=(T // tm,),
        in_specs=[rowh, rowh, rowm, rowm, hg, mo, vh, vm],
        out_specs=[pl.BlockSpec((tm, wh + wm), lambda i: (i, 0)),
                   pl.BlockSpec((wh + wm, tm), lambda i: (0, i))],
        compiler_params=_params("parallel"),
    )(o_fw, o_bw, h_fw, h_bw, z, z, gh, gm)


def _mix_bwd(dy, o_fw, o_bw, h_fw, h_bw, z, gh, gm, *, nhm, gcol, ocol, tm=256, name, comm=None):
    T, wh = o_fw.shape
    wm = h_fw.shape[1]
    tm = min(tm, T)
    rowh, rowm, hg, mo, vh, vm = _mix_specs(T, wh, wm, tm, gcol, ocol)

    def body(dy_ref, of_ref, ob_ref, hf_ref, hb_ref, hg_ref, mo_ref, gh_ref, gm_ref,
             do_ref, dh_ref, dhg_ref, dmo_ref, dgh_ref, dgm_ref):
        @pl.when(pl.program_id(0) == 0)
        def _():
            dgh_ref[...] = jnp.zeros_like(dgh_ref)
            dgm_ref[...] = jnp.zeros_like(dgm_ref)

        heads = _mix_heads(of_ref[...], ob_ref[...], hf_ref[...], hb_ref[...], wh, wm, nhm)
        for grp, sl, nv, r in heads:
            if grp == 0:
                d = dy_ref[:, sl]
                x = hg_ref[:, sl]
                sg = _sigmoid(x)
                gate = x * sg
                g = gh_ref[:, sl]
                dgh_ref[:, sl] += jnp.sum(d * nv * gate, axis=0, keepdims=True)
                dhg_ref[:, sl] = d * nv * g * sg * (1.0 + x * (1.0 - sg))
                t = d * g * gate
                do_ref[:, sl] = r * (t - nv * jnp.mean(t * nv, axis=-1, keepdims=True))
            else:
                d = dy_ref[:, slice(wh + sl.start, wh + sl.stop)]
                sg = _sigmoid(mo_ref[:, sl])
                g = gm_ref[:, sl]
                dgm_ref[:, sl] += jnp.sum(d * nv * sg, axis=0, keepdims=True)
                dmo_ref[:, sl] = d * nv * g * sg * (1.0 - sg)
                t = d * g * sg
                dh_ref[:, sl] = r * (t - jnp.mean(t, axis=-1, keepdims=True)
                                     - nv * jnp.mean(t * nv, axis=-1, keepdims=True))

    return _pcall(
        body, name=name, comm=comm,
        out_shape=[jax.ShapeDtypeStruct((T, wh), F32), jax.ShapeDtypeStruct((T, wm), F32),
                   jax.ShapeDtypeStruct((T, wh), F32), jax.ShapeDtypeStruct((T, wm), F32),
                   jax.ShapeDtypeStruct((1, wh), F32), jax.ShapeDtypeStruct((1, wm), F32)],
        grid=(T // tm,),
        in_specs=[pl.BlockSpec((tm, wh + wm), lambda i: (i, 0)), rowh, rowh, rowm, rowm, hg, mo, vh, vm],
        out_specs=[rowh, rowm, rowh, rowm, vh, vm],
        sem=("arbitrary",),
    )(dy, o_fw, o_bw, h_fw, h_bw, z, z, gh, gm)


def _pick(n, pref):
    for c in range(pref - pref % LANES, 0, -LANES):
        if n % c == 0:
            return c
    return n


def _mm(a, b, mode="nn", *, tm=1024, tn=1024, tk=2048, **kw):
    if mode == "nn":
        (M, K), N = a.shape, b.shape[1]
    else:
        (M, K), N = a.shape, kw.get("rows") or b.shape[0]
    return _matmul(a, b, mode=mode, tm=_pick(M, tm), tn=_pick(N, tn), tk=_pick(K, tk), **kw)


def _device_step(x, target, sh, small_parts, small_params, core):
    T, D = x.shape

    def full(g):
        return g.reshape(-1, D)

    def shards(g):
        return g.reshape(N_DEV, -1, D)

    def pair_add(name, g, land):
        return _pair_add(g, land, core, "rs_add_" + name)

    def dw(lhs_t, rhs, name, scale=1.0):
        return shards(_mm(lhs_t, rhs, out_dtype=BF16, scale=scale, tm=704, tk=T, name=name))

    g_w1a, g_small = _comm_call(_ag_job([sh["ffn1_w1"], small_parts]), "ag_ffn1_w1")
    w1a = full(g_w1a)
    P = small_params(g_small)
    wh = P["hgrn_norm_g"].shape[1]
    wm = P["mlstm_norm_g"].shape[1]
    nhm = P["ig_b"].shape[1]
    ng = 4 * nhm
    nch = T // CHUNK
    assert wh == wm and T % CHUNK == 0
    vcol, ocol, gcol = 7, 8, 2
    col0 = 5 * wh
    nz = 5 * wh + 4 * wm
    xb = _cast(x, name="x_cast")
    a1f, (g_w3a,) = _mm(xb, w1a, "nt", name="ffn1_up_a", comm=_ag_job([sh["ffn1_w3"]]))
    w3a = full(g_w3a)
    (a1, b1, h1, h1t), (g_w2a,) = _ffn_up(xb, None, w3a, a=a1f, name="ffn1_up_b",
                                          comm=_ag_job([sh["ffn1_w2"]]))
    w2a = full(g_w2a)
    r1, (g_win,) = _mm(h1, w2a, add=x, scale=0.5, add_scale=DN_ALPHA, tk=w2a.shape[0] // 4,
                       name="ffn1_down", comm=_ag_job([sh["w_in"]]))
    w_int = full(g_win)
    w_wgt = jnp.zeros((LANES, D), BF16).at[:ng].set(w_int[nz:])
    x1, x1b = _ln_fwd(r1, P["ln1_g"], P["ln1_b"], name="ln1")
    zm, (g_wout,) = _mm(x1b, w_int, "nt", rows=nz, name="zm", comm=_ag_job([sh["w_out"]]))
    w_wout = full(g_wout)
    zg = _mm(x1b, w_wgt, "nt", name="zg")
    gr = zg[:, :ng].reshape(nch, CHUNK, ng).transpose(0, 2, 1)
    bias = jnp.concatenate([P["ig_b"].reshape(-1), P["fg_b"].reshape(-1)])
    br = jnp.zeros((1, LANES), F32).at[0, :ng].set(bias)
    bcl = bias.reshape(ng, 1)
    lbf, lbb = P["lb"][0:1], P["lb"][1:2]
    ((o_fw, s_fw), (o_bw, s_bw)), (g_w1b,) = _run_specs(
        [_hgrn_fwd(zm, lbf, wh=wh, reverse=False), _hgrn_fwd(zm, lbb, wh=wh, reverse=True)],
        name="hgrn_fwd", comm=_ag_job([sh["ffn2_w1"]]))
    qk = _conv_fwd(zm, P["conv_w"], P["conv_b"], col0=col0, name="conv")
    mkw = dict(wm=wm, nhm=nhm, vcol=vcol)
    ((h_fw, *st_fw),), (g_w3b,) = _run_specs([_mlstm_fwd(qk, zm, zg, gr, br, bcl, reverse=False, **mkw)],
                                             name="mlstm_fw", comm=_ag_job([sh["ffn2_w3"]]))
    ((h_bw, *st_bw),) = _run_specs([_mlstm_fwd(qk, zm, zg, gr, br, bcl, reverse=True, **mkw)],
                                   name="mlstm_bw")
    w1b, w3b = full(g_w1b), full(g_w3b)
    y, yt = _mix_fwd(o_fw, o_bw, h_fw, h_bw, zm, P["hgrn_norm_g"], P["mlstm_norm_g"],
                     nhm=nhm, gcol=gcol, ocol=ocol, name="mix")
    r2 = _mm(y, w_wout, add=x1, add_scale=DN_ALPHA, name="r2")
    x2, x2b = _ln_fwd(r2, P["ln2_g"], P["ln2_b"], name="ln2")
    (a2, b2, h2, h2t), (g_w2b,) = _ffn_up(x2b, w1b, w3b, name="ffn2_up", comm=_ag_job([sh["ffn2_w2"]]))
    w2b = full(g_w2b)
    r3 = _mm(h2, w2b, add=x2, scale=0.5, add_scale=DN_ALPHA, tk=w2b.shape[0] // 4, name="ffn2_down")

    G, land = {}, {}
    dr3, dr3b, G["ln3_g"], G["ln3_b"], loss = _ln_bwd(
        None, r3, P["ln3_g"], b=P["ln3_b"], target=target, name="ln3_bwd")
    da2, db2, da2t, db2t = _ffn_bwd_act(dr3b, w2b, a2, b2, name="ffn2_dact")
    gw1b, gw3b = dw(da2t, x2b, "ffn2_dw1"), dw(db2t, x2b, "ffn2_dw3")
    gw2b = dw(h2t, dr3b, "ffn2_dw2", scale=0.5)
    dx2, l1 = _ffn_dx(da2, db2, w1b, w3b, dr3, name="ffn2_dx", comm=_pair_job([gw1b, gw3b, gw2b]))
    p_w1b, p_w3b, p_w2b = [pair_add(k, g, l) for k, g, l in
                           zip(("ffn2_w1", "ffn2_w3", "ffn2_w2"), (gw1b, gw3b, gw2b), l1)]
    dr2, dr2b, G["ln2_g"], G["ln2_b"] = _ln_bwd(dx2, r2, P["ln2_g"], name="ln2_bwd")
    dy = _mm(dr2b, w_wout, "nt", name="dy")
    gwout = shards(_mm(yt, dr2b, out_dtype=BF16, name="dwout"))
    (do, dh, dhg, dmo, G["hgrn_norm_g"], G["mlstm_norm_g"]), (l1,) = _mix_bwd(
        dy, o_fw, o_bw, h_fw, h_bw, zm, P["hgrn_norm_g"], P["mlstm_norm_g"],
        nhm=nhm, gcol=gcol, ocol=ocol, name="mix_bwd", comm=_pair_job([gwout]))
    p_wout = pair_add("w_out", gwout, l1)
    ((dqk_f, dv_f, dgr_f, dgs_f),), (land["ffn2_w1"],) = _run_specs(
        [_mlstm_bwd(qk, zm, zg, gr, br, bcl, dh, st_fw, reverse=False, **mkw)],
        name="mlstm_fw_bwd", comm=_chips_job([p_w1b]))
    ((dqk_b, dv_b, dgr_b, dgs_b),), (land["ffn2_w3"],) = _run_specs(
        [_mlstm_bwd(qk, zm, zg, gr, br, bcl, dh, st_bw, reverse=True, **mkw)],
        name="mlstm_bw_bwd", comm=_chips_job([p_w3b]))
    dmqk, G["conv_w"], G["conv_b"] = _conv_bwd(dqk_f, dqk_b, zm, P["conv_w"], P["conv_b"], col0=col0,
                                               name="conv_bwd")
    ((dhq_f, dhi_f, dhf_fw, dlb_f), (dhq_b, dhi_b, dhf_bw, dlb_b)), (land["ffn2_w2"], land["w_out"]) = \
        _run_specs([_hgrn_bwd(zm, lbf, do, s_fw, wh=wh, reverse=False),
                    _hgrn_bwd(zm, lbb, do, s_bw, wh=wh, reverse=True)],
                   name="hgrn_bwd", comm=_chips_job([p_w2b, p_wout]))
    G["lb"] = jnp.concatenate([dlb_f, dlb_b], axis=0)
    G["ig_b"] = jnp.stack([dgs_f[:nhm, 0], dgs_b[:nhm, 0]])
    G["fg_b"] = jnp.stack([dgs_f[nhm:, 0], dgs_b[nhm:, 0]])
    dzm, dzmt = _dz_pack([(dhq_f, dhq_b), (dhi_f, dhi_b), dhg, dhf_fw, dhf_bw, dmqk, (dv_f, dv_b), dmo],
                         name="dz_pack")
    dgr = jnp.concatenate([dgr_f[:, :nhm], dgr_b[:, :nhm], dgr_f[:, nhm:], dgr_b[:, nhm:]], axis=1)
    dzgt = jnp.zeros((LANES, T), F32).at[:ng].set(dgr.transpose(1, 0, 2).reshape(ng, T)).astype(BF16)
    dzg = jnp.zeros((T, LANES), F32).at[:, :ng].set(dgr.transpose(0, 2, 1).reshape(T, ng)).astype(BF16)
    gwin = shards(jnp.concatenate([_mm(dzmt, x1b, out_dtype=BF16, tk=T, name="dwin"),
                                   _mm(dzgt, x1b, out_dtype=BF16, tk=T, name="dwg")[:ng]], axis=0))
    t, (l1,) = _mm(dzg, w_wgt, add=dr2, add_scale=DN_ALPHA, name="dx1_g", comm=_pair_job([gwin]))
    p_win = pair_add("w_in", gwin, l1)
    dx1, (land["w_in"],) = _mm(dzm, w_int, rows=nz, add=t, tk=1536, name="dx1", comm=_chips_job([p_win]))
    dr1, dr1b, G["ln1_g"], G["ln1_b"] = _ln_bwd(dx1, r1, P["ln1_g"], name="ln1_bwd")
    gw2a = dw(h1t, dr1b, "ffn1_dw2", scale=0.5)
    (da1, db1, da1t, db1t), (l1,) = _ffn_bwd_act(dr1b, w2a, a1, b1, name="ffn1_dact",
                                                 comm=_pair_job([gw2a]))
    p_w2a = pair_add("ffn1_w2", gw2a, l1)
    gw1a, (land["ffn1_w2"],) = _mm(da1t, xb, out_dtype=BF16, tm=704, tk=T, name="ffn1_dw1",
                                   comm=_chips_job([p_w2a]))
    gw1a = shards(gw1a)
    gw3a, (l1,) = _mm(db1t, xb, out_dtype=BF16, tm=704, tk=T, name="ffn1_dw3", comm=_pair_job([gw1a]))
    gw3a = shards(gw3a)
    p_w1a = pair_add("ffn1_w1", gw1a, l1)
    (l1,) = _comm_call(_pair_job([gw3a]), "rs_pair_ffn1_w3")
    p_w3a = pair_add("ffn1_w3", gw3a, l1)
    gx, (land["ffn1_w1"], land["ffn1_w3"]) = _ffn_dx(
        da1, db1, w1a, w3a, dr1, name="ffn1_dx", comm=_chips_job([p_w1a, p_w3a]))
    return loss, gx, land, G


ANY = pl.BlockSpec(memory_space=pl.ANY)


def _place():
    x, y, c = lax.axis_index("x"), lax.axis_index("y"), lax.axis_index("c")
    chips = [(1 - x, y), (x, 1 - y), (1 - x, 1 - y)]
    return x, y, c, chips


def _comm_call(job, name):
    n_in, n_out = len(job.inputs), len(job.out_shape)

    def body(*refs):
        ins, outs, sems = refs[:n_in], refs[n_in:n_in + n_out], refs[n_in + n_out:]
        job.start(ins, outs, sems)
        job.finish(ins, outs, sems)

    return pl.pallas_call(body, name=name, out_shape=job.out_shape, in_specs=[ANY] * n_in,
                          out_specs=[ANY] * n_out, scratch_shapes=job.sems)(*job.inputs)


def _ag_job(parts):
    n = len(parts)

    def copies(ins, outs, sems):
        send_sems, recv_sems, local_sems = sems
        x, y, c, chips = _place()
        me, sibling = (x, y, c), (x, y, 1 - c)

        def copy(a, k, block, to, src=None):
            bx, by, bc = block
            slot = outs[a].at[4 * bx + 2 * by + bc]
            return pltpu.make_async_remote_copy(
                src_ref=slot if src is None else src, dst_ref=slot,
                send_sem=send_sems.at[a, k], recv_sem=recv_sems.at[a, k],
                device_id=to, device_id_type=MESH)

        local = [pltpu.make_async_copy(ins[a], outs[a].at[4 * x + 2 * y + c], local_sems.at[a])
                 for a in range(n)]
        first = []
        for a in range(n):
            first.append(copy(a, 0, me, sibling, src=ins[a]))
            first += [copy(a, 1 + j, me, (*chip, c), src=ins[a]) for j, chip in enumerate(chips)]
        return copy, local, first, chips, me, sibling, c

    def start(ins, outs, sems):
        _, local, first, *_ = copies(ins, outs, sems)
        for cp in local + first:
            cp.start()

    def finish(ins, outs, sems):
        copy, local, first, chips, me, sibling, c = copies(ins, outs, sems)
        passed = []
        for a in range(n):
            for j, chip in enumerate(chips):
                copy(a, 1 + j, (*chip, c), me).wait_recv()
                fwd = copy(a, 4 + j, (*chip, c), sibling)
                fwd.start()
                passed.append(fwd)
        for a in range(n):
            copy(a, 0, sibling, me).wait_recv()
            for j, chip in enumerate(chips):
                copy(a, 4 + j, (*chip, 1 - c), me).wait_recv()
        for cp in first + passed:
            cp.wait_send()
        for cp in local:
            cp.wait()

    return _Job(parts, [jax.ShapeDtypeStruct((N_DEV,) + p.shape, p.dtype) for p in parts],
                [pltpu.SemaphoreType.DMA((n, 7)), pltpu.SemaphoreType.DMA((n, 7)),
                 pltpu.SemaphoreType.DMA((n,))], start, finish)


def _pair_job(gs):
    n = len(gs)

    def copies(g_refs, land_refs, sems):
        send_sems, recv_sems = sems
        x, y, c, _ = _place()
        return [pltpu.make_async_remote_copy(
            src_ref=g_refs[a].at[2 * k + 1 - c], dst_ref=land_refs[a].at[k],
            send_sem=send_sems.at[a, k], recv_sem=recv_sems.at[a, k],
            device_id=(x, y, 1 - c), device_id_type=MESH) for a in range(n) for k in range(4)]

    def start(ins, outs, sems):
        for cp in copies(ins, outs, sems):
            cp.start()

    def finish(ins, outs, sems):
        for cp in copies(ins, outs, sems):
            cp.wait()

    return _Job(gs, [jax.ShapeDtypeStruct((4,) + g.shape[1:], g.dtype) for g in gs],
                [pltpu.SemaphoreType.DMA((n, 4)), pltpu.SemaphoreType.DMA((n, 4))], start, finish)


def _chips_job(ps):
    n = len(ps)

    def copies(p_refs, land_refs, sems):
        send_sems, recv_sems, local_sems = sems
        x, y, c, chips = _place()
        mine = 2 * x + y
        owns = [pltpu.make_async_copy(p_refs[a].at[mine], land_refs[a].at[mine], local_sems.at[a])
                for a in range(n)]
        sends = [pltpu.make_async_remote_copy(
            src_ref=p_refs[a].at[2 * chip[0] + chip[1]], dst_ref=land_refs[a].at[mine],
            send_sem=send_sems.at[a, j], recv_sem=recv_sems.at[a, j],
            device_id=(*chip, c), device_id_type=MESH) for a in range(n) for j, chip in enumerate(chips)]
        recvs = [pltpu.make_async_remote_copy(
            src_ref=p_refs[a].at[mine], dst_ref=land_refs[a].at[2 * chip[0] + chip[1]],
            send_sem=send_sems.at[a, j], recv_sem=recv_sems.at[a, j],
            device_id=(*chip, c), device_id_type=MESH) for a in range(n) for j, chip in enumerate(chips)]
        return owns, sends, recvs

    def start(ins, outs, sems):
        owns, sends, _ = copies(ins, outs, sems)
        for cp in owns + sends:
            cp.start()

    def finish(ins, outs, sems):
        owns, sends, recvs = copies(ins, outs, sems)
        for cp in recvs:
            cp.wait_recv()
        for cp in sends:
            cp.wait_send()
        for cp in owns:
            cp.wait()

    return _Job(ps, [jax.ShapeDtypeStruct(p.shape, p.dtype) for p in ps],
                [pltpu.SemaphoreType.DMA((n, 3)), pltpu.SemaphoreType.DMA((n, 3)),
                 pltpu.SemaphoreType.DMA((n,))], start, finish)


def _all_reduce_small(buf, name):
    rows = buf.shape[0]

    def body(b_ref, o_ref, slots, send_sems, recv_sems):
        x, y, c, _ = _place()
        me = 4 * x + 2 * y + c
        slots[me] = b_ref[...]
        cps = []
        for k in range(1, N_DEV):
            fx, fy, fc = (k >> 2) & 1, (k >> 1) & 1, k & 1
            peer = (x ^ fx, y ^ fy, c ^ fc)
            cps.append(pltpu.make_async_remote_copy(
                src_ref=b_ref, dst_ref=slots.at[me],
                send_sem=send_sems.at[k - 1], recv_sem=recv_sems.at[k - 1],
                device_id=peer, device_id_type=MESH))
        for cp in cps:
            cp.start()
        for k in range(1, N_DEV):
            fx, fy, fc = (k >> 2) & 1, (k >> 1) & 1, k & 1
            src = 4 * (x ^ fx) + 2 * (y ^ fy) + (c ^ fc)
            pltpu.make_async_remote_copy(
                src_ref=b_ref, dst_ref=slots.at[src],
                send_sem=send_sems.at[k - 1], recv_sem=recv_sems.at[k - 1],
                device_id=(x ^ fx, y ^ fy, c ^ fc), device_id_type=MESH).wait_recv()
        for cp in cps:
            cp.wait_send()
        acc = slots[0]
        for k in range(1, N_DEV):
            acc = acc + slots[k]
        o_ref[...] = acc

    vm = pl.BlockSpec(memory_space=pltpu.VMEM)
    return pl.pallas_call(
        body, name=name, out_shape=jax.ShapeDtypeStruct(buf.shape, F32),
        in_specs=[vm], out_specs=vm,
        scratch_shapes=[pltpu.VMEM((N_DEV, rows, LANES), F32),
                        pltpu.SemaphoreType.DMA((N_DEV - 1,)), pltpu.SemaphoreType.DMA((N_DEV - 1,))],
    )(buf)


def _row_tile(rows, pref=2048):
    for t in range(min(pref, rows) - min(pref, rows) % 8, 0, -8):
        if rows % t == 0:
            return t
    return rows


def _tile2(rows, cols, nbuf):
    budget = VMEM_LIMIT // 2 // (2 * nbuf * 4)
    for t in range(min(512, rows) // 16 * 16, 0, -16):
        if rows % t == 0 and t * cols <= budget:
            return t, cols
    return rows, _pick(cols, max(LANES, budget // rows // LANES * LANES))


def _pair_add(g, land, core, name):
    _, r, c = g.shape
    tr, tc = _tile2(r, c, 3)

    def body(core_ref, g_ref, l_ref, o_ref):
        o_ref[...] = (g_ref[...].astype(F32) + l_ref[...].astype(F32)).astype(o_ref.dtype)

    blk = pl.BlockSpec((1, tr, tc), lambda k, i, j, cr: (k, i, j))
    return pl.pallas_call(
        body, name=name, out_shape=jax.ShapeDtypeStruct(land.shape, land.dtype),
        grid_spec=pltpu.PrefetchScalarGridSpec(
            num_scalar_prefetch=1, grid=(4, r // tr, c // tc),
            in_specs=[pl.BlockSpec((1, tr, tc), lambda k, i, j, cr: (2 * k + cr[0], i, j)), blk],
            out_specs=blk),
        compiler_params=_params("parallel", "parallel", "parallel"),
    )(core, g, land)


def _adamw(w, parts, m, v, name):
    shp = w.shape
    cols = shp[-1]
    w2, m2, v2 = (t.reshape(-1, cols) for t in (w, m, v))
    rows = w2.shape[0]
    n = parts.shape[0]
    assert parts.shape[1:] == (rows, cols), (parts.shape, shp)
    tr, tc = _tile2(rows, cols, n + 7)
    c1 = 1.0 / (1.0 - ADAM_B1 ** ADAM_STEP)
    c2 = 1.0 / (1.0 - ADAM_B2 ** ADAM_STEP)

    def body(*refs):
        p_refs = refs[:n]
        w_ref, m_ref, v_ref, g_ref, d_ref, nm_ref, nv_ref = refs[n:]
        gv = p_refs[0][0].astype(F32)
        for p_ref in p_refs[1:]:
            gv = gv + p_ref[0].astype(F32)
        nm = ADAM_B1 * m_ref[...] + (1.0 - ADAM_B1) * gv
        nv = ADAM_B2 * v_ref[...] + (1.0 - ADAM_B2) * (gv * gv)
        g_ref[...] = gv
        nm_ref[...] = nm
        nv_ref[...] = nv
        d_ref[...] = -ADAM_LR * ((nm * c1) / (jnp.sqrt(nv * c2) + ADAM_EPS) + ADAM_WD * w_ref[...])

    blk = pl.BlockSpec((tr, tc), lambda i, j: (i, j))
    slab = lambda k: pl.BlockSpec((1, tr, tc), lambda i, j: (k, i, j))
    out = jax.ShapeDtypeStruct((rows, cols), F32)
    outs = pl.pallas_call(
        body, name=name, out_shape=[out] * 4, grid=(rows // tr, cols // tc),
        in_specs=[slab(k) for k in range(n)] + [blk] * 3, out_specs=[blk] * 4,
        compiler_params=_params("parallel", "parallel"),
    )(*([parts] * n), w2, m2, v2)
    return tuple(o.reshape(shp) for o in outs)


def _pack_rows(arrs):
    rows = []
    for a in arrs:
        flat = a.reshape(-1).astype(F32)
        pad = (-flat.shape[0]) % LANES
        if pad:
            flat = jnp.concatenate([flat, jnp.zeros((pad,), F32)])
        rows.append(flat.reshape(-1, LANES))
    out = jnp.concatenate(rows, axis=0)
    pad = (-out.shape[0]) % 8
    if pad:
        out = jnp.concatenate([out, jnp.zeros((pad, LANES), F32)], axis=0)
    return out


def _unpack_rows(buf, shapes):
    outs, r = [], 0
    for s in shapes:
        n = math.prod(s)
        nr = -(-n // LANES)
        outs.append(buf[r:r + nr].reshape(-1)[:n].reshape(s))
        r += nr
    return outs


BIG = ("ffn1_w1", "ffn1_w3", "ffn1_w2", "w_in", "w_out", "ffn2_w1", "ffn2_w3", "ffn2_w2")
ROW_SHARDED = ("ffn1_w2", "w_out", "ffn2_w2")
SMALL = ("ln1_g", "ln1_b", "hgrn_lb", "hgrn_norm_g", "mlstm_conv_w", "mlstm_conv_b", "mlstm_ig_b",
         "mlstm_fg_b", "mlstm_norm_g", "ln2_g", "ln2_b", "ln3_g", "ln3_b")
WEIGHTS = ("ffn1_w1", "ffn1_w3", "ffn1_w2", "ln1_g", "ln1_b", "w_in", "hgrn_lb", "hgrn_norm_g",
           "mlstm_conv_w", "mlstm_conv_b", "mlstm_ig_b", "mlstm_fg_b", "mlstm_norm_g", "w_out",
           "ln2_g", "ln2_b", "ffn2_w1", "ffn2_w3", "ffn2_w2", "ln3_g", "ln3_b")


def kernel(x, ffn1_w1, ffn1_w3, ffn1_w2, ln1_g, ln1_b, w_in, hgrn_lb, hgrn_norm_g, mlstm_conv_w, mlstm_conv_b, mlstm_ig_b, mlstm_fg_b, mlstm_norm_g, w_out, ln2_g, ln2_b, ffn2_w1, ffn2_w3, ffn2_w2, ln3_g, ln3_b, loss_target, m_ffn1_w1, m_ffn1_w3, m_ffn1_w2, m_ln1_g, m_ln1_b, m_w_in, m_hgrn_lb, m_hgrn_norm_g, m_mlstm_conv_w, m_mlstm_conv_b, m_mlstm_ig_b, m_mlstm_fg_b, m_mlstm_norm_g, m_w_out, m_ln2_g, m_ln2_b, m_ffn2_w1, m_ffn2_w3, m_ffn2_w2, m_ln3_g, m_ln3_b, v_ffn1_w1, v_ffn1_w3, v_ffn1_w2, v_ln1_g, v_ln1_b, v_w_in, v_hgrn_lb, v_hgrn_norm_g, v_mlstm_conv_w, v_mlstm_conv_b, v_mlstm_ig_b, v_mlstm_fg_b, v_mlstm_norm_g, v_w_out, v_ln2_g, v_ln2_b, v_ffn2_w1, v_ffn2_w3, v_ffn2_w2, v_ln3_g, v_ln3_b):
    args = (ffn1_w1, ffn1_w3, ffn1_w2, ln1_g, ln1_b, w_in, hgrn_lb, hgrn_norm_g, mlstm_conv_w,
            mlstm_conv_b, mlstm_ig_b, mlstm_fg_b, mlstm_norm_g, w_out, ln2_g, ln2_b, ffn2_w1, ffn2_w3,
            ffn2_w2, ln3_g, ln3_b)
    ms = (m_ffn1_w1, m_ffn1_w3, m_ffn1_w2, m_ln1_g, m_ln1_b, m_w_in, m_hgrn_lb, m_hgrn_norm_g,
          m_mlstm_conv_w, m_mlstm_conv_b, m_mlstm_ig_b, m_mlstm_fg_b, m_mlstm_norm_g, m_w_out, m_ln2_g,
          m_ln2_b, m_ffn2_w1, m_ffn2_w3, m_ffn2_w2, m_ln3_g, m_ln3_b)
    vs = (v_ffn1_w1, v_ffn1_w3, v_ffn1_w2, v_ln1_g, v_ln1_b, v_w_in, v_hgrn_lb, v_hgrn_norm_g,
          v_mlstm_conv_w, v_mlstm_conv_b, v_mlstm_ig_b, v_mlstm_fg_b, v_mlstm_norm_g, v_w_out, v_ln2_g,
          v_ln2_b, v_ffn2_w1, v_ffn2_w3, v_ffn2_w2, v_ln3_g, v_ln3_b)
    w = dict(zip(WEIGHTS, args))
    m = dict(zip(WEIGHTS, ms))
    v = dict(zip(WEIGHTS, vs))
    core = lax.axis_index("c")
    dev = 4 * lax.axis_index("x") + 2 * lax.axis_index("y") + core
    p_lb = []

    def small_params(gathered_small):
        lb_sh, cw_sh = zip(*[_unpack_rows(gathered_small[i], [hgrn_lb.shape, mlstm_conv_w.shape])
                             for i in range(N_DEV)])
        hgrn_lb_full = jnp.concatenate(lb_sh, axis=-1)
        conv_w_full = jnp.concatenate(cw_sh, axis=-1)[0]
        p_lb.append(jax.nn.softmax(hgrn_lb_full, axis=1))
        return dict(ln1_g=ln1_g, ln1_b=ln1_b, ln2_g=ln2_g, ln2_b=ln2_b, ln3_g=ln3_g, ln3_b=ln3_b,
                    lb=p_lb[0][:, 0], hgrn_norm_g=hgrn_norm_g, conv_w=conv_w_full, conv_b=mlstm_conv_b,
                    ig_b=mlstm_ig_b[0], fg_b=mlstm_fg_b[0], mlstm_norm_g=mlstm_norm_g)

    def rows_first(t, k):
        return t[0] if k in ROW_SHARDED else t[0].T

    loss_row, grad_x, land2, G = _device_step(
        x[0], loss_target[0], {k: rows_first(w[k], k).astype(BF16) for k in BIG},
        _pack_rows([hgrn_lb, mlstm_conv_w]), small_params, jnp.reshape(core, (1,)).astype(jnp.int32))
    p_lb = p_lb[0]

    dlb = G["lb"]
    g_lb = jnp.stack([dlb * p_lb[:, 0] * (1.0 - p_lb[:, 0]), -dlb * p_lb[:, 0] * p_lb[:, 1]], axis=1)
    small_full = {"ln1_g": G["ln1_g"], "ln1_b": G["ln1_b"], "hgrn_lb": g_lb, "hgrn_norm_g": G["hgrn_norm_g"],
                  "mlstm_conv_w": G["conv_w"][None], "mlstm_conv_b": G["conv_b"],
                  "mlstm_ig_b": G["ig_b"][None], "mlstm_fg_b": G["fg_b"][None],
                  "mlstm_norm_g": G["mlstm_norm_g"], "ln2_g": G["ln2_g"], "ln2_b": G["ln2_b"],
                  "ln3_g": G["ln3_g"], "ln3_b": G["ln3_b"]}
    small_list = [small_full[k] for k in SMALL] + [loss_row]
    reduced = _all_reduce_small(_pack_rows(small_list), "all_reduce_small")
    red = _unpack_rows(reduced, [a.shape for a in small_list])
    loss = red[-1][0, 0]
    small_g = {}
    for k, gk in zip(SMALL, red[:-1]):
        if k in ("hgrn_lb", "mlstm_conv_w"):
            n = w[k].shape[-1]
            gk = lax.dynamic_slice_in_dim(gk, dev * n, n, axis=gk.ndim - 1)
        small_g[k] = gk

    grads, delta, new_m, new_v = {}, {}, {}, {}
    for k in BIG:
        res = _adamw(rows_first(w[k], k), land2[k], rows_first(m[k], k), rows_first(v[k], k), "adamw_" + k)
        grads[k], delta[k], new_m[k], new_v[k] = [(r if k in ROW_SHARDED else r.T)[None] for r in res]
    sm = _adamw(_pack_rows([w[k] for k in SMALL]), _pack_rows([small_g[k] for k in SMALL])[None],
                _pack_rows([m[k] for k in SMALL]), _pack_rows([v[k] for k in SMALL]), "adamw_small")
    shapes = [w[k].shape for k in SMALL]
    for dst, buf in zip((grads, delta, new_m, new_v), sm):
        for k, val in zip(SMALL, _unpack_rows(buf, shapes)):
            dst[k] = val
    return (loss, grad_x[None], *[grads[k] for k in WEIGHTS], *[delta[k] for k in WEIGHTS],
            *[new_m[k] for k in WEIGHTS], *[new_v[k] for k in WEIGHTS])
```

```python
import math

import jax
import jax.numpy as jnp
from jax import lax
from jax.experimental import pallas as pl
from jax.experimental.pallas import tpu as pltpu

F32 = jnp.float32
BF16 = jnp.bfloat16

CHUNK = 64
HGRN_HEAD_DIM = 128
CONV_WIDTH = 5
DN_ALPHA = 2.0 ** 0.25
LN_EPS = 1e-5
NORM_EPS = 1e-6
M_INIT = -1e30
NEG = -1e30
EXP_CLAMP = 80.0
ADAM_LR = 0.001
ADAM_B1 = 0.9
ADAM_B2 = 0.999
ADAM_EPS = 1e-08
ADAM_WD = 0.01
ADAM_STEP = 10
N_DEV = 8
LANES = 128
VMEM_LIMIT = 56 * 1024 * 1024
MESH = pl.DeviceIdType.MESH


def _params(*sem):
    return pltpu.CompilerParams(dimension_semantics=sem, vmem_limit_bytes=VMEM_LIMIT)


class _Job:
    def __init__(self, inputs, out_shape, sems, start, finish, middle=None):
        self.inputs, self.out_shape, self.sems = list(inputs), list(out_shape), list(sems)
        self.start, self.finish = start, finish
        self.middle = middle or (lambda ins, outs, sems: None)


def _join(*jobs):
    def split(refs, counts):
        out, p = [], 0
        for c in counts:
            out.append(refs[p:p + c])
            p += c
        return out

    n_in = [len(j.inputs) for j in jobs]
    n_out = [len(j.out_shape) for j in jobs]
    n_sem = [len(j.sems) for j in jobs]

    def start(ins, outs, sems):
        for j, i, o, s in zip(jobs, split(ins, n_in), split(outs, n_out), split(sems, n_sem)):
            j.start(i, o, s)

    def middle(ins, outs, sems):
        for j, i, o, s in zip(jobs, split(ins, n_in), split(outs, n_out), split(sems, n_sem)):
            j.middle(i, o, s)

    def finish(ins, outs, sems):
        for j, i, o, s in zip(jobs, split(ins, n_in), split(outs, n_out), split(sems, n_sem)):
            j.finish(i, o, s)

    return _Job(sum((j.inputs for j in jobs), []), sum((j.out_shape for j in jobs), []),
                sum((j.sems for j in jobs), []), start, finish, middle)


def _pcall(body, *, name, out_shape, grid, in_specs, out_specs, sem, scratch_shapes=(), comm=None):
    single = not isinstance(out_shape, (list, tuple))
    out_shape = [out_shape] if single else list(out_shape)
    out_specs = [out_specs] if single else list(out_specs)
    in_specs, scratch_shapes = list(in_specs), list(scratch_shapes)
    if comm is None:
        call = pl.pallas_call(body, name=name, out_shape=out_shape, grid=grid, in_specs=in_specs,
                              out_specs=out_specs, scratch_shapes=scratch_shapes,
                              compiler_params=_params(*sem))

        def run(*args):
            outs = call(*args)
            return outs[0] if single else outs
        return run

    n_in, n_out, n_sc = len(in_specs), len(out_shape), len(scratch_shapes)
    c_in, c_out = len(comm.inputs), len(comm.out_shape)

    def hosted(*refs):
        ins, cins = refs[:n_in], refs[n_in:n_in + c_in]
        p = n_in + c_in
        outs, couts = refs[p:p + n_out], refs[p + n_out:p + n_out + c_out]
        p += n_out + c_out
        scratch, csems = refs[p:p + n_sc], refs[p + n_sc:]
        first = pl.program_id(0) == 0
        half = pl.program_id(0) == grid[0] // 2
        last = pl.program_id(0) == grid[0] - 1
        for d in range(1, len(grid)):
            first = first & (pl.program_id(d) == 0)
            half = half & (pl.program_id(d) == 0)
            last = last & (pl.program_id(d) == grid[d] - 1)

        @pl.when(first)
        def _():
            comm.start(cins, couts, csems)

        if grid[0] >= 2:
            @pl.when(half)
            def _():
                comm.middle(cins, couts, csems)

        body(*ins, *outs, *scratch)

        @pl.when(last)
        def _():
            if grid[0] < 2:
                comm.middle(cins, couts, csems)
            comm.finish(cins, couts, csems)

    any_spec = pl.BlockSpec(memory_space=pl.ANY)
    call = pl.pallas_call(
        hosted, name=name, out_shape=out_shape + comm.out_shape, grid=grid,
        in_specs=in_specs + [any_spec] * c_in, out_specs=out_specs + [any_spec] * c_out,
        scratch_shapes=scratch_shapes + comm.sems,
        compiler_params=_params(*(["arbitrary"] * len(grid))))

    def run(*args):
        res = call(*args, *comm.inputs)
        outs, couts = res[:n_out], list(res[n_out:])
        return (outs[0] if single else outs), couts
    return run


def _sigmoid(x):
    return 1.0 / (1.0 + jnp.exp(-x))


def _log_sigmoid(x):
    return jnp.minimum(x, 0.0) - jnp.log(1.0 + jnp.exp(-jnp.abs(x)))


def _dot(a, b, dims):
    return lax.dot_general(a.astype(BF16), b.astype(BF16), (dims, ((), ())),
                           preferred_element_type=F32)


def _dot3(a, b, dims):
    ah = a.astype(BF16)
    al = (a - ah.astype(F32)).astype(BF16)
    bh = b.astype(BF16)
    bl = (b - bh.astype(F32)).astype(BF16)
    d = (dims, ((), ()))
    out = lax.dot_general(ah, bh, d, preferred_element_type=F32)
    out = out + lax.dot_general(ah, bl, d, preferred_element_type=F32)
    return out + lax.dot_general(al, bh, d, preferred_element_type=F32)


_DIMS = {"nn": ((1,), (0,)), "nt": ((1,), (1,)), "tn": ((0,), (0,))}


def _dot_nn(a, b):
    return _dot3(a, b, _DIMS["nn"])


def _dot_nt(a, b):
    return _dot3(a, b, _DIMS["nt"])


def _dot_tn(a, b):
    return _dot3(a, b, _DIMS["tn"])


def _split3(x):
    hi = x.astype(BF16)
    r1 = x - hi.astype(F32)
    mid = r1.astype(BF16)
    lo = (r1 - mid.astype(F32)).astype(BF16)
    return hi, mid, lo


def _dot01(mask01, x, mode="nn"):
    m = mask01.astype(BF16)
    hi, mid, lo = _split3(x)
    d = (_DIMS[mode], ((), ()))
    out = lax.dot_general(m, hi, d, preferred_element_type=F32)
    out = out + lax.dot_general(m, mid, d, preferred_element_type=F32)
    return out + lax.dot_general(m, lo, d, preferred_element_type=F32)


def _matmul(a, b, *, mode="nn", out_dtype=F32, tm=512, tn=512, tk=None,
            add=None, scale=1.0, add_scale=1.0, name, comm=None, rows=None):
    if mode == "nn":
        (M, K), (K2, N) = a.shape, b.shape
        K2 = rows or K2
    elif mode == "nt":
        (M, K), (N, K2) = a.shape, b.shape
        N = rows or N
    else:
        (K, M), (K2, N) = a.shape, b.shape
    assert K == K2, (a.shape, b.shape, mode)
    tm, tn = min(tm, M), min(tn, N)
    tk = min(tk or K, K)
    assert M % tm == 0 and N % tn == 0 and K % tk == 0, (M, N, K, tm, tn, tk)
    nk = K // tk
    dims = _DIMS[mode]
    has_add = add is not None

    def body(*refs):
        if has_add:
            a_ref, b_ref, add_ref, o_ref = refs[:4]
        else:
            a_ref, b_ref, o_ref = refs[:3]
            add_ref = None
        acc_ref = refs[-1] if nk > 1 else None

        def finish(acc):
            out = acc if scale == 1.0 else acc * scale
            if has_add:
                out = out + add_ref[...].astype(F32) * add_scale
            o_ref[...] = out.astype(o_ref.dtype)

        if nk == 1:
            finish(_dot(a_ref[...], b_ref[...], dims))
        else:
            k = pl.program_id(2)

            @pl.when(k == 0)
            def _():
                acc_ref[...] = jnp.zeros_like(acc_ref)

            acc_ref[...] += _dot(a_ref[...], b_ref[...], dims)

            @pl.when(k == nk - 1)
            def _():
                finish(acc_ref[...])

    if mode == "tn":
        a_spec = pl.BlockSpec((tk, tm), lambda i, j, k: (k, i))
    else:
        a_spec = pl.BlockSpec((tm, tk), lambda i, j, k: (i, k))
    if mode == "nt":
        b_spec = pl.BlockSpec((tn, tk), lambda i, j, k: (j, k))
    else:
        b_spec = pl.BlockSpec((tk, tn), lambda i, j, k: (k, j))
    o_spec = pl.BlockSpec((tm, tn), lambda i, j, k: (i, j))
    in_specs = [a_spec, b_spec] + ([o_spec] if has_add else [])
    args = (a, b) + ((add,) if has_add else ())
    return _pcall(
        body, name=name, comm=comm,
        out_shape=jax.ShapeDtypeStruct((M, N), out_dtype),
        grid=(M // tm, N // tn, nk),
        in_specs=in_specs, out_specs=o_spec,
        scratch_shapes=[pltpu.VMEM((tm, tn), F32)] if nk > 1 else [],
        sem=("parallel", "parallel", "arbitrary"),
    )(*args)


def _ln_stats(x):
    mu = jnp.mean(x, axis=-1, keepdims=True)
    xc = x - mu
    var = jnp.mean(xc * xc, axis=-1, keepdims=True)
    rstd = lax.rsqrt(var + LN_EPS)
    return xc * rstd, rstd


def _ln_fwd(r, g, b, *, tm=256, name):
    T, D = r.shape
    tm = min(tm, T)

    def body(r_ref, g_ref, b_ref, y_ref, yb_ref):
        xhat, _ = _ln_stats(r_ref[...])
        y = xhat * g_ref[...] + b_ref[...]
        y_ref[...] = y
        yb_ref[...] = y.astype(BF16)

    row = pl.BlockSpec((tm, D), lambda i: (i, 0))
    vec = pl.BlockSpec((1, D), lambda i: (0, 0))
    return pl.pallas_call(
        body, name=name,
        out_shape=[jax.ShapeDtypeStruct((T, D), F32), jax.ShapeDtypeStruct((T, D), BF16)],
        grid=(T // tm,), in_specs=[row, vec, vec], out_specs=[row, row],
        compiler_params=_params("parallel"),
    )(r, g, b)


def _cast(x, *, tm=512, name):
    T, D = x.shape
    tm = min(tm, T)

    def body(x_ref, xb_ref):
        xb_ref[...] = x_ref[...].astype(BF16)

    row = pl.BlockSpec((tm, D), lambda i: (i, 0))
    return pl.pallas_call(
        body, name=name, out_shape=jax.ShapeDtypeStruct((T, D), BF16),
        grid=(T // tm,), in_specs=[row], out_specs=row,
        compiler_params=_params("parallel"),
    )(x)


def _ln_bwd(dy, r, g, *, tm=256, name, b=None, target=None):
    T, D = r.shape
    tm = min(tm, T)
    with_loss = target is not None

    def body(*refs):
        if with_loss:
            r_ref, g_ref, b_ref, t_ref, dr_ref, drb_ref, dg_ref, db_ref, loss_ref = refs
        else:
            dy_ref, r_ref, g_ref, dr_ref, drb_ref, dg_ref, db_ref = refs
        i = pl.program_id(0)
        xhat, rstd = _ln_stats(r_ref[...])
        gg = g_ref[...]
        if with_loss:
            err = xhat * gg + b_ref[...] - t_ref[...]
            dyv = err * (1.0 / D)
            part = jnp.sum(jnp.sum(err * err, axis=1, keepdims=True), axis=0, keepdims=True)
            part = jnp.broadcast_to(part * (0.5 / D), (1, LANES))
        else:
            dyv = dy_ref[...]
        dxh = dyv * gg
        m1 = jnp.mean(dxh, axis=-1, keepdims=True)
        m2 = jnp.mean(dxh * xhat, axis=-1, keepdims=True)
        dr = rstd * (dxh - m1 - xhat * m2)
        dr_ref[...] = dr
        drb_ref[...] = dr.astype(BF16)
        dgp = jnp.sum(dyv * xhat, axis=0, keepdims=True)
        dbp = jnp.sum(dyv, axis=0, keepdims=True)

        @pl.when(i == 0)
        def _():
            dg_ref[...] = dgp
            db_ref[...] = dbp
            if with_loss:
                loss_ref[...] = part

        @pl.when(i > 0)
        def _():
            dg_ref[...] += dgp
            db_ref[...] += dbp
            if with_loss:
                loss_ref[...] += part

    row = pl.BlockSpec((tm, D), lambda i: (i, 0))
    vec = pl.BlockSpec((1, D), lambda i: (0, 0))
    out_shape = [jax.ShapeDtypeStruct((T, D), F32), jax.ShapeDtypeStruct((T, D), BF16),
                 jax.ShapeDtypeStruct((1, D), F32), jax.ShapeDtypeStruct((1, D), F32)]
    out_specs = [row, row, vec, vec]
    if with_loss:
        in_specs, args = [row, vec, vec, row], (r, g, b, target)
        out_shape.append(jax.ShapeDtypeStruct((1, LANES), F32))
        out_specs.append(pl.BlockSpec((1, LANES), lambda i: (0, 0)))
    else:
        in_specs, args = [row, row, vec], (dy, r, g)
    return pl.pallas_call(
        body, name=name, out_shape=out_shape, grid=(T // tm,),
        in_specs=in_specs, out_specs=out_specs,
        compiler_params=_params("arbitrary"),
    )(*args)


def _ffn_up(xb, w1t, w3t, *, a=None, tm=1024, tf=512, name, comm=None):
    F, D = w3t.shape
    T = xb.shape[0]
    tm, tf = _pick(T, tm), _pick(F, tf)
    given = a is not None

    def body(x_ref, w1_ref, w3_ref, a_ref, b_ref, h_ref, ht_ref):
        xv = x_ref[...]
        av = w1_ref[...] if given else _dot(xv, w1_ref[...], _DIMS["nt"])
        b = _dot(xv, w3_ref[...], _DIMS["nt"])
        a_ref[...] = av.astype(BF16)
        b_ref[...] = b.astype(BF16)
        h = av * _sigmoid(av) * b
        h_ref[...] = h.astype(BF16)
        ht_ref[...] = h.T.astype(BF16)

    wsp = pl.BlockSpec((tf, D), lambda j, i: (j, 0))
    osp = pl.BlockSpec((tm, tf), lambda j, i: (i, j))
    act = jax.ShapeDtypeStruct((T, F), BF16)
    return _pcall(
        body, name=name, comm=comm, out_shape=[act, act, act, jax.ShapeDtypeStruct((F, T), BF16)],
        grid=(F // tf, T // tm),
        in_specs=[pl.BlockSpec((tm, D), lambda j, i: (i, 0)), osp if given else wsp, wsp],
        out_specs=[osp, osp, osp, pl.BlockSpec((tf, tm), lambda j, i: (j, i))],
        sem=("parallel", "parallel"),
    )(xb, a if given else w1t, w3t)


def _ffn_bwd_act(drb, w2, a, b, *, tm=1024, tf=512, name, comm=None):
    F, D = w2.shape
    T = drb.shape[0]
    tm, tf = _pick(T, tm), _pick(F, tf)

    def body(dr_ref, w2_ref, a_ref, b_ref, da_ref, db_ref, dat_ref, dbt_ref):
        d = 0.5 * _dot(dr_ref[...], w2_ref[...], _DIMS["nt"])
        av = a_ref[...].astype(F32)
        sig = _sigmoid(av)
        da = d * b_ref[...].astype(F32) * sig * (1.0 + av * (1.0 - sig))
        db = d * av * sig
        da_ref[...] = da.astype(BF16)
        db_ref[...] = db.astype(BF16)
        dat_ref[...] = da.T.astype(BF16)
        dbt_ref[...] = db.T.astype(BF16)

    asp = pl.BlockSpec((tm, tf), lambda i, j: (i, j))
    tsp = pl.BlockSpec((tf, tm), lambda i, j: (j, i))
    out, out_t = jax.ShapeDtypeStruct((T, F), BF16), jax.ShapeDtypeStruct((F, T), BF16)
    return _pcall(
        body, name=name, comm=comm, out_shape=[out, out, out_t, out_t], grid=(T // tm, F // tf),
        in_specs=[pl.BlockSpec((tm, D), lambda i, j: (i, 0)),
                  pl.BlockSpec((tf, D), lambda i, j: (j, 0)), asp, asp],
        out_specs=[asp, asp, tsp, tsp],
        sem=("parallel", "parallel"),
    )(drb, w2, a, b)


def _dz_pack(pieces, *, tm=256, name):
    groups = [p if isinstance(p, tuple) else (p,) for p in pieces]
    T = groups[0][0].shape[0]
    widths = [g[0].shape[1] for g in groups]
    W = sum(widths)
    tm = _pick(T, tm)
    flat = [a for g in groups for a in g]
    n = len(flat)

    def body(*refs):
        o_ref, ot_ref = refs[n], refs[n + 1]
        c, p = 0, 0
        for g, w in zip(groups, widths):
            v = refs[p][...].astype(F32)
            for r in refs[p + 1:p + len(g)]:
                v = v + r[...].astype(F32)
            p += len(g)
            o_ref[:, c:c + w] = v.astype(BF16)
            ot_ref[c:c + w, :] = v.T.astype(BF16)
            c += w

    return pl.pallas_call(
        body, name=name,
        out_shape=[jax.ShapeDtypeStruct((T, W), BF16), jax.ShapeDtypeStruct((W, T), BF16)],
        grid=(T // tm,),
        in_specs=[pl.BlockSpec((tm, a.shape[1]), lambda i: (i, 0)) for a in flat],
        out_specs=[pl.BlockSpec((tm, W), lambda i: (i, 0)), pl.BlockSpec((W, tm), lambda i: (0, i))],
        compiler_params=_params("parallel"),
    )(*flat)


def _ffn_dx(da, db, w1t, w3t, dr, *, tm=1024, tn=1024, tk=None, name, comm=None):
    T, F = da.shape
    D = w1t.shape[1]
    tm, tn = _pick(T, tm), _pick(D, tn)
    tk = _pick(F, tk or F // 4)
    nk = F // tk

    def body(da_ref, db_ref, w1_ref, w3_ref, dr_ref, o_ref, acc_ref):
        k = pl.program_id(2)

        @pl.when(k == 0)
        def _():
            acc_ref[...] = jnp.zeros_like(acc_ref)

        acc_ref[...] += (_dot(da_ref[...], w1_ref[...], _DIMS["nn"])
                         + _dot(db_ref[...], w3_ref[...], _DIMS["nn"]))

        @pl.when(k == nk - 1)
        def _():
            o_ref[...] = DN_ALPHA * dr_ref[...] + acc_ref[...]

    asp = pl.BlockSpec((tm, tk), lambda i, j, k: (i, k))
    wsp = pl.BlockSpec((tk, tn), lambda i, j, k: (k, j))
    osp = pl.BlockSpec((tm, tn), lambda i, j, k: (i, j))
    return _pcall(
        body, name=name, comm=comm, out_shape=jax.ShapeDtypeStruct((T, D), F32),
        grid=(T // tm, D // tn, nk), in_specs=[asp, asp, wsp, wsp, osp], out_specs=osp,
        scratch_shapes=[pltpu.VMEM((tm, tn), F32)],
        sem=("parallel", "parallel", "arbitrary"),
    )(da, db, w1t, w3t, dr)


def _chunk_mask(reverse, transpose=False):
    row = lax.broadcasted_iota(jnp.int32, (CHUNK, CHUNK), 0)
    col = lax.broadcasted_iota(jnp.int32, (CHUNK, CHUNK), 1)
    if reverse != transpose:
        return col >= row
    return col <= row


def _run_specs(specs, *, name, comm=None):
    counts = [(len(s["in_specs"]), len(s["out_specs"]), len(s["scratch_shapes"])) for s in specs]

    def body(*refs):
        p, parts = 0, [[], [], []]
        for kind in range(3):
            for c in counts:
                parts[kind].append(refs[p:p + c[kind]])
                p += c[kind]
        live = [s["body"](*parts[0][k], *parts[1][k], *parts[2][k]) for k, s in enumerate(specs)]
        while live:
            for g in list(live):
                if next(g, StopIteration) is StopIteration:
                    live.remove(g)

    cat = lambda key: [v for s in specs for v in s[key]]
    res = _pcall(body, name=name, comm=comm, out_shape=cat("out_shape"), grid=specs[0]["grid"],
                 in_specs=cat("in_specs"), out_specs=cat("out_specs"),
                 scratch_shapes=cat("scratch_shapes"), sem=("arbitrary",))(*cat("args"))
    outs, couts = res if comm is not None else (res, None)
    split, p = [], 0
    for c in counts:
        split.append(outs[p:p + c[1]])
        p += c[1]
    return (split, couts) if comm is not None else split


def _hgrn_pre(hq, hf, lb):
    sig = _sigmoid(hf)
    f = lb + (1.0 - lb) * sig
    q = hq * _sigmoid(hq) * (HGRN_HEAD_DIM ** -0.5)
    return q, f, sig


def _hgrn_decays(f, cmf, reverse):
    bc = _dot01(cmf, jnp.log(f))
    last = 0 if reverse else CHUNK - 1
    blast = bc[last:last + 1, :]
    bref = bc[CHUNK // 2:CHUNK // 2 + 1, :]
    eq = jnp.exp(jnp.minimum(bc - bref, EXP_CLAMP))
    ek = jnp.exp(jnp.minimum(bref - bc, EXP_CLAMP))
    return bc, blast, eq, ek


def _hgrn_fwd(z, lb, *, wh, reverse):
    T = z.shape[0]
    nch = T // CHUNK
    nh = wh // HGRN_HEAD_DIM
    hd = HGRN_HEAD_DIM

    def ci(i):
        return nch - 1 - i if reverse else i

    def body(hq_ref, hi_ref, hf_ref, lb_ref, o_ref, st_ref, s_ref):
        @pl.when(pl.program_id(0) == 0)
        def _():
            s_ref[...] = jnp.zeros_like(s_ref)

        cm = _chunk_mask(reverse)
        cmf = cm.astype(F32)
        H = range(nh)
        sls = [slice(h * hd, (h + 1) * hd) for h in H]
        q, f, _ = _hgrn_pre(hq_ref[...], hf_ref[...], lb_ref[...])
        v = hi_ref[...]
        k = 1.0 - f
        yield
        bc, blast, eq, ek = _hgrn_decays(f, cmf, reverse)
        yield
        qh, kh, qe = q * eq, k * ek, q * jnp.exp(bc)
        k2, eblast = k * jnp.exp(blast - bc), jnp.exp(blast)
        st = [s_ref[h] for h in H]
        yield
        att = [jnp.where(cm, _dot_nt(qh[:, sl], kh[:, sl]), 0.0) for sl in sls]
        yield
        inter = [_dot_nt(qe[:, sls[h]], st[h]) for h in H]
        yield
        out = [inter[h] + _dot_nn(att[h], v[:, sls[h]]) for h in H]
        yield
        snew = [eblast[:, sls[h]] * st[h] + _dot_tn(v[:, sls[h]], k2[:, sls[h]]) for h in H]
        yield
        for h in H:
            st_ref[0, h] = st[h]
            o_ref[:, sls[h]] = out[h]
            s_ref[h] = snew[h]

    blk = lambda c: pl.BlockSpec((CHUNK, wh), lambda i: (ci(i), c))
    return dict(
        body=body, grid=(nch,), args=[z, z, z, lb],
        out_shape=[jax.ShapeDtypeStruct((T, wh), F32),
                   jax.ShapeDtypeStruct((nch, nh, hd, hd), F32)],
        in_specs=[blk(0), blk(1), blk(3 + int(reverse)),
                  pl.BlockSpec((1, wh), lambda i: (0, 0))],
        out_specs=[blk(0), pl.BlockSpec((1, nh, hd, hd), lambda i: (ci(i), 0, 0, 0))],
        scratch_shapes=[pltpu.VMEM((nh, hd, hd), F32)])


def _hgrn_bwd(z, lb, do, states, *, wh, reverse):
    T = z.shape[0]
    nch = T // CHUNK
    nh = wh // HGRN_HEAD_DIM
    hd = HGRN_HEAD_DIM

    def ci(i):
        return i if reverse else nch - 1 - i

    def body(hq_ref, hi_ref, hf_ref, lb_ref, do_ref, st_ref,
             dhq_ref, dhi_ref, dhf_ref, dlb_ref, ds_ref, gs_ref):
        @pl.when(pl.program_id(0) == 0)
        def _():
            ds_ref[...] = jnp.zeros_like(ds_ref)
            gs_ref[...] = jnp.zeros_like(gs_ref)
            dlb_ref[...] = jnp.zeros_like(dlb_ref)

        cm = _chunk_mask(reverse)
        cmf = cm.astype(F32)
        cmtf = _chunk_mask(reverse, transpose=True).astype(F32)
        H = range(nh)
        sls = [slice(h * hd, (h + 1) * hd) for h in H]
        hq, lb = hq_ref[...], lb_ref[...]
        q, f, sig = _hgrn_pre(hq, hf_ref[...], lb)
        v, dov = hi_ref[...], do_ref[...]
        k = 1.0 - f
        yield
        bc, blast, eq, ek = _hgrn_decays(f, cmf, reverse)
        yield
        ebc, eb2, eblast = jnp.exp(bc), jnp.exp(blast - bc), jnp.exp(blast)
        qh, kh, qe, k2 = q * eq, k * ek, q * ebc, k * eb2
        st = [st_ref[0, h] for h in H]
        dst = [ds_ref[h] for h in H]
        yield
        att = [jnp.where(cm, _dot_nt(qh[:, sl], kh[:, sl]), 0.0) for sl in sls]
        yield
        datt = [jnp.where(cm, _dot_nt(dov[:, sl], v[:, sl]), 0.0) for sl in sls]
        yield
        dq_a = [_dot_nn(datt[h], kh[:, sls[h]]) for h in H]
        yield
        dq_s = [_dot_nn(dov[:, sls[h]], st[h]) for h in H]
        yield
        dk_a = [_dot_tn(datt[h], qh[:, sls[h]]) for h in H]
        yield
        dk_s = [_dot_nn(v[:, sls[h]], dst[h]) for h in H]
        yield
        dv = [_dot_tn(att[h], dov[:, sls[h]]) + _dot_nt(k2[:, sls[h]], dst[h]) for h in H]
        yield
        dsn = [eblast[:, sls[h]] * dst[h] + _dot_tn(dov[:, sls[h]], qe[:, sls[h]]) for h in H]
        for h in H:
            ds_ref[h] = dsn[h]
        yield
        dq = jnp.concatenate(dq_a, axis=1) * eq + ebc * jnp.concatenate(dq_s, axis=1)
        dk = jnp.concatenate(dk_a, axis=1) * ek + eb2 * jnp.concatenate(dk_s, axis=1)
        db = q * dq - k * dk
        yield
        dg = _dot01(cmtf, db) + gs_ref[...]
        gs_ref[...] += jnp.sum(db, axis=0, keepdims=True)
        yield
        df = dg / f - dk
        dhf_ref[...] = (df * (1.0 - lb) * sig * (1.0 - sig)).astype(BF16)
        dlb_ref[...] += jnp.sum(df * (1.0 - sig), axis=0, keepdims=True)
        sq = _sigmoid(hq)
        dhq_ref[...] = (dq * (HGRN_HEAD_DIM ** -0.5) * sq * (1.0 + hq * (1.0 - sq))).astype(BF16)
        dhi_ref[...] = jnp.concatenate(dv, axis=1).astype(BF16)

    blk = lambda c: pl.BlockSpec((CHUNK, wh), lambda i: (ci(i), c))
    vec = pl.BlockSpec((1, wh), lambda i: (0, 0))
    big = jax.ShapeDtypeStruct((T, wh), BF16)
    return dict(
        body=body, grid=(nch,), args=[z, z, z, lb, do, states],
        in_specs=[blk(0), blk(1), blk(3 + int(reverse)), vec, blk(0),
                  pl.BlockSpec((1, nh, hd, hd), lambda i: (ci(i), 0, 0, 0))],
        out_shape=[big, big, big, jax.ShapeDtypeStruct((1, wh), F32)],
        out_specs=[blk(0), blk(0), blk(0), vec],
        scratch_shapes=[pltpu.VMEM((nh, hd, hd), F32), pltpu.VMEM((1, wh), F32)])


def _mlstm_intra(cm, cmt, ig_row, ig_col, xf_row, xf_col, m_st):
    lf_row, lf_col = _log_sigmoid(xf_row), _log_sigmoid(xf_col)
    bcol = jnp.sum(jnp.where(cm, lf_row, 0.0), axis=1, keepdims=True)
    brow = jnp.sum(jnp.where(cmt, lf_col, 0.0), axis=0, keepdims=True)
    blast = jnp.sum(lf_row, axis=1, keepdims=True)
    dmat = jnp.where(cm, bcol - brow + ig_row, NEG)
    m_inter = bcol + m_st
    m_t = jnp.maximum(m_inter, jnp.max(dmat, axis=1, keepdims=True))
    p = jnp.exp(dmat - m_t)
    inter = jnp.exp(m_inter - m_t)
    w_col = blast - bcol + ig_col
    m_new = jnp.maximum(blast + m_st, jnp.max(w_col, axis=0, keepdims=True))
    cs = jnp.exp(blast + m_st - m_new)
    kscale = jnp.exp(w_col - m_new)
    return p, inter, m_t, m_new, cs, kscale


def _mlstm_specs(T, wm, nhm, reverse, backward):
    nch = T // CHUNK
    dm = wm // nhm
    ng = 4 * nhm

    def ci(i):
        fwd_order = nch - 1 - i if reverse else i
        return nch - 1 - fwd_order if backward else fwd_order

    row = lambda w, c: pl.BlockSpec((CHUNK, w), lambda i: (ci(i), c))
    gates_row = pl.BlockSpec((1, ng, CHUNK), lambda i: (ci(i), 0, 0))
    bias_row = pl.BlockSpec((1, LANES), lambda i: (0, 0))
    bias_col = pl.BlockSpec((ng, 1), lambda i: (0, 0))
    st_c = pl.BlockSpec((1, nhm, dm, dm), lambda i: (ci(i), 0, 0, 0))
    st_n = pl.BlockSpec((1, nhm, 1, dm), lambda i: (ci(i), 0, 0, 0))
    st_m = pl.BlockSpec((1, nhm, 1, LANES), lambda i: (ci(i), 0, 0, 0))
    return nch, dm, ng, ci, row, gates_row, bias_row, bias_col, st_c, st_n, st_m


def _mlstm_fwd(qk, z, gc, gr, br, bcl, *, wm, nhm, vcol, reverse):
    T = qk.shape[0]
    nch, dm, ng, ci, row, gates_row, bias_row, bias_col, st_c, st_n, st_m = _mlstm_specs(
        T, wm, nhm, reverse, False)
    d = int(reverse)

    def body(q_ref, k_ref, v_ref, gc_ref, gr_ref, br_ref, bc_ref,
             h_ref, cst_ref, nst_ref, mst_ref, c_ref, n_ref, m_ref):
        @pl.when(pl.program_id(0) == 0)
        def _():
            c_ref[...] = jnp.zeros_like(c_ref)
            n_ref[...] = jnp.zeros_like(n_ref)
            m_ref[...] = jnp.full_like(m_ref, M_INIT)

        cm = _chunk_mask(reverse)
        cmt = _chunk_mask(reverse, transpose=True)
        G = gc_ref[...] + br_ref[...]
        Gr = gr_ref[0] + bc_ref[...]
        H = range(nhm)
        sls = [slice(h * dm, (h + 1) * dm) for h in H]
        m_all = [m_ref[h] for h in H]
        intra = [_mlstm_intra(cm, cmt, Gr[d * nhm + h:d * nhm + h + 1, :], G[:, d * nhm + h:d * nhm + h + 1],
                              Gr[2 * nhm + d * nhm + h:2 * nhm + d * nhm + h + 1, :],
                              G[:, 2 * nhm + d * nhm + h:2 * nhm + d * nhm + h + 1], m_all[h][:, 0:1])
                 for h in H]
        p, inter, m_t, m_new, cs, kscale = zip(*intra)
        yield
        q = [q_ref[:, sl] * (dm ** -0.5) for sl in sls]
        k = [k_ref[:, sl] for sl in sls]
        v = [v_ref[:, sl] for sl in sls]
        ct = [c_ref[h] for h in H]
        n = [n_ref[h] for h in H]
        yield
        sc = [_dot_nt(q[h], k[h]) * p[h] for h in H]
        yield
        qc = [_dot_nt(q[h], ct[h]) for h in H]
        yield
        num = [_dot_nn(sc[h], v[h]) + inter[h] * qc[h] for h in H]
        yield
        den = [jnp.sum(sc[h], axis=1, keepdims=True) + inter[h] * jnp.sum(q[h] * n[h], axis=1, keepdims=True)
               for h in H]
        kw = [k[h] * kscale[h] for h in H]
        yield
        cnew = [cs[h] * ct[h] + _dot_tn(v[h], kw[h]) for h in H]
        yield
        for h in H:
            cst_ref[0, h] = ct[h]
            nst_ref[0, h] = n[h]
            mst_ref[0, h] = m_all[h]
            h_ref[:, sls[h]] = num[h] / jnp.maximum(jnp.abs(den[h]), jnp.exp(-m_t[h]))
            c_ref[h] = cnew[h]
            n_ref[h] = cs[h] * n[h] + jnp.sum(kw[h], axis=0, keepdims=True)
            m_ref[h] = jnp.broadcast_to(m_new[h], (1, LANES))

    return dict(
        body=body, grid=(nch,), args=[qk, qk, z, gc, gr, br, bcl],
        out_shape=[jax.ShapeDtypeStruct((T, wm), F32),
                   jax.ShapeDtypeStruct((nch, nhm, dm, dm), F32),
                   jax.ShapeDtypeStruct((nch, nhm, 1, dm), F32),
                   jax.ShapeDtypeStruct((nch, nhm, 1, LANES), F32)],
        in_specs=[row(wm, 0), row(wm, 1), row(wm, vcol), row(LANES, 0), gates_row, bias_row, bias_col],
        out_specs=[row(wm, 0), st_c, st_n, st_m],
        scratch_shapes=[pltpu.VMEM((nhm, dm, dm), F32), pltpu.VMEM((nhm, 1, dm), F32),
                        pltpu.VMEM((nhm, 1, LANES), F32)])


def _mlstm_bwd(qk, z, gc, gr, br, bcl, dh, states, *, wm, nhm, vcol, reverse):
    T = qk.shape[0]
    nch, dm, ng, ci, row, gates_row, bias_row, bias_col, st_c, st_n, st_m = _mlstm_specs(
        T, wm, nhm, reverse, True)
    d = int(reverse)

    def body(q_ref, k_ref, v_ref, gc_ref, gr_ref, br_ref, bc_ref, dh_ref, cst_ref, nst_ref, mst_ref,
             dqk_ref, dv_ref, dgr_ref, dgs_ref, e_ref, en_ref, fs_ref):
        @pl.when(pl.program_id(0) == 0)
        def _():
            e_ref[...] = jnp.zeros_like(e_ref)
            en_ref[...] = jnp.zeros_like(en_ref)
            fs_ref[...] = jnp.zeros_like(fs_ref)
            dgs_ref[...] = jnp.zeros_like(dgs_ref)

        cm = _chunk_mask(reverse)
        cmt = _chunk_mask(reverse, transpose=True)
        row_i = lax.broadcasted_iota(jnp.int32, (CHUNK, CHUNK), 0)
        col_i = lax.broadcasted_iota(jnp.int32, (CHUNK, CHUNK), 1)
        eye = row_i == col_i
        G = gc_ref[...] + br_ref[...]
        Gr = gr_ref[0] + bc_ref[...]
        H = range(nhm)
        sls = [slice(h * dm, (h + 1) * dm) for h in H]
        xf_row = [Gr[2 * nhm + d * nhm + h:2 * nhm + d * nhm + h + 1, :] for h in H]
        intra = [_mlstm_intra(cm, cmt, Gr[d * nhm + h:d * nhm + h + 1, :], G[:, d * nhm + h:d * nhm + h + 1],
                              xf_row[h], G[:, 2 * nhm + d * nhm + h:2 * nhm + d * nhm + h + 1],
                              mst_ref[0, h][:, 0:1]) for h in H]
        p, inter, m_t, _, cs, kscale = zip(*intra)
        yield
        q = [q_ref[:, sls[h]] * (dm ** -0.5) for h in H]
        k = [k_ref[:, sls[h]] for h in H]
        v = [v_ref[:, sls[h]] for h in H]
        dhv = [dh_ref[:, sls[h]] for h in H]
        ct = [cst_ref[0, h] for h in H]
        n = [nst_ref[0, h] for h in H]
        et = [e_ref[h] for h in H]
        en = [en_ref[h] for h in H]
        carry = [fs_ref[h][:, 0:1] for h in H]
        yield
        sc = [_dot_nt(q[h], k[h]) * p[h] for h in H]
        yield
        qc = [_dot_nt(q[h], ct[h]) for h in H]
        yield
        num = [_dot_nn(sc[h], v[h]) + inter[h] * qc[h] for h in H]
        yield
        den = [jnp.sum(sc[h], axis=1, keepdims=True) + inter[h] * jnp.sum(q[h] * n[h], axis=1, keepdims=True)
               for h in H]
        floor = [jnp.exp(-m_t[h]) for h in H]
        nstab = [jnp.maximum(jnp.abs(den[h]), floor[h]) for h in H]
        gh = [dhv[h] / nstab[h] for h in H]
        dd = [-jnp.sum(dhv[h] * (num[h] / nstab[h]), axis=1, keepdims=True) / nstab[h]
              * jnp.where(jnp.abs(den[h]) > floor[h], jnp.sign(den[h]), 0.0) for h in H]
        yield
        dsqk = [(_dot_nt(gh[h], v[h]) + dd[h]) * p[h] for h in H]
        qi = [q[h] * inter[h] for h in H]
        kw = [k[h] * kscale[h] for h in H]
        yield
        ghc = [_dot_nn(gh[h], ct[h]) for h in H]
        yield
        vet = [_dot_nn(v[h], et[h]) for h in H]
        yield
        dq = [_dot_nn(dsqk[h], k[h]) + inter[h] * (ghc[h] + dd[h] * n[h]) for h in H]
        yield
        dk = [_dot_tn(dsqk[h], q[h]) + kscale[h] * (vet[h] + en[h]) for h in H]
        yield
        dv = [_dot_tn(sc[h], gh[h]) + _dot_nt(kw[h], et[h]) for h in H]
        yield
        e_new = [cs[h] * et[h] + _dot_tn(gh[h], qi[h]) for h in H]
        en_new = [cs[h] * en[h] + jnp.sum(qi[h] * dd[h], axis=0, keepdims=True) for h in H]
        yield
        di_col = [jnp.sum(k[h] * dk[h], axis=1, keepdims=True) for h in H]
        df_col = [jnp.sum(q[h] * dq[h], axis=1, keepdims=True) - di_col[h] for h in H]
        di_row = [jnp.sum(jnp.where(eye, di_col[h], 0.0), axis=0, keepdims=True) for h in H]
        dlf_row = [jnp.sum(jnp.where(cm, df_col[h], 0.0), axis=0, keepdims=True) + carry[h] for h in H]
        dxf_row = [dlf_row[h] * (1.0 - _sigmoid(xf_row[h])) for h in H]
        yield
        for h in H:
            sl = sls[h]
            slk = slice(wm + h * dm, wm + (h + 1) * dm)
            e_ref[h] = e_new[h]
            en_ref[h] = en_new[h]
            fs_ref[h] = jnp.broadcast_to(carry[h] + jnp.sum(df_col[h], axis=0, keepdims=True), (1, LANES))
            dgr_ref[0, h:h + 1, :] = di_row[h]
            dgr_ref[0, nhm + h:nhm + h + 1, :] = dxf_row[h]
            dgs_ref[h:h + 1, :] += jnp.broadcast_to(jnp.sum(di_row[h], axis=1, keepdims=True), (1, LANES))
            dgs_ref[nhm + h:nhm + h + 1, :] += jnp.broadcast_to(
                jnp.sum(dxf_row[h], axis=1, keepdims=True), (1, LANES))
            dqk_ref[:, sl] = (dq[h] * (dm ** -0.5)).astype(BF16)
            dqk_ref[:, slk] = dk[h].astype(BF16)
            dv_ref[:, sl] = dv[h].astype(BF16)

    return dict(
        body=body, grid=(nch,), args=[qk, qk, z, gc, gr, br, bcl, dh] + list(states),
        in_specs=[row(wm, 0), row(wm, 1), row(wm, vcol), row(LANES, 0), gates_row, bias_row, bias_col,
                  row(wm, 0), st_c, st_n, st_m],
        out_shape=[jax.ShapeDtypeStruct((T, 2 * wm), BF16), jax.ShapeDtypeStruct((T, wm), BF16),
                   jax.ShapeDtypeStruct((nch, 2 * nhm, CHUNK), F32),
                   jax.ShapeDtypeStruct((2 * nhm, LANES), F32)],
        out_specs=[row(2 * wm, 0), row(wm, 0),
                   pl.BlockSpec((1, 2 * nhm, CHUNK), lambda i: (ci(i), 0, 0)),
                   pl.BlockSpec((2 * nhm, LANES), lambda i: (0, 0))],
        scratch_shapes=[pltpu.VMEM((nhm, dm, dm), F32), pltpu.VMEM((nhm, 1, dm), F32),
                        pltpu.VMEM((nhm, 1, LANES), F32)])


def _conv_taps(x, w_ref):
    T = x.shape[0]
    t = lax.broadcasted_iota(jnp.int32, x.shape, 0)
    taps = []
    acc = None
    for j in range(CONV_WIDTH):
        s = CONV_WIDTH // 2 - j
        if s == 0:
            xs = x
        else:
            xs = jnp.where((t - s >= 0) & (t - s < T), pltpu.roll(x, s % T, 0), 0.0)
        taps.append(xs)
        term = w_ref[j:j + 1, :] * xs
        acc = term if acc is None else acc + term
    return taps, acc


def _conv_fwd(z, w, b, *, col0, tc=LANES, name):
    T = z.shape[0]
    C2 = w.shape[1]
    assert col0 % tc == 0 and C2 % tc == 0

    def body(z_ref, w_ref, b_ref, o_ref):
        _, acc = _conv_taps(z_ref[...], w_ref)
        c = acc + b_ref[...]
        o_ref[...] = c * _sigmoid(c)

    return pl.pallas_call(
        body, name=name, out_shape=jax.ShapeDtypeStruct((T, C2), F32),
        grid=(C2 // tc,),
        in_specs=[pl.BlockSpec((T, tc), lambda j: (0, col0 // tc + j)),
                  pl.BlockSpec((CONV_WIDTH, tc), lambda j: (0, j)),
                  pl.BlockSpec((1, tc), lambda j: (0, j))],
        out_specs=pl.BlockSpec((T, tc), lambda j: (0, j)),
        compiler_params=_params("parallel"),
    )(z, w, b)


def _conv_bwd(dy_fw, dy_bw, z, w, b, *, col0, tc=LANES, name):
    T = z.shape[0]
    C2 = w.shape[1]

    def body(dyf_ref, dyb_ref, z_ref, w_ref, b_ref, dx_ref, dw_ref, db_ref):
        taps, acc = _conv_taps(z_ref[...], w_ref)
        c = acc + b_ref[...]
        sg = _sigmoid(c)
        dc = (dyf_ref[...].astype(F32) + dyb_ref[...].astype(F32)) * sg * (1.0 + c * (1.0 - sg))
        t = lax.broadcasted_iota(jnp.int32, dc.shape, 0)
        dx = None
        for j in range(CONV_WIDTH):
            s = j - CONV_WIDTH // 2
            if s == 0:
                ds = dc
            else:
                ds = jnp.where((t - s >= 0) & (t - s < T), pltpu.roll(dc, s % T, 0), 0.0)
            term = w_ref[j:j + 1, :] * ds
            dx = term if dx is None else dx + term
            dw_ref[j:j + 1, :] = jnp.sum(dc * taps[j], axis=0, keepdims=True)
        dx_ref[...] = dx.astype(BF16)
        db_ref[...] = jnp.sum(dc, axis=0, keepdims=True)

    col = pl.BlockSpec((T, tc), lambda j: (0, j))
    wsp = pl.BlockSpec((CONV_WIDTH, tc), lambda j: (0, j))
    bsp = pl.BlockSpec((1, tc), lambda j: (0, j))
    return pl.pallas_call(
        body, name=name,
        out_shape=[jax.ShapeDtypeStruct((T, C2), BF16), jax.ShapeDtypeStruct((CONV_WIDTH, C2), F32),
                   jax.ShapeDtypeStruct((1, C2), F32)],
        grid=(C2 // tc,),
        in_specs=[col, col, pl.BlockSpec((T, tc), lambda j: (0, col0 // tc + j)), wsp, bsp],
        out_specs=[col, wsp, bsp],
        compiler_params=_params("parallel"),
    )(dy_fw, dy_bw, z, w, b)


def _mix_heads(o_fw, o_bw, h_fw, h_bw, wh, wm, nhm):
    out = []
    hd = HGRN_HEAD_DIM
    for h in range(wh // hd):
        sl = slice(h * hd, (h + 1) * hd)
        o = o_fw[:, sl] + o_bw[:, sl]
        r = lax.rsqrt(jnp.mean(o * o, axis=-1, keepdims=True) + NORM_EPS)
        out.append((0, sl, o * r, r))
    dm = wm // nhm
    for h in range(nhm):
        sl = slice(h * dm, (h + 1) * dm)
        x = h_fw[:, sl] + h_bw[:, sl]
        xc = x - jnp.mean(x, axis=-1, keepdims=True)
        r = lax.rsqrt(jnp.mean(xc * xc, axis=-1, keepdims=True) + NORM_EPS)
        out.append((1, sl, xc * r, r))
    return out


def _mix_specs(T, wh, wm, tm, gcol, ocol):
    rowh = pl.BlockSpec((tm, wh), lambda i: (i, 0))
    rowm = pl.BlockSpec((tm, wm), lambda i: (i, 0))
    hg = pl.BlockSpec((tm, wh), lambda i: (i, gcol))
    mo = pl.BlockSpec((tm, wm), lambda i: (i, ocol))
    vh = pl.BlockSpec((1, wh), lambda i: (0, 0))
    vm = pl.BlockSpec((1, wm), lambda i: (0, 0))
    return rowh, rowm, hg, mo, vh, vm


def _mix_fwd(o_fw, o_bw, h_fw, h_bw, z, gh, gm, *, nhm, gcol, ocol, tm=256, name):
    T, wh = o_fw.shape
    wm = h_fw.shape[1]
    tm = min(tm, T)
    rowh, rowm, hg, mo, vh, vm = _mix_specs(T, wh, wm, tm, gcol, ocol)

    def body(of_ref, ob_ref, hf_ref, hb_ref, hg_ref, mo_ref, gh_ref, gm_ref, y_ref, yt_ref):
        heads = _mix_heads(of_ref[...], ob_ref[...], hf_ref[...], hb_ref[...], wh, wm, nhm)
        for grp, sl, nv, _ in heads:
            if grp == 0:
                gate = hg_ref[:, sl]
                gate = gate * _sigmoid(gate)
                y = nv * gh_ref[:, sl] * gate
                osl = sl
            else:
                y = nv * gm_ref[:, sl] * _sigmoid(mo_ref[:, sl])
                osl = slice(wh + sl.start, wh + sl.stop)
            y_ref[:, osl] = y.astype(BF16)
            yt_ref[osl, :] = y.T.astype(BF16)

    return pl.pallas_call(
        body, name=name,
        out_shape=[jax.ShapeDtypeStruct((T, wh + wm), BF16), jax.ShapeDtypeStruct((wh + wm, T), BF16)],
        grid=(T // tm,),
        in_specs=[rowh, rowh, rowm, rowm, hg, mo, vh, vm],
        out_specs=[pl.BlockSpec((tm, wh + wm), lambda i: (i, 0)),
                   pl.BlockSpec((wh + wm, tm), lambda i: (0, i))],
        compiler_params=_params("parallel"),
    )(o_fw, o_bw, h_fw, h_bw, z, z, gh, gm)


def _mix_bwd(dy, o_fw, o_bw, h_fw, h_bw, z, gh, gm, *, nhm, gcol, ocol, tm=256, name, comm=None):
    T, wh = o_fw.shape
    wm = h_fw.shape[1]
    tm = min(tm, T)
    rowh, rowm, hg, mo, vh, vm = _mix_specs(T, wh, wm, tm, gcol, ocol)

    def body(dy_ref, of_ref, ob_ref, hf_ref, hb_ref, hg_ref, mo_ref, gh_ref, gm_ref,
             do_ref, dh_ref, dhg_ref, dmo_ref, dgh_ref, dgm_ref):
        @pl.when(pl.program_id(0) == 0)
        def _():
            dgh_ref[...] = jnp.zeros_like(dgh_ref)
            dgm_ref[...] = jnp.zeros_like(dgm_ref)

        heads = _mix_heads(of_ref[...], ob_ref[...], hf_ref[...], hb_ref[...], wh, wm, nhm)
        for grp, sl, nv, r in heads:
            if grp == 0:
                d = dy_ref[:, sl]
                x = hg_ref[:, sl]
                sg = _sigmoid(x)
                gate = x * sg
                g = gh_ref[:, sl]
                dgh_ref[:, sl] += jnp.sum(d * nv * gate, axis=0, keepdims=True)
                dhg_ref[:, sl] = (d * nv * g * sg * (1.0 + x * (1.0 - sg))).astype(BF16)
                t = d * g * gate
                do_ref[:, sl] = r * (t - nv * jnp.mean(t * nv, axis=-1, keepdims=True))
            else:
                d = dy_ref[:, slice(wh + sl.start, wh + sl.stop)]
                sg = _sigmoid(mo_ref[:, sl])
                g = gm_ref[:, sl]
                dgm_ref[:, sl] += jnp.sum(d * nv * sg, axis=0, keepdims=True)
                dmo_ref[:, sl] = (d * nv * g * sg * (1.0 - sg)).astype(BF16)
                t = d * g * sg
                dh_ref[:, sl] = r * (t - jnp.mean(t, axis=-1, keepdims=True)
                                     - nv * jnp.mean(t * nv, axis=-1, keepdims=True))

    return _pcall(
        body, name=name, comm=comm,
        out_shape=[jax.ShapeDtypeStruct((T, wh), F32), jax.ShapeDtypeStruct((T, wm), F32),
                   jax.ShapeDtypeStruct((T, wh), BF16), jax.ShapeDtypeStruct((T, wm), BF16),
                   jax.ShapeDtypeStruct((1, wh), F32), jax.ShapeDtypeStruct((1, wm), F32)],
        grid=(T // tm,),
        in_specs=[pl.BlockSpec((tm, wh + wm), lambda i: (i, 0)), rowh, rowh, rowm, rowm, hg, mo, vh, vm],
        out_specs=[rowh, rowm, rowh, rowm, vh, vm],
        sem=("arbitrary",),
    )(dy, o_fw, o_bw, h_fw, h_bw, z, z, gh, gm)


def _pick(n, pref):
    for c in range(pref - pref % LANES, 0, -LANES):
        if n % c == 0:
            return c
    return n


def _mm(a, b, mode="nn", *, tm=1024, tn=1024, tk=2048, **kw):
    if mode == "nn":
        (M, K), N = a.shape, b.shape[1]
    else:
        (M, K), N = a.shape, kw.get("rows") or b.shape[0]
    return _matmul(a, b, mode=mode, tm=_pick(M, tm), tn=_pick(N, tn), tk=_pick(K, tk), **kw)


def _device_step(x, target, sh, small_parts, small_params, core):
    T, D = x.shape

    def full(g):
        return g.reshape(-1, D)

    def shards(g):
        return g.reshape(N_DEV, -1, D)

    def pair_add(name, g, land):
        return _pair_add(g, land, core, "rs_add_" + name)

    def dw(lhs_t, rhs, name, scale=1.0):
        return shards(_mm(lhs_t, rhs, out_dtype=BF16, scale=scale, tm=704, tk=T, name=name))

    g_w1a, g_small = _comm_call(_ag_job([sh["ffn1_w1"], small_parts]), "ag_ffn1_w1")
    w1a = full(g_w1a)
    P = small_params(g_small)
    wh = P["hgrn_norm_g"].shape[1]
    wm = P["mlstm_norm_g"].shape[1]
    nhm = P["ig_b"].shape[1]
    ng = 4 * nhm
    nch = T // CHUNK
    assert wh == wm and T % CHUNK == 0
    vcol, ocol, gcol = 7, 8, 2
    col0 = 5 * wh
    nz = 5 * wh + 4 * wm
    xb = _cast(x, name="x_cast")
    a1f, (g_w3a,) = _mm(xb, w1a, "nt", name="ffn1_up_a", comm=_ag_job([sh["ffn1_w3"]]))
    w3a = full(g_w3a)
    (a1, b1, h1, h1t), (g_w2a,) = _ffn_up(xb, None, w3a, a=a1f, name="ffn1_up_b",
                                          comm=_ag_job([sh["ffn1_w2"]]))
    w2a = full(g_w2a)
    r1, (g_win,) = _mm(h1, w2a, add=x, scale=0.5, add_scale=DN_ALPHA, tk=w2a.shape[0] // 4,
                       name="ffn1_down", comm=_ag_job([sh["w_in"]]))
    w_int = full(g_win)
    w_wgt = jnp.zeros((LANES, D), BF16).at[:ng].set(w_int[nz:])
    x1, x1b = _ln_fwd(r1, P["ln1_g"], P["ln1_b"], name="ln1")
    zm, (g_wout,) = _mm(x1b, w_int, "nt", rows=nz, name="zm", comm=_ag_job([sh["w_out"]]))
    w_wout = full(g_wout)
    zg = _mm(x1b, w_wgt, "nt", name="zg")
    gr = zg[:, :ng].reshape(nch, CHUNK, ng).transpose(0, 2, 1)
    bias = jnp.concatenate([P["ig_b"].reshape(-1), P["fg_b"].reshape(-1)])
    br = jnp.zeros((1, LANES), F32).at[0, :ng].set(bias)
    bcl = bias.reshape(ng, 1)
    lbf, lbb = P["lb"][0:1], P["lb"][1:2]
    ((o_fw, s_fw), (o_bw, s_bw)), (g_w1b,) = _run_specs(
        [_hgrn_fwd(zm, lbf, wh=wh, reverse=False), _hgrn_fwd(zm, lbb, wh=wh, reverse=True)],
        name="hgrn_fwd", comm=_ag_job([sh["ffn2_w1"]]))
    qk = _conv_fwd(zm, P["conv_w"], P["conv_b"], col0=col0, name="conv")
    mkw = dict(wm=wm, nhm=nhm, vcol=vcol)
    ((h_fw, *st_fw),), (g_w3b,) = _run_specs([_mlstm_fwd(qk, zm, zg, gr, br, bcl, reverse=False, **mkw)],
                                             name="mlstm_fw", comm=_ag_job([sh["ffn2_w3"]]))
    ((h_bw, *st_bw),) = _run_specs([_mlstm_fwd(qk, zm, zg, gr, br, bcl, reverse=True, **mkw)],
                                   name="mlstm_bw")
    w1b, w3b = full(g_w1b), full(g_w3b)
    y, yt = _mix_fwd(o_fw, o_bw, h_fw, h_bw, zm, P["hgrn_norm_g"], P["mlstm_norm_g"],
                     nhm=nhm, gcol=gcol, ocol=ocol, name="mix")
    r2 = _mm(y, w_wout, add=x1, add_scale=DN_ALPHA, name="r2")
    x2, x2b = _ln_fwd(r2, P["ln2_g"], P["ln2_b"], name="ln2")
    (a2, b2, h2, h2t), (g_w2b,) = _ffn_up(x2b, w1b, w3b, name="ffn2_up", comm=_ag_job([sh["ffn2_w2"]]))
    w2b = full(g_w2b)
    r3 = _mm(h2, w2b, add=x2, scale=0.5, add_scale=DN_ALPHA, tk=w2b.shape[0] // 4, name="ffn2_down")

    G, land = {}, {}
    dr3, dr3b, G["ln3_g"], G["ln3_b"], loss = _ln_bwd(
        None, r3, P["ln3_g"], b=P["ln3_b"], target=target, name="ln3_bwd")
    da2, db2, da2t, db2t = _ffn_bwd_act(dr3b, w2b, a2, b2, name="ffn2_dact")
    gw1b, gw3b = dw(da2t, x2b, "ffn2_dw1"), dw(db2t, x2b, "ffn2_dw3")
    gw2b = dw(h2t, dr3b, "ffn2_dw2", scale=0.5)
    dx2, l1 = _ffn_dx(da2, db2, w1b, w3b, dr3, name="ffn2_dx", comm=_pair_job([gw1b, gw3b, gw2b]))
    p_w1b, p_w3b, p_w2b = [pair_add(k, g, l) for k, g, l in
                           zip(("ffn2_w1", "ffn2_w3", "ffn2_w2"), (gw1b, gw3b, gw2b), l1)]
    dr2, dr2b, G["ln2_g"], G["ln2_b"] = _ln_bwd(dx2, r2, P["ln2_g"], name="ln2_bwd")
    dy = _mm(dr2b, w_wout, "nt", name="dy")
    gwout = shards(_mm(yt, dr2b, out_dtype=BF16, name="dwout"))
    (do, dh, dhg, dmo, G["hgrn_norm_g"], G["mlstm_norm_g"]), (l1,) = _mix_bwd(
        dy, o_fw, o_bw, h_fw, h_bw, zm, P["hgrn_norm_g"], P["mlstm_norm_g"],
        nhm=nhm, gcol=gcol, ocol=ocol, name="mix_bwd", comm=_pair_job([gwout]))
    p_wout = pair_add("w_out", gwout, l1)
    ((dqk_f, dv_f, dgr_f, dgs_f),), (land["ffn2_w1"],) = _run_specs(
        [_mlstm_bwd(qk, zm, zg, gr, br, bcl, dh, st_fw, reverse=False, **mkw)],
        name="mlstm_fw_bwd", comm=_chips_job([p_w1b]))
    ((dqk_b, dv_b, dgr_b, dgs_b),), (land["ffn2_w3"],) = _run_specs(
        [_mlstm_bwd(qk, zm, zg, gr, br, bcl, dh, st_bw, reverse=True, **mkw)],
        name="mlstm_bw_bwd", comm=_chips_job([p_w3b]))
    dmqk, G["conv_w"], G["conv_b"] = _conv_bwd(dqk_f, dqk_b, zm, P["conv_w"], P["conv_b"], col0=col0,
                                               name="conv_bwd")
    ((dhq_f, dhi_f, dhf_fw, dlb_f), (dhq_b, dhi_b, dhf_bw, dlb_b)), (land["ffn2_w2"], land["w_out"]) = \
        _run_specs([_hgrn_bwd(zm, lbf, do, s_fw, wh=wh, reverse=False),
                    _hgrn_bwd(zm, lbb, do, s_bw, wh=wh, reverse=True)],
                   name="hgrn_bwd", comm=_chips_job([p_w2b, p_wout]))
    G["lb"] = jnp.concatenate([dlb_f, dlb_b], axis=0)
    G["ig_b"] = jnp.stack([dgs_f[:nhm, 0], dgs_b[:nhm, 0]])
    G["fg_b"] = jnp.stack([dgs_f[nhm:, 0], dgs_b[nhm:, 0]])
    dzm, dzmt = _dz_pack([(dhq_f, dhq_b), (dhi_f, dhi_b), dhg, dhf_fw, dhf_bw, dmqk, (dv_f, dv_b), dmo],
                         name="dz_pack")
    dgr = jnp.concatenate([dgr_f[:, :nhm], dgr_b[:, :nhm], dgr_f[:, nhm:], dgr_b[:, nhm:]], axis=1)
    dzgt = jnp.zeros((LANES, T), F32).at[:ng].set(dgr.transpose(1, 0, 2).reshape(ng, T)).astype(BF16)
    dzg = jnp.zeros((T, LANES), F32).at[:, :ng].set(dgr.transpose(0, 2, 1).reshape(T, ng)).astype(BF16)
    gwin = shards(jnp.concatenate([_mm(dzmt, x1b, out_dtype=BF16, tk=T, name="dwin"),
                                   _mm(dzgt, x1b, out_dtype=BF16, tk=T, name="dwg")[:ng]], axis=0))
    t, (l1,) = _mm(dzg, w_wgt, add=dr2, add_scale=DN_ALPHA, name="dx1_g", comm=_pair_job([gwin]))
    p_win = pair_add("w_in", gwin, l1)
    dx1, (land["w_in"],) = _mm(dzm, w_int, rows=nz, add=t, tk=1536, name="dx1", comm=_chips_job([p_win]))
    dr1, dr1b, G["ln1_g"], G["ln1_b"] = _ln_bwd(dx1, r1, P["ln1_g"], name="ln1_bwd")
    gw2a = dw(h1t, dr1b, "ffn1_dw2", scale=0.5)
    (da1, db1, da1t, db1t), (l1,) = _ffn_bwd_act(dr1b, w2a, a1, b1, name="ffn1_dact",
                                                 comm=_pair_job([gw2a]))
    p_w2a = pair_add("ffn1_w2", gw2a, l1)
    gw1a, (land["ffn1_w2"],) = _mm(da1t, xb, out_dtype=BF16, tm=704, tk=T, name="ffn1_dw1",
                                   comm=_chips_job([p_w2a]))
    gw1a = shards(gw1a)
    gw3a, (l1,) = _mm(db1t, xb, out_dtype=BF16, tm=704, tk=T, name="ffn1_dw3", comm=_pair_job([gw1a]))
    gw3a = shards(gw3a)
    p_w1a = pair_add("ffn1_w1", gw1a, l1)
    (l1,) = _comm_call(_pair_job([gw3a]), "rs_pair_ffn1_w3")
    p_w3a = pair_add("ffn1_w3", gw3a, l1)
    gx, (land["ffn1_w1"], land["ffn1_w3"]) = _ffn_dx(
        da1, db1, w1a, w3a, dr1, name="ffn1_dx", comm=_chips_job([p_w1a, p_w3a]))
    return loss, gx, land, G


ANY = pl.BlockSpec(memory_space=pl.ANY)


def _place():
    x, y, c = lax.axis_index("x"), lax.axis_index("y"), lax.axis_index("c")
    chips = [(1 - x, y), (x, 1 - y), (1 - x, 1 - y)]
    return x, y, c, chips


def _comm_call(job, name):
    n_in, n_out = len(job.inputs), len(job.out_shape)

    def body(*refs):
        ins, outs, sems = refs[:n_in], refs[n_in:n_in + n_out], refs[n_in + n_out:]
        job.start(ins, outs, sems)
        job.middle(ins, outs, sems)
        job.finish(ins, outs, sems)

    return pl.pallas_call(body, name=name, out_shape=job.out_shape, in_specs=[ANY] * n_in,
                          out_specs=[ANY] * n_out, scratch_shapes=job.sems)(*job.inputs)


def _ag_job(parts):
    n = len(parts)
    halves = [p.shape[0] // 2 // 16 * 16 or p.shape[0] // 2 // 8 * 8 or p.shape[0] for p in parts]
    cut = [h < p.shape[0] for h, p in zip(halves, parts)]

    def helpers(ins, outs, sems):
        send_sems, recv_sems, local_sems = sems
        x, y, c, _ = _place()
        me, sibling = (x, y, c), (x, y, 1 - c)
        xn, yn, dg = (1 - x, y), (x, 1 - y), (1 - x, 1 - y)

        def slot(a, block, part=None):
            bx, by, bc = block
            ref = outs[a].at[4 * bx + 2 * by + bc]
            if part == 0:
                return ref.at[pl.ds(0, halves[a])]
            if part == 1:
                return ref.at[pl.ds(halves[a], parts[a].shape[0] - halves[a])]
            return ref

        def copy(a, k, block, to, src=None, part=None):
            dst = slot(a, block, part)
            return pltpu.make_async_remote_copy(
                src_ref=dst if src is None else src, dst_ref=dst,
                send_sem=send_sems.at[a, k], recv_sem=recv_sems.at[a, k],
                device_id=to, device_id_type=MESH)

        local = [pltpu.make_async_copy(ins[a], slot(a, me), local_sems.at[a]) for a in range(n)]
        return copy, local, me, sibling, xn, yn, dg, c

    def start(ins, outs, sems):
        copy, local, me, sibling, xn, yn, dg, c = helpers(ins, outs, sems)
        for cp in local:
            cp.start()
        for a in range(n):
            copy(a, 0, me, sibling, src=ins[a]).start()
            copy(a, 1, me, (*xn, c), src=ins[a]).start()
            copy(a, 2, me, (*yn, c), src=ins[a]).start()

    def middle(ins, outs, sems):
        copy, local, me, sibling, xn, yn, dg, c = helpers(ins, outs, sems)
        for a in range(n):
            copy(a, 1, (*xn, c), me).wait_recv()
            copy(a, 3, (*xn, c), (*yn, c), part=0).start()
            copy(a, 5, (*xn, c), sibling).start()
        for a in range(n):
            copy(a, 2, (*yn, c), me).wait_recv()
            if cut[a]:
                copy(a, 4, (*yn, c), (*xn, c), part=1).start()
            copy(a, 6, (*yn, c), sibling).start()

    def finish(ins, outs, sems):
        copy, local, me, sibling, xn, yn, dg, c = helpers(ins, outs, sems)
        for a in range(n):
            copy(a, 3, (*dg, c), me, part=0).wait_recv()
            if cut[a]:
                copy(a, 4, (*dg, c), me, part=1).wait_recv()
            copy(a, 7, (*dg, c), sibling).start()
        for a in range(n):
            copy(a, 0, sibling, me).wait_recv()
            copy(a, 5, (*xn, 1 - c), me).wait_recv()
            copy(a, 6, (*yn, 1 - c), me).wait_recv()
            copy(a, 7, (*dg, 1 - c), me).wait_recv()
        for a in range(n):
            copy(a, 0, me, sibling, src=ins[a]).wait_send()
            copy(a, 1, me, (*xn, c), src=ins[a]).wait_send()
            copy(a, 2, me, (*yn, c), src=ins[a]).wait_send()
            copy(a, 3, (*xn, c), (*yn, c), part=0).wait_send()
            if cut[a]:
                copy(a, 4, (*yn, c), (*xn, c), part=1).wait_send()
            for k, blk in ((5, xn), (6, yn), (7, dg)):
                copy(a, k, (*blk, c), sibling).wait_send()
        for cp in local:
            cp.wait()

    return _Job(parts, [jax.ShapeDtypeStruct((N_DEV,) + p.shape, p.dtype) for p in parts],
                [pltpu.SemaphoreType.DMA((n, 8)), pltpu.SemaphoreType.DMA((n, 8)),
                 pltpu.SemaphoreType.DMA((n,))], start, finish, middle)


def _pair_job(gs):
    n = len(gs)

    def copies(g_refs, land_refs, sems):
        send_sems, recv_sems = sems
        x, y, c, _ = _place()
        return [pltpu.make_async_remote_copy(
            src_ref=g_refs[a].at[2 * k + 1 - c], dst_ref=land_refs[a].at[k],
            send_sem=send_sems.at[a, k], recv_sem=recv_sems.at[a, k],
            device_id=(x, y, 1 - c), device_id_type=MESH) for a in range(n) for k in range(4)]

    def start(ins, outs, sems):
        for cp in copies(ins, outs, sems):
            cp.start()

    def finish(ins, outs, sems):
        for cp in copies(ins, outs, sems):
            cp.wait()

    return _Job(gs, [jax.ShapeDtypeStruct((4,) + g.shape[1:], g.dtype) for g in gs],
                [pltpu.SemaphoreType.DMA((n, 4)), pltpu.SemaphoreType.DMA((n, 4))], start, finish)


def _chips_job(ps):
    n = len(ps)

    def copies(p_refs, land_refs, sems):
        send_sems, recv_sems, local_sems = sems
        x, y, c, chips = _place()
        mine = 2 * x + y
        owns = [pltpu.make_async_copy(p_refs[a].at[mine], land_refs[a].at[mine], local_sems.at[a])
                for a in range(n)]
        sends = [pltpu.make_async_remote_copy(
            src_ref=p_refs[a].at[2 * chip[0] + chip[1]], dst_ref=land_refs[a].at[mine],
            send_sem=send_sems.at[a, j], recv_sem=recv_sems.at[a, j],
            device_id=(*chip, c), device_id_type=MESH) for a in range(n) for j, chip in enumerate(chips)]
        recvs = [pltpu.make_async_remote_copy(
            src_ref=p_refs[a].at[mine], dst_ref=land_refs[a].at[2 * chip[0] + chip[1]],
            send_sem=send_sems.at[a, j], recv_sem=recv_sems.at[a, j],
            device_id=(*chip, c), device_id_type=MESH) for a in range(n) for j, chip in enumerate(chips)]
        return owns, sends, recvs

    def start(ins, outs, sems):
        owns, sends, _ = copies(ins, outs, sems)
        for cp in owns + sends:
            cp.start()

    def finish(ins, outs, sems):
        owns, sends, recvs = copies(ins, outs, sems)
        for cp in recvs:
            cp.wait_recv()
        for cp in sends:
            cp.wait_send()
        for cp in owns:
            cp.wait()

    return _Job(ps, [jax.ShapeDtypeStruct(p.shape, p.dtype) for p in ps],
                [pltpu.SemaphoreType.DMA((n, 3)), pltpu.SemaphoreType.DMA((n, 3)),
                 pltpu.SemaphoreType.DMA((n,))], start, finish)


def _all_reduce_small(buf, name):
    rows = buf.shape[0]

    def body(b_ref, o_ref, slots, send_sems, recv_sems):
        x, y, c, _ = _place()
        me = 4 * x + 2 * y + c
        slots[me] = b_ref[...]
        cps = []
        for k in range(1, N_DEV):
            fx, fy, fc = (k >> 2) & 1, (k >> 1) & 1, k & 1
            peer = (x ^ fx, y ^ fy, c ^ fc)
            cps.append(pltpu.make_async_remote_copy(
                src_ref=b_ref, dst_ref=slots.at[me],
                send_sem=send_sems.at[k - 1], recv_sem=recv_sems.at[k - 1],
                device_id=peer, device_id_type=MESH))
        for cp in cps:
            cp.start()
        for k in range(1, N_DEV):
            fx, fy, fc = (k >> 2) & 1, (k >> 1) & 1, k & 1
            src = 4 * (x ^ fx) + 2 * (y ^ fy) + (c ^ fc)
            pltpu.make_async_remote_copy(
                src_ref=b_ref, dst_ref=slots.at[src],
                send_sem=send_sems.at[k - 1], recv_sem=recv_sems.at[k - 1],
                device_id=(x ^ fx, y ^ fy, c ^ fc), device_id_type=MESH).wait_recv()
        for cp in cps:
            cp.wait_send()
        acc = slots[0]
        for k in range(1, N_DEV):
            acc = acc + slots[k]
        o_ref[...] = acc

    vm = pl.BlockSpec(memory_space=pltpu.VMEM)
    return pl.pallas_call(
        body, name=name, out_shape=jax.ShapeDtypeStruct(buf.shape, F32),
        in_specs=[vm], out_specs=vm,
        scratch_shapes=[pltpu.VMEM((N_DEV, rows, LANES), F32),
                        pltpu.SemaphoreType.DMA((N_DEV - 1,)), pltpu.SemaphoreType.DMA((N_DEV - 1,))],
    )(buf)


def _row_tile(rows, pref=2048):
    for t in range(min(pref, rows) - min(pref, rows) % 8, 0, -8):
        if rows % t == 0:
            return t
    return rows


def _tile2(rows, cols, nbuf):
    budget = VMEM_LIMIT // 2 // (2 * nbuf * 4)
    for t in range(min(512, rows) // 16 * 16, 0, -16):
        if rows % t == 0 and t * cols <= budget:
            return t, cols
    return rows, _pick(cols, max(LANES, budget // rows // LANES * LANES))


def _pair_add(g, land, core, name):
    _, r, c = g.shape
    tr, tc = _tile2(r, c, 3)

    def body(core_ref, g_ref, l_ref, o_ref):
        o_ref[...] = (g_ref[...].astype(F32) + l_ref[...].astype(F32)).astype(o_ref.dtype)

    blk = pl.BlockSpec((1, tr, tc), lambda k, i, j, cr: (k, i, j))
    return pl.pallas_call(
        body, name=name, out_shape=jax.ShapeDtypeStruct(land.shape, land.dtype),
        grid_spec=pltpu.PrefetchScalarGridSpec(
            num_scalar_prefetch=1, grid=(4, r // tr, c // tc),
            in_specs=[pl.BlockSpec((1, tr, tc), lambda k, i, j, cr: (2 * k + cr[0], i, j)), blk],
            out_specs=blk),
        compiler_params=_params("parallel", "parallel", "parallel"),
    )(core, g, land)


def _adamw(w, parts, m, v, name):
    shp = w.shape
    cols = shp[-1]
    w2, m2, v2 = (t.reshape(-1, cols) for t in (w, m, v))
    rows = w2.shape[0]
    n = parts.shape[0]
    assert parts.shape[1:] == (rows, cols), (parts.shape, shp)
    tr, tc = _tile2(rows, cols, n + 7)
    c1 = 1.0 / (1.0 - ADAM_B1 ** ADAM_STEP)
    c2 = 1.0 / (1.0 - ADAM_B2 ** ADAM_STEP)

    def body(*refs):
        p_refs = refs[:n]
        w_ref, m_ref, v_ref, g_ref, d_ref, nm_ref, nv_ref = refs[n:]
        gv = p_refs[0][0].astype(F32)
        for p_ref in p_refs[1:]:
            gv = gv + p_ref[0].astype(F32)
        nm = ADAM_B1 * m_ref[...] + (1.0 - ADAM_B1) * gv
        nv = ADAM_B2 * v_ref[...] + (1.0 - ADAM_B2) * (gv * gv)
        g_ref[...] = gv
        nm_ref[...] = nm
        nv_ref[...] = nv
        d_ref[...] = -ADAM_LR * ((nm * c1) / (jnp.sqrt(nv * c2) + ADAM_EPS) + ADAM_WD * w_ref[...])

    blk = pl.BlockSpec((tr, tc), lambda i, j: (i, j))
    slab = lambda k: pl.BlockSpec((1, tr, tc), lambda i, j: (k, i, j))
    out = jax.ShapeDtypeStruct((rows, cols), F32)
    outs = pl.pallas_call(
        body, name=name, out_shape=[out] * 4, grid=(rows // tr, cols // tc),
        in_specs=[slab(k) for k in range(n)] + [blk] * 3, out_specs=[blk] * 4,
        compiler_params=_params("parallel", "parallel"),
    )(*([parts] * n), w2, m2, v2)
    return tuple(o.reshape(shp) for o in outs)


def _pack_rows(arrs):
    rows = []
    for a in arrs:
        flat = a.reshape(-1).astype(F32)
        pad = (-flat.shape[0]) % LANES
        if pad:
            flat = jnp.concatenate([flat, jnp.zeros((pad,), F32)])
        rows.append(flat.reshape(-1, LANES))
    out = jnp.concatenate(rows, axis=0)
    pad = (-out.shape[0]) % 8
    if pad:
        out = jnp.concatenate([out, jnp.zeros((pad, LANES), F32)], axis=0)
    return out


def _unpack_rows(buf, shapes):
    outs, r = [], 0
    for s in shapes:
        n = math.prod(s)
        nr = -(-n // LANES)
        outs.append(buf[r:r + nr].reshape(-1)[:n].reshape(s))
        r += nr
    return outs


BIG = ("ffn1_w1", "ffn1_w3", "ffn1_w2", "w_in", "w_out", "ffn2_w1", "ffn2_w3", "ffn2_w2")
ROW_SHARDED = ("ffn1_w2", "w_out", "ffn2_w2")
SMALL = ("ln1_g", "ln1_b", "hgrn_lb", "hgrn_norm_g", "mlstm_conv_w", "mlstm_conv_b", "mlstm_ig_b",
         "mlstm_fg_b", "mlstm_norm_g", "ln2_g", "ln2_b", "ln3_g", "ln3_b")
WEIGHTS = ("ffn1_w1", "ffn1_w3", "ffn1_w2", "ln1_g", "ln1_b", "w_in", "hgrn_lb", "hgrn_norm_g",
           "mlstm_conv_w", "mlstm_conv_b", "mlstm_ig_b", "mlstm_fg_b", "mlstm_norm_g", "w_out",
           "ln2_g", "ln2_b", "ffn2_w1", "ffn2_w3", "ffn2_w2", "ln3_g", "ln3_b")


def kernel(x, ffn1_w1, ffn1_w3, ffn1_w2, ln1_g, ln1_b, w_in, hgrn_lb, hgrn_norm_g, mlstm_conv_w, mlstm_conv_b, mlstm_ig_b, mlstm_fg_b, mlstm_norm_g, w_out, ln2_g, ln2_b, ffn2_w1, ffn2_w3, ffn2_w2, ln3_g, ln3_b, loss_target, m_ffn1_w1, m_ffn1_w3, m_ffn1_w2, m_ln1_g, m_ln1_b, m_w_in, m_hgrn_lb, m_hgrn_norm_g, m_mlstm_conv_w, m_mlstm_conv_b, m_mlstm_ig_b, m_mlstm_fg_b, m_mlstm_norm_g, m_w_out, m_ln2_g, m_ln2_b, m_ffn2_w1, m_ffn2_w3, m_ffn2_w2, m_ln3_g, m_ln3_b, v_ffn1_w1, v_ffn1_w3, v_ffn1_w2, v_ln1_g, v_ln1_b, v_w_in, v_hgrn_lb, v_hgrn_norm_g, v_mlstm_conv_w, v_mlstm_conv_b, v_mlstm_ig_b, v_mlstm_fg_b, v_mlstm_norm_g, v_w_out, v_ln2_g, v_ln2_b, v_ffn2_w1, v_ffn2_w3, v_ffn2_w2, v_ln3_g, v_ln3_b):
    args = (ffn1_w1, ffn1_w3, ffn1_w2, ln1_g, ln1_b, w_in, hgrn_lb, hgrn_norm_g, mlstm_conv_w,
            mlstm_conv_b, mlstm_ig_b, mlstm_fg_b, mlstm_norm_g, w_out, ln2_g, ln2_b, ffn2_w1, ffn2_w3,
            ffn2_w2, ln3_g, ln3_b)
    ms = (m_ffn1_w1, m_ffn1_w3, m_ffn1_w2, m_ln1_g, m_ln1_b, m_w_in, m_hgrn_lb, m_hgrn_norm_g,
          m_mlstm_conv_w, m_mlstm_conv_b, m_mlstm_ig_b, m_mlstm_fg_b, m_mlstm_norm_g, m_w_out, m_ln2_g,
          m_ln2_b, m_ffn2_w1, m_ffn2_w3, m_ffn2_w2, m_ln3_g, m_ln3_b)
    vs = (v_ffn1_w1, v_ffn1_w3, v_ffn1_w2, v_ln1_g, v_ln1_b, v_w_in, v_hgrn_lb, v_hgrn_norm_g,
          v_mlstm_conv_w, v_mlstm_conv_b, v_mlstm_ig_b, v_mlstm_fg_b, v_mlstm_norm_g, v_w_out, v_ln2_g,
          v_ln2_b, v_ffn2_w1, v_ffn2_w3, v_ffn2_w2, v_ln3_g, v_ln3_b)
    w = dict(zip(WEIGHTS, args))
    m = dict(zip(WEIGHTS, ms))
    v = dict(zip(WEIGHTS, vs))
    core = lax.axis_index("c")
    dev = 4 * lax.axis_index("x") + 2 * lax.axis_index("y") + core
    p_lb = []

    def small_params(gathered_small):
        lb_sh, cw_sh = zip(*[_unpack_rows(gathered_small[i], [hgrn_lb.shape, mlstm_conv_w.shape])
                             for i in range(N_DEV)])
        hgrn_lb_full = jnp.concatenate(lb_sh, axis=-1)
        conv_w_full = jnp.concatenate(cw_sh, axis=-1)[0]
        p_lb.append(jax.nn.softmax(hgrn_lb_full, axis=1))
        return dict(ln1_g=ln1_g, ln1_b=ln1_b, ln2_g=ln2_g, ln2_b=ln2_b, ln3_g=ln3_g, ln3_b=ln3_b,
                    lb=p_lb[0][:, 0], hgrn_norm_g=hgrn_norm_g, conv_w=conv_w_full, conv_b=mlstm_conv_b,
                    ig_b=mlstm_ig_b[0], fg_b=mlstm_fg_b[0], mlstm_norm_g=mlstm_norm_g)

    def rows_first(t, k):
        return t[0] if k in ROW_SHARDED else t[0].T

    loss_row, grad_x, land2, G = _device_step(
        x[0], loss_target[0], {k: rows_first(w[k], k).astype(BF16) for k in BIG},
        _pack_rows([hgrn_lb, mlstm_conv_w]), small_params, jnp.reshape(core, (1,)).astype(jnp.int32))
    p_lb = p_lb[0]

    dlb = G["lb"]
    g_lb = jnp.stack([dlb * p_lb[:, 0] * (1.0 - p_lb[:, 0]), -dlb * p_lb[:, 0] * p_lb[:, 1]], axis=1)
    small_full = {"ln1_g": G["ln1_g"], "ln1_b": G["ln1_b"], "hgrn_lb": g_lb, "hgrn_norm_g": G["hgrn_norm_g"],
                  "mlstm_conv_w": G["conv_w"][None], "mlstm_conv_b": G["conv_b"],
                  "mlstm_ig_b": G["ig_b"][None], "mlstm_fg_b": G["fg_b"][None],
                  "mlstm_norm_g": G["mlstm_norm_g"], "ln2_g": G["ln2_g"], "ln2_b": G["ln2_b"],
                  "ln3_g": G["ln3_g"], "ln3_b": G["ln3_b"]}
    small_list = [small_full[k] for k in SMALL] + [loss_row]
    reduced = _all_reduce_small(_pack_rows(small_list), "all_reduce_small")
    red = _unpack_rows(reduced, [a.shape for a in small_list])
    loss = red[-1][0, 0]
    small_g = {}
    for k, gk in zip(SMALL, red[:-1]):
        if k in ("hgrn_lb", "mlstm_conv_w"):
            n = w[k].shape[-1]
            gk = lax.dynamic_slice_in_dim(gk, dev * n, n, axis=gk.ndim - 1)
        small_g[k] = gk

    grads, delta, new_m, new_v = {}, {}, {}, {}
    for k in BIG:
        res = _adamw(rows_first(w[k], k), land2[k], rows_first(m[k], k), rows_first(v[k], k), "adamw_" + k)
        grads[k], delta[k], new_m[k], new_v[k] = [(r if k in ROW_SHARDED else r.T)[None] for r in res]
    sm = _adamw(_pack_rows([w[k] for k in SMALL]), _pack_rows([small_g[k] for k in SMALL])[None],
                _pack_rows([m[k] for k in SMALL]), _pack_rows([v[k] for k in SMALL]), "adamw_small")
    shapes = [w[k].shape for k in SMALL]
    for dst, buf in zip((grads, delta, new_m, new_v), sm):
        for k, val in zip(SMALL, _unpack_rows(buf, shapes)):
            dst[k] = val
    return (loss, grad_x[None], *[grads[k] for k in WEIGHTS], *[delta[k] for k in WEIGHTS],
            *[new_m[k] for k in WEIGHTS], *[new_v[k] for k in WEIGHTS])
```

```python
import math

import jax
import jax.numpy as jnp
from jax import lax
from jax.experimental import pallas as pl
from jax.experimental.pallas import tpu as pltpu

F32 = jnp.float32
BF16 = jnp.bfloat16

CHUNK = 64
HGRN_HEAD_DIM = 128
CONV_WIDTH = 5
DN_ALPHA = 2.0 ** 0.25
LN_EPS = 1e-5
NORM_EPS = 1e-6
M_INIT = -1e30
NEG = -1e30
EXP_CLAMP = 80.0
ADAM_LR = 0.001
ADAM_B1 = 0.9
ADAM_B2 = 0.999
ADAM_EPS = 1e-08
ADAM_WD = 0.01
ADAM_STEP = 10
N_DEV = 8
LANES = 128
VMEM_LIMIT = 56 * 1024 * 1024
MESH = pl.DeviceIdType.MESH


def _params(*sem):
    return pltpu.CompilerParams(dimension_semantics=sem, vmem_limit_bytes=VMEM_LIMIT)


class _Job:
    def __init__(self, inputs, out_shape, sems, start, finish, middle=None):
        self.inputs, self.out_shape, self.sems = list(inputs), list(out_shape), list(sems)
        self.start, self.finish = start, finish
        self.middle = middle or (lambda ins, outs, sems: None)


def _join(*jobs):
    def split(refs, counts):
        out, p = [], 0
        for c in counts:
            out.append(refs[p:p + c])
            p += c
        return out

    n_in = [len(j.inputs) for j in jobs]
    n_out = [len(j.out_shape) for j in jobs]
    n_sem = [len(j.sems) for j in jobs]

    def start(ins, outs, sems):
        for j, i, o, s in zip(jobs, split(ins, n_in), split(outs, n_out), split(sems, n_sem)):
            j.start(i, o, s)

    def middle(ins, outs, sems):
        for j, i, o, s in zip(jobs, split(ins, n_in), split(outs, n_out), split(sems, n_sem)):
            j.middle(i, o, s)

    def finish(ins, outs, sems):
        for j, i, o, s in zip(jobs, split(ins, n_in), split(outs, n_out), split(sems, n_sem)):
            j.finish(i, o, s)

    return _Job(sum((j.inputs for j in jobs), []), sum((j.out_shape for j in jobs), []),
                sum((j.sems for j in jobs), []), start, finish, middle)


def _pcall(body, *, name, out_shape, grid, in_specs, out_specs, sem, scratch_shapes=(), comm=None):
    single = not isinstance(out_shape, (list, tuple))
    out_shape = [out_shape] if single else list(out_shape)
    out_specs = [out_specs] if single else list(out_specs)
    in_specs, scratch_shapes = list(in_specs), list(scratch_shapes)
    if comm is None:
        call = pl.pallas_call(body, name=name, out_shape=out_shape, grid=grid, in_specs=in_specs,
                              out_specs=out_specs, scratch_shapes=scratch_shapes,
                              compiler_params=_params(*sem))

        def run(*args):
            outs = call(*args)
            return outs[0] if single else outs
        return run

    n_in, n_out, n_sc = len(in_specs), len(out_shape), len(scratch_shapes)
    c_in, c_out = len(comm.inputs), len(comm.out_shape)

    def hosted(*refs):
        ins, cins = refs[:n_in], refs[n_in:n_in + c_in]
        p = n_in + c_in
        outs, couts = refs[p:p + n_out], refs[p + n_out:p + n_out + c_out]
        p += n_out + c_out
        scratch, csems = refs[p:p + n_sc], refs[p + n_sc:]
        first = pl.program_id(0) == 0
        half = pl.program_id(0) == grid[0] // 2
        last = pl.program_id(0) == grid[0] - 1
        for d in range(1, len(grid)):
            first = first & (pl.program_id(d) == 0)
            half = half & (pl.program_id(d) == 0)
            last = last & (pl.program_id(d) == grid[d] - 1)

        @pl.when(first)
        def _():
            comm.start(cins, couts, csems)

        if grid[0] >= 2:
            @pl.when(half)
            def _():
                comm.middle(cins, couts, csems)

        body(*ins, *outs, *scratch)

        @pl.when(last)
        def _():
            if grid[0] < 2:
                comm.middle(cins, couts, csems)
            comm.finish(cins, couts, csems)

    any_spec = pl.BlockSpec(memory_space=pl.ANY)
    call = pl.pallas_call(
        hosted, name=name, out_shape=out_shape + comm.out_shape, grid=grid,
        in_specs=in_specs + [any_spec] * c_in, out_specs=out_specs + [any_spec] * c_out,
        scratch_shapes=scratch_shapes + comm.sems,
        compiler_params=_params(*(["arbitrary"] * len(grid))))

    def run(*args):
        res = call(*args, *comm.inputs)
        outs, couts = res[:n_out], list(res[n_out:])
        return (outs[0] if single else outs), couts
    return run


def _sigmoid(x):
    return 1.0 / (1.0 + jnp.exp(-x))


def _log_sigmoid(x):
    return jnp.minimum(x, 0.0) - jnp.log(1.0 + jnp.exp(-jnp.abs(x)))


def _dot(a, b, dims):
    return lax.dot_general(a.astype(BF16), b.astype(BF16), (dims, ((), ())),
                           preferred_element_type=F32)


def _dot3(a, b, dims):
    ah = a.astype(BF16)
    al = (a - ah.astype(F32)).astype(BF16)
    bh = b.astype(BF16)
    bl = (b - bh.astype(F32)).astype(BF16)
    d = (dims, ((), ()))
    out = lax.dot_general(ah, bh, d, preferred_element_type=F32)
    out = out + lax.dot_general(ah, bl, d, preferred_element_type=F32)
    return out + lax.dot_general(al, bh, d, preferred_element_type=F32)


_DIMS = {"nn": ((1,), (0,)), "nt": ((1,), (1,)), "tn": ((0,), (0,))}


def _dot_nn(a, b):
    return _dot3(a, b, _DIMS["nn"])


def _dot_nt(a, b):
    return _dot3(a, b, _DIMS["nt"])


def _dot_tn(a, b):
    return _dot3(a, b, _DIMS["tn"])


def _split3(x):
    hi = x.astype(BF16)
    r1 = x - hi.astype(F32)
    mid = r1.astype(BF16)
    lo = (r1 - mid.astype(F32)).astype(BF16)
    return hi, mid, lo


def _dot01(mask01, x, mode="nn"):
    m = mask01.astype(BF16)
    hi, mid, lo = _split3(x)
    d = (_DIMS[mode], ((), ()))
    out = lax.dot_general(m, hi, d, preferred_element_type=F32)
    out = out + lax.dot_general(m, mid, d, preferred_element_type=F32)
    return out + lax.dot_general(m, lo, d, preferred_element_type=F32)


def _matmul(a, b, *, mode="nn", out_dtype=F32, tm=512, tn=512, tk=None,
            add=None, scale=1.0, add_scale=1.0, name, comm=None, rows=None):
    if mode == "nn":
        (M, K), (K2, N) = a.shape, b.shape
        K2 = rows or K2
    elif mode == "nt":
        (M, K), (N, K2) = a.shape, b.shape
        N = rows or N
    else:
        (K, M), (K2, N) = a.shape, b.shape
    assert K == K2, (a.shape, b.shape, mode)
    tm, tn = min(tm, M), min(tn, N)
    tk = min(tk or K, K)
    assert M % tm == 0 and N % tn == 0 and K % tk == 0, (M, N, K, tm, tn, tk)
    nk = K // tk
    dims = _DIMS[mode]
    has_add = add is not None

    def body(*refs):
        if has_add:
            a_ref, b_ref, add_ref, o_ref = refs[:4]
        else:
            a_ref, b_ref, o_ref = refs[:3]
            add_ref = None
        acc_ref = refs[-1] if nk > 1 else None

        def finish(acc):
            out = acc if scale == 1.0 else acc * scale
            if has_add:
                out = out + add_ref[...].astype(F32) * add_scale
            o_ref[...] = out.astype(o_ref.dtype)

        if nk == 1:
            finish(_dot(a_ref[...], b_ref[...], dims))
        else:
            k = pl.program_id(2)

            @pl.when(k == 0)
            def _():
                acc_ref[...] = jnp.zeros_like(acc_ref)

            acc_ref[...] += _dot(a_ref[...], b_ref[...], dims)

            @pl.when(k == nk - 1)
            def _():
                finish(acc_ref[...])

    if mode == "tn":
        a_spec = pl.BlockSpec((tk, tm), lambda i, j, k: (k, i))
    else:
        a_spec = pl.BlockSpec((tm, tk), lambda i, j, k: (i, k))
    if mode == "nt":
        b_spec = pl.BlockSpec((tn, tk), lambda i, j, k: (j, k))
    else:
        b_spec = pl.BlockSpec((tk, tn), lambda i, j, k: (k, j))
    o_spec = pl.BlockSpec((tm, tn), lambda i, j, k: (i, j))
    in_specs = [a_spec, b_spec] + ([o_spec] if has_add else [])
    args = (a, b) + ((add,) if has_add else ())
    return _pcall(
        body, name=name, comm=comm,
        out_shape=jax.ShapeDtypeStruct((M, N), out_dtype),
        grid=(M // tm, N // tn, nk),
        in_specs=in_specs, out_specs=o_spec,
        scratch_shapes=[pltpu.VMEM((tm, tn), F32)] if nk > 1 else [],
        sem=("parallel", "parallel", "arbitrary"),
    )(*args)


def _ln_stats(x):
    mu = jnp.mean(x, axis=-1, keepdims=True)
    xc = x - mu
    var = jnp.mean(xc * xc, axis=-1, keepdims=True)
    rstd = lax.rsqrt(var + LN_EPS)
    return xc * rstd, rstd


def _ln_fwd(r, g, b, *, tm=256, name):
    T, D = r.shape
    tm = min(tm, T)

    def body(r_ref, g_ref, b_ref, y_ref, yb_ref):
        xhat, _ = _ln_stats(r_ref[...])
        y = xhat * g_ref[...] + b_ref[...]
        y_ref[...] = y
        yb_ref[...] = y.astype(BF16)

    row = pl.BlockSpec((tm, D), lambda i: (i, 0))
    vec = pl.BlockSpec((1, D), lambda i: (0, 0))
    return pl.pallas_call(
        body, name=name,
        out_shape=[jax.ShapeDtypeStruct((T, D), F32), jax.ShapeDtypeStruct((T, D), BF16)],
        grid=(T // tm,), in_specs=[row, vec, vec], out_specs=[row, row],
        compiler_params=_params("parallel"),
    )(r, g, b)


def _cast(x, *, tm=512, name):
    T, D = x.shape
    tm = min(tm, T)

    def body(x_ref, xb_ref):
        xb_ref[...] = x_ref[...].astype(BF16)

    row = pl.BlockSpec((tm, D), lambda i: (i, 0))
    return pl.pallas_call(
        body, name=name, out_shape=jax.ShapeDtypeStruct((T, D), BF16),
        grid=(T // tm,), in_specs=[row], out_specs=row,
        compiler_params=_params("parallel"),
    )(x)


def _ln_bwd(dy, r, g, *, tm=256, name, b=None, target=None):
    T, D = r.shape
    tm = min(tm, T)
    with_loss = target is not None

    def body(*refs):
        if with_loss:
            r_ref, g_ref, b_ref, t_ref, dr_ref, drb_ref, dg_ref, db_ref, loss_ref = refs
        else:
            dy_ref, r_ref, g_ref, dr_ref, drb_ref, dg_ref, db_ref = refs
        i = pl.program_id(0)
        xhat, rstd = _ln_stats(r_ref[...])
        gg = g_ref[...]
        if with_loss:
            err = xhat * gg + b_ref[...] - t_ref[...]
            dyv = err * (1.0 / D)
            part = jnp.sum(jnp.sum(err * err, axis=1, keepdims=True), axis=0, keepdims=True)
            part = jnp.broadcast_to(part * (0.5 / D), (1, LANES))
        else:
            dyv = dy_ref[...]
        dxh = dyv * gg
        m1 = jnp.mean(dxh, axis=-1, keepdims=True)
        m2 = jnp.mean(dxh * xhat, axis=-1, keepdims=True)
        dr = rstd * (dxh - m1 - xhat * m2)
        dr_ref[...] = dr
        drb_ref[...] = dr.astype(BF16)
        dgp = jnp.sum(dyv * xhat, axis=0, keepdims=True)
        dbp = jnp.sum(dyv, axis=0, keepdims=True)

        @pl.when(i == 0)
        def _():
            dg_ref[...] = dgp
            db_ref[...] = dbp
            if with_loss:
                loss_ref[...] = part

        @pl.when(i > 0)
        def _():
            dg_ref[...] += dgp
            db_ref[...] += dbp
            if with_loss:
                loss_ref[...] += part

    row = pl.BlockSpec((tm, D), lambda i: (i, 0))
    vec = pl.BlockSpec((1, D), lambda i: (0, 0))
    out_shape = [jax.ShapeDtypeStruct((T, D), F32), jax.ShapeDtypeStruct((T, D), BF16),
                 jax.ShapeDtypeStruct((1, D), F32), jax.ShapeDtypeStruct((1, D), F32)]
    out_specs = [row, row, vec, vec]
    if with_loss:
        in_specs, args = [row, vec, vec, row], (r, g, b, target)
        out_shape.append(jax.ShapeDtypeStruct((1, LANES), F32))
        out_specs.append(pl.BlockSpec((1, LANES), lambda i: (0, 0)))
    else:
        in_specs, args = [row, row, vec], (dy, r, g)
    return pl.pallas_call(
        body, name=name, out_shape=out_shape, grid=(T // tm,),
        in_specs=in_specs, out_specs=out_specs,
        compiler_params=_params("arbitrary"),
    )(*args)


def _ffn_up(xb, w1t, w3t, *, a=None, tm=1024, tf=512, name, comm=None):
    F, D = w3t.shape
    T = xb.shape[0]
    tm, tf = _pick(T, tm), _pick(F, tf)
    given = a is not None

    def body(x_ref, w1_ref, w3_ref, a_ref, b_ref, h_ref, ht_ref):
        xv = x_ref[...]
        av = w1_ref[...] if given else _dot(xv, w1_ref[...], _DIMS["nt"])
        b = _dot(xv, w3_ref[...], _DIMS["nt"])
        a_ref[...] = av.astype(BF16)
        b_ref[...] = b.astype(BF16)
        h = av * _sigmoid(av) * b
        h_ref[...] = h.astype(BF16)
        ht_ref[...] = h.T.astype(BF16)

    wsp = pl.BlockSpec((tf, D), lambda j, i: (j, 0))
    osp = pl.BlockSpec((tm, tf), lambda j, i: (i, j))
    act = jax.ShapeDtypeStruct((T, F), BF16)
    return _pcall(
        body, name=name, comm=comm, out_shape=[act, act, act, jax.ShapeDtypeStruct((F, T), BF16)],
        grid=(F // tf, T // tm),
        in_specs=[pl.BlockSpec((tm, D), lambda j, i: (i, 0)), osp if given else wsp, wsp],
        out_specs=[osp, osp, osp, pl.BlockSpec((tf, tm), lambda j, i: (j, i))],
        sem=("parallel", "parallel"),
    )(xb, a if given else w1t, w3t)


def _ffn_bwd_act(drb, w2, a, b, *, tm=1024, tf=512, name, comm=None):
    F, D = w2.shape
    T = drb.shape[0]
    tm, tf = _pick(T, tm), _pick(F, tf)

    def body(dr_ref, w2_ref, a_ref, b_ref, da_ref, db_ref, dat_ref, dbt_ref):
        d = 0.5 * _dot(dr_ref[...], w2_ref[...], _DIMS["nt"])
        av = a_ref[...].astype(F32)
        sig = _sigmoid(av)
        da = d * b_ref[...].astype(F32) * sig * (1.0 + av * (1.0 - sig))
        db = d * av * sig
        da_ref[...] = da.astype(BF16)
        db_ref[...] = db.astype(BF16)
        dat_ref[...] = da.T.astype(BF16)
        dbt_ref[...] = db.T.astype(BF16)

    asp = pl.BlockSpec((tm, tf), lambda i, j: (i, j))
    tsp = pl.BlockSpec((tf, tm), lambda i, j: (j, i))
    out, out_t = jax.ShapeDtypeStruct((T, F), BF16), jax.ShapeDtypeStruct((F, T), BF16)
    return _pcall(
        body, name=name, comm=comm, out_shape=[out, out, out_t, out_t], grid=(T // tm, F // tf),
        in_specs=[pl.BlockSpec((tm, D), lambda i, j: (i, 0)),
                  pl.BlockSpec((tf, D), lambda i, j: (j, 0)), asp, asp],
        out_specs=[asp, asp, tsp, tsp],
        sem=("parallel", "parallel"),
    )(drb, w2, a, b)


def _dz_pack(pieces, *, tm=256, name):
    groups = [p if isinstance(p, tuple) else (p,) for p in pieces]
    T = groups[0][0].shape[0]
    widths = [g[0].shape[1] for g in groups]
    W = sum(widths)
    tm = _pick(T, tm)
    flat = [a for g in groups for a in g]
    n = len(flat)

    def body(*refs):
        o_ref, ot_ref = refs[n], refs[n + 1]
        c, p = 0, 0
        for g, w in zip(groups, widths):
            v = refs[p][...].astype(F32)
            for r in refs[p + 1:p + len(g)]:
                v = v + r[...].astype(F32)
            p += len(g)
            o_ref[:, c:c + w] = v.astype(BF16)
            ot_ref[c:c + w, :] = v.T.astype(BF16)
            c += w

    return pl.pallas_call(
        body, name=name,
        out_shape=[jax.ShapeDtypeStruct((T, W), BF16), jax.ShapeDtypeStruct((W, T), BF16)],
        grid=(T // tm,),
        in_specs=[pl.BlockSpec((tm, a.shape[1]), lambda i: (i, 0)) for a in flat],
        out_specs=[pl.BlockSpec((tm, W), lambda i: (i, 0)), pl.BlockSpec((W, tm), lambda i: (0, i))],
        compiler_params=_params("parallel"),
    )(*flat)


def _ffn_dx(da, db, w1t, w3t, dr, *, tm=1024, tn=1024, tk=None, name, comm=None):
    T, F = da.shape
    D = w1t.shape[1]
    tm, tn = _pick(T, tm), _pick(D, tn)
    tk = _pick(F, tk or F // 4)
    nk = F // tk

    def body(da_ref, db_ref, w1_ref, w3_ref, dr_ref, o_ref, acc_ref):
        k = pl.program_id(2)

        @pl.when(k == 0)
        def _():
            acc_ref[...] = jnp.zeros_like(acc_ref)

        acc_ref[...] += (_dot(da_ref[...], w1_ref[...], _DIMS["nn"])
                         + _dot(db_ref[...], w3_ref[...], _DIMS["nn"]))

        @pl.when(k == nk - 1)
        def _():
            o_ref[...] = DN_ALPHA * dr_ref[...] + acc_ref[...]

    asp = pl.BlockSpec((tm, tk), lambda i, j, k: (i, k))
    wsp = pl.BlockSpec((tk, tn), lambda i, j, k: (k, j))
    osp = pl.BlockSpec((tm, tn), lambda i, j, k: (i, j))
    return _pcall(
        body, name=name, comm=comm, out_shape=jax.ShapeDtypeStruct((T, D), F32),
        grid=(T // tm, D // tn, nk), in_specs=[asp, asp, wsp, wsp, osp], out_specs=osp,
        scratch_shapes=[pltpu.VMEM((tm, tn), F32)],
        sem=("parallel", "parallel", "arbitrary"),
    )(da, db, w1t, w3t, dr)


def _chunk_mask(reverse, transpose=False):
    row = lax.broadcasted_iota(jnp.int32, (CHUNK, CHUNK), 0)
    col = lax.broadcasted_iota(jnp.int32, (CHUNK, CHUNK), 1)
    if reverse != transpose:
        return col >= row
    return col <= row


def _run_specs(specs, *, name, comm=None):
    counts = [(len(s["in_specs"]), len(s["out_specs"]), len(s["scratch_shapes"])) for s in specs]

    def body(*refs):
        p, parts = 0, [[], [], []]
        for kind in range(3):
            for c in counts:
                parts[kind].append(refs[p:p + c[kind]])
                p += c[kind]
        live = [s["body"](*parts[0][k], *parts[1][k], *parts[2][k]) for k, s in enumerate(specs)]
        while live:
            for g in list(live):
                if next(g, StopIteration) is StopIteration:
                    live.remove(g)

    cat = lambda key: [v for s in specs for v in s[key]]
    res = _pcall(body, name=name, comm=comm, out_shape=cat("out_shape"), grid=specs[0]["grid"],
                 in_specs=cat("in_specs"), out_specs=cat("out_specs"),
                 scratch_shapes=cat("scratch_shapes"), sem=("arbitrary",))(*cat("args"))
    outs, couts = res if comm is not None else (res, None)
    split, p = [], 0
    for c in counts:
        split.append(outs[p:p + c[1]])
        p += c[1]
    return (split, couts) if comm is not None else split


def _hgrn_pre(hq, hf, lb):
    sig = _sigmoid(hf)
    f = lb + (1.0 - lb) * sig
    q = hq * _sigmoid(hq) * (HGRN_HEAD_DIM ** -0.5)
    return q, f, sig


def _hgrn_decays(f, cmf, reverse):
    bc = _dot01(cmf, jnp.log(f))
    last = 0 if reverse else CHUNK - 1
    blast = bc[last:last + 1, :]
    bref = bc[CHUNK // 2:CHUNK // 2 + 1, :]
    eq = jnp.exp(jnp.minimum(bc - bref, EXP_CLAMP))
    ek = jnp.exp(jnp.minimum(bref - bc, EXP_CLAMP))
    return bc, blast, eq, ek


def _hgrn_fwd(z, lb, *, wh, reverse):
    T = z.shape[0]
    nch = T // CHUNK
    nh = wh // HGRN_HEAD_DIM
    hd = HGRN_HEAD_DIM

    def ci(i):
        return nch - 1 - i if reverse else i

    def body(hq_ref, hi_ref, hf_ref, lb_ref, o_ref, st_ref, s_ref):
        @pl.when(pl.program_id(0) == 0)
        def _():
            s_ref[...] = jnp.zeros_like(s_ref)

        cm = _chunk_mask(reverse)
        cmf = cm.astype(F32)
        H = range(nh)
        sls = [slice(h * hd, (h + 1) * hd) for h in H]
        q, f, _ = _hgrn_pre(hq_ref[...], hf_ref[...], lb_ref[...])
        v = hi_ref[...]
        k = 1.0 - f
        yield
        bc, blast, eq, ek = _hgrn_decays(f, cmf, reverse)
        yield
        qh, kh, qe = q * eq, k * ek, q * jnp.exp(bc)
        k2, eblast = k * jnp.exp(blast - bc), jnp.exp(blast)
        st = [s_ref[h] for h in H]
        yield
        att = [jnp.where(cm, _dot_nt(qh[:, sl], kh[:, sl]), 0.0) for sl in sls]
        yield
        inter = [_dot_nt(qe[:, sls[h]], st[h]) for h in H]
        yield
        out = [inter[h] + _dot_nn(att[h], v[:, sls[h]]) for h in H]
        yield
        snew = [eblast[:, sls[h]] * st[h] + _dot_tn(v[:, sls[h]], k2[:, sls[h]]) for h in H]
        yield
        for h in H:
            st_ref[0, h] = st[h]
            o_ref[:, sls[h]] = out[h]
            s_ref[h] = snew[h]

    blk = lambda c: pl.BlockSpec((CHUNK, wh), lambda i: (ci(i), c))
    return dict(
        body=body, grid=(nch,), args=[z, z, z, lb],
        out_shape=[jax.ShapeDtypeStruct((T, wh), F32),
                   jax.ShapeDtypeStruct((nch, nh, hd, hd), F32)],
        in_specs=[blk(0), blk(1), blk(3 + int(reverse)),
                  pl.BlockSpec((1, wh), lambda i: (0, 0))],
        out_specs=[blk(0), pl.BlockSpec((1, nh, hd, hd), lambda i: (ci(i), 0, 0, 0))],
        scratch_shapes=[pltpu.VMEM((nh, hd, hd), F32)])


def _hgrn_bwd(z, lb, do, states, *, wh, reverse):
    T = z.shape[0]
    nch = T // CHUNK
    nh = wh // HGRN_HEAD_DIM
    hd = HGRN_HEAD_DIM

    def ci(i):
        return i if reverse else nch - 1 - i

    def body(hq_ref, hi_ref, hf_ref, lb_ref, do_ref, st_ref,
             dhq_ref, dhi_ref, dhf_ref, dlb_ref, ds_ref, gs_ref):
        @pl.when(pl.program_id(0) == 0)
        def _():
            ds_ref[...] = jnp.zeros_like(ds_ref)
            gs_ref[...] = jnp.zeros_like(gs_ref)
            dlb_ref[...] = jnp.zeros_like(dlb_ref)

        cm = _chunk_mask(reverse)
        cmf = cm.astype(F32)
        cmtf = _chunk_mask(reverse, transpose=True).astype(F32)
        H = range(nh)
        sls = [slice(h * hd, (h + 1) * hd) for h in H]
        hq, lb = hq_ref[...], lb_ref[...]
        q, f, sig = _hgrn_pre(hq, hf_ref[...], lb)
        v, dov = hi_ref[...], do_ref[...]
        k = 1.0 - f
        yield
        bc, blast, eq, ek = _hgrn_decays(f, cmf, reverse)
        yield
        ebc, eb2, eblast = jnp.exp(bc), jnp.exp(blast - bc), jnp.exp(blast)
        qh, kh, qe, k2 = q * eq, k * ek, q * ebc, k * eb2
        st = [st_ref[0, h] for h in H]
        dst = [ds_ref[h] for h in H]
        yield
        att = [jnp.where(cm, _dot_nt(qh[:, sl], kh[:, sl]), 0.0) for sl in sls]
        yield
        datt = [jnp.where(cm, _dot_nt(dov[:, sl], v[:, sl]), 0.0) for sl in sls]
        yield
        dq_a = [_dot_nn(datt[h], kh[:, sls[h]]) for h in H]
        yield
        dq_s = [_dot_nn(dov[:, sls[h]], st[h]) for h in H]
        yield
        dk_a = [_dot_tn(datt[h], qh[:, sls[h]]) for h in H]
        yield
        dk_s = [_dot_nn(v[:, sls[h]], dst[h]) for h in H]
        yield
        dv = [_dot(att[h], dov[:, sls[h]], _DIMS["tn"]) + _dot(k2[:, sls[h]], dst[h], _DIMS["nt"])
              for h in H]
        yield
        dsn = [eblast[:, sls[h]] * dst[h] + _dot_tn(dov[:, sls[h]], qe[:, sls[h]]) for h in H]
        for h in H:
            ds_ref[h] = dsn[h]
        yield
        dq = jnp.concatenate(dq_a, axis=1) * eq + ebc * jnp.concatenate(dq_s, axis=1)
        dk = jnp.concatenate(dk_a, axis=1) * ek + eb2 * jnp.concatenate(dk_s, axis=1)
        db = q * dq - k * dk
        yield
        dg = _dot01(cmtf, db) + gs_ref[...]
        gs_ref[...] += jnp.sum(db, axis=0, keepdims=True)
        yield
        df = dg / f - dk
        dhf_ref[...] = (df * (1.0 - lb) * sig * (1.0 - sig)).astype(BF16)
        dlb_ref[...] += jnp.sum(df * (1.0 - sig), axis=0, keepdims=True)
        sq = _sigmoid(hq)
        dhq_ref[...] = (dq * (HGRN_HEAD_DIM ** -0.5) * sq * (1.0 + hq * (1.0 - sq))).astype(BF16)
        dhi_ref[...] = jnp.concatenate(dv, axis=1).astype(BF16)

    blk = lambda c: pl.BlockSpec((CHUNK, wh), lambda i: (ci(i), c))
    vec = pl.BlockSpec((1, wh), lambda i: (0, 0))
    big = jax.ShapeDtypeStruct((T, wh), BF16)
    return dict(
        body=body, grid=(nch,), args=[z, z, z, lb, do, states],
        in_specs=[blk(0), blk(1), blk(3 + int(reverse)), vec, blk(0),
                  pl.BlockSpec((1, nh, hd, hd), lambda i: (ci(i), 0, 0, 0))],
        out_shape=[big, big, big, jax.ShapeDtypeStruct((1, wh), F32)],
        out_specs=[blk(0), blk(0), blk(0), vec],
        scratch_shapes=[pltpu.VMEM((nh, hd, hd), F32), pltpu.VMEM((1, wh), F32)])


def _mlstm_intra(cm, cmt, ig_row, ig_col, xf_row, xf_col, m_st):
    lf_row, lf_col = _log_sigmoid(xf_row), _log_sigmoid(xf_col)
    bcol = jnp.sum(jnp.where(cm, lf_row, 0.0), axis=1, keepdims=True)
    brow = jnp.sum(jnp.where(cmt, lf_col, 0.0), axis=0, keepdims=True)
    blast = jnp.sum(lf_row, axis=1, keepdims=True)
    dmat = jnp.where(cm, bcol - brow + ig_row, NEG)
    m_inter = bcol + m_st
    m_t = jnp.maximum(m_inter, jnp.max(dmat, axis=1, keepdims=True))
    p = jnp.exp(dmat - m_t)
    inter = jnp.exp(m_inter - m_t)
    w_col = blast - bcol + ig_col
    m_new = jnp.maximum(blast + m_st, jnp.max(w_col, axis=0, keepdims=True))
    cs = jnp.exp(blast + m_st - m_new)
    kscale = jnp.exp(w_col - m_new)
    return p, inter, m_t, m_new, cs, kscale


def _mlstm_specs(T, wm, nhm, reverse, backward):
    nch = T // CHUNK
    dm = wm // nhm
    ng = 4 * nhm

    def ci(i):
        fwd_order = nch - 1 - i if reverse else i
        return nch - 1 - fwd_order if backward else fwd_order

    row = lambda w, c: pl.BlockSpec((CHUNK, w), lambda i: (ci(i), c))
    gates_row = pl.BlockSpec((1, ng, CHUNK), lambda i: (ci(i), 0, 0))
    bias_row = pl.BlockSpec((1, LANES), lambda i: (0, 0))
    bias_col = pl.BlockSpec((ng, 1), lambda i: (0, 0))
    st_c = pl.BlockSpec((1, nhm, dm, dm), lambda i: (ci(i), 0, 0, 0))
    st_n = pl.BlockSpec((1, nhm, 1, dm), lambda i: (ci(i), 0, 0, 0))
    st_m = pl.BlockSpec((1, nhm, 1, LANES), lambda i: (ci(i), 0, 0, 0))
    return nch, dm, ng, ci, row, gates_row, bias_row, bias_col, st_c, st_n, st_m


def _mlstm_fwd(qk, z, gc, gr, br, bcl, *, wm, nhm, vcol, reverse):
    T = qk.shape[0]
    nch, dm, ng, ci, row, gates_row, bias_row, bias_col, st_c, st_n, st_m = _mlstm_specs(
        T, wm, nhm, reverse, False)
    d = int(reverse)

    def body(q_ref, k_ref, v_ref, gc_ref, gr_ref, br_ref, bc_ref,
             h_ref, cst_ref, nst_ref, mst_ref, c_ref, n_ref, m_ref):
        @pl.when(pl.program_id(0) == 0)
        def _():
            c_ref[...] = jnp.zeros_like(c_ref)
            n_ref[...] = jnp.zeros_like(n_ref)
            m_ref[...] = jnp.full_like(m_ref, M_INIT)

        cm = _chunk_mask(reverse)
        cmt = _chunk_mask(reverse, transpose=True)
        G = gc_ref[...] + br_ref[...]
        Gr = gr_ref[0] + bc_ref[...]
        H = range(nhm)
        sls = [slice(h * dm, (h + 1) * dm) for h in H]
        m_all = [m_ref[h] for h in H]
        intra = [_mlstm_intra(cm, cmt, Gr[d * nhm + h:d * nhm + h + 1, :], G[:, d * nhm + h:d * nhm + h + 1],
                              Gr[2 * nhm + d * nhm + h:2 * nhm + d * nhm + h + 1, :],
                              G[:, 2 * nhm + d * nhm + h:2 * nhm + d * nhm + h + 1], m_all[h][:, 0:1])
                 for h in H]
        p, inter, m_t, m_new, cs, kscale = zip(*intra)
        yield
        q = [q_ref[:, sl] * (dm ** -0.5) for sl in sls]
        k = [k_ref[:, sl] for sl in sls]
        v = [v_ref[:, sl] for sl in sls]
        ct = [c_ref[h] for h in H]
        n = [n_ref[h] for h in H]
        yield
        sc = [_dot_nt(q[h], k[h]) * p[h] for h in H]
        yield
        qc = [_dot_nt(q[h], ct[h]) for h in H]
        yield
        num = [_dot_nn(sc[h], v[h]) + inter[h] * qc[h] for h in H]
        yield
        den = [jnp.sum(sc[h], axis=1, keepdims=True) + inter[h] * jnp.sum(q[h] * n[h], axis=1, keepdims=True)
               for h in H]
        kw = [k[h] * kscale[h] for h in H]
        yield
        cnew = [cs[h] * ct[h] + _dot_tn(v[h], kw[h]) for h in H]
        yield
        for h in H:
            cst_ref[0, h] = ct[h]
            nst_ref[0, h] = n[h]
            mst_ref[0, h] = m_all[h]
            h_ref[:, sls[h]] = num[h] / jnp.maximum(jnp.abs(den[h]), jnp.exp(-m_t[h]))
            c_ref[h] = cnew[h]
            n_ref[h] = cs[h] * n[h] + jnp.sum(kw[h], axis=0, keepdims=True)
            m_ref[h] = jnp.broadcast_to(m_new[h], (1, LANES))

    return dict(
        body=body, grid=(nch,), args=[qk, qk, z, gc, gr, br, bcl],
        out_shape=[jax.ShapeDtypeStruct((T, wm), F32),
                   jax.ShapeDtypeStruct((nch, nhm, dm, dm), F32),
                   jax.ShapeDtypeStruct((nch, nhm, 1, dm), F32),
                   jax.ShapeDtypeStruct((nch, nhm, 1, LANES), F32)],
        in_specs=[row(wm, 0), row(wm, 1), row(wm, vcol), row(LANES, 0), gates_row, bias_row, bias_col],
        out_specs=[row(wm, 0), st_c, st_n, st_m],
        scratch_shapes=[pltpu.VMEM((nhm, dm, dm), F32), pltpu.VMEM((nhm, 1, dm), F32),
                        pltpu.VMEM((nhm, 1, LANES), F32)])


def _mlstm_bwd(qk, z, gc, gr, br, bcl, dh, states, *, wm, nhm, vcol, reverse):
    T = qk.shape[0]
    nch, dm, ng, ci, row, gates_row, bias_row, bias_col, st_c, st_n, st_m = _mlstm_specs(
        T, wm, nhm, reverse, True)
    d = int(reverse)

    def body(q_ref, k_ref, v_ref, gc_ref, gr_ref, br_ref, bc_ref, dh_ref, cst_ref, nst_ref, mst_ref,
             dqk_ref, dv_ref, dgr_ref, dgs_ref, e_ref, en_ref, fs_ref):
        @pl.when(pl.program_id(0) == 0)
        def _():
            e_ref[...] = jnp.zeros_like(e_ref)
            en_ref[...] = jnp.zeros_like(en_ref)
            fs_ref[...] = jnp.zeros_like(fs_ref)
            dgs_ref[...] = jnp.zeros_like(dgs_ref)

        cm = _chunk_mask(reverse)
        cmt = _chunk_mask(reverse, transpose=True)
        row_i = lax.broadcasted_iota(jnp.int32, (CHUNK, CHUNK), 0)
        col_i = lax.broadcasted_iota(jnp.int32, (CHUNK, CHUNK), 1)
        eye = row_i == col_i
        G = gc_ref[...] + br_ref[...]
        Gr = gr_ref[0] + bc_ref[...]
        H = range(nhm)
        sls = [slice(h * dm, (h + 1) * dm) for h in H]
        xf_row = [Gr[2 * nhm + d * nhm + h:2 * nhm + d * nhm + h + 1, :] for h in H]
        intra = [_mlstm_intra(cm, cmt, Gr[d * nhm + h:d * nhm + h + 1, :], G[:, d * nhm + h:d * nhm + h + 1],
                              xf_row[h], G[:, 2 * nhm + d * nhm + h:2 * nhm + d * nhm + h + 1],
                              mst_ref[0, h][:, 0:1]) for h in H]
        p, inter, m_t, _, cs, kscale = zip(*intra)
        yield
        q = [q_ref[:, sls[h]] * (dm ** -0.5) for h in H]
        k = [k_ref[:, sls[h]] for h in H]
        v = [v_ref[:, sls[h]] for h in H]
        dhv = [dh_ref[:, sls[h]] for h in H]
        ct = [cst_ref[0, h] for h in H]
        n = [nst_ref[0, h] for h in H]
        et = [e_ref[h] for h in H]
        en = [en_ref[h] for h in H]
        carry = [fs_ref[h][:, 0:1] for h in H]
        yield
        sc = [_dot_nt(q[h], k[h]) * p[h] for h in H]
        yield
        qc = [_dot_nt(q[h], ct[h]) for h in H]
        yield
        num = [_dot_nn(sc[h], v[h]) + inter[h] * qc[h] for h in H]
        yield
        den = [jnp.sum(sc[h], axis=1, keepdims=True) + inter[h] * jnp.sum(q[h] * n[h], axis=1, keepdims=True)
               for h in H]
        floor = [jnp.exp(-m_t[h]) for h in H]
        nstab = [jnp.maximum(jnp.abs(den[h]), floor[h]) for h in H]
        gh = [dhv[h] / nstab[h] for h in H]
        dd = [-jnp.sum(dhv[h] * (num[h] / nstab[h]), axis=1, keepdims=True) / nstab[h]
              * jnp.where(jnp.abs(den[h]) > floor[h], jnp.sign(den[h]), 0.0) for h in H]
        yield
        dsqk = [(_dot_nt(gh[h], v[h]) + dd[h]) * p[h] for h in H]
        qi = [q[h] * inter[h] for h in H]
        kw = [k[h] * kscale[h] for h in H]
        yield
        ghc = [_dot_nn(gh[h], ct[h]) for h in H]
        yield
        vet = [_dot_nn(v[h], et[h]) for h in H]
        yield
        dq = [_dot_nn(dsqk[h], k[h]) + inter[h] * (ghc[h] + dd[h] * n[h]) for h in H]
        yield
        dk = [_dot_tn(dsqk[h], q[h]) + kscale[h] * (vet[h] + en[h]) for h in H]
        yield
        dv = [_dot(sc[h], gh[h], _DIMS["tn"]) + _dot(kw[h], et[h], _DIMS["nt"]) for h in H]
        yield
        e_new = [cs[h] * et[h] + _dot_tn(gh[h], qi[h]) for h in H]
        en_new = [cs[h] * en[h] + jnp.sum(qi[h] * dd[h], axis=0, keepdims=True) for h in H]
        yield
        di_col = [jnp.sum(k[h] * dk[h], axis=1, keepdims=True) for h in H]
        df_col = [jnp.sum(q[h] * dq[h], axis=1, keepdims=True) - di_col[h] for h in H]
        di_row = [jnp.sum(jnp.where(eye, di_col[h], 0.0), axis=0, keepdims=True) for h in H]
        dlf_row = [jnp.sum(jnp.where(cm, df_col[h], 0.0), axis=0, keepdims=True) + carry[h] for h in H]
        dxf_row = [dlf_row[h] * (1.0 - _sigmoid(xf_row[h])) for h in H]
        yield
        for h in H:
            sl = sls[h]
            slk = slice(wm + h * dm, wm + (h + 1) * dm)
            e_ref[h] = e_new[h]
            en_ref[h] = en_new[h]
            fs_ref[h] = jnp.broadcast_to(carry[h] + jnp.sum(df_col[h], axis=0, keepdims=True), (1, LANES))
            dgr_ref[0, h:h + 1, :] = di_row[h]
            dgr_ref[0, nhm + h:nhm + h + 1, :] = dxf_row[h]
            dgs_ref[h:h + 1, :] += jnp.broadcast_to(jnp.sum(di_row[h], axis=1, keepdims=True), (1, LANES))
            dgs_ref[nhm + h:nhm + h + 1, :] += jnp.broadcast_to(
                jnp.sum(dxf_row[h], axis=1, keepdims=True), (1, LANES))
            dqk_ref[:, sl] = (dq[h] * (dm ** -0.5)).astype(BF16)
            dqk_ref[:, slk] = dk[h].astype(BF16)
            dv_ref[:, sl] = dv[h].astype(BF16)

    return dict(
        body=body, grid=(nch,), args=[qk, qk, z, gc, gr, br, bcl, dh] + list(states),
        in_specs=[row(wm, 0), row(wm, 1), row(wm, vcol), row(LANES, 0), gates_row, bias_row, bias_col,
                  row(wm, 0), st_c, st_n, st_m],
        out_shape=[jax.ShapeDtypeStruct((T, 2 * wm), BF16), jax.ShapeDtypeStruct((T, wm), BF16),
                   jax.ShapeDtypeStruct((nch, 2 * nhm, CHUNK), F32),
                   jax.ShapeDtypeStruct((2 * nhm, LANES), F32)],
        out_specs=[row(2 * wm, 0), row(wm, 0),
                   pl.BlockSpec((1, 2 * nhm, CHUNK), lambda i: (ci(i), 0, 0)),
                   pl.BlockSpec((2 * nhm, LANES), lambda i: (0, 0))],
        scratch_shapes=[pltpu.VMEM((nhm, dm, dm), F32), pltpu.VMEM((nhm, 1, dm), F32),
                        pltpu.VMEM((nhm, 1, LANES), F32)])


def _conv_taps(x, w_ref):
    T = x.shape[0]
    t = lax.broadcasted_iota(jnp.int32, x.shape, 0)
    taps = []
    acc = None
    for j in range(CONV_WIDTH):
        s = CONV_WIDTH // 2 - j
        if s == 0:
            xs = x
        else:
            xs = jnp.where((t - s >= 0) & (t - s < T), pltpu.roll(x, s % T, 0), 0.0)
        taps.append(xs)
        term = w_ref[j:j + 1, :] * xs
        acc = term if acc is None else acc + term
    return taps, acc


def _conv_fwd(z, w, b, *, col0, tc=LANES, name):
    T = z.shape[0]
    C2 = w.shape[1]
    assert col0 % tc == 0 and C2 % tc == 0

    def body(z_ref, w_ref, b_ref, o_ref):
        _, acc = _conv_taps(z_ref[...], w_ref)
        c = acc + b_ref[...]
        o_ref[...] = c * _sigmoid(c)

    return pl.pallas_call(
        body, name=name, out_shape=jax.ShapeDtypeStruct((T, C2), F32),
        grid=(C2 // tc,),
        in_specs=[pl.BlockSpec((T, tc), lambda j: (0, col0 // tc + j)),
                  pl.BlockSpec((CONV_WIDTH, tc), lambda j: (0, j)),
                  pl.BlockSpec((1, tc), lambda j: (0, j))],
        out_specs=pl.BlockSpec((T, tc), lambda j: (0, j)),
        compiler_params=_params("parallel"),
    )(z, w, b)


def _conv_bwd(dy_fw, dy_bw, z, w, b, *, col0, tc=LANES, name):
    T = z.shape[0]
    C2 = w.shape[1]

    def body(dyf_ref, dyb_ref, z_ref, w_ref, b_ref, dx_ref, dw_ref, db_ref):
        taps, acc = _conv_taps(z_ref[...], w_ref)
        c = acc + b_ref[...]
        sg = _sigmoid(c)
        dc = (dyf_ref[...].astype(F32) + dyb_ref[...].astype(F32)) * sg * (1.0 + c * (1.0 - sg))
        t = lax.broadcasted_iota(jnp.int32, dc.shape, 0)
        dx = None
        for j in range(CONV_WIDTH):
            s = j - CONV_WIDTH // 2
            if s == 0:
                ds = dc
            else:
                ds = jnp.where((t - s >= 0) & (t - s < T), pltpu.roll(dc, s % T, 0), 0.0)
            term = w_ref[j:j + 1, :] * ds
            dx = term if dx is None else dx + term
            dw_ref[j:j + 1, :] = jnp.sum(dc * taps[j], axis=0, keepdims=True)
        dx_ref[...] = dx.astype(BF16)
        db_ref[...] = jnp.sum(dc, axis=0, keepdims=True)

    col = pl.BlockSpec((T, tc), lambda j: (0, j))
    wsp = pl.BlockSpec((CONV_WIDTH, tc), lambda j: (0, j))
    bsp = pl.BlockSpec((1, tc), lambda j: (0, j))
    return pl.pallas_call(
        body, name=name,
        out_shape=[jax.ShapeDtypeStruct((T, C2), BF16), jax.ShapeDtypeStruct((CONV_WIDTH, C2), F32),
                   jax.ShapeDtypeStruct((1, C2), F32)],
        grid=(C2 // tc,),
        in_specs=[col, col, pl.BlockSpec((T, tc), lambda j: (0, col0 // tc + j)), wsp, bsp],
        out_specs=[col, wsp, bsp],
        compiler_params=_params("parallel"),
    )(dy_fw, dy_bw, z, w, b)


def _mix_heads(o_fw, o_bw, h_fw, h_bw, wh, wm, nhm):
    out = []
    hd = HGRN_HEAD_DIM
    for h in range(wh // hd):
        sl = slice(h * hd, (h + 1) * hd)
        o = o_fw[:, sl] + o_bw[:, sl]
        r = lax.rsqrt(jnp.mean(o * o, axis=-1, keepdims=True) + NORM_EPS)
        out.append((0, sl, o * r, r))
    dm = wm // nhm
    for h in range(nhm):
        sl = slice(h * dm, (h + 1) * dm)
        x = h_fw[:, sl] + h_bw[:, sl]
        xc = x - jnp.mean(x, axis=-1, keepdims=True)
        r = lax.rsqrt(jnp.mean(xc * xc, axis=-1, keepdims=True) + NORM_EPS)
        out.append((1, sl, xc * r, r))
    return out


def _mix_specs(T, wh, wm, tm, gcol, ocol):
    rowh = pl.BlockSpec((tm, wh), lambda i: (i, 0))
    rowm = pl.BlockSpec((tm, wm), lambda i: (i, 0))
    hg = pl.BlockSpec((tm, wh), lambda i: (i, gcol))
    mo = pl.BlockSpec((tm, wm), lambda i: (i, ocol))
    vh = pl.BlockSpec((1, wh), lambda i: (0, 0))
    vm = pl.BlockSpec((1, wm), lambda i: (0, 0))
    return rowh, rowm, hg, mo, vh, vm


def _mix_fwd(o_fw, o_bw, h_fw, h_bw, z, gh, gm, *, nhm, gcol, ocol, tm=256, name):
    T, wh = o_fw.shape
    wm = h_fw.shape[1]
    tm = min(tm, T)
    rowh, rowm, hg, mo, vh, vm = _mix_specs(T, wh, wm, tm, gcol, ocol)

    def body(of_ref, ob_ref, hf_ref, hb_ref, hg_ref, mo_ref, gh_ref, gm_ref, y_ref, yt_ref):
        heads = _mix_heads(of_ref[...], ob_ref[...], hf_ref[...], hb_ref[...], wh, wm, nhm)
        for grp, sl, nv, _ in heads:
            if grp == 0:
                gate = hg_ref[:, sl]
                gate = gate * _sigmoid(gate)
                y = nv * gh_ref[:, sl] * gate
                osl = sl
            else:
                y = nv * gm_ref[:, sl] * _sigmoid(mo_ref[:, sl])
                osl = slice(wh + sl.start, wh + sl.stop)
            y_ref[:, osl] = y.astype(BF16)
            yt_ref[osl, :] = y.T.astype(BF16)

    return pl.pallas_call(
        body, name=name,
        out_shape=[jax.ShapeDtypeStruct((T, wh + wm), BF16), jax.ShapeDtypeStruct((wh + wm, T), BF16)],
        grid=(T // tm,),
        in_specs=[rowh, rowh, rowm, rowm, hg, mo, vh, vm],
        out_specs=[pl.BlockSpec((tm, wh + wm), lambda i: (i, 0)),
                   pl.BlockSpec((wh + wm, tm), lambda i: (0, i))],
        compiler_params=_params("parallel"),
    )(o_fw, o_bw, h_fw, h_bw, z, z, gh, gm)


def _mix_bwd(dy, o_fw, o_bw, h_fw, h_bw, z, gh, gm, *, nhm, gcol, ocol, tm=256, name, comm=None):
    T, wh = o_fw.shape
    wm = h_fw.shape[1]
    tm = min(tm, T)
    rowh, rowm, hg, mo, vh, vm = _mix_specs(T, wh, wm, tm, gcol, ocol)

    def body(dy_ref, of_ref, ob_ref, hf_ref, hb_ref, hg_ref, mo_ref, gh_ref, gm_ref,
             do_ref, dh_ref, dhg_ref, dmo_ref, dgh_ref, dgm_ref):
        @pl.when(pl.program_id(0) == 0)
        def _():
            dgh_ref[...] = jnp.zeros_like(dgh_ref)
            dgm_ref[...] = jnp.zeros_like(dgm_ref)

        heads = _mix_heads(of_ref[...], ob_ref[...], hf_ref[...], hb_ref[...], wh, wm, nhm)
        for grp, sl, nv, r in heads:
            if grp == 0:
                d = dy_ref[:, sl]
                x = hg_ref[:, sl]
                sg = _sigmoid(x)
                gate = x * sg
                g = gh_ref[:, sl]
                dgh_ref[:, sl] += jnp.sum(d * nv * gate, axis=0, keepdims=True)
                dhg_ref[:, sl] = (d * nv * g * sg * (1.0 + x * (1.0 - sg))).astype(BF16)
                t = d * g * gate
                do_ref[:, sl] = r * (t - nv * jnp.mean(t * nv, axis=-1, keepdims=True))
            else:
                d = dy_ref[:, slice(wh + sl.start, wh + sl.stop)]
                sg = _sigmoid(mo_ref[:, sl])
                g = gm_ref[:, sl]
                dgm_ref[:, sl] += jnp.sum(d * nv * sg, axis=0, keepdims=True)
                dmo_ref[:, sl] = (d * nv * g * sg * (1.0 - sg)).astype(BF16)
                t = d * g * sg
                dh_ref[:, sl] = r * (t - jnp.mean(t, axis=-1, keepdims=True)
                                     - nv * jnp.mean(t * nv, axis=-1, keepdims=True))

    return _pcall(
        body, name=name, comm=comm,
        out_shape=[jax.ShapeDtypeStruct((T, wh), F32), jax.ShapeDtypeStruct((T, wm), F32),
                   jax.ShapeDtypeStruct((T, wh), BF16), jax.ShapeDtypeStruct((T, wm), BF16),
                   jax.ShapeDtypeStruct((1, wh), F32), jax.ShapeDtypeStruct((1, wm), F32)],
        grid=(T // tm,),
        in_specs=[pl.BlockSpec((tm, wh + wm), lambda i: (i, 0)), rowh, rowh, rowm, rowm, hg, mo, vh, vm],
        out_specs=[rowh, rowm, rowh, rowm, vh, vm],
        sem=("arbitrary",),
    )(dy, o_fw, o_bw, h_fw, h_bw, z, z, gh, gm)


def _pick(n, pref):
    for c in range(pref - pref % LANES, 0, -LANES):
        if n % c == 0:
            return c
    return n


def _mm(a, b, mode="nn", *, tm=1024, tn=1024, tk=2048, **kw):
    if mode == "nn":
        (M, K), N = a.shape, b.shape[1]
    else:
        (M, K), N = a.shape, kw.get("rows") or b.shape[0]
    return _matmul(a, b, mode=mode, tm=_pick(M, tm), tn=_pick(N, tn), tk=_pick(K, tk), **kw)


def _device_step(x, target, sh, small_parts, small_params, core):
    T, D = x.shape

    def full(g):
        return g.reshape(-1, D)

    def shards(g):
        return g.reshape(N_DEV, -1, D)

    def pair_add(name, g, land):
        return _pair_add(g, land, core, "rs_add_" + name)

    def dw(lhs_t, rhs, name, scale=1.0):
        return shards(_mm(lhs_t, rhs, out_dtype=BF16, scale=scale, tm=704, tk=T, name=name))

    g_w1a, g_small = _comm_call(_ag_job([sh["ffn1_w1"], small_parts]), "ag_ffn1_w1")
    w1a = full(g_w1a)
    P = small_params(g_small)
    wh = P["hgrn_norm_g"].shape[1]
    wm = P["mlstm_norm_g"].shape[1]
    nhm = P["ig_b"].shape[1]
    ng = 4 * nhm
    nch = T // CHUNK
    assert wh == wm and T % CHUNK == 0
    vcol, ocol, gcol = 7, 8, 2
    col0 = 5 * wh
    nz = 5 * wh + 4 * wm
    xb = _cast(x, name="x_cast")
    a1f, (g_w3a,) = _mm(xb, w1a, "nt", name="ffn1_up_a", comm=_ag_job([sh["ffn1_w3"]]))
    w3a = full(g_w3a)
    cut = sh["w_in"].shape[0] // 2 // 16 * 16
    (a1, b1, h1, h1t), (g_w2a, g_wtop) = _ffn_up(xb, None, w3a, a=a1f, name="ffn1_up_b",
                                                 comm=_ag_job([sh["ffn1_w2"], sh["w_in"][:cut]]))
    w2a = full(g_w2a)
    r1, (g_wbot,) = _mm(h1, w2a, add=x, scale=0.5, add_scale=DN_ALPHA, tk=w2a.shape[0] // 4,
                        name="ffn1_down", comm=_ag_job([sh["w_in"][cut:]]))
    w_int = full(jnp.concatenate([g_wtop, g_wbot], axis=1))
    w_wgt = jnp.zeros((LANES, D), BF16).at[:ng].set(w_int[nz:])
    x1, x1b = _ln_fwd(r1, P["ln1_g"], P["ln1_b"], name="ln1")
    zm, (g_wout,) = _mm(x1b, w_int, "nt", rows=nz, name="zm", comm=_ag_job([sh["w_out"]]))
    w_wout = full(g_wout)
    zg = _mm(x1b, w_wgt, "nt", name="zg")
    gr = zg[:, :ng].reshape(nch, CHUNK, ng).transpose(0, 2, 1)
    bias = jnp.concatenate([P["ig_b"].reshape(-1), P["fg_b"].reshape(-1)])
    br = jnp.zeros((1, LANES), F32).at[0, :ng].set(bias)
    bcl = bias.reshape(ng, 1)
    lbf, lbb = P["lb"][0:1], P["lb"][1:2]
    ((o_fw, s_fw), (o_bw, s_bw)), (g_w1b,) = _run_specs(
        [_hgrn_fwd(zm, lbf, wh=wh, reverse=False), _hgrn_fwd(zm, lbb, wh=wh, reverse=True)],
        name="hgrn_fwd", comm=_ag_job([sh["ffn2_w1"]]))
    qk = _conv_fwd(zm, P["conv_w"], P["conv_b"], col0=col0, name="conv")
    mkw = dict(wm=wm, nhm=nhm, vcol=vcol)
    ((h_fw, *st_fw),), (g_w3b,) = _run_specs([_mlstm_fwd(qk, zm, zg, gr, br, bcl, reverse=False, **mkw)],
                                             name="mlstm_fw", comm=_ag_job([sh["ffn2_w3"]]))
    ((h_bw, *st_bw),) = _run_specs([_mlstm_fwd(qk, zm, zg, gr, br, bcl, reverse=True, **mkw)],
                                   name="mlstm_bw")
    w1b, w3b = full(g_w1b), full(g_w3b)
    y, yt = _mix_fwd(o_fw, o_bw, h_fw, h_bw, zm, P["hgrn_norm_g"], P["mlstm_norm_g"],
                     nhm=nhm, gcol=gcol, ocol=ocol, name="mix")
    r2 = _mm(y, w_wout, add=x1, add_scale=DN_ALPHA, name="r2")
    x2, x2b = _ln_fwd(r2, P["ln2_g"], P["ln2_b"], name="ln2")
    (a2, b2, h2, h2t), (g_w2b,) = _ffn_up(x2b, w1b, w3b, name="ffn2_up", comm=_ag_job([sh["ffn2_w2"]]))
    w2b = full(g_w2b)
    r3 = _mm(h2, w2b, add=x2, scale=0.5, add_scale=DN_ALPHA, tk=w2b.shape[0] // 4, name="ffn2_down")

    G, land = {}, {}
    dr3, dr3b, G["ln3_g"], G["ln3_b"], loss = _ln_bwd(
        None, r3, P["ln3_g"], b=P["ln3_b"], target=target, name="ln3_bwd")
    da2, db2, da2t, db2t = _ffn_bwd_act(dr3b, w2b, a2, b2, name="ffn2_dact")
    gw1b, gw3b = dw(da2t, x2b, "ffn2_dw1"), dw(db2t, x2b, "ffn2_dw3")
    gw2b = dw(h2t, dr3b, "ffn2_dw2", scale=0.5)
    dx2, l1 = _ffn_dx(da2, db2, w1b, w3b, dr3, name="ffn2_dx", comm=_pair_job([gw1b, gw3b, gw2b]))
    p_w1b, p_w3b, p_w2b = [pair_add(k, g, l) for k, g, l in
                           zip(("ffn2_w1", "ffn2_w3", "ffn2_w2"), (gw1b, gw3b, gw2b), l1)]
    dr2, dr2b, G["ln2_g"], G["ln2_b"] = _ln_bwd(dx2, r2, P["ln2_g"], name="ln2_bwd")
    dy = _mm(dr2b, w_wout, "nt", name="dy")
    gwout = shards(_mm(yt, dr2b, out_dtype=BF16, name="dwout"))
    (do, dh, dhg, dmo, G["hgrn_norm_g"], G["mlstm_norm_g"]), (l1,) = _mix_bwd(
        dy, o_fw, o_bw, h_fw, h_bw, zm, P["hgrn_norm_g"], P["mlstm_norm_g"],
        nhm=nhm, gcol=gcol, ocol=ocol, name="mix_bwd", comm=_pair_job([gwout]))
    p_wout = pair_add("w_out", gwout, l1)
    ((dqk_f, dv_f, dgr_f, dgs_f),), (land["ffn2_w1"],) = _run_specs(
        [_mlstm_bwd(qk, zm, zg, gr, br, bcl, dh, st_fw, reverse=False, **mkw)],
        name="mlstm_fw_bwd", comm=_chips_job([p_w1b]))
    ((dqk_b, dv_b, dgr_b, dgs_b),), (land["ffn2_w3"],) = _run_specs(
        [_mlstm_bwd(qk, zm, zg, gr, br, bcl, dh, st_bw, reverse=True, **mkw)],
        name="mlstm_bw_bwd", comm=_chips_job([p_w3b]))
    dmqk, G["conv_w"], G["conv_b"] = _conv_bwd(dqk_f, dqk_b, zm, P["conv_w"], P["conv_b"], col0=col0,
                                               name="conv_bwd")
    ((dhq_f, dhi_f, dhf_fw, dlb_f), (dhq_b, dhi_b, dhf_bw, dlb_b)), (land["ffn2_w2"], land["w_out"]) = \
        _run_specs([_hgrn_bwd(zm, lbf, do, s_fw, wh=wh, reverse=False),
                    _hgrn_bwd(zm, lbb, do, s_bw, wh=wh, reverse=True)],
                   name="hgrn_bwd", comm=_chips_job([p_w2b, p_wout]))
    G["lb"] = jnp.concatenate([dlb_f, dlb_b], axis=0)
    G["ig_b"] = jnp.stack([dgs_f[:nhm, 0], dgs_b[:nhm, 0]])
    G["fg_b"] = jnp.stack([dgs_f[nhm:, 0], dgs_b[nhm:, 0]])
    dzm, dzmt = _dz_pack([(dhq_f, dhq_b), (dhi_f, dhi_b), dhg, dhf_fw, dhf_bw, dmqk, (dv_f, dv_b), dmo],
                         name="dz_pack")
    dgr = jnp.concatenate([dgr_f[:, :nhm], dgr_b[:, :nhm], dgr_f[:, nhm:], dgr_b[:, nhm:]], axis=1)
    dzgt = jnp.zeros((LANES, T), F32).at[:ng].set(dgr.transpose(1, 0, 2).reshape(ng, T)).astype(BF16)
    dzg = jnp.zeros((T, LANES), F32).at[:, :ng].set(dgr.transpose(0, 2, 1).reshape(T, ng)).astype(BF16)
    gwin = shards(jnp.concatenate([_mm(dzmt, x1b, out_dtype=BF16, tk=T, name="dwin"),
                                   _mm(dzgt, x1b, out_dtype=BF16, tk=T, name="dwg")[:ng]], axis=0))
    t, (l1,) = _mm(dzg, w_wgt, add=dr2, add_scale=DN_ALPHA, name="dx1_g", comm=_pair_job([gwin]))
    p_win = pair_add("w_in", gwin, l1)
    dx1, (land["w_in"],) = _mm(dzm, w_int, rows=nz, add=t, tk=1536, name="dx1", comm=_chips_job([p_win]))
    dr1, dr1b, G["ln1_g"], G["ln1_b"] = _ln_bwd(dx1, r1, P["ln1_g"], name="ln1_bwd")
    gw2a = dw(h1t, dr1b, "ffn1_dw2", scale=0.5)
    (da1, db1, da1t, db1t), (l1,) = _ffn_bwd_act(dr1b, w2a, a1, b1, name="ffn1_dact",
                                                 comm=_pair_job([gw2a]))
    p_w2a = pair_add("ffn1_w2", gw2a, l1)
    gw1a, (land["ffn1_w2"],) = _mm(da1t, xb, out_dtype=BF16, tm=704, tk=T, name="ffn1_dw1",
                                   comm=_chips_job([p_w2a]))
    gw1a = shards(gw1a)
    gw3a, (l1,) = _mm(db1t, xb, out_dtype=BF16, tm=704, tk=T, name="ffn1_dw3", comm=_pair_job([gw1a]))
    gw3a = shards(gw3a)
    p_w1a = pair_add("ffn1_w1", gw1a, l1)
    (l1,) = _comm_call(_pair_job([gw3a]), "rs_pair_ffn1_w3")
    p_w3a = pair_add("ffn1_w3", gw3a, l1)
    gx, (land["ffn1_w1"], land["ffn1_w3"]) = _ffn_dx(
        da1, db1, w1a, w3a, dr1, name="ffn1_dx", comm=_chips_job([p_w1a, p_w3a]))
    return loss, gx, land, G


ANY = pl.BlockSpec(memory_space=pl.ANY)


def _place():
    x, y, c = lax.axis_index("x"), lax.axis_index("y"), lax.axis_index("c")
    chips = [(1 - x, y), (x, 1 - y), (1 - x, 1 - y)]
    return x, y, c, chips


def _comm_call(job, name):
    n_in, n_out = len(job.inputs), len(job.out_shape)

    def body(*refs):
        ins, outs, sems = refs[:n_in], refs[n_in:n_in + n_out], refs[n_in + n_out:]
        job.start(ins, outs, sems)
        job.middle(ins, outs, sems)
        job.finish(ins, outs, sems)

    return pl.pallas_call(body, name=name, out_shape=job.out_shape, in_specs=[ANY] * n_in,
                          out_specs=[ANY] * n_out, scratch_shapes=job.sems)(*job.inputs)


def _ag_job(parts):
    n = len(parts)
    halves = [p.shape[0] // 2 // 16 * 16 or p.shape[0] // 2 // 8 * 8 or p.shape[0] for p in parts]
    cut = [h < p.shape[0] for h, p in zip(halves, parts)]

    def helpers(ins, outs, sems):
        send_sems, recv_sems, local_sems = sems
        x, y, c, _ = _place()
        me, sibling = (x, y, c), (x, y, 1 - c)
        xn, yn, dg = (1 - x, y), (x, 1 - y), (1 - x, 1 - y)

        def slot(a, block, part=None):
            bx, by, bc = block
            ref = outs[a].at[4 * bx + 2 * by + bc]
            if part == 0:
                return ref.at[pl.ds(0, halves[a])]
            if part == 1:
                return ref.at[pl.ds(halves[a], parts[a].shape[0] - halves[a])]
            return ref

        def copy(a, k, block, to, src=None, part=None):
            dst = slot(a, block, part)
            return pltpu.make_async_remote_copy(
                src_ref=dst if src is None else src, dst_ref=dst,
                send_sem=send_sems.at[a, k], recv_sem=recv_sems.at[a, k],
                device_id=to, device_id_type=MESH)

        local = [pltpu.make_async_copy(ins[a], slot(a, me), local_sems.at[a]) for a in range(n)]
        return copy, local, me, sibling, xn, yn, dg, c

    def start(ins, outs, sems):
        copy, local, me, sibling, xn, yn, dg, c = helpers(ins, outs, sems)
        for cp in local:
            cp.start()
        for a in range(n):
            copy(a, 0, me, sibling, src=ins[a]).start()
            copy(a, 1, me, (*xn, c), src=ins[a]).start()
            copy(a, 2, me, (*yn, c), src=ins[a]).start()

    def middle(ins, outs, sems):
        copy, local, me, sibling, xn, yn, dg, c = helpers(ins, outs, sems)
        for a in range(n):
            copy(a, 1, (*xn, c), me).wait_recv()
            copy(a, 3, (*xn, c), (*yn, c), part=0).start()
            copy(a, 5, (*xn, c), sibling).start()
        for a in range(n):
            copy(a, 2, (*yn, c), me).wait_recv()
            if cut[a]:
                copy(a, 4, (*yn, c), (*xn, c), part=1).start()
            copy(a, 6, (*yn, c), sibling).start()

    def finish(ins, outs, sems):
        copy, local, me, sibling, xn, yn, dg, c = helpers(ins, outs, sems)
        for a in range(n):
            copy(a, 3, (*dg, c), me, part=0).wait_recv()
            if cut[a]:
                copy(a, 4, (*dg, c), me, part=1).wait_recv()
            copy(a, 7, (*dg, c), sibling).start()
        for a in range(n):
            copy(a, 0, sibling, me).wait_recv()
            copy(a, 5, (*xn, 1 - c), me).wait_recv()
            copy(a, 6, (*yn, 1 - c), me).wait_recv()
            copy(a, 7, (*dg, 1 - c), me).wait_recv()
        for a in range(n):
            copy(a, 0, me, sibling, src=ins[a]).wait_send()
            copy(a, 1, me, (*xn, c), src=ins[a]).wait_send()
            copy(a, 2, me, (*yn, c), src=ins[a]).wait_send()
            copy(a, 3, (*xn, c), (*yn, c), part=0).wait_send()
            if cut[a]:
                copy(a, 4, (*yn, c), (*xn, c), part=1).wait_send()
            for k, blk in ((5, xn), (6, yn), (7, dg)):
                copy(a, k, (*blk, c), sibling).wait_send()
        for cp in local:
            cp.wait()

    return _Job(parts, [jax.ShapeDtypeStruct((N_DEV,) + p.shape, p.dtype) for p in parts],
                [pltpu.SemaphoreType.DMA((n, 8)), pltpu.SemaphoreType.DMA((n, 8)),
                 pltpu.SemaphoreType.DMA((n,))], start, finish, middle)


def _pair_job(gs):
    n = len(gs)

    def copies(g_refs, land_refs, sems):
        send_sems, recv_sems = sems
        x, y, c, _ = _place()
        return [pltpu.make_async_remote_copy(
            src_ref=g_refs[a].at[2 * k + 1 - c], dst_ref=land_refs[a].at[k],
            send_sem=send_sems.at[a, k], recv_sem=recv_sems.at[a, k],
            device_id=(x, y, 1 - c), device_id_type=MESH) for a in range(n) for k in range(4)]

    def start(ins, outs, sems):
        for cp in copies(ins, outs, sems):
            cp.start()

    def finish(ins, outs, sems):
        for cp in copies(ins, outs, sems):
            cp.wait()

    return _Job(gs, [jax.ShapeDtypeStruct((4,) + g.shape[1:], g.dtype) for g in gs],
                [pltpu.SemaphoreType.DMA((n, 4)), pltpu.SemaphoreType.DMA((n, 4))], start, finish)


def _chips_job(ps):
    n = len(ps)

    def copies(p_refs, land_refs, sems):
        send_sems, recv_sems, local_sems = sems
        x, y, c, chips = _place()
        mine = 2 * x + y
        owns = [pltpu.make_async_copy(p_refs[a].at[mine], land_refs[a].at[mine], local_sems.at[a])
                for a in range(n)]
        sends = [pltpu.make_async_remote_copy(
            src_ref=p_refs[a].at[2 * chip[0] + chip[1]], dst_ref=land_refs[a].at[mine],
            send_sem=send_sems.at[a, j], recv_sem=recv_sems.at[a, j],
            device_id=(*chip, c), device_id_type=MESH) for a in range(n) for j, chip in enumerate(chips)]
        recvs = [pltpu.make_async_remote_copy(
            src_ref=p_refs[a].at[mine], dst_ref=land_refs[a].at[2 * chip[0] + chip[1]],
            send_sem=send_sems.at[a, j], recv_sem=recv_sems.at[a, j],
            device_id=(*chip, c), device_id_type=MESH) for a in range(n) for j, chip in enumerate(chips)]
        return owns, sends, recvs

    def start(ins, outs, sems):
        owns, sends, _ = copies(ins, outs, sems)
        for cp in owns + sends:
            cp.start()

    def finish(ins, outs, sems):
        owns, sends, recvs = copies(ins, outs, sems)
        for cp in recvs:
            cp.wait_recv()
        for cp in sends:
            cp.wait_send()
        for cp in owns:
            cp.wait()

    return _Job(ps, [jax.ShapeDtypeStruct(p.shape, p.dtype) for p in ps],
                [pltpu.SemaphoreType.DMA((n, 3)), pltpu.SemaphoreType.DMA((n, 3)),
                 pltpu.SemaphoreType.DMA((n,))], start, finish)


def _all_reduce_small(buf, name):
    rows = buf.shape[0]

    def body(b_ref, o_ref, slots, send_sems, recv_sems):
        x, y, c, _ = _place()
        me = 4 * x + 2 * y + c
        slots[me] = b_ref[...]
        cps = []
        for k in range(1, N_DEV):
            fx, fy, fc = (k >> 2) & 1, (k >> 1) & 1, k & 1
            peer = (x ^ fx, y ^ fy, c ^ fc)
            cps.append(pltpu.make_async_remote_copy(
                src_ref=b_ref, dst_ref=slots.at[me],
                send_sem=send_sems.at[k - 1], recv_sem=recv_sems.at[k - 1],
                device_id=peer, device_id_type=MESH))
        for cp in cps:
            cp.start()
        for k in range(1, N_DEV):
            fx, fy, fc = (k >> 2) & 1, (k >> 1) & 1, k & 1
            src = 4 * (x ^ fx) + 2 * (y ^ fy) + (c ^ fc)
            pltpu.make_async_remote_copy(
                src_ref=b_ref, dst_ref=slots.at[src],
                send_sem=send_sems.at[k - 1], recv_sem=recv_sems.at[k - 1],
                device_id=(x ^ fx, y ^ fy, c ^ fc), device_id_type=MESH).wait_recv()
        for cp in cps:
            cp.wait_send()
        acc = slots[0]
        for k in range(1, N_DEV):
            acc = acc + slots[k]
        o_ref[...] = acc

    vm = pl.BlockSpec(memory_space=pltpu.VMEM)
    return pl.pallas_call(
        body, name=name, out_shape=jax.ShapeDtypeStruct(buf.shape, F32),
        in_specs=[vm], out_specs=vm,
        scratch_shapes=[pltpu.VMEM((N_DEV, rows, LANES), F32),
                        pltpu.SemaphoreType.DMA((N_DEV - 1,)), pltpu.SemaphoreType.DMA((N_DEV - 1,))],
    )(buf)


def _row_tile(rows, pref=2048):
    for t in range(min(pref, rows) - min(pref, rows) % 8, 0, -8):
        if rows % t == 0:
            return t
    return rows


def _tile2(rows, cols, nbuf):
    budget = VMEM_LIMIT // 2 // (2 * nbuf * 4)
    for t in range(min(512, rows) // 16 * 16, 0, -16):
        if rows % t == 0 and t * cols <= budget:
            return t, cols
    return rows, _pick(cols, max(LANES, budget // rows // LANES * LANES))


def _pair_add(g, land, core, name):
    _, r, c = g.shape
    tr, tc = _tile2(r, c, 3)

    def body(core_ref, g_ref, l_ref, o_ref):
        o_ref[...] = (g_ref[...].astype(F32) + l_ref[...].astype(F32)).astype(o_ref.dtype)

    blk = pl.BlockSpec((1, tr, tc), lambda k, i, j, cr: (k, i, j))
    return pl.pallas_call(
        body, name=name, out_shape=jax.ShapeDtypeStruct(land.shape, land.dtype),
        grid_spec=pltpu.PrefetchScalarGridSpec(
            num_scalar_prefetch=1, grid=(4, r // tr, c // tc),
            in_specs=[pl.BlockSpec((1, tr, tc), lambda k, i, j, cr: (2 * k + cr[0], i, j)), blk],
            out_specs=blk),
        compiler_params=_params("parallel", "parallel", "parallel"),
    )(core, g, land)


def _adamw(w, parts, m, v, name):
    shp = w.shape
    cols = shp[-1]
    w2, m2, v2 = (t.reshape(-1, cols) for t in (w, m, v))
    rows = w2.shape[0]
    n = parts.shape[0]
    assert parts.shape[1:] == (rows, cols), (parts.shape, shp)
    tr, tc = _tile2(rows, cols, n + 7)
    c1 = 1.0 / (1.0 - ADAM_B1 ** ADAM_STEP)
    c2 = 1.0 / (1.0 - ADAM_B2 ** ADAM_STEP)

    def body(*refs):
        p_refs = refs[:n]
        w_ref, m_ref, v_ref, g_ref, d_ref, nm_ref, nv_ref = refs[n:]
        gv = p_refs[0][0].astype(F32)
        for p_ref in p_refs[1:]:
            gv = gv + p_ref[0].astype(F32)
        nm = ADAM_B1 * m_ref[...] + (1.0 - ADAM_B1) * gv
        nv = ADAM_B2 * v_ref[...] + (1.0 - ADAM_B2) * (gv * gv)
        g_ref[...] = gv
        nm_ref[...] = nm
        nv_ref[...] = nv
        d_ref[...] = -ADAM_LR * ((nm * c1) / (jnp.sqrt(nv * c2) + ADAM_EPS) + ADAM_WD * w_ref[...])

    blk = pl.BlockSpec((tr, tc), lambda i, j: (i, j))
    slab = lambda k: pl.BlockSpec((1, tr, tc), lambda i, j: (k, i, j))
    out = jax.ShapeDtypeStruct((rows, cols), F32)
    outs = pl.pallas_call(
        body, name=name, out_shape=[out] * 4, grid=(rows // tr, cols // tc),
        in_specs=[slab(k) for k in range(n)] + [blk] * 3, out_specs=[blk] * 4,
        compiler_params=_params("parallel", "parallel"),
    )(*([parts] * n), w2, m2, v2)
    return tuple(o.reshape(shp) for o in outs)


def _pack_rows(arrs):
    rows = []
    for a in arrs:
        flat = a.reshape(-1).astype(F32)
        pad = (-flat.shape[0]) % LANES
        if pad:
            flat = jnp.concatenate([flat, jnp.zeros((pad,), F32)])
        rows.append(flat.reshape(-1, LANES))
    out = jnp.concatenate(rows, axis=0)
    pad = (-out.shape[0]) % 8
    if pad:
        out = jnp.concatenate([out, jnp.zeros((pad, LANES), F32)], axis=0)
    return out


def _unpack_rows(buf, shapes):
    outs, r = [], 0
    for s in shapes:
        n = math.prod(s)
        nr = -(-n // LANES)
        outs.append(buf[r:r + nr].reshape(-1)[:n].reshape(s))
        r += nr
    return outs


BIG = ("ffn1_w1", "ffn1_w3", "ffn1_w2", "w_in", "w_out", "ffn2_w1", "ffn2_w3", "ffn2_w2")
ROW_SHARDED = ("ffn1_w2", "w_out", "ffn2_w2")
SMALL = ("ln1_g", "ln1_b", "hgrn_lb", "hgrn_norm_g", "mlstm_conv_w", "mlstm_conv_b", "mlstm_ig_b",
         "mlstm_fg_b", "mlstm_norm_g", "ln2_g", "ln2_b", "ln3_g", "ln3_b")
WEIGHTS = ("ffn1_w1", "ffn1_w3", "ffn1_w2", "ln1_g", "ln1_b", "w_in", "hgrn_lb", "hgrn_norm_g",
           "mlstm_conv_w", "mlstm_conv_b", "mlstm_ig_b", "mlstm_fg_b", "mlstm_norm_g", "w_out",
           "ln2_g", "ln2_b", "ffn2_w1", "ffn2_w3", "ffn2_w2", "ln3_g", "ln3_b")


def kernel(x, ffn1_w1, ffn1_w3, ffn1_w2, ln1_g, ln1_b, w_in, hgrn_lb, hgrn_norm_g, mlstm_conv_w, mlstm_conv_b, mlstm_ig_b, mlstm_fg_b, mlstm_norm_g, w_out, ln2_g, ln2_b, ffn2_w1, ffn2_w3, ffn2_w2, ln3_g, ln3_b, loss_target, m_ffn1_w1, m_ffn1_w3, m_ffn1_w2, m_ln1_g, m_ln1_b, m_w_in, m_hgrn_lb, m_hgrn_norm_g, m_mlstm_conv_w, m_mlstm_conv_b, m_mlstm_ig_b, m_mlstm_fg_b, m_mlstm_norm_g, m_w_out, m_ln2_g, m_ln2_b, m_ffn2_w1, m_ffn2_w3, m_ffn2_w2, m_ln3_g, m_ln3_b, v_ffn1_w1, v_ffn1_w3, v_ffn1_w2, v_ln1_g, v_ln1_b, v_w_in, v_hgrn_lb, v_hgrn_norm_g, v_mlstm_conv_w, v_mlstm_conv_b, v_mlstm_ig_b, v_mlstm_fg_b, v_mlstm_norm_g, v_w_out, v_ln2_g, v_ln2_b, v_ffn2_w1, v_ffn2_w3, v_ffn2_w2, v_ln3_g, v_ln3_b):
    args = (ffn1_w1, ffn1_w3, ffn1_w2, ln1_g, ln1_b, w_in, hgrn_lb, hgrn_norm_g, mlstm_conv_w,
            mlstm_conv_b, mlstm_ig_b, mlstm_fg_b, mlstm_norm_g, w_out, ln2_g, ln2_b, ffn2_w1, ffn2_w3,
            ffn2_w2, ln3_g, ln3_b)
    ms = (m_ffn1_w1, m_ffn1_w3, m_ffn1_w2, m_ln1_g, m_ln1_b, m_w_in, m_hgrn_lb, m_hgrn_norm_g,
          m_mlstm_conv_w, m_mlstm_conv_b, m_mlstm_ig_b, m_mlstm_fg_b, m_mlstm_norm_g, m_w_out, m_ln2_g,
          m_ln2_b, m_ffn2_w1, m_ffn2_w3, m_ffn2_w2, m_ln3_g, m_ln3_b)
    vs = (v_ffn1_w1, v_ffn1_w3, v_ffn1_w2, v_ln1_g, v_ln1_b, v_w_in, v_hgrn_lb, v_hgrn_norm_g,
          v_mlstm_conv_w, v_mlstm_conv_b, v_mlstm_ig_b, v_mlstm_fg_b, v_mlstm_norm_g, v_w_out, v_ln2_g,
          v_ln2_b, v_ffn2_w1, v_ffn2_w3, v_ffn2_w2, v_ln3_g, v_ln3_b)
    w = dict(zip(WEIGHTS, args))
    m = dict(zip(WEIGHTS, ms))
    v = dict(zip(WEIGHTS, vs))
    core = lax.axis_index("c")
    dev = 4 * lax.axis_index("x") + 2 * lax.axis_index("y") + core
    p_lb = []

    def small_params(gathered_small):
        lb_sh, cw_sh = zip(*[_unpack_rows(gathered_small[i], [hgrn_lb.shape, mlstm_conv_w.shape])
                             for i in range(N_DEV)])
        hgrn_lb_full = jnp.concatenate(lb_sh, axis=-1)
        conv_w_full = jnp.concatenate(cw_sh, axis=-1)[0]
        p_lb.append(jax.nn.softmax(hgrn_lb_full, axis=1))
        return dict(ln1_g=ln1_g, ln1_b=ln1_b, ln2_g=ln2_g, ln2_b=ln2_b, ln3_g=ln3_g, ln3_b=ln3_b,
                    lb=p_lb[0][:, 0], hgrn_norm_g=hgrn_norm_g, conv_w=conv_w_full, conv_b=mlstm_conv_b,
                    ig_b=mlstm_ig_b[0], fg_b=mlstm_fg_b[0], mlstm_norm_g=mlstm_norm_g)

    def rows_first(t, k):
        return t[0] if k in ROW_SHARDED else t[0].T

    loss_row, grad_x, land2, G = _device_step(
        x[0], loss_target[0], {k: rows_first(w[k], k).astype(BF16) for k in BIG},
        _pack_rows([hgrn_lb, mlstm_conv_w]), small_params, jnp.reshape(core, (1,)).astype(jnp.int32))
    p_lb = p_lb[0]

    dlb = G["lb"]
    g_lb = jnp.stack([dlb * p_lb[:, 0] * (1.0 - p_lb[:, 0]), -dlb * p_lb[:, 0] * p_lb[:, 1]], axis=1)
    small_full = {"ln1_g": G["ln1_g"], "ln1_b": G["ln1_b"], "hgrn_lb": g_lb, "hgrn_norm_g": G["hgrn_norm_g"],
                  "mlstm_conv_w": G["conv_w"][None], "mlstm_conv_b": G["conv_b"],
                  "mlstm_ig_b": G["ig_b"][None], "mlstm_fg_b": G["fg_b"][None],
                  "mlstm_norm_g": G["mlstm_norm_g"], "ln2_g": G["ln2_g"], "ln2_b": G["ln2_b"],
                  "ln3_g": G["ln3_g"], "ln3_b": G["ln3_b"]}
    small_list = [small_full[k] for k in SMALL] + [loss_row]
    reduced = _all_reduce_small(_pack_rows(small_list), "all_reduce_small")
    red = _unpack_rows(reduced, [a.shape for a in small_list])
    loss = red[-1][0, 0]
    small_g = {}
    for k, gk in zip(SMALL, red[:-1]):
        if k in ("hgrn_lb", "mlstm_conv_w"):
            n = w[k].shape[-1]
            gk = lax.dynamic_slice_in_dim(gk, dev * n, n, axis=gk.ndim - 1)
        small_g[k] = gk

    grads, delta, new_m, new_v = {}, {}, {}, {}
    for k in BIG:
        res = _adamw(rows_first(w[k], k), land2[k], rows_first(m[k], k), rows_first(v[k], k), "adamw_" + k)
        grads[k], delta[k], new_m[k], new_v[k] = [(r if k in ROW_SHARDED else r.T)[None] for r in res]
    sm = _adamw(_pack_rows([w[k] for k in SMALL]), _pack_rows([small_g[k] for k in SMALL])[None],
                _pack_rows([m[k] for k in SMALL]), _pack_rows([v[k] for k in SMALL]), "adamw_small")
    shapes = [w[k].shape for k in SMALL]
    for dst, buf in zip((grads, delta, new_m, new_v), sm):
        for k, val in zip(SMALL, _unpack_rows(buf, shapes)):
            dst[k] = val
    return (loss, grad_x[None], *[grads[k] for k in WEIGHTS], *[delta[k] for k in WEIGHTS],
            *[new_m[k] for k in WEIGHTS], *[new_v[k] for k in WEIGHTS])
```

```python
import math

import jax
import jax.numpy as jnp
from jax import lax
from jax.experimental import pallas as pl
from jax.experimental.pallas import tpu as pltpu

F32 = jnp.float32
BF16 = jnp.bfloat16

CHUNK = 64
HGRN_HEAD_DIM = 128
CONV_WIDTH = 5
DN_ALPHA = 2.0 ** 0.25
LN_EPS = 1e-5
NORM_EPS = 1e-6
M_INIT = -1e30
NEG = -1e30
EXP_CLAMP = 80.0
ADAM_LR = 0.001
ADAM_B1 = 0.9
ADAM_B2 = 0.999
ADAM_EPS = 1e-08
ADAM_WD = 0.01
ADAM_STEP = 10
N_DEV = 8
LANES = 128
VMEM_LIMIT = 56 * 1024 * 1024
MESH = pl.DeviceIdType.MESH


def _params(*sem):
    return pltpu.CompilerParams(dimension_semantics=sem, vmem_limit_bytes=VMEM_LIMIT)


class _Job:
    def __init__(self, inputs, out_shape, sems, start, finish, middle=None):
        self.inputs, self.out_shape, self.sems = list(inputs), list(out_shape), list(sems)
        self.start, self.finish = start, finish
        self.middle = middle or (lambda ins, outs, sems: None)


def _join(*jobs):
    def split(refs, counts):
        out, p = [], 0
        for c in counts:
            out.append(refs[p:p + c])
            p += c
        return out

    n_in = [len(j.inputs) for j in jobs]
    n_out = [len(j.out_shape) for j in jobs]
    n_sem = [len(j.sems) for j in jobs]

    def start(ins, outs, sems):
        for j, i, o, s in zip(jobs, split(ins, n_in), split(outs, n_out), split(sems, n_sem)):
            j.start(i, o, s)

    def middle(ins, outs, sems):
        for j, i, o, s in zip(jobs, split(ins, n_in), split(outs, n_out), split(sems, n_sem)):
            j.middle(i, o, s)

    def finish(ins, outs, sems):
        for j, i, o, s in zip(jobs, split(ins, n_in), split(outs, n_out), split(sems, n_sem)):
            j.finish(i, o, s)

    return _Job(sum((j.inputs for j in jobs), []), sum((j.out_shape for j in jobs), []),
                sum((j.sems for j in jobs), []), start, finish, middle)


def _pcall(body, *, name, out_shape, grid, in_specs, out_specs, sem, scratch_shapes=(), comm=None):
    single = not isinstance(out_shape, (list, tuple))
    out_shape = [out_shape] if single else list(out_shape)
    out_specs = [out_specs] if single else list(out_specs)
    in_specs, scratch_shapes = list(in_specs), list(scratch_shapes)
    if comm is None:
        call = pl.pallas_call(body, name=name, out_shape=out_shape, grid=grid, in_specs=in_specs,
                              out_specs=out_specs, scratch_shapes=scratch_shapes,
                              compiler_params=_params(*sem))

        def run(*args):
            outs = call(*args)
            return outs[0] if single else outs
        return run

    n_in, n_out, n_sc = len(in_specs), len(out_shape), len(scratch_shapes)
    c_in, c_out = len(comm.inputs), len(comm.out_shape)

    def hosted(*refs):
        ins, cins = refs[:n_in], refs[n_in:n_in + c_in]
        p = n_in + c_in
        outs, couts = refs[p:p + n_out], refs[p + n_out:p + n_out + c_out]
        p += n_out + c_out
        scratch, csems = refs[p:p + n_sc], refs[p + n_sc:]
        first = pl.program_id(0) == 0
        half = pl.program_id(0) == grid[0] // 2
        last = pl.program_id(0) == grid[0] - 1
        for d in range(1, len(grid)):
            first = first & (pl.program_id(d) == 0)
            half = half & (pl.program_id(d) == 0)
            last = last & (pl.program_id(d) == grid[d] - 1)

        @pl.when(first)
        def _():
            comm.start(cins, couts, csems)

        if grid[0] >= 2:
            @pl.when(half)
            def _():
                comm.middle(cins, couts, csems)

        body(*ins, *outs, *scratch)

        @pl.when(last)
        def _():
            if grid[0] < 2:
                comm.middle(cins, couts, csems)
            comm.finish(cins, couts, csems)

    any_spec = pl.BlockSpec(memory_space=pl.ANY)
    call = pl.pallas_call(
        hosted, name=name, out_shape=out_shape + comm.out_shape, grid=grid,
        in_specs=in_specs + [any_spec] * c_in, out_specs=out_specs + [any_spec] * c_out,
        scratch_shapes=scratch_shapes + comm.sems,
        compiler_params=_params(*(["arbitrary"] * len(grid))))

    def run(*args):
        res = call(*args, *comm.inputs)
        outs, couts = res[:n_out], list(res[n_out:])
        return (outs[0] if single else outs), couts
    return run


def _sigmoid(x):
    return 1.0 / (1.0 + jnp.exp(-x))


def _log_sigmoid(x):
    return jnp.minimum(x, 0.0) - jnp.log(1.0 + jnp.exp(-jnp.abs(x)))


def _dot(a, b, dims):
    return lax.dot_general(a.astype(BF16), b.astype(BF16), (dims, ((), ())),
                           preferred_element_type=F32)


def _dot3(a, b, dims):
    ah = a.astype(BF16)
    al = (a - ah.astype(F32)).astype(BF16)
    bh = b.astype(BF16)
    bl = (b - bh.astype(F32)).astype(BF16)
    d = (dims, ((), ()))
    out = lax.dot_general(ah, bh, d, preferred_element_type=F32)
    out = out + lax.dot_general(ah, bl, d, preferred_element_type=F32)
    return out + lax.dot_general(al, bh, d, preferred_element_type=F32)


_DIMS = {"nn": ((1,), (0,)), "nt": ((1,), (1,)), "tn": ((0,), (0,))}


def _dot_nn(a, b):
    return _dot3(a, b, _DIMS["nn"])


def _dot_nt(a, b):
    return _dot3(a, b, _DIMS["nt"])


def _dot_tn(a, b):
    return _dot3(a, b, _DIMS["tn"])


def _split3(x):
    hi = x.astype(BF16)
    r1 = x - hi.astype(F32)
    mid = r1.astype(BF16)
    lo = (r1 - mid.astype(F32)).astype(BF16)
    return hi, mid, lo


def _dot01(mask01, x, mode="nn"):
    m = mask01.astype(BF16)
    hi, mid, lo = _split3(x)
    d = (_DIMS[mode], ((), ()))
    out = lax.dot_general(m, hi, d, preferred_element_type=F32)
    out = out + lax.dot_general(m, mid, d, preferred_element_type=F32)
    return out + lax.dot_general(m, lo, d, preferred_element_type=F32)


def _matmul(a, b, *, mode="nn", out_dtype=F32, tm=512, tn=512, tk=None,
            add=None, scale=1.0, add_scale=1.0, name, comm=None, rows=None):
    if mode == "nn":
        (M, K), (K2, N) = a.shape, b.shape
        K2 = rows or K2
    elif mode == "nt":
        (M, K), (N, K2) = a.shape, b.shape
        N = rows or N
    else:
        (K, M), (K2, N) = a.shape, b.shape
    assert K == K2, (a.shape, b.shape, mode)
    tm, tn = min(tm, M), min(tn, N)
    tk = min(tk or K, K)
    assert M % tm == 0 and N % tn == 0 and K % tk == 0, (M, N, K, tm, tn, tk)
    nk = K // tk
    dims = _DIMS[mode]
    has_add = add is not None

    def body(*refs):
        if has_add:
            a_ref, b_ref, add_ref, o_ref = refs[:4]
        else:
            a_ref, b_ref, o_ref = refs[:3]
            add_ref = None
        acc_ref = refs[-1] if nk > 1 else None

        def finish(acc):
            out = acc if scale == 1.0 else acc * scale
            if has_add:
                out = out + add_ref[...].astype(F32) * add_scale
            o_ref[...] = out.astype(o_ref.dtype)

        if nk == 1:
            finish(_dot(a_ref[...], b_ref[...], dims))
        else:
            k = pl.program_id(2)

            @pl.when(k == 0)
            def _():
                acc_ref[...] = jnp.zeros_like(acc_ref)

            acc_ref[...] += _dot(a_ref[...], b_ref[...], dims)

            @pl.when(k == nk - 1)
            def _():
                finish(acc_ref[...])

    if mode == "tn":
        a_spec = pl.BlockSpec((tk, tm), lambda i, j, k: (k, i))
    else:
        a_spec = pl.BlockSpec((tm, tk), lambda i, j, k: (i, k))
    if mode == "nt":
        b_spec = pl.BlockSpec((tn, tk), lambda i, j, k: (j, k))
    else:
        b_spec = pl.BlockSpec((tk, tn), lambda i, j, k: (k, j))
    o_spec = pl.BlockSpec((tm, tn), lambda i, j, k: (i, j))
    in_specs = [a_spec, b_spec] + ([o_spec] if has_add else [])
    args = (a, b) + ((add,) if has_add else ())
    return _pcall(
        body, name=name, comm=comm,
        out_shape=jax.ShapeDtypeStruct((M, N), out_dtype),
        grid=(M // tm, N // tn, nk),
        in_specs=in_specs, out_specs=o_spec,
        scratch_shapes=[pltpu.VMEM((tm, tn), F32)] if nk > 1 else [],
        sem=("parallel", "parallel", "arbitrary"),
    )(*args)


def _ln_stats(x):
    mu = jnp.mean(x, axis=-1, keepdims=True)
    xc = x - mu
    var = jnp.mean(xc * xc, axis=-1, keepdims=True)
    rstd = lax.rsqrt(var + LN_EPS)
    return xc * rstd, rstd


def _ln_fwd(r, g, b, *, tm=256, name):
    T, D = r.shape
    tm = min(tm, T)

    def body(r_ref, g_ref, b_ref, y_ref, yb_ref):
        xhat, _ = _ln_stats(r_ref[...])
        y = xhat * g_ref[...] + b_ref[...]
        y_ref[...] = y
        yb_ref[...] = y.astype(BF16)

    row = pl.BlockSpec((tm, D), lambda i: (i, 0))
    vec = pl.BlockSpec((1, D), lambda i: (0, 0))
    return pl.pallas_call(
        body, name=name,
        out_shape=[jax.ShapeDtypeStruct((T, D), F32), jax.ShapeDtypeStruct((T, D), BF16)],
        grid=(T // tm,), in_specs=[row, vec, vec], out_specs=[row, row],
        compiler_params=_params("parallel"),
    )(r, g, b)


def _cast(x, *, tm=512, name):
    T, D = x.shape
    tm = min(tm, T)

    def body(x_ref, xb_ref):
        xb_ref[...] = x_ref[...].astype(BF16)

    row = pl.BlockSpec((tm, D), lambda i: (i, 0))
    return pl.pallas_call(
        body, name=name, out_shape=jax.ShapeDtypeStruct((T, D), BF16),
        grid=(T // tm,), in_specs=[row], out_specs=row,
        compiler_params=_params("parallel"),
    )(x)


def _ln_bwd(dy, r, g, *, tm=256, name, b=None, target=None):
    T, D = r.shape
    tm = min(tm, T)
    with_loss = target is not None

    def body(*refs):
        if with_loss:
            r_ref, g_ref, b_ref, t_ref, dr_ref, drb_ref, dg_ref, db_ref, loss_ref = refs
        else:
            dy_ref, r_ref, g_ref, dr_ref, drb_ref, dg_ref, db_ref = refs
        i = pl.program_id(0)
        xhat, rstd = _ln_stats(r_ref[...])
        gg = g_ref[...]
        if with_loss:
            err = xhat * gg + b_ref[...] - t_ref[...]
            dyv = err * (1.0 / D)
            part = jnp.sum(jnp.sum(err * err, axis=1, keepdims=True), axis=0, keepdims=True)
            part = jnp.broadcast_to(part * (0.5 / D), (1, LANES))
        else:
            dyv = dy_ref[...]
        dxh = dyv * gg
        m1 = jnp.mean(dxh, axis=-1, keepdims=True)
        m2 = jnp.mean(dxh * xhat, axis=-1, keepdims=True)
        dr = rstd * (dxh - m1 - xhat * m2)
        dr_ref[...] = dr
        drb_ref[...] = dr.astype(BF16)
        dgp = jnp.sum(dyv * xhat, axis=0, keepdims=True)
        dbp = jnp.sum(dyv, axis=0, keepdims=True)

        @pl.when(i == 0)
        def _():
            dg_ref[...] = dgp
            db_ref[...] = dbp
            if with_loss:
                loss_ref[...] = part

        @pl.when(i > 0)
        def _():
            dg_ref[...] += dgp
            db_ref[...] += dbp
            if with_loss:
                loss_ref[...] += part

    row = pl.BlockSpec((tm, D), lambda i: (i, 0))
    vec = pl.BlockSpec((1, D), lambda i: (0, 0))
    out_shape = [jax.ShapeDtypeStruct((T, D), F32), jax.ShapeDtypeStruct((T, D), BF16),
                 jax.ShapeDtypeStruct((1, D), F32), jax.ShapeDtypeStruct((1, D), F32)]
    out_specs = [row, row, vec, vec]
    if with_loss:
        in_specs, args = [row, vec, vec, row], (r, g, b, target)
        out_shape.append(jax.ShapeDtypeStruct((1, LANES), F32))
        out_specs.append(pl.BlockSpec((1, LANES), lambda i: (0, 0)))
    else:
        in_specs, args = [row, row, vec], (dy, r, g)
    return pl.pallas_call(
        body, name=name, out_shape=out_shape, grid=(T // tm,),
        in_specs=in_specs, out_specs=out_specs,
        compiler_params=_params("arbitrary"),
    )(*args)


def _ffn_up(xb, w1t, w3t, *, a=None, tm=1024, tf=512, name, comm=None):
    F, D = w3t.shape
    T = xb.shape[0]
    tm, tf = _pick(T, tm), _pick(F, tf)
    given = a is not None

    def body(x_ref, w1_ref, w3_ref, a_ref, b_ref, h_ref, ht_ref):
        xv = x_ref[...]
        av = w1_ref[...] if given else _dot(xv, w1_ref[...], _DIMS["nt"])
        b = _dot(xv, w3_ref[...], _DIMS["nt"])
        a_ref[...] = av.astype(BF16)
        b_ref[...] = b.astype(BF16)
        h = av * _sigmoid(av) * b
        h_ref[...] = h.astype(BF16)
        ht_ref[...] = h.T.astype(BF16)

    wsp = pl.BlockSpec((tf, D), lambda j, i: (j, 0))
    osp = pl.BlockSpec((tm, tf), lambda j, i: (i, j))
    act = jax.ShapeDtypeStruct((T, F), BF16)
    return _pcall(
        body, name=name, comm=comm, out_shape=[act, act, act, jax.ShapeDtypeStruct((F, T), BF16)],
        grid=(F // tf, T // tm),
        in_specs=[pl.BlockSpec((tm, D), lambda j, i: (i, 0)), osp if given else wsp, wsp],
        out_specs=[osp, osp, osp, pl.BlockSpec((tf, tm), lambda j, i: (j, i))],
        sem=("parallel", "parallel"),
    )(xb, a if given else w1t, w3t)


def _ffn_bwd_act(drb, w2, a, b, *, tm=1024, tf=512, name, comm=None):
    F, D = w2.shape
    T = drb.shape[0]
    tm, tf = _pick(T, tm), _pick(F, tf)

    def body(dr_ref, w2_ref, a_ref, b_ref, da_ref, db_ref, dat_ref, dbt_ref):
        d = 0.5 * _dot(dr_ref[...], w2_ref[...], _DIMS["nt"])
        av = a_ref[...].astype(F32)
        sig = _sigmoid(av)
        da = d * b_ref[...].astype(F32) * sig * (1.0 + av * (1.0 - sig))
        db = d * av * sig
        da_ref[...] = da.astype(BF16)
        db_ref[...] = db.astype(BF16)
        dat_ref[...] = da.T.astype(BF16)
        dbt_ref[...] = db.T.astype(BF16)

    asp = pl.BlockSpec((tm, tf), lambda i, j: (i, j))
    tsp = pl.BlockSpec((tf, tm), lambda i, j: (j, i))
    out, out_t = jax.ShapeDtypeStruct((T, F), BF16), jax.ShapeDtypeStruct((F, T), BF16)
    return _pcall(
        body, name=name, comm=comm, out_shape=[out, out, out_t, out_t], grid=(T // tm, F // tf),
        in_specs=[pl.BlockSpec((tm, D), lambda i, j: (i, 0)),
                  pl.BlockSpec((tf, D), lambda i, j: (j, 0)), asp, asp],
        out_specs=[asp, asp, tsp, tsp],
        sem=("parallel", "parallel"),
    )(drb, w2, a, b)


def _dz_pack(pieces, *, tm=256, name):
    groups = [p if isinstance(p, tuple) else (p,) for p in pieces]
    T = groups[0][0].shape[0]
    widths = [g[0].shape[1] for g in groups]
    W = sum(widths)
    tm = _pick(T, tm)
    flat = [a for g in groups for a in g]
    n = len(flat)

    def body(*refs):
        o_ref, ot_ref = refs[n], refs[n + 1]
        c, p = 0, 0
        for g, w in zip(groups, widths):
            v = refs[p][...].astype(F32)
            for r in refs[p + 1:p + len(g)]:
                v = v + r[...].astype(F32)
            p += len(g)
            o_ref[:, c:c + w] = v.astype(BF16)
            ot_ref[c:c + w, :] = v.T.astype(BF16)
            c += w

    return pl.pallas_call(
        body, name=name,
        out_shape=[jax.ShapeDtypeStruct((T, W), BF16), jax.ShapeDtypeStruct((W, T), BF16)],
        grid=(T // tm,),
        in_specs=[pl.BlockSpec((tm, a.shape[1]), lambda i: (i, 0)) for a in flat],
        out_specs=[pl.BlockSpec((tm, W), lambda i: (i, 0)), pl.BlockSpec((W, tm), lambda i: (0, i))],
        compiler_params=_params("parallel"),
    )(*flat)


def _ffn_dx(da, db, w1t, w3t, dr, *, tm=1024, tn=1024, tk=None, name, comm=None):
    T, F = da.shape
    D = w1t.shape[1]
    tm, tn = _pick(T, tm), _pick(D, tn)
    tk = _pick(F, tk or F // 4)
    nk = F // tk

    def body(da_ref, db_ref, w1_ref, w3_ref, dr_ref, o_ref, acc_ref):
        k = pl.program_id(2)

        @pl.when(k == 0)
        def _():
            acc_ref[...] = jnp.zeros_like(acc_ref)

        acc_ref[...] += (_dot(da_ref[...], w1_ref[...], _DIMS["nn"])
                         + _dot(db_ref[...], w3_ref[...], _DIMS["nn"]))

        @pl.when(k == nk - 1)
        def _():
            o_ref[...] = DN_ALPHA * dr_ref[...] + acc_ref[...]

    asp = pl.BlockSpec((tm, tk), lambda i, j, k: (i, k))
    wsp = pl.BlockSpec((tk, tn), lambda i, j, k: (k, j))
    osp = pl.BlockSpec((tm, tn), lambda i, j, k: (i, j))
    return _pcall(
        body, name=name, comm=comm, out_shape=jax.ShapeDtypeStruct((T, D), F32),
        grid=(T // tm, D // tn, nk), in_specs=[asp, asp, wsp, wsp, osp], out_specs=osp,
        scratch_shapes=[pltpu.VMEM((tm, tn), F32)],
        sem=("parallel", "parallel", "arbitrary"),
    )(da, db, w1t, w3t, dr)


def _chunk_mask(reverse, transpose=False):
    row = lax.broadcasted_iota(jnp.int32, (CHUNK, CHUNK), 0)
    col = lax.broadcasted_iota(jnp.int32, (CHUNK, CHUNK), 1)
    if reverse != transpose:
        return col >= row
    return col <= row


def _run_specs(specs, *, name, comm=None):
    counts = [(len(s["in_specs"]), len(s["out_specs"]), len(s["scratch_shapes"])) for s in specs]

    def body(*refs):
        p, parts = 0, [[], [], []]
        for kind in range(3):
            for c in counts:
                parts[kind].append(refs[p:p + c[kind]])
                p += c[kind]
        live = [s["body"](*parts[0][k], *parts[1][k], *parts[2][k]) for k, s in enumerate(specs)]
        while live:
            for g in list(live):
                if next(g, StopIteration) is StopIteration:
                    live.remove(g)

    cat = lambda key: [v for s in specs for v in s[key]]
    res = _pcall(body, name=name, comm=comm, out_shape=cat("out_shape"), grid=specs[0]["grid"],
                 in_specs=cat("in_specs"), out_specs=cat("out_specs"),
                 scratch_shapes=cat("scratch_shapes"), sem=("arbitrary",))(*cat("args"))
    outs, couts = res if comm is not None else (res, None)
    split, p = [], 0
    for c in counts:
        split.append(outs[p:p + c[1]])
        p += c[1]
    return (split, couts) if comm is not None else split


def _hgrn_pre(hq, hf, lb):
    sig = _sigmoid(hf)
    f = lb + (1.0 - lb) * sig
    q = hq * _sigmoid(hq) * (HGRN_HEAD_DIM ** -0.5)
    return q, f, sig


def _hgrn_decays(f, cmf, reverse):
    bc = _dot01(cmf, jnp.log(f))
    last = 0 if reverse else CHUNK - 1
    blast = bc[last:last + 1, :]
    bref = bc[CHUNK // 2:CHUNK // 2 + 1, :]
    eq = jnp.exp(jnp.minimum(bc - bref, EXP_CLAMP))
    ek = jnp.exp(jnp.minimum(bref - bc, EXP_CLAMP))
    return bc, blast, eq, ek


def _hgrn_fwd(z, lb, *, wh, reverse):
    T = z.shape[0]
    nch = T // CHUNK
    nh = wh // HGRN_HEAD_DIM
    hd = HGRN_HEAD_DIM

    def ci(i):
        return nch - 1 - i if reverse else i

    def body(hq_ref, hi_ref, hf_ref, lb_ref, o_ref, st_ref, s_ref):
        @pl.when(pl.program_id(0) == 0)
        def _():
            s_ref[...] = jnp.zeros_like(s_ref)

        cm = _chunk_mask(reverse)
        cmf = cm.astype(F32)
        H = range(nh)
        sls = [slice(h * hd, (h + 1) * hd) for h in H]
        q, f, _ = _hgrn_pre(hq_ref[...], hf_ref[...], lb_ref[...])
        v = hi_ref[...]
        k = 1.0 - f
        yield
        bc, blast, eq, ek = _hgrn_decays(f, cmf, reverse)
        yield
        qh, kh, qe = q * eq, k * ek, q * jnp.exp(bc)
        k2, eblast = k * jnp.exp(blast - bc), jnp.exp(blast)
        st = [s_ref[h] for h in H]
        yield
        att = [jnp.where(cm, _dot_nt(qh[:, sl], kh[:, sl]), 0.0) for sl in sls]
        yield
        inter = [_dot(qe[:, sls[h]], st[h], _DIMS["nt"]) for h in H]
        yield
        out = [inter[h] + _dot(att[h], v[:, sls[h]], _DIMS["nn"]) for h in H]
        yield
        snew = [eblast[:, sls[h]] * st[h] + _dot_tn(v[:, sls[h]], k2[:, sls[h]]) for h in H]
        yield
        for h in H:
            st_ref[0, h] = st[h]
            o_ref[:, sls[h]] = out[h]
            s_ref[h] = snew[h]

    blk = lambda c: pl.BlockSpec((CHUNK, wh), lambda i: (ci(i), c))
    return dict(
        body=body, grid=(nch,), args=[z, z, z, lb],
        out_shape=[jax.ShapeDtypeStruct((T, wh), F32),
                   jax.ShapeDtypeStruct((nch, nh, hd, hd), F32)],
        in_specs=[blk(0), blk(1), blk(3 + int(reverse)),
                  pl.BlockSpec((1, wh), lambda i: (0, 0))],
        out_specs=[blk(0), pl.BlockSpec((1, nh, hd, hd), lambda i: (ci(i), 0, 0, 0))],
        scratch_shapes=[pltpu.VMEM((nh, hd, hd), F32)])


def _hgrn_bwd(z, lb, do, states, *, wh, reverse):
    T = z.shape[0]
    nch = T // CHUNK
    nh = wh // HGRN_HEAD_DIM
    hd = HGRN_HEAD_DIM

    def ci(i):
        return i if reverse else nch - 1 - i

    def body(hq_ref, hi_ref, hf_ref, lb_ref, do_ref, st_ref,
             dhq_ref, dhi_ref, dhf_ref, dlb_ref, ds_ref, gs_ref):
        @pl.when(pl.program_id(0) == 0)
        def _():
            ds_ref[...] = jnp.zeros_like(ds_ref)
            gs_ref[...] = jnp.zeros_like(gs_ref)
            dlb_ref[...] = jnp.zeros_like(dlb_ref)

        cm = _chunk_mask(reverse)
        cmf = cm.astype(F32)
        cmtf = _chunk_mask(reverse, transpose=True).astype(F32)
        H = range(nh)
        sls = [slice(h * hd, (h + 1) * hd) for h in H]
        hq, lb = hq_ref[...], lb_ref[...]
        q, f, sig = _hgrn_pre(hq, hf_ref[...], lb)
        v, dov = hi_ref[...], do_ref[...]
        k = 1.0 - f
        yield
        bc, blast, eq, ek = _hgrn_decays(f, cmf, reverse)
        yield
        ebc, eb2, eblast = jnp.exp(bc), jnp.exp(blast - bc), jnp.exp(blast)
        qh, kh, qe, k2 = q * eq, k * ek, q * ebc, k * eb2
        st = [st_ref[0, h] for h in H]
        dst = [ds_ref[h] for h in H]
        yield
        att = [jnp.where(cm, _dot_nt(qh[:, sl], kh[:, sl]), 0.0) for sl in sls]
        yield
        datt = [jnp.where(cm, _dot_nt(dov[:, sl], v[:, sl]), 0.0) for sl in sls]
        yield
        dq_a = [_dot_nn(datt[h], kh[:, sls[h]]) for h in H]
        yield
        dq_s = [_dot_nn(dov[:, sls[h]], st[h]) for h in H]
        yield
        dk_a = [_dot_tn(datt[h], qh[:, sls[h]]) for h in H]
        yield
        dk_s = [_dot_nn(v[:, sls[h]], dst[h]) for h in H]
        yield
        dv = [_dot(att[h], dov[:, sls[h]], _DIMS["tn"]) + _dot(k2[:, sls[h]], dst[h], _DIMS["nt"])
              for h in H]
        yield
        dsn = [eblast[:, sls[h]] * dst[h] + _dot_tn(dov[:, sls[h]], qe[:, sls[h]]) for h in H]
        for h in H:
            ds_ref[h] = dsn[h]
        yield
        dq = jnp.concatenate(dq_a, axis=1) * eq + ebc * jnp.concatenate(dq_s, axis=1)
        dk = jnp.concatenate(dk_a, axis=1) * ek + eb2 * jnp.concatenate(dk_s, axis=1)
        db = q * dq - k * dk
        yield
        dg = _dot01(cmtf, db) + gs_ref[...]
        gs_ref[...] += jnp.sum(db, axis=0, keepdims=True)
        yield
        df = dg / f - dk
        dhf_ref[...] = (df * (1.0 - lb) * sig * (1.0 - sig)).astype(BF16)
        dlb_ref[...] += jnp.sum(df * (1.0 - sig), axis=0, keepdims=True)
        sq = _sigmoid(hq)
        dhq_ref[...] = (dq * (HGRN_HEAD_DIM ** -0.5) * sq * (1.0 + hq * (1.0 - sq))).astype(BF16)
        dhi_ref[...] = jnp.concatenate(dv, axis=1).astype(BF16)

    blk = lambda c: pl.BlockSpec((CHUNK, wh), lambda i: (ci(i), c))
    vec = pl.BlockSpec((1, wh), lambda i: (0, 0))
    big = jax.ShapeDtypeStruct((T, wh), BF16)
    return dict(
        body=body, grid=(nch,), args=[z, z, z, lb, do, states],
        in_specs=[blk(0), blk(1), blk(3 + int(reverse)), vec, blk(0),
                  pl.BlockSpec((1, nh, hd, hd), lambda i: (ci(i), 0, 0, 0))],
        out_shape=[big, big, big, jax.ShapeDtypeStruct((1, wh), F32)],
        out_specs=[blk(0), blk(0), blk(0), vec],
        scratch_shapes=[pltpu.VMEM((nh, hd, hd), F32), pltpu.VMEM((1, wh), F32)])


def _mlstm_intra(cm, cmt, ig_row, ig_col, xf_row, xf_col, m_st):
    lf_row, lf_col = _log_sigmoid(xf_row), _log_sigmoid(xf_col)
    bcol = jnp.sum(jnp.where(cm, lf_row, 0.0), axis=1, keepdims=True)
    brow = jnp.sum(jnp.where(cmt, lf_col, 0.0), axis=0, keepdims=True)
    blast = jnp.sum(lf_row, axis=1, keepdims=True)
    dmat = jnp.where(cm, bcol - brow + ig_row, NEG)
    m_inter = bcol + m_st
    m_t = jnp.maximum(m_inter, jnp.max(dmat, axis=1, keepdims=True))
    p = jnp.exp(dmat - m_t)
    inter = jnp.exp(m_inter - m_t)
    w_col = blast - bcol + ig_col
    m_new = jnp.maximum(blast + m_st, jnp.max(w_col, axis=0, keepdims=True))
    cs = jnp.exp(blast + m_st - m_new)
    kscale = jnp.exp(w_col - m_new)
    return p, inter, m_t, m_new, cs, kscale


def _mlstm_specs(T, wm, nhm, reverse, backward):
    nch = T // CHUNK
    dm = wm // nhm
    ng = 4 * nhm

    def ci(i):
        fwd_order = nch - 1 - i if reverse else i
        return nch - 1 - fwd_order if backward else fwd_order

    row = lambda w, c: pl.BlockSpec((CHUNK, w), lambda i: (ci(i), c))
    gates_row = pl.BlockSpec((1, ng, CHUNK), lambda i: (ci(i), 0, 0))
    bias_row = pl.BlockSpec((1, LANES), lambda i: (0, 0))
    bias_col = pl.BlockSpec((ng, 1), lambda i: (0, 0))
    st_c = pl.BlockSpec((1, nhm, dm, dm), lambda i: (ci(i), 0, 0, 0))
    st_n = pl.BlockSpec((1, nhm, 1, dm), lambda i: (ci(i), 0, 0, 0))
    st_m = pl.BlockSpec((1, nhm, 1, LANES), lambda i: (ci(i), 0, 0, 0))
    return nch, dm, ng, ci, row, gates_row, bias_row, bias_col, st_c, st_n, st_m


def _mlstm_fwd(qk, z, gc, gr, br, bcl, *, wm, nhm, vcol, reverse):
    T = qk.shape[0]
    nch, dm, ng, ci, row, gates_row, bias_row, bias_col, st_c, st_n, st_m = _mlstm_specs(
        T, wm, nhm, reverse, False)
    d = int(reverse)

    def body(q_ref, k_ref, v_ref, gc_ref, gr_ref, br_ref, bc_ref,
             h_ref, cst_ref, nst_ref, mst_ref, c_ref, n_ref, m_ref):
        @pl.when(pl.program_id(0) == 0)
        def _():
            c_ref[...] = jnp.zeros_like(c_ref)
            n_ref[...] = jnp.zeros_like(n_ref)
            m_ref[...] = jnp.full_like(m_ref, M_INIT)

        cm = _chunk_mask(reverse)
        cmt = _chunk_mask(reverse, transpose=True)
        G = gc_ref[...] + br_ref[...]
        Gr = gr_ref[0] + bc_ref[...]
        H = range(nhm)
        sls = [slice(h * dm, (h + 1) * dm) for h in H]
        m_all = [m_ref[h] for h in H]
        intra = [_mlstm_intra(cm, cmt, Gr[d * nhm + h:d * nhm + h + 1, :], G[:, d * nhm + h:d * nhm + h + 1],
                              Gr[2 * nhm + d * nhm + h:2 * nhm + d * nhm + h + 1, :],
                              G[:, 2 * nhm + d * nhm + h:2 * nhm + d * nhm + h + 1], m_all[h][:, 0:1])
                 for h in H]
        p, inter, m_t, m_new, cs, kscale = zip(*intra)
        yield
        q = [q_ref[:, sl] * (dm ** -0.5) for sl in sls]
        k = [k_ref[:, sl] for sl in sls]
        v = [v_ref[:, sl] for sl in sls]
        ct = [c_ref[h] for h in H]
        n = [n_ref[h] for h in H]
        yield
        sc = [_dot_nt(q[h], k[h]) * p[h] for h in H]
        yield
        qc = [_dot(q[h], ct[h], _DIMS["nt"]) for h in H]
        yield
        num = [_dot(sc[h], v[h], _DIMS["nn"]) + inter[h] * qc[h] for h in H]
        yield
        den = [jnp.sum(sc[h], axis=1, keepdims=True) + inter[h] * jnp.sum(q[h] * n[h], axis=1, keepdims=True)
               for h in H]
        kw = [k[h] * kscale[h] for h in H]
        yield
        cnew = [cs[h] * ct[h] + _dot_tn(v[h], kw[h]) for h in H]
        yield
        for h in H:
            cst_ref[0, h] = ct[h]
            nst_ref[0, h] = n[h]
            mst_ref[0, h] = m_all[h]
            h_ref[:, sls[h]] = num[h] / jnp.maximum(jnp.abs(den[h]), jnp.exp(-m_t[h]))
            c_ref[h] = cnew[h]
            n_ref[h] = cs[h] * n[h] + jnp.sum(kw[h], axis=0, keepdims=True)
            m_ref[h] = jnp.broadcast_to(m_new[h], (1, LANES))

    return dict(
        body=body, grid=(nch,), args=[qk, qk, z, gc, gr, br, bcl],
        out_shape=[jax.ShapeDtypeStruct((T, wm), F32),
                   jax.ShapeDtypeStruct((nch, nhm, dm, dm), F32),
                   jax.ShapeDtypeStruct((nch, nhm, 1, dm), F32),
                   jax.ShapeDtypeStruct((nch, nhm, 1, LANES), F32)],
        in_specs=[row(wm, 0), row(wm, 1), row(wm, vcol), row(LANES, 0), gates_row, bias_row, bias_col],
        out_specs=[row(wm, 0), st_c, st_n, st_m],
        scratch_shapes=[pltpu.VMEM((nhm, dm, dm), F32), pltpu.VMEM((nhm, 1, dm), F32),
                        pltpu.VMEM((nhm, 1, LANES), F32)])


def _mlstm_bwd(qk, z, gc, gr, br, bcl, dh, states, *, wm, nhm, vcol, reverse):
    T = qk.shape[0]
    nch, dm, ng, ci, row, gates_row, bias_row, bias_col, st_c, st_n, st_m = _mlstm_specs(
        T, wm, nhm, reverse, True)
    d = int(reverse)

    def body(q_ref, k_ref, v_ref, gc_ref, gr_ref, br_ref, bc_ref, dh_ref, cst_ref, nst_ref, mst_ref,
             dqk_ref, dv_ref, dgr_ref, dgs_ref, e_ref, en_ref, fs_ref):
        @pl.when(pl.program_id(0) == 0)
        def _():
            e_ref[...] = jnp.zeros_like(e_ref)
            en_ref[...] = jnp.zeros_like(en_ref)
            fs_ref[...] = jnp.zeros_like(fs_ref)
            dgs_ref[...] = jnp.zeros_like(dgs_ref)

        cm = _chunk_mask(reverse)
        cmt = _chunk_mask(reverse, transpose=True)
        row_i = lax.broadcasted_iota(jnp.int32, (CHUNK, CHUNK), 0)
        col_i = lax.broadcasted_iota(jnp.int32, (CHUNK, CHUNK), 1)
        eye = row_i == col_i
        G = gc_ref[...] + br_ref[...]
        Gr = gr_ref[0] + bc_ref[...]
        H = range(nhm)
        sls = [slice(h * dm, (h + 1) * dm) for h in H]
        xf_row = [Gr[2 * nhm + d * nhm + h:2 * nhm + d * nhm + h + 1, :] for h in H]
        intra = [_mlstm_intra(cm, cmt, Gr[d * nhm + h:d * nhm + h + 1, :], G[:, d * nhm + h:d * nhm + h + 1],
                              xf_row[h], G[:, 2 * nhm + d * nhm + h:2 * nhm + d * nhm + h + 1],
                              mst_ref[0, h][:, 0:1]) for h in H]
        p, inter, m_t, _, cs, kscale = zip(*intra)
        yield
        q = [q_ref[:, sls[h]] * (dm ** -0.5) for h in H]
        k = [k_ref[:, sls[h]] for h in H]
        v = [v_ref[:, sls[h]] for h in H]
        dhv = [dh_ref[:, sls[h]] for h in H]
        ct = [cst_ref[0, h] for h in H]
        n = [nst_ref[0, h] for h in H]
        et = [e_ref[h] for h in H]
        en = [en_ref[h] for h in H]
        carry = [fs_ref[h][:, 0:1] for h in H]
        yield
        sc = [_dot_nt(q[h], k[h]) * p[h] for h in H]
        yield
        qc = [_dot_nt(q[h], ct[h]) for h in H]
        yield
        num = [_dot_nn(sc[h], v[h]) + inter[h] * qc[h] for h in H]
        yield
        den = [jnp.sum(sc[h], axis=1, keepdims=True) + inter[h] * jnp.sum(q[h] * n[h], axis=1, keepdims=True)
               for h in H]
        floor = [jnp.exp(-m_t[h]) for h in H]
        nstab = [jnp.maximum(jnp.abs(den[h]), floor[h]) for h in H]
        gh = [dhv[h] / nstab[h] for h in H]
        dd = [-jnp.sum(dhv[h] * (num[h] / nstab[h]), axis=1, keepdims=True) / nstab[h]
              * jnp.where(jnp.abs(den[h]) > floor[h], jnp.sign(den[h]), 0.0) for h in H]
        yield
        dsqk = [(_dot_nt(gh[h], v[h]) + dd[h]) * p[h] for h in H]
        qi = [q[h] * inter[h] for h in H]
        kw = [k[h] * kscale[h] for h in H]
        yield
        ghc = [_dot_nn(gh[h], ct[h]) for h in H]
        yield
        vet = [_dot_nn(v[h], et[h]) for h in H]
        yield
        dq = [_dot_nn(dsqk[h], k[h]) + inter[h] * (ghc[h] + dd[h] * n[h]) for h in H]
        yield
        dk = [_dot_tn(dsqk[h], q[h]) + kscale[h] * (vet[h] + en[h]) for h in H]
        yield
        dv = [_dot(sc[h], gh[h], _DIMS["tn"]) + _dot(kw[h], et[h], _DIMS["nt"]) for h in H]
        yield
        e_new = [cs[h] * et[h] + _dot_tn(gh[h], qi[h]) for h in H]
        en_new = [cs[h] * en[h] + jnp.sum(qi[h] * dd[h], axis=0, keepdims=True) for h in H]
        yield
        di_col = [jnp.sum(k[h] * dk[h], axis=1, keepdims=True) for h in H]
        df_col = [jnp.sum(q[h] * dq[h], axis=1, keepdims=True) - di_col[h] for h in H]
        di_row = [jnp.sum(jnp.where(eye, di_col[h], 0.0), axis=0, keepdims=True) for h in H]
        dlf_row = [jnp.sum(jnp.where(cm, df_col[h], 0.0), axis=0, keepdims=True) + carry[h] for h in H]
        dxf_row = [dlf_row[h] * (1.0 - _sigmoid(xf_row[h])) for h in H]
        yield
        for h in H:
            sl = sls[h]
            slk = slice(wm + h * dm, wm + (h + 1) * dm)
            e_ref[h] = e_new[h]
            en_ref[h] = en_new[h]
            fs_ref[h] = jnp.broadcast_to(carry[h] + jnp.sum(df_col[h], axis=0, keepdims=True), (1, LANES))
            dgr_ref[0, h:h + 1, :] = di_row[h]
            dgr_ref[0, nhm + h:nhm + h + 1, :] = dxf_row[h]
            dgs_ref[h:h + 1, :] += jnp.broadcast_to(jnp.sum(di_row[h], axis=1, keepdims=True), (1, LANES))
            dgs_ref[nhm + h:nhm + h + 1, :] += jnp.broadcast_to(
                jnp.sum(dxf_row[h], axis=1, keepdims=True), (1, LANES))
            dqk_ref[:, sl] = (dq[h] * (dm ** -0.5)).astype(BF16)
            dqk_ref[:, slk] = dk[h].astype(BF16)
            dv_ref[:, sl] = dv[h].astype(BF16)

    return dict(
        body=body, grid=(nch,), args=[qk, qk, z, gc, gr, br, bcl, dh] + list(states),
        in_specs=[row(wm, 0), row(wm, 1), row(wm, vcol), row(LANES, 0), gates_row, bias_row, bias_col,
                  row(wm, 0), st_c, st_n, st_m],
        out_shape=[jax.ShapeDtypeStruct((T, 2 * wm), BF16), jax.ShapeDtypeStruct((T, wm), BF16),
                   jax.ShapeDtypeStruct((nch, 2 * nhm, CHUNK), F32),
                   jax.ShapeDtypeStruct((2 * nhm, LANES), F32)],
        out_specs=[row(2 * wm, 0), row(wm, 0),
                   pl.BlockSpec((1, 2 * nhm, CHUNK), lambda i: (ci(i), 0, 0)),
                   pl.BlockSpec((2 * nhm, LANES), lambda i: (0, 0))],
        scratch_shapes=[pltpu.VMEM((nhm, dm, dm), F32), pltpu.VMEM((nhm, 1, dm), F32),
                        pltpu.VMEM((nhm, 1, LANES), F32)])


def _conv_taps(x, w_ref):
    T = x.shape[0]
    t = lax.broadcasted_iota(jnp.int32, x.shape, 0)
    taps = []
    acc = None
    for j in range(CONV_WIDTH):
        s = CONV_WIDTH // 2 - j
        if s == 0:
            xs = x
        else:
            xs = jnp.where((t - s >= 0) & (t - s < T), pltpu.roll(x, s % T, 0), 0.0)
        taps.append(xs)
        term = w_ref[j:j + 1, :] * xs
        acc = term if acc is None else acc + term
    return taps, acc


def _conv_fwd(z, w, b, *, col0, tc=LANES, name):
    T = z.shape[0]
    C2 = w.shape[1]
    assert col0 % tc == 0 and C2 % tc == 0

    def body(z_ref, w_ref, b_ref, o_ref):
        _, acc = _conv_taps(z_ref[...], w_ref)
        c = acc + b_ref[...]
        o_ref[...] = c * _sigmoid(c)

    return pl.pallas_call(
        body, name=name, out_shape=jax.ShapeDtypeStruct((T, C2), F32),
        grid=(C2 // tc,),
        in_specs=[pl.BlockSpec((T, tc), lambda j: (0, col0 // tc + j)),
                  pl.BlockSpec((CONV_WIDTH, tc), lambda j: (0, j)),
                  pl.BlockSpec((1, tc), lambda j: (0, j))],
        out_specs=pl.BlockSpec((T, tc), lambda j: (0, j)),
        compiler_params=_params("parallel"),
    )(z, w, b)


def _conv_bwd(dy_fw, dy_bw, z, w, b, *, col0, tc=LANES, name):
    T = z.shape[0]
    C2 = w.shape[1]

    def body(dyf_ref, dyb_ref, z_ref, w_ref, b_ref, dx_ref, dw_ref, db_ref):
        taps, acc = _conv_taps(z_ref[...], w_ref)
        c = acc + b_ref[...]
        sg = _sigmoid(c)
        dc = (dyf_ref[...].astype(F32) + dyb_ref[...].astype(F32)) * sg * (1.0 + c * (1.0 - sg))
        t = lax.broadcasted_iota(jnp.int32, dc.shape, 0)
        dx = None
        for j in range(CONV_WIDTH):
            s = j - CONV_WIDTH // 2
            if s == 0:
                ds = dc
            else:
                ds = jnp.where((t - s >= 0) & (t - s < T), pltpu.roll(dc, s % T, 0), 0.0)
            term = w_ref[j:j + 1, :] * ds
            dx = term if dx is None else dx + term
            dw_ref[j:j + 1, :] = jnp.sum(dc * taps[j], axis=0, keepdims=True)
        dx_ref[...] = dx.astype(BF16)
        db_ref[...] = jnp.sum(dc, axis=0, keepdims=True)

    col = pl.BlockSpec((T, tc), lambda j: (0, j))
    wsp = pl.BlockSpec((CONV_WIDTH, tc), lambda j: (0, j))
    bsp = pl.BlockSpec((1, tc), lambda j: (0, j))
    return pl.pallas_call(
        body, name=name,
        out_shape=[jax.ShapeDtypeStruct((T, C2), BF16), jax.ShapeDtypeStruct((CONV_WIDTH, C2), F32),
                   jax.ShapeDtypeStruct((1, C2), F32)],
        grid=(C2 // tc,),
        in_specs=[col, col, pl.BlockSpec((T, tc), lambda j: (0, col0 // tc + j)), wsp, bsp],
        out_specs=[col, wsp, bsp],
        compiler_params=_params("parallel"),
    )(dy_fw, dy_bw, z, w, b)


def _mix_heads(o_fw, o_bw, h_fw, h_bw, wh, wm, nhm):
    out = []
    hd = HGRN_HEAD_DIM
    for h in range(wh // hd):
        sl = slice(h * hd, (h + 1) * hd)
        o = o_fw[:, sl] + o_bw[:, sl]
        r = lax.rsqrt(jnp.mean(o * o, axis=-1, keepdims=True) + NORM_EPS)
        out.append((0, sl, o * r, r))
    dm = wm // nhm
    for h in range(nhm):
        sl = slice(h * dm, (h + 1) * dm)
        x = h_fw[:, sl] + h_bw[:, sl]
        xc = x - jnp.mean(x, axis=-1, keepdims=True)
        r = lax.rsqrt(jnp.mean(xc * xc, axis=-1, keepdims=True) + NORM_EPS)
        out.append((1, sl, xc * r, r))
    return out


def _mix_specs(T, wh, wm, tm, gcol, ocol):
    rowh = pl.BlockSpec((tm, wh), lambda i: (i, 0))
    rowm = pl.BlockSpec((tm, wm), lambda i: (i, 0))
    hg = pl.BlockSpec((tm, wh), lambda i: (i, gcol))
    mo = pl.BlockSpec((tm, wm), lambda i: (i, ocol))
    vh = pl.BlockSpec((1, wh), lambda i: (0, 0))
    vm = pl.BlockSpec((1, wm), lambda i: (0, 0))
    return rowh, rowm, hg, mo, vh, vm


def _mix_fwd(o_fw, o_bw, h_fw, h_bw, z, gh, gm, *, nhm, gcol, ocol, tm=256, name):
    T, wh = o_fw.shape
    wm = h_fw.shape[1]
    tm = min(tm, T)
    rowh, rowm, hg, mo, vh, vm = _mix_specs(T, wh, wm, tm, gcol, ocol)

    def body(of_ref, ob_ref, hf_ref, hb_ref, hg_ref, mo_ref, gh_ref, gm_ref, y_ref, yt_ref):
        heads = _mix_heads(of_ref[...], ob_ref[...], hf_ref[...], hb_ref[...], wh, wm, nhm)
        for grp, sl, nv, _ in heads:
            if grp == 0:
                gate = hg_ref[:, sl]
                gate = gate * _sigmoid(gate)
                y = nv * gh_ref[:, sl] * gate
                osl = sl
            else:
                y = nv * gm_ref[:, sl] * _sigmoid(mo_ref[:, sl])
                osl = slice(wh + sl.start, wh + sl.stop)
            y_ref[:, osl] = y.astype(BF16)
            yt_ref[osl, :] = y.T.astype(BF16)

    return pl.pallas_call(
        body, name=name,
        out_shape=[jax.ShapeDtypeStruct((T, wh + wm), BF16), jax.ShapeDtypeStruct((wh + wm, T), BF16)],
        grid=(T // tm,),
        in_specs=[rowh, rowh, rowm, rowm, hg, mo, vh, vm],
        out_specs=[pl.BlockSpec((tm, wh + wm), lambda i: (i, 0)),
                   pl.BlockSpec((wh + wm, tm), lambda i: (0, i))],
        compiler_params=_params("parallel"),
    )(o_fw, o_bw, h_fw, h_bw, z, z, gh, gm)


def _mix_bwd(dy, o_fw, o_bw, h_fw, h_bw, z, gh, gm, *, nhm, gcol, ocol, tm=256, name, comm=None):
    T, wh = o_fw.shape
    wm = h_fw.shape[1]
    tm = min(tm, T)
    rowh, rowm, hg, mo, vh, vm = _mix_specs(T, wh, wm, tm, gcol, ocol)

    def body(dy_ref, of_ref, ob_ref, hf_ref, hb_ref, hg_ref, mo_ref, gh_ref, gm_ref,
             do_ref, dh_ref, dhg_ref, dmo_ref, dgh_ref, dgm_ref):
        @pl.when(pl.program_id(0) == 0)
        def _():
            dgh_ref[...] = jnp.zeros_like(dgh_ref)
            dgm_ref[...] = jnp.zeros_like(dgm_ref)

        heads = _mix_heads(of_ref[...], ob_ref[...], hf_ref[...], hb_ref[...], wh, wm, nhm)
        for grp, sl, nv, r in heads:
            if grp == 0:
                d = dy_ref[:, sl]
                x = hg_ref[:, sl]
                sg = _sigmoid(x)
                gate = x * sg
                g = gh_ref[:, sl]
                dgh_ref[:, sl] += jnp.sum(d * nv * gate, axis=0, keepdims=True)
                dhg_ref[:, sl] = (d * nv * g * sg * (1.0 + x * (1.0 - sg))).astype(BF16)
                t = d * g * gate
                do_ref[:, sl] = r * (t - nv * jnp.mean(t * nv, axis=-1, keepdims=True))
            else:
                d = dy_ref[:, slice(wh + sl.start, wh + sl.stop)]
                sg = _sigmoid(mo_ref[:, sl])
                g = gm_ref[:, sl]
                dgm_ref[:, sl] += jnp.sum(d * nv * sg, axis=0, keepdims=True)
                dmo_ref[:, sl] = (d * nv * g * sg * (1.0 - sg)).astype(BF16)
                t = d * g * sg
                dh_ref[:, sl] = r * (t - jnp.mean(t, axis=-1, keepdims=True)
                                     - nv * jnp.mean(t * nv, axis=-1, keepdims=True))

    return _pcall(
        body, name=name, comm=comm,
        out_shape=[jax.ShapeDtypeStruct((T, wh), F32), jax.ShapeDtypeStruct((T, wm), F32),
                   jax.ShapeDtypeStruct((T, wh), BF16), jax.ShapeDtypeStruct((T, wm), BF16),
                   jax.ShapeDtypeStruct((1, wh), F32), jax.ShapeDtypeStruct((1, wm), F32)],
        grid=(T // tm,),
        in_specs=[pl.BlockSpec((tm, wh + wm), lambda i: (i, 0)), rowh, rowh, rowm, rowm, hg, mo, vh, vm],
        out_specs=[rowh, rowm, rowh, rowm, vh, vm],
        sem=("arbitrary",),
    )(dy, o_fw, o_bw, h_fw, h_bw, z, z, gh, gm)


def _pick(n, pref):
    for c in range(pref - pref % LANES, 0, -LANES):
        if n % c == 0:
            return c
    return n


def _mm(a, b, mode="nn", *, tm=1024, tn=1024, tk=2048, **kw):
    if mode == "nn":
        (M, K), N = a.shape, b.shape[1]
    else:
        (M, K), N = a.shape, kw.get("rows") or b.shape[0]
    return _matmul(a, b, mode=mode, tm=_pick(M, tm), tn=_pick(N, tn), tk=_pick(K, tk), **kw)


def _device_step(x, target, sh, small_parts, small_params, core):
    T, D = x.shape

    def full(g):
        return g.reshape(-1, D)

    def shards(g):
        return g.reshape(N_DEV, -1, D)

    def pair_add(name, g, land):
        return _pair_add(g, land, core, "rs_add_" + name)

    def dw(lhs_t, rhs, name, scale=1.0):
        return shards(_mm(lhs_t, rhs, out_dtype=BF16, scale=scale, tm=704, tk=T, name=name))

    g_w1a, g_small = _comm_call(_ag_job([sh["ffn1_w1"], small_parts]), "ag_ffn1_w1")
    w1a = full(g_w1a)
    P = small_params(g_small)
    wh = P["hgrn_norm_g"].shape[1]
    wm = P["mlstm_norm_g"].shape[1]
    nhm = P["ig_b"].shape[1]
    ng = 4 * nhm
    nch = T // CHUNK
    assert wh == wm and T % CHUNK == 0
    vcol, ocol, gcol = 7, 8, 2
    col0 = 5 * wh
    nz = 5 * wh + 4 * wm
    xb = _cast(x, name="x_cast")
    a1f, (g_w3a,) = _mm(xb, w1a, "nt", name="ffn1_up_a", comm=_ag_job([sh["ffn1_w3"]]))
    w3a = full(g_w3a)
    cut = sh["w_in"].shape[0] // 2 // 16 * 16
    (a1, b1, h1, h1t), (g_w2a, g_wtop) = _ffn_up(xb, None, w3a, a=a1f, name="ffn1_up_b",
                                                 comm=_ag_job([sh["ffn1_w2"], sh["w_in"][:cut]]))
    w2a = full(g_w2a)
    r1, (g_wbot,) = _mm(h1, w2a, add=x, scale=0.5, add_scale=DN_ALPHA, tk=w2a.shape[0] // 4,
                        name="ffn1_down", comm=_ag_job([sh["w_in"][cut:]]))
    w_int = full(jnp.concatenate([g_wtop, g_wbot], axis=1))
    w_wgt = jnp.zeros((LANES, D), BF16).at[:ng].set(w_int[nz:])
    x1, x1b = _ln_fwd(r1, P["ln1_g"], P["ln1_b"], name="ln1")
    zm, (g_wout,) = _mm(x1b, w_int, "nt", rows=nz, name="zm", comm=_ag_job([sh["w_out"]]))
    w_wout = full(g_wout)
    zg = _mm(x1b, w_wgt, "nt", name="zg")
    gr = zg[:, :ng].reshape(nch, CHUNK, ng).transpose(0, 2, 1)
    bias = jnp.concatenate([P["ig_b"].reshape(-1), P["fg_b"].reshape(-1)])
    br = jnp.zeros((1, LANES), F32).at[0, :ng].set(bias)
    bcl = bias.reshape(ng, 1)
    lbf, lbb = P["lb"][0:1], P["lb"][1:2]
    ((o_fw, s_fw), (o_bw, s_bw)), (g_w1b,) = _run_specs(
        [_hgrn_fwd(zm, lbf, wh=wh, reverse=False), _hgrn_fwd(zm, lbb, wh=wh, reverse=True)],
        name="hgrn_fwd", comm=_ag_job([sh["ffn2_w1"]]))
    qk = _conv_fwd(zm, P["conv_w"], P["conv_b"], col0=col0, name="conv")
    mkw = dict(wm=wm, nhm=nhm, vcol=vcol)
    ((h_fw, *st_fw),), (g_w3b,) = _run_specs([_mlstm_fwd(qk, zm, zg, gr, br, bcl, reverse=False, **mkw)],
                                             name="mlstm_fw", comm=_ag_job([sh["ffn2_w3"]]))
    ((h_bw, *st_bw),) = _run_specs([_mlstm_fwd(qk, zm, zg, gr, br, bcl, reverse=True, **mkw)],
                                   name="mlstm_bw")
    w1b, w3b = full(g_w1b), full(g_w3b)
    y, yt = _mix_fwd(o_fw, o_bw, h_fw, h_bw, zm, P["hgrn_norm_g"], P["mlstm_norm_g"],
                     nhm=nhm, gcol=gcol, ocol=ocol, name="mix")
    r2 = _mm(y, w_wout, add=x1, add_scale=DN_ALPHA, name="r2")
    x2, x2b = _ln_fwd(r2, P["ln2_g"], P["ln2_b"], name="ln2")
    (a2, b2, h2, h2t), (g_w2b,) = _ffn_up(x2b, w1b, w3b, name="ffn2_up", comm=_ag_job([sh["ffn2_w2"]]))
    w2b = full(g_w2b)
    r3 = _mm(h2, w2b, add=x2, scale=0.5, add_scale=DN_ALPHA, tk=w2b.shape[0] // 4, name="ffn2_down")

    G, land = {}, {}
    dr3, dr3b, G["ln3_g"], G["ln3_b"], loss = _ln_bwd(
        None, r3, P["ln3_g"], b=P["ln3_b"], target=target, name="ln3_bwd")
    da2, db2, da2t, db2t = _ffn_bwd_act(dr3b, w2b, a2, b2, name="ffn2_dact")
    gw1b, gw3b = dw(da2t, x2b, "ffn2_dw1"), dw(db2t, x2b, "ffn2_dw3")
    gw2b = dw(h2t, dr3b, "ffn2_dw2", scale=0.5)
    dx2, l1 = _ffn_dx(da2, db2, w1b, w3b, dr3, name="ffn2_dx", comm=_pair_job([gw1b, gw3b, gw2b]))
    p_w1b, p_w3b, p_w2b = [pair_add(k, g, l) for k, g, l in
                           zip(("ffn2_w1", "ffn2_w3", "ffn2_w2"), (gw1b, gw3b, gw2b), l1)]
    dr2, dr2b, G["ln2_g"], G["ln2_b"] = _ln_bwd(dx2, r2, P["ln2_g"], name="ln2_bwd")
    dy = _mm(dr2b, w_wout, "nt", name="dy")
    gwout = shards(_mm(yt, dr2b, out_dtype=BF16, name="dwout"))
    (do, dh, dhg, dmo, G["hgrn_norm_g"], G["mlstm_norm_g"]), (l1,) = _mix_bwd(
        dy, o_fw, o_bw, h_fw, h_bw, zm, P["hgrn_norm_g"], P["mlstm_norm_g"],
        nhm=nhm, gcol=gcol, ocol=ocol, name="mix_bwd", comm=_pair_job([gwout]))
    p_wout = pair_add("w_out", gwout, l1)
    ((dqk_f, dv_f, dgr_f, dgs_f),), (land["ffn2_w1"],) = _run_specs(
        [_mlstm_bwd(qk, zm, zg, gr, br, bcl, dh, st_fw, reverse=False, **mkw)],
        name="mlstm_fw_bwd", comm=_chips_job([p_w1b]))
    ((dqk_b, dv_b, dgr_b, dgs_b),), (land["ffn2_w3"],) = _run_specs(
        [_mlstm_bwd(qk, zm, zg, gr, br, bcl, dh, st_bw, reverse=True, **mkw)],
        name="mlstm_bw_bwd", comm=_chips_job([p_w3b]))
    dmqk, G["conv_w"], G["conv_b"] = _conv_bwd(dqk_f, dqk_b, zm, P["conv_w"], P["conv_b"], col0=col0,
                                               name="conv_bwd")
    ((dhq_f, dhi_f, dhf_fw, dlb_f), (dhq_b, dhi_b, dhf_bw, dlb_b)), (land["ffn2_w2"], land["w_out"]) = \
        _run_specs([_hgrn_bwd(zm, lbf, do, s_fw, wh=wh, reverse=False),
                    _hgrn_bwd(zm, lbb, do, s_bw, wh=wh, reverse=True)],
                   name="hgrn_bwd", comm=_chips_job([p_w2b, p_wout]))
    G["lb"] = jnp.concatenate([dlb_f, dlb_b], axis=0)
    G["ig_b"] = jnp.stack([dgs_f[:nhm, 0], dgs_b[:nhm, 0]])
    G["fg_b"] = jnp.stack([dgs_f[nhm:, 0], dgs_b[nhm:, 0]])
    dzm, dzmt = _dz_pack([(dhq_f, dhq_b), (dhi_f, dhi_b), dhg, dhf_fw, dhf_bw, dmqk, (dv_f, dv_b), dmo],
                         name="dz_pack")
    dgr = jnp.concatenate([dgr_f[:, :nhm], dgr_b[:, :nhm], dgr_f[:, nhm:], dgr_b[:, nhm:]], axis=1)
    dzgt = jnp.zeros((LANES, T), F32).at[:ng].set(dgr.transpose(1, 0, 2).reshape(ng, T)).astype(BF16)
    dzg = jnp.zeros((T, LANES), F32).at[:, :ng].set(dgr.transpose(0, 2, 1).reshape(T, ng)).astype(BF16)
    gwin = shards(jnp.concatenate([_mm(dzmt, x1b, out_dtype=BF16, tk=T, name="dwin"),
                                   _mm(dzgt, x1b, out_dtype=BF16, tk=T, name="dwg")[:ng]], axis=0))
    t, (l1,) = _mm(dzg, w_wgt, add=dr2, add_scale=DN_ALPHA, name="dx1_g", comm=_pair_job([gwin]))
    p_win = pair_add("w_in", gwin, l1)
    dx1, (land["w_in"],) = _mm(dzm, w_int, rows=nz, add=t, tk=1536, name="dx1", comm=_chips_job([p_win]))
    dr1, dr1b, G["ln1_g"], G["ln1_b"] = _ln_bwd(dx1, r1, P["ln1_g"], name="ln1_bwd")
    gw2a = dw(h1t, dr1b, "ffn1_dw2", scale=0.5)
    (da1, db1, da1t, db1t), (l1,) = _ffn_bwd_act(dr1b, w2a, a1, b1, name="ffn1_dact",
                                                 comm=_pair_job([gw2a]))
    p_w2a = pair_add("ffn1_w2", gw2a, l1)
    gw1a, (land["ffn1_w2"],) = _mm(da1t, xb, out_dtype=BF16, tm=704, tk=T, name="ffn1_dw1",
                                   comm=_chips_job([p_w2a]))
    gw1a = shards(gw1a)
    gw3a, (l1,) = _mm(db1t, xb, out_dtype=BF16, tm=704, tk=T, name="ffn1_dw3", comm=_pair_job([gw1a]))
    gw3a = shards(gw3a)
    p_w1a = pair_add("ffn1_w1", gw1a, l1)
    (l1,) = _comm_call(_pair_job([gw3a]), "rs_pair_ffn1_w3")
    p_w3a = pair_add("ffn1_w3", gw3a, l1)
    gx, (land["ffn1_w1"], land["ffn1_w3"]) = _ffn_dx(
        da1, db1, w1a, w3a, dr1, name="ffn1_dx", comm=_chips_job([p_w1a, p_w3a]))
    return loss, gx, land, G


ANY = pl.BlockSpec(memory_space=pl.ANY)


def _place():
    x, y, c = lax.axis_index("x"), lax.axis_index("y"), lax.axis_index("c")
    chips = [(1 - x, y), (x, 1 - y), (1 - x, 1 - y)]
    return x, y, c, chips


def _comm_call(job, name):
    n_in, n_out = len(job.inputs), len(job.out_shape)

    def body(*refs):
        ins, outs, sems = refs[:n_in], refs[n_in:n_in + n_out], refs[n_in + n_out:]
        job.start(ins, outs, sems)
        job.middle(ins, outs, sems)
        job.finish(ins, outs, sems)

    return pl.pallas_call(body, name=name, out_shape=job.out_shape, in_specs=[ANY] * n_in,
                          out_specs=[ANY] * n_out, scratch_shapes=job.sems)(*job.inputs)


def _ag_job(parts):
    n = len(parts)
    halves = [p.shape[0] // 2 // 16 * 16 or p.shape[0] // 2 // 8 * 8 or p.shape[0] for p in parts]
    cut = [h < p.shape[0] for h, p in zip(halves, parts)]

    def helpers(ins, outs, sems):
        send_sems, recv_sems, local_sems = sems
        x, y, c, _ = _place()
        me, sibling = (x, y, c), (x, y, 1 - c)
        xn, yn, dg = (1 - x, y), (x, 1 - y), (1 - x, 1 - y)

        def slot(a, block, part=None):
            bx, by, bc = block
            ref = outs[a].at[4 * bx + 2 * by + bc]
            if part == 0:
                return ref.at[pl.ds(0, halves[a])]
            if part == 1:
                return ref.at[pl.ds(halves[a], parts[a].shape[0] - halves[a])]
            return ref

        def copy(a, k, block, to, src=None, part=None):
            dst = slot(a, block, part)
            return pltpu.make_async_remote_copy(
                src_ref=dst if src is None else src, dst_ref=dst,
                send_sem=send_sems.at[a, k], recv_sem=recv_sems.at[a, k],
                device_id=to, device_id_type=MESH)

        local = [pltpu.make_async_copy(ins[a], slot(a, me), local_sems.at[a]) for a in range(n)]
        return copy, local, me, sibling, xn, yn, dg, c

    def start(ins, outs, sems):
        copy, local, me, sibling, xn, yn, dg, c = helpers(ins, outs, sems)
        for cp in local:
            cp.start()
        for a in range(n):
            copy(a, 0, me, sibling, src=ins[a]).start()
            copy(a, 1, me, (*xn, c), src=ins[a]).start()
            copy(a, 2, me, (*yn, c), src=ins[a]).start()

    def middle(ins, outs, sems):
        copy, local, me, sibling, xn, yn, dg, c = helpers(ins, outs, sems)
        for a in range(n):
            copy(a, 1, (*xn, c), me).wait_recv()
            copy(a, 3, (*xn, c), (*yn, c), part=0).start()
            copy(a, 5, (*xn, c), sibling).start()
        for a in range(n):
            copy(a, 2, (*yn, c), me).wait_recv()
            if cut[a]:
                copy(a, 4, (*yn, c), (*xn, c), part=1).start()
            copy(a, 6, (*yn, c), sibling).start()

    def finish(ins, outs, sems):
        copy, local, me, sibling, xn, yn, dg, c = helpers(ins, outs, sems)
        for a in range(n):
            copy(a, 3, (*dg, c), me, part=0).wait_recv()
            if cut[a]:
                copy(a, 4, (*dg, c), me, part=1).wait_recv()
            copy(a, 7, (*dg, c), sibling).start()
        for a in range(n):
            copy(a, 0, sibling, me).wait_recv()
            copy(a, 5, (*xn, 1 - c), me).wait_recv()
            copy(a, 6, (*yn, 1 - c), me).wait_recv()
            copy(a, 7, (*dg, 1 - c), me).wait_recv()
        for a in range(n):
            copy(a, 0, me, sibling, src=ins[a]).wait_send()
            copy(a, 1, me, (*xn, c), src=ins[a]).wait_send()
            copy(a, 2, me, (*yn, c), src=ins[a]).wait_send()
            copy(a, 3, (*xn, c), (*yn, c), part=0).wait_send()
            if cut[a]:
                copy(a, 4, (*yn, c), (*xn, c), part=1).wait_send()
            for k, blk in ((5, xn), (6, yn), (7, dg)):
                copy(a, k, (*blk, c), sibling).wait_send()
        for cp in local:
            cp.wait()

    return _Job(parts, [jax.ShapeDtypeStruct((N_DEV,) + p.shape, p.dtype) for p in parts],
                [pltpu.SemaphoreType.DMA((n, 8)), pltpu.SemaphoreType.DMA((n, 8)),
                 pltpu.SemaphoreType.DMA((n,))], start, finish, middle)


def _pair_job(gs):
    n = len(gs)

    def copies(g_refs, land_refs, sems):
        send_sems, recv_sems = sems
        x, y, c, _ = _place()
        return [pltpu.make_async_remote_copy(
            src_ref=g_refs[a].at[2 * k + 1 - c], dst_ref=land_refs[a].at[k],
            send_sem=send_sems.at[a, k], recv_sem=recv_sems.at[a, k],
            device_id=(x, y, 1 - c), device_id_type=MESH) for a in range(n) for k in range(4)]

    def start(ins, outs, sems):
        for cp in copies(ins, outs, sems):
            cp.start()

    def finish(ins, outs, sems):
        for cp in copies(ins, outs, sems):
            cp.wait()

    return _Job(gs, [jax.ShapeDtypeStruct((4,) + g.shape[1:], g.dtype) for g in gs],
                [pltpu.SemaphoreType.DMA((n, 4)), pltpu.SemaphoreType.DMA((n, 4))], start, finish)


def _chips_job(ps):
    n = len(ps)

    def copies(p_refs, land_refs, sems):
        send_sems, recv_sems, local_sems = sems
        x, y, c, chips = _place()
        mine = 2 * x + y
        owns = [pltpu.make_async_copy(p_refs[a].at[mine], land_refs[a].at[mine], local_sems.at[a])
                for a in range(n)]
        sends = [pltpu.make_async_remote_copy(
            src_ref=p_refs[a].at[2 * chip[0] + chip[1]], dst_ref=land_refs[a].at[mine],
            send_sem=send_sems.at[a, j], recv_sem=recv_sems.at[a, j],
            device_id=(*chip, c), device_id_type=MESH) for a in range(n) for j, chip in enumerate(chips)]
        recvs = [pltpu.make_async_remote_copy(
            src_ref=p_refs[a].at[mine], dst_ref=land_refs[a].at[2 * chip[0] + chip[1]],
            send_sem=send_sems.at[a, j], recv_sem=recv_sems.at[a, j],
            device_id=(*chip, c), device_id_type=MESH) for a in range(n) for j, chip in enumerate(chips)]
        return owns, sends, recvs

    def start(ins, outs, sems):
        owns, sends, _ = copies(ins, outs, sems)
        for cp in owns + sends:
            cp.start()

    def finish(ins, outs, sems):
        owns, sends, recvs = copies(ins, outs, sems)
        for cp in recvs:
            cp.wait_recv()
        for cp in sends:
            cp.wait_send()
        for cp in owns:
            cp.wait()

    return _Job(ps, [jax.ShapeDtypeStruct(p.shape, p.dtype) for p in ps],
                [pltpu.SemaphoreType.DMA((n, 3)), pltpu.SemaphoreType.DMA((n, 3)),
                 pltpu.SemaphoreType.DMA((n,))], start, finish)


def _all_reduce_small(buf, name):
    rows = buf.shape[0]

    def body(b_ref, o_ref, slots, send_sems, recv_sems):
        x, y, c, _ = _place()
        me = 4 * x + 2 * y + c
        slots[me] = b_ref[...]
        cps = []
        for k in range(1, N_DEV):
            fx, fy, fc = (k >> 2) & 1, (k >> 1) & 1, k & 1
            peer = (x ^ fx, y ^ fy, c ^ fc)
            cps.append(pltpu.make_async_remote_copy(
                src_ref=b_ref, dst_ref=slots.at[me],
                send_sem=send_sems.at[k - 1], recv_sem=recv_sems.at[k - 1],
                device_id=peer, device_id_type=MESH))
        for cp in cps:
            cp.start()
        for k in range(1, N_DEV):
            fx, fy, fc = (k >> 2) & 1, (k >> 1) & 1, k & 1
            src = 4 * (x ^ fx) + 2 * (y ^ fy) + (c ^ fc)
            pltpu.make_async_remote_copy(
                src_ref=b_ref, dst_ref=slots.at[src],
                send_sem=send_sems.at[k - 1], recv_sem=recv_sems.at[k - 1],
                device_id=(x ^ fx, y ^ fy, c ^ fc), device_id_type=MESH).wait_recv()
        for cp in cps:
            cp.wait_send()
        acc = slots[0]
        for k in range(1, N_DEV):
            acc = acc + slots[k]
        o_ref[...] = acc

    vm = pl.BlockSpec(memory_space=pltpu.VMEM)
    return pl.pallas_call(
        body, name=name, out_shape=jax.ShapeDtypeStruct(buf.shape, F32),
        in_specs=[vm], out_specs=vm,
        scratch_shapes=[pltpu.VMEM((N_DEV, rows, LANES), F32),
                        pltpu.SemaphoreType.DMA((N_DEV - 1,)), pltpu.SemaphoreType.DMA((N_DEV - 1,))],
    )(buf)


def _row_tile(rows, pref=2048):
    for t in range(min(pref, rows) - min(pref, rows) % 8, 0, -8):
        if rows % t == 0:
            return t
    return rows


def _tile2(rows, cols, nbuf):
    budget = VMEM_LIMIT // 2 // (2 * nbuf * 4)
    for t in range(min(512, rows) // 16 * 16, 0, -16):
        if rows % t == 0 and t * cols <= budget:
            return t, cols
    return rows, _pick(cols, max(LANES, budget // rows // LANES * LANES))


def _pair_add(g, land, core, name):
    _, r, c = g.shape
    tr, tc = _tile2(r, c, 3)

    def body(core_ref, g_ref, l_ref, o_ref):
        o_ref[...] = (g_ref[...].astype(F32) + l_ref[...].astype(F32)).astype(o_ref.dtype)

    blk = pl.BlockSpec((1, tr, tc), lambda k, i, j, cr: (k, i, j))
    return pl.pallas_call(
        body, name=name, out_shape=jax.ShapeDtypeStruct(land.shape, land.dtype),
        grid_spec=pltpu.PrefetchScalarGridSpec(
            num_scalar_prefetch=1, grid=(4, r // tr, c // tc),
            in_specs=[pl.BlockSpec((1, tr, tc), lambda k, i, j, cr: (2 * k + cr[0], i, j)), blk],
            out_specs=blk),
        compiler_params=_params("parallel", "parallel", "parallel"),
    )(core, g, land)


def _adamw(w, parts, m, v, name):
    shp = w.shape
    cols = shp[-1]
    w2, m2, v2 = (t.reshape(-1, cols) for t in (w, m, v))
    rows = w2.shape[0]
    n = parts.shape[0]
    assert parts.shape[1:] == (rows, cols), (parts.shape, shp)
    tr, tc = _tile2(rows, cols, n + 7)
    c1 = 1.0 / (1.0 - ADAM_B1 ** ADAM_STEP)
    c2 = 1.0 / (1.0 - ADAM_B2 ** ADAM_STEP)

    def body(*refs):
        p_refs = refs[:n]
        w_ref, m_ref, v_ref, g_ref, d_ref, nm_ref, nv_ref = refs[n:]
        gv = p_refs[0][0].astype(F32)
        for p_ref in p_refs[1:]:
            gv = gv + p_ref[0].astype(F32)
        nm = ADAM_B1 * m_ref[...] + (1.0 - ADAM_B1) * gv
        nv = ADAM_B2 * v_ref[...] + (1.0 - ADAM_B2) * (gv * gv)
        g_ref[...] = gv
        nm_ref[...] = nm
        nv_ref[...] = nv
        d_ref[...] = -ADAM_LR * ((nm * c1) / (jnp.sqrt(nv * c2) + ADAM_EPS) + ADAM_WD * w_ref[...])

    blk = pl.BlockSpec((tr, tc), lambda i, j: (i, j))
    slab = lambda k: pl.BlockSpec((1, tr, tc), lambda i, j: (k, i, j))
    out = jax.ShapeDtypeStruct((rows, cols), F32)
    outs = pl.pallas_call(
        body, name=name, out_shape=[out] * 4, grid=(rows // tr, cols // tc),
        in_specs=[slab(k) for k in range(n)] + [blk] * 3, out_specs=[blk] * 4,
        compiler_params=_params("parallel", "parallel"),
    )(*([parts] * n), w2, m2, v2)
    return tuple(o.reshape(shp) for o in outs)


def _pack_rows(arrs):
    rows = []
    for a in arrs:
        flat = a.reshape(-1).astype(F32)
        pad = (-flat.shape[0]) % LANES
        if pad:
            flat = jnp.concatenate([flat, jnp.zeros((pad,), F32)])
        rows.append(flat.reshape(-1, LANES))
    out = jnp.concatenate(rows, axis=0)
    pad = (-out.shape[0]) % 8
    if pad:
        out = jnp.concatenate([out, jnp.zeros((pad, LANES), F32)], axis=0)
    return out


def _unpack_rows(buf, shapes):
    outs, r = [], 0
    for s in shapes:
        n = math.prod(s)
        nr = -(-n // LANES)
        outs.append(buf[r:r + nr].reshape(-1)[:n].reshape(s))
        r += nr
    return outs


BIG = ("ffn1_w1", "ffn1_w3", "ffn1_w2", "w_in", "w_out", "ffn2_w1", "ffn2_w3", "ffn2_w2")
ROW_SHARDED = ("ffn1_w2", "w_out", "ffn2_w2")
SMALL = ("ln1_g", "ln1_b", "hgrn_lb", "hgrn_norm_g", "mlstm_conv_w", "mlstm_conv_b", "mlstm_ig_b",
         "mlstm_fg_b", "mlstm_norm_g", "ln2_g", "ln2_b", "ln3_g", "ln3_b")
WEIGHTS = ("ffn1_w1", "ffn1_w3", "ffn1_w2", "ln1_g", "ln1_b", "w_in", "hgrn_lb", "hgrn_norm_g",
           "mlstm_conv_w", "mlstm_conv_b", "mlstm_ig_b", "mlstm_fg_b", "mlstm_norm_g", "w_out",
           "ln2_g", "ln2_b", "ffn2_w1", "ffn2_w3", "ffn2_w2", "ln3_g", "ln3_b")


def kernel(x, ffn1_w1, ffn1_w3, ffn1_w2, ln1_g, ln1_b, w_in, hgrn_lb, hgrn_norm_g, mlstm_conv_w, mlstm_conv_b, mlstm_ig_b, mlstm_fg_b, mlstm_norm_g, w_out, ln2_g, ln2_b, ffn2_w1, ffn2_w3, ffn2_w2, ln3_g, ln3_b, loss_target, m_ffn1_w1, m_ffn1_w3, m_ffn1_w2, m_ln1_g, m_ln1_b, m_w_in, m_hgrn_lb, m_hgrn_norm_g, m_mlstm_conv_w, m_mlstm_conv_b, m_mlstm_ig_b, m_mlstm_fg_b, m_mlstm_norm_g, m_w_out, m_ln2_g, m_ln2_b, m_ffn2_w1, m_ffn2_w3, m_ffn2_w2, m_ln3_g, m_ln3_b, v_ffn1_w1, v_ffn1_w3, v_ffn1_w2, v_ln1_g, v_ln1_b, v_w_in, v_hgrn_lb, v_hgrn_norm_g, v_mlstm_conv_w, v_mlstm_conv_b, v_mlstm_ig_b, v_mlstm_fg_b, v_mlstm_norm_g, v_w_out, v_ln2_g, v_ln2_b, v_ffn2_w1, v_ffn2_w3, v_ffn2_w2, v_ln3_g, v_ln3_b):
    args = (ffn1_w1, ffn1_w3, ffn1_w2, ln1_g, ln1_b, w_in, hgrn_lb, hgrn_norm_g, mlstm_conv_w,
            mlstm_conv_b, mlstm_ig_b, mlstm_fg_b, mlstm_norm_g, w_out, ln2_g, ln2_b, ffn2_w1, ffn2_w3,
            ffn2_w2, ln3_g, ln3_b)
    ms = (m_ffn1_w1, m_ffn1_w3, m_ffn1_w2, m_ln1_g, m_ln1_b, m_w_in, m_hgrn_lb, m_hgrn_norm_g,
          m_mlstm_conv_w, m_mlstm_conv_b, m_mlstm_ig_b, m_mlstm_fg_b, m_mlstm_norm_g, m_w_out, m_ln2_g,
          m_ln2_b, m_ffn2_w1, m_ffn2_w3, m_ffn2_w2, m_ln3_g, m_ln3_b)
    vs = (v_ffn1_w1, v_ffn1_w3, v_ffn1_w2, v_ln1_g, v_ln1_b, v_w_in, v_hgrn_lb, v_hgrn_norm_g,
          v_mlstm_conv_w, v_mlstm_conv_b, v_mlstm_ig_b, v_mlstm_fg_b, v_mlstm_norm_g, v_w_out, v_ln2_g,
          v_ln2_b, v_ffn2_w1, v_ffn2_w3, v_ffn2_w2, v_ln3_g, v_ln3_b)
    w = dict(zip(WEIGHTS, args))
    m = dict(zip(WEIGHTS, ms))
    v = dict(zip(WEIGHTS, vs))
    core = lax.axis_index("c")
    dev = 4 * lax.axis_index("x") + 2 * lax.axis_index("y") + core
    p_lb = []

    def small_params(gathered_small):
        lb_sh, cw_sh = zip(*[_unpack_rows(gathered_small[i], [hgrn_lb.shape, mlstm_conv_w.shape])
                             for i in range(N_DEV)])
        hgrn_lb_full = jnp.concatenate(lb_sh, axis=-1)
        conv_w_full = jnp.concatenate(cw_sh, axis=-1)[0]
        p_lb.append(jax.nn.softmax(hgrn_lb_full, axis=1))
        return dict(ln1_g=ln1_g, ln1_b=ln1_b, ln2_g=ln2_g, ln2_b=ln2_b, ln3_g=ln3_g, ln3_b=ln3_b,
                    lb=p_lb[0][:, 0], hgrn_norm_g=hgrn_norm_g, conv_w=conv_w_full, conv_b=mlstm_conv_b,
                    ig_b=mlstm_ig_b[0], fg_b=mlstm_fg_b[0], mlstm_norm_g=mlstm_norm_g)

    def rows_first(t, k):
        return t[0] if k in ROW_SHARDED else t[0].T

    loss_row, grad_x, land2, G = _device_step(
        x[0], loss_target[0], {k: rows_first(w[k], k).astype(BF16) for k in BIG},
        _pack_rows([hgrn_lb, mlstm_conv_w]), small_params, jnp.reshape(core, (1,)).astype(jnp.int32))
    p_lb = p_lb[0]

    dlb = G["lb"]
    g_lb = jnp.stack([dlb * p_lb[:, 0] * (1.0 - p_lb[:, 0]), -dlb * p_lb[:, 0] * p_lb[:, 1]], axis=1)
    small_full = {"ln1_g": G["ln1_g"], "ln1_b": G["ln1_b"], "hgrn_lb": g_lb, "hgrn_norm_g": G["hgrn_norm_g"],
                  "mlstm_conv_w": G["conv_w"][None], "mlstm_conv_b": G["conv_b"],
                  "mlstm_ig_b": G["ig_b"][None], "mlstm_fg_b": G["fg_b"][None],
                  "mlstm_norm_g": G["mlstm_norm_g"], "ln2_g": G["ln2_g"], "ln2_b": G["ln2_b"],
                  "ln3_g": G["ln3_g"], "ln3_b": G["ln3_b"]}
    small_list = [small_full[k] for k in SMALL] + [loss_row]
    reduced = _all_reduce_small(_pack_rows(small_list), "all_reduce_small")
    red = _unpack_rows(reduced, [a.shape for a in small_list])
    loss = red[-1][0, 0]
    small_g = {}
    for k, gk in zip(SMALL, red[:-1]):
        if k in ("hgrn_lb", "mlstm_conv_w"):
            n = w[k].shape[-1]
            gk = lax.dynamic_slice_in_dim(gk, dev * n, n, axis=gk.ndim - 1)
        small_g[k] = gk

    grads, delta, new_m, new_v = {}, {}, {}, {}
    for k in BIG:
        res = _adamw(rows_first(w[k], k), land2[k], rows_first(m[k], k), rows_first(v[k], k), "adamw_" + k)
        grads[k], delta[k], new_m[k], new_v[k] = [(r if k in ROW_SHARDED else r.T)[None] for r in res]
    sm = _adamw(_pack_rows([w[k] for k in SMALL]), _pack_rows([small_g[k] for k in SMALL])[None],
                _pack_rows([m[k] for k in SMALL]), _pack_rows([v[k] for k in SMALL]), "adamw_small")
    shapes = [w[k].shape for k in SMALL]
    for dst, buf in zip((grads, delta, new_m, new_v), sm):
        for k, val in zip(SMALL, _unpack_rows(buf, shapes)):
            dst[k] = val
    return (loss, grad_x[None], *[grads[k] for k in WEIGHTS], *[delta[k] for k in WEIGHTS],
            *[new_m[k] for k in WEIGHTS], *[new_v[k] for k in WEIGHTS])
```

```python
import math

import jax
import jax.numpy as jnp
from jax import lax
from jax.experimental import pallas as pl
from jax.experimental.pallas import tpu as pltpu

F32 = jnp.float32
BF16 = jnp.bfloat16

CHUNK = 64
HGRN_HEAD_DIM = 128
CONV_WIDTH = 5
DN_ALPHA = 2.0 ** 0.25
LN_EPS = 1e-5
NORM_EPS = 1e-6
M_INIT = -1e30
NEG = -1e30
EXP_CLAMP = 80.0
ADAM_LR = 0.001
ADAM_B1 = 0.9
ADAM_B2 = 0.999
ADAM_EPS = 1e-08
ADAM_WD = 0.01
ADAM_STEP = 10
N_DEV = 8
LANES = 128
VMEM_LIMIT = 56 * 1024 * 1024
MESH = pl.DeviceIdType.MESH


def _params(*sem):
    return pltpu.CompilerParams(dimension_semantics=sem, vmem_limit_bytes=VMEM_LIMIT)


class _Job:
    def __init__(self, inputs, out_shape, sems, start, finish, middle=None):
        self.inputs, self.out_shape, self.sems = list(inputs), list(out_shape), list(sems)
        self.start, self.finish = start, finish
        self.middle = middle or (lambda ins, outs, sems: None)


def _join(*jobs):
    def split(refs, counts):
        out, p = [], 0
        for c in counts:
            out.append(refs[p:p + c])
            p += c
        return out

    n_in = [len(j.inputs) for j in jobs]
    n_out = [len(j.out_shape) for j in jobs]
    n_sem = [len(j.sems) for j in jobs]

    def start(ins, outs, sems):
        for j, i, o, s in zip(jobs, split(ins, n_in), split(outs, n_out), split(sems, n_sem)):
            j.start(i, o, s)

    def middle(ins, outs, sems):
        for j, i, o, s in zip(jobs, split(ins, n_in), split(outs, n_out), split(sems, n_sem)):
            j.middle(i, o, s)

    def finish(ins, outs, sems):
        for j, i, o, s in zip(jobs, split(ins, n_in), split(outs, n_out), split(sems, n_sem)):
            j.finish(i, o, s)

    return _Job(sum((j.inputs for j in jobs), []), sum((j.out_shape for j in jobs), []),
                sum((j.sems for j in jobs), []), start, finish, middle)


def _pcall(body, *, name, out_shape, grid, in_specs, out_specs, sem, scratch_shapes=(), comm=None):
    single = not isinstance(out_shape, (list, tuple))
    out_shape = [out_shape] if single else list(out_shape)
    out_specs = [out_specs] if single else list(out_specs)
    in_specs, scratch_shapes = list(in_specs), list(scratch_shapes)
    if comm is None:
        call = pl.pallas_call(body, name=name, out_shape=out_shape, grid=grid, in_specs=in_specs,
                              out_specs=out_specs, scratch_shapes=scratch_shapes,
                              compiler_params=_params(*sem))

        def run(*args):
            outs = call(*args)
            return outs[0] if single else outs
        return run

    n_in, n_out, n_sc = len(in_specs), len(out_shape), len(scratch_shapes)
    c_in, c_out = len(comm.inputs), len(comm.out_shape)

    def hosted(*refs):
        ins, cins = refs[:n_in], refs[n_in:n_in + c_in]
        p = n_in + c_in
        outs, couts = refs[p:p + n_out], refs[p + n_out:p + n_out + c_out]
        p += n_out + c_out
        scratch, csems = refs[p:p + n_sc], refs[p + n_sc:]
        first = pl.program_id(0) == 0
        half = pl.program_id(0) == grid[0] // 2
        last = pl.program_id(0) == grid[0] - 1
        for d in range(1, len(grid)):
            first = first & (pl.program_id(d) == 0)
            half = half & (pl.program_id(d) == 0)
            last = last & (pl.program_id(d) == grid[d] - 1)

        @pl.when(first)
        def _():
            comm.start(cins, couts, csems)

        if grid[0] >= 2:
            @pl.when(half)
            def _():
                comm.middle(cins, couts, csems)

        body(*ins, *outs, *scratch)

        @pl.when(last)
        def _():
            if grid[0] < 2:
                comm.middle(cins, couts, csems)
            comm.finish(cins, couts, csems)

    any_spec = pl.BlockSpec(memory_space=pl.ANY)
    call = pl.pallas_call(
        hosted, name=name, out_shape=out_shape + comm.out_shape, grid=grid,
        in_specs=in_specs + [any_spec] * c_in, out_specs=out_specs + [any_spec] * c_out,
        scratch_shapes=scratch_shapes + comm.sems,
        compiler_params=_params(*(["arbitrary"] * len(grid))))

    def run(*args):
        res = call(*args, *comm.inputs)
        outs, couts = res[:n_out], list(res[n_out:])
        return (outs[0] if single else outs), couts
    return run


def _sigmoid(x):
    return 1.0 / (1.0 + jnp.exp(-x))


def _log_sigmoid(x):
    return jnp.minimum(x, 0.0) - jnp.log(1.0 + jnp.exp(-jnp.abs(x)))


def _dot(a, b, dims):
    return lax.dot_general(a.astype(BF16), b.astype(BF16), (dims, ((), ())),
                           preferred_element_type=F32)


def _dot3(a, b, dims):
    ah = a.astype(BF16)
    al = (a - ah.astype(F32)).astype(BF16)
    bh = b.astype(BF16)
    bl = (b - bh.astype(F32)).astype(BF16)
    d = (dims, ((), ()))
    out = lax.dot_general(ah, bh, d, preferred_element_type=F32)
    out = out + lax.dot_general(ah, bl, d, preferred_element_type=F32)
    return out + lax.dot_general(al, bh, d, preferred_element_type=F32)


_DIMS = {"nn": ((1,), (0,)), "nt": ((1,), (1,)), "tn": ((0,), (0,))}


def _dot_nn(a, b):
    return _dot3(a, b, _DIMS["nn"])


def _dot_nt(a, b):
    return _dot3(a, b, _DIMS["nt"])


def _dot_tn(a, b):
    return _dot3(a, b, _DIMS["tn"])


def _split3(x):
    hi = x.astype(BF16)
    r1 = x - hi.astype(F32)
    mid = r1.astype(BF16)
    lo = (r1 - mid.astype(F32)).astype(BF16)
    return hi, mid, lo


def _dot01(mask01, x, mode="nn"):
    m = mask01.astype(BF16)
    hi, mid, lo = _split3(x)
    d = (_DIMS[mode], ((), ()))
    out = lax.dot_general(m, hi, d, preferred_element_type=F32)
    out = out + lax.dot_general(m, mid, d, preferred_element_type=F32)
    return out + lax.dot_general(m, lo, d, preferred_element_type=F32)


def _matmul(a, b, *, mode="nn", out_dtype=F32, tm=512, tn=512, tk=None,
            add=None, scale=1.0, add_scale=1.0, name, comm=None, rows=None):
    if mode == "nn":
        (M, K), (K2, N) = a.shape, b.shape
        K2 = rows or K2
    elif mode == "nt":
        (M, K), (N, K2) = a.shape, b.shape
        N = rows or N
    else:
        (K, M), (K2, N) = a.shape, b.shape
    assert K == K2, (a.shape, b.shape, mode)
    tm, tn = min(tm, M), min(tn, N)
    tk = min(tk or K, K)
    assert M % tm == 0 and N % tn == 0 and K % tk == 0, (M, N, K, tm, tn, tk)
    nk = K // tk
    dims = _DIMS[mode]
    has_add = add is not None

    def body(*refs):
        if has_add:
            a_ref, b_ref, add_ref, o_ref = refs[:4]
        else:
            a_ref, b_ref, o_ref = refs[:3]
            add_ref = None
        acc_ref = refs[-1] if nk > 1 else None

        def finish(acc):
            out = acc if scale == 1.0 else acc * scale
            if has_add:
                out = out + add_ref[...].astype(F32) * add_scale
            o_ref[...] = out.astype(o_ref.dtype)

        if nk == 1:
            finish(_dot(a_ref[...], b_ref[...], dims))
        else:
            k = pl.program_id(2)

            @pl.when(k == 0)
            def _():
                acc_ref[...] = jnp.zeros_like(acc_ref)

            acc_ref[...] += _dot(a_ref[...], b_ref[...], dims)

            @pl.when(k == nk - 1)
            def _():
                finish(acc_ref[...])

    if mode == "tn":
        a_spec = pl.BlockSpec((tk, tm), lambda i, j, k: (k, i))
    else:
        a_spec = pl.BlockSpec((tm, tk), lambda i, j, k: (i, k))
    if mode == "nt":
        b_spec = pl.BlockSpec((tn, tk), lambda i, j, k: (j, k))
    else:
        b_spec = pl.BlockSpec((tk, tn), lambda i, j, k: (k, j))
    o_spec = pl.BlockSpec((tm, tn), lambda i, j, k: (i, j))
    in_specs = [a_spec, b_spec] + ([o_spec] if has_add else [])
    args = (a, b) + ((add,) if has_add else ())
    return _pcall(
        body, name=name, comm=comm,
        out_shape=jax.ShapeDtypeStruct((M, N), out_dtype),
        grid=(M // tm, N // tn, nk),
        in_specs=in_specs, out_specs=o_spec,
        scratch_shapes=[pltpu.VMEM((tm, tn), F32)] if nk > 1 else [],
        sem=("parallel", "parallel", "arbitrary"),
    )(*args)


def _ln_stats(x):
    mu = jnp.mean(x, axis=-1, keepdims=True)
    xc = x - mu
    var = jnp.mean(xc * xc, axis=-1, keepdims=True)
    rstd = lax.rsqrt(var + LN_EPS)
    return xc * rstd, rstd


def _ln_fwd(r, g, b, *, tm=256, name):
    T, D = r.shape
    tm = min(tm, T)

    def body(r_ref, g_ref, b_ref, y_ref, yb_ref):
        xhat, _ = _ln_stats(r_ref[...])
        y = xhat * g_ref[...] + b_ref[...]
        y_ref[...] = y
        yb_ref[...] = y.astype(BF16)

    row = pl.BlockSpec((tm, D), lambda i: (i, 0))
    vec = pl.BlockSpec((1, D), lambda i: (0, 0))
    return pl.pallas_call(
        body, name=name,
        out_shape=[jax.ShapeDtypeStruct((T, D), F32), jax.ShapeDtypeStruct((T, D), BF16)],
        grid=(T // tm,), in_specs=[row, vec, vec], out_specs=[row, row],
        compiler_params=_params("parallel"),
    )(r, g, b)


def _cast(x, *, tm=512, name, comm=None):
    T, D = x.shape
    tm = min(tm, T)

    def body(x_ref, xb_ref):
        xb_ref[...] = x_ref[...].astype(BF16)

    row = pl.BlockSpec((tm, D), lambda i: (i, 0))
    return _pcall(
        body, name=name, comm=comm, out_shape=jax.ShapeDtypeStruct((T, D), BF16),
        grid=(T // tm,), in_specs=[row], out_specs=row, sem=("parallel",),
    )(x)


def _ln_bwd(dy, r, g, *, tm=256, name, b=None, target=None):
    T, D = r.shape
    tm = min(tm, T)
    with_loss = target is not None

    def body(*refs):
        if with_loss:
            r_ref, g_ref, b_ref, t_ref, dr_ref, drb_ref, dg_ref, db_ref, loss_ref = refs
        else:
            dy_ref, r_ref, g_ref, dr_ref, drb_ref, dg_ref, db_ref = refs
        i = pl.program_id(0)
        xhat, rstd = _ln_stats(r_ref[...])
        gg = g_ref[...]
        if with_loss:
            err = xhat * gg + b_ref[...] - t_ref[...]
            dyv = err * (1.0 / D)
            part = jnp.sum(jnp.sum(err * err, axis=1, keepdims=True), axis=0, keepdims=True)
            part = jnp.broadcast_to(part * (0.5 / D), (1, LANES))
        else:
            dyv = dy_ref[...]
        dxh = dyv * gg
        m1 = jnp.mean(dxh, axis=-1, keepdims=True)
        m2 = jnp.mean(dxh * xhat, axis=-1, keepdims=True)
        dr = rstd * (dxh - m1 - xhat * m2)
        dr_ref[...] = dr
        drb_ref[...] = dr.astype(BF16)
        dgp = jnp.sum(dyv * xhat, axis=0, keepdims=True)
        dbp = jnp.sum(dyv, axis=0, keepdims=True)

        @pl.when(i == 0)
        def _():
            dg_ref[...] = dgp
            db_ref[...] = dbp
            if with_loss:
                loss_ref[...] = part

        @pl.when(i > 0)
        def _():
            dg_ref[...] += dgp
            db_ref[...] += dbp
            if with_loss:
                loss_ref[...] += part

    row = pl.BlockSpec((tm, D), lambda i: (i, 0))
    vec = pl.BlockSpec((1, D), lambda i: (0, 0))
    out_shape = [jax.ShapeDtypeStruct((T, D), F32), jax.ShapeDtypeStruct((T, D), BF16),
                 jax.ShapeDtypeStruct((1, D), F32), jax.ShapeDtypeStruct((1, D), F32)]
    out_specs = [row, row, vec, vec]
    if with_loss:
        in_specs, args = [row, vec, vec, row], (r, g, b, target)
        out_shape.append(jax.ShapeDtypeStruct((1, LANES), F32))
        out_specs.append(pl.BlockSpec((1, LANES), lambda i: (0, 0)))
    else:
        in_specs, args = [row, row, vec], (dy, r, g)
    return pl.pallas_call(
        body, name=name, out_shape=out_shape, grid=(T // tm,),
        in_specs=in_specs, out_specs=out_specs,
        compiler_params=_params("arbitrary"),
    )(*args)


def _ffn_up(xb, w1t, w3t, *, a=None, tm=1024, tf=512, name, comm=None):
    F, D = w3t.shape
    T = xb.shape[0]
    tm, tf = _pick(T, tm), _pick(F, tf)
    given = a is not None

    def body(x_ref, w1_ref, w3_ref, a_ref, b_ref, h_ref, ht_ref):
        xv = x_ref[...]
        av = w1_ref[...].astype(F32) if given else _dot(xv, w1_ref[...], _DIMS["nt"])
        b = _dot(xv, w3_ref[...], _DIMS["nt"])
        a_ref[...] = av.astype(BF16)
        b_ref[...] = b.astype(BF16)
        h = av * _sigmoid(av) * b
        h_ref[...] = h.astype(BF16)
        ht_ref[...] = h.T.astype(BF16)

    wsp = pl.BlockSpec((tf, D), lambda j, i: (j, 0))
    osp = pl.BlockSpec((tm, tf), lambda j, i: (i, j))
    act = jax.ShapeDtypeStruct((T, F), BF16)
    return _pcall(
        body, name=name, comm=comm, out_shape=[act, act, act, jax.ShapeDtypeStruct((F, T), BF16)],
        grid=(F // tf, T // tm),
        in_specs=[pl.BlockSpec((tm, D), lambda j, i: (i, 0)), osp if given else wsp, wsp],
        out_specs=[osp, osp, osp, pl.BlockSpec((tf, tm), lambda j, i: (j, i))],
        sem=("parallel", "parallel"),
    )(xb, a if given else w1t, w3t)


def _ffn_bwd_act(drb, w2, a, b, *, tm=1024, tf=512, name, comm=None):
    F, D = w2.shape
    T = drb.shape[0]
    tm, tf = _pick(T, tm), _pick(F, tf)

    def body(dr_ref, w2_ref, a_ref, b_ref, da_ref, db_ref, dat_ref, dbt_ref):
        d = 0.5 * _dot(dr_ref[...], w2_ref[...], _DIMS["nt"])
        av = a_ref[...].astype(F32)
        sig = _sigmoid(av)
        da = d * b_ref[...].astype(F32) * sig * (1.0 + av * (1.0 - sig))
        db = d * av * sig
        da_ref[...] = da.astype(BF16)
        db_ref[...] = db.astype(BF16)
        dat_ref[...] = da.T.astype(BF16)
        dbt_ref[...] = db.T.astype(BF16)

    asp = pl.BlockSpec((tm, tf), lambda i, j: (i, j))
    tsp = pl.BlockSpec((tf, tm), lambda i, j: (j, i))
    out, out_t = jax.ShapeDtypeStruct((T, F), BF16), jax.ShapeDtypeStruct((F, T), BF16)
    return _pcall(
        body, name=name, comm=comm, out_shape=[out, out, out_t, out_t], grid=(T // tm, F // tf),
        in_specs=[pl.BlockSpec((tm, D), lambda i, j: (i, 0)),
                  pl.BlockSpec((tf, D), lambda i, j: (j, 0)), asp, asp],
        out_specs=[asp, asp, tsp, tsp],
        sem=("parallel", "parallel"),
    )(drb, w2, a, b)


def _dz_pack(pieces, *, tm=256, name):
    groups = [p if isinstance(p, tuple) else (p,) for p in pieces]
    T = groups[0][0].shape[0]
    widths = [g[0].shape[1] for g in groups]
    W = sum(widths)
    tm = _pick(T, tm)
    flat = [a for g in groups for a in g]
    n = len(flat)

    def body(*refs):
        o_ref, ot_ref = refs[n], refs[n + 1]
        c, p = 0, 0
        for g, w in zip(groups, widths):
            v = refs[p][...].astype(F32)
            for r in refs[p + 1:p + len(g)]:
                v = v + r[...].astype(F32)
            p += len(g)
            o_ref[:, c:c + w] = v.astype(BF16)
            ot_ref[c:c + w, :] = v.T.astype(BF16)
            c += w

    return pl.pallas_call(
        body, name=name,
        out_shape=[jax.ShapeDtypeStruct((T, W), BF16), jax.ShapeDtypeStruct((W, T), BF16)],
        grid=(T // tm,),
        in_specs=[pl.BlockSpec((tm, a.shape[1]), lambda i: (i, 0)) for a in flat],
        out_specs=[pl.BlockSpec((tm, W), lambda i: (i, 0)), pl.BlockSpec((W, tm), lambda i: (0, i))],
        compiler_params=_params("parallel"),
    )(*flat)


def _ffn_dx(da, db, w1t, w3t, dr, *, tm=1024, tn=1024, tk=None, name, comm=None):
    T, F = da.shape
    D = w1t.shape[1]
    tm, tn = _pick(T, tm), _pick(D, tn)
    tk = _pick(F, tk or F // 4)
    nk = F // tk

    def body(da_ref, db_ref, w1_ref, w3_ref, dr_ref, o_ref, acc_ref):
        k = pl.program_id(2)

        @pl.when(k == 0)
        def _():
            acc_ref[...] = jnp.zeros_like(acc_ref)

        acc_ref[...] += (_dot(da_ref[...], w1_ref[...], _DIMS["nn"])
                         + _dot(db_ref[...], w3_ref[...], _DIMS["nn"]))

        @pl.when(k == nk - 1)
        def _():
            o_ref[...] = DN_ALPHA * dr_ref[...] + acc_ref[...]

    asp = pl.BlockSpec((tm, tk), lambda i, j, k: (i, k))
    wsp = pl.BlockSpec((tk, tn), lambda i, j, k: (k, j))
    osp = pl.BlockSpec((tm, tn), lambda i, j, k: (i, j))
    return _pcall(
        body, name=name, comm=comm, out_shape=jax.ShapeDtypeStruct((T, D), F32),
        grid=(T // tm, D // tn, nk), in_specs=[asp, asp, wsp, wsp, osp], out_specs=osp,
        scratch_shapes=[pltpu.VMEM((tm, tn), F32)],
        sem=("parallel", "parallel", "arbitrary"),
    )(da, db, w1t, w3t, dr)


def _chunk_mask(reverse, transpose=False):
    row = lax.broadcasted_iota(jnp.int32, (CHUNK, CHUNK), 0)
    col = lax.broadcasted_iota(jnp.int32, (CHUNK, CHUNK), 1)
    if reverse != transpose:
        return col >= row
    return col <= row


def _run_specs(specs, *, name, comm=None):
    counts = [(len(s["in_specs"]), len(s["out_specs"]), len(s["scratch_shapes"])) for s in specs]

    def body(*refs):
        p, parts = 0, [[], [], []]
        for kind in range(3):
            for c in counts:
                parts[kind].append(refs[p:p + c[kind]])
                p += c[kind]
        live = [s["body"](*parts[0][k], *parts[1][k], *parts[2][k]) for k, s in enumerate(specs)]
        while live:
            for g in list(live):
                if next(g, StopIteration) is StopIteration:
                    live.remove(g)

    cat = lambda key: [v for s in specs for v in s[key]]
    res = _pcall(body, name=name, comm=comm, out_shape=cat("out_shape"), grid=specs[0]["grid"],
                 in_specs=cat("in_specs"), out_specs=cat("out_specs"),
                 scratch_shapes=cat("scratch_shapes"), sem=("arbitrary",))(*cat("args"))
    outs, couts = res if comm is not None else (res, None)
    split, p = [], 0
    for c in counts:
        split.append(outs[p:p + c[1]])
        p += c[1]
    return (split, couts) if comm is not None else split


def _hgrn_pre(hq, hf, lb):
    sig = _sigmoid(hf)
    f = lb + (1.0 - lb) * sig
    q = hq * _sigmoid(hq) * (HGRN_HEAD_DIM ** -0.5)
    return q, f, sig


def _hgrn_decays(f, cmf, reverse):
    bc = _dot01(cmf, jnp.log(f))
    last = 0 if reverse else CHUNK - 1
    blast = bc[last:last + 1, :]
    bref = bc[CHUNK // 2:CHUNK // 2 + 1, :]
    eq = jnp.exp(jnp.minimum(bc - bref, EXP_CLAMP))
    ek = jnp.exp(jnp.minimum(bref - bc, EXP_CLAMP))
    return bc, blast, eq, ek


def _hgrn_fwd(z, lb, *, wh, reverse):
    T = z.shape[0]
    nch = T // CHUNK
    nh = wh // HGRN_HEAD_DIM
    hd = HGRN_HEAD_DIM

    def ci(i):
        return nch - 1 - i if reverse else i

    def body(hq_ref, hi_ref, hf_ref, lb_ref, o_ref, st_ref, s_ref):
        @pl.when(pl.program_id(0) == 0)
        def _():
            s_ref[...] = jnp.zeros_like(s_ref)

        cm = _chunk_mask(reverse)
        cmf = cm.astype(F32)
        H = range(nh)
        sls = [slice(h * hd, (h + 1) * hd) for h in H]
        q, f, _ = _hgrn_pre(hq_ref[...], hf_ref[...], lb_ref[...])
        v = hi_ref[...]
        k = 1.0 - f
        yield
        bc, blast, eq, ek = _hgrn_decays(f, cmf, reverse)
        yield
        qh, kh, qe = q * eq, k * ek, q * jnp.exp(bc)
        k2, eblast = k * jnp.exp(blast - bc), jnp.exp(blast)
        st = [s_ref[h] for h in H]
        yield
        att = [jnp.where(cm, _dot_nt(qh[:, sl], kh[:, sl]), 0.0) for sl in sls]
        yield
        inter = [_dot(qe[:, sls[h]], st[h], _DIMS["nt"]) for h in H]
        yield
        out = [inter[h] + _dot(att[h], v[:, sls[h]], _DIMS["nn"]) for h in H]
        yield
        snew = [eblast[:, sls[h]] * st[h] + _dot_tn(v[:, sls[h]], k2[:, sls[h]]) for h in H]
        yield
        for h in H:
            st_ref[0, h] = st[h]
            o_ref[:, sls[h]] = out[h]
            s_ref[h] = snew[h]

    blk = lambda c: pl.BlockSpec((CHUNK, wh), lambda i: (ci(i), c))
    return dict(
        body=body, grid=(nch,), args=[z, z, z, lb],
        out_shape=[jax.ShapeDtypeStruct((T, wh), F32),
                   jax.ShapeDtypeStruct((nch, nh, hd, hd), F32)],
        in_specs=[blk(0), blk(1), blk(3 + int(reverse)),
                  pl.BlockSpec((1, wh), lambda i: (0, 0))],
        out_specs=[blk(0), pl.BlockSpec((1, nh, hd, hd), lambda i: (ci(i), 0, 0, 0))],
        scratch_shapes=[pltpu.VMEM((nh, hd, hd), F32)])


def _hgrn_bwd(z, lb, do, states, *, wh, reverse):
    T = z.shape[0]
    nch = T // CHUNK
    nh = wh // HGRN_HEAD_DIM
    hd = HGRN_HEAD_DIM

    def ci(i):
        return i if reverse else nch - 1 - i

    def body(hq_ref, hi_ref, hf_ref, lb_ref, do_ref, st_ref,
             dhq_ref, dhi_ref, dhf_ref, dlb_ref, ds_ref, gs_ref):
        @pl.when(pl.program_id(0) == 0)
        def _():
            ds_ref[...] = jnp.zeros_like(ds_ref)
            gs_ref[...] = jnp.zeros_like(gs_ref)
            dlb_ref[...] = jnp.zeros_like(dlb_ref)

        cm = _chunk_mask(reverse)
        cmf = cm.astype(F32)
        cmtf = _chunk_mask(reverse, transpose=True).astype(F32)
        H = range(nh)
        sls = [slice(h * hd, (h + 1) * hd) for h in H]
        hq, lb = hq_ref[...], lb_ref[...]
        q, f, sig = _hgrn_pre(hq, hf_ref[...], lb)
        v, dov = hi_ref[...], do_ref[...]
        k = 1.0 - f
        yield
        bc, blast, eq, ek = _hgrn_decays(f, cmf, reverse)
        yield
        ebc, eb2, eblast = jnp.exp(bc), jnp.exp(blast - bc), jnp.exp(blast)
        qh, kh, qe, k2 = q * eq, k * ek, q * ebc, k * eb2
        st = [st_ref[0, h] for h in H]
        dst = [ds_ref[h] for h in H]
        yield
        att = [jnp.where(cm, _dot(qh[:, sl], kh[:, sl], _DIMS["nt"]), 0.0) for sl in sls]
        yield
        datt = [jnp.where(cm, _dot_nt(dov[:, sl], v[:, sl]), 0.0) for sl in sls]
        yield
        dq_a = [_dot_nn(datt[h], kh[:, sls[h]]) for h in H]
        yield
        dq_s = [_dot_nn(dov[:, sls[h]], st[h]) for h in H]
        yield
        dk_a = [_dot_tn(datt[h], qh[:, sls[h]]) for h in H]
        yield
        dk_s = [_dot_nn(v[:, sls[h]], dst[h]) for h in H]
        yield
        dv = [_dot(att[h], dov[:, sls[h]], _DIMS["tn"]) + _dot(k2[:, sls[h]], dst[h], _DIMS["nt"])
              for h in H]
        yield
        dsn = [eblast[:, sls[h]] * dst[h] + _dot_tn(dov[:, sls[h]], qe[:, sls[h]]) for h in H]
        for h in H:
            ds_ref[h] = dsn[h]
        yield
        dq = jnp.concatenate(dq_a, axis=1) * eq + ebc * jnp.concatenate(dq_s, axis=1)
        dk = jnp.concatenate(dk_a, axis=1) * ek + eb2 * jnp.concatenate(dk_s, axis=1)
        db = q * dq - k * dk
        yield
        dg = _dot01(cmtf, db) + gs_ref[...]
        gs_ref[...] += jnp.sum(db, axis=0, keepdims=True)
        yield
        df = dg / f - dk
        dhf_ref[...] = (df * (1.0 - lb) * sig * (1.0 - sig)).astype(BF16)
        dlb_ref[...] += jnp.sum(df * (1.0 - sig), axis=0, keepdims=True)
        sq = _sigmoid(hq)
        dhq_ref[...] = (dq * (HGRN_HEAD_DIM ** -0.5) * sq * (1.0 + hq * (1.0 - sq))).astype(BF16)
        dhi_ref[...] = jnp.concatenate(dv, axis=1).astype(BF16)

    blk = lambda c: pl.BlockSpec((CHUNK, wh), lambda i: (ci(i), c))
    vec = pl.BlockSpec((1, wh), lambda i: (0, 0))
    big = jax.ShapeDtypeStruct((T, wh), BF16)
    return dict(
        body=body, grid=(nch,), args=[z, z, z, lb, do, states],
        in_specs=[blk(0), blk(1), blk(3 + int(reverse)), vec, blk(0),
                  pl.BlockSpec((1, nh, hd, hd), lambda i: (ci(i), 0, 0, 0))],
        out_shape=[big, big, big, jax.ShapeDtypeStruct((1, wh), F32)],
        out_specs=[blk(0), blk(0), blk(0), vec],
        scratch_shapes=[pltpu.VMEM((nh, hd, hd), F32), pltpu.VMEM((1, wh), F32)])


def _mlstm_intra(cm, cmt, ig_row, ig_col, xf_row, xf_col, m_st):
    lf_row, lf_col = _log_sigmoid(xf_row), _log_sigmoid(xf_col)
    bcol = jnp.sum(jnp.where(cm, lf_row, 0.0), axis=1, keepdims=True)
    brow = jnp.sum(jnp.where(cmt, lf_col, 0.0), axis=0, keepdims=True)
    blast = jnp.sum(lf_row, axis=1, keepdims=True)
    dmat = jnp.where(cm, bcol - brow + ig_row, NEG)
    m_inter = bcol + m_st
    m_t = jnp.maximum(m_inter, jnp.max(dmat, axis=1, keepdims=True))
    p = jnp.exp(dmat - m_t)
    inter = jnp.exp(m_inter - m_t)
    w_col = blast - bcol + ig_col
    m_new = jnp.maximum(blast + m_st, jnp.max(w_col, axis=0, keepdims=True))
    cs = jnp.exp(blast + m_st - m_new)
    kscale = jnp.exp(w_col - m_new)
    return p, inter, m_t, m_new, cs, kscale


def _mlstm_specs(T, wm, nhm, reverse, backward):
    nch = T // CHUNK
    dm = wm // nhm
    ng = 4 * nhm

    def ci(i):
        fwd_order = nch - 1 - i if reverse else i
        return nch - 1 - fwd_order if backward else fwd_order

    row = lambda w, c: pl.BlockSpec((CHUNK, w), lambda i: (ci(i), c))
    gates_row = pl.BlockSpec((1, ng, CHUNK), lambda i: (ci(i), 0, 0))
    bias_row = pl.BlockSpec((1, LANES), lambda i: (0, 0))
    bias_col = pl.BlockSpec((ng, 1), lambda i: (0, 0))
    st_c = pl.BlockSpec((1, nhm, dm, dm), lambda i: (ci(i), 0, 0, 0))
    st_n = pl.BlockSpec((1, nhm, 1, dm), lambda i: (ci(i), 0, 0, 0))
    st_m = pl.BlockSpec((1, nhm, 1, LANES), lambda i: (ci(i), 0, 0, 0))
    return nch, dm, ng, ci, row, gates_row, bias_row, bias_col, st_c, st_n, st_m


def _mlstm_fwd(qk, z, gc, gr, br, bcl, *, wm, nhm, vcol, reverse):
    T = qk.shape[0]
    nch, dm, ng, ci, row, gates_row, bias_row, bias_col, st_c, st_n, st_m = _mlstm_specs(
        T, wm, nhm, reverse, False)
    d = int(reverse)

    def body(q_ref, k_ref, v_ref, gc_ref, gr_ref, br_ref, bc_ref,
             h_ref, cst_ref, nst_ref, mst_ref, c_ref, n_ref, m_ref):
        @pl.when(pl.program_id(0) == 0)
        def _():
            c_ref[...] = jnp.zeros_like(c_ref)
            n_ref[...] = jnp.zeros_like(n_ref)
            m_ref[...] = jnp.full_like(m_ref, M_INIT)

        cm = _chunk_mask(reverse)
        cmt = _chunk_mask(reverse, transpose=True)
        G = gc_ref[...] + br_ref[...]
        Gr = gr_ref[0] + bc_ref[...]
        H = range(nhm)
        sls = [slice(h * dm, (h + 1) * dm) for h in H]
        m_all = [m_ref[h] for h in H]
        intra = [_mlstm_intra(cm, cmt, Gr[d * nhm + h:d * nhm + h + 1, :], G[:, d * nhm + h:d * nhm + h + 1],
                              Gr[2 * nhm + d * nhm + h:2 * nhm + d * nhm + h + 1, :],
                              G[:, 2 * nhm + d * nhm + h:2 * nhm + d * nhm + h + 1], m_all[h][:, 0:1])
                 for h in H]
        p, inter, m_t, m_new, cs, kscale = zip(*intra)
        yield
        q = [q_ref[:, sl] * (dm ** -0.5) for sl in sls]
        k = [k_ref[:, sl] for sl in sls]
        v = [v_ref[:, sl] for sl in sls]
        ct = [c_ref[h] for h in H]
        n = [n_ref[h] for h in H]
        yield
        sc = [_dot_nt(q[h], k[h]) * p[h] for h in H]
        yield
        qc = [_dot(q[h], ct[h], _DIMS["nt"]) for h in H]
        yield
        num = [_dot(sc[h], v[h], _DIMS["nn"]) + inter[h] * qc[h] for h in H]
        yield
        den = [jnp.sum(sc[h], axis=1, keepdims=True) + inter[h] * jnp.sum(q[h] * n[h], axis=1, keepdims=True)
               for h in H]
        kw = [k[h] * kscale[h] for h in H]
        yield
        cnew = [cs[h] * ct[h] + _dot_tn(v[h], kw[h]) for h in H]
        yield
        for h in H:
            cst_ref[0, h] = ct[h]
            nst_ref[0, h] = n[h]
            mst_ref[0, h] = m_all[h]
            h_ref[:, sls[h]] = num[h] / jnp.maximum(jnp.abs(den[h]), jnp.exp(-m_t[h]))
            c_ref[h] = cnew[h]
            n_ref[h] = cs[h] * n[h] + jnp.sum(kw[h], axis=0, keepdims=True)
            m_ref[h] = jnp.broadcast_to(m_new[h], (1, LANES))

    return dict(
        body=body, grid=(nch,), args=[qk, qk, z, gc, gr, br, bcl],
        out_shape=[jax.ShapeDtypeStruct((T, wm), F32),
                   jax.ShapeDtypeStruct((nch, nhm, dm, dm), F32),
                   jax.ShapeDtypeStruct((nch, nhm, 1, dm), F32),
                   jax.ShapeDtypeStruct((nch, nhm, 1, LANES), F32)],
        in_specs=[row(wm, 0), row(wm, 1), row(wm, vcol), row(LANES, 0), gates_row, bias_row, bias_col],
        out_specs=[row(wm, 0), st_c, st_n, st_m],
        scratch_shapes=[pltpu.VMEM((nhm, dm, dm), F32), pltpu.VMEM((nhm, 1, dm), F32),
                        pltpu.VMEM((nhm, 1, LANES), F32)])


def _mlstm_bwd(qk, z, gc, gr, br, bcl, dh, states, *, wm, nhm, vcol, reverse):
    T = qk.shape[0]
    nch, dm, ng, ci, row, gates_row, bias_row, bias_col, st_c, st_n, st_m = _mlstm_specs(
        T, wm, nhm, reverse, True)
    d = int(reverse)

    def body(q_ref, k_ref, v_ref, gc_ref, gr_ref, br_ref, bc_ref, dh_ref, cst_ref, nst_ref, mst_ref,
             dqk_ref, dv_ref, dgr_ref, dgs_ref, e_ref, en_ref, fs_ref):
        @pl.when(pl.program_id(0) == 0)
        def _():
            e_ref[...] = jnp.zeros_like(e_ref)
            en_ref[...] = jnp.zeros_like(en_ref)
            fs_ref[...] = jnp.zeros_like(fs_ref)
            dgs_ref[...] = jnp.zeros_like(dgs_ref)

        cm = _chunk_mask(reverse)
        cmt = _chunk_mask(reverse, transpose=True)
        row_i = lax.broadcasted_iota(jnp.int32, (CHUNK, CHUNK), 0)
        col_i = lax.broadcasted_iota(jnp.int32, (CHUNK, CHUNK), 1)
        eye = row_i == col_i
        G = gc_ref[...] + br_ref[...]
        Gr = gr_ref[0] + bc_ref[...]
        H = range(nhm)
        sls = [slice(h * dm, (h + 1) * dm) for h in H]
        xf_row = [Gr[2 * nhm + d * nhm + h:2 * nhm + d * nhm + h + 1, :] for h in H]
        intra = [_mlstm_intra(cm, cmt, Gr[d * nhm + h:d * nhm + h + 1, :], G[:, d * nhm + h:d * nhm + h + 1],
                              xf_row[h], G[:, 2 * nhm + d * nhm + h:2 * nhm + d * nhm + h + 1],
                              mst_ref[0, h][:, 0:1]) for h in H]
        p, inter, m_t, _, cs, kscale = zip(*intra)
        yield
        q = [q_ref[:, sls[h]] * (dm ** -0.5) for h in H]
        k = [k_ref[:, sls[h]] for h in H]
        v = [v_ref[:, sls[h]] for h in H]
        dhv = [dh_ref[:, sls[h]] for h in H]
        ct = [cst_ref[0, h] for h in H]
        n = [nst_ref[0, h] for h in H]
        et = [e_ref[h] for h in H]
        en = [en_ref[h] for h in H]
        carry = [fs_ref[h][:, 0:1] for h in H]
        yield
        sc = [_dot_nt(q[h], k[h]) * p[h] for h in H]
        yield
        qc = [_dot_nt(q[h], ct[h]) for h in H]
        yield
        num = [_dot_nn(sc[h], v[h]) + inter[h] * qc[h] for h in H]
        yield
        den = [jnp.sum(sc[h], axis=1, keepdims=True) + inter[h] * jnp.sum(q[h] * n[h], axis=1, keepdims=True)
               for h in H]
        floor = [jnp.exp(-m_t[h]) for h in H]
        nstab = [jnp.maximum(jnp.abs(den[h]), floor[h]) for h in H]
        gh = [dhv[h] / nstab[h] for h in H]
        dd = [-jnp.sum(dhv[h] * (num[h] / nstab[h]), axis=1, keepdims=True) / nstab[h]
              * jnp.where(jnp.abs(den[h]) > floor[h], jnp.sign(den[h]), 0.0) for h in H]
        yield
        dsqk = [(_dot_nt(gh[h], v[h]) + dd[h]) * p[h] for h in H]
        qi = [q[h] * inter[h] for h in H]
        kw = [k[h] * kscale[h] for h in H]
        yield
        ghc = [_dot_nn(gh[h], ct[h]) for h in H]
        yield
        vet = [_dot_nn(v[h], et[h]) for h in H]
        yield
        dq = [_dot_nn(dsqk[h], k[h]) + inter[h] * (ghc[h] + dd[h] * n[h]) for h in H]
        yield
        dk = [_dot_tn(dsqk[h], q[h]) + kscale[h] * (vet[h] + en[h]) for h in H]
        yield
        dv = [_dot(sc[h], gh[h], _DIMS["tn"]) + _dot(kw[h], et[h], _DIMS["nt"]) for h in H]
        yield
        e_new = [cs[h] * et[h] + _dot_tn(gh[h], qi[h]) for h in H]
        en_new = [cs[h] * en[h] + jnp.sum(qi[h] * dd[h], axis=0, keepdims=True) for h in H]
        yield
        di_col = [jnp.sum(k[h] * dk[h], axis=1, keepdims=True) for h in H]
        df_col = [jnp.sum(q[h] * dq[h], axis=1, keepdims=True) - di_col[h] for h in H]
        di_row = [jnp.sum(jnp.where(eye, di_col[h], 0.0), axis=0, keepdims=True) for h in H]
        dlf_row = [jnp.sum(jnp.where(cm, df_col[h], 0.0), axis=0, keepdims=True) + carry[h] for h in H]
        dxf_row = [dlf_row[h] * (1.0 - _sigmoid(xf_row[h])) for h in H]
        yield
        for h in H:
            sl = sls[h]
            slk = slice(wm + h * dm, wm + (h + 1) * dm)
            e_ref[h] = e_new[h]
            en_ref[h] = en_new[h]
            fs_ref[h] = jnp.broadcast_to(carry[h] + jnp.sum(df_col[h], axis=0, keepdims=True), (1, LANES))
            dgr_ref[0, h:h + 1, :] = di_row[h]
            dgr_ref[0, nhm + h:nhm + h + 1, :] = dxf_row[h]
            dgs_ref[h:h + 1, :] += jnp.broadcast_to(jnp.sum(di_row[h], axis=1, keepdims=True), (1, LANES))
            dgs_ref[nhm + h:nhm + h + 1, :] += jnp.broadcast_to(
                jnp.sum(dxf_row[h], axis=1, keepdims=True), (1, LANES))
            dqk_ref[:, sl] = (dq[h] * (dm ** -0.5)).astype(BF16)
            dqk_ref[:, slk] = dk[h].astype(BF16)
            dv_ref[:, sl] = dv[h].astype(BF16)

    return dict(
        body=body, grid=(nch,), args=[qk, qk, z, gc, gr, br, bcl, dh] + list(states),
        in_specs=[row(wm, 0), row(wm, 1), row(wm, vcol), row(LANES, 0), gates_row, bias_row, bias_col,
                  row(wm, 0), st_c, st_n, st_m],
        out_shape=[jax.ShapeDtypeStruct((T, 2 * wm), BF16), jax.ShapeDtypeStruct((T, wm), BF16),
                   jax.ShapeDtypeStruct((nch, 2 * nhm, CHUNK), F32),
                   jax.ShapeDtypeStruct((2 * nhm, LANES), F32)],
        out_specs=[row(2 * wm, 0), row(wm, 0),
                   pl.BlockSpec((1, 2 * nhm, CHUNK), lambda i: (ci(i), 0, 0)),
                   pl.BlockSpec((2 * nhm, LANES), lambda i: (0, 0))],
        scratch_shapes=[pltpu.VMEM((nhm, dm, dm), F32), pltpu.VMEM((nhm, 1, dm), F32),
                        pltpu.VMEM((nhm, 1, LANES), F32)])


def _conv_taps(x, w_ref):
    T = x.shape[0]
    t = lax.broadcasted_iota(jnp.int32, x.shape, 0)
    taps = []
    acc = None
    for j in range(CONV_WIDTH):
        s = CONV_WIDTH // 2 - j
        if s == 0:
            xs = x
        else:
            xs = jnp.where((t - s >= 0) & (t - s < T), pltpu.roll(x, s % T, 0), 0.0)
        taps.append(xs)
        term = w_ref[j:j + 1, :] * xs
        acc = term if acc is None else acc + term
    return taps, acc


def _conv_fwd(z, w, b, *, col0, tc=LANES, name):
    T = z.shape[0]
    C2 = w.shape[1]
    assert col0 % tc == 0 and C2 % tc == 0

    def body(z_ref, w_ref, b_ref, o_ref):
        _, acc = _conv_taps(z_ref[...], w_ref)
        c = acc + b_ref[...]
        o_ref[...] = c * _sigmoid(c)

    return pl.pallas_call(
        body, name=name, out_shape=jax.ShapeDtypeStruct((T, C2), F32),
        grid=(C2 // tc,),
        in_specs=[pl.BlockSpec((T, tc), lambda j: (0, col0 // tc + j)),
                  pl.BlockSpec((CONV_WIDTH, tc), lambda j: (0, j)),
                  pl.BlockSpec((1, tc), lambda j: (0, j))],
        out_specs=pl.BlockSpec((T, tc), lambda j: (0, j)),
        compiler_params=_params("parallel"),
    )(z, w, b)


def _conv_bwd(dy_fw, dy_bw, z, w, b, *, col0, tc=LANES, name):
    T = z.shape[0]
    C2 = w.shape[1]

    def body(dyf_ref, dyb_ref, z_ref, w_ref, b_ref, dx_ref, dw_ref, db_ref):
        taps, acc = _conv_taps(z_ref[...], w_ref)
        c = acc + b_ref[...]
        sg = _sigmoid(c)
        dc = (dyf_ref[...].astype(F32) + dyb_ref[...].astype(F32)) * sg * (1.0 + c * (1.0 - sg))
        t = lax.broadcasted_iota(jnp.int32, dc.shape, 0)
        dx = None
        for j in range(CONV_WIDTH):
            s = j - CONV_WIDTH // 2
            if s == 0:
                ds = dc
            else:
                ds = jnp.where((t - s >= 0) & (t - s < T), pltpu.roll(dc, s % T, 0), 0.0)
            term = w_ref[j:j + 1, :] * ds
            dx = term if dx is None else dx + term
            dw_ref[j:j + 1, :] = jnp.sum(dc * taps[j], axis=0, keepdims=True)
        dx_ref[...] = dx.astype(BF16)
        db_ref[...] = jnp.sum(dc, axis=0, keepdims=True)

    col = pl.BlockSpec((T, tc), lambda j: (0, j))
    wsp = pl.BlockSpec((CONV_WIDTH, tc), lambda j: (0, j))
    bsp = pl.BlockSpec((1, tc), lambda j: (0, j))
    return pl.pallas_call(
        body, name=name,
        out_shape=[jax.ShapeDtypeStruct((T, C2), BF16), jax.ShapeDtypeStruct((CONV_WIDTH, C2), F32),
                   jax.ShapeDtypeStruct((1, C2), F32)],
        grid=(C2 // tc,),
        in_specs=[col, col, pl.BlockSpec((T, tc), lambda j: (0, col0 // tc + j)), wsp, bsp],
        out_specs=[col, wsp, bsp],
        compiler_params=_params("parallel"),
    )(dy_fw, dy_bw, z, w, b)


def _mix_heads(o_fw, o_bw, h_fw, h_bw, wh, wm, nhm):
    out = []
    hd = HGRN_HEAD_DIM
    for h in range(wh // hd):
        sl = slice(h * hd, (h + 1) * hd)
        o = o_fw[:, sl] + o_bw[:, sl]
        r = lax.rsqrt(jnp.mean(o * o, axis=-1, keepdims=True) + NORM_EPS)
        out.append((0, sl, o * r, r))
    dm = wm // nhm
    for h in range(nhm):
        sl = slice(h * dm, (h + 1) * dm)
        x = h_fw[:, sl] + h_bw[:, sl]
        xc = x - jnp.mean(x, axis=-1, keepdims=True)
        r = lax.rsqrt(jnp.mean(xc * xc, axis=-1, keepdims=True) + NORM_EPS)
        out.append((1, sl, xc * r, r))
    return out


def _mix_specs(T, wh, wm, tm, gcol, ocol):
    rowh = pl.BlockSpec((tm, wh), lambda i: (i, 0))
    rowm = pl.BlockSpec((tm, wm), lambda i: (i, 0))
    hg = pl.BlockSpec((tm, wh), lambda i: (i, gcol))
    mo = pl.BlockSpec((tm, wm), lambda i: (i, ocol))
    vh = pl.BlockSpec((1, wh), lambda i: (0, 0))
    vm = pl.BlockSpec((1, wm), lambda i: (0, 0))
    return rowh, rowm, hg, mo, vh, vm


def _mix_fwd(o_fw, o_bw, h_fw, h_bw, z, gh, gm, *, nhm, gcol, ocol, tm=256, name):
    T, wh = o_fw.shape
    wm = h_fw.shape[1]
    tm = min(tm, T)
    rowh, rowm, hg, mo, vh, vm = _mix_specs(T, wh, wm, tm, gcol, ocol)

    def body(of_ref, ob_ref, hf_ref, hb_ref, hg_ref, mo_ref, gh_ref, gm_ref, y_ref, yt_ref):
        heads = _mix_heads(of_ref[...], ob_ref[...], hf_ref[...], hb_ref[...], wh, wm, nhm)
        for grp, sl, nv, _ in heads:
            if grp == 0:
                gate = hg_ref[:, sl]
                gate = gate * _sigmoid(gate)
                y = nv * gh_ref[:, sl] * gate
                osl = sl
            else:
                y = nv * gm_ref[:, sl] * _sigmoid(mo_ref[:, sl])
                osl = slice(wh + sl.start, wh + sl.stop)
            y_ref[:, osl] = y.astype(BF16)
            yt_ref[osl, :] = y.T.astype(BF16)

    return pl.pallas_call(
        body, name=name,
        out_shape=[jax.ShapeDtypeStruct((T, wh + wm), BF16), jax.ShapeDtypeStruct((wh + wm, T), BF16)],
        grid=(T // tm,),
        in_specs=[rowh, rowh, rowm, rowm, hg, mo, vh, vm],
        out_specs=[pl.BlockSpec((tm, wh + wm), lambda i: (i, 0)),
                   pl.BlockSpec((wh + wm, tm), lambda i: (0, i))],
        compiler_params=_params("parallel"),
    )(o_fw, o_bw, h_fw, h_bw, z, z, gh, gm)


def _mix_bwd(dy, o_fw, o_bw, h_fw, h_bw, z, gh, gm, *, nhm, gcol, ocol, tm=256, name, comm=None):
    T, wh = o_fw.shape
    wm = h_fw.shape[1]
    tm = min(tm, T)
    rowh, rowm, hg, mo, vh, vm = _mix_specs(T, wh, wm, tm, gcol, ocol)

    def body(dy_ref, of_ref, ob_ref, hf_ref, hb_ref, hg_ref, mo_ref, gh_ref, gm_ref,
             do_ref, dh_ref, dhg_ref, dmo_ref, dgh_ref, dgm_ref):
        @pl.when(pl.program_id(0) == 0)
        def _():
            dgh_ref[...] = jnp.zeros_like(dgh_ref)
            dgm_ref[...] = jnp.zeros_like(dgm_ref)

        heads = _mix_heads(of_ref[...], ob_ref[...], hf_ref[...], hb_ref[...], wh, wm, nhm)
        for grp, sl, nv, r in heads:
            if grp == 0:
                d = dy_ref[:, sl]
                x = hg_ref[:, sl]
                sg = _sigmoid(x)
                gate = x * sg
                g = gh_ref[:, sl]
                dgh_ref[:, sl] += jnp.sum(d * nv * gate, axis=0, keepdims=True)
                dhg_ref[:, sl] = (d * nv * g * sg * (1.0 + x * (1.0 - sg))).astype(BF16)
                t = d * g * gate
                do_ref[:, sl] = r * (t - nv * jnp.mean(t * nv, axis=-1, keepdims=True))
            else:
                d = dy_ref[:, slice(wh + sl.start, wh + sl.stop)]
                sg = _sigmoid(mo_ref[:, sl])
                g = gm_ref[:, sl]
                dgm_ref[:, sl] += jnp.sum(d * nv * sg, axis=0, keepdims=True)
                dmo_ref[:, sl] = (d * nv * g * sg * (1.0 - sg)).astype(BF16)
                t = d * g * sg
                dh_ref[:, sl] = r * (t - jnp.mean(t, axis=-1, keepdims=True)
                                     - nv * jnp.mean(t * nv, axis=-1, keepdims=True))

    return _pcall(
        body, name=name, comm=comm,
        out_shape=[jax.ShapeDtypeStruct((T, wh), F32), jax.ShapeDtypeStruct((T, wm), F32),
                   jax.ShapeDtypeStruct((T, wh), BF16), jax.ShapeDtypeStruct((T, wm), BF16),
                   jax.ShapeDtypeStruct((1, wh), F32), jax.ShapeDtypeStruct((1, wm), F32)],
        grid=(T // tm,),
        in_specs=[pl.BlockSpec((tm, wh + wm), lambda i: (i, 0)), rowh, rowh, rowm, rowm, hg, mo, vh, vm],
        out_specs=[rowh, rowm, rowh, rowm, vh, vm],
        sem=("arbitrary",),
    )(dy, o_fw, o_bw, h_fw, h_bw, z, z, gh, gm)


def _pick(n, pref):
    for c in range(pref - pref % LANES, 0, -LANES):
        if n % c == 0:
            return c
    return n


def _mm(a, b, mode="nn", *, tm=1024, tn=1024, tk=2048, **kw):
    if mode == "nn":
        (M, K), N = a.shape, b.shape[1]
    else:
        (M, K), N = a.shape, kw.get("rows") or b.shape[0]
    return _matmul(a, b, mode=mode, tm=_pick(M, tm), tn=_pick(N, tn), tk=_pick(K, tk), **kw)


def _device_step(x, target, sh, small_parts, small_params, core):
    T, D = x.shape

    def full(g):
        return g.reshape(-1, D)

    def shards(g):
        return g.reshape(N_DEV, -1, D)

    def pair_add(name, g, land):
        return _pair_add(g, land, core, "rs_add_" + name)

    def dw(lhs_t, rhs, name, scale=1.0):
        return shards(_mm(lhs_t, rhs, out_dtype=BF16, scale=scale, tm=704, tk=T, name=name))

    xb, (g_w1a, g_small) = _cast(x, name="x_cast", comm=_ag_job([sh["ffn1_w1"], small_parts]))
    w1a = full(g_w1a)
    P = small_params(g_small)
    wh = P["hgrn_norm_g"].shape[1]
    wm = P["mlstm_norm_g"].shape[1]
    nhm = P["ig_b"].shape[1]
    ng = 4 * nhm
    nch = T // CHUNK
    assert wh == wm and T % CHUNK == 0
    vcol, ocol, gcol = 7, 8, 2
    col0 = 5 * wh
    nz = 5 * wh + 4 * wm
    a1f, (g_w3a,) = _mm(xb, w1a, "nt", out_dtype=BF16, name="ffn1_up_a", comm=_ag_job([sh["ffn1_w3"]]))
    w3a = full(g_w3a)
    cut = sh["w_in"].shape[0] // 2 // 16 * 16
    (a1, b1, h1, h1t), (g_w2a, g_wtop) = _ffn_up(xb, None, w3a, a=a1f, name="ffn1_up_b",
                                                 comm=_ag_job([sh["ffn1_w2"], sh["w_in"][:cut]]))
    w2a = full(g_w2a)
    r1, (g_wbot,) = _mm(h1, w2a, add=x, scale=0.5, add_scale=DN_ALPHA, tk=w2a.shape[0] // 4,
                        name="ffn1_down", comm=_ag_job([sh["w_in"][cut:]]))
    w_int = full(jnp.concatenate([g_wtop, g_wbot], axis=1))
    w_wgt = jnp.zeros((LANES, D), BF16).at[:ng].set(w_int[nz:])
    x1, x1b = _ln_fwd(r1, P["ln1_g"], P["ln1_b"], name="ln1")
    zm, (g_wout,) = _mm(x1b, w_int, "nt", rows=nz, name="zm", comm=_ag_job([sh["w_out"]]))
    w_wout = full(g_wout)
    zg = _mm(x1b, w_wgt, "nt", name="zg")
    gr = zg[:, :ng].reshape(nch, CHUNK, ng).transpose(0, 2, 1)
    bias = jnp.concatenate([P["ig_b"].reshape(-1), P["fg_b"].reshape(-1)])
    br = jnp.zeros((1, LANES), F32).at[0, :ng].set(bias)
    bcl = bias.reshape(ng, 1)
    lbf, lbb = P["lb"][0:1], P["lb"][1:2]
    ((o_fw, s_fw), (o_bw, s_bw)), (g_w1b,) = _run_specs(
        [_hgrn_fwd(zm, lbf, wh=wh, reverse=False), _hgrn_fwd(zm, lbb, wh=wh, reverse=True)],
        name="hgrn_fwd", comm=_ag_job([sh["ffn2_w1"]]))
    qk = _conv_fwd(zm, P["conv_w"], P["conv_b"], col0=col0, name="conv")
    mkw = dict(wm=wm, nhm=nhm, vcol=vcol)
    ((h_fw, *st_fw),), (g_w3b,) = _run_specs([_mlstm_fwd(qk, zm, zg, gr, br, bcl, reverse=False, **mkw)],
                                             name="mlstm_fw", comm=_ag_job([sh["ffn2_w3"]]))
    ((h_bw, *st_bw),) = _run_specs([_mlstm_fwd(qk, zm, zg, gr, br, bcl, reverse=True, **mkw)],
                                   name="mlstm_bw")
    w1b, w3b = full(g_w1b), full(g_w3b)
    y, yt = _mix_fwd(o_fw, o_bw, h_fw, h_bw, zm, P["hgrn_norm_g"], P["mlstm_norm_g"],
                     nhm=nhm, gcol=gcol, ocol=ocol, name="mix")
    r2 = _mm(y, w_wout, add=x1, add_scale=DN_ALPHA, name="r2")
    x2, x2b = _ln_fwd(r2, P["ln2_g"], P["ln2_b"], name="ln2")
    (a2, b2, h2, h2t), (g_w2b,) = _ffn_up(x2b, w1b, w3b, name="ffn2_up", comm=_ag_job([sh["ffn2_w2"]]))
    w2b = full(g_w2b)
    r3 = _mm(h2, w2b, add=x2, scale=0.5, add_scale=DN_ALPHA, tk=w2b.shape[0] // 4, name="ffn2_down")

    G, land = {}, {}
    dr3, dr3b, G["ln3_g"], G["ln3_b"], loss = _ln_bwd(
        None, r3, P["ln3_g"], b=P["ln3_b"], target=target, name="ln3_bwd")
    da2, db2, da2t, db2t = _ffn_bwd_act(dr3b, w2b, a2, b2, name="ffn2_dact")
    gw1b, gw3b = dw(da2t, x2b, "ffn2_dw1"), dw(db2t, x2b, "ffn2_dw3")
    gw2b = dw(h2t, dr3b, "ffn2_dw2", scale=0.5)
    dx2, l1 = _ffn_dx(da2, db2, w1b, w3b, dr3, name="ffn2_dx", comm=_pair_job([gw1b, gw3b, gw2b]))
    p_w1b, p_w3b, p_w2b = [pair_add(k, g, l) for k, g, l in
                           zip(("ffn2_w1", "ffn2_w3", "ffn2_w2"), (gw1b, gw3b, gw2b), l1)]
    dr2, dr2b, G["ln2_g"], G["ln2_b"] = _ln_bwd(dx2, r2, P["ln2_g"], name="ln2_bwd")
    dy = _mm(dr2b, w_wout, "nt", name="dy")
    gwout = shards(_mm(yt, dr2b, out_dtype=BF16, name="dwout"))
    (do, dh, dhg, dmo, G["hgrn_norm_g"], G["mlstm_norm_g"]), (l1,) = _mix_bwd(
        dy, o_fw, o_bw, h_fw, h_bw, zm, P["hgrn_norm_g"], P["mlstm_norm_g"],
        nhm=nhm, gcol=gcol, ocol=ocol, name="mix_bwd", comm=_pair_job([gwout]))
    p_wout = pair_add("w_out", gwout, l1)
    ((dqk_f, dv_f, dgr_f, dgs_f),), (land["ffn2_w1"],) = _run_specs(
        [_mlstm_bwd(qk, zm, zg, gr, br, bcl, dh, st_fw, reverse=False, **mkw)],
        name="mlstm_fw_bwd", comm=_chips_job([p_w1b]))
    ((dqk_b, dv_b, dgr_b, dgs_b),), (land["ffn2_w3"],) = _run_specs(
        [_mlstm_bwd(qk, zm, zg, gr, br, bcl, dh, st_bw, reverse=True, **mkw)],
        name="mlstm_bw_bwd", comm=_chips_job([p_w3b]))
    dmqk, G["conv_w"], G["conv_b"] = _conv_bwd(dqk_f, dqk_b, zm, P["conv_w"], P["conv_b"], col0=col0,
                                               name="conv_bwd")
    ((dhq_f, dhi_f, dhf_fw, dlb_f), (dhq_b, dhi_b, dhf_bw, dlb_b)), (land["ffn2_w2"], land["w_out"]) = \
        _run_specs([_hgrn_bwd(zm, lbf, do, s_fw, wh=wh, reverse=False),
                    _hgrn_bwd(zm, lbb, do, s_bw, wh=wh, reverse=True)],
                   name="hgrn_bwd", comm=_chips_job([p_w2b, p_wout]))
    G["lb"] = jnp.concatenate([dlb_f, dlb_b], axis=0)
    G["ig_b"] = jnp.stack([dgs_f[:nhm, 0], dgs_b[:nhm, 0]])
    G["fg_b"] = jnp.stack([dgs_f[nhm:, 0], dgs_b[nhm:, 0]])
    dzm, dzmt = _dz_pack([(dhq_f, dhq_b), (dhi_f, dhi_b), dhg, dhf_fw, dhf_bw, dmqk, (dv_f, dv_b), dmo],
                         name="dz_pack")
    dgr = jnp.concatenate([dgr_f[:, :nhm], dgr_b[:, :nhm], dgr_f[:, nhm:], dgr_b[:, nhm:]], axis=1)
    dzgt = jnp.zeros((LANES, T), F32).at[:ng].set(dgr.transpose(1, 0, 2).reshape(ng, T)).astype(BF16)
    dzg = jnp.zeros((T, LANES), F32).at[:, :ng].set(dgr.transpose(0, 2, 1).reshape(T, ng)).astype(BF16)
    gwin = shards(jnp.concatenate([_mm(dzmt, x1b, out_dtype=BF16, tk=T, name="dwin"),
                                   _mm(dzgt, x1b, out_dtype=BF16, tk=T, name="dwg")[:ng]], axis=0))
    t, (l1,) = _mm(dzg, w_wgt, add=dr2, add_scale=DN_ALPHA, name="dx1_g", comm=_pair_job([gwin]))
    p_win = pair_add("w_in", gwin, l1)
    dx1, (land["w_in"],) = _mm(dzm, w_int, rows=nz, add=t, tk=1536, name="dx1", comm=_chips_job([p_win]))
    dr1, dr1b, G["ln1_g"], G["ln1_b"] = _ln_bwd(dx1, r1, P["ln1_g"], name="ln1_bwd")
    gw2a = dw(h1t, dr1b, "ffn1_dw2", scale=0.5)
    (da1, db1, da1t, db1t), (l1,) = _ffn_bwd_act(dr1b, w2a, a1, b1, name="ffn1_dact",
                                                 comm=_pair_job([gw2a]))
    p_w2a = pair_add("ffn1_w2", gw2a, l1)
    gw1a, (land["ffn1_w2"],) = _mm(da1t, xb, out_dtype=BF16, tm=704, tk=T, name="ffn1_dw1",
                                   comm=_chips_job([p_w2a]))
    gw1a = shards(gw1a)
    gw3a, (l1,) = _mm(db1t, xb, out_dtype=BF16, tm=704, tk=T, name="ffn1_dw3", comm=_pair_job([gw1a]))
    gw3a = shards(gw3a)
    p_w1a = pair_add("ffn1_w1", gw1a, l1)
    (l1,) = _comm_call(_pair_job([gw3a]), "rs_pair_ffn1_w3")
    p_w3a = pair_add("ffn1_w3", gw3a, l1)
    gx, (land["ffn1_w1"], land["ffn1_w3"]) = _ffn_dx(
        da1, db1, w1a, w3a, dr1, name="ffn1_dx", comm=_chips_job([p_w1a, p_w3a]))
    return loss, gx, land, G


ANY = pl.BlockSpec(memory_space=pl.ANY)


def _place():
    x, y, c = lax.axis_index("x"), lax.axis_index("y"), lax.axis_index("c")
    chips = [(1 - x, y), (x, 1 - y), (1 - x, 1 - y)]
    return x, y, c, chips


def _comm_call(job, name):
    n_in, n_out = len(job.inputs), len(job.out_shape)

    def body(*refs):
        ins, outs, sems = refs[:n_in], refs[n_in:n_in + n_out], refs[n_in + n_out:]
        job.start(ins, outs, sems)
        job.middle(ins, outs, sems)
        job.finish(ins, outs, sems)

    return pl.pallas_call(body, name=name, out_shape=job.out_shape, in_specs=[ANY] * n_in,
                          out_specs=[ANY] * n_out, scratch_shapes=job.sems)(*job.inputs)


def _ag_job(parts):
    n = len(parts)
    halves = [p.shape[0] // 2 // 16 * 16 or p.shape[0] // 2 // 8 * 8 or p.shape[0] for p in parts]
    cut = [h < p.shape[0] for h, p in zip(halves, parts)]

    def helpers(ins, outs, sems):
        send_sems, recv_sems, local_sems = sems
        x, y, c, _ = _place()
        me, sibling = (x, y, c), (x, y, 1 - c)
        xn, yn, dg = (1 - x, y), (x, 1 - y), (1 - x, 1 - y)

        def slot(a, block, part=None):
            bx, by, bc = block
            ref = outs[a].at[4 * bx + 2 * by + bc]
            if part == 0:
                return ref.at[pl.ds(0, halves[a])]
            if part == 1:
                return ref.at[pl.ds(halves[a], parts[a].shape[0] - halves[a])]
            return ref

        def copy(a, k, block, to, src=None, part=None):
            dst = slot(a, block, part)
            return pltpu.make_async_remote_copy(
                src_ref=dst if src is None else src, dst_ref=dst,
                send_sem=send_sems.at[a, k], recv_sem=recv_sems.at[a, k],
                device_id=to, device_id_type=MESH)

        local = [pltpu.make_async_copy(ins[a], slot(a, me), local_sems.at[a]) for a in range(n)]
        return copy, local, me, sibling, xn, yn, dg, c

    def start(ins, outs, sems):
        copy, local, me, sibling, xn, yn, dg, c = helpers(ins, outs, sems)
        for cp in local:
            cp.start()
        for a in range(n):
            copy(a, 0, me, sibling, src=ins[a]).start()
            copy(a, 1, me, (*xn, c), src=ins[a]).start()
            copy(a, 2, me, (*yn, c), src=ins[a]).start()

    def middle(ins, outs, sems):
        copy, local, me, sibling, xn, yn, dg, c = helpers(ins, outs, sems)
        for a in range(n):
            copy(a, 1, (*xn, c), me).wait_recv()
            copy(a, 3, (*xn, c), (*yn, c), part=0).start()
            copy(a, 5, (*xn, c), sibling).start()
        for a in range(n):
            copy(a, 2, (*yn, c), me).wait_recv()
            if cut[a]:
                copy(a, 4, (*yn, c), (*xn, c), part=1).start()
            copy(a, 6, (*yn, c), sibling).start()

    def finish(ins, outs, sems):
        copy, local, me, sibling, xn, yn, dg, c = helpers(ins, outs, sems)
        for a in range(n):
            copy(a, 3, (*dg, c), me, part=0).wait_recv()
            if cut[a]:
                copy(a, 4, (*dg, c), me, part=1).wait_recv()
            copy(a, 7, (*dg, c), sibling).start()
        for a in range(n):
            copy(a, 0, sibling, me).wait_recv()
            copy(a, 5, (*xn, 1 - c), me).wait_recv()
            copy(a, 6, (*yn, 1 - c), me).wait_recv()
            copy(a, 7, (*dg, 1 - c), me).wait_recv()
        for a in range(n):
            copy(a, 0, me, sibling, src=ins[a]).wait_send()
            copy(a, 1, me, (*xn, c), src=ins[a]).wait_send()
            copy(a, 2, me, (*yn, c), src=ins[a]).wait_send()
            copy(a, 3, (*xn, c), (*yn, c), part=0).wait_send()
            if cut[a]:
                copy(a, 4, (*yn, c), (*xn, c), part=1).wait_send()
            for k, blk in ((5, xn), (6, yn), (7, dg)):
                copy(a, k, (*blk, c), sibling).wait_send()
        for cp in local:
            cp.wait()

    return _Job(parts, [jax.ShapeDtypeStruct((N_DEV,) + p.shape, p.dtype) for p in parts],
                [pltpu.SemaphoreType.DMA((n, 8)), pltpu.SemaphoreType.DMA((n, 8)),
                 pltpu.SemaphoreType.DMA((n,))], start, finish, middle)


def _pair_job(gs):
    n = len(gs)

    def copies(g_refs, land_refs, sems):
        send_sems, recv_sems = sems
        x, y, c, _ = _place()
        return [pltpu.make_async_remote_copy(
            src_ref=g_refs[a].at[2 * k + 1 - c], dst_ref=land_refs[a].at[k],
            send_sem=send_sems.at[a, k], recv_sem=recv_sems.at[a, k],
            device_id=(x, y, 1 - c), device_id_type=MESH) for a in range(n) for k in range(4)]

    def start(ins, outs, sems):
        for cp in copies(ins, outs, sems):
            cp.start()

    def finish(ins, outs, sems):
        for cp in copies(ins, outs, sems):
            cp.wait()

    return _Job(gs, [jax.ShapeDtypeStruct((4,) + g.shape[1:], g.dtype) for g in gs],
                [pltpu.SemaphoreType.DMA((n, 4)), pltpu.SemaphoreType.DMA((n, 4))], start, finish)


def _chips_job(ps):
    n = len(ps)

    def copies(p_refs, land_refs, sems):
        send_sems, recv_sems, local_sems = sems
        x, y, c, chips = _place()
        mine = 2 * x + y
        owns = [pltpu.make_async_copy(p_refs[a].at[mine], land_refs[a].at[mine], local_sems.at[a])
                for a in range(n)]
        sends = [pltpu.make_async_remote_copy(
            src_ref=p_refs[a].at[2 * chip[0] + chip[1]], dst_ref=land_refs[a].at[mine],
            send_sem=send_sems.at[a, j], recv_sem=recv_sems.at[a, j],
            device_id=(*chip, c), device_id_type=MESH) for a in range(n) for j, chip in enumerate(chips)]
        recvs = [pltpu.make_async_remote_copy(
            src_ref=p_refs[a].at[mine], dst_ref=land_refs[a].at[2 * chip[0] + chip[1]],
            send_sem=send_sems.at[a, j], recv_sem=recv_sems.at[a, j],
            device_id=(*chip, c), device_id_type=MESH) for a in range(n) for j, chip in enumerate(chips)]
        return owns, sends, recvs

    def start(ins, outs, sems):
        owns, sends, _ = copies(ins, outs, sems)
        for cp in owns + sends:
            cp.start()

    def finish(ins, outs, sems):
        owns, sends, recvs = copies(ins, outs, sems)
        for cp in recvs:
            cp.wait_recv()
        for cp in sends:
            cp.wait_send()
        for cp in owns:
            cp.wait()

    return _Job(ps, [jax.ShapeDtypeStruct(p.shape, p.dtype) for p in ps],
                [pltpu.SemaphoreType.DMA((n, 3)), pltpu.SemaphoreType.DMA((n, 3)),
                 pltpu.SemaphoreType.DMA((n,))], start, finish)


def _all_reduce_small(buf, name):
    rows = buf.shape[0]

    def body(b_ref, o_ref, slots, send_sems, recv_sems):
        x, y, c, _ = _place()
        me = 4 * x + 2 * y + c
        slots[me] = b_ref[...]
        cps = []
        for k in range(1, N_DEV):
            fx, fy, fc = (k >> 2) & 1, (k >> 1) & 1, k & 1
            peer = (x ^ fx, y ^ fy, c ^ fc)
            cps.append(pltpu.make_async_remote_copy(
                src_ref=b_ref, dst_ref=slots.at[me],
                send_sem=send_sems.at[k - 1], recv_sem=recv_sems.at[k - 1],
                device_id=peer, device_id_type=MESH))
        for cp in cps:
            cp.start()
        for k in range(1, N_DEV):
            fx, fy, fc = (k >> 2) & 1, (k >> 1) & 1, k & 1
            src = 4 * (x ^ fx) + 2 * (y ^ fy) + (c ^ fc)
            pltpu.make_async_remote_copy(
                src_ref=b_ref, dst_ref=slots.at[src],
                send_sem=send_sems.at[k - 1], recv_sem=recv_sems.at[k - 1],
                device_id=(x ^ fx, y ^ fy, c ^ fc), device_id_type=MESH).wait_recv()
        for cp in cps:
            cp.wait_send()
        acc = slots[0]
        for k in range(1, N_DEV):
            acc = acc + slots[k]
        o_ref[...] = acc

    vm = pl.BlockSpec(memory_space=pltpu.VMEM)
    return pl.pallas_call(
        body, name=name, out_shape=jax.ShapeDtypeStruct(buf.shape, F32),
        in_specs=[vm], out_specs=vm,
        scratch_shapes=[pltpu.VMEM((N_DEV, rows, LANES), F32),
                        pltpu.SemaphoreType.DMA((N_DEV - 1,)), pltpu.SemaphoreType.DMA((N_DEV - 1,))],
    )(buf)


def _row_tile(rows, pref=2048):
    for t in range(min(pref, rows) - min(pref, rows) % 8, 0, -8):
        if rows % t == 0:
            return t
    return rows


def _tile2(rows, cols, nbuf):
    budget = VMEM_LIMIT // 2 // (2 * nbuf * 4)
    for t in range(min(512, rows) // 16 * 16, 0, -16):
        if rows % t == 0 and t * cols <= budget:
            return t, cols
    return rows, _pick(cols, max(LANES, budget // rows // LANES * LANES))


def _pair_add(g, land, core, name):
    _, r, c = g.shape
    tr, tc = _tile2(r, c, 3)

    def body(core_ref, g_ref, l_ref, o_ref):
        o_ref[...] = (g_ref[...].astype(F32) + l_ref[...].astype(F32)).astype(o_ref.dtype)

    blk = pl.BlockSpec((1, tr, tc), lambda k, i, j, cr: (k, i, j))
    return pl.pallas_call(
        body, name=name, out_shape=jax.ShapeDtypeStruct(land.shape, land.dtype),
        grid_spec=pltpu.PrefetchScalarGridSpec(
            num_scalar_prefetch=1, grid=(4, r // tr, c // tc),
            in_specs=[pl.BlockSpec((1, tr, tc), lambda k, i, j, cr: (2 * k + cr[0], i, j)), blk],
            out_specs=blk),
        compiler_params=_params("parallel", "parallel", "parallel"),
    )(core, g, land)


def _adamw(w, parts, m, v, name):
    shp = w.shape
    cols = shp[-1]
    w2, m2, v2 = (t.reshape(-1, cols) for t in (w, m, v))
    rows = w2.shape[0]
    n = parts.shape[0]
    assert parts.shape[1:] == (rows, cols), (parts.shape, shp)
    tr, tc = _tile2(rows, cols, n + 7)
    c1 = 1.0 / (1.0 - ADAM_B1 ** ADAM_STEP)
    c2 = 1.0 / (1.0 - ADAM_B2 ** ADAM_STEP)

    def body(*refs):
        p_refs = refs[:n]
        w_ref, m_ref, v_ref, g_ref, d_ref, nm_ref, nv_ref = refs[n:]
        gv = p_refs[0][0].astype(F32)
        for p_ref in p_refs[1:]:
            gv = gv + p_ref[0].astype(F32)
        nm = ADAM_B1 * m_ref[...] + (1.0 - ADAM_B1) * gv
        nv = ADAM_B2 * v_ref[...] + (1.0 - ADAM_B2) * (gv * gv)
        g_ref[...] = gv
        nm_ref[...] = nm
        nv_ref[...] = nv
        d_ref[...] = -ADAM_LR * ((nm * c1) / (jnp.sqrt(nv * c2) + ADAM_EPS) + ADAM_WD * w_ref[...])

    blk = pl.BlockSpec((tr, tc), lambda i, j: (i, j))
    slab = lambda k: pl.BlockSpec((1, tr, tc), lambda i, j: (k, i, j))
    out = jax.ShapeDtypeStruct((rows, cols), F32)
    outs = pl.pallas_call(
        body, name=name, out_shape=[out] * 4, grid=(rows // tr, cols // tc),
        in_specs=[slab(k) for k in range(n)] + [blk] * 3, out_specs=[blk] * 4,
        compiler_params=_params("parallel", "parallel"),
    )(*([parts] * n), w2, m2, v2)
    return tuple(o.reshape(shp) for o in outs)


def _pack_rows(arrs):
    rows = []
    for a in arrs:
        flat = a.reshape(-1).astype(F32)
        pad = (-flat.shape[0]) % LANES
        if pad:
            flat = jnp.concatenate([flat, jnp.zeros((pad,), F32)])
        rows.append(flat.reshape(-1, LANES))
    out = jnp.concatenate(rows, axis=0)
    pad = (-out.shape[0]) % 8
    if pad:
        out = jnp.concatenate([out, jnp.zeros((pad, LANES), F32)], axis=0)
    return out


def _unpack_rows(buf, shapes):
    outs, r = [], 0
    for s in shapes:
        n = math.prod(s)
        nr = -(-n // LANES)
        outs.append(buf[r:r + nr].reshape(-1)[:n].reshape(s))
        r += nr
    return outs


BIG = ("ffn1_w1", "ffn1_w3", "ffn1_w2", "w_in", "w_out", "ffn2_w1", "ffn2_w3", "ffn2_w2")
ROW_SHARDED = ("ffn1_w2", "w_out", "ffn2_w2")
SMALL = ("ln1_g", "ln1_b", "hgrn_lb", "hgrn_norm_g", "mlstm_conv_w", "mlstm_conv_b", "mlstm_ig_b",
         "mlstm_fg_b", "mlstm_norm_g", "ln2_g", "ln2_b", "ln3_g", "ln3_b")
WEIGHTS = ("ffn1_w1", "ffn1_w3", "ffn1_w2", "ln1_g", "ln1_b", "w_in", "hgrn_lb", "hgrn_norm_g",
           "mlstm_conv_w", "mlstm_conv_b", "mlstm_ig_b", "mlstm_fg_b", "mlstm_norm_g", "w_out",
           "ln2_g", "ln2_b", "ffn2_w1", "ffn2_w3", "ffn2_w2", "ln3_g", "ln3_b")


def kernel(x, ffn1_w1, ffn1_w3, ffn1_w2, ln1_g, ln1_b, w_in, hgrn_lb, hgrn_norm_g, mlstm_conv_w, mlstm_conv_b, mlstm_ig_b, mlstm_fg_b, mlstm_norm_g, w_out, ln2_g, ln2_b, ffn2_w1, ffn2_w3, ffn2_w2, ln3_g, ln3_b, loss_target, m_ffn1_w1, m_ffn1_w3, m_ffn1_w2, m_ln1_g, m_ln1_b, m_w_in, m_hgrn_lb, m_hgrn_norm_g, m_mlstm_conv_w, m_mlstm_conv_b, m_mlstm_ig_b, m_mlstm_fg_b, m_mlstm_norm_g, m_w_out, m_ln2_g, m_ln2_b, m_ffn2_w1, m_ffn2_w3, m_ffn2_w2, m_ln3_g, m_ln3_b, v_ffn1_w1, v_ffn1_w3, v_ffn1_w2, v_ln1_g, v_ln1_b, v_w_in, v_hgrn_lb, v_hgrn_norm_g, v_mlstm_conv_w, v_mlstm_conv_b, v_mlstm_ig_b, v_mlstm_fg_b, v_mlstm_norm_g, v_w_out, v_ln2_g, v_ln2_b, v_ffn2_w1, v_ffn2_w3, v_ffn2_w2, v_ln3_g, v_ln3_b):
    args = (ffn1_w1, ffn1_w3, ffn1_w2, ln1_g, ln1_b, w_in, hgrn_lb, hgrn_norm_g, mlstm_conv_w,
            mlstm_conv_b, mlstm_ig_b, mlstm_fg_b, mlstm_norm_g, w_out, ln2_g, ln2_b, ffn2_w1, ffn2_w3,
            ffn2_w2, ln3_g, ln3_b)
    ms = (m_ffn1_w1, m_ffn1_w3, m_ffn1_w2, m_ln1_g, m_ln1_b, m_w_in, m_hgrn_lb, m_hgrn_norm_g,
          m_mlstm_conv_w, m_mlstm_conv_b, m_mlstm_ig_b, m_mlstm_fg_b, m_mlstm_norm_g, m_w_out, m_ln2_g,
          m_ln2_b, m_ffn2_w1, m_ffn2_w3, m_ffn2_w2, m_ln3_g, m_ln3_b)
    vs = (v_ffn1_w1, v_ffn1_w3, v_ffn1_w2, v_ln1_g, v_ln1_b, v_w_in, v_hgrn_lb, v_hgrn_norm_g,
          v_mlstm_conv_w, v_mlstm_conv_b, v_mlstm_ig_b, v_mlstm_fg_b, v_mlstm_norm_g, v_w_out, v_ln2_g,
          v_ln2_b, v_ffn2_w1, v_ffn2_w3, v_ffn2_w2, v_ln3_g, v_ln3_b)
    w = dict(zip(WEIGHTS, args))
    m = dict(zip(WEIGHTS, ms))
    v = dict(zip(WEIGHTS, vs))
    core = lax.axis_index("c")
    dev = 4 * lax.axis_index("x") + 2 * lax.axis_index("y") + core
    p_lb = []

    def small_params(gathered_small):
        lb_sh, cw_sh = zip(*[_unpack_rows(gathered_small[i], [hgrn_lb.shape, mlstm_conv_w.shape])
                             for i in range(N_DEV)])
        hgrn_lb_full = jnp.concatenate(lb_sh, axis=-1)
        conv_w_full = jnp.concatenate(cw_sh, axis=-1)[0]
        p_lb.append(jax.nn.softmax(hgrn_lb_full, axis=1))
        return dict(ln1_g=ln1_g, ln1_b=ln1_b, ln2_g=ln2_g, ln2_b=ln2_b, ln3_g=ln3_g, ln3_b=ln3_b,
                    lb=p_lb[0][:, 0], hgrn_norm_g=hgrn_norm_g, conv_w=conv_w_full, conv_b=mlstm_conv_b,
                    ig_b=mlstm_ig_b[0], fg_b=mlstm_fg_b[0], mlstm_norm_g=mlstm_norm_g)

    def rows_first(t, k):
        return t[0] if k in ROW_SHARDED else t[0].T

    loss_row, grad_x, land2, G = _device_step(
        x[0], loss_target[0], {k: rows_first(w[k], k).astype(BF16) for k in BIG},
        _pack_rows([hgrn_lb, mlstm_conv_w]), small_params, jnp.reshape(core, (1,)).astype(jnp.int32))
    p_lb = p_lb[0]

    dlb = G["lb"]
    g_lb = jnp.stack([dlb * p_lb[:, 0] * (1.0 - p_lb[:, 0]), -dlb * p_lb[:, 0] * p_lb[:, 1]], axis=1)
    small_full = {"ln1_g": G["ln1_g"], "ln1_b": G["ln1_b"], "hgrn_lb": g_lb, "hgrn_norm_g": G["hgrn_norm_g"],
                  "mlstm_conv_w": G["conv_w"][None], "mlstm_conv_b": G["conv_b"],
                  "mlstm_ig_b": G["ig_b"][None], "mlstm_fg_b": G["fg_b"][None],
                  "mlstm_norm_g": G["mlstm_norm_g"], "ln2_g": G["ln2_g"], "ln2_b": G["ln2_b"],
                  "ln3_g": G["ln3_g"], "ln3_b": G["ln3_b"]}
    small_list = [small_full[k] for k in SMALL] + [loss_row]
    reduced = _all_reduce_small(_pack_rows(small_list), "all_reduce_small")
    red = _unpack_rows(reduced, [a.shape for a in small_list])
    loss = red[-1][0, 0]
    small_g = {}
    for k, gk in zip(SMALL, red[:-1]):
        if k in ("hgrn_lb", "mlstm_conv_w"):
            n = w[k].shape[-1]
            gk = lax.dynamic_slice_in_dim(gk, dev * n, n, axis=gk.ndim - 1)
        small_g[k] = gk

    grads, delta, new_m, new_v = {}, {}, {}, {}
    for k in BIG:
        res = _adamw(rows_first(w[k], k), land2[k], rows_first(m[k], k), rows_first(v[k], k), "adamw_" + k)
        grads[k], delta[k], new_m[k], new_v[k] = [(r if k in ROW_SHARDED else r.T)[None] for r in res]
    sm = _adamw(_pack_rows([w[k] for k in SMALL]), _pack_rows([small_g[k] for k in SMALL])[None],
                _pack_rows([m[k] for k in SMALL]), _pack_rows([v[k] for k in SMALL]), "adamw_small")
    shapes = [w[k].shape for k in SMALL]
    for dst, buf in zip((grads, delta, new_m, new_v), sm):
        for k, val in zip(SMALL, _unpack_rows(buf, shapes)):
            dst[k] = val
    return (loss, grad_x[None], *[grads[k] for k in WEIGHTS], *[delta[k] for k in WEIGHTS],
            *[new_m[k] for k in WEIGHTS], *[new_v[k] for k in WEIGHTS])
```

```python
import math

import jax
import jax.numpy as jnp
from jax import lax
from jax.experimental import pallas as pl
from jax.experimental.pallas import tpu as pltpu

F32 = jnp.float32
BF16 = jnp.bfloat16

CHUNK = 64
HGRN_HEAD_DIM = 128
CONV_WIDTH = 5
DN_ALPHA = 2.0 ** 0.25
LN_EPS = 1e-5
NORM_EPS = 1e-6
M_INIT = -1e30
NEG = -1e30
EXP_CLAMP = 80.0
ADAM_LR = 0.001
ADAM_B1 = 0.9
ADAM_B2 = 0.999
ADAM_EPS = 1e-08
ADAM_WD = 0.01
ADAM_STEP = 10
N_DEV = 8
LANES = 128
VMEM_LIMIT = 56 * 1024 * 1024
MESH = pl.DeviceIdType.MESH


def _params(*sem):
    return pltpu.CompilerParams(dimension_semantics=sem, vmem_limit_bytes=VMEM_LIMIT)


class _Job:
    def __init__(self, inputs, out_shape, sems, start, finish, middle=None):
        self.inputs, self.out_shape, self.sems = list(inputs), list(out_shape), list(sems)
        self.start, self.finish = start, finish
        self.middle = middle or (lambda ins, outs, sems: None)


def _join(*jobs):
    def split(refs, counts):
        out, p = [], 0
        for c in counts:
            out.append(refs[p:p + c])
            p += c
        return out

    n_in = [len(j.inputs) for j in jobs]
    n_out = [len(j.out_shape) for j in jobs]
    n_sem = [len(j.sems) for j in jobs]

    def start(ins, outs, sems):
        for j, i, o, s in zip(jobs, split(ins, n_in), split(outs, n_out), split(sems, n_sem)):
            j.start(i, o, s)

    def middle(ins, outs, sems):
        for j, i, o, s in zip(jobs, split(ins, n_in), split(outs, n_out), split(sems, n_sem)):
            j.middle(i, o, s)

    def finish(ins, outs, sems):
        for j, i, o, s in zip(jobs, split(ins, n_in), split(outs, n_out), split(sems, n_sem)):
            j.finish(i, o, s)

    return _Job(sum((j.inputs for j in jobs), []), sum((j.out_shape for j in jobs), []),
                sum((j.sems for j in jobs), []), start, finish, middle)


def _pcall(body, *, name, out_shape, grid, in_specs, out_specs, sem, scratch_shapes=(), comm=None):
    single = not isinstance(out_shape, (list, tuple))
    out_shape = [out_shape] if single else list(out_shape)
    out_specs = [out_specs] if single else list(out_specs)
    in_specs, scratch_shapes = list(in_specs), list(scratch_shapes)
    if comm is None:
        call = pl.pallas_call(body, name=name, out_shape=out_shape, grid=grid, in_specs=in_specs,
                              out_specs=out_specs, scratch_shapes=scratch_shapes,
                              compiler_params=_params(*sem))

        def run(*args):
            outs = call(*args)
            return outs[0] if single else outs
        return run

    n_in, n_out, n_sc = len(in_specs), len(out_shape), len(scratch_shapes)
    c_in, c_out = len(comm.inputs), len(comm.out_shape)

    def hosted(*refs):
        ins, cins = refs[:n_in], refs[n_in:n_in + c_in]
        p = n_in + c_in
        outs, couts = refs[p:p + n_out], refs[p + n_out:p + n_out + c_out]
        p += n_out + c_out
        scratch, csems = refs[p:p + n_sc], refs[p + n_sc:]
        first = pl.program_id(0) == 0
        half = pl.program_id(0) == grid[0] // 2
        last = pl.program_id(0) == grid[0] - 1
        for d in range(1, len(grid)):
            first = first & (pl.program_id(d) == 0)
            half = half & (pl.program_id(d) == 0)
            last = last & (pl.program_id(d) == grid[d] - 1)

        @pl.when(first)
        def _():
            comm.start(cins, couts, csems)

        if grid[0] >= 2:
            @pl.when(half)
            def _():
                comm.middle(cins, couts, csems)

        body(*ins, *outs, *scratch)

        @pl.when(last)
        def _():
            if grid[0] < 2:
                comm.middle(cins, couts, csems)
            comm.finish(cins, couts, csems)

    any_spec = pl.BlockSpec(memory_space=pl.ANY)
    call = pl.pallas_call(
        hosted, name=name, out_shape=out_shape + comm.out_shape, grid=grid,
        in_specs=in_specs + [any_spec] * c_in, out_specs=out_specs + [any_spec] * c_out,
        scratch_shapes=scratch_shapes + comm.sems,
        compiler_params=_params(*(["arbitrary"] * len(grid))))

    def run(*args):
        res = call(*args, *comm.inputs)
        outs, couts = res[:n_out], list(res[n_out:])
        return (outs[0] if single else outs), couts
    return run


def _sigmoid(x):
    return 1.0 / (1.0 + jnp.exp(-x))


def _log_sigmoid(x):
    return jnp.minimum(x, 0.0) - jnp.log(1.0 + jnp.exp(-jnp.abs(x)))


def _dot(a, b, dims):
    return lax.dot_general(a.astype(BF16), b.astype(BF16), (dims, ((), ())),
                           preferred_element_type=F32)


def _dot3(a, b, dims):
    ah = a.astype(BF16)
    al = (a - ah.astype(F32)).astype(BF16)
    bh = b.astype(BF16)
    bl = (b - bh.astype(F32)).astype(BF16)
    d = (dims, ((), ()))
    out = lax.dot_general(ah, bh, d, preferred_element_type=F32)
    out = out + lax.dot_general(ah, bl, d, preferred_element_type=F32)
    return out + lax.dot_general(al, bh, d, preferred_element_type=F32)


_DIMS = {"nn": ((1,), (0,)), "nt": ((1,), (1,)), "tn": ((0,), (0,))}


def _dot_nn(a, b):
    return _dot3(a, b, _DIMS["nn"])


def _dot_nt(a, b):
    return _dot3(a, b, _DIMS["nt"])


def _dot_tn(a, b):
    return _dot3(a, b, _DIMS["tn"])


def _split3(x):
    hi = x.astype(BF16)
    r1 = x - hi.astype(F32)
    mid = r1.astype(BF16)
    lo = (r1 - mid.astype(F32)).astype(BF16)
    return hi, mid, lo


def _dot01(mask01, x, mode="nn"):
    m = mask01.astype(BF16)
    hi, mid, lo = _split3(x)
    d = (_DIMS[mode], ((), ()))
    out = lax.dot_general(m, hi, d, preferred_element_type=F32)
    out = out + lax.dot_general(m, mid, d, preferred_element_type=F32)
    return out + lax.dot_general(m, lo, d, preferred_element_type=F32)


def _matmul(a, b, *, mode="nn", out_dtype=F32, tm=512, tn=512, tk=None,
            add=None, scale=1.0, add_scale=1.0, name, comm=None, rows=None):
    if mode == "nn":
        (M, K), (K2, N) = a.shape, b.shape
        K2 = rows or K2
    elif mode == "nt":
        (M, K), (N, K2) = a.shape, b.shape
        N = rows or N
    else:
        (K, M), (K2, N) = a.shape, b.shape
    assert K == K2, (a.shape, b.shape, mode)
    tm, tn = min(tm, M), min(tn, N)
    tk = min(tk or K, K)
    assert M % tm == 0 and N % tn == 0 and K % tk == 0, (M, N, K, tm, tn, tk)
    nk = K // tk
    dims = _DIMS[mode]
    has_add = add is not None

    def body(*refs):
        if has_add:
            a_ref, b_ref, add_ref, o_ref = refs[:4]
        else:
            a_ref, b_ref, o_ref = refs[:3]
            add_ref = None
        acc_ref = refs[-1] if nk > 1 else None

        def finish(acc):
            out = acc if scale == 1.0 else acc * scale
            if has_add:
                out = out + add_ref[...].astype(F32) * add_scale
            o_ref[...] = out.astype(o_ref.dtype)

        if nk == 1:
            finish(_dot(a_ref[...], b_ref[...], dims))
        else:
            k = pl.program_id(2)

            @pl.when(k == 0)
            def _():
                acc_ref[...] = jnp.zeros_like(acc_ref)

            acc_ref[...] += _dot(a_ref[...], b_ref[...], dims)

            @pl.when(k == nk - 1)
            def _():
                finish(acc_ref[...])

    if mode == "tn":
        a_spec = pl.BlockSpec((tk, tm), lambda i, j, k: (k, i))
    else:
        a_spec = pl.BlockSpec((tm, tk), lambda i, j, k: (i, k))
    if mode == "nt":
        b_spec = pl.BlockSpec((tn, tk), lambda i, j, k: (j, k))
    else:
        b_spec = pl.BlockSpec((tk, tn), lambda i, j, k: (k, j))
    o_spec = pl.BlockSpec((tm, tn), lambda i, j, k: (i, j))
    in_specs = [a_spec, b_spec] + ([o_spec] if has_add else [])
    args = (a, b) + ((add,) if has_add else ())
    return _pcall(
        body, name=name, comm=comm,
        out_shape=jax.ShapeDtypeStruct((M, N), out_dtype),
        grid=(M // tm, N // tn, nk),
        in_specs=in_specs, out_specs=o_spec,
        scratch_shapes=[pltpu.VMEM((tm, tn), F32)] if nk > 1 else [],
        sem=("parallel", "parallel", "arbitrary"),
    )(*args)


def _ln_stats(x):
    mu = jnp.mean(x, axis=-1, keepdims=True)
    xc = x - mu
    var = jnp.mean(xc * xc, axis=-1, keepdims=True)
    rstd = lax.rsqrt(var + LN_EPS)
    return xc * rstd, rstd


def _ln_fwd(r, g, b, *, tm=256, name):
    T, D = r.shape
    tm = min(tm, T)

    def body(r_ref, g_ref, b_ref, y_ref, yb_ref):
        xhat, _ = _ln_stats(r_ref[...])
        y = xhat * g_ref[...] + b_ref[...]
        y_ref[...] = y
        yb_ref[...] = y.astype(BF16)

    row = pl.BlockSpec((tm, D), lambda i: (i, 0))
    vec = pl.BlockSpec((1, D), lambda i: (0, 0))
    return pl.pallas_call(
        body, name=name,
        out_shape=[jax.ShapeDtypeStruct((T, D), F32), jax.ShapeDtypeStruct((T, D), BF16)],
        grid=(T // tm,), in_specs=[row, vec, vec], out_specs=[row, row],
        compiler_params=_params("parallel"),
    )(r, g, b)


def _cast(x, *, tm=512, name, comm=None):
    T, D = x.shape
    tm = min(tm, T)

    def body(x_ref, xb_ref):
        xb_ref[...] = x_ref[...].astype(BF16)

    row = pl.BlockSpec((tm, D), lambda i: (i, 0))
    return _pcall(
        body, name=name, comm=comm, out_shape=jax.ShapeDtypeStruct((T, D), BF16),
        grid=(T // tm,), in_specs=[row], out_specs=row, sem=("parallel",),
    )(x)


def _ln_bwd(dy, r, g, *, tm=256, name, b=None, target=None):
    T, D = r.shape
    tm = min(tm, T)
    with_loss = target is not None

    def body(*refs):
        if with_loss:
            r_ref, g_ref, b_ref, t_ref, dr_ref, drb_ref, dg_ref, db_ref, loss_ref = refs
        else:
            dy_ref, r_ref, g_ref, dr_ref, drb_ref, dg_ref, db_ref = refs
        i = pl.program_id(0)
        xhat, rstd = _ln_stats(r_ref[...])
        gg = g_ref[...]
        if with_loss:
            err = xhat * gg + b_ref[...] - t_ref[...]
            dyv = err * (1.0 / D)
            part = jnp.sum(jnp.sum(err * err, axis=1, keepdims=True), axis=0, keepdims=True)
            part = jnp.broadcast_to(part * (0.5 / D), (1, LANES))
        else:
            dyv = dy_ref[...]
        dxh = dyv * gg
        m1 = jnp.mean(dxh, axis=-1, keepdims=True)
        m2 = jnp.mean(dxh * xhat, axis=-1, keepdims=True)
        dr = rstd * (dxh - m1 - xhat * m2)
        dr_ref[...] = dr
        drb_ref[...] = dr.astype(BF16)
        dgp = jnp.sum(dyv * xhat, axis=0, keepdims=True)
        dbp = jnp.sum(dyv, axis=0, keepdims=True)

        @pl.when(i == 0)
        def _():
            dg_ref[...] = dgp
            db_ref[...] = dbp
            if with_loss:
                loss_ref[...] = part

        @pl.when(i > 0)
        def _():
            dg_ref[...] += dgp
            db_ref[...] += dbp
            if with_loss:
                loss_ref[...] += part

    row = pl.BlockSpec((tm, D), lambda i: (i, 0))
    vec = pl.BlockSpec((1, D), lambda i: (0, 0))
    out_shape = [jax.ShapeDtypeStruct((T, D), F32), jax.ShapeDtypeStruct((T, D), BF16),
                 jax.ShapeDtypeStruct((1, D), F32), jax.ShapeDtypeStruct((1, D), F32)]
    out_specs = [row, row, vec, vec]
    if with_loss:
        in_specs, args = [row, vec, vec, row], (r, g, b, target)
        out_shape.append(jax.ShapeDtypeStruct((1, LANES), F32))
        out_specs.append(pl.BlockSpec((1, LANES), lambda i: (0, 0)))
    else:
        in_specs, args = [row, row, vec], (dy, r, g)
    return pl.pallas_call(
        body, name=name, out_shape=out_shape, grid=(T // tm,),
        in_specs=in_specs, out_specs=out_specs,
        compiler_params=_params("arbitrary"),
    )(*args)


def _ffn_up(xb, w1t, w3t, *, a=None, tm=1024, tf=512, name, comm=None):
    F, D = w3t.shape
    T = xb.shape[0]
    tm, tf = _pick(T, tm), _pick(F, tf)
    given = a is not None

    def body(x_ref, w1_ref, w3_ref, a_ref, b_ref, h_ref, ht_ref):
        xv = x_ref[...]
        av = w1_ref[...].astype(F32) if given else _dot(xv, w1_ref[...], _DIMS["nt"])
        b = _dot(xv, w3_ref[...], _DIMS["nt"])
        a_ref[...] = av.astype(BF16)
        b_ref[...] = b.astype(BF16)
        h = av * _sigmoid(av) * b
        h_ref[...] = h.astype(BF16)
        ht_ref[...] = h.T.astype(BF16)

    ij = (lambda p, q: (p, q)) if given else (lambda p, q: (q, p))
    wsp = pl.BlockSpec((tf, D), lambda p, q: (ij(p, q)[1], 0))
    osp = pl.BlockSpec((tm, tf), lambda p, q: ij(p, q))
    act = jax.ShapeDtypeStruct((T, F), BF16)
    return _pcall(
        body, name=name, comm=comm, out_shape=[act, act, act, jax.ShapeDtypeStruct((F, T), BF16)],
        grid=(T // tm, F // tf) if given else (F // tf, T // tm),
        in_specs=[pl.BlockSpec((tm, D), lambda p, q: (ij(p, q)[0], 0)), osp if given else wsp, wsp],
        out_specs=[osp, osp, osp, pl.BlockSpec((tf, tm), lambda p, q: ij(p, q)[::-1])],
        sem=("parallel", "parallel"),
    )(xb, a if given else w1t, w3t)


def _ffn_bwd_act(drb, w2, a, b, *, tm=1024, tf=512, name, comm=None):
    F, D = w2.shape
    T = drb.shape[0]
    tm, tf = _pick(T, tm), _pick(F, tf)

    def body(dr_ref, w2_ref, a_ref, b_ref, da_ref, db_ref, dat_ref, dbt_ref):
        d = 0.5 * _dot(dr_ref[...], w2_ref[...], _DIMS["nt"])
        av = a_ref[...].astype(F32)
        sig = _sigmoid(av)
        da = d * b_ref[...].astype(F32) * sig * (1.0 + av * (1.0 - sig))
        db = d * av * sig
        da_ref[...] = da.astype(BF16)
        db_ref[...] = db.astype(BF16)
        dat_ref[...] = da.T.astype(BF16)
        dbt_ref[...] = db.T.astype(BF16)

    asp = pl.BlockSpec((tm, tf), lambda i, j: (i, j))
    tsp = pl.BlockSpec((tf, tm), lambda i, j: (j, i))
    out, out_t = jax.ShapeDtypeStruct((T, F), BF16), jax.ShapeDtypeStruct((F, T), BF16)
    return _pcall(
        body, name=name, comm=comm, out_shape=[out, out, out_t, out_t], grid=(T // tm, F // tf),
        in_specs=[pl.BlockSpec((tm, D), lambda i, j: (i, 0)),
                  pl.BlockSpec((tf, D), lambda i, j: (j, 0)), asp, asp],
        out_specs=[asp, asp, tsp, tsp],
        sem=("parallel", "parallel"),
    )(drb, w2, a, b)


def _dz_pack(pieces, *, tm=256, name):
    groups = [p if isinstance(p, tuple) else (p,) for p in pieces]
    T = groups[0][0].shape[0]
    widths = [g[0].shape[1] for g in groups]
    W = sum(widths)
    tm = _pick(T, tm)
    flat = [a for g in groups for a in g]
    n = len(flat)

    def body(*refs):
        o_ref, ot_ref = refs[n], refs[n + 1]
        c, p = 0, 0
        for g, w in zip(groups, widths):
            v = refs[p][...].astype(F32)
            for r in refs[p + 1:p + len(g)]:
                v = v + r[...].astype(F32)
            p += len(g)
            o_ref[:, c:c + w] = v.astype(BF16)
            ot_ref[c:c + w, :] = v.T.astype(BF16)
            c += w

    return pl.pallas_call(
        body, name=name,
        out_shape=[jax.ShapeDtypeStruct((T, W), BF16), jax.ShapeDtypeStruct((W, T), BF16)],
        grid=(T // tm,),
        in_specs=[pl.BlockSpec((tm, a.shape[1]), lambda i: (i, 0)) for a in flat],
        out_specs=[pl.BlockSpec((tm, W), lambda i: (i, 0)), pl.BlockSpec((W, tm), lambda i: (0, i))],
        compiler_params=_params("parallel"),
    )(*flat)


def _ffn_dx(da, db, w1t, w3t, dr, *, tm=1024, tn=1024, tk=None, name, comm=None):
    T, F = da.shape
    D = w1t.shape[1]
    tm, tn = _pick(T, tm), _pick(D, tn)
    tk = _pick(F, tk or F // 4)
    nk = F // tk

    def body(da_ref, db_ref, w1_ref, w3_ref, dr_ref, o_ref, acc_ref):
        k = pl.program_id(2)

        @pl.when(k == 0)
        def _():
            acc_ref[...] = jnp.zeros_like(acc_ref)

        acc_ref[...] += (_dot(da_ref[...], w1_ref[...], _DIMS["nn"])
                         + _dot(db_ref[...], w3_ref[...], _DIMS["nn"]))

        @pl.when(k == nk - 1)
        def _():
            o_ref[...] = DN_ALPHA * dr_ref[...] + acc_ref[...]

    asp = pl.BlockSpec((tm, tk), lambda i, j, k: (i, k))
    wsp = pl.BlockSpec((tk, tn), lambda i, j, k: (k, j))
    osp = pl.BlockSpec((tm, tn), lambda i, j, k: (i, j))
    return _pcall(
        body, name=name, comm=comm, out_shape=jax.ShapeDtypeStruct((T, D), F32),
        grid=(T // tm, D // tn, nk), in_specs=[asp, asp, wsp, wsp, osp], out_specs=osp,
        scratch_shapes=[pltpu.VMEM((tm, tn), F32)],
        sem=("parallel", "parallel", "arbitrary"),
    )(da, db, w1t, w3t, dr)


def _chunk_mask(reverse, transpose=False):
    row = lax.broadcasted_iota(jnp.int32, (CHUNK, CHUNK), 0)
    col = lax.broadcasted_iota(jnp.int32, (CHUNK, CHUNK), 1)
    if reverse != transpose:
        return col >= row
    return col <= row


def _run_specs(specs, *, name, comm=None):
    counts = [(len(s["in_specs"]), len(s["out_specs"]), len(s["scratch_shapes"])) for s in specs]

    def body(*refs):
        p, parts = 0, [[], [], []]
        for kind in range(3):
            for c in counts:
                parts[kind].append(refs[p:p + c[kind]])
                p += c[kind]
        live = [s["body"](*parts[0][k], *parts[1][k], *parts[2][k]) for k, s in enumerate(specs)]
        while live:
            for g in list(live):
                if next(g, StopIteration) is StopIteration:
                    live.remove(g)

    cat = lambda key: [v for s in specs for v in s[key]]
    res = _pcall(body, name=name, comm=comm, out_shape=cat("out_shape"), grid=specs[0]["grid"],
                 in_specs=cat("in_specs"), out_specs=cat("out_specs"),
                 scratch_shapes=cat("scratch_shapes"), sem=("arbitrary",))(*cat("args"))
    outs, couts = res if comm is not None else (res, None)
    split, p = [], 0
    for c in counts:
        split.append(outs[p:p + c[1]])
        p += c[1]
    return (split, couts) if comm is not None else split


def _hgrn_pre(hq, hf, lb):
    sig = _sigmoid(hf)
    f = lb + (1.0 - lb) * sig
    q = hq * _sigmoid(hq) * (HGRN_HEAD_DIM ** -0.5)
    return q, f, sig


def _hgrn_decays(f, cmf, reverse):
    bc = _dot01(cmf, jnp.log(f))
    last = 0 if reverse else CHUNK - 1
    blast = bc[last:last + 1, :]
    bref = bc[CHUNK // 2:CHUNK // 2 + 1, :]
    eq = jnp.exp(jnp.minimum(bc - bref, EXP_CLAMP))
    ek = jnp.exp(jnp.minimum(bref - bc, EXP_CLAMP))
    return bc, blast, eq, ek


def _hgrn_fwd(z, lb, *, wh, reverse):
    T = z.shape[0]
    nch = T // CHUNK
    nh = wh // HGRN_HEAD_DIM
    hd = HGRN_HEAD_DIM

    def ci(i):
        return nch - 1 - i if reverse else i

    def body(hq_ref, hi_ref, hf_ref, lb_ref, o_ref, st_ref, s_ref):
        @pl.when(pl.program_id(0) == 0)
        def _():
            s_ref[...] = jnp.zeros_like(s_ref)

        cm = _chunk_mask(reverse)
        cmf = cm.astype(F32)
        H = range(nh)
        sls = [slice(h * hd, (h + 1) * hd) for h in H]
        q, f, _ = _hgrn_pre(hq_ref[...], hf_ref[...], lb_ref[...])
        v = hi_ref[...]
        k = 1.0 - f
        yield
        bc, blast, eq, ek = _hgrn_decays(f, cmf, reverse)
        yield
        qh, kh, qe = q * eq, k * ek, q * jnp.exp(bc)
        k2, eblast = k * jnp.exp(blast - bc), jnp.exp(blast)
        st = [s_ref[h] for h in H]
        yield
        att = [jnp.where(cm, _dot_nt(qh[:, sl], kh[:, sl]), 0.0) for sl in sls]
        yield
        inter = [_dot(qe[:, sls[h]], st[h], _DIMS["nt"]) for h in H]
        yield
        out = [inter[h] + _dot(att[h], v[:, sls[h]], _DIMS["nn"]) for h in H]
        yield
        snew = [eblast[:, sls[h]] * st[h] + _dot_tn(v[:, sls[h]], k2[:, sls[h]]) for h in H]
        yield
        for h in H:
            st_ref[0, h] = st[h]
            o_ref[:, sls[h]] = out[h]
            s_ref[h] = snew[h]

    blk = lambda c: pl.BlockSpec((CHUNK, wh), lambda i: (ci(i), c))
    return dict(
        body=body, grid=(nch,), args=[z, z, z, lb],
        out_shape=[jax.ShapeDtypeStruct((T, wh), F32),
                   jax.ShapeDtypeStruct((nch, nh, hd, hd), F32)],
        in_specs=[blk(0), blk(1), blk(3 + int(reverse)),
                  pl.BlockSpec((1, wh), lambda i: (0, 0))],
        out_specs=[blk(0), pl.BlockSpec((1, nh, hd, hd), lambda i: (ci(i), 0, 0, 0))],
        scratch_shapes=[pltpu.VMEM((nh, hd, hd), F32)])


def _hgrn_bwd(z, lb, do, states, *, wh, reverse):
    T = z.shape[0]
    nch = T // CHUNK
    nh = wh // HGRN_HEAD_DIM
    hd = HGRN_HEAD_DIM

    def ci(i):
        return i if reverse else nch - 1 - i

    def body(hq_ref, hi_ref, hf_ref, lb_ref, do_ref, st_ref,
             dhq_ref, dhi_ref, dhf_ref, dlb_ref, ds_ref, gs_ref):
        @pl.when(pl.program_id(0) == 0)
        def _():
            ds_ref[...] = jnp.zeros_like(ds_ref)
            gs_ref[...] = jnp.zeros_like(gs_ref)
            dlb_ref[...] = jnp.zeros_like(dlb_ref)

        cm = _chunk_mask(reverse)
        cmf = cm.astype(F32)
        cmtf = _chunk_mask(reverse, transpose=True).astype(F32)
        H = range(nh)
        sls = [slice(h * hd, (h + 1) * hd) for h in H]
        hq, lb = hq_ref[...], lb_ref[...]
        q, f, sig = _hgrn_pre(hq, hf_ref[...], lb)
        v, dov = hi_ref[...], do_ref[...]
        k = 1.0 - f
        yield
        bc, blast, eq, ek = _hgrn_decays(f, cmf, reverse)
        yield
        ebc, eb2, eblast = jnp.exp(bc), jnp.exp(blast - bc), jnp.exp(blast)
        qh, kh, qe, k2 = q * eq, k * ek, q * ebc, k * eb2
        st = [st_ref[0, h] for h in H]
        dst = [ds_ref[h] for h in H]
        yield
        att = [jnp.where(cm, _dot(qh[:, sl], kh[:, sl], _DIMS["nt"]), 0.0) for sl in sls]
        yield
        datt = [jnp.where(cm, _dot_nt(dov[:, sl], v[:, sl]), 0.0) for sl in sls]
        yield
        dq_a = [_dot_nn(datt[h], kh[:, sls[h]]) for h in H]
        yield
        dq_s = [_dot_nn(dov[:, sls[h]], st[h]) for h in H]
        yield
        dk_a = [_dot_tn(datt[h], qh[:, sls[h]]) for h in H]
        yield
        dk_s = [_dot_nn(v[:, sls[h]], dst[h]) for h in H]
        yield
        dv = [_dot(att[h], dov[:, sls[h]], _DIMS["tn"]) + _dot(k2[:, sls[h]], dst[h], _DIMS["nt"])
              for h in H]
        yield
        dsn = [eblast[:, sls[h]] * dst[h] + _dot_tn(dov[:, sls[h]], qe[:, sls[h]]) for h in H]
        for h in H:
            ds_ref[h] = dsn[h]
        yield
        dq = jnp.concatenate(dq_a, axis=1) * eq + ebc * jnp.concatenate(dq_s, axis=1)
        dk = jnp.concatenate(dk_a, axis=1) * ek + eb2 * jnp.concatenate(dk_s, axis=1)
        db = q * dq - k * dk
        yield
        dg = _dot01(cmtf, db) + gs_ref[...]
        gs_ref[...] += jnp.sum(db, axis=0, keepdims=True)
        yield
        df = dg / f - dk
        dhf_ref[...] = (df * (1.0 - lb) * sig * (1.0 - sig)).astype(BF16)
        dlb_ref[...] += jnp.sum(df * (1.0 - sig), axis=0, keepdims=True)
        sq = _sigmoid(hq)
        dhq_ref[...] = (dq * (HGRN_HEAD_DIM ** -0.5) * sq * (1.0 + hq * (1.0 - sq))).astype(BF16)
        dhi_ref[...] = jnp.concatenate(dv, axis=1).astype(BF16)

    blk = lambda c: pl.BlockSpec((CHUNK, wh), lambda i: (ci(i), c))
    vec = pl.BlockSpec((1, wh), lambda i: (0, 0))
    big = jax.ShapeDtypeStruct((T, wh), BF16)
    return dict(
        body=body, grid=(nch,), args=[z, z, z, lb, do, states],
        in_specs=[blk(0), blk(1), blk(3 + int(reverse)), vec, blk(0),
                  pl.BlockSpec((1, nh, hd, hd), lambda i: (ci(i), 0, 0, 0))],
        out_shape=[big, big, big, jax.ShapeDtypeStruct((1, wh), F32)],
        out_specs=[blk(0), blk(0), blk(0), vec],
        scratch_shapes=[pltpu.VMEM((nh, hd, hd), F32), pltpu.VMEM((1, wh), F32)])


def _mlstm_intra(cm, cmt, ig_row, ig_col, xf_row, xf_col, m_st):
    lf_row, lf_col = _log_sigmoid(xf_row), _log_sigmoid(xf_col)
    bcol = jnp.sum(jnp.where(cm, lf_row, 0.0), axis=1, keepdims=True)
    brow = jnp.sum(jnp.where(cmt, lf_col, 0.0), axis=0, keepdims=True)
    blast = jnp.sum(lf_row, axis=1, keepdims=True)
    dmat = jnp.where(cm, bcol - brow + ig_row, NEG)
    m_inter = bcol + m_st
    m_t = jnp.maximum(m_inter, jnp.max(dmat, axis=1, keepdims=True))
    p = jnp.exp(dmat - m_t)
    inter = jnp.exp(m_inter - m_t)
    w_col = blast - bcol + ig_col
    m_new = jnp.maximum(blast + m_st, jnp.max(w_col, axis=0, keepdims=True))
    cs = jnp.exp(blast + m_st - m_new)
    kscale = jnp.exp(w_col - m_new)
    return p, inter, m_t, m_new, cs, kscale


def _mlstm_specs(T, wm, nhm, reverse, backward):
    nch = T // CHUNK
    dm = wm // nhm
    ng = 4 * nhm

    def ci(i):
        fwd_order = nch - 1 - i if reverse else i
        return nch - 1 - fwd_order if backward else fwd_order

    row = lambda w, c: pl.BlockSpec((CHUNK, w), lambda i: (ci(i), c))
    gates_row = pl.BlockSpec((1, ng, CHUNK), lambda i: (ci(i), 0, 0))
    bias_row = pl.BlockSpec((1, LANES), lambda i: (0, 0))
    bias_col = pl.BlockSpec((ng, 1), lambda i: (0, 0))
    st_c = pl.BlockSpec((1, nhm, dm, dm), lambda i: (ci(i), 0, 0, 0))
    st_n = pl.BlockSpec((1, nhm, 1, dm), lambda i: (ci(i), 0, 0, 0))
    st_m = pl.BlockSpec((1, nhm, 1, LANES), lambda i: (ci(i), 0, 0, 0))
    return nch, dm, ng, ci, row, gates_row, bias_row, bias_col, st_c, st_n, st_m


def _mlstm_fwd(qk, z, gc, gr, br, bcl, *, wm, nhm, vcol, reverse):
    T = qk.shape[0]
    nch, dm, ng, ci, row, gates_row, bias_row, bias_col, st_c, st_n, st_m = _mlstm_specs(
        T, wm, nhm, reverse, False)
    d = int(reverse)

    def body(q_ref, k_ref, v_ref, gc_ref, gr_ref, br_ref, bc_ref,
             h_ref, cst_ref, nst_ref, mst_ref, c_ref, n_ref, m_ref):
        @pl.when(pl.program_id(0) == 0)
        def _():
            c_ref[...] = jnp.zeros_like(c_ref)
            n_ref[...] = jnp.zeros_like(n_ref)
            m_ref[...] = jnp.full_like(m_ref, M_INIT)

        cm = _chunk_mask(reverse)
        cmt = _chunk_mask(reverse, transpose=True)
        G = gc_ref[...] + br_ref[...]
        Gr = gr_ref[0] + bc_ref[...]
        H = range(nhm)
        sls = [slice(h * dm, (h + 1) * dm) for h in H]
        m_all = [m_ref[h] for h in H]
        intra = [_mlstm_intra(cm, cmt, Gr[d * nhm + h:d * nhm + h + 1, :], G[:, d * nhm + h:d * nhm + h + 1],
                              Gr[2 * nhm + d * nhm + h:2 * nhm + d * nhm + h + 1, :],
                              G[:, 2 * nhm + d * nhm + h:2 * nhm + d * nhm + h + 1], m_all[h][:, 0:1])
                 for h in H]
        p, inter, m_t, m_new, cs, kscale = zip(*intra)
        yield
        q = [q_ref[:, sl] * (dm ** -0.5) for sl in sls]
        k = [k_ref[:, sl] for sl in sls]
        v = [v_ref[:, sl] for sl in sls]
        ct = [c_ref[h] for h in H]
        n = [n_ref[h] for h in H]
        yield
        sc = [_dot_nt(q[h], k[h]) * p[h] for h in H]
        yield
        qc = [_dot(q[h], ct[h], _DIMS["nt"]) for h in H]
        yield
        num = [_dot(sc[h], v[h], _DIMS["nn"]) + inter[h] * qc[h] for h in H]
        yield
        den = [jnp.sum(sc[h], axis=1, keepdims=True) + inter[h] * jnp.sum(q[h] * n[h], axis=1, keepdims=True)
               for h in H]
        kw = [k[h] * kscale[h] for h in H]
        yield
        cnew = [cs[h] * ct[h] + _dot_tn(v[h], kw[h]) for h in H]
        yield
        for h in H:
            cst_ref[0, h] = ct[h]
            nst_ref[0, h] = n[h]
            mst_ref[0, h] = m_all[h]
            h_ref[:, sls[h]] = num[h] / jnp.maximum(jnp.abs(den[h]), jnp.exp(-m_t[h]))
            c_ref[h] = cnew[h]
            n_ref[h] = cs[h] * n[h] + jnp.sum(kw[h], axis=0, keepdims=True)
            m_ref[h] = jnp.broadcast_to(m_new[h], (1, LANES))

    return dict(
        body=body, grid=(nch,), args=[qk, qk, z, gc, gr, br, bcl],
        out_shape=[jax.ShapeDtypeStruct((T, wm), F32),
                   jax.ShapeDtypeStruct((nch, nhm, dm, dm), F32),
                   jax.ShapeDtypeStruct((nch, nhm, 1, dm), F32),
                   jax.ShapeDtypeStruct((nch, nhm, 1, LANES), F32)],
        in_specs=[row(wm, 0), row(wm, 1), row(wm, vcol), row(LANES, 0), gates_row, bias_row, bias_col],
        out_specs=[row(wm, 0), st_c, st_n, st_m],
        scratch_shapes=[pltpu.VMEM((nhm, dm, dm), F32), pltpu.VMEM((nhm, 1, dm), F32),
                        pltpu.VMEM((nhm, 1, LANES), F32)])


def _mlstm_bwd(qk, z, gc, gr, br, bcl, dh, states, *, wm, nhm, vcol, reverse):
    T = qk.shape[0]
    nch, dm, ng, ci, row, gates_row, bias_row, bias_col, st_c, st_n, st_m = _mlstm_specs(
        T, wm, nhm, reverse, True)
    d = int(reverse)

    def body(q_ref, k_ref, v_ref, gc_ref, gr_ref, br_ref, bc_ref, dh_ref, cst_ref, nst_ref, mst_ref,
             dqk_ref, dv_ref, dgr_ref, dgs_ref, e_ref, en_ref, fs_ref):
        @pl.when(pl.program_id(0) == 0)
        def _():
            e_ref[...] = jnp.zeros_like(e_ref)
            en_ref[...] = jnp.zeros_like(en_ref)
            fs_ref[...] = jnp.zeros_like(fs_ref)
            dgs_ref[...] = jnp.zeros_like(dgs_ref)

        cm = _chunk_mask(reverse)
        cmt = _chunk_mask(reverse, transpose=True)
        row_i = lax.broadcasted_iota(jnp.int32, (CHUNK, CHUNK), 0)
        col_i = lax.broadcasted_iota(jnp.int32, (CHUNK, CHUNK), 1)
        eye = row_i == col_i
        G = gc_ref[...] + br_ref[...]
        Gr = gr_ref[0] + bc_ref[...]
        H = range(nhm)
        sls = [slice(h * dm, (h + 1) * dm) for h in H]
        xf_row = [Gr[2 * nhm + d * nhm + h:2 * nhm + d * nhm + h + 1, :] for h in H]
        intra = [_mlstm_intra(cm, cmt, Gr[d * nhm + h:d * nhm + h + 1, :], G[:, d * nhm + h:d * nhm + h + 1],
                              xf_row[h], G[:, 2 * nhm + d * nhm + h:2 * nhm + d * nhm + h + 1],
                              mst_ref[0, h][:, 0:1]) for h in H]
        p, inter, m_t, _, cs, kscale = zip(*intra)
        yield
        q = [q_ref[:, sls[h]] * (dm ** -0.5) for h in H]
        k = [k_ref[:, sls[h]] for h in H]
        v = [v_ref[:, sls[h]] for h in H]
        dhv = [dh_ref[:, sls[h]] for h in H]
        ct = [cst_ref[0, h] for h in H]
        n = [nst_ref[0, h] for h in H]
        et = [e_ref[h] for h in H]
        en = [en_ref[h] for h in H]
        carry = [fs_ref[h][:, 0:1] for h in H]
        yield
        sc = [_dot_nt(q[h], k[h]) * p[h] for h in H]
        yield
        qc = [_dot_nt(q[h], ct[h]) for h in H]
        yield
        num = [_dot_nn(sc[h], v[h]) + inter[h] * qc[h] for h in H]
        yield
        den = [jnp.sum(sc[h], axis=1, keepdims=True) + inter[h] * jnp.sum(q[h] * n[h], axis=1, keepdims=True)
               for h in H]
        floor = [jnp.exp(-m_t[h]) for h in H]
        nstab = [jnp.maximum(jnp.abs(den[h]), floor[h]) for h in H]
        gh = [dhv[h] / nstab[h] for h in H]
        dd = [-jnp.sum(dhv[h] * (num[h] / nstab[h]), axis=1, keepdims=True) / nstab[h]
              * jnp.where(jnp.abs(den[h]) > floor[h], jnp.sign(den[h]), 0.0) for h in H]
        yield
        dsqk = [(_dot_nt(gh[h], v[h]) + dd[h]) * p[h] for h in H]
        qi = [q[h] * inter[h] for h in H]
        kw = [k[h] * kscale[h] for h in H]
        yield
        ghc = [_dot_nn(gh[h], ct[h]) for h in H]
        yield
        vet = [_dot_nn(v[h], et[h]) for h in H]
        yield
        dq = [_dot_nn(dsqk[h], k[h]) + inter[h] * (ghc[h] + dd[h] * n[h]) for h in H]
        yield
        dk = [_dot_tn(dsqk[h], q[h]) + kscale[h] * (vet[h] + en[h]) for h in H]
        yield
        dv = [_dot(sc[h], gh[h], _DIMS["tn"]) + _dot(kw[h], et[h], _DIMS["nt"]) for h in H]
        yield
        e_new = [cs[h] * et[h] + _dot_tn(gh[h], qi[h]) for h in H]
        en_new = [cs[h] * en[h] + jnp.sum(qi[h] * dd[h], axis=0, keepdims=True) for h in H]
        yield
        di_col = [jnp.sum(k[h] * dk[h], axis=1, keepdims=True) for h in H]
        df_col = [jnp.sum(q[h] * dq[h], axis=1, keepdims=True) - di_col[h] for h in H]
        di_row = [jnp.sum(jnp.where(eye, di_col[h], 0.0), axis=0, keepdims=True) for h in H]
        dlf_row = [jnp.sum(jnp.where(cm, df_col[h], 0.0), axis=0, keepdims=True) + carry[h] for h in H]
        dxf_row = [dlf_row[h] * (1.0 - _sigmoid(xf_row[h])) for h in H]
        yield
        for h in H:
            sl = sls[h]
            slk = slice(wm + h * dm, wm + (h + 1) * dm)
            e_ref[h] = e_new[h]
            en_ref[h] = en_new[h]
            fs_ref[h] = jnp.broadcast_to(carry[h] + jnp.sum(df_col[h], axis=0, keepdims=True), (1, LANES))
            dgr_ref[0, h:h + 1, :] = di_row[h]
            dgr_ref[0, nhm + h:nhm + h + 1, :] = dxf_row[h]
            dgs_ref[h:h + 1, :] += jnp.broadcast_to(jnp.sum(di_row[h], axis=1, keepdims=True), (1, LANES))
            dgs_ref[nhm + h:nhm + h + 1, :] += jnp.broadcast_to(
                jnp.sum(dxf_row[h], axis=1, keepdims=True), (1, LANES))
            dqk_ref[:, sl] = (dq[h] * (dm ** -0.5)).astype(BF16)
            dqk_ref[:, slk] = dk[h].astype(BF16)
            dv_ref[:, sl] = dv[h].astype(BF16)

    return dict(
        body=body, grid=(nch,), args=[qk, qk, z, gc, gr, br, bcl, dh] + list(states),
        in_specs=[row(wm, 0), row(wm, 1), row(wm, vcol), row(LANES, 0), gates_row, bias_row, bias_col,
                  row(wm, 0), st_c, st_n, st_m],
        out_shape=[jax.ShapeDtypeStruct((T, 2 * wm), BF16), jax.ShapeDtypeStruct((T, wm), BF16),
                   jax.ShapeDtypeStruct((nch, 2 * nhm, CHUNK), F32),
                   jax.ShapeDtypeStruct((2 * nhm, LANES), F32)],
        out_specs=[row(2 * wm, 0), row(wm, 0),
                   pl.BlockSpec((1, 2 * nhm, CHUNK), lambda i: (ci(i), 0, 0)),
                   pl.BlockSpec((2 * nhm, LANES), lambda i: (0, 0))],
        scratch_shapes=[pltpu.VMEM((nhm, dm, dm), F32), pltpu.VMEM((nhm, 1, dm), F32),
                        pltpu.VMEM((nhm, 1, LANES), F32)])


def _conv_taps(x, w_ref):
    T = x.shape[0]
    t = lax.broadcasted_iota(jnp.int32, x.shape, 0)
    taps = []
    acc = None
    for j in range(CONV_WIDTH):
        s = CONV_WIDTH // 2 - j
        if s == 0:
            xs = x
        else:
            xs = jnp.where((t - s >= 0) & (t - s < T), pltpu.roll(x, s % T, 0), 0.0)
        taps.append(xs)
        term = w_ref[j:j + 1, :] * xs
        acc = term if acc is None else acc + term
    return taps, acc


def _conv_fwd(z, w, b, *, col0, tc=LANES, name):
    T = z.shape[0]
    C2 = w.shape[1]
    assert col0 % tc == 0 and C2 % tc == 0

    def body(z_ref, w_ref, b_ref, o_ref):
        _, acc = _conv_taps(z_ref[...], w_ref)
        c = acc + b_ref[...]
        o_ref[...] = c * _sigmoid(c)

    return pl.pallas_call(
        body, name=name, out_shape=jax.ShapeDtypeStruct((T, C2), F32),
        grid=(C2 // tc,),
        in_specs=[pl.BlockSpec((T, tc), lambda j: (0, col0 // tc + j)),
                  pl.BlockSpec((CONV_WIDTH, tc), lambda j: (0, j)),
                  pl.BlockSpec((1, tc), lambda j: (0, j))],
        out_specs=pl.BlockSpec((T, tc), lambda j: (0, j)),
        compiler_params=_params("parallel"),
    )(z, w, b)


def _conv_bwd(dy_fw, dy_bw, z, w, b, *, col0, tc=LANES, name):
    T = z.shape[0]
    C2 = w.shape[1]

    def body(dyf_ref, dyb_ref, z_ref, w_ref, b_ref, dx_ref, dw_ref, db_ref):
        taps, acc = _conv_taps(z_ref[...], w_ref)
        c = acc + b_ref[...]
        sg = _sigmoid(c)
        dc = (dyf_ref[...].astype(F32) + dyb_ref[...].astype(F32)) * sg * (1.0 + c * (1.0 - sg))
        t = lax.broadcasted_iota(jnp.int32, dc.shape, 0)
        dx = None
        for j in range(CONV_WIDTH):
            s = j - CONV_WIDTH // 2
            if s == 0:
                ds = dc
            else:
                ds = jnp.where((t - s >= 0) & (t - s < T), pltpu.roll(dc, s % T, 0), 0.0)
            term = w_ref[j:j + 1, :] * ds
            dx = term if dx is None else dx + term
            dw_ref[j:j + 1, :] = jnp.sum(dc * taps[j], axis=0, keepdims=True)
        dx_ref[...] = dx.astype(BF16)
        db_ref[...] = jnp.sum(dc, axis=0, keepdims=True)

    col = pl.BlockSpec((T, tc), lambda j: (0, j))
    wsp = pl.BlockSpec((CONV_WIDTH, tc), lambda j: (0, j))
    bsp = pl.BlockSpec((1, tc), lambda j: (0, j))
    return pl.pallas_call(
        body, name=name,
        out_shape=[jax.ShapeDtypeStruct((T, C2), BF16), jax.ShapeDtypeStruct((CONV_WIDTH, C2), F32),
                   jax.ShapeDtypeStruct((1, C2), F32)],
        grid=(C2 // tc,),
        in_specs=[col, col, pl.BlockSpec((T, tc), lambda j: (0, col0 // tc + j)), wsp, bsp],
        out_specs=[col, wsp, bsp],
        compiler_params=_params("parallel"),
    )(dy_fw, dy_bw, z, w, b)


def _mix_heads(o_fw, o_bw, h_fw, h_bw, wh, wm, nhm):
    out = []
    hd = HGRN_HEAD_DIM
    for h in range(wh // hd):
        sl = slice(h * hd, (h + 1) * hd)
        o = o_fw[:, sl] + o_bw[:, sl]
        r = lax.rsqrt(jnp.mean(o * o, axis=-1, keepdims=True) + NORM_EPS)
        out.append((0, sl, o * r, r))
    dm = wm // nhm
    for h in range(nhm):
        sl = slice(h * dm, (h + 1) * dm)
        x = h_fw[:, sl] + h_bw[:, sl]
        xc = x - jnp.mean(x, axis=-1, keepdims=True)
        r = lax.rsqrt(jnp.mean(xc * xc, axis=-1, keepdims=True) + NORM_EPS)
        out.append((1, sl, xc * r, r))
    return out


def _mix_specs(T, wh, wm, tm, gcol, ocol):
    rowh = pl.BlockSpec((tm, wh), lambda i: (i, 0))
    rowm = pl.BlockSpec((tm, wm), lambda i: (i, 0))
    hg = pl.BlockSpec((tm, wh), lambda i: (i, gcol))
    mo = pl.BlockSpec((tm, wm), lambda i: (i, ocol))
    vh = pl.BlockSpec((1, wh), lambda i: (0, 0))
    vm = pl.BlockSpec((1, wm), lambda i: (0, 0))
    return rowh, rowm, hg, mo, vh, vm


def _mix_fwd(o_fw, o_bw, h_fw, h_bw, z, gh, gm, *, nhm, gcol, ocol, tm=256, name):
    T, wh = o_fw.shape
    wm = h_fw.shape[1]
    tm = min(tm, T)
    rowh, rowm, hg, mo, vh, vm = _mix_specs(T, wh, wm, tm, gcol, ocol)

    def body(of_ref, ob_ref, hf_ref, hb_ref, hg_ref, mo_ref, gh_ref, gm_ref, y_ref, yt_ref):
        heads = _mix_heads(of_ref[...], ob_ref[...], hf_ref[...], hb_ref[...], wh, wm, nhm)
        for grp, sl, nv, _ in heads:
            if grp == 0:
                gate = hg_ref[:, sl]
                gate = gate * _sigmoid(gate)
                y = nv * gh_ref[:, sl] * gate
                osl = sl
            else:
                y = nv * gm_ref[:, sl] * _sigmoid(mo_ref[:, sl])
                osl = slice(wh + sl.start, wh + sl.stop)
            y_ref[:, osl] = y.astype(BF16)
            yt_ref[osl, :] = y.T.astype(BF16)

    return pl.pallas_call(
        body, name=name,
        out_shape=[jax.ShapeDtypeStruct((T, wh + wm), BF16), jax.ShapeDtypeStruct((wh + wm, T), BF16)],
        grid=(T // tm,),
        in_specs=[rowh, rowh, rowm, rowm, hg, mo, vh, vm],
        out_specs=[pl.BlockSpec((tm, wh + wm), lambda i: (i, 0)),
                   pl.BlockSpec((wh + wm, tm), lambda i: (0, i))],
        compiler_params=_params("parallel"),
    )(o_fw, o_bw, h_fw, h_bw, z, z, gh, gm)


def _mix_bwd(dy, o_fw, o_bw, h_fw, h_bw, z, gh, gm, *, nhm, gcol, ocol, tm=256, name, comm=None):
    T, wh = o_fw.shape
    wm = h_fw.shape[1]
    tm = min(tm, T)
    rowh, rowm, hg, mo, vh, vm = _mix_specs(T, wh, wm, tm, gcol, ocol)

    def body(dy_ref, of_ref, ob_ref, hf_ref, hb_ref, hg_ref, mo_ref, gh_ref, gm_ref,
             do_ref, dh_ref, dhg_ref, dmo_ref, dgh_ref, dgm_ref):
        @pl.when(pl.program_id(0) == 0)
        def _():
            dgh_ref[...] = jnp.zeros_like(dgh_ref)
            dgm_ref[...] = jnp.zeros_like(dgm_ref)

        heads = _mix_heads(of_ref[...], ob_ref[...], hf_ref[...], hb_ref[...], wh, wm, nhm)
        for grp, sl, nv, r in heads:
            if grp == 0:
                d = dy_ref[:, sl]
                x = hg_ref[:, sl]
                sg = _sigmoid(x)
                gate = x * sg
                g = gh_ref[:, sl]
                dgh_ref[:, sl] += jnp.sum(d * nv * gate, axis=0, keepdims=True)
                dhg_ref[:, sl] = (d * nv * g * sg * (1.0 + x * (1.0 - sg))).astype(BF16)
                t = d * g * gate
                do_ref[:, sl] = r * (t - nv * jnp.mean(t * nv, axis=-1, keepdims=True))
            else:
                d = dy_ref[:, slice(wh + sl.start, wh + sl.stop)]
                sg = _sigmoid(mo_ref[:, sl])
                g = gm_ref[:, sl]
                dgm_ref[:, sl] += jnp.sum(d * nv * sg, axis=0, keepdims=True)
                dmo_ref[:, sl] = (d * nv * g * sg * (1.0 - sg)).astype(BF16)
                t = d * g * sg
                dh_ref[:, sl] = r * (t - jnp.mean(t, axis=-1, keepdims=True)
                                     - nv * jnp.mean(t * nv, axis=-1, keepdims=True))

    return _pcall(
        body, name=name, comm=comm,
        out_shape=[jax.ShapeDtypeStruct((T, wh), F32), jax.ShapeDtypeStruct((T, wm), F32),
                   jax.ShapeDtypeStruct((T, wh), BF16), jax.ShapeDtypeStruct((T, wm), BF16),
                   jax.ShapeDtypeStruct((1, wh), F32), jax.ShapeDtypeStruct((1, wm), F32)],
        grid=(T // tm,),
        in_specs=[pl.BlockSpec((tm, wh + wm), lambda i: (i, 0)), rowh, rowh, rowm, rowm, hg, mo, vh, vm],
        out_specs=[rowh, rowm, rowh, rowm, vh, vm],
        sem=("arbitrary",),
    )(dy, o_fw, o_bw, h_fw, h_bw, z, z, gh, gm)


def _pick(n, pref):
    for c in range(pref - pref % LANES, 0, -LANES):
        if n % c == 0:
            return c
    return n


def _mm(a, b, mode="nn", *, tm=1024, tn=1024, tk=2048, **kw):
    if mode == "nn":
        (M, K), N = a.shape, b.shape[1]
    else:
        (M, K), N = a.shape, kw.get("rows") or b.shape[0]
    return _matmul(a, b, mode=mode, tm=_pick(M, tm), tn=_pick(N, tn), tk=_pick(K, tk), **kw)


def _device_step(x, target, sh, small_parts, small_params, core):
    T, D = x.shape

    def full(g):
        return g.reshape(-1, D)

    def shards(g):
        return g.reshape(N_DEV, -1, D)

    def pair_add(name, g, land):
        return _pair_add(g, land, core, "rs_add_" + name)

    def dw(lhs_t, rhs, name, scale=1.0):
        return shards(_mm(lhs_t, rhs, out_dtype=BF16, scale=scale, tm=704, tk=T, name=name))

    xb, (g_w1a, g_small) = _cast(x, name="x_cast", comm=_ag_job([sh["ffn1_w1"], small_parts]))
    w1a = full(g_w1a)
    P = small_params(g_small)
    wh = P["hgrn_norm_g"].shape[1]
    wm = P["mlstm_norm_g"].shape[1]
    nhm = P["ig_b"].shape[1]
    ng = 4 * nhm
    nch = T // CHUNK
    assert wh == wm and T % CHUNK == 0
    vcol, ocol, gcol = 7, 8, 2
    col0 = 5 * wh
    nz = 5 * wh + 4 * wm
    c1 = sh["w_in"].shape[0] // 4 // 16 * 16
    a1f, (g_w3a, g_wp1) = _mm(xb, w1a, "nt", out_dtype=BF16, name="ffn1_up_a",
                              comm=_ag_job([sh["ffn1_w3"], sh["w_in"][:c1]]))
    w3a = full(g_w3a)
    (a1, b1, h1, h1t), (g_w2a, g_wp2) = _ffn_up(xb, None, w3a, a=a1f, name="ffn1_up_b",
                                                comm=_ag_job([sh["ffn1_w2"], sh["w_in"][c1:2 * c1]]))
    w2a = full(g_w2a)
    r1, (g_wp3,) = _mm(h1, w2a, add=x, scale=0.5, add_scale=DN_ALPHA, tk=w2a.shape[0] // 4,
                       name="ffn1_down", comm=_ag_job([sh["w_in"][2 * c1:]]))
    w_int = full(jnp.concatenate([g_wp1, g_wp2, g_wp3], axis=1))
    w_wgt = jnp.zeros((LANES, D), BF16).at[:ng].set(w_int[nz:])
    x1, x1b = _ln_fwd(r1, P["ln1_g"], P["ln1_b"], name="ln1")
    zm, (g_wout,) = _mm(x1b, w_int, "nt", rows=nz, name="zm", comm=_ag_job([sh["w_out"]]))
    w_wout = full(g_wout)
    zg = _mm(x1b, w_wgt, "nt", name="zg")
    gr = zg[:, :ng].reshape(nch, CHUNK, ng).transpose(0, 2, 1)
    bias = jnp.concatenate([P["ig_b"].reshape(-1), P["fg_b"].reshape(-1)])
    br = jnp.zeros((1, LANES), F32).at[0, :ng].set(bias)
    bcl = bias.reshape(ng, 1)
    lbf, lbb = P["lb"][0:1], P["lb"][1:2]
    ((o_fw, s_fw), (o_bw, s_bw)), (g_w1b,) = _run_specs(
        [_hgrn_fwd(zm, lbf, wh=wh, reverse=False), _hgrn_fwd(zm, lbb, wh=wh, reverse=True)],
        name="hgrn_fwd", comm=_ag_job([sh["ffn2_w1"]]))
    qk = _conv_fwd(zm, P["conv_w"], P["conv_b"], col0=col0, name="conv")
    mkw = dict(wm=wm, nhm=nhm, vcol=vcol)
    ((h_fw, *st_fw),), (g_w3b,) = _run_specs([_mlstm_fwd(qk, zm, zg, gr, br, bcl, reverse=False, **mkw)],
                                             name="mlstm_fw", comm=_ag_job([sh["ffn2_w3"]]))
    ((h_bw, *st_bw),) = _run_specs([_mlstm_fwd(qk, zm, zg, gr, br, bcl, reverse=True, **mkw)],
                                   name="mlstm_bw")
    w1b, w3b = full(g_w1b), full(g_w3b)
    y, yt = _mix_fwd(o_fw, o_bw, h_fw, h_bw, zm, P["hgrn_norm_g"], P["mlstm_norm_g"],
                     nhm=nhm, gcol=gcol, ocol=ocol, name="mix")
    r2 = _mm(y, w_wout, add=x1, add_scale=DN_ALPHA, name="r2")
    x2, x2b = _ln_fwd(r2, P["ln2_g"], P["ln2_b"], name="ln2")
    (a2, b2, h2, h2t), (g_w2b,) = _ffn_up(x2b, w1b, w3b, name="ffn2_up", comm=_ag_job([sh["ffn2_w2"]]))
    w2b = full(g_w2b)
    r3 = _mm(h2, w2b, add=x2, scale=0.5, add_scale=DN_ALPHA, tk=w2b.shape[0] // 4, name="ffn2_down")

    G, land = {}, {}
    dr3, dr3b, G["ln3_g"], G["ln3_b"], loss = _ln_bwd(
        None, r3, P["ln3_g"], b=P["ln3_b"], target=target, name="ln3_bwd")
    da2, db2, da2t, db2t = _ffn_bwd_act(dr3b, w2b, a2, b2, name="ffn2_dact")
    gw1b, gw3b = dw(da2t, x2b, "ffn2_dw1"), dw(db2t, x2b, "ffn2_dw3")
    gw2b = dw(h2t, dr3b, "ffn2_dw2", scale=0.5)
    dx2, l1 = _ffn_dx(da2, db2, w1b, w3b, dr3, name="ffn2_dx", comm=_pair_job([gw1b, gw3b, gw2b]))
    p_w1b, p_w3b, p_w2b = [pair_add(k, g, l) for k, g, l in
                           zip(("ffn2_w1", "ffn2_w3", "ffn2_w2"), (gw1b, gw3b, gw2b), l1)]
    dr2, dr2b, G["ln2_g"], G["ln2_b"] = _ln_bwd(dx2, r2, P["ln2_g"], name="ln2_bwd")
    dy = _mm(dr2b, w_wout, "nt", name="dy")
    gwout = shards(_mm(yt, dr2b, out_dtype=BF16, name="dwout"))
    (do, dh, dhg, dmo, G["hgrn_norm_g"], G["mlstm_norm_g"]), (l1,) = _mix_bwd(
        dy, o_fw, o_bw, h_fw, h_bw, zm, P["hgrn_norm_g"], P["mlstm_norm_g"],
        nhm=nhm, gcol=gcol, ocol=ocol, name="mix_bwd", comm=_pair_job([gwout]))
    p_wout = pair_add("w_out", gwout, l1)
    ((dqk_f, dv_f, dgr_f, dgs_f),), (land["ffn2_w1"],) = _run_specs(
        [_mlstm_bwd(qk, zm, zg, gr, br, bcl, dh, st_fw, reverse=False, **mkw)],
        name="mlstm_fw_bwd", comm=_chips_job([p_w1b]))
    ((dqk_b, dv_b, dgr_b, dgs_b),), (land["ffn2_w3"],) = _run_specs(
        [_mlstm_bwd(qk, zm, zg, gr, br, bcl, dh, st_bw, reverse=True, **mkw)],
        name="mlstm_bw_bwd", comm=_chips_job([p_w3b]))
    dmqk, G["conv_w"], G["conv_b"] = _conv_bwd(dqk_f, dqk_b, zm, P["conv_w"], P["conv_b"], col0=col0,
                                               name="conv_bwd")
    ((dhq_f, dhi_f, dhf_fw, dlb_f), (dhq_b, dhi_b, dhf_bw, dlb_b)), (land["ffn2_w2"], land["w_out"]) = \
        _run_specs([_hgrn_bwd(zm, lbf, do, s_fw, wh=wh, reverse=False),
                    _hgrn_bwd(zm, lbb, do, s_bw, wh=wh, reverse=True)],
                   name="hgrn_bwd", comm=_chips_job([p_w2b, p_wout]))
    G["lb"] = jnp.concatenate([dlb_f, dlb_b], axis=0)
    G["ig_b"] = jnp.stack([dgs_f[:nhm, 0], dgs_b[:nhm, 0]])
    G["fg_b"] = jnp.stack([dgs_f[nhm:, 0], dgs_b[nhm:, 0]])
    dzm, dzmt = _dz_pack([(dhq_f, dhq_b), (dhi_f, dhi_b), dhg, dhf_fw, dhf_bw, dmqk, (dv_f, dv_b), dmo],
                         name="dz_pack")
    dgr = jnp.concatenate([dgr_f[:, :nhm], dgr_b[:, :nhm], dgr_f[:, nhm:], dgr_b[:, nhm:]], axis=1)
    dzgt = jnp.zeros((LANES, T), F32).at[:ng].set(dgr.transpose(1, 0, 2).reshape(ng, T)).astype(BF16)
    dzg = jnp.zeros((T, LANES), F32).at[:, :ng].set(dgr.transpose(0, 2, 1).reshape(T, ng)).astype(BF16)
    gwin = shards(jnp.concatenate([_mm(dzmt, x1b, out_dtype=BF16, tk=T, name="dwin"),
                                   _mm(dzgt, x1b, out_dtype=BF16, tk=T, name="dwg")[:ng]], axis=0))
    t, (l1,) = _mm(dzg, w_wgt, add=dr2, add_scale=DN_ALPHA, name="dx1_g", comm=_pair_job([gwin]))
    p_win = pair_add("w_in", gwin, l1)
    dx1, (land["w_in"],) = _mm(dzm, w_int, rows=nz, add=t, tk=1536, name="dx1", comm=_chips_job([p_win]))
    dr1, dr1b, G["ln1_g"], G["ln1_b"] = _ln_bwd(dx1, r1, P["ln1_g"], name="ln1_bwd")
    gw2a = dw(h1t, dr1b, "ffn1_dw2", scale=0.5)
    (da1, db1, da1t, db1t), (l1,) = _ffn_bwd_act(dr1b, w2a, a1, b1, name="ffn1_dact",
                                                 comm=_pair_job([gw2a]))
    p_w2a = pair_add("ffn1_w2", gw2a, l1)
    gw1a, (land["ffn1_w2"],) = _mm(da1t, xb, out_dtype=BF16, tm=704, tk=T, name="ffn1_dw1",
                                   comm=_chips_job([p_w2a]))
    gw1a = shards(gw1a)
    gw3a, (l1,) = _mm(db1t, xb, out_dtype=BF16, tm=704, tk=T, name="ffn1_dw3", comm=_pair_job([gw1a]))
    gw3a = shards(gw3a)
    p_w1a = pair_add("ffn1_w1", gw1a, l1)
    (l1,) = _comm_call(_pair_job([gw3a]), "rs_pair_ffn1_w3")
    p_w3a = pair_add("ffn1_w3", gw3a, l1)
    gx, (land["ffn1_w1"], land["ffn1_w3"]) = _ffn_dx(
        da1, db1, w1a, w3a, dr1, name="ffn1_dx", comm=_chips_job([p_w1a, p_w3a]))
    return loss, gx, land, G


ANY = pl.BlockSpec(memory_space=pl.ANY)


def _place():
    x, y, c = lax.axis_index("x"), lax.axis_index("y"), lax.axis_index("c")
    chips = [(1 - x, y), (x, 1 - y), (1 - x, 1 - y)]
    return x, y, c, chips


def _comm_call(job, name):
    n_in, n_out = len(job.inputs), len(job.out_shape)

    def body(*refs):
        ins, outs, sems = refs[:n_in], refs[n_in:n_in + n_out], refs[n_in + n_out:]
        job.start(ins, outs, sems)
        job.middle(ins, outs, sems)
        job.finish(ins, outs, sems)

    return pl.pallas_call(body, name=name, out_shape=job.out_shape, in_specs=[ANY] * n_in,
                          out_specs=[ANY] * n_out, scratch_shapes=job.sems)(*job.inputs)


def _ag_job(parts):
    n = len(parts)
    halves = [p.shape[0] // 2 // 16 * 16 or p.shape[0] // 2 // 8 * 8 or p.shape[0] for p in parts]
    cut = [h < p.shape[0] for h, p in zip(halves, parts)]

    def helpers(ins, outs, sems):
        send_sems, recv_sems, local_sems = sems
        x, y, c, _ = _place()
        me, sibling = (x, y, c), (x, y, 1 - c)
        xn, yn, dg = (1 - x, y), (x, 1 - y), (1 - x, 1 - y)

        def slot(a, block, part=None):
            bx, by, bc = block
            ref = outs[a].at[4 * bx + 2 * by + bc]
            if part == 0:
                return ref.at[pl.ds(0, halves[a])]
            if part == 1:
                return ref.at[pl.ds(halves[a], parts[a].shape[0] - halves[a])]
            return ref

        def copy(a, k, block, to, src=None, part=None):
            dst = slot(a, block, part)
            return pltpu.make_async_remote_copy(
                src_ref=dst if src is None else src, dst_ref=dst,
                send_sem=send_sems.at[a, k], recv_sem=recv_sems.at[a, k],
                device_id=to, device_id_type=MESH)

        local = [pltpu.make_async_copy(ins[a], slot(a, me), local_sems.at[a]) for a in range(n)]
        return copy, local, me, sibling, xn, yn, dg, c

    def start(ins, outs, sems):
        copy, local, me, sibling, xn, yn, dg, c = helpers(ins, outs, sems)
        for cp in local:
            cp.start()
        for a in range(n):
            copy(a, 0, me, sibling, src=ins[a]).start()
            copy(a, 1, me, (*xn, c), src=ins[a]).start()
            copy(a, 2, me, (*yn, c), src=ins[a]).start()

    def middle(ins, outs, sems):
        copy, local, me, sibling, xn, yn, dg, c = helpers(ins, outs, sems)
        for a in range(n):
            copy(a, 1, (*xn, c), me).wait_recv()
            copy(a, 3, (*xn, c), (*yn, c), part=0).start()
            copy(a, 5, (*xn, c), sibling).start()
        for a in range(n):
            copy(a, 2, (*yn, c), me).wait_recv()
            if cut[a]:
                copy(a, 4, (*yn, c), (*xn, c), part=1).start()
            copy(a, 6, (*yn, c), sibling).start()

    def finish(ins, outs, sems):
        copy, local, me, sibling, xn, yn, dg, c = helpers(ins, outs, sems)
        for a in range(n):
            copy(a, 3, (*dg, c), me, part=0).wait_recv()
            if cut[a]:
                copy(a, 4, (*dg, c), me, part=1).wait_recv()
            copy(a, 7, (*dg, c), sibling).start()
        for a in range(n):
            copy(a, 0, sibling, me).wait_recv()
            copy(a, 5, (*xn, 1 - c), me).wait_recv()
            copy(a, 6, (*yn, 1 - c), me).wait_recv()
            copy(a, 7, (*dg, 1 - c), me).wait_recv()
        for a in range(n):
            copy(a, 0, me, sibling, src=ins[a]).wait_send()
            copy(a, 1, me, (*xn, c), src=ins[a]).wait_send()
            copy(a, 2, me, (*yn, c), src=ins[a]).wait_send()
            copy(a, 3, (*xn, c), (*yn, c), part=0).wait_send()
            if cut[a]:
                copy(a, 4, (*yn, c), (*xn, c), part=1).wait_send()
            for k, blk in ((5, xn), (6, yn), (7, dg)):
                copy(a, k, (*blk, c), sibling).wait_send()
        for cp in local:
            cp.wait()

    return _Job(parts, [jax.ShapeDtypeStruct((N_DEV,) + p.shape, p.dtype) for p in parts],
                [pltpu.SemaphoreType.DMA((n, 8)), pltpu.SemaphoreType.DMA((n, 8)),
                 pltpu.SemaphoreType.DMA((n,))], start, finish, middle)


def _pair_job(gs):
    n = len(gs)

    def copies(g_refs, land_refs, sems):
        send_sems, recv_sems = sems
        x, y, c, _ = _place()
        return [pltpu.make_async_remote_copy(
            src_ref=g_refs[a].at[2 * k + 1 - c], dst_ref=land_refs[a].at[k],
            send_sem=send_sems.at[a, k], recv_sem=recv_sems.at[a, k],
            device_id=(x, y, 1 - c), device_id_type=MESH) for a in range(n) for k in range(4)]

    def start(ins, outs, sems):
        for cp in copies(ins, outs, sems):
            cp.start()

    def finish(ins, outs, sems):
        for cp in copies(ins, outs, sems):
            cp.wait()

    return _Job(gs, [jax.ShapeDtypeStruct((4,) + g.shape[1:], g.dtype) for g in gs],
                [pltpu.SemaphoreType.DMA((n, 4)), pltpu.SemaphoreType.DMA((n, 4))], start, finish)


def _chips_job(ps):
    n = len(ps)

    def copies(p_refs, land_refs, sems):
        send_sems, recv_sems, local_sems = sems
        x, y, c, chips = _place()
        mine = 2 * x + y
        owns = [pltpu.make_async_copy(p_refs[a].at[mine], land_refs[a].at[mine], local_sems.at[a])
                for a in range(n)]
        sends = [pltpu.make_async_remote_copy(
            src_ref=p_refs[a].at[2 * chip[0] + chip[1]], dst_ref=land_refs[a].at[mine],
            send_sem=send_sems.at[a, j], recv_sem=recv_sems.at[a, j],
            device_id=(*chip, c), device_id_type=MESH) for a in range(n) for j, chip in enumerate(chips)]
        recvs = [pltpu.make_async_remote_copy(
            src_ref=p_refs[a].at[mine], dst_ref=land_refs[a].at[2 * chip[0] + chip[1]],
            send_sem=send_sems.at[a, j], recv_sem=recv_sems.at[a, j],
            device_id=(*chip, c), device_id_type=MESH) for a in range(n) for j, chip in enumerate(chips)]
        return owns, sends, recvs

    def start(ins, outs, sems):
        owns, sends, _ = copies(ins, outs, sems)
        for cp in owns + sends:
            cp.start()

    def finish(ins, outs, sems):
        owns, sends, recvs = copies(ins, outs, sems)
        for cp in recvs:
            cp.wait_recv()
        for cp in sends:
            cp.wait_send()
        for cp in owns:
            cp.wait()

    return _Job(ps, [jax.ShapeDtypeStruct(p.shape, p.dtype) for p in ps],
                [pltpu.SemaphoreType.DMA((n, 3)), pltpu.SemaphoreType.DMA((n, 3)),
                 pltpu.SemaphoreType.DMA((n,))], start, finish)


def _all_reduce_small(buf, name):
    rows = buf.shape[0]

    def body(b_ref, o_ref, slots, send_sems, recv_sems):
        x, y, c, _ = _place()
        me = 4 * x + 2 * y + c
        slots[me] = b_ref[...]
        cps = []
        for k in range(1, N_DEV):
            fx, fy, fc = (k >> 2) & 1, (k >> 1) & 1, k & 1
            peer = (x ^ fx, y ^ fy, c ^ fc)
            cps.append(pltpu.make_async_remote_copy(
                src_ref=b_ref, dst_ref=slots.at[me],
                send_sem=send_sems.at[k - 1], recv_sem=recv_sems.at[k - 1],
                device_id=peer, device_id_type=MESH))
        for cp in cps:
            cp.start()
        for k in range(1, N_DEV):
            fx, fy, fc = (k >> 2) & 1, (k >> 1) & 1, k & 1
            src = 4 * (x ^ fx) + 2 * (y ^ fy) + (c ^ fc)
            pltpu.make_async_remote_copy(
                src_ref=b_ref, dst_ref=slots.at[src],
                send_sem=send_sems.at[k - 1], recv_sem=recv_sems.at[k - 1],
                device_id=(x ^ fx, y ^ fy, c ^ fc), device_id_type=MESH).wait_recv()
        for cp in cps:
            cp.wait_send()
        acc = slots[0]
        for k in range(1, N_DEV):
            acc = acc + slots[k]
        o_ref[...] = acc

    vm = pl.BlockSpec(memory_space=pltpu.VMEM)
    return pl.pallas_call(
        body, name=name, out_shape=jax.ShapeDtypeStruct(buf.shape, F32),
        in_specs=[vm], out_specs=vm,
        scratch_shapes=[pltpu.VMEM((N_DEV, rows, LANES), F32),
                        pltpu.SemaphoreType.DMA((N_DEV - 1,)), pltpu.SemaphoreType.DMA((N_DEV - 1,))],
    )(buf)


def _row_tile(rows, pref=2048):
    for t in range(min(pref, rows) - min(pref, rows) % 8, 0, -8):
        if rows % t == 0:
            return t
    return rows


def _tile2(rows, cols, nbuf):
    budget = VMEM_LIMIT // 2 // (2 * nbuf * 4)
    for t in range(min(512, rows) // 16 * 16, 0, -16):
        if rows % t == 0 and t * cols <= budget:
            return t, cols
    return rows, _pick(cols, max(LANES, budget // rows // LANES * LANES))


def _pair_add(g, land, core, name):
    _, r, c = g.shape
    tr, tc = _tile2(r, c, 3)

    def body(core_ref, g_ref, l_ref, o_ref):
        o_ref[...] = (g_ref[...].astype(F32) + l_ref[...].astype(F32)).astype(o_ref.dtype)

    blk = pl.BlockSpec((1, tr, tc), lambda k, i, j, cr: (k, i, j))
    return pl.pallas_call(
        body, name=name, out_shape=jax.ShapeDtypeStruct(land.shape, land.dtype),
        grid_spec=pltpu.PrefetchScalarGridSpec(
            num_scalar_prefetch=1, grid=(4, r // tr, c // tc),
            in_specs=[pl.BlockSpec((1, tr, tc), lambda k, i, j, cr: (2 * k + cr[0], i, j)), blk],
            out_specs=blk),
        compiler_params=_params("parallel", "parallel", "parallel"),
    )(core, g, land)


def _adamw(w, parts, m, v, name):
    shp = w.shape
    cols = shp[-1]
    w2, m2, v2 = (t.reshape(-1, cols) for t in (w, m, v))
    rows = w2.shape[0]
    n = parts.shape[0]
    assert parts.shape[1:] == (rows, cols), (parts.shape, shp)
    tr, tc = _tile2(rows, cols, n + 7)
    c1 = 1.0 / (1.0 - ADAM_B1 ** ADAM_STEP)
    c2 = 1.0 / (1.0 - ADAM_B2 ** ADAM_STEP)

    def body(*refs):
        p_refs = refs[:n]
        w_ref, m_ref, v_ref, g_ref, d_ref, nm_ref, nv_ref = refs[n:]
        gv = p_refs[0][0].astype(F32)
        for p_ref in p_refs[1:]:
            gv = gv + p_ref[0].astype(F32)
        nm = ADAM_B1 * m_ref[...] + (1.0 - ADAM_B1) * gv
        nv = ADAM_B2 * v_ref[...] + (1.0 - ADAM_B2) * (gv * gv)
        g_ref[...] = gv
        nm_ref[...] = nm
        nv_ref[...] = nv
        d_ref[...] = -ADAM_LR * ((nm * c1) / (jnp.sqrt(nv * c2) + ADAM_EPS) + ADAM_WD * w_ref[...])

    blk = pl.BlockSpec((tr, tc), lambda i, j: (i, j))
    slab = lambda k: pl.BlockSpec((1, tr, tc), lambda i, j: (k, i, j))
    out = jax.ShapeDtypeStruct((rows, cols), F32)
    outs = pl.pallas_call(
        body, name=name, out_shape=[out] * 4, grid=(rows // tr, cols // tc),
        in_specs=[slab(k) for k in range(n)] + [blk] * 3, out_specs=[blk] * 4,
        compiler_params=_params("parallel", "parallel"),
    )(*([parts] * n), w2, m2, v2)
    return tuple(o.reshape(shp) for o in outs)


def _pack_rows(arrs):
    rows = []
    for a in arrs:
        flat = a.reshape(-1).astype(F32)
        pad = (-flat.shape[0]) % LANES
        if pad:
            flat = jnp.concatenate([flat, jnp.zeros((pad,), F32)])
        rows.append(flat.reshape(-1, LANES))
    out = jnp.concatenate(rows, axis=0)
    pad = (-out.shape[0]) % 8
    if pad:
        out = jnp.concatenate([out, jnp.zeros((pad, LANES), F32)], axis=0)
    return out


def _unpack_rows(buf, shapes):
    outs, r = [], 0
    for s in shapes:
        n = math.prod(s)
        nr = -(-n // LANES)
        outs.append(buf[r:r + nr].reshape(-1)[:n].reshape(s))
        r += nr
    return outs


BIG = ("ffn1_w1", "ffn1_w3", "ffn1_w2", "w_in", "w_out", "ffn2_w1", "ffn2_w3", "ffn2_w2")
ROW_SHARDED = ("ffn1_w2", "w_out", "ffn2_w2")
SMALL = ("ln1_g", "ln1_b", "hgrn_lb", "hgrn_norm_g", "mlstm_conv_w", "mlstm_conv_b", "mlstm_ig_b",
         "mlstm_fg_b", "mlstm_norm_g", "ln2_g", "ln2_b", "ln3_g", "ln3_b")
WEIGHTS = ("ffn1_w1", "ffn1_w3", "ffn1_w2", "ln1_g", "ln1_b", "w_in", "hgrn_lb", "hgrn_norm_g",
           "mlstm_conv_w", "mlstm_conv_b", "mlstm_ig_b", "mlstm_fg_b", "mlstm_norm_g", "w_out",
           "ln2_g", "ln2_b", "ffn2_w1", "ffn2_w3", "ffn2_w2", "ln3_g", "ln3_b")


def kernel(x, ffn1_w1, ffn1_w3, ffn1_w2, ln1_g, ln1_b, w_in, hgrn_lb, hgrn_norm_g, mlstm_conv_w, mlstm_conv_b, mlstm_ig_b, mlstm_fg_b, mlstm_norm_g, w_out, ln2_g, ln2_b, ffn2_w1, ffn2_w3, ffn2_w2, ln3_g, ln3_b, loss_target, m_ffn1_w1, m_ffn1_w3, m_ffn1_w2, m_ln1_g, m_ln1_b, m_w_in, m_hgrn_lb, m_hgrn_norm_g, m_mlstm_conv_w, m_mlstm_conv_b, m_mlstm_ig_b, m_mlstm_fg_b, m_mlstm_norm_g, m_w_out, m_ln2_g, m_ln2_b, m_ffn2_w1, m_ffn2_w3, m_ffn2_w2, m_ln3_g, m_ln3_b, v_ffn1_w1, v_ffn1_w3, v_ffn1_w2, v_ln1_g, v_ln1_b, v_w_in, v_hgrn_lb, v_hgrn_norm_g, v_mlstm_conv_w, v_mlstm_conv_b, v_mlstm_ig_b, v_mlstm_fg_b, v_mlstm_norm_g, v_w_out, v_ln2_g, v_ln2_b, v_ffn2_w1, v_ffn2_w3, v_ffn2_w2, v_ln3_g, v_ln3_b):
    args = (ffn1_w1, ffn1_w3, ffn1_w2, ln1_g, ln1_b, w_in, hgrn_lb, hgrn_norm_g, mlstm_conv_w,
            mlstm_conv_b, mlstm_ig_b, mlstm_fg_b, mlstm_norm_g, w_out, ln2_g, ln2_b, ffn2_w1, ffn2_w3,
            ffn2_w2, ln3_g, ln3_b)
    ms = (m_ffn1_w1, m_ffn1_w3, m_ffn1_w2, m_ln1_g, m_ln1_b, m_w_in, m_hgrn_lb, m_hgrn_norm_g,
          m_mlstm_conv_w, m_mlstm_conv_b, m_mlstm_ig_b, m_mlstm_fg_b, m_mlstm_norm_g, m_w_out, m_ln2_g,
          m_ln2_b, m_ffn2_w1, m_ffn2_w3, m_ffn2_w2, m_ln3_g, m_ln3_b)
    vs = (v_ffn1_w1, v_ffn1_w3, v_ffn1_w2, v_ln1_g, v_ln1_b, v_w_in, v_hgrn_lb, v_hgrn_norm_g,
          v_mlstm_conv_w, v_mlstm_conv_b, v_mlstm_ig_b, v_mlstm_fg_b, v_mlstm_norm_g, v_w_out, v_ln2_g,
          v_ln2_b, v_ffn2_w1, v_ffn2_w3, v_ffn2_w2, v_ln3_g, v_ln3_b)
    w = dict(zip(WEIGHTS, args))
    m = dict(zip(WEIGHTS, ms))
    v = dict(zip(WEIGHTS, vs))
    core = lax.axis_index("c")
    dev = 4 * lax.axis_index("x") + 2 * lax.axis_index("y") + core
    p_lb = []

    def small_params(gathered_small):
        lb_sh, cw_sh = zip(*[_unpack_rows(gathered_small[i], [hgrn_lb.shape, mlstm_conv_w.shape])
                             for i in range(N_DEV)])
        hgrn_lb_full = jnp.concatenate(lb_sh, axis=-1)
        conv_w_full = jnp.concatenate(cw_sh, axis=-1)[0]
        p_lb.append(jax.nn.softmax(hgrn_lb_full, axis=1))
        return dict(ln1_g=ln1_g, ln1_b=ln1_b, ln2_g=ln2_g, ln2_b=ln2_b, ln3_g=ln3_g, ln3_b=ln3_b,
                    lb=p_lb[0][:, 0], hgrn_norm_g=hgrn_norm_g, conv_w=conv_w_full, conv_b=mlstm_conv_b,
                    ig_b=mlstm_ig_b[0], fg_b=mlstm_fg_b[0], mlstm_norm_g=mlstm_norm_g)

    def rows_first(t, k):
        return t[0] if k in ROW_SHARDED else t[0].T

    loss_row, grad_x, land2, G = _device_step(
        x[0], loss_target[0], {k: rows_first(w[k], k).astype(BF16) for k in BIG},
        _pack_rows([hgrn_lb, mlstm_conv_w]), small_params, jnp.reshape(core, (1,)).astype(jnp.int32))
    p_lb = p_lb[0]

    dlb = G["lb"]
    g_lb = jnp.stack([dlb * p_lb[:, 0] * (1.0 - p_lb[:, 0]), -dlb * p_lb[:, 0] * p_lb[:, 1]], axis=1)
    small_full = {"ln1_g": G["ln1_g"], "ln1_b": G["ln1_b"], "hgrn_lb": g_lb, "hgrn_norm_g": G["hgrn_norm_g"],
                  "mlstm_conv_w": G["conv_w"][None], "mlstm_conv_b": G["conv_b"],
                  "mlstm_ig_b": G["ig_b"][None], "mlstm_fg_b": G["fg_b"][None],
                  "mlstm_norm_g": G["mlstm_norm_g"], "ln2_g": G["ln2_g"], "ln2_b": G["ln2_b"],
                  "ln3_g": G["ln3_g"], "ln3_b": G["ln3_b"]}
    small_list = [small_full[k] for k in SMALL] + [loss_row]
    reduced = _all_reduce_small(_pack_rows(small_list), "all_reduce_small")
    red = _unpack_rows(reduced, [a.shape for a in small_list])
    loss = red[-1][0, 0]
    small_g = {}
    for k, gk in zip(SMALL, red[:-1]):
        if k in ("hgrn_lb", "mlstm_conv_w"):
            n = w[k].shape[-1]
            gk = lax.dynamic_slice_in_dim(gk, dev * n, n, axis=gk.ndim - 1)
        small_g[k] = gk

    grads, delta, new_m, new_v = {}, {}, {}, {}
    for k in BIG:
        res = _adamw(rows_first(w[k], k), land2[k], rows_first(m[k], k), rows_first(v[k], k), "adamw_" + k)
        grads[k], delta[k], new_m[k], new_v[k] = [(r if k in ROW_SHARDED else r.T)[None] for r in res]
    sm = _adamw(_pack_rows([w[k] for k in SMALL]), _pack_rows([small_g[k] for k in SMALL])[None],
                _pack_rows([m[k] for k in SMALL]), _pack_rows([v[k] for k in SMALL]), "adamw_small")
    shapes = [w[k].shape for k in SMALL]
    for dst, buf in zip((grads, delta, new_m, new_v), sm):
        for k, val in zip(SMALL, _unpack_rows(buf, shapes)):
            dst[k] = val
    return (loss, grad_x[None], *[grads[k] for k in WEIGHTS], *[delta[k] for k in WEIGHTS],
            *[new_m[k] for k in WEIGHTS], *[new_v[k] for k in WEIGHTS])
```

```python
import math

import jax
import jax.numpy as jnp
from jax import lax
from jax.experimental import pallas as pl
from jax.experimental.pallas import tpu as pltpu

F32 = jnp.float32
BF16 = jnp.bfloat16

CHUNK = 64
HGRN_HEAD_DIM = 128
CONV_WIDTH = 5
DN_ALPHA = 2.0 ** 0.25
LN_EPS = 1e-5
NORM_EPS = 1e-6
M_INIT = -1e30
NEG = -1e30
EXP_CLAMP = 80.0
ADAM_LR = 0.001
ADAM_B1 = 0.9
ADAM_B2 = 0.999
ADAM_EPS = 1e-08
ADAM_WD = 0.01
ADAM_STEP = 10
N_DEV = 8
LANES = 128
VMEM_LIMIT = 56 * 1024 * 1024
MESH = pl.DeviceIdType.MESH


def _params(*sem):
    return pltpu.CompilerParams(dimension_semantics=sem, vmem_limit_bytes=VMEM_LIMIT)


class _Job:
    def __init__(self, inputs, out_shape, sems, start, finish, middle=None):
        self.inputs, self.out_shape, self.sems = list(inputs), list(out_shape), list(sems)
        self.start, self.finish = start, finish
        self.middle = middle or (lambda ins, outs, sems: None)


def _join(*jobs):
    def split(refs, counts):
        out, p = [], 0
        for c in counts:
            out.append(refs[p:p + c])
            p += c
        return out

    n_in = [len(j.inputs) for j in jobs]
    n_out = [len(j.out_shape) for j in jobs]
    n_sem = [len(j.sems) for j in jobs]

    def start(ins, outs, sems):
        for j, i, o, s in zip(jobs, split(ins, n_in), split(outs, n_out), split(sems, n_sem)):
            j.start(i, o, s)

    def middle(ins, outs, sems):
        for j, i, o, s in zip(jobs, split(ins, n_in), split(outs, n_out), split(sems, n_sem)):
            j.middle(i, o, s)

    def finish(ins, outs, sems):
        for j, i, o, s in zip(jobs, split(ins, n_in), split(outs, n_out), split(sems, n_sem)):
            j.finish(i, o, s)

    return _Job(sum((j.inputs for j in jobs), []), sum((j.out_shape for j in jobs), []),
                sum((j.sems for j in jobs), []), start, finish, middle)


def _pcall(body, *, name, out_shape, grid, in_specs, out_specs, sem, scratch_shapes=(), comm=None):
    single = not isinstance(out_shape, (list, tuple))
    out_shape = [out_shape] if single else list(out_shape)
    out_specs = [out_specs] if single else list(out_specs)
    in_specs, scratch_shapes = list(in_specs), list(scratch_shapes)
    if comm is None:
        call = pl.pallas_call(body, name=name, out_shape=out_shape, grid=grid, in_specs=in_specs,
                              out_specs=out_specs, scratch_shapes=scratch_shapes,
                              compiler_params=_params(*sem))

        def run(*args):
            outs = call(*args)
            return outs[0] if single else outs
        return run

    n_in, n_out, n_sc = len(in_specs), len(out_shape), len(scratch_shapes)
    c_in, c_out = len(comm.inputs), len(comm.out_shape)

    def hosted(*refs):
        ins, cins = refs[:n_in], refs[n_in:n_in + c_in]
        p = n_in + c_in
        outs, couts = refs[p:p + n_out], refs[p + n_out:p + n_out + c_out]
        p += n_out + c_out
        scratch, csems = refs[p:p + n_sc], refs[p + n_sc:]
        first = pl.program_id(0) == 0
        half = pl.program_id(0) == grid[0] // 2
        last = pl.program_id(0) == grid[0] - 1
        for d in range(1, len(grid)):
            first = first & (pl.program_id(d) == 0)
            half = half & (pl.program_id(d) == 0)
            last = last & (pl.program_id(d) == grid[d] - 1)

        @pl.when(first)
        def _():
            comm.start(cins, couts, csems)

        if grid[0] >= 2:
            @pl.when(half)
            def _():
                comm.middle(cins, couts, csems)

        body(*ins, *outs, *scratch)

        @pl.when(last)
        def _():
            if grid[0] < 2:
                comm.middle(cins, couts, csems)
            comm.finish(cins, couts, csems)

    any_spec = pl.BlockSpec(memory_space=pl.ANY)
    call = pl.pallas_call(
        hosted, name=name, out_shape=out_shape + comm.out_shape, grid=grid,
        in_specs=in_specs + [any_spec] * c_in, out_specs=out_specs + [any_spec] * c_out,
        scratch_shapes=scratch_shapes + comm.sems,
        compiler_params=_params(*(["arbitrary"] * len(grid))))

    def run(*args):
        res = call(*args, *comm.inputs)
        outs, couts = res[:n_out], list(res[n_out:])
        return (outs[0] if single else outs), couts
    return run


def _sigmoid(x):
    return 1.0 / (1.0 + jnp.exp(-x))


def _log_sigmoid(x):
    return jnp.minimum(x, 0.0) - jnp.log(1.0 + jnp.exp(-jnp.abs(x)))


def _dot(a, b, dims):
    return lax.dot_general(a.astype(BF16), b.astype(BF16), (dims, ((), ())),
                           preferred_element_type=F32)


def _dot3(a, b, dims):
    ah = a.astype(BF16)
    al = (a - ah.astype(F32)).astype(BF16)
    bh = b.astype(BF16)
    bl = (b - bh.astype(F32)).astype(BF16)
    d = (dims, ((), ()))
    out = lax.dot_general(ah, bh, d, preferred_element_type=F32)
    out = out + lax.dot_general(ah, bl, d, preferred_element_type=F32)
    return out + lax.dot_general(al, bh, d, preferred_element_type=F32)


_DIMS = {"nn": ((1,), (0,)), "nt": ((1,), (1,)), "tn": ((0,), (0,))}


def _dot_nn(a, b):
    return _dot3(a, b, _DIMS["nn"])


def _dot_nt(a, b):
    return _dot3(a, b, _DIMS["nt"])


def _dot_tn(a, b):
    return _dot3(a, b, _DIMS["tn"])


def _split3(x):
    hi = x.astype(BF16)
    r1 = x - hi.astype(F32)
    mid = r1.astype(BF16)
    lo = (r1 - mid.astype(F32)).astype(BF16)
    return hi, mid, lo


def _dot01(mask01, x, mode="nn"):
    m = mask01.astype(BF16)
    hi, mid, lo = _split3(x)
    d = (_DIMS[mode], ((), ()))
    out = lax.dot_general(m, hi, d, preferred_element_type=F32)
    out = out + lax.dot_general(m, mid, d, preferred_element_type=F32)
    return out + lax.dot_general(m, lo, d, preferred_element_type=F32)


def _matmul(a, b, *, mode="nn", out_dtype=F32, tm=512, tn=512, tk=None,
            add=None, scale=1.0, add_scale=1.0, name, comm=None, rows=None):
    if mode == "nn":
        (M, K), (K2, N) = a.shape, b.shape
        K2 = rows or K2
    elif mode == "nt":
        (M, K), (N, K2) = a.shape, b.shape
        N = rows or N
    else:
        (K, M), (K2, N) = a.shape, b.shape
    assert K == K2, (a.shape, b.shape, mode)
    tm, tn = min(tm, M), min(tn, N)
    tk = min(tk or K, K)
    assert M % tm == 0 and N % tn == 0 and K % tk == 0, (M, N, K, tm, tn, tk)
    nk = K // tk
    dims = _DIMS[mode]
    has_add = add is not None

    def body(*refs):
        if has_add:
            a_ref, b_ref, add_ref, o_ref = refs[:4]
        else:
            a_ref, b_ref, o_ref = refs[:3]
            add_ref = None
        acc_ref = refs[-1] if nk > 1 else None

        def finish(acc):
            out = acc if scale == 1.0 else acc * scale
            if has_add:
                out = out + add_ref[...].astype(F32) * add_scale
            o_ref[...] = out.astype(o_ref.dtype)

        if nk == 1:
            finish(_dot(a_ref[...], b_ref[...], dims))
        else:
            k = pl.program_id(2)

            @pl.when(k == 0)
            def _():
                acc_ref[...] = jnp.zeros_like(acc_ref)

            acc_ref[...] += _dot(a_ref[...], b_ref[...], dims)

            @pl.when(k == nk - 1)
            def _():
                finish(acc_ref[...])

    if mode == "tn":
        a_spec = pl.BlockSpec((tk, tm), lambda i, j, k: (k, i))
    else:
        a_spec = pl.BlockSpec((tm, tk), lambda i, j, k: (i, k))
    if mode == "nt":
        b_spec = pl.BlockSpec((tn, tk), lambda i, j, k: (j, k))
    else:
        b_spec = pl.BlockSpec((tk, tn), lambda i, j, k: (k, j))
    o_spec = pl.BlockSpec((tm, tn), lambda i, j, k: (i, j))
    in_specs = [a_spec, b_spec] + ([o_spec] if has_add else [])
    args = (a, b) + ((add,) if has_add else ())
    return _pcall(
        body, name=name, comm=comm,
        out_shape=jax.ShapeDtypeStruct((M, N), out_dtype),
        grid=(M // tm, N // tn, nk),
        in_specs=in_specs, out_specs=o_spec,
        scratch_shapes=[pltpu.VMEM((tm, tn), F32)] if nk > 1 else [],
        sem=("parallel", "parallel", "arbitrary"),
    )(*args)


def _ln_stats(x):
    mu = jnp.mean(x, axis=-1, keepdims=True)
    xc = x - mu
    var = jnp.mean(xc * xc, axis=-1, keepdims=True)
    rstd = lax.rsqrt(var + LN_EPS)
    return xc * rstd, rstd


def _ln_fwd(r, g, b, *, tm=256, name):
    T, D = r.shape
    tm = min(tm, T)

    def body(r_ref, g_ref, b_ref, y_ref, yb_ref):
        xhat, _ = _ln_stats(r_ref[...])
        y = xhat * g_ref[...] + b_ref[...]
        y_ref[...] = y
        yb_ref[...] = y.astype(BF16)

    row = pl.BlockSpec((tm, D), lambda i: (i, 0))
    vec = pl.BlockSpec((1, D), lambda i: (0, 0))
    return pl.pallas_call(
        body, name=name,
        out_shape=[jax.ShapeDtypeStruct((T, D), F32), jax.ShapeDtypeStruct((T, D), BF16)],
        grid=(T // tm,), in_specs=[row, vec, vec], out_specs=[row, row],
        compiler_params=_params("parallel"),
    )(r, g, b)


def _cast(x, *, tm=512, name, comm=None):
    T, D = x.shape
    tm = min(tm, T)

    def body(x_ref, xb_ref):
        xb_ref[...] = x_ref[...].astype(BF16)

    row = pl.BlockSpec((tm, D), lambda i: (i, 0))
    return _pcall(
        body, name=name, comm=comm, out_shape=jax.ShapeDtypeStruct((T, D), BF16),
        grid=(T // tm,), in_specs=[row], out_specs=row, sem=("parallel",),
    )(x)


def _ln_bwd(dy, r, g, *, tm=256, name, b=None, target=None):
    T, D = r.shape
    tm = min(tm, T)
    with_loss = target is not None

    def body(*refs):
        if with_loss:
            r_ref, g_ref, b_ref, t_ref, dr_ref, drb_ref, dg_ref, db_ref, loss_ref = refs
        else:
            dy_ref, r_ref, g_ref, dr_ref, drb_ref, dg_ref, db_ref = refs
        i = pl.program_id(0)
        xhat, rstd = _ln_stats(r_ref[...])
        gg = g_ref[...]
        if with_loss:
            err = xhat * gg + b_ref[...] - t_ref[...]
            dyv = err * (1.0 / D)
            part = jnp.sum(jnp.sum(err * err, axis=1, keepdims=True), axis=0, keepdims=True)
            part = jnp.broadcast_to(part * (0.5 / D), (1, LANES))
        else:
            dyv = dy_ref[...]
        dxh = dyv * gg
        m1 = jnp.mean(dxh, axis=-1, keepdims=True)
        m2 = jnp.mean(dxh * xhat, axis=-1, keepdims=True)
        dr = rstd * (dxh - m1 - xhat * m2)
        dr_ref[...] = dr
        drb_ref[...] = dr.astype(BF16)
        dgp = jnp.sum(dyv * xhat, axis=0, keepdims=True)
        dbp = jnp.sum(dyv, axis=0, keepdims=True)

        @pl.when(i == 0)
        def _():
            dg_ref[...] = dgp
            db_ref[...] = dbp
            if with_loss:
                loss_ref[...] = part

        @pl.when(i > 0)
        def _():
            dg_ref[...] += dgp
            db_ref[...] += dbp
            if with_loss:
                loss_ref[...] += part

    row = pl.BlockSpec((tm, D), lambda i: (i, 0))
    vec = pl.BlockSpec((1, D), lambda i: (0, 0))
    out_shape = [jax.ShapeDtypeStruct((T, D), F32), jax.ShapeDtypeStruct((T, D), BF16),
                 jax.ShapeDtypeStruct((1, D), F32), jax.ShapeDtypeStruct((1, D), F32)]
    out_specs = [row, row, vec, vec]
    if with_loss:
        in_specs, args = [row, vec, vec, row], (r, g, b, target)
        out_shape.append(jax.ShapeDtypeStruct((1, LANES), F32))
        out_specs.append(pl.BlockSpec((1, LANES), lambda i: (0, 0)))
    else:
        in_specs, args = [row, row, vec], (dy, r, g)
    return pl.pallas_call(
        body, name=name, out_shape=out_shape, grid=(T // tm,),
        in_specs=in_specs, out_specs=out_specs,
        compiler_params=_params("arbitrary"),
    )(*args)


def _ffn_up(xb, w1t, w3t, *, a=None, tm=1024, tf=512, name, comm=None):
    F, D = w3t.shape
    T = xb.shape[0]
    tm, tf = _pick(T, tm), _pick(F, tf)
    given = a is not None

    def body(x_ref, w1_ref, w3_ref, a_ref, b_ref, h_ref, ht_ref):
        xv = x_ref[...]
        av = w1_ref[...].astype(F32) if given else _dot(xv, w1_ref[...], _DIMS["nt"])
        b = _dot(xv, w3_ref[...], _DIMS["nt"])
        a_ref[...] = av.astype(BF16)
        b_ref[...] = b.astype(BF16)
        h = av * _sigmoid(av) * b
        h_ref[...] = h.astype(BF16)
        ht_ref[...] = h.T.astype(BF16)

    ij = (lambda p, q: (p, q)) if given else (lambda p, q: (q, p))
    wsp = pl.BlockSpec((tf, D), lambda p, q: (ij(p, q)[1], 0))
    osp = pl.BlockSpec((tm, tf), lambda p, q: ij(p, q))
    act = jax.ShapeDtypeStruct((T, F), BF16)
    return _pcall(
        body, name=name, comm=comm, out_shape=[act, act, act, jax.ShapeDtypeStruct((F, T), BF16)],
        grid=(T // tm, F // tf) if given else (F // tf, T // tm),
        in_specs=[pl.BlockSpec((tm, D), lambda p, q: (ij(p, q)[0], 0)), osp if given else wsp, wsp],
        out_specs=[osp, osp, osp, pl.BlockSpec((tf, tm), lambda p, q: ij(p, q)[::-1])],
        sem=("parallel", "parallel"),
    )(xb, a if given else w1t, w3t)


def _ffn_bwd_act(drb, w2, a, b, *, tm=1024, tf=512, name, comm=None):
    F, D = w2.shape
    T = drb.shape[0]
    tm, tf = _pick(T, tm), _pick(F, tf)

    def body(dr_ref, w2_ref, a_ref, b_ref, da_ref, db_ref, dat_ref, dbt_ref):
        d = 0.5 * _dot(dr_ref[...], w2_ref[...], _DIMS["nt"])
        av = a_ref[...].astype(F32)
        sig = _sigmoid(av)
        da = d * b_ref[...].astype(F32) * sig * (1.0 + av * (1.0 - sig))
        db = d * av * sig
        da_ref[...] = da.astype(BF16)
        db_ref[...] = db.astype(BF16)
        dat_ref[...] = da.T.astype(BF16)
        dbt_ref[...] = db.T.astype(BF16)

    asp = pl.BlockSpec((tm, tf), lambda i, j: (i, j))
    tsp = pl.BlockSpec((tf, tm), lambda i, j: (j, i))
    out, out_t = jax.ShapeDtypeStruct((T, F), BF16), jax.ShapeDtypeStruct((F, T), BF16)
    return _pcall(
        body, name=name, comm=comm, out_shape=[out, out, out_t, out_t], grid=(T // tm, F // tf),
        in_specs=[pl.BlockSpec((tm, D), lambda i, j: (i, 0)),
                  pl.BlockSpec((tf, D), lambda i, j: (j, 0)), asp, asp],
        out_specs=[asp, asp, tsp, tsp],
        sem=("parallel", "parallel"),
    )(drb, w2, a, b)


def _dz_pack(pieces, *, tm=256, name):
    groups = [p if isinstance(p, tuple) else (p,) for p in pieces]
    T = groups[0][0].shape[0]
    widths = [g[0].shape[1] for g in groups]
    W = sum(widths)
    tm = _pick(T, tm)
    flat = [a for g in groups for a in g]
    n = len(flat)

    def body(*refs):
        o_ref, ot_ref = refs[n], refs[n + 1]
        c, p = 0, 0
        for g, w in zip(groups, widths):
            v = refs[p][...].astype(F32)
            for r in refs[p + 1:p + len(g)]:
                v = v + r[...].astype(F32)
            p += len(g)
            o_ref[:, c:c + w] = v.astype(BF16)
            ot_ref[c:c + w, :] = v.T.astype(BF16)
            c += w

    return pl.pallas_call(
        body, name=name,
        out_shape=[jax.ShapeDtypeStruct((T, W), BF16), jax.ShapeDtypeStruct((W, T), BF16)],
        grid=(T // tm,),
        in_specs=[pl.BlockSpec((tm, a.shape[1]), lambda i: (i, 0)) for a in flat],
        out_specs=[pl.BlockSpec((tm, W), lambda i: (i, 0)), pl.BlockSpec((W, tm), lambda i: (0, i))],
        compiler_params=_params("parallel"),
    )(*flat)


def _ffn_dx(da, db, w1t, w3t, dr, *, tm=1024, tn=1024, tk=None, name, comm=None):
    T, F = da.shape
    D = w1t.shape[1]
    tm, tn = _pick(T, tm), _pick(D, tn)
    tk = _pick(F, tk or F // 4)
    nk = F // tk

    def body(da_ref, db_ref, w1_ref, w3_ref, dr_ref, o_ref, acc_ref):
        k = pl.program_id(2)

        @pl.when(k == 0)
        def _():
            acc_ref[...] = jnp.zeros_like(acc_ref)

        acc_ref[...] += (_dot(da_ref[...], w1_ref[...], _DIMS["nn"])
                         + _dot(db_ref[...], w3_ref[...], _DIMS["nn"]))

        @pl.when(k == nk - 1)
        def _():
            o_ref[...] = DN_ALPHA * dr_ref[...] + acc_ref[...]

    asp = pl.BlockSpec((tm, tk), lambda i, j, k: (i, k))
    wsp = pl.BlockSpec((tk, tn), lambda i, j, k: (k, j))
    osp = pl.BlockSpec((tm, tn), lambda i, j, k: (i, j))
    return _pcall(
        body, name=name, comm=comm, out_shape=jax.ShapeDtypeStruct((T, D), F32),
        grid=(T // tm, D // tn, nk), in_specs=[asp, asp, wsp, wsp, osp], out_specs=osp,
        scratch_shapes=[pltpu.VMEM((tm, tn), F32)],
        sem=("parallel", "parallel", "arbitrary"),
    )(da, db, w1t, w3t, dr)


def _chunk_mask(reverse, transpose=False):
    row = lax.broadcasted_iota(jnp.int32, (CHUNK, CHUNK), 0)
    col = lax.broadcasted_iota(jnp.int32, (CHUNK, CHUNK), 1)
    if reverse != transpose:
        return col >= row
    return col <= row


def _run_specs(specs, *, name, comm=None):
    counts = [(len(s["in_specs"]), len(s["out_specs"]), len(s["scratch_shapes"])) for s in specs]

    def body(*refs):
        p, parts = 0, [[], [], []]
        for kind in range(3):
            for c in counts:
                parts[kind].append(refs[p:p + c[kind]])
                p += c[kind]
        live = [s["body"](*parts[0][k], *parts[1][k], *parts[2][k]) for k, s in enumerate(specs)]
        while live:
            for g in list(live):
                if next(g, StopIteration) is StopIteration:
                    live.remove(g)

    cat = lambda key: [v for s in specs for v in s[key]]
    res = _pcall(body, name=name, comm=comm, out_shape=cat("out_shape"), grid=specs[0]["grid"],
                 in_specs=cat("in_specs"), out_specs=cat("out_specs"),
                 scratch_shapes=cat("scratch_shapes"), sem=("arbitrary",))(*cat("args"))
    outs, couts = res if comm is not None else (res, None)
    split, p = [], 0
    for c in counts:
        split.append(outs[p:p + c[1]])
        p += c[1]
    return (split, couts) if comm is not None else split


def _hgrn_pre(hq, hf, lb):
    sig = _sigmoid(hf)
    f = lb + (1.0 - lb) * sig
    q = hq * _sigmoid(hq) * (HGRN_HEAD_DIM ** -0.5)
    return q, f, sig


def _hgrn_decays(f, cmf, reverse):
    bc = _dot01(cmf, jnp.log(f))
    last = 0 if reverse else CHUNK - 1
    blast = bc[last:last + 1, :]
    bref = bc[CHUNK // 2:CHUNK // 2 + 1, :]
    eq = jnp.exp(jnp.minimum(bc - bref, EXP_CLAMP))
    ek = jnp.exp(jnp.minimum(bref - bc, EXP_CLAMP))
    return bc, blast, eq, ek


def _hgrn_fwd(z, lb, *, wh, reverse):
    T = z.shape[0]
    nch = T // CHUNK
    nh = wh // HGRN_HEAD_DIM
    hd = HGRN_HEAD_DIM

    def ci(i):
        return nch - 1 - i if reverse else i

    def body(hq_ref, hi_ref, hf_ref, lb_ref, o_ref, st_ref, s_ref):
        @pl.when(pl.program_id(0) == 0)
        def _():
            s_ref[...] = jnp.zeros_like(s_ref)

        cm = _chunk_mask(reverse)
        cmf = cm.astype(F32)
        H = range(nh)
        sls = [slice(h * hd, (h + 1) * hd) for h in H]
        q, f, _ = _hgrn_pre(hq_ref[...], hf_ref[...], lb_ref[...])
        v = hi_ref[...]
        k = 1.0 - f
        yield
        bc, blast, eq, ek = _hgrn_decays(f, cmf, reverse)
        yield
        qh, kh, qe = q * eq, k * ek, q * jnp.exp(bc)
        k2, eblast = k * jnp.exp(blast - bc), jnp.exp(blast)
        st = [s_ref[h] for h in H]
        yield
        att = [jnp.where(cm, _dot_nt(qh[:, sl], kh[:, sl]), 0.0) for sl in sls]
        yield
        inter = [_dot(qe[:, sls[h]], st[h], _DIMS["nt"]) for h in H]
        yield
        out = [inter[h] + _dot(att[h], v[:, sls[h]], _DIMS["nn"]) for h in H]
        yield
        snew = [eblast[:, sls[h]] * st[h] + _dot_tn(v[:, sls[h]], k2[:, sls[h]]) for h in H]
        yield
        for h in H:
            st_ref[0, h] = st[h]
            o_ref[:, sls[h]] = out[h]
            s_ref[h] = snew[h]

    blk = lambda c: pl.BlockSpec((CHUNK, wh), lambda i: (ci(i), c))
    return dict(
        body=body, grid=(nch,), args=[z, z, z, lb],
        out_shape=[jax.ShapeDtypeStruct((T, wh), F32),
                   jax.ShapeDtypeStruct((nch, nh, hd, hd), F32)],
        in_specs=[blk(0), blk(1), blk(3 + int(reverse)),
                  pl.BlockSpec((1, wh), lambda i: (0, 0))],
        out_specs=[blk(0), pl.BlockSpec((1, nh, hd, hd), lambda i: (ci(i), 0, 0, 0))],
        scratch_shapes=[pltpu.VMEM((nh, hd, hd), F32)])


def _hgrn_bwd(z, lb, do, states, *, wh, reverse):
    T = z.shape[0]
    nch = T // CHUNK
    nh = wh // HGRN_HEAD_DIM
    hd = HGRN_HEAD_DIM

    def ci(i):
        return i if reverse else nch - 1 - i

    def body(hq_ref, hi_ref, hf_ref, lb_ref, do_ref, st_ref,
             dhq_ref, dhi_ref, dhf_ref, dlb_ref, ds_ref, gs_ref):
        @pl.when(pl.program_id(0) == 0)
        def _():
            ds_ref[...] = jnp.zeros_like(ds_ref)
            gs_ref[...] = jnp.zeros_like(gs_ref)
            dlb_ref[...] = jnp.zeros_like(dlb_ref)

        cm = _chunk_mask(reverse)
        cmf = cm.astype(F32)
        cmtf = _chunk_mask(reverse, transpose=True).astype(F32)
        H = range(nh)
        sls = [slice(h * hd, (h + 1) * hd) for h in H]
        hq, lb = hq_ref[...], lb_ref[...]
        q, f, sig = _hgrn_pre(hq, hf_ref[...], lb)
        v, dov = hi_ref[...], do_ref[...]
        k = 1.0 - f
        yield
        bc, blast, eq, ek = _hgrn_decays(f, cmf, reverse)
        yield
        ebc, eb2, eblast = jnp.exp(bc), jnp.exp(blast - bc), jnp.exp(blast)
        qh, kh, qe, k2 = q * eq, k * ek, q * ebc, k * eb2
        st = [st_ref[0, h] for h in H]
        dst = [ds_ref[h] for h in H]
        yield
        att = [jnp.where(cm, _dot(qh[:, sl], kh[:, sl], _DIMS["nt"]), 0.0) for sl in sls]
        yield
        datt = [jnp.where(cm, _dot_nt(dov[:, sl], v[:, sl]), 0.0) for sl in sls]
        yield
        dq_a = [_dot_nn(datt[h], kh[:, sls[h]]) for h in H]
        yield
        dq_s = [_dot_nn(dov[:, sls[h]], st[h]) for h in H]
        yield
        dk_a = [_dot_tn(datt[h], qh[:, sls[h]]) for h in H]
        yield
        dk_s = [_dot_nn(v[:, sls[h]], dst[h]) for h in H]
        yield
        dv = [_dot(att[h], dov[:, sls[h]], _DIMS["tn"]) + _dot(k2[:, sls[h]], dst[h], _DIMS["nt"])
              for h in H]
        yield
        dsn = [eblast[:, sls[h]] * dst[h] + _dot_tn(dov[:, sls[h]], qe[:, sls[h]]) for h in H]
        for h in H:
            ds_ref[h] = dsn[h]
        yield
        dq = jnp.concatenate(dq_a, axis=1) * eq + ebc * jnp.concatenate(dq_s, axis=1)
        dk = jnp.concatenate(dk_a, axis=1) * ek + eb2 * jnp.concatenate(dk_s, axis=1)
        db = q * dq - k * dk
        yield
        dg = _dot01(cmtf, db) + gs_ref[...]
        gs_ref[...] += jnp.sum(db, axis=0, keepdims=True)
        yield
        df = dg / f - dk
        dhf_ref[...] = (df * (1.0 - lb) * sig * (1.0 - sig)).astype(BF16)
        dlb_ref[...] += jnp.sum(df * (1.0 - sig), axis=0, keepdims=True)
        sq = _sigmoid(hq)
        dhq_ref[...] = (dq * (HGRN_HEAD_DIM ** -0.5) * sq * (1.0 + hq * (1.0 - sq))).astype(BF16)
        dhi_ref[...] = jnp.concatenate(dv, axis=1).astype(BF16)

    blk = lambda c: pl.BlockSpec((CHUNK, wh), lambda i: (ci(i), c))
    vec = pl.BlockSpec((1, wh), lambda i: (0, 0))
    big = jax.ShapeDtypeStruct((T, wh), BF16)
    return dict(
        body=body, grid=(nch,), args=[z, z, z, lb, do, states],
        in_specs=[blk(0), blk(1), blk(3 + int(reverse)), vec, blk(0),
                  pl.BlockSpec((1, nh, hd, hd), lambda i: (ci(i), 0, 0, 0))],
        out_shape=[big, big, big, jax.ShapeDtypeStruct((1, wh), F32)],
        out_specs=[blk(0), blk(0), blk(0), vec],
        scratch_shapes=[pltpu.VMEM((nh, hd, hd), F32), pltpu.VMEM((1, wh), F32)])


def _mlstm_intra(cm, cmt, ig_row, ig_col, xf_row, xf_col, m_st):
    lf_row, lf_col = _log_sigmoid(xf_row), _log_sigmoid(xf_col)
    bcol = jnp.sum(jnp.where(cm, lf_row, 0.0), axis=1, keepdims=True)
    brow = jnp.sum(jnp.where(cmt, lf_col, 0.0), axis=0, keepdims=True)
    blast = jnp.sum(lf_row, axis=1, keepdims=True)
    dmat = jnp.where(cm, bcol - brow + ig_row, NEG)
    m_inter = bcol + m_st
    m_t = jnp.maximum(m_inter, jnp.max(dmat, axis=1, keepdims=True))
    p = jnp.exp(dmat - m_t)
    inter = jnp.exp(m_inter - m_t)
    w_col = blast - bcol + ig_col
    m_new = jnp.maximum(blast + m_st, jnp.max(w_col, axis=0, keepdims=True))
    cs = jnp.exp(blast + m_st - m_new)
    kscale = jnp.exp(w_col - m_new)
    return p, inter, m_t, m_new, cs, kscale


def _mlstm_specs(T, wm, nhm, reverse, backward):
    nch = T // CHUNK
    dm = wm // nhm
    ng = 4 * nhm

    def ci(i):
        fwd_order = nch - 1 - i if reverse else i
        return nch - 1 - fwd_order if backward else fwd_order

    row = lambda w, c: pl.BlockSpec((CHUNK, w), lambda i: (ci(i), c))
    gates_row = pl.BlockSpec((1, ng, CHUNK), lambda i: (ci(i), 0, 0))
    bias_row = pl.BlockSpec((1, LANES), lambda i: (0, 0))
    bias_col = pl.BlockSpec((ng, 1), lambda i: (0, 0))
    st_c = pl.BlockSpec((1, nhm, dm, dm), lambda i: (ci(i), 0, 0, 0))
    st_n = pl.BlockSpec((1, nhm, 1, dm), lambda i: (ci(i), 0, 0, 0))
    st_m = pl.BlockSpec((1, nhm, 1, LANES), lambda i: (ci(i), 0, 0, 0))
    return nch, dm, ng, ci, row, gates_row, bias_row, bias_col, st_c, st_n, st_m


def _mlstm_fwd(qk, z, gc, gr, br, bcl, *, wm, nhm, vcol, reverse):
    T = qk.shape[0]
    nch, dm, ng, ci, row, gates_row, bias_row, bias_col, st_c, st_n, st_m = _mlstm_specs(
        T, wm, nhm, reverse, False)
    d = int(reverse)

    def body(q_ref, k_ref, v_ref, gc_ref, gr_ref, br_ref, bc_ref,
             h_ref, cst_ref, nst_ref, mst_ref, c_ref, n_ref, m_ref):
        @pl.when(pl.program_id(0) == 0)
        def _():
            c_ref[...] = jnp.zeros_like(c_ref)
            n_ref[...] = jnp.zeros_like(n_ref)
            m_ref[...] = jnp.full_like(m_ref, M_INIT)

        cm = _chunk_mask(reverse)
        cmt = _chunk_mask(reverse, transpose=True)
        G = gc_ref[...] + br_ref[...]
        Gr = gr_ref[0] + bc_ref[...]
        H = range(nhm)
        sls = [slice(h * dm, (h + 1) * dm) for h in H]
        m_all = [m_ref[h] for h in H]
        intra = [_mlstm_intra(cm, cmt, Gr[d * nhm + h:d * nhm + h + 1, :], G[:, d * nhm + h:d * nhm + h + 1],
                              Gr[2 * nhm + d * nhm + h:2 * nhm + d * nhm + h + 1, :],
                              G[:, 2 * nhm + d * nhm + h:2 * nhm + d * nhm + h + 1], m_all[h][:, 0:1])
                 for h in H]
        p, inter, m_t, m_new, cs, kscale = zip(*intra)
        yield
        q = [q_ref[:, sl] * (dm ** -0.5) for sl in sls]
        k = [k_ref[:, sl] for sl in sls]
        v = [v_ref[:, sl] for sl in sls]
        ct = [c_ref[h] for h in H]
        n = [n_ref[h] for h in H]
        yield
        sc = [_dot_nt(q[h], k[h]) * p[h] for h in H]
        yield
        qc = [_dot(q[h], ct[h], _DIMS["nt"]) for h in H]
        yield
        num = [_dot(sc[h], v[h], _DIMS["nn"]) + inter[h] * qc[h] for h in H]
        yield
        den = [jnp.sum(sc[h], axis=1, keepdims=True) + inter[h] * jnp.sum(q[h] * n[h], axis=1, keepdims=True)
               for h in H]
        kw = [k[h] * kscale[h] for h in H]
        yield
        cnew = [cs[h] * ct[h] + _dot_tn(v[h], kw[h]) for h in H]
        yield
        for h in H:
            cst_ref[0, h] = ct[h]
            nst_ref[0, h] = n[h]
            mst_ref[0, h] = m_all[h]
            h_ref[:, sls[h]] = num[h] / jnp.maximum(jnp.abs(den[h]), jnp.exp(-m_t[h]))
            c_ref[h] = cnew[h]
            n_ref[h] = cs[h] * n[h] + jnp.sum(kw[h], axis=0, keepdims=True)
            m_ref[h] = jnp.broadcast_to(m_new[h], (1, LANES))

    return dict(
        body=body, grid=(nch,), args=[qk, qk, z, gc, gr, br, bcl],
        out_shape=[jax.ShapeDtypeStruct((T, wm), F32),
                   jax.ShapeDtypeStruct((nch, nhm, dm, dm), F32),
                   jax.ShapeDtypeStruct((nch, nhm, 1, dm), F32),
                   jax.ShapeDtypeStruct((nch, nhm, 1, LANES), F32)],
        in_specs=[row(wm, 0), row(wm, 1), row(wm, vcol), row(LANES, 0), gates_row, bias_row, bias_col],
        out_specs=[row(wm, 0), st_c, st_n, st_m],
        scratch_shapes=[pltpu.VMEM((nhm, dm, dm), F32), pltpu.VMEM((nhm, 1, dm), F32),
                        pltpu.VMEM((nhm, 1, LANES), F32)])


def _mlstm_bwd(qk, z, gc, gr, br, bcl, dh, states, *, wm, nhm, vcol, reverse):
    T = qk.shape[0]
    nch, dm, ng, ci, row, gates_row, bias_row, bias_col, st_c, st_n, st_m = _mlstm_specs(
        T, wm, nhm, reverse, True)
    d = int(reverse)

    def body(q_ref, k_ref, v_ref, gc_ref, gr_ref, br_ref, bc_ref, dh_ref, cst_ref, nst_ref, mst_ref,
             dqk_ref, dv_ref, dgr_ref, dgs_ref, e_ref, en_ref, fs_ref):
        @pl.when(pl.program_id(0) == 0)
        def _():
            e_ref[...] = jnp.zeros_like(e_ref)
            en_ref[...] = jnp.zeros_like(en_ref)
            fs_ref[...] = jnp.zeros_like(fs_ref)
            dgs_ref[...] = jnp.zeros_like(dgs_ref)

        cm = _chunk_mask(reverse)
        cmt = _chunk_mask(reverse, transpose=True)
        row_i = lax.broadcasted_iota(jnp.int32, (CHUNK, CHUNK), 0)
        col_i = lax.broadcasted_iota(jnp.int32, (CHUNK, CHUNK), 1)
        eye = row_i == col_i
        G = gc_ref[...] + br_ref[...]
        Gr = gr_ref[0] + bc_ref[...]
        H = range(nhm)
        sls = [slice(h * dm, (h + 1) * dm) for h in H]
        xf_row = [Gr[2 * nhm + d * nhm + h:2 * nhm + d * nhm + h + 1, :] for h in H]
        intra = [_mlstm_intra(cm, cmt, Gr[d * nhm + h:d * nhm + h + 1, :], G[:, d * nhm + h:d * nhm + h + 1],
                              xf_row[h], G[:, 2 * nhm + d * nhm + h:2 * nhm + d * nhm + h + 1],
                              mst_ref[0, h][:, 0:1]) for h in H]
        p, inter, m_t, _, cs, kscale = zip(*intra)
        yield
        q = [q_ref[:, sls[h]] * (dm ** -0.5) for h in H]
        k = [k_ref[:, sls[h]] for h in H]
        v = [v_ref[:, sls[h]] for h in H]
        dhv = [dh_ref[:, sls[h]] for h in H]
        ct = [cst_ref[0, h] for h in H]
        n = [nst_ref[0, h] for h in H]
        et = [e_ref[h] for h in H]
        en = [en_ref[h] for h in H]
        carry = [fs_ref[h][:, 0:1] for h in H]
        yield
        sc = [_dot_nt(q[h], k[h]) * p[h] for h in H]
        yield
        qc = [_dot_nt(q[h], ct[h]) for h in H]
        yield
        num = [_dot_nn(sc[h], v[h]) + inter[h] * qc[h] for h in H]
        yield
        den = [jnp.sum(sc[h], axis=1, keepdims=True) + inter[h] * jnp.sum(q[h] * n[h], axis=1, keepdims=True)
               for h in H]
        floor = [jnp.exp(-m_t[h]) for h in H]
        nstab = [jnp.maximum(jnp.abs(den[h]), floor[h]) for h in H]
        gh = [dhv[h] / nstab[h] for h in H]
        dd = [-jnp.sum(dhv[h] * (num[h] / nstab[h]), axis=1, keepdims=True) / nstab[h]
              * jnp.where(jnp.abs(den[h]) > floor[h], jnp.sign(den[h]), 0.0) for h in H]
        yield
        dsqk = [(_dot_nt(gh[h], v[h]) + dd[h]) * p[h] for h in H]
        qi = [q[h] * inter[h] for h in H]
        kw = [k[h] * kscale[h] for h in H]
        yield
        ghc = [_dot_nn(gh[h], ct[h]) for h in H]
        yield
        vet = [_dot_nn(v[h], et[h]) for h in H]
        yield
        dq = [_dot_nn(dsqk[h], k[h]) + inter[h] * (ghc[h] + dd[h] * n[h]) for h in H]
        yield
        dk = [_dot_tn(dsqk[h], q[h]) + kscale[h] * (vet[h] + en[h]) for h in H]
        yield
        dv = [_dot(sc[h], gh[h], _DIMS["tn"]) + _dot(kw[h], et[h], _DIMS["nt"]) for h in H]
        yield
        e_new = [cs[h] * et[h] + _dot_tn(gh[h], qi[h]) for h in H]
        en_new = [cs[h] * en[h] + jnp.sum(qi[h] * dd[h], axis=0, keepdims=True) for h in H]
        yield
        di_col = [jnp.sum(k[h] * dk[h], axis=1, keepdims=True) for h in H]
        df_col = [jnp.sum(q[h] * dq[h], axis=1, keepdims=True) - di_col[h] for h in H]
        di_row = [jnp.sum(jnp.where(eye, di_col[h], 0.0), axis=0, keepdims=True) for h in H]
        dlf_row = [jnp.sum(jnp.where(cm, df_col[h], 0.0), axis=0, keepdims=True) + carry[h] for h in H]
        dxf_row = [dlf_row[h] * (1.0 - _sigmoid(xf_row[h])) for h in H]
        yield
        for h in H:
            sl = sls[h]
            slk = slice(wm + h * dm, wm + (h + 1) * dm)
            e_ref[h] = e_new[h]
            en_ref[h] = en_new[h]
            fs_ref[h] = jnp.broadcast_to(carry[h] + jnp.sum(df_col[h], axis=0, keepdims=True), (1, LANES))
            dgr_ref[0, h:h + 1, :] = di_row[h]
            dgr_ref[0, nhm + h:nhm + h + 1, :] = dxf_row[h]
            dgs_ref[h:h + 1, :] += jnp.broadcast_to(jnp.sum(di_row[h], axis=1, keepdims=True), (1, LANES))
            dgs_ref[nhm + h:nhm + h + 1, :] += jnp.broadcast_to(
                jnp.sum(dxf_row[h], axis=1, keepdims=True), (1, LANES))
            dqk_ref[:, sl] = (dq[h] * (dm ** -0.5)).astype(BF16)
            dqk_ref[:, slk] = dk[h].astype(BF16)
            dv_ref[:, sl] = dv[h].astype(BF16)

    return dict(
        body=body, grid=(nch,), args=[qk, qk, z, gc, gr, br, bcl, dh] + list(states),
        in_specs=[row(wm, 0), row(wm, 1), row(wm, vcol), row(LANES, 0), gates_row, bias_row, bias_col,
                  row(wm, 0), st_c, st_n, st_m],
        out_shape=[jax.ShapeDtypeStruct((T, 2 * wm), BF16), jax.ShapeDtypeStruct((T, wm), BF16),
                   jax.ShapeDtypeStruct((nch, 2 * nhm, CHUNK), F32),
                   jax.ShapeDtypeStruct((2 * nhm, LANES), F32)],
        out_specs=[row(2 * wm, 0), row(wm, 0),
                   pl.BlockSpec((1, 2 * nhm, CHUNK), lambda i: (ci(i), 0, 0)),
                   pl.BlockSpec((2 * nhm, LANES), lambda i: (0, 0))],
        scratch_shapes=[pltpu.VMEM((nhm, dm, dm), F32), pltpu.VMEM((nhm, 1, dm), F32),
                        pltpu.VMEM((nhm, 1, LANES), F32)])


def _conv_taps(x, w_ref):
    T = x.shape[0]
    t = lax.broadcasted_iota(jnp.int32, x.shape, 0)
    taps = []
    acc = None
    for j in range(CONV_WIDTH):
        s = CONV_WIDTH // 2 - j
        if s == 0:
            xs = x
        else:
            xs = jnp.where((t - s >= 0) & (t - s < T), pltpu.roll(x, s % T, 0), 0.0)
        taps.append(xs)
        term = w_ref[j:j + 1, :] * xs
        acc = term if acc is None else acc + term
    return taps, acc


def _conv_fwd(z, w, b, *, col0, tc=LANES, name):
    T = z.shape[0]
    C2 = w.shape[1]
    assert col0 % tc == 0 and C2 % tc == 0

    def body(z_ref, w_ref, b_ref, o_ref):
        _, acc = _conv_taps(z_ref[...], w_ref)
        c = acc + b_ref[...]
        o_ref[...] = c * _sigmoid(c)

    return pl.pallas_call(
        body, name=name, out_shape=jax.ShapeDtypeStruct((T, C2), F32),
        grid=(C2 // tc,),
        in_specs=[pl.BlockSpec((T, tc), lambda j: (0, col0 // tc + j)),
                  pl.BlockSpec((CONV_WIDTH, tc), lambda j: (0, j)),
                  pl.BlockSpec((1, tc), lambda j: (0, j))],
        out_specs=pl.BlockSpec((T, tc), lambda j: (0, j)),
        compiler_params=_params("parallel"),
    )(z, w, b)


def _conv_bwd(dy_fw, dy_bw, z, w, b, *, col0, tc=LANES, name):
    T = z.shape[0]
    C2 = w.shape[1]

    def body(dyf_ref, dyb_ref, z_ref, w_ref, b_ref, dx_ref, dw_ref, db_ref):
        taps, acc = _conv_taps(z_ref[...], w_ref)
        c = acc + b_ref[...]
        sg = _sigmoid(c)
        dc = (dyf_ref[...].astype(F32) + dyb_ref[...].astype(F32)) * sg * (1.0 + c * (1.0 - sg))
        t = lax.broadcasted_iota(jnp.int32, dc.shape, 0)
        dx = None
        for j in range(CONV_WIDTH):
            s = j - CONV_WIDTH // 2
            if s == 0:
                ds = dc
            else:
                ds = jnp.where((t - s >= 0) & (t - s < T), pltpu.roll(dc, s % T, 0), 0.0)
            term = w_ref[j:j + 1, :] * ds
            dx = term if dx is None else dx + term
            dw_ref[j:j + 1, :] = jnp.sum(dc * taps[j], axis=0, keepdims=True)
        dx_ref[...] = dx.astype(BF16)
        db_ref[...] = jnp.sum(dc, axis=0, keepdims=True)

    col = pl.BlockSpec((T, tc), lambda j: (0, j))
    wsp = pl.BlockSpec((CONV_WIDTH, tc), lambda j: (0, j))
    bsp = pl.BlockSpec((1, tc), lambda j: (0, j))
    return pl.pallas_call(
        body, name=name,
        out_shape=[jax.ShapeDtypeStruct((T, C2), BF16), jax.ShapeDtypeStruct((CONV_WIDTH, C2), F32),
                   jax.ShapeDtypeStruct((1, C2), F32)],
        grid=(C2 // tc,),
        in_specs=[col, col, pl.BlockSpec((T, tc), lambda j: (0, col0 // tc + j)), wsp, bsp],
        out_specs=[col, wsp, bsp],
        compiler_params=_params("parallel"),
    )(dy_fw, dy_bw, z, w, b)


def _mix_heads(o_fw, o_bw, h_fw, h_bw, wh, wm, nhm):
    out = []
    hd = HGRN_HEAD_DIM
    for h in range(wh // hd):
        sl = slice(h * hd, (h + 1) * hd)
        o = o_fw[:, sl] + o_bw[:, sl]
        r = lax.rsqrt(jnp.mean(o * o, axis=-1, keepdims=True) + NORM_EPS)
        out.append((0, sl, o * r, r))
    dm = wm // nhm
    for h in range(nhm):
        sl = slice(h * dm, (h + 1) * dm)
        x = h_fw[:, sl] + h_bw[:, sl]
        xc = x - jnp.mean(x, axis=-1, keepdims=True)
        r = lax.rsqrt(jnp.mean(xc * xc, axis=-1, keepdims=True) + NORM_EPS)
        out.append((1, sl, xc * r, r))
    return out


def _mix_specs(T, wh, wm, tm, gcol, ocol):
    rowh = pl.BlockSpec((tm, wh), lambda i: (i, 0))
    rowm = pl.BlockSpec((tm, wm), lambda i: (i, 0))
    hg = pl.BlockSpec((tm, wh), lambda i: (i, gcol))
    mo = pl.BlockSpec((tm, wm), lambda i: (i, ocol))
    vh = pl.BlockSpec((1, wh), lambda i: (0, 0))
    vm = pl.BlockSpec((1, wm), lambda i: (0, 0))
    return rowh, rowm, hg, mo, vh, vm


def _mix_fwd(o_fw, o_bw, h_fw, h_bw, z, gh, gm, *, nhm, gcol, ocol, tm=256, name):
    T, wh = o_fw.shape
    wm = h_fw.shape[1]
    tm = min(tm, T)
    rowh, rowm, hg, mo, vh, vm = _mix_specs(T, wh, wm, tm, gcol, ocol)

    def body(of_ref, ob_ref, hf_ref, hb_ref, hg_ref, mo_ref, gh_ref, gm_ref, y_ref, yt_ref):
        heads = _mix_heads(of_ref[...], ob_ref[...], hf_ref[...], hb_ref[...], wh, wm, nhm)
        for grp, sl, nv, _ in heads:
            if grp == 0:
                gate = hg_ref[:, sl]
                gate = gate * _sigmoid(gate)
                y = nv * gh_ref[:, sl] * gate
                osl = sl
            else:
                y = nv * gm_ref[:, sl] * _sigmoid(mo_ref[:, sl])
                osl = slice(wh + sl.start, wh + sl.stop)
            y_ref[:, osl] = y.astype(BF16)
            yt_ref[osl, :] = y.T.astype(BF16)

    return pl.pallas_call(
        body, name=name,
        out_shape=[jax.ShapeDtypeStruct((T, wh + wm), BF16), jax.ShapeDtypeStruct((wh + wm, T), BF16)],
        grid=(T // tm,),
        in_specs=[rowh, rowh, rowm, rowm, hg, mo, vh, vm],
        out_specs=[pl.BlockSpec((tm, wh + wm), lambda i: (i, 0)),
                   pl.BlockSpec((wh + wm, tm), lambda i: (0, i))],
        compiler_params=_params("parallel"),
    )(o_fw, o_bw, h_fw, h_bw, z, z, gh, gm)


def _mix_bwd(dy, o_fw, o_bw, h_fw, h_bw, z, gh, gm, *, nhm, gcol, ocol, tm=256, name, comm=None):
    T, wh = o_fw.shape
    wm = h_fw.shape[1]
    tm = min(tm, T)
    rowh, rowm, hg, mo, vh, vm = _mix_specs(T, wh, wm, tm, gcol, ocol)

    def body(dy_ref, of_ref, ob_ref, hf_ref, hb_ref, hg_ref, mo_ref, gh_ref, gm_ref,
             do_ref, dh_ref, dhg_ref, dmo_ref, dgh_ref, dgm_ref):
        @pl.when(pl.program_id(0) == 0)
        def _():
            dgh_ref[...] = jnp.zeros_like(dgh_ref)
            dgm_ref[...] = jnp.zeros_like(dgm_ref)

        heads = _mix_heads(of_ref[...], ob_ref[...], hf_ref[...], hb_ref[...], wh, wm, nhm)
        for grp, sl, nv, r in heads:
            if grp == 0:
                d = dy_ref[:, sl]
                x = hg_ref[:, sl]
                sg = _sigmoid(x)
                gate = x * sg
                g = gh_ref[:, sl]
                dgh_ref[:, sl] += jnp.sum(d * nv * gate, axis=0, keepdims=True)
                dhg_ref[:, sl] = (d * nv * g * sg * (1.0 + x * (1.0 - sg))).astype(BF16)
                t = d * g * gate
                do_ref[:, sl] = r * (t - nv * jnp.mean(t * nv, axis=-1, keepdims=True))
            else:
                d = dy_ref[:, slice(wh + sl.start, wh + sl.stop)]
                sg = _sigmoid(mo_ref[:, sl])
                g = gm_ref[:, sl]
                dgm_ref[:, sl] += jnp.sum(d * nv * sg, axis=0, keepdims=True)
                dmo_ref[:, sl] = (d * nv * g * sg * (1.0 - sg)).astype(BF16)
                t = d * g * sg
                dh_ref[:, sl] = r * (t - jnp.mean(t, axis=-1, keepdims=True)
                                     - nv * jnp.mean(t * nv, axis=-1, keepdims=True))

    return _pcall(
        body, name=name, comm=comm,
        out_shape=[jax.ShapeDtypeStruct((T, wh), F32), jax.ShapeDtypeStruct((T, wm), F32),
                   jax.ShapeDtypeStruct((T, wh), BF16), jax.ShapeDtypeStruct((T, wm), BF16),
                   jax.ShapeDtypeStruct((1, wh), F32), jax.ShapeDtypeStruct((1, wm), F32)],
        grid=(T // tm,),
        in_specs=[pl.BlockSpec((tm, wh + wm), lambda i: (i, 0)), rowh, rowh, rowm, rowm, hg, mo, vh, vm],
        out_specs=[rowh, rowm, rowh, rowm, vh, vm],
        sem=("arbitrary",),
    )(dy, o_fw, o_bw, h_fw, h_bw, z, z, gh, gm)


def _pick(n, pref):
    for c in range(pref - pref % LANES, 0, -LANES):
        if n % c == 0:
            return c
    return n


def _mm(a, b, mode="nn", *, tm=1024, tn=1024, tk=2048, **kw):
    if mode == "nn":
        (M, K), N = a.shape, b.shape[1]
    else:
        (M, K), N = a.shape, kw.get("rows") or b.shape[0]
    return _matmul(a, b, mode=mode, tm=_pick(M, tm), tn=_pick(N, tn), tk=_pick(K, tk), **kw)


def _device_step(x, target, sh, small_parts, small_params, core):
    T, D = x.shape

    def full(g):
        return g.reshape(-1, D)

    def shards(g):
        return g.reshape(N_DEV, -1, D)

    def pair_add(name, g, land):
        return _pair_add(g, land, core, "rs_add_" + name)

    def dw(lhs_t, rhs, name, scale=1.0):
        return shards(_mm(lhs_t, rhs, out_dtype=BF16, scale=scale, tm=704, tk=T, name=name))

    xb, (g_w1a, g_small) = _cast(x, name="x_cast", comm=_ag_job([sh["ffn1_w1"], small_parts]))
    w1a = full(g_w1a)
    P = small_params(g_small)
    wh = P["hgrn_norm_g"].shape[1]
    wm = P["mlstm_norm_g"].shape[1]
    nhm = P["ig_b"].shape[1]
    ng = 4 * nhm
    nch = T // CHUNK
    assert wh == wm and T % CHUNK == 0
    vcol, ocol, gcol = 7, 8, 2
    col0 = 5 * wh
    nz = 5 * wh + 4 * wm
    c1 = sh["w_in"].shape[0] // 4 // 16 * 16
    a1f, (g_w3a, g_wp1) = _mm(xb, w1a, "nt", out_dtype=BF16, name="ffn1_up_a",
                              comm=_ag_job([sh["ffn1_w3"], sh["w_in"][:c1]]))
    w3a = full(g_w3a)
    (a1, b1, h1, h1t), (g_w2a, g_wp2) = _ffn_up(xb, None, w3a, a=a1f, name="ffn1_up_b",
                                                comm=_ag_job([sh["ffn1_w2"], sh["w_in"][c1:2 * c1]]))
    w2a = full(g_w2a)
    r1, (g_wp3,) = _mm(h1, w2a, add=x, scale=0.5, add_scale=DN_ALPHA, tk=w2a.shape[0] // 4,
                       name="ffn1_down", comm=_ag_job([sh["w_in"][2 * c1:]]))
    w_int = full(jnp.concatenate([g_wp1, g_wp2, g_wp3], axis=1))
    w_wgt = jnp.zeros((LANES, D), BF16).at[:ng].set(w_int[nz:])
    x1, x1b = _ln_fwd(r1, P["ln1_g"], P["ln1_b"], name="ln1")
    zm, (g_wout, g_w3b) = _mm(x1b, w_int, "nt", rows=nz, name="zm",
                              comm=_ag_job([sh["w_out"], sh["ffn2_w3"]]))
    w_wout = full(g_wout)
    zg = _mm(x1b, w_wgt, "nt", name="zg")
    gr = zg[:, :ng].reshape(nch, CHUNK, ng).transpose(0, 2, 1)
    bias = jnp.concatenate([P["ig_b"].reshape(-1), P["fg_b"].reshape(-1)])
    br = jnp.zeros((1, LANES), F32).at[0, :ng].set(bias)
    bcl = bias.reshape(ng, 1)
    lbf, lbb = P["lb"][0:1], P["lb"][1:2]
    ((o_fw, s_fw), (o_bw, s_bw)), (g_w1b,) = _run_specs(
        [_hgrn_fwd(zm, lbf, wh=wh, reverse=False), _hgrn_fwd(zm, lbb, wh=wh, reverse=True)],
        name="hgrn_fwd", comm=_ag_job([sh["ffn2_w1"]]))
    qk = _conv_fwd(zm, P["conv_w"], P["conv_b"], col0=col0, name="conv")
    mkw = dict(wm=wm, nhm=nhm, vcol=vcol)
    ((h_fw, *st_fw),) = _run_specs([_mlstm_fwd(qk, zm, zg, gr, br, bcl, reverse=False, **mkw)],
                                   name="mlstm_fw")
    ((h_bw, *st_bw),) = _run_specs([_mlstm_fwd(qk, zm, zg, gr, br, bcl, reverse=True, **mkw)],
                                   name="mlstm_bw")
    w1b, w3b = full(g_w1b), full(g_w3b)
    y, yt = _mix_fwd(o_fw, o_bw, h_fw, h_bw, zm, P["hgrn_norm_g"], P["mlstm_norm_g"],
                     nhm=nhm, gcol=gcol, ocol=ocol, name="mix")
    r2 = _mm(y, w_wout, add=x1, add_scale=DN_ALPHA, name="r2")
    x2, x2b = _ln_fwd(r2, P["ln2_g"], P["ln2_b"], name="ln2")
    (a2, b2, h2, h2t), (g_w2b,) = _ffn_up(x2b, w1b, w3b, name="ffn2_up", comm=_ag_job([sh["ffn2_w2"]]))
    w2b = full(g_w2b)
    r3 = _mm(h2, w2b, add=x2, scale=0.5, add_scale=DN_ALPHA, tk=w2b.shape[0] // 4, name="ffn2_down")

    G, land = {}, {}
    dr3, dr3b, G["ln3_g"], G["ln3_b"], loss = _ln_bwd(
        None, r3, P["ln3_g"], b=P["ln3_b"], target=target, name="ln3_bwd")
    da2, db2, da2t, db2t = _ffn_bwd_act(dr3b, w2b, a2, b2, name="ffn2_dact")
    gw1b, gw3b = dw(da2t, x2b, "ffn2_dw1"), dw(db2t, x2b, "ffn2_dw3")
    gw2b = dw(h2t, dr3b, "ffn2_dw2", scale=0.5)
    dx2, l1 = _ffn_dx(da2, db2, w1b, w3b, dr3, name="ffn2_dx", comm=_pair_job([gw1b, gw3b, gw2b]))
    p_w1b, p_w3b, p_w2b = [pair_add(k, g, l) for k, g, l in
                           zip(("ffn2_w1", "ffn2_w3", "ffn2_w2"), (gw1b, gw3b, gw2b), l1)]
    dr2, dr2b, G["ln2_g"], G["ln2_b"] = _ln_bwd(dx2, r2, P["ln2_g"], name="ln2_bwd")
    dy = _mm(dr2b, w_wout, "nt", name="dy")
    gwout = shards(_mm(yt, dr2b, out_dtype=BF16, name="dwout"))
    (do, dh, dhg, dmo, G["hgrn_norm_g"], G["mlstm_norm_g"]), (l1,) = _mix_bwd(
        dy, o_fw, o_bw, h_fw, h_bw, zm, P["hgrn_norm_g"], P["mlstm_norm_g"],
        nhm=nhm, gcol=gcol, ocol=ocol, name="mix_bwd", comm=_pair_job([gwout]))
    p_wout = pair_add("w_out", gwout, l1)
    ((dqk_f, dv_f, dgr_f, dgs_f),), (land["ffn2_w1"],) = _run_specs(
        [_mlstm_bwd(qk, zm, zg, gr, br, bcl, dh, st_fw, reverse=False, **mkw)],
        name="mlstm_fw_bwd", comm=_chips_job([p_w1b]))
    ((dqk_b, dv_b, dgr_b, dgs_b),), (land["ffn2_w3"],) = _run_specs(
        [_mlstm_bwd(qk, zm, zg, gr, br, bcl, dh, st_bw, reverse=True, **mkw)],
        name="mlstm_bw_bwd", comm=_chips_job([p_w3b]))
    dmqk, G["conv_w"], G["conv_b"] = _conv_bwd(dqk_f, dqk_b, zm, P["conv_w"], P["conv_b"], col0=col0,
                                               name="conv_bwd")
    ((dhq_f, dhi_f, dhf_fw, dlb_f), (dhq_b, dhi_b, dhf_bw, dlb_b)), (land["ffn2_w2"], land["w_out"]) = \
        _run_specs([_hgrn_bwd(zm, lbf, do, s_fw, wh=wh, reverse=False),
                    _hgrn_bwd(zm, lbb, do, s_bw, wh=wh, reverse=True)],
                   name="hgrn_bwd", comm=_chips_job([p_w2b, p_wout]))
    G["lb"] = jnp.concatenate([dlb_f, dlb_b], axis=0)
    G["ig_b"] = jnp.stack([dgs_f[:nhm, 0], dgs_b[:nhm, 0]])
    G["fg_b"] = jnp.stack([dgs_f[nhm:, 0], dgs_b[nhm:, 0]])
    dzm, dzmt = _dz_pack([(dhq_f, dhq_b), (dhi_f, dhi_b), dhg, dhf_fw, dhf_bw, dmqk, (dv_f, dv_b), dmo],
                         name="dz_pack")
    dgr = jnp.concatenate([dgr_f[:, :nhm], dgr_b[:, :nhm], dgr_f[:, nhm:], dgr_b[:, nhm:]], axis=1)
    dzgt = jnp.zeros((LANES, T), F32).at[:ng].set(dgr.transpose(1, 0, 2).reshape(ng, T)).astype(BF16)
    dzg = jnp.zeros((T, LANES), F32).at[:, :ng].set(dgr.transpose(0, 2, 1).reshape(T, ng)).astype(BF16)
    gwin = shards(jnp.concatenate([_mm(dzmt, x1b, out_dtype=BF16, tk=T, name="dwin"),
                                   _mm(dzgt, x1b, out_dtype=BF16, tk=T, name="dwg")[:ng]], axis=0))
    t, (l1,) = _mm(dzg, w_wgt, add=dr2, add_scale=DN_ALPHA, name="dx1_g", comm=_pair_job([gwin]))
    p_win = pair_add("w_in", gwin, l1)
    dx1, (land["w_in"],) = _mm(dzm, w_int, rows=nz, add=t, tk=1536, name="dx1", comm=_chips_job([p_win]))
    dr1, dr1b, G["ln1_g"], G["ln1_b"] = _ln_bwd(dx1, r1, P["ln1_g"], name="ln1_bwd")
    gw2a = dw(h1t, dr1b, "ffn1_dw2", scale=0.5)
    (da1, db1, da1t, db1t), (l1,) = _ffn_bwd_act(dr1b, w2a, a1, b1, name="ffn1_dact",
                                                 comm=_pair_job([gw2a]))
    p_w2a = pair_add("ffn1_w2", gw2a, l1)
    gw1a, (land["ffn1_w2"],) = _mm(da1t, xb, out_dtype=BF16, tm=704, tk=T, name="ffn1_dw1",
                                   comm=_chips_job([p_w2a]))
    gw1a = shards(gw1a)
    gw3a, (l1,) = _mm(db1t, xb, out_dtype=BF16, tm=704, tk=T, name="ffn1_dw3", comm=_pair_job([gw1a]))
    gw3a = shards(gw3a)
    p_w1a = pair_add("ffn1_w1", gw1a, l1)
    (l1,) = _comm_call(_pair_job([gw3a]), "rs_pair_ffn1_w3")
    p_w3a = pair_add("ffn1_w3", gw3a, l1)
    gx, (land["ffn1_w1"], land["ffn1_w3"]) = _ffn_dx(
        da1, db1, w1a, w3a, dr1, name="ffn1_dx", comm=_chips_job([p_w1a, p_w3a]))
    return loss, gx, land, G


ANY = pl.BlockSpec(memory_space=pl.ANY)


def _place():
    x, y, c = lax.axis_index("x"), lax.axis_index("y"), lax.axis_index("c")
    chips = [(1 - x, y), (x, 1 - y), (1 - x, 1 - y)]
    return x, y, c, chips


def _comm_call(job, name):
    n_in, n_out = len(job.inputs), len(job.out_shape)

    def body(*refs):
        ins, outs, sems = refs[:n_in], refs[n_in:n_in + n_out], refs[n_in + n_out:]
        job.start(ins, outs, sems)
        job.middle(ins, outs, sems)
        job.finish(ins, outs, sems)

    return pl.pallas_call(body, name=name, out_shape=job.out_shape, in_specs=[ANY] * n_in,
                          out_specs=[ANY] * n_out, scratch_shapes=job.sems)(*job.inputs)


def _ag_job(parts):
    n = len(parts)
    halves = [p.shape[0] // 2 // 16 * 16 or p.shape[0] // 2 // 8 * 8 or p.shape[0] for p in parts]
    cut = [h < p.shape[0] for h, p in zip(halves, parts)]

    def helpers(ins, outs, sems):
        send_sems, recv_sems, local_sems = sems
        x, y, c, _ = _place()
        me, sibling = (x, y, c), (x, y, 1 - c)
        xn, yn, dg = (1 - x, y), (x, 1 - y), (1 - x, 1 - y)

        def slot(a, block, part=None):
            bx, by, bc = block
            ref = outs[a].at[4 * bx + 2 * by + bc]
            if part == 0:
                return ref.at[pl.ds(0, halves[a])]
            if part == 1:
                return ref.at[pl.ds(halves[a], parts[a].shape[0] - halves[a])]
            return ref

        def copy(a, k, block, to, src=None, part=None):
            dst = slot(a, block, part)
            return pltpu.make_async_remote_copy(
                src_ref=dst if src is None else src, dst_ref=dst,
                send_sem=send_sems.at[a, k], recv_sem=recv_sems.at[a, k],
                device_id=to, device_id_type=MESH)

        local = [pltpu.make_async_copy(ins[a], slot(a, me), local_sems.at[a]) for a in range(n)]
        return copy, local, me, sibling, xn, yn, dg, c

    def start(ins, outs, sems):
        copy, local, me, sibling, xn, yn, dg, c = helpers(ins, outs, sems)
        for cp in local:
            cp.start()
        for a in range(n):
            copy(a, 0, me, sibling, src=ins[a]).start()
            copy(a, 1, me, (*xn, c), src=ins[a]).start()
            copy(a, 2, me, (*yn, c), src=ins[a]).start()

    def middle(ins, outs, sems):
        copy, local, me, sibling, xn, yn, dg, c = helpers(ins, outs, sems)
        for a in range(n):
            copy(a, 1, (*xn, c), me).wait_recv()
            copy(a, 3, (*xn, c), (*yn, c), part=0).start()
            copy(a, 5, (*xn, c), sibling).start()
        for a in range(n):
            copy(a, 2, (*yn, c), me).wait_recv()
            if cut[a]:
                copy(a, 4, (*yn, c), (*xn, c), part=1).start()
            copy(a, 6, (*yn, c), sibling).start()

    def finish(ins, outs, sems):
        copy, local, me, sibling, xn, yn, dg, c = helpers(ins, outs, sems)
        for a in range(n):
            copy(a, 3, (*dg, c), me, part=0).wait_recv()
            if cut[a]:
                copy(a, 4, (*dg, c), me, part=1).wait_recv()
            copy(a, 7, (*dg, c), sibling).start()
        for a in range(n):
            copy(a, 0, sibling, me).wait_recv()
            copy(a, 5, (*xn, 1 - c), me).wait_recv()
            copy(a, 6, (*yn, 1 - c), me).wait_recv()
            copy(a, 7, (*dg, 1 - c), me).wait_recv()
        for a in range(n):
            copy(a, 0, me, sibling, src=ins[a]).wait_send()
            copy(a, 1, me, (*xn, c), src=ins[a]).wait_send()
            copy(a, 2, me, (*yn, c), src=ins[a]).wait_send()
            copy(a, 3, (*xn, c), (*yn, c), part=0).wait_send()
            if cut[a]:
                copy(a, 4, (*yn, c), (*xn, c), part=1).wait_send()
            for k, blk in ((5, xn), (6, yn), (7, dg)):
                copy(a, k, (*blk, c), sibling).wait_send()
        for cp in local:
            cp.wait()

    return _Job(parts, [jax.ShapeDtypeStruct((N_DEV,) + p.shape, p.dtype) for p in parts],
                [pltpu.SemaphoreType.DMA((n, 8)), pltpu.SemaphoreType.DMA((n, 8)),
                 pltpu.SemaphoreType.DMA((n,))], start, finish, middle)


def _pair_job(gs):
    n = len(gs)

    def copies(g_refs, land_refs, sems):
        send_sems, recv_sems = sems
        x, y, c, _ = _place()
        return [pltpu.make_async_remote_copy(
            src_ref=g_refs[a].at[2 * k + 1 - c], dst_ref=land_refs[a].at[k],
            send_sem=send_sems.at[a, k], recv_sem=recv_sems.at[a, k],
            device_id=(x, y, 1 - c), device_id_type=MESH) for a in range(n) for k in range(4)]

    def start(ins, outs, sems):
        for cp in copies(ins, outs, sems):
            cp.start()

    def finish(ins, outs, sems):
        for cp in copies(ins, outs, sems):
            cp.wait()

    return _Job(gs, [jax.ShapeDtypeStruct((4,) + g.shape[1:], g.dtype) for g in gs],
                [pltpu.SemaphoreType.DMA((n, 4)), pltpu.SemaphoreType.DMA((n, 4))], start, finish)


def _chips_job(ps):
    n = len(ps)

    def copies(p_refs, land_refs, sems):
        send_sems, recv_sems, local_sems = sems
        x, y, c, chips = _place()
        mine = 2 * x + y
        owns = [pltpu.make_async_copy(p_refs[a].at[mine], land_refs[a].at[mine], local_sems.at[a])
                for a in range(n)]
        sends = [pltpu.make_async_remote_copy(
            src_ref=p_refs[a].at[2 * chip[0] + chip[1]], dst_ref=land_refs[a].at[mine],
            send_sem=send_sems.at[a, j], recv_sem=recv_sems.at[a, j],
            device_id=(*chip, c), device_id_type=MESH) for a in range(n) for j, chip in enumerate(chips)]
        recvs = [pltpu.make_async_remote_copy(
            src_ref=p_refs[a].at[mine], dst_ref=land_refs[a].at[2 * chip[0] + chip[1]],
            send_sem=send_sems.at[a, j], recv_sem=recv_sems.at[a, j],
            device_id=(*chip, c), device_id_type=MESH) for a in range(n) for j, chip in enumerate(chips)]
        return owns, sends, recvs

    def start(ins, outs, sems):
        owns, sends, _ = copies(ins, outs, sems)
        for cp in owns + sends:
            cp.start()

    def finish(ins, outs, sems):
        owns, sends, recvs = copies(ins, outs, sems)
        for cp in recvs:
            cp.wait_recv()
        for cp in sends:
            cp.wait_send()
        for cp in owns:
            cp.wait()

    return _Job(ps, [jax.ShapeDtypeStruct(p.shape, p.dtype) for p in ps],
                [pltpu.SemaphoreType.DMA((n, 3)), pltpu.SemaphoreType.DMA((n, 3)),
                 pltpu.SemaphoreType.DMA((n,))], start, finish)


def _all_reduce_small(buf, name):
    rows = buf.shape[0]

    def body(b_ref, o_ref, slots, send_sems, recv_sems):
        x, y, c, _ = _place()
        me = 4 * x + 2 * y + c
        slots[me] = b_ref[...]
        cps = []
        for k in range(1, N_DEV):
            fx, fy, fc = (k >> 2) & 1, (k >> 1) & 1, k & 1
            peer = (x ^ fx, y ^ fy, c ^ fc)
            cps.append(pltpu.make_async_remote_copy(
                src_ref=b_ref, dst_ref=slots.at[me],
                send_sem=send_sems.at[k - 1], recv_sem=recv_sems.at[k - 1],
                device_id=peer, device_id_type=MESH))
        for cp in cps:
            cp.start()
        for k in range(1, N_DEV):
            fx, fy, fc = (k >> 2) & 1, (k >> 1) & 1, k & 1
            src = 4 * (x ^ fx) + 2 * (y ^ fy) + (c ^ fc)
            pltpu.make_async_remote_copy(
                src_ref=b_ref, dst_ref=slots.at[src],
                send_sem=send_sems.at[k - 1], recv_sem=recv_sems.at[k - 1],
                device_id=(x ^ fx, y ^ fy, c ^ fc), device_id_type=MESH).wait_recv()
        for cp in cps:
            cp.wait_send()
        acc = slots[0]
        for k in range(1, N_DEV):
            acc = acc + slots[k]
        o_ref[...] = acc

    vm = pl.BlockSpec(memory_space=pltpu.VMEM)
    return pl.pallas_call(
        body, name=name, out_shape=jax.ShapeDtypeStruct(buf.shape, F32),
        in_specs=[vm], out_specs=vm,
        scratch_shapes=[pltpu.VMEM((N_DEV, rows, LANES), F32),
                        pltpu.SemaphoreType.DMA((N_DEV - 1,)), pltpu.SemaphoreType.DMA((N_DEV - 1,))],
    )(buf)


def _row_tile(rows, pref=2048):
    for t in range(min(pref, rows) - min(pref, rows) % 8, 0, -8):
        if rows % t == 0:
            return t
    return rows


def _tile2(rows, cols, nbuf):
    budget = VMEM_LIMIT // 2 // (2 * nbuf * 4)
    for t in range(min(512, rows) // 16 * 16, 0, -16):
        if rows % t == 0 and t * cols <= budget:
            return t, cols
    return rows, _pick(cols, max(LANES, budget // rows // LANES * LANES))


def _pair_add(g, land, core, name):
    _, r, c = g.shape
    tr, tc = _tile2(r, c, 3)

    def body(core_ref, g_ref, l_ref, o_ref):
        o_ref[...] = (g_ref[...].astype(F32) + l_ref[...].astype(F32)).astype(o_ref.dtype)

    blk = pl.BlockSpec((1, tr, tc), lambda k, i, j, cr: (k, i, j))
    return pl.pallas_call(
        body, name=name, out_shape=jax.ShapeDtypeStruct(land.shape, land.dtype),
        grid_spec=pltpu.PrefetchScalarGridSpec(
            num_scalar_prefetch=1, grid=(4, r // tr, c // tc),
            in_specs=[pl.BlockSpec((1, tr, tc), lambda k, i, j, cr: (2 * k + cr[0], i, j)), blk],
            out_specs=blk),
        compiler_params=_params("parallel", "parallel", "parallel"),
    )(core, g, land)


def _adamw(w, parts, m, v, name):
    shp = w.shape
    cols = shp[-1]
    w2, m2, v2 = (t.reshape(-1, cols) for t in (w, m, v))
    rows = w2.shape[0]
    n = parts.shape[0]
    assert parts.shape[1:] == (rows, cols), (parts.shape, shp)
    tr, tc = _tile2(rows, cols, n + 7)
    c1 = 1.0 / (1.0 - ADAM_B1 ** ADAM_STEP)
    c2 = 1.0 / (1.0 - ADAM_B2 ** ADAM_STEP)

    def body(*refs):
        p_refs = refs[:n]
        w_ref, m_ref, v_ref, g_ref, d_ref, nm_ref, nv_ref = refs[n:]
        gv = p_refs[0][0].astype(F32)
        for p_ref in p_refs[1:]:
            gv = gv + p_ref[0].astype(F32)
        nm = ADAM_B1 * m_ref[...] + (1.0 - ADAM_B1) * gv
        nv = ADAM_B2 * v_ref[...] + (1.0 - ADAM_B2) * (gv * gv)
        g_ref[...] = gv
        nm_ref[...] = nm
        nv_ref[...] = nv
        d_ref[...] = -ADAM_LR * ((nm * c1) / (jnp.sqrt(nv * c2) + ADAM_EPS) + ADAM_WD * w_ref[...])

    blk = pl.BlockSpec((tr, tc), lambda i, j: (i, j))
    slab = lambda k: pl.BlockSpec((1, tr, tc), lambda i, j: (k, i, j))
    out = jax.ShapeDtypeStruct((rows, cols), F32)
    outs = pl.pallas_call(
        body, name=name, out_shape=[out] * 4, grid=(rows // tr, cols // tc),
        in_specs=[slab(k) for k in range(n)] + [blk] * 3, out_specs=[blk] * 4,
        compiler_params=_params("parallel", "parallel"),
    )(*([parts] * n), w2, m2, v2)
    return tuple(o.reshape(shp) for o in outs)


def _pack_rows(arrs):
    rows = []
    for a in arrs:
        flat = a.reshape(-1).astype(F32)
        pad = (-flat.shape[0]) % LANES
        if pad:
            flat = jnp.concatenate([flat, jnp.zeros((pad,), F32)])
        rows.append(flat.reshape(-1, LANES))
    out = jnp.concatenate(rows, axis=0)
    pad = (-out.shape[0]) % 8
    if pad:
        out = jnp.concatenate([out, jnp.zeros((pad, LANES), F32)], axis=0)
    return out


def _unpack_rows(buf, shapes):
    outs, r = [], 0
    for s in shapes:
        n = math.prod(s)
        nr = -(-n // LANES)
        outs.append(buf[r:r + nr].reshape(-1)[:n].reshape(s))
        r += nr
    return outs


BIG = ("ffn1_w1", "ffn1_w3", "ffn1_w2", "w_in", "w_out", "ffn2_w1", "ffn2_w3", "ffn2_w2")
ROW_SHARDED = ("ffn1_w2", "w_out", "ffn2_w2")
SMALL = ("ln1_g", "ln1_b", "hgrn_lb", "hgrn_norm_g", "mlstm_conv_w", "mlstm_conv_b", "mlstm_ig_b",
         "mlstm_fg_b", "mlstm_norm_g", "ln2_g", "ln2_b", "ln3_g", "ln3_b")
WEIGHTS = ("ffn1_w1", "ffn1_w3", "ffn1_w2", "ln1_g", "ln1_b", "w_in", "hgrn_lb", "hgrn_norm_g",
           "mlstm_conv_w", "mlstm_conv_b", "mlstm_ig_b", "mlstm_fg_b", "mlstm_norm_g", "w_out",
           "ln2_g", "ln2_b", "ffn2_w1", "ffn2_w3", "ffn2_w2", "ln3_g", "ln3_b")


def kernel(x, ffn1_w1, ffn1_w3, ffn1_w2, ln1_g, ln1_b, w_in, hgrn_lb, hgrn_norm_g, mlstm_conv_w, mlstm_conv_b, mlstm_ig_b, mlstm_fg_b, mlstm_norm_g, w_out, ln2_g, ln2_b, ffn2_w1, ffn2_w3, ffn2_w2, ln3_g, ln3_b, loss_target, m_ffn1_w1, m_ffn1_w3, m_ffn1_w2, m_ln1_g, m_ln1_b, m_w_in, m_hgrn_lb, m_hgrn_norm_g, m_mlstm_conv_w, m_mlstm_conv_b, m_mlstm_ig_b, m_mlstm_fg_b, m_mlstm_norm_g, m_w_out, m_ln2_g, m_ln2_b, m_ffn2_w1, m_ffn2_w3, m_ffn2_w2, m_ln3_g, m_ln3_b, v_ffn1_w1, v_ffn1_w3, v_ffn1_w2, v_ln1_g, v_ln1_b, v_w_in, v_hgrn_lb, v_hgrn_norm_g, v_mlstm_conv_w, v_mlstm_conv_b, v_mlstm_ig_b, v_mlstm_fg_b, v_mlstm_norm_g, v_w_out, v_ln2_g, v_ln2_b, v_ffn2_w1, v_ffn2_w3, v_ffn2_w2, v_ln3_g, v_ln3_b):
    args = (ffn1_w1, ffn1_w3, ffn1_w2, ln1_g, ln1_b, w_in, hgrn_lb, hgrn_norm_g, mlstm_conv_w,
            mlstm_conv_b, mlstm_ig_b, mlstm_fg_b, mlstm_norm_g, w_out, ln2_g, ln2_b, ffn2_w1, ffn2_w3,
            ffn2_w2, ln3_g, ln3_b)
    ms = (m_ffn1_w1, m_ffn1_w3, m_ffn1_w2, m_ln1_g, m_ln1_b, m_w_in, m_hgrn_lb, m_hgrn_norm_g,
          m_mlstm_conv_w, m_mlstm_conv_b, m_mlstm_ig_b, m_mlstm_fg_b, m_mlstm_norm_g, m_w_out, m_ln2_g,
          m_ln2_b, m_ffn2_w1, m_ffn2_w3, m_ffn2_w2, m_ln3_g, m_ln3_b)
    vs = (v_ffn1_w1, v_ffn1_w3, v_ffn1_w2, v_ln1_g, v_ln1_b, v_w_in, v_hgrn_lb, v_hgrn_norm_g,
          v_mlstm_conv_w, v_mlstm_conv_b, v_mlstm_ig_b, v_mlstm_fg_b, v_mlstm_norm_g, v_w_out, v_ln2_g,
          v_ln2_b, v_ffn2_w1, v_ffn2_w3, v_ffn2_w2, v_ln3_g, v_ln3_b)
    w = dict(zip(WEIGHTS, args))
    m = dict(zip(WEIGHTS, ms))
    v = dict(zip(WEIGHTS, vs))
    core = lax.axis_index("c")
    dev = 4 * lax.axis_index("x") + 2 * lax.axis_index("y") + core
    p_lb = []

    def small_params(gathered_small):
        lb_sh, cw_sh = zip(*[_unpack_rows(gathered_small[i], [hgrn_lb.shape, mlstm_conv_w.shape])
                             for i in range(N_DEV)])
        hgrn_lb_full = jnp.concatenate(lb_sh, axis=-1)
        conv_w_full = jnp.concatenate(cw_sh, axis=-1)[0]
        p_lb.append(jax.nn.softmax(hgrn_lb_full, axis=1))
        return dict(ln1_g=ln1_g, ln1_b=ln1_b, ln2_g=ln2_g, ln2_b=ln2_b, ln3_g=ln3_g, ln3_b=ln3_b,
                    lb=p_lb[0][:, 0], hgrn_norm_g=hgrn_norm_g, conv_w=conv_w_full, conv_b=mlstm_conv_b,
                    ig_b=mlstm_ig_b[0], fg_b=mlstm_fg_b[0], mlstm_norm_g=mlstm_norm_g)

    def rows_first(t, k):
        return t[0] if k in ROW_SHARDED else t[0].T

    loss_row, grad_x, land2, G = _device_step(
        x[0], loss_target[0], {k: rows_first(w[k], k).astype(BF16) for k in BIG},
        _pack_rows([hgrn_lb, mlstm_conv_w]), small_params, jnp.reshape(core, (1,)).astype(jnp.int32))
    p_lb = p_lb[0]

    dlb = G["lb"]
    g_lb = jnp.stack([dlb * p_lb[:, 0] * (1.0 - p_lb[:, 0]), -dlb * p_lb[:, 0] * p_lb[:, 1]], axis=1)
    small_full = {"ln1_g": G["ln1_g"], "ln1_b": G["ln1_b"], "hgrn_lb": g_lb, "hgrn_norm_g": G["hgrn_norm_g"],
                  "mlstm_conv_w": G["conv_w"][None], "mlstm_conv_b": G["conv_b"],
                  "mlstm_ig_b": G["ig_b"][None], "mlstm_fg_b": G["fg_b"][None],
                  "mlstm_norm_g": G["mlstm_norm_g"], "ln2_g": G["ln2_g"], "ln2_b": G["ln2_b"],
                  "ln3_g": G["ln3_g"], "ln3_b": G["ln3_b"]}
    small_list = [small_full[k] for k in SMALL] + [loss_row]
    reduced = _all_reduce_small(_pack_rows(small_list), "all_reduce_small")
    red = _unpack_rows(reduced, [a.shape for a in small_list])
    loss = red[-1][0, 0]
    small_g = {}
    for k, gk in zip(SMALL, red[:-1]):
        if k in ("hgrn_lb", "mlstm_conv_w"):
            n = w[k].shape[-1]
            gk = lax.dynamic_slice_in_dim(gk, dev * n, n, axis=gk.ndim - 1)
        small_g[k] = gk

    grads, delta, new_m, new_v = {}, {}, {}, {}
    for k in BIG:
        res = _adamw(rows_first(w[k], k), land2[k], rows_first(m[k], k), rows_first(v[k], k), "adamw_" + k)
        grads[k], delta[k], new_m[k], new_v[k] = [(r if k in ROW_SHARDED else r.T)[None] for r in res]
    sm = _adamw(_pack_rows([w[k] for k in SMALL]), _pack_rows([small_g[k] for k in SMALL])[None],
                _pack_rows([m[k] for k in SMALL]), _pack_rows([v[k] for k in SMALL]), "adamw_small")
    shapes = [w[k].shape for k in SMALL]
    for dst, buf in zip((grads, delta, new_m, new_v), sm):
        for k, val in zip(SMALL, _unpack_rows(buf, shapes)):
            dst[k] = val
    return (loss, grad_x[None], *[grads[k] for k in WEIGHTS], *[delta[k] for k in WEIGHTS],
            *[new_m[k] for k in WEIGHTS], *[new_v[k] for k in WEIGHTS])
```

```python
import math

import jax
import jax.numpy as jnp
from jax import lax
from jax.experimental import pallas as pl
from jax.experimental.pallas import tpu as pltpu

F32 = jnp.float32
BF16 = jnp.bfloat16

CHUNK = 64
HGRN_HEAD_DIM = 128
CONV_WIDTH = 5
DN_ALPHA = 2.0 ** 0.25
LN_EPS = 1e-5
NORM_EPS = 1e-6
M_INIT = -1e30
NEG = -1e30
EXP_CLAMP = 80.0
ADAM_LR = 0.001
ADAM_B1 = 0.9
ADAM_B2 = 0.999
ADAM_EPS = 1e-08
ADAM_WD = 0.01
ADAM_STEP = 10
N_DEV = 8
LANES = 128
VMEM_LIMIT = 56 * 1024 * 1024
MESH = pl.DeviceIdType.MESH


def _params(*sem):
    return pltpu.CompilerParams(dimension_semantics=sem, vmem_limit_bytes=VMEM_LIMIT)


class _Job:
    def __init__(self, inputs, out_shape, sems, start, finish, middle=None):
        self.inputs, self.out_shape, self.sems = list(inputs), list(out_shape), list(sems)
        self.start, self.finish = start, finish
        self.middle = middle or (lambda ins, outs, sems: None)


def _join(*jobs):
    def split(refs, counts):
        out, p = [], 0
        for c in counts:
            out.append(refs[p:p + c])
            p += c
        return out

    n_in = [len(j.inputs) for j in jobs]
    n_out = [len(j.out_shape) for j in jobs]
    n_sem = [len(j.sems) for j in jobs]

    def start(ins, outs, sems):
        for j, i, o, s in zip(jobs, split(ins, n_in), split(outs, n_out), split(sems, n_sem)):
            j.start(i, o, s)

    def middle(ins, outs, sems):
        for j, i, o, s in zip(jobs, split(ins, n_in), split(outs, n_out), split(sems, n_sem)):
            j.middle(i, o, s)

    def finish(ins, outs, sems):
        for j, i, o, s in zip(jobs, split(ins, n_in), split(outs, n_out), split(sems, n_sem)):
            j.finish(i, o, s)

    return _Job(sum((j.inputs for j in jobs), []), sum((j.out_shape for j in jobs), []),
                sum((j.sems for j in jobs), []), start, finish, middle)


def _pcall(body, *, name, out_shape, grid, in_specs, out_specs, sem, scratch_shapes=(), comm=None):
    single = not isinstance(out_shape, (list, tuple))
    out_shape = [out_shape] if single else list(out_shape)
    out_specs = [out_specs] if single else list(out_specs)
    in_specs, scratch_shapes = list(in_specs), list(scratch_shapes)
    if comm is None:
        call = pl.pallas_call(body, name=name, out_shape=out_shape, grid=grid, in_specs=in_specs,
                              out_specs=out_specs, scratch_shapes=scratch_shapes,
                              compiler_params=_params(*sem))

        def run(*args):
            outs = call(*args)
            return outs[0] if single else outs
        return run

    n_in, n_out, n_sc = len(in_specs), len(out_shape), len(scratch_shapes)
    c_in, c_out = len(comm.inputs), len(comm.out_shape)

    def hosted(*refs):
        ins, cins = refs[:n_in], refs[n_in:n_in + c_in]
        p = n_in + c_in
        outs, couts = refs[p:p + n_out], refs[p + n_out:p + n_out + c_out]
        p += n_out + c_out
        scratch, csems = refs[p:p + n_sc], refs[p + n_sc:]
        first = pl.program_id(0) == 0
        half = pl.program_id(0) == grid[0] // 2
        last = pl.program_id(0) == grid[0] - 1
        for d in range(1, len(grid)):
            first = first & (pl.program_id(d) == 0)
            half = half & (pl.program_id(d) == 0)
            last = last & (pl.program_id(d) == grid[d] - 1)

        @pl.when(first)
        def _():
            comm.start(cins, couts, csems)

        if grid[0] >= 2:
            @pl.when(half)
            def _():
                comm.middle(cins, couts, csems)

        body(*ins, *outs, *scratch)

        @pl.when(last)
        def _():
            if grid[0] < 2:
                comm.middle(cins, couts, csems)
            comm.finish(cins, couts, csems)

    any_spec = pl.BlockSpec(memory_space=pl.ANY)
    call = pl.pallas_call(
        hosted, name=name, out_shape=out_shape + comm.out_shape, grid=grid,
        in_specs=in_specs + [any_spec] * c_in, out_specs=out_specs + [any_spec] * c_out,
        scratch_shapes=scratch_shapes + comm.sems,
        compiler_params=_params(*(["arbitrary"] * len(grid))))

    def run(*args):
        res = call(*args, *comm.inputs)
        outs, couts = res[:n_out], list(res[n_out:])
        return (outs[0] if single else outs), couts
    return run


def _sigmoid(x):
    return 1.0 / (1.0 + jnp.exp(-x))


def _log_sigmoid(x):
    return jnp.minimum(x, 0.0) - jnp.log(1.0 + jnp.exp(-jnp.abs(x)))


def _dot(a, b, dims):
    return lax.dot_general(a.astype(BF16), b.astype(BF16), (dims, ((), ())),
                           preferred_element_type=F32)


def _dot3(a, b, dims):
    ah = a.astype(BF16)
    al = (a - ah.astype(F32)).astype(BF16)
    bh = b.astype(BF16)
    bl = (b - bh.astype(F32)).astype(BF16)
    d = (dims, ((), ()))
    out = lax.dot_general(ah, bh, d, preferred_element_type=F32)
    out = out + lax.dot_general(ah, bl, d, preferred_element_type=F32)
    return out + lax.dot_general(al, bh, d, preferred_element_type=F32)


_DIMS = {"nn": ((1,), (0,)), "nt": ((1,), (1,)), "tn": ((0,), (0,))}


def _dot_nn(a, b):
    return _dot3(a, b, _DIMS["nn"])


def _dot_nt(a, b):
    return _dot3(a, b, _DIMS["nt"])


def _dot_tn(a, b):
    return _dot3(a, b, _DIMS["tn"])


def _split3(x):
    hi = x.astype(BF16)
    r1 = x - hi.astype(F32)
    mid = r1.astype(BF16)
    lo = (r1 - mid.astype(F32)).astype(BF16)
    return hi, mid, lo


def _dot01(mask01, x, mode="nn"):
    m = mask01.astype(BF16)
    hi, mid, lo = _split3(x)
    d = (_DIMS[mode], ((), ()))
    out = lax.dot_general(m, hi, d, preferred_element_type=F32)
    out = out + lax.dot_general(m, mid, d, preferred_element_type=F32)
    return out + lax.dot_general(m, lo, d, preferred_element_type=F32)


def _matmul(a, b, *, mode="nn", out_dtype=F32, tm=512, tn=512, tk=None,
            add=None, scale=1.0, add_scale=1.0, name, comm=None, rows=None):
    if mode == "nn":
        (M, K), (K2, N) = a.shape, b.shape
        K2 = rows or K2
    elif mode == "nt":
        (M, K), (N, K2) = a.shape, b.shape
        N = rows or N
    else:
        (K, M), (K2, N) = a.shape, b.shape
    assert K == K2, (a.shape, b.shape, mode)
    tm, tn = min(tm, M), min(tn, N)
    tk = min(tk or K, K)
    assert M % tm == 0 and N % tn == 0 and K % tk == 0, (M, N, K, tm, tn, tk)
    nk = K // tk
    dims = _DIMS[mode]
    has_add = add is not None

    def body(*refs):
        if has_add:
            a_ref, b_ref, add_ref, o_ref = refs[:4]
        else:
            a_ref, b_ref, o_ref = refs[:3]
            add_ref = None
        acc_ref = refs[-1] if nk > 1 else None

        def finish(acc):
            out = acc if scale == 1.0 else acc * scale
            if has_add:
                out = out + add_ref[...].astype(F32) * add_scale
            o_ref[...] = out.astype(o_ref.dtype)

        if nk == 1:
            finish(_dot(a_ref[...], b_ref[...], dims))
        else:
            k = pl.program_id(2)

            @pl.when(k == 0)
            def _():
                acc_ref[...] = jnp.zeros_like(acc_ref)

            acc_ref[...] += _dot(a_ref[...], b_ref[...], dims)

            @pl.when(k == nk - 1)
            def _():
                finish(acc_ref[...])

    if mode == "tn":
        a_spec = pl.BlockSpec((tk, tm), lambda i, j, k: (k, i))
    else:
        a_spec = pl.BlockSpec((tm, tk), lambda i, j, k: (i, k))
    if mode == "nt":
        b_spec = pl.BlockSpec((tn, tk), lambda i, j, k: (j, k))
    else:
        b_spec = pl.BlockSpec((tk, tn), lambda i, j, k: (k, j))
    o_spec = pl.BlockSpec((tm, tn), lambda i, j, k: (i, j))
    in_specs = [a_spec, b_spec] + ([o_spec] if has_add else [])
    args = (a, b) + ((add,) if has_add else ())
    return _pcall(
        body, name=name, comm=comm,
        out_shape=jax.ShapeDtypeStruct((M, N), out_dtype),
        grid=(M // tm, N // tn, nk),
        in_specs=in_specs, out_specs=o_spec,
        scratch_shapes=[pltpu.VMEM((tm, tn), F32)] if nk > 1 else [],
        sem=("parallel", "parallel", "arbitrary"),
    )(*args)


def _ln_stats(x):
    mu = jnp.mean(x, axis=-1, keepdims=True)
    xc = x - mu
    var = jnp.mean(xc * xc, axis=-1, keepdims=True)
    rstd = lax.rsqrt(var + LN_EPS)
    return xc * rstd, rstd


def _ln_fwd(r, g, b, *, tm=256, name):
    T, D = r.shape
    tm = min(tm, T)

    def body(r_ref, g_ref, b_ref, y_ref, yb_ref):
        xhat, _ = _ln_stats(r_ref[...])
        y = xhat * g_ref[...] + b_ref[...]
        y_ref[...] = y
        yb_ref[...] = y.astype(BF16)

    row = pl.BlockSpec((tm, D), lambda i: (i, 0))
    vec = pl.BlockSpec((1, D), lambda i: (0, 0))
    return pl.pallas_call(
        body, name=name,
        out_shape=[jax.ShapeDtypeStruct((T, D), F32), jax.ShapeDtypeStruct((T, D), BF16)],
        grid=(T // tm,), in_specs=[row, vec, vec], out_specs=[row, row],
        compiler_params=_params("parallel"),
    )(r, g, b)


def _cast(x, *, tm=512, name, comm=None):
    T, D = x.shape
    tm = min(tm, T)

    def body(x_ref, xb_ref):
        xb_ref[...] = x_ref[...].astype(BF16)

    row = pl.BlockSpec((tm, D), lambda i: (i, 0))
    return _pcall(
        body, name=name, comm=comm, out_shape=jax.ShapeDtypeStruct((T, D), BF16),
        grid=(T // tm,), in_specs=[row], out_specs=row, sem=("parallel",),
    )(x)


def _ln_bwd(dy, r, g, *, tm=256, name, b=None, target=None):
    T, D = r.shape
    tm = min(tm, T)
    with_loss = target is not None

    def body(*refs):
        if with_loss:
            r_ref, g_ref, b_ref, t_ref, dr_ref, drb_ref, dg_ref, db_ref, loss_ref = refs
        else:
            dy_ref, r_ref, g_ref, dr_ref, drb_ref, dg_ref, db_ref = refs
        i = pl.program_id(0)
        xhat, rstd = _ln_stats(r_ref[...])
        gg = g_ref[...]
        if with_loss:
            err = xhat * gg + b_ref[...] - t_ref[...]
            dyv = err * (1.0 / D)
            part = jnp.sum(jnp.sum(err * err, axis=1, keepdims=True), axis=0, keepdims=True)
            part = jnp.broadcast_to(part * (0.5 / D), (1, LANES))
        else:
            dyv = dy_ref[...]
        dxh = dyv * gg
        m1 = jnp.mean(dxh, axis=-1, keepdims=True)
        m2 = jnp.mean(dxh * xhat, axis=-1, keepdims=True)
        dr = rstd * (dxh - m1 - xhat * m2)
        dr_ref[...] = dr
        drb_ref[...] = dr.astype(BF16)
        dgp = jnp.sum(dyv * xhat, axis=0, keepdims=True)
        dbp = jnp.sum(dyv, axis=0, keepdims=True)

        @pl.when(i == 0)
        def _():
            dg_ref[...] = dgp
            db_ref[...] = dbp
            if with_loss:
                loss_ref[...] = part

        @pl.when(i > 0)
        def _():
            dg_ref[...] += dgp
            db_ref[...] += dbp
            if with_loss:
                loss_ref[...] += part

    row = pl.BlockSpec((tm, D), lambda i: (i, 0))
    vec = pl.BlockSpec((1, D), lambda i: (0, 0))
    out_shape = [jax.ShapeDtypeStruct((T, D), F32), jax.ShapeDtypeStruct((T, D), BF16),
                 jax.ShapeDtypeStruct((1, D), F32), jax.ShapeDtypeStruct((1, D), F32)]
    out_specs = [row, row, vec, vec]
    if with_loss:
        in_specs, args = [row, vec, vec, row], (r, g, b, target)
        out_shape.append(jax.ShapeDtypeStruct((1, LANES), F32))
        out_specs.append(pl.BlockSpec((1, LANES), lambda i: (0, 0)))
    else:
        in_specs, args = [row, row, vec], (dy, r, g)
    return pl.pallas_call(
        body, name=name, out_shape=out_shape, grid=(T // tm,),
        in_specs=in_specs, out_specs=out_specs,
        compiler_params=_params("arbitrary"),
    )(*args)


def _ffn_up(xb, w1t, w3t, *, a=None, tm=1024, tf=512, name, comm=None):
    F, D = w3t.shape
    T = xb.shape[0]
    tm, tf = _pick(T, tm), _pick(F, tf)
    given = a is not None

    def body(x_ref, w1_ref, w3_ref, a_ref, b_ref, h_ref, ht_ref):
        xv = x_ref[...]
        av = w1_ref[...].astype(F32) if given else _dot(xv, w1_ref[...], _DIMS["nt"])
        b = _dot(xv, w3_ref[...], _DIMS["nt"])
        a_ref[...] = av.astype(BF16)
        b_ref[...] = b.astype(BF16)
        h = av * _sigmoid(av) * b
        h_ref[...] = h.astype(BF16)
        ht_ref[...] = h.T.astype(BF16)

    ij = (lambda p, q: (p, q)) if given else (lambda p, q: (q, p))
    wsp = pl.BlockSpec((tf, D), lambda p, q: (ij(p, q)[1], 0))
    osp = pl.BlockSpec((tm, tf), lambda p, q: ij(p, q))
    act = jax.ShapeDtypeStruct((T, F), BF16)
    return _pcall(
        body, name=name, comm=comm, out_shape=[act, act, act, jax.ShapeDtypeStruct((F, T), BF16)],
        grid=(T // tm, F // tf) if given else (F // tf, T // tm),
        in_specs=[pl.BlockSpec((tm, D), lambda p, q: (ij(p, q)[0], 0)), osp if given else wsp, wsp],
        out_specs=[osp, osp, osp, pl.BlockSpec((tf, tm), lambda p, q: ij(p, q)[::-1])],
        sem=("parallel", "parallel"),
    )(xb, a if given else w1t, w3t)


def _ffn_bwd_act(drb, w2, a, b, *, tm=1024, tf=512, name, comm=None):
    F, D = w2.shape
    T = drb.shape[0]
    tm, tf = _pick(T, tm), _pick(F, tf)

    def body(dr_ref, w2_ref, a_ref, b_ref, da_ref, db_ref, dat_ref, dbt_ref):
        d = 0.5 * _dot(dr_ref[...], w2_ref[...], _DIMS["nt"])
        av = a_ref[...].astype(F32)
        sig = _sigmoid(av)
        da = d * b_ref[...].astype(F32) * sig * (1.0 + av * (1.0 - sig))
        db = d * av * sig
        da_ref[...] = da.astype(BF16)
        db_ref[...] = db.astype(BF16)
        dat_ref[...] = da.T.astype(BF16)
        dbt_ref[...] = db.T.astype(BF16)

    asp = pl.BlockSpec((tm, tf), lambda i, j: (i, j))
    tsp = pl.BlockSpec((tf, tm), lambda i, j: (j, i))
    out, out_t = jax.ShapeDtypeStruct((T, F), BF16), jax.ShapeDtypeStruct((F, T), BF16)
    return _pcall(
        body, name=name, comm=comm, out_shape=[out, out, out_t, out_t], grid=(T // tm, F // tf),
        in_specs=[pl.BlockSpec((tm, D), lambda i, j: (i, 0)),
                  pl.BlockSpec((tf, D), lambda i, j: (j, 0)), asp, asp],
        out_specs=[asp, asp, tsp, tsp],
        sem=("parallel", "parallel"),
    )(drb, w2, a, b)


def _dz_pack(pieces, *, tm=256, name):
    groups = [p if isinstance(p, tuple) else (p,) for p in pieces]
    T = groups[0][0].shape[0]
    widths = [g[0].shape[1] for g in groups]
    W = sum(widths)
    tm = _pick(T, tm)
    flat = [a for g in groups for a in g]
    n = len(flat)

    def body(*refs):
        o_ref, ot_ref = refs[n], refs[n + 1]
        c, p = 0, 0
        for g, w in zip(groups, widths):
            v = refs[p][...].astype(F32)
            for r in refs[p + 1:p + len(g)]:
                v = v + r[...].astype(F32)
            p += len(g)
            o_ref[:, c:c + w] = v.astype(BF16)
            ot_ref[c:c + w, :] = v.T.astype(BF16)
            c += w

    return pl.pallas_call(
        body, name=name,
        out_shape=[jax.ShapeDtypeStruct((T, W), BF16), jax.ShapeDtypeStruct((W, T), BF16)],
        grid=(T // tm,),
        in_specs=[pl.BlockSpec((tm, a.shape[1]), lambda i: (i, 0)) for a in flat],
        out_specs=[pl.BlockSpec((tm, W), lambda i: (i, 0)), pl.BlockSpec((W, tm), lambda i: (0, i))],
        compiler_params=_params("parallel"),
    )(*flat)


def _ffn_dx(da, db, w1t, w3t, dr, *, tm=1024, tn=1024, tk=None, name, comm=None):
    T, F = da.shape
    D = w1t.shape[1]
    tm, tn = _pick(T, tm), _pick(D, tn)
    tk = _pick(F, tk or F // 4)
    nk = F // tk

    def body(da_ref, db_ref, w1_ref, w3_ref, dr_ref, o_ref, acc_ref):
        k = pl.program_id(2)

        @pl.when(k == 0)
        def _():
            acc_ref[...] = jnp.zeros_like(acc_ref)

        acc_ref[...] += (_dot(da_ref[...], w1_ref[...], _DIMS["nn"])
                         + _dot(db_ref[...], w3_ref[...], _DIMS["nn"]))

        @pl.when(k == nk - 1)
        def _():
            o_ref[...] = DN_ALPHA * dr_ref[...] + acc_ref[...]

    asp = pl.BlockSpec((tm, tk), lambda i, j, k: (i, k))
    wsp = pl.BlockSpec((tk, tn), lambda i, j, k: (k, j))
    osp = pl.BlockSpec((tm, tn), lambda i, j, k: (i, j))
    return _pcall(
        body, name=name, comm=comm, out_shape=jax.ShapeDtypeStruct((T, D), F32),
        grid=(T // tm, D // tn, nk), in_specs=[asp, asp, wsp, wsp, osp], out_specs=osp,
        scratch_shapes=[pltpu.VMEM((tm, tn), F32)],
        sem=("parallel", "parallel", "arbitrary"),
    )(da, db, w1t, w3t, dr)


def _chunk_mask(reverse, transpose=False):
    row = lax.broadcasted_iota(jnp.int32, (CHUNK, CHUNK), 0)
    col = lax.broadcasted_iota(jnp.int32, (CHUNK, CHUNK), 1)
    if reverse != transpose:
        return col >= row
    return col <= row


def _run_specs(specs, *, name, comm=None):
    counts = [(len(s["in_specs"]), len(s["out_specs"]), len(s["scratch_shapes"])) for s in specs]

    def body(*refs):
        p, parts = 0, [[], [], []]
        for kind in range(3):
            for c in counts:
                parts[kind].append(refs[p:p + c[kind]])
                p += c[kind]
        live = [s["body"](*parts[0][k], *parts[1][k], *parts[2][k]) for k, s in enumerate(specs)]
        while live:
            for g in list(live):
                if next(g, StopIteration) is StopIteration:
                    live.remove(g)

    cat = lambda key: [v for s in specs for v in s[key]]
    res = _pcall(body, name=name, comm=comm, out_shape=cat("out_shape"), grid=specs[0]["grid"],
                 in_specs=cat("in_specs"), out_specs=cat("out_specs"),
                 scratch_shapes=cat("scratch_shapes"), sem=("arbitrary",))(*cat("args"))
    outs, couts = res if comm is not None else (res, None)
    split, p = [], 0
    for c in counts:
        split.append(outs[p:p + c[1]])
        p += c[1]
    return (split, couts) if comm is not None else split


def _hgrn_pre(hq, hf, lb):
    sig = _sigmoid(hf)
    f = lb + (1.0 - lb) * sig
    q = hq * _sigmoid(hq) * (HGRN_HEAD_DIM ** -0.5)
    return q, f, sig


def _hgrn_decays(f, cmf, reverse):
    bc = _dot01(cmf, jnp.log(f))
    last = 0 if reverse else CHUNK - 1
    blast = bc[last:last + 1, :]
    bref = bc[CHUNK // 2:CHUNK // 2 + 1, :]
    eq = jnp.exp(jnp.minimum(bc - bref, EXP_CLAMP))
    ek = jnp.exp(jnp.minimum(bref - bc, EXP_CLAMP))
    return bc, blast, eq, ek


def _hgrn_fwd(z, lb, *, wh, reverse):
    T = z.shape[0]
    nch = T // CHUNK
    nh = wh // HGRN_HEAD_DIM
    hd = HGRN_HEAD_DIM

    def ci(i):
        return nch - 1 - i if reverse else i

    def body(hq_ref, hi_ref, hf_ref, lb_ref, o_ref, st_ref, s_ref):
        @pl.when(pl.program_id(0) == 0)
        def _():
            s_ref[...] = jnp.zeros_like(s_ref)

        cm = _chunk_mask(reverse)
        cmf = cm.astype(F32)
        H = range(nh)
        sls = [slice(h * hd, (h + 1) * hd) for h in H]
        q, f, _ = _hgrn_pre(hq_ref[...], hf_ref[...], lb_ref[...])
        v = hi_ref[...]
        k = 1.0 - f
        yield
        bc, blast, eq, ek = _hgrn_decays(f, cmf, reverse)
        yield
        qh, kh, qe = q * eq, k * ek, q * jnp.exp(bc)
        k2, eblast = k * jnp.exp(blast - bc), jnp.exp(blast)
        st = [s_ref[h] for h in H]
        yield
        att = [jnp.where(cm, _dot_nt(qh[:, sl], kh[:, sl]), 0.0) for sl in sls]
        yield
        inter = [_dot(qe[:, sls[h]], st[h], _DIMS["nt"]) for h in H]
        yield
        out = [inter[h] + _dot(att[h], v[:, sls[h]], _DIMS["nn"]) for h in H]
        yield
        snew = [eblast[:, sls[h]] * st[h] + _dot_tn(v[:, sls[h]], k2[:, sls[h]]) for h in H]
        yield
        for h in H:
            st_ref[0, h] = st[h]
            o_ref[:, sls[h]] = out[h]
            s_ref[h] = snew[h]

    blk = lambda c: pl.BlockSpec((CHUNK, wh), lambda i: (ci(i), c))
    return dict(
        body=body, grid=(nch,), args=[z, z, z, lb],
        out_shape=[jax.ShapeDtypeStruct((T, wh), F32),
                   jax.ShapeDtypeStruct((nch, nh, hd, hd), F32)],
        in_specs=[blk(0), blk(1), blk(3 + int(reverse)),
                  pl.BlockSpec((1, wh), lambda i: (0, 0))],
        out_specs=[blk(0), pl.BlockSpec((1, nh, hd, hd), lambda i: (ci(i), 0, 0, 0))],
        scratch_shapes=[pltpu.VMEM((nh, hd, hd), F32)])


def _hgrn_bwd(z, lb, do, states, *, wh, reverse):
    T = z.shape[0]
    nch = T // CHUNK
    nh = wh // HGRN_HEAD_DIM
    hd = HGRN_HEAD_DIM

    def ci(i):
        return i if reverse else nch - 1 - i

    def body(hq_ref, hi_ref, hf_ref, lb_ref, do_ref, st_ref,
             dhq_ref, dhi_ref, dhf_ref, dlb_ref, ds_ref, gs_ref):
        @pl.when(pl.program_id(0) == 0)
        def _():
            ds_ref[...] = jnp.zeros_like(ds_ref)
            gs_ref[...] = jnp.zeros_like(gs_ref)
            dlb_ref[...] = jnp.zeros_like(dlb_ref)

        cm = _chunk_mask(reverse)
        cmf = cm.astype(F32)
        cmtf = _chunk_mask(reverse, transpose=True).astype(F32)
        H = range(nh)
        sls = [slice(h * hd, (h + 1) * hd) for h in H]
        hq, lb = hq_ref[...], lb_ref[...]
        q, f, sig = _hgrn_pre(hq, hf_ref[...], lb)
        v, dov = hi_ref[...], do_ref[...]
        k = 1.0 - f
        yield
        bc, blast, eq, ek = _hgrn_decays(f, cmf, reverse)
        yield
        ebc, eb2, eblast = jnp.exp(bc), jnp.exp(blast - bc), jnp.exp(blast)
        qh, kh, qe, k2 = q * eq, k * ek, q * ebc, k * eb2
        st = [st_ref[0, h] for h in H]
        dst = [ds_ref[h] for h in H]
        yield
        att = [jnp.where(cm, _dot(qh[:, sl], kh[:, sl], _DIMS["nt"]), 0.0) for sl in sls]
        yield
        datt = [jnp.where(cm, _dot_nt(dov[:, sl], v[:, sl]), 0.0) for sl in sls]
        yield
        dq_a = [_dot_nn(datt[h], kh[:, sls[h]]) for h in H]
        yield
        dq_s = [_dot_nn(dov[:, sls[h]], st[h]) for h in H]
        yield
        dk_a = [_dot_tn(datt[h], qh[:, sls[h]]) for h in H]
        yield
        dk_s = [_dot_nn(v[:, sls[h]], dst[h]) for h in H]
        yield
        dv = [_dot(att[h], dov[:, sls[h]], _DIMS["tn"]) + _dot(k2[:, sls[h]], dst[h], _DIMS["nt"])
              for h in H]
        yield
        dsn = [eblast[:, sls[h]] * dst[h] + _dot_tn(dov[:, sls[h]], qe[:, sls[h]]) for h in H]
        for h in H:
            ds_ref[h] = dsn[h]
        yield
        dq = jnp.concatenate(dq_a, axis=1) * eq + ebc * jnp.concatenate(dq_s, axis=1)
        dk = jnp.concatenate(dk_a, axis=1) * ek + eb2 * jnp.concatenate(dk_s, axis=1)
        db = q * dq - k * dk
        yield
        dg = _dot01(cmtf, db) + gs_ref[...]
        gs_ref[...] += jnp.sum(db, axis=0, keepdims=True)
        yield
        df = dg / f - dk
        dhf_ref[...] = (df * (1.0 - lb) * sig * (1.0 - sig)).astype(BF16)
        dlb_ref[...] += jnp.sum(df * (1.0 - sig), axis=0, keepdims=True)
        sq = _sigmoid(hq)
        dhq_ref[...] = (dq * (HGRN_HEAD_DIM ** -0.5) * sq * (1.0 + hq * (1.0 - sq))).astype(BF16)
        dhi_ref[...] = jnp.concatenate(dv, axis=1).astype(BF16)

    blk = lambda c: pl.BlockSpec((CHUNK, wh), lambda i: (ci(i), c))
    vec = pl.BlockSpec((1, wh), lambda i: (0, 0))
    big = jax.ShapeDtypeStruct((T, wh), BF16)
    return dict(
        body=body, grid=(nch,), args=[z, z, z, lb, do, states],
        in_specs=[blk(0), blk(1), blk(3 + int(reverse)), vec, blk(0),
                  pl.BlockSpec((1, nh, hd, hd), lambda i: (ci(i), 0, 0, 0))],
        out_shape=[big, big, big, jax.ShapeDtypeStruct((1, wh), F32)],
        out_specs=[blk(0), blk(0), blk(0), vec],
        scratch_shapes=[pltpu.VMEM((nh, hd, hd), F32), pltpu.VMEM((1, wh), F32)])


def _mlstm_intra(cm, cmt, ig_row, ig_col, xf_row, xf_col, m_st):
    lf_row, lf_col = _log_sigmoid(xf_row), _log_sigmoid(xf_col)
    bcol = jnp.sum(jnp.where(cm, lf_row, 0.0), axis=1, keepdims=True)
    brow = jnp.sum(jnp.where(cmt, lf_col, 0.0), axis=0, keepdims=True)
    blast = jnp.sum(lf_row, axis=1, keepdims=True)
    dmat = jnp.where(cm, bcol - brow + ig_row, NEG)
    m_inter = bcol + m_st
    m_t = jnp.maximum(m_inter, jnp.max(dmat, axis=1, keepdims=True))
    p = jnp.exp(dmat - m_t)
    inter = jnp.exp(m_inter - m_t)
    w_col = blast - bcol + ig_col
    m_new = jnp.maximum(blast + m_st, jnp.max(w_col, axis=0, keepdims=True))
    cs = jnp.exp(blast + m_st - m_new)
    kscale = jnp.exp(w_col - m_new)
    return p, inter, m_t, m_new, cs, kscale


def _mlstm_specs(T, wm, nhm, reverse, backward):
    nch = T // CHUNK
    dm = wm // nhm
    ng = 4 * nhm

    def ci(i):
        fwd_order = nch - 1 - i if reverse else i
        return nch - 1 - fwd_order if backward else fwd_order

    row = lambda w, c: pl.BlockSpec((CHUNK, w), lambda i: (ci(i), c))
    gates_row = pl.BlockSpec((1, ng, CHUNK), lambda i: (ci(i), 0, 0))
    bias_row = pl.BlockSpec((1, LANES), lambda i: (0, 0))
    bias_col = pl.BlockSpec((ng, 1), lambda i: (0, 0))
    st_c = pl.BlockSpec((1, nhm, dm, dm), lambda i: (ci(i), 0, 0, 0))
    st_n = pl.BlockSpec((1, nhm, 1, dm), lambda i: (ci(i), 0, 0, 0))
    st_m = pl.BlockSpec((1, nhm, 1, LANES), lambda i: (ci(i), 0, 0, 0))
    return nch, dm, ng, ci, row, gates_row, bias_row, bias_col, st_c, st_n, st_m


def _mlstm_fwd(qk, z, gc, gr, br, bcl, *, wm, nhm, vcol, reverse):
    T = qk.shape[0]
    nch, dm, ng, ci, row, gates_row, bias_row, bias_col, st_c, st_n, st_m = _mlstm_specs(
        T, wm, nhm, reverse, False)
    d = int(reverse)

    def body(q_ref, k_ref, v_ref, gc_ref, gr_ref, br_ref, bc_ref,
             h_ref, cst_ref, nst_ref, mst_ref, c_ref, n_ref, m_ref):
        @pl.when(pl.program_id(0) == 0)
        def _():
            c_ref[...] = jnp.zeros_like(c_ref)
            n_ref[...] = jnp.zeros_like(n_ref)
            m_ref[...] = jnp.full_like(m_ref, M_INIT)

        cm = _chunk_mask(reverse)
        cmt = _chunk_mask(reverse, transpose=True)
        G = gc_ref[...] + br_ref[...]
        Gr = gr_ref[0] + bc_ref[...]
        H = range(nhm)
        sls = [slice(h * dm, (h + 1) * dm) for h in H]
        m_all = [m_ref[h] for h in H]
        intra = [_mlstm_intra(cm, cmt, Gr[d * nhm + h:d * nhm + h + 1, :], G[:, d * nhm + h:d * nhm + h + 1],
                              Gr[2 * nhm + d * nhm + h:2 * nhm + d * nhm + h + 1, :],
                              G[:, 2 * nhm + d * nhm + h:2 * nhm + d * nhm + h + 1], m_all[h][:, 0:1])
                 for h in H]
        p, inter, m_t, m_new, cs, kscale = zip(*intra)
        yield
        q = [q_ref[:, sl] * (dm ** -0.5) for sl in sls]
        k = [k_ref[:, sl] for sl in sls]
        v = [v_ref[:, sl] for sl in sls]
        ct = [c_ref[h] for h in H]
        n = [n_ref[h] for h in H]
        yield
        sc = [_dot_nt(q[h], k[h]) * p[h] for h in H]
        yield
        qc = [_dot(q[h], ct[h], _DIMS["nt"]) for h in H]
        yield
        num = [_dot(sc[h], v[h], _DIMS["nn"]) + inter[h] * qc[h] for h in H]
        yield
        den = [jnp.sum(sc[h], axis=1, keepdims=True) + inter[h] * jnp.sum(q[h] * n[h], axis=1, keepdims=True)
               for h in H]
        kw = [k[h] * kscale[h] for h in H]
        yield
        cnew = [cs[h] * ct[h] + _dot_tn(v[h], kw[h]) for h in H]
        yield
        for h in H:
            cst_ref[0, h] = ct[h]
            nst_ref[0, h] = n[h]
            mst_ref[0, h] = m_all[h]
            h_ref[:, sls[h]] = num[h] / jnp.maximum(jnp.abs(den[h]), jnp.exp(-m_t[h]))
            c_ref[h] = cnew[h]
            n_ref[h] = cs[h] * n[h] + jnp.sum(kw[h], axis=0, keepdims=True)
            m_ref[h] = jnp.broadcast_to(m_new[h], (1, LANES))

    return dict(
        body=body, grid=(nch,), args=[qk, qk, z, gc, gr, br, bcl],
        out_shape=[jax.ShapeDtypeStruct((T, wm), F32),
                   jax.ShapeDtypeStruct((nch, nhm, dm, dm), F32),
                   jax.ShapeDtypeStruct((nch, nhm, 1, dm), F32),
                   jax.ShapeDtypeStruct((nch, nhm, 1, LANES), F32)],
        in_specs=[row(wm, 0), row(wm, 1), row(wm, vcol), row(LANES, 0), gates_row, bias_row, bias_col],
        out_specs=[row(wm, 0), st_c, st_n, st_m],
        scratch_shapes=[pltpu.VMEM((nhm, dm, dm), F32), pltpu.VMEM((nhm, 1, dm), F32),
                        pltpu.VMEM((nhm, 1, LANES), F32)])


def _mlstm_bwd(qk, z, gc, gr, br, bcl, dh, states, *, wm, nhm, vcol, reverse):
    T = qk.shape[0]
    nch, dm, ng, ci, row, gates_row, bias_row, bias_col, st_c, st_n, st_m = _mlstm_specs(
        T, wm, nhm, reverse, True)
    d = int(reverse)

    def body(q_ref, k_ref, v_ref, gc_ref, gr_ref, br_ref, bc_ref, dh_ref, cst_ref, nst_ref, mst_ref,
             dqk_ref, dv_ref, dgr_ref, dgs_ref, e_ref, en_ref, fs_ref):
        @pl.when(pl.program_id(0) == 0)
        def _():
            e_ref[...] = jnp.zeros_like(e_ref)
            en_ref[...] = jnp.zeros_like(en_ref)
            fs_ref[...] = jnp.zeros_like(fs_ref)
            dgs_ref[...] = jnp.zeros_like(dgs_ref)

        cm = _chunk_mask(reverse)
        cmt = _chunk_mask(reverse, transpose=True)
        row_i = lax.broadcasted_iota(jnp.int32, (CHUNK, CHUNK), 0)
        col_i = lax.broadcasted_iota(jnp.int32, (CHUNK, CHUNK), 1)
        eye = row_i == col_i
        G = gc_ref[...] + br_ref[...]
        Gr = gr_ref[0] + bc_ref[...]
        H = range(nhm)
        sls = [slice(h * dm, (h + 1) * dm) for h in H]
        xf_row = [Gr[2 * nhm + d * nhm + h:2 * nhm + d * nhm + h + 1, :] for h in H]
        intra = [_mlstm_intra(cm, cmt, Gr[d * nhm + h:d * nhm + h + 1, :], G[:, d * nhm + h:d * nhm + h + 1],
                              xf_row[h], G[:, 2 * nhm + d * nhm + h:2 * nhm + d * nhm + h + 1],
                              mst_ref[0, h][:, 0:1]) for h in H]
        p, inter, m_t, _, cs, kscale = zip(*intra)
        yield
        q = [q_ref[:, sls[h]] * (dm ** -0.5) for h in H]
        k = [k_ref[:, sls[h]] for h in H]
        v = [v_ref[:, sls[h]] for h in H]
        dhv = [dh_ref[:, sls[h]] for h in H]
        ct = [cst_ref[0, h] for h in H]
        n = [nst_ref[0, h] for h in H]
        et = [e_ref[h] for h in H]
        en = [en_ref[h] for h in H]
        carry = [fs_ref[h][:, 0:1] for h in H]
        yield
        sc = [_dot_nt(q[h], k[h]) * p[h] for h in H]
        yield
        qc = [_dot_nt(q[h], ct[h]) for h in H]
        yield
        num = [_dot_nn(sc[h], v[h]) + inter[h] * qc[h] for h in H]
        yield
        den = [jnp.sum(sc[h], axis=1, keepdims=True) + inter[h] * jnp.sum(q[h] * n[h], axis=1, keepdims=True)
               for h in H]
        floor = [jnp.exp(-m_t[h]) for h in H]
        nstab = [jnp.maximum(jnp.abs(den[h]), floor[h]) for h in H]
        gh = [dhv[h] / nstab[h] for h in H]
        dd = [-jnp.sum(dhv[h] * (num[h] / nstab[h]), axis=1, keepdims=True) / nstab[h]
              * jnp.where(jnp.abs(den[h]) > floor[h], jnp.sign(den[h]), 0.0) for h in H]
        yield
        dsqk = [(_dot_nt(gh[h], v[h]) + dd[h]) * p[h] for h in H]
        qi = [q[h] * inter[h] for h in H]
        kw = [k[h] * kscale[h] for h in H]
        yield
        ghc = [_dot_nn(gh[h], ct[h]) for h in H]
        yield
        vet = [_dot_nn(v[h], et[h]) for h in H]
        yield
        dq = [_dot_nn(dsqk[h], k[h]) + inter[h] * (ghc[h] + dd[h] * n[h]) for h in H]
        yield
        dk = [_dot_tn(dsqk[h], q[h]) + kscale[h] * (vet[h] + en[h]) for h in H]
        yield
        dv = [_dot(sc[h], gh[h], _DIMS["tn"]) + _dot(kw[h], et[h], _DIMS["nt"]) for h in H]
        yield
        e_new = [cs[h] * et[h] + _dot_tn(gh[h], qi[h]) for h in H]
        en_new = [cs[h] * en[h] + jnp.sum(qi[h] * dd[h], axis=0, keepdims=True) for h in H]
        yield
        di_col = [jnp.sum(k[h] * dk[h], axis=1, keepdims=True) for h in H]
        df_col = [jnp.sum(q[h] * dq[h], axis=1, keepdims=True) - di_col[h] for h in H]
        di_row = [jnp.sum(jnp.where(eye, di_col[h], 0.0), axis=0, keepdims=True) for h in H]
        dlf_row = [jnp.sum(jnp.where(cm, df_col[h], 0.0), axis=0, keepdims=True) + carry[h] for h in H]
        dxf_row = [dlf_row[h] * (1.0 - _sigmoid(xf_row[h])) for h in H]
        yield
        for h in H:
            sl = sls[h]
            slk = slice(wm + h * dm, wm + (h + 1) * dm)
            e_ref[h] = e_new[h]
            en_ref[h] = en_new[h]
            fs_ref[h] = jnp.broadcast_to(carry[h] + jnp.sum(df_col[h], axis=0, keepdims=True), (1, LANES))
            dgr_ref[0, h:h + 1, :] = di_row[h]
            dgr_ref[0, nhm + h:nhm + h + 1, :] = dxf_row[h]
            dgs_ref[h:h + 1, :] += jnp.broadcast_to(jnp.sum(di_row[h], axis=1, keepdims=True), (1, LANES))
            dgs_ref[nhm + h:nhm + h + 1, :] += jnp.broadcast_to(
                jnp.sum(dxf_row[h], axis=1, keepdims=True), (1, LANES))
            dqk_ref[:, sl] = (dq[h] * (dm ** -0.5)).astype(BF16)
            dqk_ref[:, slk] = dk[h].astype(BF16)
            dv_ref[:, sl] = dv[h].astype(BF16)

    return dict(
        body=body, grid=(nch,), args=[qk, qk, z, gc, gr, br, bcl, dh] + list(states),
        in_specs=[row(wm, 0), row(wm, 1), row(wm, vcol), row(LANES, 0), gates_row, bias_row, bias_col,
                  row(wm, 0), st_c, st_n, st_m],
        out_shape=[jax.ShapeDtypeStruct((T, 2 * wm), BF16), jax.ShapeDtypeStruct((T, wm), BF16),
                   jax.ShapeDtypeStruct((nch, 2 * nhm, CHUNK), F32),
                   jax.ShapeDtypeStruct((2 * nhm, LANES), F32)],
        out_specs=[row(2 * wm, 0), row(wm, 0),
                   pl.BlockSpec((1, 2 * nhm, CHUNK), lambda i: (ci(i), 0, 0)),
                   pl.BlockSpec((2 * nhm, LANES), lambda i: (0, 0))],
        scratch_shapes=[pltpu.VMEM((nhm, dm, dm), F32), pltpu.VMEM((nhm, 1, dm), F32),
                        pltpu.VMEM((nhm, 1, LANES), F32)])


def _conv_taps(x, w_ref):
    T = x.shape[0]
    t = lax.broadcasted_iota(jnp.int32, x.shape, 0)
    taps = []
    acc = None
    for j in range(CONV_WIDTH):
        s = CONV_WIDTH // 2 - j
        if s == 0:
            xs = x
        else:
            xs = jnp.where((t - s >= 0) & (t - s < T), pltpu.roll(x, s % T, 0), 0.0)
        taps.append(xs)
        term = w_ref[j:j + 1, :] * xs
        acc = term if acc is None else acc + term
    return taps, acc


def _conv_fwd(z, w, b, *, col0, tc=LANES, name):
    T = z.shape[0]
    C2 = w.shape[1]
    assert col0 % tc == 0 and C2 % tc == 0

    def body(z_ref, w_ref, b_ref, o_ref):
        _, acc = _conv_taps(z_ref[...], w_ref)
        c = acc + b_ref[...]
        o_ref[...] = c * _sigmoid(c)

    return pl.pallas_call(
        body, name=name, out_shape=jax.ShapeDtypeStruct((T, C2), F32),
        grid=(C2 // tc,),
        in_specs=[pl.BlockSpec((T, tc), lambda j: (0, col0 // tc + j)),
                  pl.BlockSpec((CONV_WIDTH, tc), lambda j: (0, j)),
                  pl.BlockSpec((1, tc), lambda j: (0, j))],
        out_specs=pl.BlockSpec((T, tc), lambda j: (0, j)),
        compiler_params=_params("parallel"),
    )(z, w, b)


def _conv_bwd(dy_fw, dy_bw, z, w, b, *, col0, tc=LANES, name):
    T = z.shape[0]
    C2 = w.shape[1]

    def body(dyf_ref, dyb_ref, z_ref, w_ref, b_ref, dx_ref, dw_ref, db_ref):
        taps, acc = _conv_taps(z_ref[...], w_ref)
        c = acc + b_ref[...]
        sg = _sigmoid(c)
        dc = (dyf_ref[...].astype(F32) + dyb_ref[...].astype(F32)) * sg * (1.0 + c * (1.0 - sg))
        t = lax.broadcasted_iota(jnp.int32, dc.shape, 0)
        dx = None
        for j in range(CONV_WIDTH):
            s = j - CONV_WIDTH // 2
            if s == 0:
                ds = dc
            else:
                ds = jnp.where((t - s >= 0) & (t - s < T), pltpu.roll(dc, s % T, 0), 0.0)
            term = w_ref[j:j + 1, :] * ds
            dx = term if dx is None else dx + term
            dw_ref[j:j + 1, :] = jnp.sum(dc * taps[j], axis=0, keepdims=True)
        dx_ref[...] = dx.astype(BF16)
        db_ref[...] = jnp.sum(dc, axis=0, keepdims=True)

    col = pl.BlockSpec((T, tc), lambda j: (0, j))
    wsp = pl.BlockSpec((CONV_WIDTH, tc), lambda j: (0, j))
    bsp = pl.BlockSpec((1, tc), lambda j: (0, j))
    return pl.pallas_call(
        body, name=name,
        out_shape=[jax.ShapeDtypeStruct((T, C2), BF16), jax.ShapeDtypeStruct((CONV_WIDTH, C2), F32),
                   jax.ShapeDtypeStruct((1, C2), F32)],
        grid=(C2 // tc,),
        in_specs=[col, col, pl.BlockSpec((T, tc), lambda j: (0, col0 // tc + j)), wsp, bsp],
        out_specs=[col, wsp, bsp],
        compiler_params=_params("parallel"),
    )(dy_fw, dy_bw, z, w, b)


def _mix_heads(o_fw, o_bw, h_fw, h_bw, wh, wm, nhm):
    out = []
    hd = HGRN_HEAD_DIM
    for h in range(wh // hd):
        sl = slice(h * hd, (h + 1) * hd)
        o = o_fw[:, sl] + o_bw[:, sl]
        r = lax.rsqrt(jnp.mean(o * o, axis=-1, keepdims=True) + NORM_EPS)
        out.append((0, sl, o * r, r))
    dm = wm // nhm
    for h in range(nhm):
        sl = slice(h * dm, (h + 1) * dm)
        x = h_fw[:, sl] + h_bw[:, sl]
        xc = x - jnp.mean(x, axis=-1, keepdims=True)
        r = lax.rsqrt(jnp.mean(xc * xc, axis=-1, keepdims=True) + NORM_EPS)
        out.append((1, sl, xc * r, r))
    return out


def _mix_specs(T, wh, wm, tm, gcol, ocol):
    rowh = pl.BlockSpec((tm, wh), lambda i: (i, 0))
    rowm = pl.BlockSpec((tm, wm), lambda i: (i, 0))
    hg = pl.BlockSpec((tm, wh), lambda i: (i, gcol))
    mo = pl.BlockSpec((tm, wm), lambda i: (i, ocol))
    vh = pl.BlockSpec((1, wh), lambda i: (0, 0))
    vm = pl.BlockSpec((1, wm), lambda i: (0, 0))
    return rowh, rowm, hg, mo, vh, vm


def _mix_fwd(o_fw, o_bw, h_fw, h_bw, z, gh, gm, *, nhm, gcol, ocol, tm=256, name):
    T, wh = o_fw.shape
    wm = h_fw.shape[1]
    tm = min(tm, T)
    rowh, rowm, hg, mo, vh, vm = _mix_specs(T, wh, wm, tm, gcol, ocol)

    def body(of_ref, ob_ref, hf_ref, hb_ref, hg_ref, mo_ref, gh_ref, gm_ref, y_ref, yt_ref):
        heads = _mix_heads(of_ref[...], ob_ref[...], hf_ref[...], hb_ref[...], wh, wm, nhm)
        for grp, sl, nv, _ in heads:
            if grp == 0:
                gate = hg_ref[:, sl]
                gate = gate * _sigmoid(gate)
                y = nv * gh_ref[:, sl] * gate
                osl = sl
            else:
                y = nv * gm_ref[:, sl] * _sigmoid(mo_ref[:, sl])
                osl = slice(wh + sl.start, wh + sl.stop)
            y_ref[:, osl] = y.astype(BF16)
            yt_ref[osl, :] = y.T.astype(BF16)

    return pl.pallas_call(
        body, name=name,
        out_shape=[jax.ShapeDtypeStruct((T, wh + wm), BF16), jax.ShapeDtypeStruct((wh + wm, T), BF16)],
        grid=(T // tm,),
        in_specs=[rowh, rowh, rowm, rowm, hg, mo, vh, vm],
        out_specs=[pl.BlockSpec((tm, wh + wm), lambda i: (i, 0)),
                   pl.BlockSpec((wh + wm, tm), lambda i: (0, i))],
        compiler_params=_params("parallel"),
    )(o_fw, o_bw, h_fw, h_bw, z, z, gh, gm)


def _mix_bwd(dy, o_fw, o_bw, h_fw, h_bw, z, gh, gm, *, nhm, gcol, ocol, tm=256, name, comm=None):
    T, wh = o_fw.shape
    wm = h_fw.shape[1]
    tm = min(tm, T)
    rowh, rowm, hg, mo, vh, vm = _mix_specs(T, wh, wm, tm, gcol, ocol)

    def body(dy_ref, of_ref, ob_ref, hf_ref, hb_ref, hg_ref, mo_ref, gh_ref, gm_ref,
             do_ref, dh_ref, dhg_ref, dmo_ref, dgh_ref, dgm_ref):
        @pl.when(pl.program_id(0) == 0)
        def _():
            dgh_ref[...] = jnp.zeros_like(dgh_ref)
            dgm_ref[...] = jnp.zeros_like(dgm_ref)

        heads = _mix_heads(of_ref[...], ob_ref[...], hf_ref[...], hb_ref[...], wh, wm, nhm)
        for grp, sl, nv, r in heads:
            if grp == 0:
                d = dy_ref[:, sl]
                x = hg_ref[:, sl]
                sg = _sigmoid(x)
                gate = x * sg
                g = gh_ref[:, sl]
                dgh_ref[:, sl] += jnp.sum(d * nv * gate, axis=0, keepdims=True)
                dhg_ref[:, sl] = (d * nv * g * sg * (1.0 + x * (1.0 - sg))).astype(BF16)
                t = d * g * gate
                do_ref[:, sl] = r * (t - nv * jnp.mean(t * nv, axis=-1, keepdims=True))
            else:
                d = dy_ref[:, slice(wh + sl.start, wh + sl.stop)]
                sg = _sigmoid(mo_ref[:, sl])
                g = gm_ref[:, sl]
                dgm_ref[:, sl] += jnp.sum(d * nv * sg, axis=0, keepdims=True)
                dmo_ref[:, sl] = (d * nv * g * sg * (1.0 - sg)).astype(BF16)
                t = d * g * sg
                dh_ref[:, sl] = r * (t - jnp.mean(t, axis=-1, keepdims=True)
                                     - nv * jnp.mean(t * nv, axis=-1, keepdims=True))

    return _pcall(
        body, name=name, comm=comm,
        out_shape=[jax.ShapeDtypeStruct((T, wh), F32), jax.ShapeDtypeStruct((T, wm), F32),
                   jax.ShapeDtypeStruct((T, wh), BF16), jax.ShapeDtypeStruct((T, wm), BF16),
                   jax.ShapeDtypeStruct((1, wh), F32), jax.ShapeDtypeStruct((1, wm), F32)],
        grid=(T // tm,),
        in_specs=[pl.BlockSpec((tm, wh + wm), lambda i: (i, 0)), rowh, rowh, rowm, rowm, hg, mo, vh, vm],
        out_specs=[rowh, rowm, rowh, rowm, vh, vm],
        sem=("arbitrary",),
    )(dy, o_fw, o_bw, h_fw, h_bw, z, z, gh, gm)


def _pick(n, pref):
    for c in range(pref - pref % LANES, 0, -LANES):
        if n % c == 0:
            return c
    return n


def _mm(a, b, mode="nn", *, tm=1024, tn=1024, tk=2048, **kw):
    if mode == "nn":
        (M, K), N = a.shape, b.shape[1]
    else:
        (M, K), N = a.shape, kw.get("rows") or b.shape[0]
    return _matmul(a, b, mode=mode, tm=_pick(M, tm), tn=_pick(N, tn), tk=_pick(K, tk), **kw)


def _device_step(x, target, sh, small_parts, small_params, core):
    T, D = x.shape

    def full(g):
        return g.reshape(-1, D)

    def shards(g):
        return g.reshape(N_DEV, -1, D)

    def pair_add(name, g, land):
        return _pair_add(g, land, core, "rs_add_" + name)

    def dw(lhs_t, rhs, name, scale=1.0):
        return shards(_mm(lhs_t, rhs, out_dtype=BF16, scale=scale, tm=704, tk=T, name=name))

    xb, (g_w1a, g_small) = _cast(x, name="x_cast", comm=_ag_job([sh["ffn1_w1"], small_parts]))
    w1a = full(g_w1a)
    P = small_params(g_small)
    wh = P["hgrn_norm_g"].shape[1]
    wm = P["mlstm_norm_g"].shape[1]
    nhm = P["ig_b"].shape[1]
    ng = 4 * nhm
    nch = T // CHUNK
    assert wh == wm and T % CHUNK == 0
    vcol, ocol, gcol = 7, 8, 2
    col0 = 5 * wh
    nz = 5 * wh + 4 * wm
    c1 = sh["w_in"].shape[0] // 4 // 16 * 16
    a1f, (g_w3a, g_wp1) = _mm(xb, w1a, "nt", out_dtype=BF16, name="ffn1_up_a",
                              comm=_ag_job([sh["ffn1_w3"], sh["w_in"][:c1]]))
    w3a = full(g_w3a)
    (a1, b1, h1, h1t), (g_w2a, g_wp2) = _ffn_up(xb, None, w3a, a=a1f, name="ffn1_up_b",
                                                comm=_ag_job([sh["ffn1_w2"], sh["w_in"][c1:2 * c1]]))
    w2a = full(g_w2a)
    r1, (g_wp3,) = _mm(h1, w2a, add=x, scale=0.5, add_scale=DN_ALPHA, tk=w2a.shape[0] // 4,
                       name="ffn1_down", comm=_ag_job([sh["w_in"][2 * c1:]]))
    w_int = full(jnp.concatenate([g_wp1, g_wp2, g_wp3], axis=1))
    w_wgt = jnp.zeros((LANES, D), BF16).at[:ng].set(w_int[nz:])
    x1, x1b = _ln_fwd(r1, P["ln1_g"], P["ln1_b"], name="ln1")
    zm, (g_w3b,) = _mm(x1b, w_int, "nt", rows=nz, name="zm", comm=_ag_job([sh["ffn2_w3"]]))
    zg = _mm(x1b, w_wgt, "nt", name="zg")
    gr = zg[:, :ng].reshape(nch, CHUNK, ng).transpose(0, 2, 1)
    bias = jnp.concatenate([P["ig_b"].reshape(-1), P["fg_b"].reshape(-1)])
    br = jnp.zeros((1, LANES), F32).at[0, :ng].set(bias)
    bcl = bias.reshape(ng, 1)
    lbf, lbb = P["lb"][0:1], P["lb"][1:2]
    ((o_fw, s_fw), (o_bw, s_bw)), (g_w1b, g_wout) = _run_specs(
        [_hgrn_fwd(zm, lbf, wh=wh, reverse=False), _hgrn_fwd(zm, lbb, wh=wh, reverse=True)],
        name="hgrn_fwd", comm=_ag_job([sh["ffn2_w1"], sh["w_out"]]))
    w_wout = full(g_wout)
    qk = _conv_fwd(zm, P["conv_w"], P["conv_b"], col0=col0, name="conv")
    mkw = dict(wm=wm, nhm=nhm, vcol=vcol)
    ((h_fw, *st_fw),) = _run_specs([_mlstm_fwd(qk, zm, zg, gr, br, bcl, reverse=False, **mkw)],
                                   name="mlstm_fw")
    ((h_bw, *st_bw),) = _run_specs([_mlstm_fwd(qk, zm, zg, gr, br, bcl, reverse=True, **mkw)],
                                   name="mlstm_bw")
    w1b, w3b = full(g_w1b), full(g_w3b)
    y, yt = _mix_fwd(o_fw, o_bw, h_fw, h_bw, zm, P["hgrn_norm_g"], P["mlstm_norm_g"],
                     nhm=nhm, gcol=gcol, ocol=ocol, name="mix")
    r2 = _mm(y, w_wout, add=x1, add_scale=DN_ALPHA, name="r2")
    x2, x2b = _ln_fwd(r2, P["ln2_g"], P["ln2_b"], name="ln2")
    (a2, b2, h2, h2t), (g_w2b,) = _ffn_up(x2b, w1b, w3b, name="ffn2_up", comm=_ag_job([sh["ffn2_w2"]]))
    w2b = full(g_w2b)
    r3 = _mm(h2, w2b, add=x2, scale=0.5, add_scale=DN_ALPHA, tk=w2b.shape[0] // 4, name="ffn2_down")

    G, land = {}, {}
    dr3, dr3b, G["ln3_g"], G["ln3_b"], loss = _ln_bwd(
        None, r3, P["ln3_g"], b=P["ln3_b"], target=target, name="ln3_bwd")
    da2, db2, da2t, db2t = _ffn_bwd_act(dr3b, w2b, a2, b2, name="ffn2_dact")
    gw1b, gw3b = dw(da2t, x2b, "ffn2_dw1"), dw(db2t, x2b, "ffn2_dw3")
    gw2b = dw(h2t, dr3b, "ffn2_dw2", scale=0.5)
    dx2, l1 = _ffn_dx(da2, db2, w1b, w3b, dr3, name="ffn2_dx", comm=_pair_job([gw1b, gw3b, gw2b]))
    p_w1b, p_w3b, p_w2b = [pair_add(k, g, l) for k, g, l in
                           zip(("ffn2_w1", "ffn2_w3", "ffn2_w2"), (gw1b, gw3b, gw2b), l1)]
    dr2, dr2b, G["ln2_g"], G["ln2_b"] = _ln_bwd(dx2, r2, P["ln2_g"], name="ln2_bwd")
    dy = _mm(dr2b, w_wout, "nt", name="dy")
    gwout = shards(_mm(yt, dr2b, out_dtype=BF16, name="dwout"))
    (do, dh, dhg, dmo, G["hgrn_norm_g"], G["mlstm_norm_g"]), (l1,) = _mix_bwd(
        dy, o_fw, o_bw, h_fw, h_bw, zm, P["hgrn_norm_g"], P["mlstm_norm_g"],
        nhm=nhm, gcol=gcol, ocol=ocol, name="mix_bwd", comm=_pair_job([gwout]))
    p_wout = pair_add("w_out", gwout, l1)
    ((dqk_f, dv_f, dgr_f, dgs_f),), (land["ffn2_w1"],) = _run_specs(
        [_mlstm_bwd(qk, zm, zg, gr, br, bcl, dh, st_fw, reverse=False, **mkw)],
        name="mlstm_fw_bwd", comm=_chips_job([p_w1b]))
    ((dqk_b, dv_b, dgr_b, dgs_b),), (land["ffn2_w3"],) = _run_specs(
        [_mlstm_bwd(qk, zm, zg, gr, br, bcl, dh, st_bw, reverse=True, **mkw)],
        name="mlstm_bw_bwd", comm=_chips_job([p_w3b]))
    dmqk, G["conv_w"], G["conv_b"] = _conv_bwd(dqk_f, dqk_b, zm, P["conv_w"], P["conv_b"], col0=col0,
                                               name="conv_bwd")
    ((dhq_f, dhi_f, dhf_fw, dlb_f), (dhq_b, dhi_b, dhf_bw, dlb_b)), (land["ffn2_w2"], land["w_out"]) = \
        _run_specs([_hgrn_bwd(zm, lbf, do, s_fw, wh=wh, reverse=False),
                    _hgrn_bwd(zm, lbb, do, s_bw, wh=wh, reverse=True)],
                   name="hgrn_bwd", comm=_chips_job([p_w2b, p_wout]))
    G["lb"] = jnp.concatenate([dlb_f, dlb_b], axis=0)
    G["ig_b"] = jnp.stack([dgs_f[:nhm, 0], dgs_b[:nhm, 0]])
    G["fg_b"] = jnp.stack([dgs_f[nhm:, 0], dgs_b[nhm:, 0]])
    dzm, dzmt = _dz_pack([(dhq_f, dhq_b), (dhi_f, dhi_b), dhg, dhf_fw, dhf_bw, dmqk, (dv_f, dv_b), dmo],
                         name="dz_pack")
    dgr = jnp.concatenate([dgr_f[:, :nhm], dgr_b[:, :nhm], dgr_f[:, nhm:], dgr_b[:, nhm:]], axis=1)
    dzgt = jnp.zeros((LANES, T), F32).at[:ng].set(dgr.transpose(1, 0, 2).reshape(ng, T)).astype(BF16)
    dzg = jnp.zeros((T, LANES), F32).at[:, :ng].set(dgr.transpose(0, 2, 1).reshape(T, ng)).astype(BF16)
    gwin = shards(jnp.concatenate([_mm(dzmt, x1b, out_dtype=BF16, tk=T, name="dwin"),
                                   _mm(dzgt, x1b, out_dtype=BF16, tk=T, name="dwg")[:ng]], axis=0))
    t, (l1,) = _mm(dzg, w_wgt, add=dr2, add_scale=DN_ALPHA, name="dx1_g", comm=_pair_job([gwin]))
    p_win = pair_add("w_in", gwin, l1)
    dx1, (land["w_in"],) = _mm(dzm, w_int, rows=nz, add=t, tk=1536, name="dx1", comm=_chips_job([p_win]))
    dr1, dr1b, G["ln1_g"], G["ln1_b"] = _ln_bwd(dx1, r1, P["ln1_g"], name="ln1_bwd")
    gw2a = dw(h1t, dr1b, "ffn1_dw2", scale=0.5)
    (da1, db1, da1t, db1t), (l1,) = _ffn_bwd_act(dr1b, w2a, a1, b1, name="ffn1_dact",
                                                 comm=_pair_job([gw2a]))
    p_w2a = pair_add("ffn1_w2", gw2a, l1)
    gw1a, (land["ffn1_w2"],) = _mm(da1t, xb, out_dtype=BF16, tm=704, tk=T, name="ffn1_dw1",
                                   comm=_chips_job([p_w2a]))
    gw1a = shards(gw1a)
    gw3a, (l1,) = _mm(db1t, xb, out_dtype=BF16, tm=704, tk=T, name="ffn1_dw3", comm=_pair_job([gw1a]))
    gw3a = shards(gw3a)
    p_w1a = pair_add("ffn1_w1", gw1a, l1)
    (l1,) = _comm_call(_pair_job([gw3a]), "rs_pair_ffn1_w3")
    p_w3a = pair_add("ffn1_w3", gw3a, l1)
    gx, (land["ffn1_w1"], land["ffn1_w3"]) = _ffn_dx(
        da1, db1, w1a, w3a, dr1, name="ffn1_dx", comm=_chips_job([p_w1a, p_w3a]))
    return loss, gx, land, G


ANY = pl.BlockSpec(memory_space=pl.ANY)


def _place():
    x, y, c = lax.axis_index("x"), lax.axis_index("y"), lax.axis_index("c")
    chips = [(1 - x, y), (x, 1 - y), (1 - x, 1 - y)]
    return x, y, c, chips


def _comm_call(job, name):
    n_in, n_out = len(job.inputs), len(job.out_shape)

    def body(*refs):
        ins, outs, sems = refs[:n_in], refs[n_in:n_in + n_out], refs[n_in + n_out:]
        job.start(ins, outs, sems)
        job.middle(ins, outs, sems)
        job.finish(ins, outs, sems)

    return pl.pallas_call(body, name=name, out_shape=job.out_shape, in_specs=[ANY] * n_in,
                          out_specs=[ANY] * n_out, scratch_shapes=job.sems)(*job.inputs)


def _ag_job(parts):
    n = len(parts)
    halves = [p.shape[0] // 2 // 16 * 16 or p.shape[0] // 2 // 8 * 8 or p.shape[0] for p in parts]
    cut = [h < p.shape[0] for h, p in zip(halves, parts)]

    def helpers(ins, outs, sems):
        send_sems, recv_sems, local_sems = sems
        x, y, c, _ = _place()
        me, sibling = (x, y, c), (x, y, 1 - c)
        xn, yn, dg = (1 - x, y), (x, 1 - y), (1 - x, 1 - y)

        def slot(a, block, part=None):
            bx, by, bc = block
            ref = outs[a].at[4 * bx + 2 * by + bc]
            if part == 0:
                return ref.at[pl.ds(0, halves[a])]
            if part == 1:
                return ref.at[pl.ds(halves[a], parts[a].shape[0] - halves[a])]
            return ref

        def copy(a, k, block, to, src=None, part=None):
            dst = slot(a, block, part)
            return pltpu.make_async_remote_copy(
                src_ref=dst if src is None else src, dst_ref=dst,
                send_sem=send_sems.at[a, k], recv_sem=recv_sems.at[a, k],
                device_id=to, device_id_type=MESH)

        local = [pltpu.make_async_copy(ins[a], slot(a, me), local_sems.at[a]) for a in range(n)]
        return copy, local, me, sibling, xn, yn, dg, c

    def start(ins, outs, sems):
        copy, local, me, sibling, xn, yn, dg, c = helpers(ins, outs, sems)
        for cp in local:
            cp.start()
        for a in range(n):
            copy(a, 0, me, sibling, src=ins[a]).start()
            copy(a, 1, me, (*xn, c), src=ins[a]).start()
            copy(a, 2, me, (*yn, c), src=ins[a]).start()

    def middle(ins, outs, sems):
        copy, local, me, sibling, xn, yn, dg, c = helpers(ins, outs, sems)
        for a in range(n):
            copy(a, 1, (*xn, c), me).wait_recv()
            copy(a, 3, (*xn, c), (*yn, c), part=0).start()
            copy(a, 5, (*xn, c), sibling).start()
        for a in range(n):
            copy(a, 2, (*yn, c), me).wait_recv()
            if cut[a]:
                copy(a, 4, (*yn, c), (*xn, c), part=1).start()
            copy(a, 6, (*yn, c), sibling).start()

    def finish(ins, outs, sems):
        copy, local, me, sibling, xn, yn, dg, c = helpers(ins, outs, sems)
        for a in range(n):
            copy(a, 3, (*dg, c), me, part=0).wait_recv()
            if cut[a]:
                copy(a, 4, (*dg, c), me, part=1).wait_recv()
            copy(a, 7, (*dg, c), sibling).start()
        for a in range(n):
            copy(a, 0, sibling, me).wait_recv()
            copy(a, 5, (*xn, 1 - c), me).wait_recv()
            copy(a, 6, (*yn, 1 - c), me).wait_recv()
            copy(a, 7, (*dg, 1 - c), me).wait_recv()
        for a in range(n):
            copy(a, 0, me, sibling, src=ins[a]).wait_send()
            copy(a, 1, me, (*xn, c), src=ins[a]).wait_send()
            copy(a, 2, me, (*yn, c), src=ins[a]).wait_send()
            copy(a, 3, (*xn, c), (*yn, c), part=0).wait_send()
            if cut[a]:
                copy(a, 4, (*yn, c), (*xn, c), part=1).wait_send()
            for k, blk in ((5, xn), (6, yn), (7, dg)):
                copy(a, k, (*blk, c), sibling).wait_send()
        for cp in local:
            cp.wait()

    return _Job(parts, [jax.ShapeDtypeStruct((N_DEV,) + p.shape, p.dtype) for p in parts],
                [pltpu.SemaphoreType.DMA((n, 8)), pltpu.SemaphoreType.DMA((n, 8)),
                 pltpu.SemaphoreType.DMA((n,))], start, finish, middle)


def _pair_job(gs):
    n = len(gs)

    def copies(g_refs, land_refs, sems):
        send_sems, recv_sems = sems
        x, y, c, _ = _place()
        return [pltpu.make_async_remote_copy(
            src_ref=g_refs[a].at[2 * k + 1 - c], dst_ref=land_refs[a].at[k],
            send_sem=send_sems.at[a, k], recv_sem=recv_sems.at[a, k],
            device_id=(x, y, 1 - c), device_id_type=MESH) for a in range(n) for k in range(4)]

    def start(ins, outs, sems):
        for cp in copies(ins, outs, sems):
            cp.start()

    def finish(ins, outs, sems):
        for cp in copies(ins, outs, sems):
            cp.wait()

    return _Job(gs, [jax.ShapeDtypeStruct((4,) + g.shape[1:], g.dtype) for g in gs],
                [pltpu.SemaphoreType.DMA((n, 4)), pltpu.SemaphoreType.DMA((n, 4))], start, finish)


def _chips_job(ps):
    n = len(ps)

    def copies(p_refs, land_refs, sems):
        send_sems, recv_sems, local_sems = sems
        x, y, c, chips = _place()
        mine = 2 * x + y
        owns = [pltpu.make_async_copy(p_refs[a].at[mine], land_refs[a].at[mine], local_sems.at[a])
                for a in range(n)]
        sends = [pltpu.make_async_remote_copy(
            src_ref=p_refs[a].at[2 * chip[0] + chip[1]], dst_ref=land_refs[a].at[mine],
            send_sem=send_sems.at[a, j], recv_sem=recv_sems.at[a, j],
            device_id=(*chip, c), device_id_type=MESH) for a in range(n) for j, chip in enumerate(chips)]
        recvs = [pltpu.make_async_remote_copy(
            src_ref=p_refs[a].at[mine], dst_ref=land_refs[a].at[2 * chip[0] + chip[1]],
            send_sem=send_sems.at[a, j], recv_sem=recv_sems.at[a, j],
            device_id=(*chip, c), device_id_type=MESH) for a in range(n) for j, chip in enumerate(chips)]
        return owns, sends, recvs

    def start(ins, outs, sems):
        owns, sends, _ = copies(ins, outs, sems)
        for cp in owns + sends:
            cp.start()

    def finish(ins, outs, sems):
        owns, sends, recvs = copies(ins, outs, sems)
        for cp in recvs:
            cp.wait_recv()
        for cp in sends:
            cp.wait_send()
        for cp in owns:
            cp.wait()

    return _Job(ps, [jax.ShapeDtypeStruct(p.shape, p.dtype) for p in ps],
                [pltpu.SemaphoreType.DMA((n, 3)), pltpu.SemaphoreType.DMA((n, 3)),
                 pltpu.SemaphoreType.DMA((n,))], start, finish)


def _all_reduce_small(buf, name):
    rows = buf.shape[0]

    def body(b_ref, o_ref, slots, send_sems, recv_sems):
        x, y, c, _ = _place()
        me = 4 * x + 2 * y + c
        slots[me] = b_ref[...]
        cps = []
        for k in range(1, N_DEV):
            fx, fy, fc = (k >> 2) & 1, (k >> 1) & 1, k & 1
            peer = (x ^ fx, y ^ fy, c ^ fc)
            cps.append(pltpu.make_async_remote_copy(
                src_ref=b_ref, dst_ref=slots.at[me],
                send_sem=send_sems.at[k - 1], recv_sem=recv_sems.at[k - 1],
                device_id=peer, device_id_type=MESH))
        for cp in cps:
            cp.start()
        for k in range(1, N_DEV):
            fx, fy, fc = (k >> 2) & 1, (k >> 1) & 1, k & 1
            src = 4 * (x ^ fx) + 2 * (y ^ fy) + (c ^ fc)
            pltpu.make_async_remote_copy(
                src_ref=b_ref, dst_ref=slots.at[src],
                send_sem=send_sems.at[k - 1], recv_sem=recv_sems.at[k - 1],
                device_id=(x ^ fx, y ^ fy, c ^ fc), device_id_type=MESH).wait_recv()
        for cp in cps:
            cp.wait_send()
        acc = slots[0]
        for k in range(1, N_DEV):
            acc = acc + slots[k]
        o_ref[...] = acc

    vm = pl.BlockSpec(memory_space=pltpu.VMEM)
    return pl.pallas_call(
        body, name=name, out_shape=jax.ShapeDtypeStruct(buf.shape, F32),
        in_specs=[vm], out_specs=vm,
        scratch_shapes=[pltpu.VMEM((N_DEV, rows, LANES), F32),
                        pltpu.SemaphoreType.DMA((N_DEV - 1,)), pltpu.SemaphoreType.DMA((N_DEV - 1,))],
    )(buf)


def _row_tile(rows, pref=2048):
    for t in range(min(pref, rows) - min(pref, rows) % 8, 0, -8):
        if rows % t == 0:
            return t
    return rows


def _tile2(rows, cols, nbuf):
    budget = VMEM_LIMIT // 2 // (2 * nbuf * 4)
    for t in range(min(512, rows) // 16 * 16, 0, -16):
        if rows % t == 0 and t * cols <= budget:
            return t, cols
    return rows, _pick(cols, max(LANES, budget // rows // LANES * LANES))


def _pair_add(g, land, core, name):
    _, r, c = g.shape
    tr, tc = _tile2(r, c, 3)

    def body(core_ref, g_ref, l_ref, o_ref):
        o_ref[...] = (g_ref[...].astype(F32) + l_ref[...].astype(F32)).astype(o_ref.dtype)

    blk = pl.BlockSpec((1, tr, tc), lambda k, i, j, cr: (k, i, j))
    return pl.pallas_call(
        body, name=name, out_shape=jax.ShapeDtypeStruct(land.shape, land.dtype),
        grid_spec=pltpu.PrefetchScalarGridSpec(
            num_scalar_prefetch=1, grid=(4, r // tr, c // tc),
            in_specs=[pl.BlockSpec((1, tr, tc), lambda k, i, j, cr: (2 * k + cr[0], i, j)), blk],
            out_specs=blk),
        compiler_params=_params("parallel", "parallel", "parallel"),
    )(core, g, land)


def _adamw(w, parts, m, v, name):
    shp = w.shape
    cols = shp[-1]
    w2, m2, v2 = (t.reshape(-1, cols) for t in (w, m, v))
    rows = w2.shape[0]
    n = parts.shape[0]
    assert parts.shape[1:] == (rows, cols), (parts.shape, shp)
    tr, tc = _tile2(rows, cols, n + 7)
    c1 = 1.0 / (1.0 - ADAM_B1 ** ADAM_STEP)
    c2 = 1.0 / (1.0 - ADAM_B2 ** ADAM_STEP)

    def body(*refs):
        p_refs = refs[:n]
        w_ref, m_ref, v_ref, g_ref, d_ref, nm_ref, nv_ref = refs[n:]
        gv = p_refs[0][0].astype(F32)
        for p_ref in p_refs[1:]:
            gv = gv + p_ref[0].astype(F32)
        nm = ADAM_B1 * m_ref[...] + (1.0 - ADAM_B1) * gv
        nv = ADAM_B2 * v_ref[...] + (1.0 - ADAM_B2) * (gv * gv)
        g_ref[...] = gv
        nm_ref[...] = nm
        nv_ref[...] = nv
        d_ref[...] = -ADAM_LR * ((nm * c1) / (jnp.sqrt(nv * c2) + ADAM_EPS) + ADAM_WD * w_ref[...])

    blk = pl.BlockSpec((tr, tc), lambda i, j: (i, j))
    slab = lambda k: pl.BlockSpec((1, tr, tc), lambda i, j: (k, i, j))
    out = jax.ShapeDtypeStruct((rows, cols), F32)
    outs = pl.pallas_call(
        body, name=name, out_shape=[out] * 4, grid=(rows // tr, cols // tc),
        in_specs=[slab(k) for k in range(n)] + [blk] * 3, out_specs=[blk] * 4,
        compiler_params=_params("parallel", "parallel"),
    )(*([parts] * n), w2, m2, v2)
    return tuple(o.reshape(shp) for o in outs)


def _pack_rows(arrs):
    rows = []
    for a in arrs:
        flat = a.reshape(-1).astype(F32)
        pad = (-flat.shape[0]) % LANES
        if pad:
            flat = jnp.concatenate([flat, jnp.zeros((pad,), F32)])
        rows.append(flat.reshape(-1, LANES))
    out = jnp.concatenate(rows, axis=0)
    pad = (-out.shape[0]) % 8
    if pad:
        out = jnp.concatenate([out, jnp.zeros((pad, LANES), F32)], axis=0)
    return out


def _unpack_rows(buf, shapes):
    outs, r = [], 0
    for s in shapes:
        n = math.prod(s)
        nr = -(-n // LANES)
        outs.append(buf[r:r + nr].reshape(-1)[:n].reshape(s))
        r += nr
    return outs


BIG = ("ffn1_w1", "ffn1_w3", "ffn1_w2", "w_in", "w_out", "ffn2_w1", "ffn2_w3", "ffn2_w2")
ROW_SHARDED = ("ffn1_w2", "w_out", "ffn2_w2")
SMALL = ("ln1_g", "ln1_b", "hgrn_lb", "hgrn_norm_g", "mlstm_conv_w", "mlstm_conv_b", "mlstm_ig_b",
         "mlstm_fg_b", "mlstm_norm_g", "ln2_g", "ln2_b", "ln3_g", "ln3_b")
WEIGHTS = ("ffn1_w1", "ffn1_w3", "ffn1_w2", "ln1_g", "ln1_b", "w_in", "hgrn_lb", "hgrn_norm_g",
           "mlstm_conv_w", "mlstm_conv_b", "mlstm_ig_b", "mlstm_fg_b", "mlstm_norm_g", "w_out",
           "ln2_g", "ln2_b", "ffn2_w1", "ffn2_w3", "ffn2_w2", "ln3_g", "ln3_b")


def kernel(x, ffn1_w1, ffn1_w3, ffn1_w2, ln1_g, ln1_b, w_in, hgrn_lb, hgrn_norm_g, mlstm_conv_w, mlstm_conv_b, mlstm_ig_b, mlstm_fg_b, mlstm_norm_g, w_out, ln2_g, ln2_b, ffn2_w1, ffn2_w3, ffn2_w2, ln3_g, ln3_b, loss_target, m_ffn1_w1, m_ffn1_w3, m_ffn1_w2, m_ln1_g, m_ln1_b, m_w_in, m_hgrn_lb, m_hgrn_norm_g, m_mlstm_conv_w, m_mlstm_conv_b, m_mlstm_ig_b, m_mlstm_fg_b, m_mlstm_norm_g, m_w_out, m_ln2_g, m_ln2_b, m_ffn2_w1, m_ffn2_w3, m_ffn2_w2, m_ln3_g, m_ln3_b, v_ffn1_w1, v_ffn1_w3, v_ffn1_w2, v_ln1_g, v_ln1_b, v_w_in, v_hgrn_lb, v_hgrn_norm_g, v_mlstm_conv_w, v_mlstm_conv_b, v_mlstm_ig_b, v_mlstm_fg_b, v_mlstm_norm_g, v_w_out, v_ln2_g, v_ln2_b, v_ffn2_w1, v_ffn2_w3, v_ffn2_w2, v_ln3_g, v_ln3_b):
    args = (ffn1_w1, ffn1_w3, ffn1_w2, ln1_g, ln1_b, w_in, hgrn_lb, hgrn_norm_g, mlstm_conv_w,
            mlstm_conv_b, mlstm_ig_b, mlstm_fg_b, mlstm_norm_g, w_out, ln2_g, ln2_b, ffn2_w1, ffn2_w3,
            ffn2_w2, ln3_g, ln3_b)
    ms = (m_ffn1_w1, m_ffn1_w3, m_ffn1_w2, m_ln1_g, m_ln1_b, m_w_in, m_hgrn_lb, m_hgrn_norm_g,
          m_mlstm_conv_w, m_mlstm_conv_b, m_mlstm_ig_b, m_mlstm_fg_b, m_mlstm_norm_g, m_w_out, m_ln2_g,
          m_ln2_b, m_ffn2_w1, m_ffn2_w3, m_ffn2_w2, m_ln3_g, m_ln3_b)
    vs = (v_ffn1_w1, v_ffn1_w3, v_ffn1_w2, v_ln1_g, v_ln1_b, v_w_in, v_hgrn_lb, v_hgrn_norm_g,
          v_mlstm_conv_w, v_mlstm_conv_b, v_mlstm_ig_b, v_mlstm_fg_b, v_mlstm_norm_g, v_w_out, v_ln2_g,
          v_ln2_b, v_ffn2_w1, v_ffn2_w3, v_ffn2_w2, v_ln3_g, v_ln3_b)
    w = dict(zip(WEIGHTS, args))
    m = dict(zip(WEIGHTS, ms))
    v = dict(zip(WEIGHTS, vs))
    core = lax.axis_index("c")
    dev = 4 * lax.axis_index("x") + 2 * lax.axis_index("y") + core
    p_lb = []

    def small_params(gathered_small):
        lb_sh, cw_sh = zip(*[_unpack_rows(gathered_small[i], [hgrn_lb.shape, mlstm_conv_w.shape])
                             for i in range(N_DEV)])
        hgrn_lb_full = jnp.concatenate(lb_sh, axis=-1)
        conv_w_full = jnp.concatenate(cw_sh, axis=-1)[0]
        p_lb.append(jax.nn.softmax(hgrn_lb_full, axis=1))
        return dict(ln1_g=ln1_g, ln1_b=ln1_b, ln2_g=ln2_g, ln2_b=ln2_b, ln3_g=ln3_g, ln3_b=ln3_b,
                    lb=p_lb[0][:, 0], hgrn_norm_g=hgrn_norm_g, conv_w=conv_w_full, conv_b=mlstm_conv_b,
                    ig_b=mlstm_ig_b[0], fg_b=mlstm_fg_b[0], mlstm_norm_g=mlstm_norm_g)

    def rows_first(t, k):
        return t[0] if k in ROW_SHARDED else t[0].T

    loss_row, grad_x, land2, G = _device_step(
        x[0], loss_target[0], {k: rows_first(w[k], k).astype(BF16) for k in BIG},
        _pack_rows([hgrn_lb, mlstm_conv_w]), small_params, jnp.reshape(core, (1,)).astype(jnp.int32))
    p_lb = p_lb[0]

    dlb = G["lb"]
    g_lb = jnp.stack([dlb * p_lb[:, 0] * (1.0 - p_lb[:, 0]), -dlb * p_lb[:, 0] * p_lb[:, 1]], axis=1)
    small_full = {"ln1_g": G["ln1_g"], "ln1_b": G["ln1_b"], "hgrn_lb": g_lb, "hgrn_norm_g": G["hgrn_norm_g"],
                  "mlstm_conv_w": G["conv_w"][None], "mlstm_conv_b": G["conv_b"],
                  "mlstm_ig_b": G["ig_b"][None], "mlstm_fg_b": G["fg_b"][None],
                  "mlstm_norm_g": G["mlstm_norm_g"], "ln2_g": G["ln2_g"], "ln2_b": G["ln2_b"],
                  "ln3_g": G["ln3_g"], "ln3_b": G["ln3_b"]}
    small_list = [small_full[k] for k in SMALL] + [loss_row]
    reduced = _all_reduce_small(_pack_rows(small_list), "all_reduce_small")
    red = _unpack_rows(reduced, [a.shape for a in small_list])
    loss = red[-1][0, 0]
    small_g = {}
    for k, gk in zip(SMALL, red[:-1]):
        if k in ("hgrn_lb", "mlstm_conv_w"):
            n = w[k].shape[-1]
            gk = lax.dynamic_slice_in_dim(gk, dev * n, n, axis=gk.ndim - 1)
        small_g[k] = gk

    grads, delta, new_m, new_v = {}, {}, {}, {}
    for k in BIG:
        res = _adamw(rows_first(w[k], k), land2[k], rows_first(m[k], k), rows_first(v[k], k), "adamw_" + k)
        grads[k], delta[k], new_m[k], new_v[k] = [(r if k in ROW_SHARDED else r.T)[None] for r in res]
    sm = _adamw(_pack_rows([w[k] for k in SMALL]), _pack_rows([small_g[k] for k in SMALL])[None],
                _pack_rows([m[k] for k in SMALL]), _pack_rows([v[k] for k in SMALL]), "adamw_small")
    shapes = [w[k].shape for k in SMALL]
    for dst, buf in zip((grads, delta, new_m, new_v), sm):
        for k, val in zip(SMALL, _unpack_rows(buf, shapes)):
            dst[k] = val
    return (loss, grad_x[None], *[grads[k] for k in WEIGHTS], *[delta[k] for k in WEIGHTS],
            *[new_m[k] for k in WEIGHTS], *[new_v[k] for k in WEIGHTS])
```

```python
import math

import jax
import jax.numpy as jnp
from jax import lax
from jax.experimental import pallas as pl
from jax.experimental.pallas import tpu as pltpu

F32 = jnp.float32
BF16 = jnp.bfloat16

CHUNK = 64
HGRN_HEAD_DIM = 128
CONV_WIDTH = 5
DN_ALPHA = 2.0 ** 0.25
LN_EPS = 1e-5
NORM_EPS = 1e-6
M_INIT = -1e30
NEG = -1e30
EXP_CLAMP = 80.0
ADAM_LR = 0.001
ADAM_B1 = 0.9
ADAM_B2 = 0.999
ADAM_EPS = 1e-08
ADAM_WD = 0.01
ADAM_STEP = 10
N_DEV = 8
LANES = 128
VMEM_LIMIT = 56 * 1024 * 1024
MESH = pl.DeviceIdType.MESH


def _params(*sem):
    return pltpu.CompilerParams(dimension_semantics=sem, vmem_limit_bytes=VMEM_LIMIT)


class _Job:
    def __init__(self, inputs, out_shape, sems, start, finish, middle=None):
        self.inputs, self.out_shape, self.sems = list(inputs), list(out_shape), list(sems)
        self.start, self.finish = start, finish
        self.middle = middle or (lambda ins, outs, sems: None)


def _pcall(body, *, name, out_shape, grid, in_specs, out_specs, sem, scratch_shapes=(), comm=None):
    single = not isinstance(out_shape, (list, tuple))
    out_shape = [out_shape] if single else list(out_shape)
    out_specs = [out_specs] if single else list(out_specs)
    in_specs, scratch_shapes = list(in_specs), list(scratch_shapes)
    if comm is None:
        call = pl.pallas_call(body, name=name, out_shape=out_shape, grid=grid, in_specs=in_specs,
                              out_specs=out_specs, scratch_shapes=scratch_shapes,
                              compiler_params=_params(*sem))

        def run(*args):
            outs = call(*args)
            return outs[0] if single else outs
        return run

    n_in, n_out, n_sc = len(in_specs), len(out_shape), len(scratch_shapes)
    c_in, c_out = len(comm.inputs), len(comm.out_shape)

    def hosted(*refs):
        ins, cins = refs[:n_in], refs[n_in:n_in + c_in]
        p = n_in + c_in
        outs, couts = refs[p:p + n_out], refs[p + n_out:p + n_out + c_out]
        p += n_out + c_out
        scratch, csems = refs[p:p + n_sc], refs[p + n_sc:]
        first = pl.program_id(0) == 0
        half = pl.program_id(0) == grid[0] // 2
        last = pl.program_id(0) == grid[0] - 1
        for d in range(1, len(grid)):
            first = first & (pl.program_id(d) == 0)
            half = half & (pl.program_id(d) == 0)
            last = last & (pl.program_id(d) == grid[d] - 1)

        @pl.when(first)
        def _():
            comm.start(cins, couts, csems)

        if grid[0] >= 2:
            @pl.when(half)
            def _():
                comm.middle(cins, couts, csems)

        body(*ins, *outs, *scratch)

        @pl.when(last)
        def _():
            if grid[0] < 2:
                comm.middle(cins, couts, csems)
            comm.finish(cins, couts, csems)

    any_spec = pl.BlockSpec(memory_space=pl.ANY)
    call = pl.pallas_call(
        hosted, name=name, out_shape=out_shape + comm.out_shape, grid=grid,
        in_specs=in_specs + [any_spec] * c_in, out_specs=out_specs + [any_spec] * c_out,
        scratch_shapes=scratch_shapes + comm.sems,
        compiler_params=_params(*(["arbitrary"] * len(grid))))

    def run(*args):
        res = call(*args, *comm.inputs)
        outs, couts = res[:n_out], list(res[n_out:])
        return (outs[0] if single else outs), couts
    return run


def _sigmoid(x):
    return 1.0 / (1.0 + jnp.exp(-x))


def _log_sigmoid(x):
    return jnp.minimum(x, 0.0) - jnp.log(1.0 + jnp.exp(-jnp.abs(x)))


def _dot(a, b, dims):
    return lax.dot_general(a.astype(BF16), b.astype(BF16), (dims, ((), ())),
                           preferred_element_type=F32)


def _dot3(a, b, dims):
    ah = a.astype(BF16)
    al = (a - ah.astype(F32)).astype(BF16)
    bh = b.astype(BF16)
    bl = (b - bh.astype(F32)).astype(BF16)
    d = (dims, ((), ()))
    out = lax.dot_general(ah, bh, d, preferred_element_type=F32)
    out = out + lax.dot_general(ah, bl, d, preferred_element_type=F32)
    return out + lax.dot_general(al, bh, d, preferred_element_type=F32)


_DIMS = {"nn": ((1,), (0,)), "nt": ((1,), (1,)), "tn": ((0,), (0,))}


def _dot_nn(a, b):
    return _dot3(a, b, _DIMS["nn"])


def _dot_nt(a, b):
    return _dot3(a, b, _DIMS["nt"])


def _dot_tn(a, b):
    return _dot3(a, b, _DIMS["tn"])


def _split3(x):
    hi = x.astype(BF16)
    r1 = x - hi.astype(F32)
    mid = r1.astype(BF16)
    lo = (r1 - mid.astype(F32)).astype(BF16)
    return hi, mid, lo


def _dot01(mask01, x, mode="nn"):
    m = mask01.astype(BF16)
    hi, mid, lo = _split3(x)
    d = (_DIMS[mode], ((), ()))
    out = lax.dot_general(m, hi, d, preferred_element_type=F32)
    out = out + lax.dot_general(m, mid, d, preferred_element_type=F32)
    return out + lax.dot_general(m, lo, d, preferred_element_type=F32)


def _matmul(a, b, *, mode="nn", out_dtype=F32, tm=512, tn=512, tk=None,
            add=None, scale=1.0, add_scale=1.0, name, comm=None, rows=None):
    if mode == "nn":
        (M, K), (K2, N) = a.shape, b.shape
        K2 = rows or K2
    elif mode == "nt":
        (M, K), (N, K2) = a.shape, b.shape
        N = rows or N
    else:
        (K, M), (K2, N) = a.shape, b.shape
    assert K == K2, (a.shape, b.shape, mode)
    tm, tn = min(tm, M), min(tn, N)
    tk = min(tk or K, K)
    assert M % tm == 0 and N % tn == 0 and K % tk == 0, (M, N, K, tm, tn, tk)
    nk = K // tk
    dims = _DIMS[mode]
    has_add = add is not None

    def body(*refs):
        if has_add:
            a_ref, b_ref, add_ref, o_ref = refs[:4]
        else:
            a_ref, b_ref, o_ref = refs[:3]
            add_ref = None
        acc_ref = refs[-1] if nk > 1 else None

        def finish(acc):
            out = acc if scale == 1.0 else acc * scale
            if has_add:
                out = out + add_ref[...].astype(F32) * add_scale
            o_ref[...] = out.astype(o_ref.dtype)

        if nk == 1:
            finish(_dot(a_ref[...], b_ref[...], dims))
        else:
            k = pl.program_id(2)

            @pl.when(k == 0)
            def _():
                acc_ref[...] = jnp.zeros_like(acc_ref)

            acc_ref[...] += _dot(a_ref[...], b_ref[...], dims)

            @pl.when(k == nk - 1)
            def _():
                finish(acc_ref[...])

    if mode == "tn":
        a_spec = pl.BlockSpec((tk, tm), lambda i, j, k: (k, i))
    else:
        a_spec = pl.BlockSpec((tm, tk), lambda i, j, k: (i, k))
    if mode == "nt":
        b_spec = pl.BlockSpec((tn, tk), lambda i, j, k: (j, k))
    else:
        b_spec = pl.BlockSpec((tk, tn), lambda i, j, k: (k, j))
    o_spec = pl.BlockSpec((tm, tn), lambda i, j, k: (i, j))
    in_specs = [a_spec, b_spec] + ([o_spec] if has_add else [])
    args = (a, b) + ((add,) if has_add else ())
    return _pcall(
        body, name=name, comm=comm,
        out_shape=jax.ShapeDtypeStruct((M, N), out_dtype),
        grid=(M // tm, N // tn, nk),
        in_specs=in_specs, out_specs=o_spec,
        scratch_shapes=[pltpu.VMEM((tm, tn), F32)] if nk > 1 else [],
        sem=("parallel", "parallel", "arbitrary"),
    )(*args)


def _ln_stats(x):
    mu = jnp.mean(x, axis=-1, keepdims=True)
    xc = x - mu
    var = jnp.mean(xc * xc, axis=-1, keepdims=True)
    rstd = lax.rsqrt(var + LN_EPS)
    return xc * rstd, rstd


def _ln_fwd(r, g, b, *, tm=256, name):
    T, D = r.shape
    tm = min(tm, T)

    def body(r_ref, g_ref, b_ref, y_ref, yb_ref):
        xhat, _ = _ln_stats(r_ref[...])
        y = xhat * g_ref[...] + b_ref[...]
        y_ref[...] = y
        yb_ref[...] = y.astype(BF16)

    row = pl.BlockSpec((tm, D), lambda i: (i, 0))
    vec = pl.BlockSpec((1, D), lambda i: (0, 0))
    return pl.pallas_call(
        body, name=name,
        out_shape=[jax.ShapeDtypeStruct((T, D), F32), jax.ShapeDtypeStruct((T, D), BF16)],
        grid=(T // tm,), in_specs=[row, vec, vec], out_specs=[row, row],
        compiler_params=_params("parallel"),
    )(r, g, b)


def _cast(x, *, tm=512, name, comm=None):
    T, D = x.shape
    tm = min(tm, T)

    def body(x_ref, xb_ref):
        xb_ref[...] = x_ref[...].astype(BF16)

    row = pl.BlockSpec((tm, D), lambda i: (i, 0))
    return _pcall(
        body, name=name, comm=comm, out_shape=jax.ShapeDtypeStruct((T, D), BF16),
        grid=(T // tm,), in_specs=[row], out_specs=row, sem=("parallel",),
    )(x)


def _ln_bwd(dy, r, g, *, tm=256, name, b=None, target=None):
    T, D = r.shape
    tm = min(tm, T)
    with_loss = target is not None

    def body(*refs):
        if with_loss:
            r_ref, g_ref, b_ref, t_ref, dr_ref, drb_ref, dg_ref, db_ref, loss_ref = refs
        else:
            dy_ref, r_ref, g_ref, dr_ref, drb_ref, dg_ref, db_ref = refs
        i = pl.program_id(0)
        xhat, rstd = _ln_stats(r_ref[...])
        gg = g_ref[...]
        if with_loss:
            err = xhat * gg + b_ref[...] - t_ref[...]
            dyv = err * (1.0 / D)
            part = jnp.sum(jnp.sum(err * err, axis=1, keepdims=True), axis=0, keepdims=True)
            part = jnp.broadcast_to(part * (0.5 / D), (1, LANES))
        else:
            dyv = dy_ref[...]
        dxh = dyv * gg
        m1 = jnp.mean(dxh, axis=-1, keepdims=True)
        m2 = jnp.mean(dxh * xhat, axis=-1, keepdims=True)
        dr = rstd * (dxh - m1 - xhat * m2)
        dr_ref[...] = dr
        drb_ref[...] = dr.astype(BF16)
        dgp = jnp.sum(dyv * xhat, axis=0, keepdims=True)
        dbp = jnp.sum(dyv, axis=0, keepdims=True)

        @pl.when(i == 0)
        def _():
            dg_ref[...] = dgp
            db_ref[...] = dbp
            if with_loss:
                loss_ref[...] = part

        @pl.when(i > 0)
        def _():
            dg_ref[...] += dgp
            db_ref[...] += dbp
            if with_loss:
                loss_ref[...] += part

    row = pl.BlockSpec((tm, D), lambda i: (i, 0))
    vec = pl.BlockSpec((1, D), lambda i: (0, 0))
    out_shape = [jax.ShapeDtypeStruct((T, D), F32), jax.ShapeDtypeStruct((T, D), BF16),
                 jax.ShapeDtypeStruct((1, D), F32), jax.ShapeDtypeStruct((1, D), F32)]
    out_specs = [row, row, vec, vec]
    if with_loss:
        in_specs, args = [row, vec, vec, row], (r, g, b, target)
        out_shape.append(jax.ShapeDtypeStruct((1, LANES), F32))
        out_specs.append(pl.BlockSpec((1, LANES), lambda i: (0, 0)))
    else:
        in_specs, args = [row, row, vec], (dy, r, g)
    return pl.pallas_call(
        body, name=name, out_shape=out_shape, grid=(T // tm,),
        in_specs=in_specs, out_specs=out_specs,
        compiler_params=_params("arbitrary"),
    )(*args)


def _ffn_up(xb, w1t, w3t, *, a=None, tm=1024, tf=512, name, comm=None):
    F, D = w3t.shape
    T = xb.shape[0]
    tm, tf = _pick(T, tm), _pick(F, tf)
    given = a is not None

    def body(x_ref, w1_ref, w3_ref, a_ref, b_ref, h_ref, ht_ref):
        xv = x_ref[...]
        av = w1_ref[...].astype(F32) if given else _dot(xv, w1_ref[...], _DIMS["nt"])
        b = _dot(xv, w3_ref[...], _DIMS["nt"])
        a_ref[...] = av.astype(BF16)
        b_ref[...] = b.astype(BF16)
        h = av * _sigmoid(av) * b
        h_ref[...] = h.astype(BF16)
        ht_ref[...] = h.T.astype(BF16)

    ij = (lambda p, q: (p, q)) if given else (lambda p, q: (q, p))
    wsp = pl.BlockSpec((tf, D), lambda p, q: (ij(p, q)[1], 0))
    osp = pl.BlockSpec((tm, tf), lambda p, q: ij(p, q))
    act = jax.ShapeDtypeStruct((T, F), BF16)
    return _pcall(
        body, name=name, comm=comm, out_shape=[act, act, act, jax.ShapeDtypeStruct((F, T), BF16)],
        grid=(T // tm, F // tf) if given else (F // tf, T // tm),
        in_specs=[pl.BlockSpec((tm, D), lambda p, q: (ij(p, q)[0], 0)), osp if given else wsp, wsp],
        out_specs=[osp, osp, osp, pl.BlockSpec((tf, tm), lambda p, q: ij(p, q)[::-1])],
        sem=("parallel", "parallel"),
    )(xb, a if given else w1t, w3t)


def _ffn_bwd_act(drb, w2, a, b, *, tm=1024, tf=512, name, comm=None):
    F, D = w2.shape
    T = drb.shape[0]
    tm, tf = _pick(T, tm), _pick(F, tf)

    def body(dr_ref, w2_ref, a_ref, b_ref, da_ref, db_ref, dat_ref, dbt_ref):
        d = 0.5 * _dot(dr_ref[...], w2_ref[...], _DIMS["nt"])
        av = a_ref[...].astype(F32)
        sig = _sigmoid(av)
        da = d * b_ref[...].astype(F32) * sig * (1.0 + av * (1.0 - sig))
        db = d * av * sig
        da_ref[...] = da.astype(BF16)
        db_ref[...] = db.astype(BF16)
        dat_ref[...] = da.T.astype(BF16)
        dbt_ref[...] = db.T.astype(BF16)

    asp = pl.BlockSpec((tm, tf), lambda i, j: (i, j))
    tsp = pl.BlockSpec((tf, tm), lambda i, j: (j, i))
    out, out_t = jax.ShapeDtypeStruct((T, F), BF16), jax.ShapeDtypeStruct((F, T), BF16)
    return _pcall(
        body, name=name, comm=comm, out_shape=[out, out, out_t, out_t], grid=(T // tm, F // tf),
        in_specs=[pl.BlockSpec((tm, D), lambda i, j: (i, 0)),
                  pl.BlockSpec((tf, D), lambda i, j: (j, 0)), asp, asp],
        out_specs=[asp, asp, tsp, tsp],
        sem=("parallel", "parallel"),
    )(drb, w2, a, b)


def _dz_pack(pieces, *, tm=256, name):
    groups = [p if isinstance(p, tuple) else (p,) for p in pieces]
    T = groups[0][0].shape[0]
    widths = [g[0].shape[1] for g in groups]
    W = sum(widths)
    tm = _pick(T, tm)
    flat = [a for g in groups for a in g]
    n = len(flat)

    def body(*refs):
        o_ref, ot_ref = refs[n], refs[n + 1]
        c, p = 0, 0
        for g, w in zip(groups, widths):
            v = refs[p][...].astype(F32)
            for r in refs[p + 1:p + len(g)]:
                v = v + r[...].astype(F32)
            p += len(g)
            o_ref[:, c:c + w] = v.astype(BF16)
            ot_ref[c:c + w, :] = v.T.astype(BF16)
            c += w

    return pl.pallas_call(
        body, name=name,
        out_shape=[jax.ShapeDtypeStruct((T, W), BF16), jax.ShapeDtypeStruct((W, T), BF16)],
        grid=(T // tm,),
        in_specs=[pl.BlockSpec((tm, a.shape[1]), lambda i: (i, 0)) for a in flat],
        out_specs=[pl.BlockSpec((tm, W), lambda i: (i, 0)), pl.BlockSpec((W, tm), lambda i: (0, i))],
        compiler_params=_params("parallel"),
    )(*flat)


def _ffn_dx(da, db, w1t, w3t, dr, *, tm=1024, tn=1024, tk=None, name, comm=None):
    T, F = da.shape
    D = w1t.shape[1]
    tm, tn = _pick(T, tm), _pick(D, tn)
    tk = _pick(F, tk or F // 4)
    nk = F // tk

    def body(da_ref, db_ref, w1_ref, w3_ref, dr_ref, o_ref, acc_ref):
        k = pl.program_id(2)

        @pl.when(k == 0)
        def _():
            acc_ref[...] = jnp.zeros_like(acc_ref)

        acc_ref[...] += (_dot(da_ref[...], w1_ref[...], _DIMS["nn"])
                         + _dot(db_ref[...], w3_ref[...], _DIMS["nn"]))

        @pl.when(k == nk - 1)
        def _():
            o_ref[...] = DN_ALPHA * dr_ref[...] + acc_ref[...]

    asp = pl.BlockSpec((tm, tk), lambda i, j, k: (i, k))
    wsp = pl.BlockSpec((tk, tn), lambda i, j, k: (k, j))
    osp = pl.BlockSpec((tm, tn), lambda i, j, k: (i, j))
    return _pcall(
        body, name=name, comm=comm, out_shape=jax.ShapeDtypeStruct((T, D), F32),
        grid=(T // tm, D // tn, nk), in_specs=[asp, asp, wsp, wsp, osp], out_specs=osp,
        scratch_shapes=[pltpu.VMEM((tm, tn), F32)],
        sem=("parallel", "parallel", "arbitrary"),
    )(da, db, w1t, w3t, dr)


def _chunk_mask(reverse, transpose=False):
    row = lax.broadcasted_iota(jnp.int32, (CHUNK, CHUNK), 0)
    col = lax.broadcasted_iota(jnp.int32, (CHUNK, CHUNK), 1)
    if reverse != transpose:
        return col >= row
    return col <= row


def _run_specs(specs, *, name, comm=None):
    counts = [(len(s["in_specs"]), len(s["out_specs"]), len(s["scratch_shapes"])) for s in specs]

    def body(*refs):
        p, parts = 0, [[], [], []]
        for kind in range(3):
            for c in counts:
                parts[kind].append(refs[p:p + c[kind]])
                p += c[kind]
        live = [s["body"](*parts[0][k], *parts[1][k], *parts[2][k]) for k, s in enumerate(specs)]
        while live:
            for g in list(live):
                if next(g, StopIteration) is StopIteration:
                    live.remove(g)

    cat = lambda key: [v for s in specs for v in s[key]]
    res = _pcall(body, name=name, comm=comm, out_shape=cat("out_shape"), grid=specs[0]["grid"],
                 in_specs=cat("in_specs"), out_specs=cat("out_specs"),
                 scratch_shapes=cat("scratch_shapes"), sem=("arbitrary",))(*cat("args"))
    outs, couts = res if comm is not None else (res, None)
    split, p = [], 0
    for c in counts:
        split.append(outs[p:p + c[1]])
        p += c[1]
    return (split, couts) if comm is not None else split


def _hgrn_pre(hq, hf, lb):
    sig = _sigmoid(hf)
    f = lb + (1.0 - lb) * sig
    q = hq * _sigmoid(hq) * (HGRN_HEAD_DIM ** -0.5)
    return q, f, sig


def _hgrn_decays(f, cmf, reverse):
    bc = _dot01(cmf, jnp.log(f))
    last = 0 if reverse else CHUNK - 1
    blast = bc[last:last + 1, :]
    bref = bc[CHUNK // 2:CHUNK // 2 + 1, :]
    eq = jnp.exp(jnp.minimum(bc - bref, EXP_CLAMP))
    ek = jnp.exp(jnp.minimum(bref - bc, EXP_CLAMP))
    return bc, blast, eq, ek


def _hgrn_fwd(z, lb, *, wh, reverse):
    T = z.shape[0]
    nch = T // CHUNK
    nh = wh // HGRN_HEAD_DIM
    hd = HGRN_HEAD_DIM

    def ci(i):
        return nch - 1 - i if reverse else i

    def body(hq_ref, hi_ref, hf_ref, lb_ref, o_ref, st_ref, s_ref):
        @pl.when(pl.program_id(0) == 0)
        def _():
            s_ref[...] = jnp.zeros_like(s_ref)

        cm = _chunk_mask(reverse)
        cmf = cm.astype(F32)
        H = range(nh)
        sls = [slice(h * hd, (h + 1) * hd) for h in H]
        q, f, _ = _hgrn_pre(hq_ref[...], hf_ref[...], lb_ref[...])
        v = hi_ref[...]
        k = 1.0 - f
        yield
        bc, blast, eq, ek = _hgrn_decays(f, cmf, reverse)
        yield
        qh, kh, qe = q * eq, k * ek, q * jnp.exp(bc)
        k2, eblast = k * jnp.exp(blast - bc), jnp.exp(blast)
        st = [s_ref[h] for h in H]
        yield
        att = [jnp.where(cm, _dot_nt(qh[:, sl], kh[:, sl]), 0.0) for sl in sls]
        yield
        inter = [_dot(qe[:, sls[h]], st[h], _DIMS["nt"]) for h in H]
        yield
        out = [inter[h] + _dot(att[h], v[:, sls[h]], _DIMS["nn"]) for h in H]
        yield
        snew = [eblast[:, sls[h]] * st[h] + _dot_tn(v[:, sls[h]], k2[:, sls[h]]) for h in H]
        yield
        for h in H:
            st_ref[0, h] = st[h]
            o_ref[:, sls[h]] = out[h]
            s_ref[h] = snew[h]

    blk = lambda c: pl.BlockSpec((CHUNK, wh), lambda i: (ci(i), c))
    return dict(
        body=body, grid=(nch,), args=[z, z, z, lb],
        out_shape=[jax.ShapeDtypeStruct((T, wh), F32),
                   jax.ShapeDtypeStruct((nch, nh, hd, hd), F32)],
        in_specs=[blk(0), blk(1), blk(3 + int(reverse)),
                  pl.BlockSpec((1, wh), lambda i: (0, 0))],
        out_specs=[blk(0), pl.BlockSpec((1, nh, hd, hd), lambda i: (ci(i), 0, 0, 0))],
        scratch_shapes=[pltpu.VMEM((nh, hd, hd), F32)])


def _hgrn_bwd(z, lb, do, states, *, wh, reverse):
    T = z.shape[0]
    nch = T // CHUNK
    nh = wh // HGRN_HEAD_DIM
    hd = HGRN_HEAD_DIM

    def ci(i):
        return i if reverse else nch - 1 - i

    def body(hq_ref, hi_ref, hf_ref, lb_ref, do_ref, st_ref,
             dhq_ref, dhi_ref, dhf_ref, dlb_ref, ds_ref, gs_ref):
        @pl.when(pl.program_id(0) == 0)
        def _():
            ds_ref[...] = jnp.zeros_like(ds_ref)
            gs_ref[...] = jnp.zeros_like(gs_ref)
            dlb_ref[...] = jnp.zeros_like(dlb_ref)

        cm = _chunk_mask(reverse)
        cmf = cm.astype(F32)
        cmtf = _chunk_mask(reverse, transpose=True).astype(F32)
        H = range(nh)
        sls = [slice(h * hd, (h + 1) * hd) for h in H]
        hq, lb = hq_ref[...], lb_ref[...]
        q, f, sig = _hgrn_pre(hq, hf_ref[...], lb)
        v, dov = hi_ref[...], do_ref[...]
        k = 1.0 - f
        yield
        bc, blast, eq, ek = _hgrn_decays(f, cmf, reverse)
        yield
        ebc, eb2, eblast = jnp.exp(bc), jnp.exp(blast - bc), jnp.exp(blast)
        qh, kh, qe, k2 = q * eq, k * ek, q * ebc, k * eb2
        st = [st_ref[0, h] for h in H]
        dst = [ds_ref[h] for h in H]
        yield
        att = [jnp.where(cm, _dot(qh[:, sl], kh[:, sl], _DIMS["nt"]), 0.0) for sl in sls]
        yield
        datt = [jnp.where(cm, _dot_nt(dov[:, sl], v[:, sl]), 0.0) for sl in sls]
        yield
        dq_a = [_dot_nn(datt[h], kh[:, sls[h]]) for h in H]
        yield
        dq_s = [_dot_nn(dov[:, sls[h]], st[h]) for h in H]
        yield
        dk_a = [_dot_tn(datt[h], qh[:, sls[h]]) for h in H]
        yield
        dk_s = [_dot_nn(v[:, sls[h]], dst[h]) for h in H]
        yield
        dv = [_dot(att[h], dov[:, sls[h]], _DIMS["tn"]) + _dot(k2[:, sls[h]], dst[h], _DIMS["nt"])
              for h in H]
        yield
        dsn = [eblast[:, sls[h]] * dst[h] + _dot_tn(dov[:, sls[h]], qe[:, sls[h]]) for h in H]
        for h in H:
            ds_ref[h] = dsn[h]
        yield
        dq = jnp.concatenate(dq_a, axis=1) * eq + ebc * jnp.concatenate(dq_s, axis=1)
        dk = jnp.concatenate(dk_a, axis=1) * ek + eb2 * jnp.concatenate(dk_s, axis=1)
        db = q * dq - k * dk
        yield
        dg = _dot01(cmtf, db) + gs_ref[...]
        gs_ref[...] += jnp.sum(db, axis=0, keepdims=True)
        yield
        df = dg / f - dk
        dhf_ref[...] = (df * (1.0 - lb) * sig * (1.0 - sig)).astype(BF16)
        dlb_ref[...] += jnp.sum(df * (1.0 - sig), axis=0, keepdims=True)
        sq = _sigmoid(hq)
        dhq_ref[...] = (dq * (HGRN_HEAD_DIM ** -0.5) * sq * (1.0 + hq * (1.0 - sq))).astype(BF16)
        dhi_ref[...] = jnp.concatenate(dv, axis=1).astype(BF16)

    blk = lambda c: pl.BlockSpec((CHUNK, wh), lambda i: (ci(i), c))
    vec = pl.BlockSpec((1, wh), lambda i: (0, 0))
    big = jax.ShapeDtypeStruct((T, wh), BF16)
    return dict(
        body=body, grid=(nch,), args=[z, z, z, lb, do, states],
        in_specs=[blk(0), blk(1), blk(3 + int(reverse)), vec, blk(0),
                  pl.BlockSpec((1, nh, hd, hd), lambda i: (ci(i), 0, 0, 0))],
        out_shape=[big, big, big, jax.ShapeDtypeStruct((1, wh), F32)],
        out_specs=[blk(0), blk(0), blk(0), vec],
        scratch_shapes=[pltpu.VMEM((nh, hd, hd), F32), pltpu.VMEM((1, wh), F32)])


def _mlstm_intra(cm, cmt, ig_row, ig_col, xf_row, xf_col, m_st):
    lf_row, lf_col = _log_sigmoid(xf_row), _log_sigmoid(xf_col)
    bcol = jnp.sum(jnp.where(cm, lf_row, 0.0), axis=1, keepdims=True)
    brow = jnp.sum(jnp.where(cmt, lf_col, 0.0), axis=0, keepdims=True)
    blast = jnp.sum(lf_row, axis=1, keepdims=True)
    dmat = jnp.where(cm, bcol - brow + ig_row, NEG)
    m_inter = bcol + m_st
    m_t = jnp.maximum(m_inter, jnp.max(dmat, axis=1, keepdims=True))
    p = jnp.exp(dmat - m_t)
    inter = jnp.exp(m_inter - m_t)
    w_col = blast - bcol + ig_col
    m_new = jnp.maximum(blast + m_st, jnp.max(w_col, axis=0, keepdims=True))
    cs = jnp.exp(blast + m_st - m_new)
    kscale = jnp.exp(w_col - m_new)
    return p, inter, m_t, m_new, cs, kscale


def _mlstm_specs(T, wm, nhm, reverse, backward):
    nch = T // CHUNK
    dm = wm // nhm
    ng = 4 * nhm

    def ci(i):
        fwd_order = nch - 1 - i if reverse else i
        return nch - 1 - fwd_order if backward else fwd_order

    row = lambda w, c: pl.BlockSpec((CHUNK, w), lambda i: (ci(i), c))
    gates_row = pl.BlockSpec((1, ng, CHUNK), lambda i: (ci(i), 0, 0))
    bias_row = pl.BlockSpec((1, LANES), lambda i: (0, 0))
    bias_col = pl.BlockSpec((ng, 1), lambda i: (0, 0))
    st_c = pl.BlockSpec((1, nhm, dm, dm), lambda i: (ci(i), 0, 0, 0))
    st_n = pl.BlockSpec((1, nhm, 1, dm), lambda i: (ci(i), 0, 0, 0))
    st_m = pl.BlockSpec((1, nhm, 1, LANES), lambda i: (ci(i), 0, 0, 0))
    return nch, dm, ng, ci, row, gates_row, bias_row, bias_col, st_c, st_n, st_m


def _mlstm_fwd(qk, z, gc, gr, br, bcl, *, wm, nhm, vcol, reverse):
    T = qk.shape[0]
    nch, dm, ng, ci, row, gates_row, bias_row, bias_col, st_c, st_n, st_m = _mlstm_specs(
        T, wm, nhm, reverse, False)
    d = int(reverse)

    def body(q_ref, k_ref, v_ref, gc_ref, gr_ref, br_ref, bc_ref,
             h_ref, cst_ref, nst_ref, mst_ref, c_ref, n_ref, m_ref):
        @pl.when(pl.program_id(0) == 0)
        def _():
            c_ref[...] = jnp.zeros_like(c_ref)
            n_ref[...] = jnp.zeros_like(n_ref)
            m_ref[...] = jnp.full_like(m_ref, M_INIT)

        cm = _chunk_mask(reverse)
        cmt = _chunk_mask(reverse, transpose=True)
        G = gc_ref[...] + br_ref[...]
        Gr = gr_ref[0] + bc_ref[...]
        H = range(nhm)
        sls = [slice(h * dm, (h + 1) * dm) for h in H]
        m_all = [m_ref[h] for h in H]
        intra = [_mlstm_intra(cm, cmt, Gr[d * nhm + h:d * nhm + h + 1, :], G[:, d * nhm + h:d * nhm + h + 1],
                              Gr[2 * nhm + d * nhm + h:2 * nhm + d * nhm + h + 1, :],
                              G[:, 2 * nhm + d * nhm + h:2 * nhm + d * nhm + h + 1], m_all[h][:, 0:1])
                 for h in H]
        p, inter, m_t, m_new, cs, kscale = zip(*intra)
        yield
        q = [q_ref[:, sl] * (dm ** -0.5) for sl in sls]
        k = [k_ref[:, sl] for sl in sls]
        v = [v_ref[:, sl] for sl in sls]
        ct = [c_ref[h] for h in H]
        n = [n_ref[h] for h in H]
        yield
        sc = [_dot_nt(q[h], k[h]) * p[h] for h in H]
        yield
        qc = [_dot(q[h], ct[h], _DIMS["nt"]) for h in H]
        yield
        num = [_dot(sc[h], v[h], _DIMS["nn"]) + inter[h] * qc[h] for h in H]
        yield
        den = [jnp.sum(sc[h], axis=1, keepdims=True) + inter[h] * jnp.sum(q[h] * n[h], axis=1, keepdims=True)
               for h in H]
        kw = [k[h] * kscale[h] for h in H]
        yield
        cnew = [cs[h] * ct[h] + _dot_tn(v[h], kw[h]) for h in H]
        yield
        for h in H:
            cst_ref[0, h] = ct[h]
            nst_ref[0, h] = n[h]
            mst_ref[0, h] = m_all[h]
            h_ref[:, sls[h]] = num[h] / jnp.maximum(jnp.abs(den[h]), jnp.exp(-m_t[h]))
            c_ref[h] = cnew[h]
            n_ref[h] = cs[h] * n[h] + jnp.sum(kw[h], axis=0, keepdims=True)
            m_ref[h] = jnp.broadcast_to(m_new[h], (1, LANES))

    return dict(
        body=body, grid=(nch,), args=[qk, qk, z, gc, gr, br, bcl],
        out_shape=[jax.ShapeDtypeStruct((T, wm), F32),
                   jax.ShapeDtypeStruct((nch, nhm, dm, dm), F32),
                   jax.ShapeDtypeStruct((nch, nhm, 1, dm), F32),
                   jax.ShapeDtypeStruct((nch, nhm, 1, LANES), F32)],
        in_specs=[row(wm, 0), row(wm, 1), row(wm, vcol), row(LANES, 0), gates_row, bias_row, bias_col],
        out_specs=[row(wm, 0), st_c, st_n, st_m],
        scratch_shapes=[pltpu.VMEM((nhm, dm, dm), F32), pltpu.VMEM((nhm, 1, dm), F32),
                        pltpu.VMEM((nhm, 1, LANES), F32)])


def _mlstm_bwd(qk, z, gc, gr, br, bcl, dh, states, *, wm, nhm, vcol, reverse):
    T = qk.shape[0]
    nch, dm, ng, ci, row, gates_row, bias_row, bias_col, st_c, st_n, st_m = _mlstm_specs(
        T, wm, nhm, reverse, True)
    d = int(reverse)

    stage_refs = []

    def body(q_ref, k_ref, v_ref, gc_ref, gr_ref, br_ref, bc_ref, dh_ref, cst_ref, nst_ref, mst_ref,
             dqk_ref, dv_ref, dgr_ref, dgs_ref, e_ref, en_ref, fs_ref):
        @pl.when(pl.program_id(0) == 0)
        def _():
            e_ref[...] = jnp.zeros_like(e_ref)
            en_ref[...] = jnp.zeros_like(en_ref)
            fs_ref[...] = jnp.zeros_like(fs_ref)
            dgs_ref[...] = jnp.zeros_like(dgs_ref)

        cm = _chunk_mask(reverse)
        cmt = _chunk_mask(reverse, transpose=True)
        row_i = lax.broadcasted_iota(jnp.int32, (CHUNK, CHUNK), 0)
        col_i = lax.broadcasted_iota(jnp.int32, (CHUNK, CHUNK), 1)
        eye = row_i == col_i
        G = gc_ref[...] + br_ref[...]
        Gr = gr_ref[0] + bc_ref[...]
        stage_refs[:] = (q_ref, k_ref, v_ref, gc_ref, gr_ref, br_ref, bc_ref, dh_ref, cst_ref, nst_ref,
                         mst_ref, dqk_ref, dv_ref, dgr_ref, dgs_ref, e_ref, en_ref, fs_ref)
        group = max(1, nhm // 2)
        for h0 in range(0, nhm, group):
            yield from stages(range(h0, h0 + group), cm, cmt, eye, G, Gr)

    def stages(H, cm, cmt, eye, G, Gr):
        (q_ref, k_ref, v_ref, gc_ref, gr_ref, br_ref, bc_ref, dh_ref, cst_ref, nst_ref, mst_ref,
         dqk_ref, dv_ref, dgr_ref, dgs_ref, e_ref, en_ref, fs_ref) = stage_refs
        sls = {h: slice(h * dm, (h + 1) * dm) for h in H}
        xf_row = {h: Gr[2 * nhm + d * nhm + h:2 * nhm + d * nhm + h + 1, :] for h in H}
        intra = {h: _mlstm_intra(cm, cmt, Gr[d * nhm + h:d * nhm + h + 1, :], G[:, d * nhm + h:d * nhm + h + 1],
                                 xf_row[h], G[:, 2 * nhm + d * nhm + h:2 * nhm + d * nhm + h + 1],
                                 mst_ref[0, h][:, 0:1]) for h in H}
        p, inter, m_t, cs, kscale = ({h: intra[h][i] for h in H} for i in (0, 1, 2, 4, 5))
        yield
        q = {h: q_ref[:, sls[h]] * (dm ** -0.5) for h in H}
        k = {h: k_ref[:, sls[h]] for h in H}
        v = {h: v_ref[:, sls[h]] for h in H}
        dhv = {h: dh_ref[:, sls[h]] for h in H}
        ct = {h: cst_ref[0, h] for h in H}
        n = {h: nst_ref[0, h] for h in H}
        et = {h: e_ref[h] for h in H}
        en = {h: en_ref[h] for h in H}
        carry = {h: fs_ref[h][:, 0:1] for h in H}
        yield
        sc = {h: _dot_nt(q[h], k[h]) * p[h] for h in H}
        yield
        qc = {h: _dot_nt(q[h], ct[h]) for h in H}
        yield
        num = {h: _dot_nn(sc[h], v[h]) + inter[h] * qc[h] for h in H}
        yield
        den = {h: jnp.sum(sc[h], axis=1, keepdims=True) + inter[h] * jnp.sum(q[h] * n[h], axis=1, keepdims=True)
               for h in H}
        floor = {h: jnp.exp(-m_t[h]) for h in H}
        nstab = {h: jnp.maximum(jnp.abs(den[h]), floor[h]) for h in H}
        gh = {h: dhv[h] / nstab[h] for h in H}
        dd = {h: -jnp.sum(dhv[h] * (num[h] / nstab[h]), axis=1, keepdims=True) / nstab[h]
              * jnp.where(jnp.abs(den[h]) > floor[h], jnp.sign(den[h]), 0.0) for h in H}
        yield
        dsqk = {h: (_dot_nt(gh[h], v[h]) + dd[h]) * p[h] for h in H}
        qi = {h: q[h] * inter[h] for h in H}
        kw = {h: k[h] * kscale[h] for h in H}
        yield
        ghc = {h: _dot_nn(gh[h], ct[h]) for h in H}
        yield
        vet = {h: _dot_nn(v[h], et[h]) for h in H}
        yield
        dq = {h: _dot_nn(dsqk[h], k[h]) + inter[h] * (ghc[h] + dd[h] * n[h]) for h in H}
        yield
        dk = {h: _dot_tn(dsqk[h], q[h]) + kscale[h] * (vet[h] + en[h]) for h in H}
        yield
        dv = {h: _dot(sc[h], gh[h], _DIMS["tn"]) + _dot(kw[h], et[h], _DIMS["nt"]) for h in H}
        yield
        e_new = {h: cs[h] * et[h] + _dot_tn(gh[h], qi[h]) for h in H}
        en_new = {h: cs[h] * en[h] + jnp.sum(qi[h] * dd[h], axis=0, keepdims=True) for h in H}
        yield
        di_col = {h: jnp.sum(k[h] * dk[h], axis=1, keepdims=True) for h in H}
        df_col = {h: jnp.sum(q[h] * dq[h], axis=1, keepdims=True) - di_col[h] for h in H}
        di_row = {h: jnp.sum(jnp.where(eye, di_col[h], 0.0), axis=0, keepdims=True) for h in H}
        dlf_row = {h: jnp.sum(jnp.where(cm, df_col[h], 0.0), axis=0, keepdims=True) + carry[h] for h in H}
        dxf_row = {h: dlf_row[h] * (1.0 - _sigmoid(xf_row[h])) for h in H}
        yield
        for h in H:
            sl = sls[h]
            slk = slice(wm + h * dm, wm + (h + 1) * dm)
            e_ref[h] = e_new[h]
            en_ref[h] = en_new[h]
            fs_ref[h] = jnp.broadcast_to(carry[h] + jnp.sum(df_col[h], axis=0, keepdims=True), (1, LANES))
            dgr_ref[0, h:h + 1, :] = di_row[h]
            dgr_ref[0, nhm + h:nhm + h + 1, :] = dxf_row[h]
            dgs_ref[h:h + 1, :] += jnp.broadcast_to(jnp.sum(di_row[h], axis=1, keepdims=True), (1, LANES))
            dgs_ref[nhm + h:nhm + h + 1, :] += jnp.broadcast_to(
                jnp.sum(dxf_row[h], axis=1, keepdims=True), (1, LANES))
            dqk_ref[:, sl] = (dq[h] * (dm ** -0.5)).astype(BF16)
            dqk_ref[:, slk] = dk[h].astype(BF16)
            dv_ref[:, sl] = dv[h].astype(BF16)

    return dict(
        body=body, grid=(nch,), args=[qk, qk, z, gc, gr, br, bcl, dh] + list(states),
        in_specs=[row(wm, 0), row(wm, 1), row(wm, vcol), row(LANES, 0), gates_row, bias_row, bias_col,
                  row(wm, 0), st_c, st_n, st_m],
        out_shape=[jax.ShapeDtypeStruct((T, 2 * wm), BF16), jax.ShapeDtypeStruct((T, wm), BF16),
                   jax.ShapeDtypeStruct((nch, 2 * nhm, CHUNK), F32),
                   jax.ShapeDtypeStruct((2 * nhm, LANES), F32)],
        out_specs=[row(2 * wm, 0), row(wm, 0),
                   pl.BlockSpec((1, 2 * nhm, CHUNK), lambda i: (ci(i), 0, 0)),
                   pl.BlockSpec((2 * nhm, LANES), lambda i: (0, 0))],
        scratch_shapes=[pltpu.VMEM((nhm, dm, dm), F32), pltpu.VMEM((nhm, 1, dm), F32),
                        pltpu.VMEM((nhm, 1, LANES), F32)])


def _conv_taps(x, w_ref):
    T = x.shape[0]
    t = lax.broadcasted_iota(jnp.int32, x.shape, 0)
    taps = []
    acc = None
    for j in range(CONV_WIDTH):
        s = CONV_WIDTH // 2 - j
        if s == 0:
            xs = x
        else:
            xs = jnp.where((t - s >= 0) & (t - s < T), pltpu.roll(x, s % T, 0), 0.0)
        taps.append(xs)
        term = w_ref[j:j + 1, :] * xs
        acc = term if acc is None else acc + term
    return taps, acc


def _conv_fwd(z, w, b, *, col0, tc=LANES, name):
    T = z.shape[0]
    C2 = w.shape[1]
    assert col0 % tc == 0 and C2 % tc == 0

    def body(z_ref, w_ref, b_ref, o_ref):
        _, acc = _conv_taps(z_ref[...], w_ref)
        c = acc + b_ref[...]
        o_ref[...] = c * _sigmoid(c)

    return pl.pallas_call(
        body, name=name, out_shape=jax.ShapeDtypeStruct((T, C2), F32),
        grid=(C2 // tc,),
        in_specs=[pl.BlockSpec((T, tc), lambda j: (0, col0 // tc + j)),
                  pl.BlockSpec((CONV_WIDTH, tc), lambda j: (0, j)),
                  pl.BlockSpec((1, tc), lambda j: (0, j))],
        out_specs=pl.BlockSpec((T, tc), lambda j: (0, j)),
        compiler_params=_params("parallel"),
    )(z, w, b)


def _conv_bwd(dy_fw, dy_bw, z, w, b, *, col0, tc=LANES, name):
    T = z.shape[0]
    C2 = w.shape[1]

    def body(dyf_ref, dyb_ref, z_ref, w_ref, b_ref, dx_ref, dw_ref, db_ref):
        taps, acc = _conv_taps(z_ref[...], w_ref)
        c = acc + b_ref[...]
        sg = _sigmoid(c)
        dc = (dyf_ref[...].astype(F32) + dyb_ref[...].astype(F32)) * sg * (1.0 + c * (1.0 - sg))
        t = lax.broadcasted_iota(jnp.int32, dc.shape, 0)
        dx = None
        for j in range(CONV_WIDTH):
            s = j - CONV_WIDTH // 2
            if s == 0:
                ds = dc
            else:
                ds = jnp.where((t - s >= 0) & (t - s < T), pltpu.roll(dc, s % T, 0), 0.0)
            term = w_ref[j:j + 1, :] * ds
            dx = term if dx is None else dx + term
            dw_ref[j:j + 1, :] = jnp.sum(dc * taps[j], axis=0, keepdims=True)
        dx_ref[...] = dx.astype(BF16)
        db_ref[...] = jnp.sum(dc, axis=0, keepdims=True)

    col = pl.BlockSpec((T, tc), lambda j: (0, j))
    wsp = pl.BlockSpec((CONV_WIDTH, tc), lambda j: (0, j))
    bsp = pl.BlockSpec((1, tc), lambda j: (0, j))
    return pl.pallas_call(
        body, name=name,
        out_shape=[jax.ShapeDtypeStruct((T, C2), BF16), jax.ShapeDtypeStruct((CONV_WIDTH, C2), F32),
                   jax.ShapeDtypeStruct((1, C2), F32)],
        grid=(C2 // tc,),
        in_specs=[col, col, pl.BlockSpec((T, tc), lambda j: (0, col0 // tc + j)), wsp, bsp],
        out_specs=[col, wsp, bsp],
        compiler_params=_params("parallel"),
    )(dy_fw, dy_bw, z, w, b)


def _mix_heads(o_fw, o_bw, h_fw, h_bw, wh, wm, nhm):
    out = []
    hd = HGRN_HEAD_DIM
    for h in range(wh // hd):
        sl = slice(h * hd, (h + 1) * hd)
        o = o_fw[:, sl] + o_bw[:, sl]
        r = lax.rsqrt(jnp.mean(o * o, axis=-1, keepdims=True) + NORM_EPS)
        out.append((0, sl, o * r, r))
    dm = wm // nhm
    for h in range(nhm):
        sl = slice(h * dm, (h + 1) * dm)
        x = h_fw[:, sl] + h_bw[:, sl]
        xc = x - jnp.mean(x, axis=-1, keepdims=True)
        r = lax.rsqrt(jnp.mean(xc * xc, axis=-1, keepdims=True) + NORM_EPS)
        out.append((1, sl, xc * r, r))
    return out


def _mix_specs(T, wh, wm, tm, gcol, ocol):
    rowh = pl.BlockSpec((tm, wh), lambda i: (i, 0))
    rowm = pl.BlockSpec((tm, wm), lambda i: (i, 0))
    hg = pl.BlockSpec((tm, wh), lambda i: (i, gcol))
    mo = pl.BlockSpec((tm, wm), lambda i: (i, ocol))
    vh = pl.BlockSpec((1, wh), lambda i: (0, 0))
    vm = pl.BlockSpec((1, wm), lambda i: (0, 0))
    return rowh, rowm, hg, mo, vh, vm


def _mix_fwd(o_fw, o_bw, h_fw, h_bw, z, gh, gm, *, nhm, gcol, ocol, tm=256, name):
    T, wh = o_fw.shape
    wm = h_fw.shape[1]
    tm = min(tm, T)
    rowh, rowm, hg, mo, vh, vm = _mix_specs(T, wh, wm, tm, gcol, ocol)

    def body(of_ref, ob_ref, hf_ref, hb_ref, hg_ref, mo_ref, gh_ref, gm_ref, y_ref, yt_ref):
        heads = _mix_heads(of_ref[...], ob_ref[...], hf_ref[...], hb_ref[...], wh, wm, nhm)
        for grp, sl, nv, _ in heads:
            if grp == 0:
                gate = hg_ref[:, sl]
                gate = gate * _sigmoid(gate)
                y = nv * gh_ref[:, sl] * gate
                osl = sl
            else:
                y = nv * gm_ref[:, sl] * _sigmoid(mo_ref[:, sl])
                osl = slice(wh + sl.start, wh + sl.stop)
            y_ref[:, osl] = y.astype(BF16)
            yt_ref[osl, :] = y.T.astype(BF16)

    return pl.pallas_call(
        body, name=name,
        out_shape=[jax.ShapeDtypeStruct((T, wh + wm), BF16), jax.ShapeDtypeStruct((wh + wm, T), BF16)],
        grid=(T // tm,),
        in_specs=[rowh, rowh, rowm, rowm, hg, mo, vh, vm],
        out_specs=[pl.BlockSpec((tm, wh + wm), lambda i: (i, 0)),
                   pl.BlockSpec((wh + wm, tm), lambda i: (0, i))],
        compiler_params=_params("parallel"),
    )(o_fw, o_bw, h_fw, h_bw, z, z, gh, gm)


def _mix_bwd(dy, o_fw, o_bw, h_fw, h_bw, z, gh, gm, *, nhm, gcol, ocol, tm=256, name, comm=None):
    T, wh = o_fw.shape
    wm = h_fw.shape[1]
    tm = min(tm, T)
    rowh, rowm, hg, mo, vh, vm = _mix_specs(T, wh, wm, tm, gcol, ocol)

    def body(dy_ref, of_ref, ob_ref, hf_ref, hb_ref, hg_ref, mo_ref, gh_ref, gm_ref,
             do_ref, dh_ref, dhg_ref, dmo_ref, dgh_ref, dgm_ref):
        @pl.when(pl.program_id(0) == 0)
        def _():
            dgh_ref[...] = jnp.zeros_like(dgh_ref)
            dgm_ref[...] = jnp.zeros_like(dgm_ref)

        heads = _mix_heads(of_ref[...], ob_ref[...], hf_ref[...], hb_ref[...], wh, wm, nhm)
        for grp, sl, nv, r in heads:
            if grp == 0:
                d = dy_ref[:, sl]
                x = hg_ref[:, sl]
                sg = _sigmoid(x)
                gate = x * sg
                g = gh_ref[:, sl]
                dgh_ref[:, sl] += jnp.sum(d * nv * gate, axis=0, keepdims=True)
                dhg_ref[:, sl] = (d * nv * g * sg * (1.0 + x * (1.0 - sg))).astype(BF16)
                t = d * g * gate
                do_ref[:, sl] = r * (t - nv * jnp.mean(t * nv, axis=-1, keepdims=True))
            else:
                d = dy_ref[:, slice(wh + sl.start, wh + sl.stop)]
                sg = _sigmoid(mo_ref[:, sl])
                g = gm_ref[:, sl]
                dgm_ref[:, sl] += jnp.sum(d * nv * sg, axis=0, keepdims=True)
                dmo_ref[:, sl] = (d * nv * g * sg * (1.0 - sg)).astype(BF16)
                t = d * g * sg
                dh_ref[:, sl] = r * (t - jnp.mean(t, axis=-1, keepdims=True)
                                     - nv * jnp.mean(t * nv, axis=-1, keepdims=True))

    return _pcall(
        body, name=name, comm=comm,
        out_shape=[jax.ShapeDtypeStruct((T, wh), F32), jax.ShapeDtypeStruct((T, wm), F32),
                   jax.ShapeDtypeStruct((T, wh), BF16), jax.ShapeDtypeStruct((T, wm), BF16),
                   jax.ShapeDtypeStruct((1, wh), F32), jax.ShapeDtypeStruct((1, wm), F32)],
        grid=(T // tm,),
        in_specs=[pl.BlockSpec((tm, wh + wm), lambda i: (i, 0)), rowh, rowh, rowm, rowm, hg, mo, vh, vm],
        out_specs=[rowh, rowm, rowh, rowm, vh, vm],
        sem=("arbitrary",),
    )(dy, o_fw, o_bw, h_fw, h_bw, z, z, gh, gm)


def _pick(n, pref):
    for c in range(pref - pref % LANES, 0, -LANES):
        if n % c == 0:
            return c
    return n


def _mm(a, b, mode="nn", *, tm=1024, tn=1024, tk=2048, **kw):
    if mode == "nn":
        (M, K), N = a.shape, b.shape[1]
    else:
        (M, K), N = a.shape, kw.get("rows") or b.shape[0]
    return _matmul(a, b, mode=mode, tm=_pick(M, tm), tn=_pick(N, tn), tk=_pick(K, tk), **kw)


def _device_step(x, target, sh, small_parts, small_params, core):
    T, D = x.shape

    def full(g):
        return g.reshape(-1, D)

    def shards(g):
        return g.reshape(N_DEV, -1, D)

    def pair_add(name, g, land):
        return _pair_add(g, land, core, "rs_add_" + name)

    def dw(lhs_t, rhs, name, scale=1.0):
        return shards(_mm(lhs_t, rhs, out_dtype=BF16, scale=scale, tm=704, tk=T, name=name))

    xb, (g_w1a, g_small) = _cast(x, name="x_cast", comm=_ag_job([sh["ffn1_w1"], small_parts]))
    w1a = full(g_w1a)
    P = small_params(g_small)
    wh = P["hgrn_norm_g"].shape[1]
    wm = P["mlstm_norm_g"].shape[1]
    nhm = P["ig_b"].shape[1]
    ng = 4 * nhm
    nch = T // CHUNK
    assert wh == wm and T % CHUNK == 0
    vcol, ocol, gcol = 7, 8, 2
    col0 = 5 * wh
    nz = 5 * wh + 4 * wm
    c1 = sh["w_in"].shape[0] // 4 // 16 * 16
    a1f, (g_w3a, g_wp1) = _mm(xb, w1a, "nt", out_dtype=BF16, name="ffn1_up_a",
                              comm=_ag_job([sh["ffn1_w3"], sh["w_in"][:c1]]))
    w3a = full(g_w3a)
    (a1, b1, h1, h1t), (g_w2a, g_wp2) = _ffn_up(xb, None, w3a, a=a1f, name="ffn1_up_b",
                                                comm=_ag_job([sh["ffn1_w2"], sh["w_in"][c1:2 * c1]]))
    w2a = full(g_w2a)
    r1, (g_wp3,) = _mm(h1, w2a, add=x, scale=0.5, add_scale=DN_ALPHA, tk=w2a.shape[0] // 4,
                       name="ffn1_down", comm=_ag_job([sh["w_in"][2 * c1:]]))
    w_int = full(jnp.concatenate([g_wp1, g_wp2, g_wp3], axis=1))
    w_wgt = jnp.zeros((LANES, D), BF16).at[:ng].set(w_int[nz:])
    x1, x1b = _ln_fwd(r1, P["ln1_g"], P["ln1_b"], name="ln1")
    zm, (g_wout, g_w3b) = _mm(x1b, w_int, "nt", rows=nz, name="zm",
                              comm=_ag_job([sh["w_out"], sh["ffn2_w3"]]))
    w_wout = full(g_wout)
    zg = _mm(x1b, w_wgt, "nt", name="zg")
    gr = zg[:, :ng].reshape(nch, CHUNK, ng).transpose(0, 2, 1)
    bias = jnp.concatenate([P["ig_b"].reshape(-1), P["fg_b"].reshape(-1)])
    br = jnp.zeros((1, LANES), F32).at[0, :ng].set(bias)
    bcl = bias.reshape(ng, 1)
    lbf, lbb = P["lb"][0:1], P["lb"][1:2]
    ((o_fw, s_fw), (o_bw, s_bw)), (g_w1b,) = _run_specs(
        [_hgrn_fwd(zm, lbf, wh=wh, reverse=False), _hgrn_fwd(zm, lbb, wh=wh, reverse=True)],
        name="hgrn_fwd", comm=_ag_job([sh["ffn2_w1"]]))
    qk = _conv_fwd(zm, P["conv_w"], P["conv_b"], col0=col0, name="conv")
    mkw = dict(wm=wm, nhm=nhm, vcol=vcol)
    ((h_fw, *st_fw),) = _run_specs([_mlstm_fwd(qk, zm, zg, gr, br, bcl, reverse=False, **mkw)],
                                   name="mlstm_fw")
    ((h_bw, *st_bw),) = _run_specs([_mlstm_fwd(qk, zm, zg, gr, br, bcl, reverse=True, **mkw)],
                                   name="mlstm_bw")
    w1b, w3b = full(g_w1b), full(g_w3b)
    y, yt = _mix_fwd(o_fw, o_bw, h_fw, h_bw, zm, P["hgrn_norm_g"], P["mlstm_norm_g"],
                     nhm=nhm, gcol=gcol, ocol=ocol, name="mix")
    r2 = _mm(y, w_wout, add=x1, add_scale=DN_ALPHA, name="r2")
    x2, x2b = _ln_fwd(r2, P["ln2_g"], P["ln2_b"], name="ln2")
    (a2, b2, h2, h2t), (g_w2b,) = _ffn_up(x2b, w1b, w3b, name="ffn2_up", comm=_ag_job([sh["ffn2_w2"]]))
    w2b = full(g_w2b)
    r3 = _mm(h2, w2b, add=x2, scale=0.5, add_scale=DN_ALPHA, tk=w2b.shape[0] // 4, name="ffn2_down")

    G, land = {}, {}
    dr3, dr3b, G["ln3_g"], G["ln3_b"], loss = _ln_bwd(
        None, r3, P["ln3_g"], b=P["ln3_b"], target=target, name="ln3_bwd")
    da2, db2, da2t, db2t = _ffn_bwd_act(dr3b, w2b, a2, b2, name="ffn2_dact")
    gw1b, gw3b = dw(da2t, x2b, "ffn2_dw1"), dw(db2t, x2b, "ffn2_dw3")
    gw2b = dw(h2t, dr3b, "ffn2_dw2", scale=0.5)
    dx2, l1 = _ffn_dx(da2, db2, w1b, w3b, dr3, name="ffn2_dx", comm=_pair_job([gw1b, gw3b, gw2b]))
    p_w1b, p_w3b, p_w2b = [pair_add(k, g, l) for k, g, l in
                           zip(("ffn2_w1", "ffn2_w3", "ffn2_w2"), (gw1b, gw3b, gw2b), l1)]
    dr2, dr2b, G["ln2_g"], G["ln2_b"] = _ln_bwd(dx2, r2, P["ln2_g"], name="ln2_bwd")
    dy = _mm(dr2b, w_wout, "nt", name="dy")
    gwout = shards(_mm(yt, dr2b, out_dtype=BF16, name="dwout"))
    (do, dh, dhg, dmo, G["hgrn_norm_g"], G["mlstm_norm_g"]), (l1,) = _mix_bwd(
        dy, o_fw, o_bw, h_fw, h_bw, zm, P["hgrn_norm_g"], P["mlstm_norm_g"],
        nhm=nhm, gcol=gcol, ocol=ocol, name="mix_bwd", comm=_pair_job([gwout]))
    p_wout = pair_add("w_out", gwout, l1)
    ((dqk_f, dv_f, dgr_f, dgs_f),), (land["ffn2_w1"],) = _run_specs(
        [_mlstm_bwd(qk, zm, zg, gr, br, bcl, dh, st_fw, reverse=False, **mkw)],
        name="mlstm_fw_bwd", comm=_chips_job([p_w1b]))
    ((dqk_b, dv_b, dgr_b, dgs_b),), (land["ffn2_w3"],) = _run_specs(
        [_mlstm_bwd(qk, zm, zg, gr, br, bcl, dh, st_bw, reverse=True, **mkw)],
        name="mlstm_bw_bwd", comm=_chips_job([p_w3b]))
    dmqk, G["conv_w"], G["conv_b"] = _conv_bwd(dqk_f, dqk_b, zm, P["conv_w"], P["conv_b"], col0=col0,
                                               name="conv_bwd")
    ((dhq_f, dhi_f, dhf_fw, dlb_f), (dhq_b, dhi_b, dhf_bw, dlb_b)), (land["ffn2_w2"], land["w_out"]) = \
        _run_specs([_hgrn_bwd(zm, lbf, do, s_fw, wh=wh, reverse=False),
                    _hgrn_bwd(zm, lbb, do, s_bw, wh=wh, reverse=True)],
                   name="hgrn_bwd", comm=_chips_job([p_w2b, p_wout]))
    G["lb"] = jnp.concatenate([dlb_f, dlb_b], axis=0)
    G["ig_b"] = jnp.stack([dgs_f[:nhm, 0], dgs_b[:nhm, 0]])
    G["fg_b"] = jnp.stack([dgs_f[nhm:, 0], dgs_b[nhm:, 0]])
    dzm, dzmt = _dz_pack([(dhq_f, dhq_b), (dhi_f, dhi_b), dhg, dhf_fw, dhf_bw, dmqk, (dv_f, dv_b), dmo],
                         name="dz_pack")
    dgr = jnp.concatenate([dgr_f[:, :nhm], dgr_b[:, :nhm], dgr_f[:, nhm:], dgr_b[:, nhm:]], axis=1)
    dzgt = jnp.zeros((LANES, T), F32).at[:ng].set(dgr.transpose(1, 0, 2).reshape(ng, T)).astype(BF16)
    dzg = jnp.zeros((T, LANES), F32).at[:, :ng].set(dgr.transpose(0, 2, 1).reshape(T, ng)).astype(BF16)
    gwin = shards(jnp.concatenate([_mm(dzmt, x1b, out_dtype=BF16, tk=T, name="dwin"),
                                   _mm(dzgt, x1b, out_dtype=BF16, tk=T, name="dwg")[:ng]], axis=0))
    t, (l1,) = _mm(dzg, w_wgt, add=dr2, add_scale=DN_ALPHA, name="dx1_g", comm=_pair_job([gwin]))
    p_win = pair_add("w_in", gwin, l1)
    dx1, (land["w_in"],) = _mm(dzm, w_int, rows=nz, add=t, tk=1536, name="dx1", comm=_chips_job([p_win]))
    dr1, dr1b, G["ln1_g"], G["ln1_b"] = _ln_bwd(dx1, r1, P["ln1_g"], name="ln1_bwd")
    gw2a = dw(h1t, dr1b, "ffn1_dw2", scale=0.5)
    (da1, db1, da1t, db1t), (l1,) = _ffn_bwd_act(dr1b, w2a, a1, b1, name="ffn1_dact",
                                                 comm=_pair_job([gw2a]))
    p_w2a = pair_add("ffn1_w2", gw2a, l1)
    gw1a, (land["ffn1_w2"],) = _mm(da1t, xb, out_dtype=BF16, tm=704, tk=T, name="ffn1_dw1",
                                   comm=_chips_job([p_w2a]))
    gw1a = shards(gw1a)
    gw3a, (l1,) = _mm(db1t, xb, out_dtype=BF16, tm=704, tk=T, name="ffn1_dw3", comm=_pair_job([gw1a]))
    gw3a = shards(gw3a)
    p_w1a = pair_add("ffn1_w1", gw1a, l1)
    (l1,) = _comm_call(_pair_job([gw3a]), "rs_pair_ffn1_w3")
    p_w3a = pair_add("ffn1_w3", gw3a, l1)
    gx, (land["ffn1_w1"], land["ffn1_w3"]) = _ffn_dx(
        da1, db1, w1a, w3a, dr1, name="ffn1_dx", comm=_chips_job([p_w1a, p_w3a]))
    return loss, gx, land, G


ANY = pl.BlockSpec(memory_space=pl.ANY)


def _place():
    x, y, c = lax.axis_index("x"), lax.axis_index("y"), lax.axis_index("c")
    chips = [(1 - x, y), (x, 1 - y), (1 - x, 1 - y)]
    return x, y, c, chips


def _comm_call(job, name):
    n_in, n_out = len(job.inputs), len(job.out_shape)

    def body(*refs):
        ins, outs, sems = refs[:n_in], refs[n_in:n_in + n_out], refs[n_in + n_out:]
        job.start(ins, outs, sems)
        job.middle(ins, outs, sems)
        job.finish(ins, outs, sems)

    return pl.pallas_call(body, name=name, out_shape=job.out_shape, in_specs=[ANY] * n_in,
                          out_specs=[ANY] * n_out, scratch_shapes=job.sems)(*job.inputs)


def _ag_job(parts):
    n = len(parts)
    halves = [p.shape[0] // 2 // 16 * 16 or p.shape[0] // 2 // 8 * 8 or p.shape[0] for p in parts]
    cut = [h < p.shape[0] for h, p in zip(halves, parts)]

    def helpers(ins, outs, sems):
        send_sems, recv_sems, local_sems = sems
        x, y, c, _ = _place()
        me, sibling = (x, y, c), (x, y, 1 - c)
        xn, yn, dg = (1 - x, y), (x, 1 - y), (1 - x, 1 - y)

        def slot(a, block, part=None):
            bx, by, bc = block
            ref = outs[a].at[4 * bx + 2 * by + bc]
            if part == 0:
                return ref.at[pl.ds(0, halves[a])]
            if part == 1:
                return ref.at[pl.ds(halves[a], parts[a].shape[0] - halves[a])]
            return ref

        def copy(a, k, block, to, src=None, part=None):
            dst = slot(a, block, part)
            return pltpu.make_async_remote_copy(
                src_ref=dst if src is None else src, dst_ref=dst,
                send_sem=send_sems.at[a, k], recv_sem=recv_sems.at[a, k],
                device_id=to, device_id_type=MESH)

        local = [pltpu.make_async_copy(ins[a], slot(a, me), local_sems.at[a]) for a in range(n)]
        return copy, local, me, sibling, xn, yn, dg, c

    def start(ins, outs, sems):
        copy, local, me, sibling, xn, yn, dg, c = helpers(ins, outs, sems)
        for cp in local:
            cp.start()
        for a in range(n):
            copy(a, 0, me, sibling, src=ins[a]).start()
            copy(a, 1, me, (*xn, c), src=ins[a]).start()
            copy(a, 2, me, (*yn, c), src=ins[a]).start()

    def middle(ins, outs, sems):
        copy, local, me, sibling, xn, yn, dg, c = helpers(ins, outs, sems)
        for a in range(n):
            copy(a, 1, (*xn, c), me).wait_recv()
            copy(a, 3, (*xn, c), (*yn, c), part=0).start()
            copy(a, 5, (*xn, c), sibling).start()
        for a in range(n):
            copy(a, 2, (*yn, c), me).wait_recv()
            if cut[a]:
                copy(a, 4, (*yn, c), (*xn, c), part=1).start()
            copy(a, 6, (*yn, c), sibling).start()

    def finish(ins, outs, sems):
        copy, local, me, sibling, xn, yn, dg, c = helpers(ins, outs, sems)
        for a in range(n):
            copy(a, 3, (*dg, c), me, part=0).wait_recv()
            if cut[a]:
                copy(a, 4, (*dg, c), me, part=1).wait_recv()
            copy(a, 7, (*dg, c), sibling).start()
        for a in range(n):
            copy(a, 0, sibling, me).wait_recv()
            copy(a, 5, (*xn, 1 - c), me).wait_recv()
            copy(a, 6, (*yn, 1 - c), me).wait_recv()
            copy(a, 7, (*dg, 1 - c), me).wait_recv()
        for a in range(n):
            copy(a, 0, me, sibling, src=ins[a]).wait_send()
            copy(a, 1, me, (*xn, c), src=ins[a]).wait_send()
            copy(a, 2, me, (*yn, c), src=ins[a]).wait_send()
            copy(a, 3, (*xn, c), (*yn, c), part=0).wait_send()
            if cut[a]:
                copy(a, 4, (*yn, c), (*xn, c), part=1).wait_send()
            for k, blk in ((5, xn), (6, yn), (7, dg)):
                copy(a, k, (*blk, c), sibling).wait_send()
        for cp in local:
            cp.wait()

    return _Job(parts, [jax.ShapeDtypeStruct((N_DEV,) + p.shape, p.dtype) for p in parts],
                [pltpu.SemaphoreType.DMA((n, 8)), pltpu.SemaphoreType.DMA((n, 8)),
                 pltpu.SemaphoreType.DMA((n,))], start, finish, middle)


def _pair_job(gs):
    n = len(gs)

    def copies(g_refs, land_refs, sems):
        send_sems, recv_sems = sems
        x, y, c, _ = _place()
        return [pltpu.make_async_remote_copy(
            src_ref=g_refs[a].at[2 * k + 1 - c], dst_ref=land_refs[a].at[k],
            send_sem=send_sems.at[a, k], recv_sem=recv_sems.at[a, k],
            device_id=(x, y, 1 - c), device_id_type=MESH) for a in range(n) for k in range(4)]

    def start(ins, outs, sems):
        for cp in copies(ins, outs, sems):
            cp.start()

    def finish(ins, outs, sems):
        for cp in copies(ins, outs, sems):
            cp.wait()

    return _Job(gs, [jax.ShapeDtypeStruct((4,) + g.shape[1:], g.dtype) for g in gs],
                [pltpu.SemaphoreType.DMA((n, 4)), pltpu.SemaphoreType.DMA((n, 4))], start, finish)


def _chips_job(ps):
    n = len(ps)

    def copies(p_refs, land_refs, sems):
        send_sems, recv_sems, local_sems = sems
        x, y, c, chips = _place()
        mine = 2 * x + y
        owns = [pltpu.make_async_copy(p_refs[a].at[mine], land_refs[a].at[mine], local_sems.at[a])
                for a in range(n)]
        sends = [pltpu.make_async_remote_copy(
            src_ref=p_refs[a].at[2 * chip[0] + chip[1]], dst_ref=land_refs[a].at[mine],
            send_sem=send_sems.at[a, j], recv_sem=recv_sems.at[a, j],
            device_id=(*chip, c), device_id_type=MESH) for a in range(n) for j, chip in enumerate(chips)]
        recvs = [pltpu.make_async_remote_copy(
            src_ref=p_refs[a].at[mine], dst_ref=land_refs[a].at[2 * chip[0] + chip[1]],
            send_sem=send_sems.at[a, j], recv_sem=recv_sems.at[a, j],
            device_id=(*chip, c), device_id_type=MESH) for a in range(n) for j, chip in enumerate(chips)]
        return owns, sends, recvs

    def start(ins, outs, sems):
        owns, sends, _ = copies(ins, outs, sems)
        for cp in owns + sends:
            cp.start()

    def finish(ins, outs, sems):
        owns, sends, recvs = copies(ins, outs, sems)
        for cp in recvs:
            cp.wait_recv()
        for cp in sends:
            cp.wait_send()
        for cp in owns:
            cp.wait()

    return _Job(ps, [jax.ShapeDtypeStruct(p.shape, p.dtype) for p in ps],
                [pltpu.SemaphoreType.DMA((n, 3)), pltpu.SemaphoreType.DMA((n, 3)),
                 pltpu.SemaphoreType.DMA((n,))], start, finish)


def _all_reduce_small(buf, name):
    rows = buf.shape[0]

    def body(b_ref, o_ref, slots, send_sems, recv_sems):
        x, y, c, _ = _place()
        me = 4 * x + 2 * y + c
        slots[me] = b_ref[...]
        cps = []
        for k in range(1, N_DEV):
            fx, fy, fc = (k >> 2) & 1, (k >> 1) & 1, k & 1
            peer = (x ^ fx, y ^ fy, c ^ fc)
            cps.append(pltpu.make_async_remote_copy(
                src_ref=b_ref, dst_ref=slots.at[me],
                send_sem=send_sems.at[k - 1], recv_sem=recv_sems.at[k - 1],
                device_id=peer, device_id_type=MESH))
        for cp in cps:
            cp.start()
        for k in range(1, N_DEV):
            fx, fy, fc = (k >> 2) & 1, (k >> 1) & 1, k & 1
            src = 4 * (x ^ fx) + 2 * (y ^ fy) + (c ^ fc)
            pltpu.make_async_remote_copy(
                src_ref=b_ref, dst_ref=slots.at[src],
                send_sem=send_sems.at[k - 1], recv_sem=recv_sems.at[k - 1],
                device_id=(x ^ fx, y ^ fy, c ^ fc), device_id_type=MESH).wait_recv()
        for cp in cps:
            cp.wait_send()
        acc = slots[0]
        for k in range(1, N_DEV):
            acc = acc + slots[k]
        o_ref[...] = acc

    vm = pl.BlockSpec(memory_space=pltpu.VMEM)
    return pl.pallas_call(
        body, name=name, out_shape=jax.ShapeDtypeStruct(buf.shape, F32),
        in_specs=[vm], out_specs=vm,
        scratch_shapes=[pltpu.VMEM((N_DEV, rows, LANES), F32),
                        pltpu.SemaphoreType.DMA((N_DEV - 1,)), pltpu.SemaphoreType.DMA((N_DEV - 1,))],
    )(buf)


def _tile2(rows, cols, nbuf):
    budget = VMEM_LIMIT // 2 // (2 * nbuf * 4)
    for t in range(min(512, rows) // 16 * 16, 0, -16):
        if rows % t == 0 and t * cols <= budget:
            return t, cols
    return rows, _pick(cols, max(LANES, budget // rows // LANES * LANES))


def _pair_add(g, land, core, name):
    _, r, c = g.shape
    tr, tc = _tile2(r, c, 3)

    def body(core_ref, g_ref, l_ref, o_ref):
        o_ref[...] = (g_ref[...].astype(F32) + l_ref[...].astype(F32)).astype(o_ref.dtype)

    blk = pl.BlockSpec((1, tr, tc), lambda k, i, j, cr: (k, i, j))
    return pl.pallas_call(
        body, name=name, out_shape=jax.ShapeDtypeStruct(land.shape, land.dtype),
        grid_spec=pltpu.PrefetchScalarGridSpec(
            num_scalar_prefetch=1, grid=(4, r // tr, c // tc),
            in_specs=[pl.BlockSpec((1, tr, tc), lambda k, i, j, cr: (2 * k + cr[0], i, j)), blk],
            out_specs=blk),
        compiler_params=_params("parallel", "parallel", "parallel"),
    )(core, g, land)


def _adamw(w, parts, m, v, name):
    shp = w.shape
    cols = shp[-1]
    w2, m2, v2 = (t.reshape(-1, cols) for t in (w, m, v))
    rows = w2.shape[0]
    n = parts.shape[0]
    assert parts.shape[1:] == (rows, cols), (parts.shape, shp)
    tr, tc = _tile2(rows, cols, n + 7)
    c1 = 1.0 / (1.0 - ADAM_B1 ** ADAM_STEP)
    c2 = 1.0 / (1.0 - ADAM_B2 ** ADAM_STEP)

    def body(*refs):
        p_refs = refs[:n]
        w_ref, m_ref, v_ref, g_ref, d_ref, nm_ref, nv_ref = refs[n:]
        gv = p_refs[0][0].astype(F32)
        for p_ref in p_refs[1:]:
            gv = gv + p_ref[0].astype(F32)
        nm = ADAM_B1 * m_ref[...] + (1.0 - ADAM_B1) * gv
        nv = ADAM_B2 * v_ref[...] + (1.0 - ADAM_B2) * (gv * gv)
        g_ref[...] = gv
        nm_ref[...] = nm
        nv_ref[...] = nv
        d_ref[...] = -ADAM_LR * ((nm * c1) / (jnp.sqrt(nv * c2) + ADAM_EPS) + ADAM_WD * w_ref[...])

    blk = pl.BlockSpec((tr, tc), lambda i, j: (i, j))
    slab = lambda k: pl.BlockSpec((1, tr, tc), lambda i, j: (k, i, j))
    out = jax.ShapeDtypeStruct((rows, cols), F32)
    outs = pl.pallas_call(
        body, name=name, out_shape=[out] * 4, grid=(rows // tr, cols // tc),
        in_specs=[slab(k) for k in range(n)] + [blk] * 3, out_specs=[blk] * 4,
        compiler_params=_params("parallel", "parallel"),
    )(*([parts] * n), w2, m2, v2)
    return tuple(o.reshape(shp) for o in outs)


def _pack_rows(arrs):
    rows = []
    for a in arrs:
        flat = a.reshape(-1).astype(F32)
        pad = (-flat.shape[0]) % LANES
        if pad:
            flat = jnp.concatenate([flat, jnp.zeros((pad,), F32)])
        rows.append(flat.reshape(-1, LANES))
    out = jnp.concatenate(rows, axis=0)
    pad = (-out.shape[0]) % 8
    if pad:
        out = jnp.concatenate([out, jnp.zeros((pad, LANES), F32)], axis=0)
    return out


def _unpack_rows(buf, shapes):
    outs, r = [], 0
    for s in shapes:
        n = math.prod(s)
        nr = -(-n // LANES)
        outs.append(buf[r:r + nr].reshape(-1)[:n].reshape(s))
        r += nr
    return outs


BIG = ("ffn1_w1", "ffn1_w3", "ffn1_w2", "w_in", "w_out", "ffn2_w1", "ffn2_w3", "ffn2_w2")
ROW_SHARDED = ("ffn1_w2", "w_out", "ffn2_w2")
SMALL = ("ln1_g", "ln1_b", "hgrn_lb", "hgrn_norm_g", "mlstm_conv_w", "mlstm_conv_b", "mlstm_ig_b",
         "mlstm_fg_b", "mlstm_norm_g", "ln2_g", "ln2_b", "ln3_g", "ln3_b")
WEIGHTS = ("ffn1_w1", "ffn1_w3", "ffn1_w2", "ln1_g", "ln1_b", "w_in", "hgrn_lb", "hgrn_norm_g",
           "mlstm_conv_w", "mlstm_conv_b", "mlstm_ig_b", "mlstm_fg_b", "mlstm_norm_g", "w_out",
           "ln2_g", "ln2_b", "ffn2_w1", "ffn2_w3", "ffn2_w2", "ln3_g", "ln3_b")


def kernel(x, ffn1_w1, ffn1_w3, ffn1_w2, ln1_g, ln1_b, w_in, hgrn_lb, hgrn_norm_g, mlstm_conv_w, mlstm_conv_b, mlstm_ig_b, mlstm_fg_b, mlstm_norm_g, w_out, ln2_g, ln2_b, ffn2_w1, ffn2_w3, ffn2_w2, ln3_g, ln3_b, loss_target, m_ffn1_w1, m_ffn1_w3, m_ffn1_w2, m_ln1_g, m_ln1_b, m_w_in, m_hgrn_lb, m_hgrn_norm_g, m_mlstm_conv_w, m_mlstm_conv_b, m_mlstm_ig_b, m_mlstm_fg_b, m_mlstm_norm_g, m_w_out, m_ln2_g, m_ln2_b, m_ffn2_w1, m_ffn2_w3, m_ffn2_w2, m_ln3_g, m_ln3_b, v_ffn1_w1, v_ffn1_w3, v_ffn1_w2, v_ln1_g, v_ln1_b, v_w_in, v_hgrn_lb, v_hgrn_norm_g, v_mlstm_conv_w, v_mlstm_conv_b, v_mlstm_ig_b, v_mlstm_fg_b, v_mlstm_norm_g, v_w_out, v_ln2_g, v_ln2_b, v_ffn2_w1, v_ffn2_w3, v_ffn2_w2, v_ln3_g, v_ln3_b):
    args = (ffn1_w1, ffn1_w3, ffn1_w2, ln1_g, ln1_b, w_in, hgrn_lb, hgrn_norm_g, mlstm_conv_w,
            mlstm_conv_b, mlstm_ig_b, mlstm_fg_b, mlstm_norm_g, w_out, ln2_g, ln2_b, ffn2_w1, ffn2_w3,
            ffn2_w2, ln3_g, ln3_b)
    ms = (m_ffn1_w1, m_ffn1_w3, m_ffn1_w2, m_ln1_g, m_ln1_b, m_w_in, m_hgrn_lb, m_hgrn_norm_g,
          m_mlstm_conv_w, m_mlstm_conv_b, m_mlstm_ig_b, m_mlstm_fg_b, m_mlstm_norm_g, m_w_out, m_ln2_g,
          m_ln2_b, m_ffn2_w1, m_ffn2_w3, m_ffn2_w2, m_ln3_g, m_ln3_b)
    vs = (v_ffn1_w1, v_ffn1_w3, v_ffn1_w2, v_ln1_g, v_ln1_b, v_w_in, v_hgrn_lb, v_hgrn_norm_g,
          v_mlstm_conv_w, v_mlstm_conv_b, v_mlstm_ig_b, v_mlstm_fg_b, v_mlstm_norm_g, v_w_out, v_ln2_g,
          v_ln2_b, v_ffn2_w1, v_ffn2_w3, v_ffn2_w2, v_ln3_g, v_ln3_b)
    w = dict(zip(WEIGHTS, args))
    m = dict(zip(WEIGHTS, ms))
    v = dict(zip(WEIGHTS, vs))
    core = lax.axis_index("c")
    dev = 4 * lax.axis_index("x") + 2 * lax.axis_index("y") + core
    p_lb = []

    def small_params(gathered_small):
        lb_sh, cw_sh = zip(*[_unpack_rows(gathered_small[i], [hgrn_lb.shape, mlstm_conv_w.shape])
                             for i in range(N_DEV)])
        hgrn_lb_full = jnp.concatenate(lb_sh, axis=-1)
        conv_w_full = jnp.concatenate(cw_sh, axis=-1)[0]
        p_lb.append(jax.nn.softmax(hgrn_lb_full, axis=1))
        return dict(ln1_g=ln1_g, ln1_b=ln1_b, ln2_g=ln2_g, ln2_b=ln2_b, ln3_g=ln3_g, ln3_b=ln3_b,
                    lb=p_lb[0][:, 0], hgrn_norm_g=hgrn_norm_g, conv_w=conv_w_full, conv_b=mlstm_conv_b,
                    ig_b=mlstm_ig_b[0], fg_b=mlstm_fg_b[0], mlstm_norm_g=mlstm_norm_g)

    def rows_first(t, k):
        return t[0] if k in ROW_SHARDED else t[0].T

    loss_row, grad_x, land2, G = _device_step(
        x[0], loss_target[0], {k: rows_first(w[k], k).astype(BF16) for k in BIG},
        _pack_rows([hgrn_lb, mlstm_conv_w]), small_params, jnp.reshape(core, (1,)).astype(jnp.int32))
    p_lb = p_lb[0]

    dlb = G["lb"]
    g_lb = jnp.stack([dlb * p_lb[:, 0] * (1.0 - p_lb[:, 0]), -dlb * p_lb[:, 0] * p_lb[:, 1]], axis=1)
    small_full = {"ln1_g": G["ln1_g"], "ln1_b": G["ln1_b"], "hgrn_lb": g_lb, "hgrn_norm_g": G["hgrn_norm_g"],
                  "mlstm_conv_w": G["conv_w"][None], "mlstm_conv_b": G["conv_b"],
                  "mlstm_ig_b": G["ig_b"][None], "mlstm_fg_b": G["fg_b"][None],
                  "mlstm_norm_g": G["mlstm_norm_g"], "ln2_g": G["ln2_g"], "ln2_b": G["ln2_b"],
                  "ln3_g": G["ln3_g"], "ln3_b": G["ln3_b"]}
    small_list = [small_full[k] for k in SMALL] + [loss_row]
    reduced = _all_reduce_small(_pack_rows(small_list), "all_reduce_small")
    red = _unpack_rows(reduced, [a.shape for a in small_list])
    loss = red[-1][0, 0]
    small_g = {}
    for k, gk in zip(SMALL, red[:-1]):
        if k in ("hgrn_lb", "mlstm_conv_w"):
            n = w[k].shape[-1]
            gk = lax.dynamic_slice_in_dim(gk, dev * n, n, axis=gk.ndim - 1)
        small_g[k] = gk

    grads, delta, new_m, new_v = {}, {}, {}, {}
    for k in BIG:
        res = _adamw(rows_first(w[k], k), land2[k], rows_first(m[k], k), rows_first(v[k], k), "adamw_" + k)
        grads[k], delta[k], new_m[k], new_v[k] = [(r if k in ROW_SHARDED else r.T)[None] for r in res]
    sm = _adamw(_pack_rows([w[k] for k in SMALL]), _pack_rows([small_g[k] for k in SMALL])[None],
                _pack_rows([m[k] for k in SMALL]), _pack_rows([v[k] for k in SMALL]), "adamw_small")
    shapes = [w[k].shape for k in SMALL]
    for dst, buf in zip((grads, delta, new_m, new_v), sm):
        for k, val in zip(SMALL, _unpack_rows(buf, shapes)):
            dst[k] = val
    return (loss, grad_x[None], *[grads[k] for k in WEIGHTS], *[delta[k] for k in WEIGHTS],
            *[new_m[k] for k in WEIGHTS], *[new_v[k] for k in WEIGHTS])
```
